```python
import jax, jax.numpy as jnp
from jax import lax
import numpy as np

D_MODEL = 1024
BATCH = 2
SEQ = 8192
DEPTH = 2
DEC_BATCH = 128
DEC_SEQ = 4
PAST_LEN = 16384
PAGE_SIZE = 128

N_A_LAYERS = DEPTH // 2
N_B_LAYERS = DEPTH - N_A_LAYERS

SSM_EXPAND = 2
SSM_D_INNER = SSM_EXPAND * D_MODEL
SSM_HEAD_DIM = 64
SSM_HEADS = SSM_D_INNER // SSM_HEAD_DIM
SSM_GROUPS = 4
SSM_STATE = 128
SSM_CONV = 4
SSM_CHUNK = 128
SSM_GN = SSM_GROUPS * SSM_STATE
SSM_CONV_DIM = SSM_D_INNER + 2 * SSM_GN
SSM_IN_DIM = SSM_D_INNER + SSM_CONV_DIM + SSM_HEADS

ATT_HEAD_DIM = 64
ATT_Q_HEADS = D_MODEL // ATT_HEAD_DIM
ATT_KV_HEADS = 4
ATT_GROUP = ATT_Q_HEADS // ATT_KV_HEADS
ATT_KV_WIDTH = ATT_KV_HEADS * ATT_HEAD_DIM
WINDOW = 128
ROT_DIM = ATT_HEAD_DIM // 4
ROPE_THETA = 500000.0

MOE_GROUPS = 4
MOE_EXPERTS_PER_GROUP = 8
MOE_EXPERTS = MOE_GROUPS * MOE_EXPERTS_PER_GROUP
MOE_TOP_K = 2
MOE_HIDDEN = D_MODEL // 2
MOE_BLOCK = 128

EPS = 1e-6

kernel_name = 'yoco_mamba2_swa_sink_hmoe_step'


def rmsnorm(x, g):
    xf = x.astype(jnp.float32)
    y = xf * lax.rsqrt(jnp.mean(xf * xf, axis=-1, keepdims=True) + EPS)
    return (y * g.astype(jnp.float32)).astype(x.dtype)


def rope_partial(x, pos):
    inv = ROPE_THETA ** (-jnp.arange(0, ROT_DIM, 2, dtype=jnp.float32) / ROT_DIM)
    ang = pos.astype(jnp.float32)[:, None] * inv[None, :]
    cos = jnp.cos(ang)[None, :, None, :]
    sin = jnp.sin(ang)[None, :, None, :]
    xr = x[..., :ROT_DIM].astype(jnp.float32)
    x1, x2 = xr[..., :ROT_DIM // 2], xr[..., ROT_DIM // 2:]
    rot = jnp.concatenate([x1 * cos - x2 * sin, x2 * cos + x1 * sin], axis=-1)
    return jnp.concatenate([rot.astype(x.dtype), x[..., ROT_DIM:]], axis=-1)


def ssd_scan(x, dt, a, b_in, c_in, s0):
    f32 = jnp.float32
    bsz, L = x.shape[0], x.shape[1]
    q = min(SSM_CHUNK, L)
    lp = -(-L // q) * q
    nc = lp // q
    r = SSM_HEADS // SSM_GROUPS
    pw = ((0, 0), (0, lp - L))
    xf = jnp.pad(x.astype(f32), pw + ((0, 0), (0, 0)))
    dtf = jnp.pad(dt, pw + ((0, 0),))
    bc = jnp.pad(b_in.astype(f32), pw + ((0, 0), (0, 0))).reshape(bsz, nc, q, SSM_GROUPS, SSM_STATE)
    cc = jnp.pad(c_in.astype(f32), pw + ((0, 0), (0, 0))).reshape(bsz, nc, q, SSM_GROUPS, SSM_STATE)
    xdt = (xf * dtf[..., None]).reshape(bsz, nc, q, SSM_GROUPS, r, SSM_HEAD_DIM)
    acum = jnp.cumsum((dtf * a).reshape(bsz, nc, q, SSM_GROUPS, r), axis=2)
    act = jnp.moveaxis(acum, 2, -1)
    causal = jnp.tril(jnp.ones((q, q), dtype=bool))
    seg = act[..., :, None] - act[..., None, :]
    lmat = jnp.exp(jnp.where(causal, seg, -jnp.inf))
    cb = jnp.einsum('bcign,bcjgn->bcgij', cc, bc)
    y_diag = jnp.einsum('bcgrij,bcjgrp->bcigrp', cb[:, :, :, None] * lmat, xdt)
    decay = jnp.exp(act[..., -1:] - act)
    states = jnp.einsum('bcjgn,bcgrj,bcjgrp->bcgrpn', bc, decay, xdt)
    chunk_decay = jnp.exp(act[..., -1])
    s_init = s0.astype(f32).reshape(bsz, SSM_GROUPS, r, SSM_HEAD_DIM, SSM_STATE)

    def step(s, inp):
        dec, st = inp
        return s * dec[..., None, None] + st, s

    s_final, prev = lax.scan(step, s_init, (jnp.moveaxis(chunk_decay, 1, 0), jnp.moveaxis(states, 1, 0)))
    prev = jnp.moveaxis(prev, 0, 1)
    y_off = jnp.einsum('bcign,bcgrpn,bcgri->bcigrp', cc, prev, jnp.exp(act))
    y = (y_diag + y_off).reshape(bsz, lp, SSM_HEADS, SSM_HEAD_DIM)[:, :L]
    return y, s_final.reshape(bsz, SSM_HEADS, SSM_HEAD_DIM, SSM_STATE)


def mamba2_mixer(u, conv_buf, s0, w_in, conv_w, conv_b, dt_bias, a_log, d_skip, gate_norm, w_out):
    bsz, L, _ = u.shape
    zxbcdt = u @ w_in
    z = zxbcdt[..., :SSM_D_INNER]
    xbc = zxbcdt[..., SSM_D_INNER:SSM_D_INNER + SSM_CONV_DIM]
    dt_raw = zxbcdt[..., SSM_D_INNER + SSM_CONV_DIM:]
    xp = jnp.concatenate([conv_buf.astype(xbc.dtype), xbc], axis=1)
    acc = conv_b
    for k in range(SSM_CONV):
        acc = acc + xp[:, k:k + L] * conv_w[k]
    xbc_c = jax.nn.silu(acc)
    new_buf = xp[:, L:]
    xs = xbc_c[..., :SSM_D_INNER].reshape(bsz, L, SSM_HEADS, SSM_HEAD_DIM)
    b_in = xbc_c[..., SSM_D_INNER:SSM_D_INNER + SSM_GN].reshape(bsz, L, SSM_GROUPS, SSM_STATE)
    c_in = xbc_c[..., SSM_D_INNER + SSM_GN:].reshape(bsz, L, SSM_GROUPS, SSM_STATE)
    dt = jax.nn.softplus(dt_raw.astype(jnp.float32) + dt_bias.astype(jnp.float32))
    a = -jnp.exp(a_log.astype(jnp.float32))
    y, s_new = ssd_scan(xs, dt, a, b_in, c_in, s0)
    y = y + xs.astype(jnp.float32) * d_skip.astype(jnp.float32)[:, None]
    y = y.reshape(bsz, L, SSM_D_INNER) * jax.nn.silu(z.astype(jnp.float32))
    yg = rmsnorm(y.reshape(bsz, L, SSM_GROUPS, SSM_D_INNER // SSM_GROUPS),
                 gate_norm.reshape(SSM_GROUPS, SSM_D_INNER // SSM_GROUPS))
    out = yg.reshape(bsz, L, SSM_D_INNER).astype(u.dtype) @ w_out
    return out, new_buf, s_new.astype(s0.dtype)


def shared_kv(h, pos, kv_norm, w_kv, k_norm):
    bsz, L, _ = h.shape
    kv = rmsnorm(h, kv_norm) @ w_kv
    k = kv[..., :ATT_KV_WIDTH].reshape(bsz, L, ATT_KV_HEADS, ATT_HEAD_DIM)
    v = kv[..., ATT_KV_WIDTH:].reshape(bsz, L, ATT_KV_HEADS, ATT_HEAD_DIM)
    k = rope_partial(rmsnorm(k, k_norm), pos)
    return k, v


def attn_queries(h, pos, attn_norm, w_q, q_norm):
    bsz, L, _ = h.shape
    q = (rmsnorm(h, attn_norm) @ w_q).reshape(bsz, L, ATT_Q_HEADS, ATT_HEAD_DIM)
    q = rope_partial(rmsnorm(q, q_norm), pos)
    return q.reshape(bsz, L, ATT_KV_HEADS, ATT_GROUP, ATT_HEAD_DIM)


def sink_attention(q, k, v, mask, sinks):
    s = jnp.einsum('bnqhgd,bnkhd->bnhgqk', q, k, preferred_element_type=jnp.float32)
    s = jnp.where(mask[None, :, None, None], s * (ATT_HEAD_DIM ** -0.5), -jnp.inf)
    sink = sinks.astype(jnp.float32)[None, None, :, :, None, None]
    m = jnp.maximum(jnp.max(s, axis=-1, keepdims=True), sink)
    p = jnp.exp(s - m)
    denom = jnp.sum(p, axis=-1, keepdims=True) + jnp.exp(sink - m)
    return jnp.einsum('bnhgqk,bnkhd->bnqhgd', (p / denom).astype(v.dtype), v)


def window_attn_prompt(q, k, v, sinks):
    bsz, L = q.shape[0], q.shape[1]
    nb = L // WINDOW
    qb = q.reshape(bsz, nb, WINDOW, ATT_KV_HEADS, ATT_GROUP, ATT_HEAD_DIM)
    pw = ((0, 0), (WINDOW, 0), (0, 0), (0, 0))
    kp = jnp.pad(k, pw).reshape(bsz, nb + 1, WINDOW, ATT_KV_HEADS, ATT_HEAD_DIM)
    vp = jnp.pad(v, pw).reshape(bsz, nb + 1, WINDOW, ATT_KV_HEADS, ATT_HEAD_DIM)
    kb = jnp.concatenate([kp[:, :-1], kp[:, 1:]], axis=2)
    vb = jnp.concatenate([vp[:, :-1], vp[:, 1:]], axis=2)
    blk = jnp.arange(nb)[:, None]
    qpos = blk * WINDOW + jnp.arange(WINDOW)[None, :]
    kpos = (blk - 1) * WINDOW + jnp.arange(2 * WINDOW)[None, :]
    d = qpos[:, :, None] - kpos[:, None, :]
    mask = (d >= 0) & (d < WINDOW) & (kpos[:, None, :] >= 0)
    o = sink_attention(qb, kb, vb, mask, sinks)
    return o.reshape(bsz, L, ATT_Q_HEADS * ATT_HEAD_DIM)


def window_attn_sample(q, kf, vf, qpos, kpos, sinks):
    bsz, S = q.shape[0], q.shape[1]
    d = qpos[:, None] - kpos[None, :]
    mask = (d >= 0) & (d < WINDOW) & (kpos[None, :] >= 0)
    o = sink_attention(q[:, None], kf[:, None], vf[:, None], mask[None], sinks)
    return o.reshape(bsz, S, ATT_Q_HEADS * ATT_HEAD_DIM)


def moe_dispatch(xt, eidx, gates, w_gate, w_up, w_down):
    n, d = xt.shape
    na = n * MOE_TOP_K
    e_flat = eidx.reshape(-1)
    tok = jnp.arange(na, dtype=jnp.int32) // MOE_TOP_K
    order = jnp.argsort(e_flat)
    e_s = e_flat[order]
    tok_s = tok[order]
    g_s = gates.reshape(-1)[order]
    counts = jax.ops.segment_sum(jnp.ones((na,), jnp.int32), e_flat, num_segments=MOE_EXPERTS)
    starts = jnp.cumsum(counts) - counts
    pcounts = (counts + MOE_BLOCK - 1) // MOE_BLOCK * MOE_BLOCK
    pends = jnp.cumsum(pcounts)
    pstarts = pends - pcounts
    dest = pstarts[e_s] + (jnp.arange(na, dtype=jnp.int32) - starts[e_s])
    n_blocks = -(-(na + MOE_EXPERTS * (MOE_BLOCK - 1)) // MOE_BLOCK)
    rows = n_blocks * MOE_BLOCK
    xbuf = jnp.zeros((rows, d), xt.dtype).at[dest].set(xt[tok_s])
    blk_start = jnp.arange(n_blocks, dtype=jnp.int32) * MOE_BLOCK
    blk_e = jnp.minimum(jnp.searchsorted(pends, blk_start, side='right'), MOE_EXPERTS - 1)

    def run_block(args):
        xb, e = args
        hid = jax.nn.silu(xb @ w_gate[e]) * (xb @ w_up[e])
        return hid @ w_down[e]

    ybuf = lax.map(run_block, (xbuf.reshape(n_blocks, MOE_BLOCK, d), blk_e)).reshape(rows, d)
    contrib = ybuf[dest] * g_s[:, None].astype(ybuf.dtype)
    return jnp.zeros((n, d), xt.dtype).at[tok_s].add(contrib)


def hier_moe(x, ffn_norm, w_grp, b_grp, w_rt, b_rt, w_gate, w_up, w_down):
    bsz, L, d = x.shape
    xn = rmsnorm(x, ffn_norm).reshape(bsz * L, d)
    gprob = jax.nn.softmax((xn @ w_grp).astype(jnp.float32) + b_grp.astype(jnp.float32), axis=-1)
    gp, gi = lax.top_k(gprob, 1)
    elog = ((xn @ w_rt).astype(jnp.float32) + b_rt.astype(jnp.float32)).reshape(bsz * L, MOE_GROUPS, MOE_EXPERTS_PER_GROUP)
    elog_g = jnp.take_along_axis(elog, gi[:, :, None], axis=1)[:, 0]
    ep, ei = lax.top_k(jax.nn.softmax(elog_g, axis=-1), MOE_TOP_K)
    gates = gp * ep / jnp.sum(ep, axis=-1, keepdims=True)
    eidx = gi * MOE_EXPERTS_PER_GROUP + ei
    y = moe_dispatch(xn, eidx, gates, w_gate, w_up, w_down)
    return y.reshape(bsz, L, d)


def run_trunk(x, pos, ssm0, conv0, k_cache, v_cache, p):
    h = x
    new_ssm, new_conv = [], []
    kf = vf = kpos = k_win = v_win = None
    for layer in range(DEPTH):
        if layer < N_A_LAYERS:
            i = layer
            out, cbuf, st = mamba2_mixer(rmsnorm(h, p['ssm_norm'][i]), conv0[i], ssm0[i], p['ssm_w_in'][i],
                                         p['ssm_conv_w'][i], p['ssm_conv_b'][i], p['ssm_dt_bias'][i],
                                         p['ssm_a_log'][i], p['ssm_d'][i], p['ssm_gate_norm'][i], p['ssm_w_out'][i])
            h = h + out
            new_conv.append(cbuf)
            new_ssm.append(st)
        else:
            j = layer - N_A_LAYERS
            if j == 0:
                k_sh, v_sh = shared_kv(h, pos, p['kv_norm'], p['w_kv'], p['k_norm'])
                if k_cache is None:
                    wl = min(WINDOW, h.shape[1])
                    k_win, v_win = k_sh[:, -wl:], v_sh[:, -wl:]
                else:
                    lc = k_cache.shape[1]
                    kf = jnp.concatenate([k_cache.astype(k_sh.dtype), k_sh], axis=1)
                    vf = jnp.concatenate([v_cache.astype(v_sh.dtype), v_sh], axis=1)
                    kpos = PAST_LEN - lc + jnp.arange(kf.shape[1], dtype=jnp.int32)
                    k_win, v_win = kf[:, -lc:], vf[:, -lc:]
            q = attn_queries(h, pos, p['attn_norm'][j], p['w_q'][j], p['q_norm'][j])
            sinks = p['sinks'][j].reshape(ATT_KV_HEADS, ATT_GROUP)
            if k_cache is None:
                o = window_attn_prompt(q, k_sh, v_sh, sinks)
            else:
                o = window_attn_sample(q, kf, vf, pos, kpos, sinks)
            h = h + o @ p['w_o'][j]
        h = h + hier_moe(h, p['ffn_norm'][layer], p['moe_w_group'][layer], p['moe_b_group'][layer],
                         p['moe_w_router'][layer], p['moe_b_router'][layer], p['moe_w_gate'][layer],
                         p['moe_w_up'][layer], p['moe_w_down'][layer])
    return h, jnp.stack(new_ssm), jnp.stack(new_conv), k_win, v_win


def setup_inputs(seed: int = 0) -> dict:
    key = jax.random.key(seed)
    ks = jax.random.split(key, 40)
    f32 = jnp.float32
    nrm = lambda k, s, sc: jax.random.normal(k, s, f32) * sc
    gain = lambda k, s: 1.0 + 0.01 * jax.random.normal(k, s, f32)
    lc = min(WINDOW, PAST_LEN)
    dt = jnp.exp(jax.random.uniform(ks[0], (N_A_LAYERS, SSM_HEADS), f32) * (jnp.log(0.1) - jnp.log(0.001)) + jnp.log(0.001))
    dt_bias = dt + jnp.log(-jnp.expm1(-dt))
    a_log = jnp.log(jax.random.uniform(ks[1], (N_A_LAYERS, SSM_HEADS), f32, 1.0, 16.0))
    return {
        'x_prompt': nrm(ks[2], (BATCH, SEQ, D_MODEL), 1.0),
        'x_sample': nrm(ks[3], (DEC_BATCH, DEC_SEQ, D_MODEL), 1.0),
        'state_ssm': nrm(ks[4], (N_A_LAYERS, DEC_BATCH, SSM_HEADS, SSM_HEAD_DIM, SSM_STATE), 0.1),
        'state_conv': nrm(ks[5], (N_A_LAYERS, DEC_BATCH, SSM_CONV - 1, SSM_CONV_DIM), 1.0),
        'cache_k_win': nrm(ks[6], (DEC_BATCH, lc, ATT_KV_HEADS, ATT_HEAD_DIM), 1.0),
        'cache_v_win': nrm(ks[7], (DEC_BATCH, lc, ATT_KV_HEADS, ATT_HEAD_DIM), 1.0),
        'ssm_norm': gain(ks[8], (N_A_LAYERS, D_MODEL)),
        'ssm_w_in': nrm(ks[9], (N_A_LAYERS, D_MODEL, SSM_IN_DIM), D_MODEL ** -0.5),
        'ssm_conv_w': nrm(ks[10], (N_A_LAYERS, SSM_CONV, SSM_CONV_DIM), SSM_CONV ** -0.5),
        'ssm_conv_b': nrm(ks[11], (N_A_LAYERS, SSM_CONV_DIM), 0.01),
        'ssm_dt_bias': dt_bias,
        'ssm_a_log': a_log,
        'ssm_d': gain(ks[12], (N_A_LAYERS, SSM_HEADS)),
        'ssm_gate_norm': gain(ks[13], (N_A_LAYERS, SSM_D_INNER)),
        'ssm_w_out': nrm(ks[14], (N_A_LAYERS, SSM_D_INNER, D_MODEL), SSM_D_INNER ** -0.5),
        'kv_norm': gain(ks[15], (D_MODEL,)),
        'w_kv': nrm(ks[16], (D_MODEL, 2 * ATT_KV_WIDTH), D_MODEL ** -0.5),
        'k_norm': gain(ks[17], (ATT_HEAD_DIM,)),
        'attn_norm': gain(ks[18], (N_B_LAYERS, D_MODEL)),
        'w_q': nrm(ks[19], (N_B_LAYERS, D_MODEL, ATT_Q_HEADS * ATT_HEAD_DIM), D_MODEL ** -0.5),
        'q_norm': gain(ks[20], (N_B_LAYERS, ATT_HEAD_DIM)),
        'sinks': nrm(ks[21], (N_B_LAYERS, ATT_Q_HEADS), 0.5),
        'w_o': nrm(ks[22], (N_B_LAYERS, ATT_Q_HEADS * ATT_HEAD_DIM, D_MODEL), (ATT_Q_HEADS * ATT_HEAD_DIM) ** -0.5),
        'ffn_norm': gain(ks[23], (DEPTH, D_MODEL)),
        'moe_w_group': nrm(ks[24], (DEPTH, D_MODEL, MOE_GROUPS), D_MODEL ** -0.5),
        'moe_b_group': nrm(ks[25], (DEPTH, MOE_GROUPS), 0.01),
        'moe_w_router': nrm(ks[26], (DEPTH, D_MODEL, MOE_EXPERTS), D_MODEL ** -0.5),
        'moe_b_router': nrm(ks[27], (DEPTH, MOE_EXPERTS), 0.01),
        'moe_w_gate': nrm(ks[28], (DEPTH, MOE_EXPERTS, D_MODEL, MOE_HIDDEN), D_MODEL ** -0.5),
        'moe_w_up': nrm(ks[29], (DEPTH, MOE_EXPERTS, D_MODEL, MOE_HIDDEN), D_MODEL ** -0.5),
        'moe_w_down': nrm(ks[30], (DEPTH, MOE_EXPERTS, MOE_HIDDEN, D_MODEL), MOE_HIDDEN ** -0.5),
    }


def reference(x_prompt, x_sample, state_ssm, state_conv, cache_k_win, cache_v_win,
              ssm_norm, ssm_w_in, ssm_conv_w, ssm_conv_b, ssm_dt_bias, ssm_a_log, ssm_d, ssm_gate_norm, ssm_w_out,
              kv_norm, w_kv, k_norm, attn_norm, w_q, q_norm, sinks, w_o,
              ffn_norm, moe_w_group, moe_b_group, moe_w_router, moe_b_router, moe_w_gate, moe_w_up, moe_w_down):
    p = dict(ssm_norm=ssm_norm, ssm_w_in=ssm_w_in, ssm_conv_w=ssm_conv_w, ssm_conv_b=ssm_conv_b,
             ssm_dt_bias=ssm_dt_bias, ssm_a_log=ssm_a_log, ssm_d=ssm_d, ssm_gate_norm=ssm_gate_norm,
             ssm_w_out=ssm_w_out, kv_norm=kv_norm, w_kv=w_kv, k_norm=k_norm, attn_norm=attn_norm,
             w_q=w_q, q_norm=q_norm, sinks=sinks, w_o=w_o, ffn_norm=ffn_norm, moe_w_group=moe_w_group,
             moe_b_group=moe_b_group, moe_w_router=moe_w_router, moe_b_router=moe_b_router,
             moe_w_gate=moe_w_gate, moe_w_up=moe_w_up, moe_w_down=moe_w_down)
    bp, lp_ = x_prompt.shape[0], x_prompt.shape[1]
    pos_p = jnp.arange(lp_, dtype=jnp.int32)
    ssm0 = jnp.zeros((N_A_LAYERS, bp, SSM_HEADS, SSM_HEAD_DIM, SSM_STATE), jnp.float32)
    conv0 = jnp.zeros((N_A_LAYERS, bp, SSM_CONV - 1, SSM_CONV_DIM), x_prompt.dtype)
    y_p, ssm_p, conv_p, k_p, v_p = run_trunk(x_prompt, pos_p, ssm0, conv0, None, None, p)
    pos_s = PAST_LEN + jnp.arange(x_sample.shape[1], dtype=jnp.int32)
    y_s, ssm_s, conv_s, k_s, v_s = run_trunk(x_sample, pos_s, state_ssm, state_conv, cache_k_win, cache_v_win, p)
    return (y_p, y_s, ssm_p, conv_p, k_p, v_p, ssm_s, conv_s, k_s, v_s)
```

```python
import functools

import jax
import jax.numpy as jnp
from jax import lax
from jax.experimental import pallas as pl
from jax.experimental.pallas import tpu as pltpu

F32 = jnp.float32
BF16 = jnp.bfloat16
I32 = jnp.int32

EPS = 1e-6
SSM_HEAD_DIM = 64
SSM_GROUPS = 4
SSM_STATE = 128
SSM_CONV = 4
SSM_CHUNK = 128
ATT_HEAD_DIM = 64
ATT_KV_HEADS = 4
WINDOW = 128
ROT_DIM = ATT_HEAD_DIM // 4
ROPE_THETA = 500000.0
MOE_GROUPS = 4
MOE_EXPERTS_PER_GROUP = 8
MOE_EXPERTS = MOE_GROUPS * MOE_EXPERTS_PER_GROUP
MOE_BLOCK = 128
PAST_LEN = 16384

LANES = 128
SUBLANES = 8
SEQ_PER_STEP = SUBLANES
TOKEN_TILE = 256
VMEM_LIMIT = 56 * 1024 * 1024


def _cparams(sem):
    return pltpu.CompilerParams(dimension_semantics=sem, vmem_limit_bytes=VMEM_LIMIT)


def _const_spec(shape):
    nd = len(shape)
    return pl.BlockSpec(shape, lambda *_: (0,) * nd)


def _split_bf16(v, n):
    parts = []
    r = v
    for k in range(n):
        p = r.astype(BF16)
        parts.append(p)
        if k + 1 < n:
            r = r - p.astype(F32)
    return parts


def _mm01(v, m01, n=3):
    acc = None
    for p in _split_bf16(v, n):
        d = jnp.dot(p, m01, preferred_element_type=F32)
        acc = d if acc is None else acc + d
    return acc


def _mm01_left(m01, v, n=3):
    acc = None
    for p in _split_bf16(v, n):
        d = jnp.dot(m01, p, preferred_element_type=F32)
        acc = d if acc is None else acc + d
    return acc


def _dot_nt(a, b):
    return lax.dot_general(a, b, (((1,), (1,)), ((), ())), preferred_element_type=F32)


def _sigmoid(x):
    return 1.0 / (1.0 + jnp.exp(-x))


def _silu(x):
    return x * _sigmoid(x)


def _softplus(x):
    return jnp.maximum(x, 0.0) + jnp.log1p(jnp.exp(-jnp.abs(x)))


def _rms_scale(x):
    return x * lax.rsqrt(jnp.mean(x * x, axis=-1, keepdims=True) + EPS)


def _gate_norm(y, z, gn, n_groups):
    yz = y * _silu(z)
    w = yz.shape[-1] // n_groups
    outs = []
    for g in range(n_groups):
        v = yz[:, g * w:(g + 1) * w]
        outs.append(_rms_scale(v) * gn[:, g * w:(g + 1) * w])
    return jnp.concatenate(outs, axis=1)


def _rope(x, c, s1, s2):
    w = x.shape[-1]
    return x * c + pltpu.roll(x, w - ROT_DIM // 2, 1) * s1 + pltpu.roll(x, ROT_DIM // 2, 1) * s2


def _tile_lanes(t, reps):
    return t if reps == 1 else jnp.concatenate([t] * reps, axis=1)


def _inproj_kernel(xp_ref, xs_ref, g_ref, wz_ref, wx_ref, wd_ref, z_ref, xbc_ref, dt_ref, *, n_p):
    i = pl.program_id(0)
    x = jnp.where(i < n_p, xp_ref[...], xs_ref[...])
    xn = (_rms_scale(x) * g_ref[...]).astype(BF16)
    z_ref[...] = jnp.dot(xn, wz_ref[...], preferred_element_type=F32)
    xbc_ref[...] = jnp.dot(xn, wx_ref[...], preferred_element_type=F32)
    dt_ref[...] = jnp.dot(xn, wd_ref[...], preferred_element_type=F32)


def _inproj(xp2, xs2, g, wz, wx, wd):
    tp, d = xp2.shape
    ts = xs2.shape[0]
    tm = TOKEN_TILE
    n_p, n_s = tp // tm, ts // tm
    t = tp + ts
    return pl.pallas_call(
        functools.partial(_inproj_kernel, n_p=n_p),
        grid=(n_p + n_s,),
        in_specs=[
            pl.BlockSpec((tm, d), lambda i: (jnp.minimum(i, n_p - 1), 0)),
            pl.BlockSpec((tm, d), lambda i: (jnp.maximum(i - n_p, 0), 0)),
            _const_spec(g.shape), _const_spec(wz.shape), _const_spec(wx.shape), _const_spec(wd.shape),
        ],
        out_specs=[
            pl.BlockSpec((tm, wz.shape[1]), lambda i: (i, 0)),
            pl.BlockSpec((tm, wx.shape[1]), lambda i: (i, 0)),
            pl.BlockSpec((tm, wd.shape[1]), lambda i: (i, 0)),
        ],
        out_shape=[jax.ShapeDtypeStruct((t, wz.shape[1]), F32),
                   jax.ShapeDtypeStruct((t, wx.shape[1]), F32),
                   jax.ShapeDtypeStruct((t, wd.shape[1]), F32)],
        compiler_params=_cparams(("arbitrary",)),
        name="inproj",
    )(xp2, xs2, g, wz, wx, wd)


def _proj_res_kernel(*refs, n_p, two_a, two_r):
    refs = list(refs)
    o_ref = refs.pop()
    i = pl.program_id(0)
    a_p = refs.pop(0)
    a = jnp.where(i < n_p, a_p[...], refs.pop(0)[...]) if two_a else a_p[...]
    w_ref = refs.pop(0)
    r_p = refs.pop(0)
    r = jnp.where(i < n_p, r_p[...], refs.pop(0)[...]) if two_r else r_p[...]
    o_ref[...] = r + jnp.dot(a.astype(BF16), w_ref[...], preferred_element_type=F32)


def _proj_res(a_list, w, r_list, tp, ts):
    tm = TOKEN_TILE
    n_p, n_s = tp // tm, ts // tm
    k, n = w.shape

    def specs(lst, width):
        if len(lst) == 2:
            return [pl.BlockSpec((tm, width), lambda i: (jnp.minimum(i, n_p - 1), 0)),
                    pl.BlockSpec((tm, width), lambda i: (jnp.maximum(i - n_p, 0), 0))]
        return [pl.BlockSpec((tm, width), lambda i: (i, 0))]

    return pl.pallas_call(
        functools.partial(_proj_res_kernel, n_p=n_p, two_a=len(a_list) == 2, two_r=len(r_list) == 2),
        grid=(n_p + n_s,),
        in_specs=specs(a_list, k) + [_const_spec(w.shape)] + specs(r_list, n),
        out_specs=pl.BlockSpec((tm, n), lambda i: (i, 0)),
        out_shape=jax.ShapeDtypeStruct((tp + ts, n), F32),
        compiler_params=_cparams(("arbitrary",)),
        name="proj_res",
    )(*a_list, w, *r_list)


def _ssd_prompt_kernel(z_ref, xbc_ref, dt_ref, cw_ref, cb_ref, dtb_ref, alog_ref, dsk_ref, gn_ref,
                       tril_ref, e_ref, yg_ref, sfin_ref, cfin_ref, xpad_sc, st_sc):
    c = pl.program_id(1)
    q = SSM_CHUNK
    cd = xbc_ref.shape[1]
    di = z_ref.shape[1]
    gn_w = SSM_GROUPS * SSM_STATE
    hpg = di // SSM_GROUPS
    pad = SUBLANES

    @pl.when(c == 0)
    def _():
        xpad_sc[0:pad, :] = jnp.zeros((pad, cd), F32)
        st_sc[...] = jnp.zeros(st_sc.shape, F32)

    @pl.when(c > 0)
    def _():
        xpad_sc[0:pad, :] = xpad_sc[q:q + pad, :]

    xpad_sc[pad:pad + q, :] = xbc_ref[...]
    acc = cb_ref[...]
    for k in range(SSM_CONV):
        off = pad - (SSM_CONV - 1) + k
        acc = acc + xpad_sc[off:off + q, :] * cw_ref[k:k + 1, :]
    xc = _silu(acc)
    xs = xc[:, :di]
    bm = xc[:, di:di + gn_w]
    cm = xc[:, di + gn_w:]

    dt = _softplus(dt_ref[...] + dtb_ref[...])
    a = -jnp.exp(alog_ref[...])
    act = _mm01_left(tril_ref[...], dt * a)
    act_t = act.T
    act_last = act[q - 1:q, :]
    e01 = e_ref[...]
    xdt = xs * _mm01(dt, e01)
    xd = xdt * _mm01(jnp.exp(act_last - act), e01)
    eax = _mm01(jnp.exp(act), e01)
    cdx = _mm01(jnp.exp(act[q - SUBLANES:q, :]), e01)[SUBLANES - 1:SUBLANES, :]

    row = lax.broadcasted_iota(I32, (q, q), 0)
    col = lax.broadcasted_iota(I32, (q, q), 1)
    causal = row >= col
    lane = lax.broadcasted_iota(I32, (q, LANES), 1)
    lo_half = lane < SSM_HEAD_DIM

    y_parts = []
    heads_per_group = hpg // SSM_HEAD_DIM
    for g in range(SSM_GROUPS):
        cg = cm[:, g * SSM_STATE:(g + 1) * SSM_STATE].astype(BF16)
        bg = bm[:, g * SSM_STATE:(g + 1) * SSM_STATE]
        cb = _dot_nt(cg, bg.astype(BF16))
        st_g = st_sc[:, g * hpg:(g + 1) * hpg]
        y_off = jnp.dot(cg, st_g.astype(BF16), preferred_element_type=F32)
        for pr in range(heads_per_group // 2):
            h0 = g * heads_per_group + 2 * pr
            ms = []
            for h in (h0, h0 + 1):
                seg = act[:, h:h + 1] - act_t[h:h + 1, :]
                lm = jnp.exp(jnp.where(causal, seg, -jnp.inf))
                ms.append((cb * lm).astype(BF16))
            m2 = jnp.concatenate(ms, axis=1)
            xpair = xdt[:, h0 * SSM_HEAD_DIM:(h0 + 2) * SSM_HEAD_DIM]
            rhs = jnp.concatenate([jnp.where(lo_half, xpair, 0.0),
                                   jnp.where(lo_half, 0.0, xpair)], axis=0).astype(BF16)
            y_d = jnp.dot(m2, rhs, preferred_element_type=F32)
            lo = 2 * pr * SSM_HEAD_DIM
            y_parts.append(y_d + y_off[:, lo:lo + LANES] * eax[:, g * hpg + lo:g * hpg + lo + LANES])
        upd = jnp.dot(bg.T.astype(BF16), xd[:, g * hpg:(g + 1) * hpg].astype(BF16),
                      preferred_element_type=F32)
        st_sc[:, g * hpg:(g + 1) * hpg] = st_g * cdx[:, g * hpg:(g + 1) * hpg] + upd

    y = jnp.concatenate(y_parts, axis=1) + xs * dsk_ref[...]
    yg_ref[...] = _gate_norm(y, z_ref[...], gn_ref[...], SSM_GROUPS)

    @pl.when(c == pl.num_programs(1) - 1)
    def _():
        sfin_ref[0] = st_sc[...].T
        cfin_ref[0] = xpad_sc[q:q + pad, :]


def _ssd_prompt(z, xbc, dt, cw, cb, dtb, alog, dsk, gnw, tril, e01, bp, seq):
    nc = seq // SSM_CHUNK
    q = SSM_CHUNK
    di, cd = z.shape[1], xbc.shape[1]
    rows = lambda b, c: (b * nc + c, 0)
    return pl.pallas_call(
        _ssd_prompt_kernel,
        grid=(bp, nc),
        in_specs=[
            pl.BlockSpec((q, di), rows), pl.BlockSpec((q, cd), rows), pl.BlockSpec((q, LANES), rows),
            _const_spec(cw.shape), _const_spec(cb.shape), _const_spec(dtb.shape), _const_spec(alog.shape),
            _const_spec(dsk.shape), _const_spec(gnw.shape), _const_spec(tril.shape), _const_spec(e01.shape),
        ],
        out_specs=[
            pl.BlockSpec((q, di), rows),
            pl.BlockSpec((1, di, SSM_STATE), lambda b, c: (b, 0, 0)),
            pl.BlockSpec((1, SUBLANES, cd), lambda b, c: (b, 0, 0)),
        ],
        out_shape=[jax.ShapeDtypeStruct((bp * seq, di), F32),
                   jax.ShapeDtypeStruct((bp, di, SSM_STATE), F32),
                   jax.ShapeDtypeStruct((bp, SUBLANES, cd), F32)],
        scratch_shapes=[pltpu.VMEM((q + 2 * SUBLANES, cd), F32), pltpu.VMEM((SSM_STATE, di), F32)],
        compiler_params=_cparams(("arbitrary", "arbitrary")),
        name="ssd_prompt",
    )(z, xbc, dt, cw, cb, dtb, alog, dsk, gnw, tril, e01)


def _ssd_sample_kernel(z_ref, xp_ref, dt_ref, s0_ref, cw_ref, cb_ref, dtb_ref, alog_ref, dsk_ref, gn_ref,
                       e_ref, g1_ref, yg_ref, sn_ref, yoff_sc, *, n_tok):
    hf = pl.program_id(1)
    nb = SEQ_PER_STEP
    half = nb // 2
    q = n_tok * nb
    di = z_ref.shape[1]
    gn_w = SSM_GROUPS * SSM_STATE
    hpg = di // SSM_GROUPS

    taps = [xp_ref[0, m] for m in range(n_tok + SSM_CONV - 1)]
    slabs = []
    for t in range(n_tok):
        acc = cb_ref[...]
        for k in range(SSM_CONV):
            acc = acc + taps[t + k] * cw_ref[k:k + 1, :]
        slabs.append(_silu(acc))
    xc = jnp.concatenate(slabs, axis=0)
    xs = xc[:, :di]
    bm = xc[:, di:di + gn_w]
    cm = xc[:, di + gn_w:]

    dt = _softplus(dt_ref[...] + dtb_ref[...])
    da = dt * (-jnp.exp(alog_ref[...]))
    acts = []
    run = None
    for t in range(n_tok):
        d = da[t * nb:(t + 1) * nb, :]
        run = d if run is None else run + d
        acts.append(run)
    act = jnp.concatenate(acts, axis=0)
    act_last = jnp.concatenate([acts[-1]] * n_tok, axis=0)
    e01 = e_ref[...]
    xdt = xs * _mm01(dt, e01)
    xd = xdt * _mm01(jnp.exp(act_last - act), e01)
    eax = _mm01(jnp.exp(act), e01)
    cdx = _mm01(jnp.exp(acts[-1]), e01)

    pairs = [(t, u) for t in range(n_tok) for u in range(t + 1)]
    cbp = jnp.concatenate([cm[t * nb:(t + 1) * nb, :] * bm[u * nb:(u + 1) * nb, :] for t, u in pairs], axis=0)
    seg = jnp.concatenate([acts[t] - acts[u] for t, u in pairs], axis=0)
    coef = _mm01(_mm01(cbp, g1_ref[...]) * jnp.exp(seg), e01)
    y_slabs = []
    for t in range(n_tok):
        acc = None
        for pi, (tt, u) in enumerate(pairs):
            if tt != t:
                continue
            term = coef[pi * nb:(pi + 1) * nb, :] * xdt[u * nb:(u + 1) * nb, :]
            acc = term if acc is None else acc + term
        y_slabs.append(acc)
    y_diag = jnp.concatenate(y_slabs, axis=0)

    zpad = jnp.concatenate([xd,
                            jnp.where(hf == 0, cdx[0:half, :], cdx[half:nb, :]),
                            jnp.zeros((LANES - q - half, di), F32)], axis=0)
    zt = zpad.T
    row_seq = lax.broadcasted_iota(I32, (q, 1), 0) % nb
    cm_b = cm.astype(BF16)
    y_off_g = [None] * SSM_GROUPS
    for sl in range(half):
        in_seq = row_seq == hf * half + sl
        for g in range(SSM_GROUPS):
            s_old = s0_ref[sl, g * hpg:(g + 1) * hpg, :]
            c_g = jnp.where(in_seq, cm_b[:, g * SSM_STATE:(g + 1) * SSM_STATE], jnp.zeros((), BF16))
            yo = _dot_nt(c_g, s_old.astype(BF16))
            y_off_g[g] = yo if y_off_g[g] is None else y_off_g[g] + yo
            b_g = jnp.where(in_seq, bm[:, g * SSM_STATE:(g + 1) * SSM_STATE], 0.0)
            b_pad = jnp.concatenate([b_g, jnp.zeros((LANES - q, SSM_STATE), F32)], axis=0).astype(BF16)
            zt_g = zt[g * hpg:(g + 1) * hpg, :]
            upd = jnp.dot(zt_g.astype(BF16), b_pad, preferred_element_type=F32)
            decay = zt_g[:, q + sl:q + sl + 1]
            sn_ref[sl, g * hpg:(g + 1) * hpg, :] = s_old * decay + upd
    y_off = jnp.concatenate(y_off_g, axis=1)

    @pl.when(hf == 0)
    def _():
        yoff_sc[...] = y_off

    @pl.when(hf == 1)
    def _():
        y = y_diag + (yoff_sc[...] + y_off) * eax + xs * dsk_ref[...]
        yg_ref[...] = _gate_norm(y, z_ref[...], gn_ref[...], SSM_GROUPS)


def _ssd_sample(z, xp7, dt, s0, cw, cb, dtb, alog, dsk, gnw, e01, g1, tp, n_tok):
    nsteps = xp7.shape[0]
    nb = SEQ_PER_STEP
    half = nb // 2
    q = n_tok * nb
    di = z.shape[1]
    cd = xp7.shape[3]
    base = tp // q
    rows = lambda s, hf: (base + s, 0)
    return pl.pallas_call(
        functools.partial(_ssd_sample_kernel, n_tok=n_tok),
        grid=(nsteps, 2),
        in_specs=[
            pl.BlockSpec((q, di), rows),
            pl.BlockSpec((1, n_tok + SSM_CONV - 1, nb, cd), lambda s, hf: (s, 0, 0, 0)),
            pl.BlockSpec((q, LANES), rows),
            pl.BlockSpec((half, di, SSM_STATE), lambda s, hf: (2 * s + hf, 0, 0)),
            _const_spec(cw.shape), _const_spec(cb.shape), _const_spec(dtb.shape), _const_spec(alog.shape),
            _const_spec(dsk.shape), _const_spec(gnw.shape), _const_spec(e01.shape), _const_spec(g1.shape),
        ],
        out_specs=[
            pl.BlockSpec((q, di), lambda s, hf: (s, 0)),
            pl.BlockSpec((half, di, SSM_STATE), lambda s, hf: (2 * s + hf, 0, 0)),
        ],
        out_shape=[jax.ShapeDtypeStruct((nsteps * q, di), F32),
                   jax.ShapeDtypeStruct(s0.shape, F32)],
        scratch_shapes=[pltpu.VMEM((q, di), F32)],
        compiler_params=_cparams(("arbitrary", "arbitrary")),
        name="ssd_sample",
    )(z, xp7, dt, s0, cw, cb, dtb, alog, dsk, gnw, e01, g1)


def _qkv_kernel(h_ref, kvn_ref, an_ref, wkv_ref, wq_ref, kn_ref, qn_ref, rc_ref, rs1_ref, rs2_ref,
                hsum_ref, rq_ref, eq_ref, q_ref, k_ref, v_ref):
    hn = _rms_scale(h_ref[...])
    kvw = k_ref.shape[1]
    kv = jnp.dot((hn * kvn_ref[...]).astype(BF16), wkv_ref[...], preferred_element_type=F32)
    k = kv[:, :kvw]
    v_ref[...] = kv[:, kvw:]
    inv_hd = 1.0 / ATT_HEAD_DIM
    k = k * lax.rsqrt(_mm01(k * k, hsum_ref[...]) * inv_hd + EPS) * kn_ref[...]
    rc, rs1, rs2 = rc_ref[...], rs1_ref[...], rs2_ref[...]
    rk = kvw // LANES
    k_ref[...] = _rope(k, _tile_lanes(rc, rk), _tile_lanes(rs1, rk), _tile_lanes(rs2, rk))
    q = jnp.dot((hn * an_ref[...]).astype(BF16), wq_ref[...], preferred_element_type=F32)
    rsq = lax.rsqrt(_mm01(q * q, rq_ref[...]) * inv_hd + EPS)
    q = q * _mm01(rsq, eq_ref[...]) * qn_ref[...]
    rq = q.shape[1] // LANES
    q_ref[...] = _rope(q, _tile_lanes(rc, rq), _tile_lanes(rs1, rq), _tile_lanes(rs2, rq))


def _qkv(h, kvn, an, wkv, wq, knt, qnt, rc, rs1, rs2, hsum, rq, eq):
    t, d = h.shape
    tm = TOKEN_TILE
    kvw = wkv.shape[1] // 2
    qw = wq.shape[1]
    rows = lambda i: (i, 0)
    return pl.pallas_call(
        _qkv_kernel,
        grid=(t // tm,),
        in_specs=[pl.BlockSpec((tm, d), rows)] + [_const_spec(a.shape) for a in (kvn, an, wkv, wq, knt, qnt)]
        + [pl.BlockSpec((tm, LANES), rows)] * 3 + [_const_spec(a.shape) for a in (hsum, rq, eq)],
        out_specs=[pl.BlockSpec((tm, qw), rows), pl.BlockSpec((tm, kvw), rows), pl.BlockSpec((tm, kvw), rows)],
        out_shape=[jax.ShapeDtypeStruct((t, qw), F32), jax.ShapeDtypeStruct((t, kvw), F32),
                   jax.ShapeDtypeStruct((t, kvw), F32)],
        compiler_params=_cparams(("arbitrary",)),
        name="qkv",
    )(h, kvn, an, wkv, wq, knt, qnt, rc, rs1, rs2, hsum, rq, eq)


def _attn_prompt_kernel(sink_ref, q_ref, kc_ref, kp_ref, vc_ref, vp_ref, o_ref):
    i = pl.program_id(1)
    w = WINDOW
    hd = ATT_HEAD_DIM
    n_q = q_ref.shape[1] // hd
    grp = n_q // ATT_KV_HEADS
    row = lax.broadcasted_iota(I32, (w, 2 * w), 0)
    col = lax.broadcasted_iota(I32, (w, 2 * w), 1)
    dist = row + w - col
    mask = (dist >= 0) & (dist < w) & ((col >= w) | (i > 0))
    q = q_ref[...]
    outs = []
    for g in range(ATT_KV_HEADS):
        sl = slice(g * hd, (g + 1) * hd)
        kk = jnp.concatenate([kp_ref[:, sl], kc_ref[:, sl]], axis=0).astype(BF16)
        vv = jnp.concatenate([vp_ref[:, sl], vc_ref[:, sl]], axis=0).astype(BF16)
        for hq in range(grp):
            h = g * grp + hq
            s = _dot_nt(q[:, h * hd:(h + 1) * hd].astype(BF16), kk)
            s = jnp.where(mask, s * (hd ** -0.5), -jnp.inf)
            sink = sink_ref[h]
            m = jnp.maximum(jnp.max(s, axis=-1, keepdims=True), sink)
            p = jnp.exp(s - m)
            denom = jnp.sum(p, axis=-1, keepdims=True) + jnp.exp(sink - m)
            outs.append(jnp.dot((p / denom).astype(BF16), vv, preferred_element_type=F32))
    o_ref[...] = jnp.concatenate(outs, axis=1)


def _attn_prompt(sinks, q, k, v, bp, seq):
    w = WINDOW
    nb = seq // w
    qw, kvw = q.shape[1], k.shape[1]
    cur = lambda b, i: (b * nb + i, 0)
    prev = lambda b, i: (b * nb + jnp.maximum(i - 1, 0), 0)
    return pl.pallas_call(
        _attn_prompt_kernel,
        grid=(bp, nb),
        in_specs=[pl.BlockSpec(memory_space=pltpu.SMEM),
                  pl.BlockSpec((w, qw), cur), pl.BlockSpec((w, kvw), cur), pl.BlockSpec((w, kvw), prev),
                  pl.BlockSpec((w, kvw), cur), pl.BlockSpec((w, kvw), prev)],
        out_specs=pl.BlockSpec((w, qw), cur),
        out_shape=jax.ShapeDtypeStruct((bp * seq, qw), F32),
        compiler_params=_cparams(("arbitrary", "arbitrary")),
        name="attn_prompt",
    )(sinks, q, k, k, v, v)


def _attn_sample_kernel(sink_ref, q_ref, kn_ref, vn_ref, kc_ref, vc_ref, o_ref, *, n_tok):
    nb = SEQ_PER_STEP
    qn = n_tok * nb
    hd = ATT_HEAD_DIM
    lc = kc_ref.shape[1]
    n_q = q_ref.shape[1] // hd
    grp = n_q // ATT_KV_HEADS
    rows = grp * qn
    r = lax.broadcasted_iota(I32, (rows, 1), 0)
    r_seq = r % nb
    r_tok = (r % qn) // nb
    ccol = lax.broadcasted_iota(I32, (rows, lc), 1)
    mask_c = ccol >= r_tok + 1 + (lc - WINDOW)
    ncol = lax.broadcasted_iota(I32, (rows, LANES), 1)
    mask_n = (ncol < qn) & (ncol % nb == r_seq) & (ncol // nb <= r_tok)
    q = q_ref[...]
    zpad = jnp.zeros((LANES - qn, hd), F32)
    outs = [None] * n_q
    for g in range(ATT_KV_HEADS):
        sl = slice(g * hd, (g + 1) * hd)
        qs = jnp.concatenate([q[:, (g * grp + hq) * hd:(g * grp + hq + 1) * hd] for hq in range(grp)],
                             axis=0).astype(BF16)
        k_new = jnp.concatenate([kn_ref[:, sl], zpad], axis=0).astype(BF16)
        v_new = jnp.concatenate([vn_ref[:, sl], zpad], axis=0).astype(BF16)
        s_n = jnp.where(mask_n, _dot_nt(qs, k_new) * (hd ** -0.5), -jnp.inf)
        s_c = jnp.zeros((rows, lc), F32)
        for sq in range(nb):
            s_sq = _dot_nt(qs, kc_ref[sq, :, sl].astype(BF16))
            s_c = jnp.where(r_seq == sq, s_sq, s_c)
        s_c = jnp.where(mask_c, s_c * (hd ** -0.5), -jnp.inf)
        sink = jnp.zeros((rows, 1), F32)
        for hq in range(grp):
            sink = jnp.where(r // qn == hq, sink_ref[g * grp + hq], sink)
        m = jnp.maximum(jnp.maximum(jnp.max(s_c, axis=-1, keepdims=True),
                                    jnp.max(s_n, axis=-1, keepdims=True)), sink)
        p_c = jnp.exp(s_c - m)
        p_n = jnp.exp(s_n - m)
        denom = (jnp.sum(p_c, axis=-1, keepdims=True) + jnp.sum(p_n, axis=-1, keepdims=True)
                 + jnp.exp(sink - m))
        p_c = p_c / denom
        o = jnp.dot((p_n / denom).astype(BF16), v_new, preferred_element_type=F32)
        for sq in range(nb):
            o = o + jnp.dot(jnp.where(r_seq == sq, p_c, 0.0).astype(BF16), vc_ref[sq, :, sl].astype(BF16),
                            preferred_element_type=F32)
        for hq in range(grp):
            outs[g * grp + hq] = o[hq * qn:(hq + 1) * qn, :]
    o_ref[...] = jnp.concatenate(outs, axis=1)


def _attn_sample(sinks, q, k, v, kc, vc, tp, n_tok):
    nb = SEQ_PER_STEP
    qn = n_tok * nb
    nsteps = kc.shape[0] // nb
    lc, kvw = kc.shape[1], kc.shape[2]
    qw = q.shape[1]
    base = tp // qn
    rows = lambda s: (base + s, 0)
    return pl.pallas_call(
        functools.partial(_attn_sample_kernel, n_tok=n_tok),
        grid=(nsteps,),
        in_specs=[pl.BlockSpec(memory_space=pltpu.SMEM),
                  pl.BlockSpec((qn, qw), rows), pl.BlockSpec((qn, kvw), rows), pl.BlockSpec((qn, kvw), rows),
                  pl.BlockSpec((nb, lc, kvw), lambda s: (s, 0, 0)), pl.BlockSpec((nb, lc, kvw), lambda s: (s, 0, 0))],
        out_specs=pl.BlockSpec((qn, qw), lambda s: (s, 0)),
        out_shape=jax.ShapeDtypeStruct((nsteps * qn, qw), F32),
        compiler_params=_cparams(("arbitrary",)),
        name="attn_sample",
    )(sinks, q, k, v, kc, vc)


def _router_kernel(h_ref, fn_ref, wh_ref, wl_ref, b_ref, tril_ref, xn_ref, info_ref, cnt_ref, carry_sc):
    i = pl.program_id(0)

    @pl.when(i == 0)
    def _():
        carry_sc[...] = jnp.zeros(carry_sc.shape, F32)

    xn = _rms_scale(h_ref[...]) * fn_ref[...]
    xn_ref[...] = xn
    x_hi, x_lo = _split_bf16(xn, 2)
    wh, wl = wh_ref[...], wl_ref[...]
    logits = (jnp.dot(x_hi, wh, preferred_element_type=F32) + jnp.dot(x_hi, wl, preferred_element_type=F32)
              + jnp.dot(x_lo, wh, preferred_element_type=F32)) + b_ref[...]
    tm = logits.shape[0]
    lane = lax.broadcasted_iota(I32, (tm, LANES), 1).astype(F32)
    big = float(LANES)
    neg = -jnp.inf

    is_grp = (lane >= MOE_EXPERTS) & (lane < MOE_EXPERTS + MOE_GROUPS)
    lg = jnp.where(is_grp, logits, neg)
    mg = jnp.max(lg, axis=-1, keepdims=True)
    gp = 1.0 / jnp.sum(jnp.exp(lg - mg), axis=-1, keepdims=True)
    gi = jnp.min(jnp.where(lg == mg, lane, big), axis=-1, keepdims=True) - MOE_EXPERTS

    lo = gi * MOE_EXPERTS_PER_GROUP
    le = jnp.where((lane >= lo) & (lane < lo + MOE_EXPERTS_PER_GROUP), logits, neg)
    m1 = jnp.max(le, axis=-1, keepdims=True)
    i1 = jnp.min(jnp.where(le == m1, lane, big), axis=-1, keepdims=True)
    le2 = jnp.where(lane == i1, neg, le)
    m2 = jnp.max(le2, axis=-1, keepdims=True)
    i2 = jnp.min(jnp.where(le2 == m2, lane, big), axis=-1, keepdims=True)
    e2 = jnp.exp(m2 - m1)
    g1 = gp * (1.0 / (1.0 + e2))
    g2 = gp * (e2 / (1.0 + e2))

    a1 = lane == i1
    a2 = lane == i2
    onehot = jnp.where(a1 | a2, 1.0, 0.0)
    before = jnp.dot(tril_ref[...], onehot.astype(BF16), preferred_element_type=F32) + carry_sc[...]
    r1 = jnp.sum(jnp.where(a1, before, 0.0), axis=-1, keepdims=True)
    r2 = jnp.sum(jnp.where(a2, before, 0.0), axis=-1, keepdims=True)
    carry_sc[...] = carry_sc[...] + jnp.sum(onehot, axis=0, keepdims=True)
    cnt_ref[...] = carry_sc[...]

    cols = (i1, i2, g1, g2, r1, r2)
    info = jnp.zeros((tm, LANES), F32)
    for k, cval in enumerate(cols):
        info = jnp.where(lane == k, cval, info)
    info_ref[...] = info


def _router(h, fn, wh, wl, b, tril):
    t, d = h.shape
    tm = TOKEN_TILE
    rows = lambda i: (i, 0)
    return pl.pallas_call(
        _router_kernel,
        grid=(t // tm,),
        in_specs=[pl.BlockSpec((tm, d), rows)] + [_const_spec(a.shape) for a in (fn, wh, wl, b, tril)],
        out_specs=[pl.BlockSpec((tm, d), rows), pl.BlockSpec((tm, LANES), rows), _const_spec((1, LANES))],
        out_shape=[jax.ShapeDtypeStruct((t, d), F32), jax.ShapeDtypeStruct((t, LANES), F32),
                   jax.ShapeDtypeStruct((1, LANES), F32)],
        scratch_shapes=[pltpu.VMEM((1, LANES), F32)],
        compiler_params=_cparams(("arbitrary",)),
        name="moe_router",
    )(h, fn, wh, wl, b, tril)


def _row_gather_start(src_hbm, idx_ref, base, dst, sem, n_rows):
    def body(r, carry):
        pltpu.make_async_copy(src_hbm.at[pl.ds(idx_ref[base + r], 1)], dst.at[pl.ds(r, 1)], sem).start()
        return carry
    lax.fori_loop(0, n_rows, body, 0, unroll=8)


def _row_gather_wait(dst, sem):
    pltpu.make_async_copy(dst, dst, sem).wait()


def _expert_kernel(blk_e_ref, row_tok_ref, nvalid_ref, xn_hbm, wg_ref, wu_ref, wd_ref, y_ref,
                   xg_sc, sem, wg_sc, wu_sc, wd_sc):
    i = pl.program_id(0)
    nv = nvalid_ref[0]
    rows = xg_sc.shape[1]

    def start(blk):
        slot = blk % 2
        _row_gather_start(xn_hbm, row_tok_ref, blk * rows, xg_sc.at[slot], sem.at[slot], rows)

    @pl.when(i == 0)
    def _():
        start(i)

    @pl.when(i + 1 < nv)
    def _():
        start(i + 1)

    @pl.when(i < nv)
    def _():
        slot = i % 2
        _row_gather_wait(xg_sc.at[slot], sem.at[slot])
        e = blk_e_ref[i]
        e_prev = blk_e_ref[jnp.maximum(i - 1, 0)]

        @pl.when((i == 0) | (e != e_prev))
        def _():
            wg_sc[...] = wg_ref[0].astype(BF16)
            wu_sc[...] = wu_ref[0].astype(BF16)
            wd_sc[...] = wd_ref[0].astype(BF16)

        x = xg_sc[slot].astype(BF16)
        hid = _silu(jnp.dot(x, wg_sc[...], preferred_element_type=F32)) * jnp.dot(
            x, wu_sc[...], preferred_element_type=F32)
        y_ref[...] = jnp.dot(hid.astype(BF16), wd_sc[...], preferred_element_type=F32)

    @pl.when(i >= nv)
    def _():
        y_ref[...] = jnp.zeros(y_ref.shape, F32)


def _experts(blk_e, row_tok, nvalid, xn, wg, wu, wd):
    nblk = blk_e.shape[0]
    d = xn.shape[1]
    hdim = wg.shape[2]
    rows = MOE_BLOCK
    wsel = lambda i, be, rt, nv: (be[i], 0, 0)
    grid_spec = pltpu.PrefetchScalarGridSpec(
        num_scalar_prefetch=3,
        grid=(nblk,),
        in_specs=[pl.BlockSpec(memory_space=pl.ANY),
                  pl.BlockSpec((1, d, hdim), wsel), pl.BlockSpec((1, d, hdim), wsel),
                  pl.BlockSpec((1, hdim, d), wsel)],
        out_specs=pl.BlockSpec((rows, d), lambda i, be, rt, nv: (i, 0)),
        scratch_shapes=[pltpu.VMEM((2, rows, d), F32), pltpu.SemaphoreType.DMA((2,)),
                        pltpu.VMEM((d, hdim), BF16), pltpu.VMEM((d, hdim), BF16), pltpu.VMEM((hdim, d), BF16)],
    )
    return pl.pallas_call(
        _expert_kernel,
        grid_spec=grid_spec,
        out_shape=jax.ShapeDtypeStruct((nblk * rows, d), F32),
        compiler_params=_cparams(("arbitrary",)),
        name="moe_experts",
    )(blk_e, row_tok, nvalid, xn, wg, wu, wd)


def _combine_kernel(d1_ref, d2_ref, h_ref, info_ref, y_hbm, o_ref, r_sc, sem, *, base_tile):
    i = pl.program_id(0)
    n = pl.num_programs(0)
    tm = h_ref.shape[0]

    def start(step):
        slot = step % 2
        base = (base_tile + step) * tm
        _row_gather_start(y_hbm, d1_ref, base, r_sc.at[slot, 0], sem.at[slot, 0], tm)
        _row_gather_start(y_hbm, d2_ref, base, r_sc.at[slot, 1], sem.at[slot, 1], tm)

    @pl.when(i == 0)
    def _():
        start(i)

    @pl.when(i + 1 < n)
    def _():
        start(i + 1)

    slot = i % 2
    _row_gather_wait(r_sc.at[slot, 0], sem.at[slot, 0])
    _row_gather_wait(r_sc.at[slot, 1], sem.at[slot, 1])
    info = info_ref[...]
    o_ref[...] = h_ref[...] + (r_sc[slot, 0] * info[:, 2:3] + r_sc[slot, 1] * info[:, 3:4])


def _combine(dest1, dest2, h, info, ybuf, row0, nrows):
    d = h.shape[1]
    tm = MOE_BLOCK
    base_tile = row0 // tm
    rows = lambda i, a, b: (base_tile + i, 0)
    grid_spec = pltpu.PrefetchScalarGridSpec(
        num_scalar_prefetch=2,
        grid=(nrows // tm,),
        in_specs=[pl.BlockSpec((tm, d), rows), pl.BlockSpec((tm, LANES), rows), pl.BlockSpec(memory_space=pl.ANY)],
        out_specs=pl.BlockSpec((tm, d), lambda i, a, b: (i, 0)),
        scratch_shapes=[pltpu.VMEM((2, 2, tm, d), F32), pltpu.SemaphoreType.DMA((2, 2))],
    )
    return pl.pallas_call(
        functools.partial(_combine_kernel, base_tile=base_tile),
        grid_spec=grid_spec,
        out_shape=jax.ShapeDtypeStruct((nrows, d), F32),
        compiler_params=_cparams(("arbitrary",)),
        name="moe_combine",
    )(dest1, dest2, h, info, ybuf)


def _moe(h, fn, w_grp, b_grp, w_rt, b_rt, wg, wu, wd, tril, out_ranges):
    t, d = h.shape
    w_cat = jnp.zeros((d, LANES), F32).at[:, :MOE_EXPERTS].set(w_rt).at[:, MOE_EXPERTS:MOE_EXPERTS + MOE_GROUPS].set(w_grp)
    b_cat = jnp.zeros((1, LANES), F32).at[0, :MOE_EXPERTS].set(b_rt).at[0, MOE_EXPERTS:MOE_EXPERTS + MOE_GROUPS].set(b_grp)
    w_hi = w_cat.astype(BF16)
    w_lo = (w_cat - w_hi.astype(F32)).astype(BF16)
    xn, info, cnt = _router(h, fn.reshape(1, d), w_hi, w_lo, b_cat, tril)

    counts = cnt[0, :MOE_EXPERTS].astype(I32)
    pcounts = (counts + MOE_BLOCK - 1) // MOE_BLOCK * MOE_BLOCK
    pends = jnp.cumsum(pcounts)
    pstarts = pends - pcounts
    e1, e2 = info[:, 0].astype(I32), info[:, 1].astype(I32)
    dest1 = pstarts[e1] + info[:, 4].astype(I32)
    dest2 = pstarts[e2] + info[:, 5].astype(I32)
    nblk = -(-(2 * t + MOE_EXPERTS * (MOE_BLOCK - 1)) // MOE_BLOCK)
    tok = jnp.arange(t, dtype=I32)
    row_tok = jnp.zeros((nblk * MOE_BLOCK,), I32).at[dest1].set(tok).at[dest2].set(tok)
    blk_start = jnp.arange(nblk, dtype=I32) * MOE_BLOCK
    blk_e = jnp.minimum(jnp.searchsorted(pends, blk_start, side='right'), MOE_EXPERTS - 1).astype(I32)
    nvalid = (pends[-1:] // MOE_BLOCK).astype(I32)

    ybuf = _experts(blk_e, row_tok, nvalid, xn, wg, wu, wd)
    return [_combine(dest1, dest2, h, info, ybuf, r0, nr) for r0, nr in out_ranges]


def _rope_tables(pos):
    half = ROT_DIM // 2
    inv = ROPE_THETA ** (-jnp.arange(0, ROT_DIM, 2, dtype=F32) / ROT_DIM)
    ang = pos.astype(F32)[:, None] * inv[None, :]
    cos, sin = jnp.cos(ang), jnp.sin(ang)
    n = pos.shape[0]
    ones = jnp.ones((n, ATT_HEAD_DIM - ROT_DIM), F32)
    zeros_r = jnp.zeros((n, ATT_HEAD_DIM - ROT_DIM), F32)
    zeros_h = jnp.zeros((n, half), F32)
    c = jnp.concatenate([cos, cos, ones], axis=1)
    s1 = jnp.concatenate([-sin, zeros_h, zeros_r], axis=1)
    s2 = jnp.concatenate([zeros_h, sin, zeros_r], axis=1)
    reps = LANES // ATT_HEAD_DIM
    return tuple(jnp.tile(a, (1, reps)) for a in (c, s1, s2))


def _to_step_order(a, nsteps, n_tok):
    c = a.shape[-1]
    return a.reshape(nsteps, SEQ_PER_STEP, n_tok, c).transpose(0, 2, 1, 3).reshape(nsteps * n_tok * SEQ_PER_STEP, c)


def _from_step_order(a, nsteps, n_tok):
    c = a.shape[-1]
    return a.reshape(nsteps, n_tok, SEQ_PER_STEP, c).transpose(0, 2, 1, 3).reshape(nsteps * SEQ_PER_STEP, n_tok, c)


def kernel(x_prompt, x_sample, state_ssm, state_conv, cache_k_win, cache_v_win, ssm_norm, ssm_w_in, ssm_conv_w, ssm_conv_b, ssm_dt_bias, ssm_a_log, ssm_d, ssm_gate_norm, ssm_w_out, kv_norm, w_kv, k_norm, attn_norm, w_q, q_norm, sinks, w_o, ffn_norm, moe_w_group, moe_b_group, moe_w_router, moe_b_router, moe_w_gate, moe_w_up, moe_w_down):
    bp, seq, d = x_prompt.shape
    bs, n_tok, _ = x_sample.shape
    tp, ts = bp * seq, bs * n_tok
    nsteps = bs // SEQ_PER_STEP
    n_heads = ssm_d.shape[1]
    di = n_heads * SSM_HEAD_DIM
    gn_w = SSM_GROUPS * SSM_STATE
    cdim = di + 2 * gn_w
    n_q = sinks.shape[1]
    kvw = ATT_KV_HEADS * ATT_HEAD_DIM

    xp2 = x_prompt.reshape(tp, d)
    xs2 = _to_step_order(x_sample, nsteps, n_tok)

    lane_i = jnp.arange(LANES)
    e01 = (lane_i[:, None] == (jnp.arange(di) // SSM_HEAD_DIM)[None, :]).astype(BF16)
    hpg = di // SSM_GROUPS
    g1 = ((jnp.arange(gn_w) // SSM_STATE)[:, None] == (lane_i // (hpg // SSM_HEAD_DIM))[None, :])
    g1 = (g1 & (lane_i < n_heads)[None, :]).astype(BF16)
    tril_c = (jnp.arange(SSM_CHUNK)[:, None] >= jnp.arange(SSM_CHUNK)[None, :]).astype(BF16)
    tril_x = (jnp.arange(TOKEN_TILE)[:, None] > jnp.arange(TOKEN_TILE)[None, :]).astype(BF16)
    hsum = ((jnp.arange(kvw) // ATT_HEAD_DIM)[:, None] == (jnp.arange(kvw) // ATT_HEAD_DIM)[None, :]).astype(BF16)
    qw = n_q * ATT_HEAD_DIM
    rq = ((jnp.arange(qw) // ATT_HEAD_DIM)[:, None] == lane_i[None, :]).astype(BF16)
    eq = rq.T

    w_in = ssm_w_in[0]
    wz = w_in[:, :di].astype(BF16)
    wx = w_in[:, di:di + cdim].astype(BF16)
    wd = jnp.zeros((d, LANES), F32).at[:, :n_heads].set(w_in[:, di + cdim:]).astype(BF16)
    z, xbc, dt = _inproj(xp2, xs2, ssm_norm[0].reshape(1, d), wz, wx, wd)

    pad_h = lambda v: jnp.zeros((1, LANES), F32).at[0, :n_heads].set(v)
    cw, cb = ssm_conv_w[0], ssm_conv_b[0].reshape(1, cdim)
    dtb, alog = pad_h(ssm_dt_bias[0]), pad_h(ssm_a_log[0])
    dsk = jnp.repeat(ssm_d[0], SSM_HEAD_DIM).reshape(1, di)
    gnw = ssm_gate_norm[0].reshape(1, di)

    yg_p, s_fin, c_fin = _ssd_prompt(z, xbc, dt, cw, cb, dtb, alog, dsk, gnw, tril_c, e01, bp, seq)
    ssm_p = s_fin.reshape(1, bp, n_heads, SSM_HEAD_DIM, SSM_STATE)
    conv_p = c_fin[:, SUBLANES - (SSM_CONV - 1):, :].reshape(1, bp, SSM_CONV - 1, cdim)

    xbc_s = xbc[tp:].reshape(nsteps, n_tok, SEQ_PER_STEP, cdim)
    conv_in = state_conv[0].reshape(nsteps, SEQ_PER_STEP, SSM_CONV - 1, cdim).transpose(0, 2, 1, 3)
    xp7 = jnp.concatenate([conv_in, xbc_s], axis=1)
    s0 = state_ssm[0].reshape(bs, di, SSM_STATE)
    yg_s, s_new = _ssd_sample(z, xp7, dt, s0, cw, cb, dtb, alog, dsk, gnw, e01, g1, tp, n_tok)
    ssm_s = s_new.reshape(1, bs, n_heads, SSM_HEAD_DIM, SSM_STATE)
    conv_s = _from_step_order(xbc[tp:], nsteps, n_tok)[:, n_tok - (SSM_CONV - 1):, :].reshape(
        1, bs, SSM_CONV - 1, cdim)

    h = _proj_res([yg_p, yg_s], ssm_w_out[0].astype(BF16), [xp2, xs2], tp, ts)
    (h,) = _moe(h, ffn_norm[0], moe_w_group[0], moe_b_group[0], moe_w_router[0], moe_b_router[0],
                moe_w_gate[0], moe_w_up[0], moe_w_down[0], tril_x, [(0, tp + ts)])

    pos = jnp.concatenate([jnp.tile(jnp.arange(seq, dtype=I32), bp),
                           jnp.tile(jnp.repeat(PAST_LEN + jnp.arange(n_tok, dtype=I32), SEQ_PER_STEP), nsteps)])
    rc, rs1, rs2 = _rope_tables(pos)
    q, k, v = _qkv(h, kv_norm.reshape(1, d), attn_norm[0].reshape(1, d), w_kv.astype(BF16), w_q[0].astype(BF16),
                   jnp.tile(k_norm, ATT_KV_HEADS).reshape(1, kvw), jnp.tile(q_norm[0], n_q).reshape(1, qw),
                   rc, rs1, rs2, hsum, rq, eq)
    sk = sinks[0]
    o_p = _attn_prompt(sk, q, k, v, bp, seq)
    lc = cache_k_win.shape[1]
    kc = cache_k_win.reshape(bs, lc, kvw)
    vc = cache_v_win.reshape(bs, lc, kvw)
    o_s = _attn_sample(sk, q, k, v, kc, vc, tp, n_tok)
    h = _proj_res([o_p, o_s], w_o[0].astype(BF16), [h], tp, ts)
    y_p, y_s = _moe(h, ffn_norm[1], moe_w_group[1], moe_b_group[1], moe_w_router[1], moe_b_router[1],
                    moe_w_gate[1], moe_w_up[1], moe_w_down[1], tril_x, [(0, tp), (tp, ts)])

    wl = min(WINDOW, seq)
    k_p = k[:tp].reshape(bp, seq, ATT_KV_HEADS, ATT_HEAD_DIM)[:, seq - wl:]
    v_p = v[:tp].reshape(bp, seq, ATT_KV_HEADS, ATT_HEAD_DIM)[:, seq - wl:]
    k_new = _from_step_order(k[tp:], nsteps, n_tok)
    v_new = _from_step_order(v[tp:], nsteps, n_tok)
    k_s = jnp.concatenate([kc, k_new], axis=1)[:, n_tok:].reshape(bs, lc, ATT_KV_HEADS, ATT_HEAD_DIM)
    v_s = jnp.concatenate([vc, v_new], axis=1)[:, n_tok:].reshape(bs, lc, ATT_KV_HEADS, ATT_HEAD_DIM)
    return (y_p.reshape(bp, seq, d), _from_step_order(y_s, nsteps, n_tok),
            ssm_p, conv_p, k_p, v_p, ssm_s, conv_s, k_s, v_s)
```

```python
import functools

import jax
import jax.numpy as jnp
from jax import lax
from jax.experimental import pallas as pl
from jax.experimental.pallas import tpu as pltpu

F32 = jnp.float32
BF16 = jnp.bfloat16
I32 = jnp.int32

EPS = 1e-6
SSM_HEAD_DIM = 64
SSM_GROUPS = 4
SSM_STATE = 128
SSM_CONV = 4
SSM_CHUNK = 128
ATT_HEAD_DIM = 64
ATT_KV_HEADS = 4
WINDOW = 128
ROT_DIM = ATT_HEAD_DIM // 4
ROPE_THETA = 500000.0
MOE_GROUPS = 4
MOE_EXPERTS_PER_GROUP = 8
MOE_EXPERTS = MOE_GROUPS * MOE_EXPERTS_PER_GROUP
MOE_BLOCK = 128
PAST_LEN = 16384

LANES = 128
SUBLANES = 8
SEQ_PER_STEP = SUBLANES
TOKEN_TILE = 256
ROW_TILES = 8
EXPERT_ROWS = 256
VMEM_LIMIT = 56 * 1024 * 1024


def _cparams(sem):
    return pltpu.CompilerParams(dimension_semantics=sem, vmem_limit_bytes=VMEM_LIMIT)


def _const_spec(shape):
    nd = len(shape)
    return pl.BlockSpec(shape, lambda *_: (0,) * nd)


def _split_bf16(v, n):
    parts = []
    r = v
    for k in range(n):
        p = r.astype(BF16)
        parts.append(p)
        if k + 1 < n:
            r = r - p.astype(F32)
    return parts


def _mm01(v, m01, n=3):
    acc = None
    for p in _split_bf16(v, n):
        d = jnp.dot(p, m01, preferred_element_type=F32)
        acc = d if acc is None else acc + d
    return acc


def _mm01_left(m01, v, n=3):
    acc = None
    for p in _split_bf16(v, n):
        d = jnp.dot(m01, p, preferred_element_type=F32)
        acc = d if acc is None else acc + d
    return acc


def _dot_nt(a, b):
    return lax.dot_general(a, b, (((1,), (1,)), ((), ())), preferred_element_type=F32)


def _sigmoid(x):
    return 1.0 / (1.0 + jnp.exp(-x))


def _silu(x):
    return x * _sigmoid(x)


def _softplus(x):
    return jnp.maximum(x, 0.0) + jnp.log1p(jnp.exp(-jnp.abs(x)))


def _rms_scale(x):
    return x * lax.rsqrt(jnp.mean(x * x, axis=-1, keepdims=True) + EPS)


def _gate_norm(y, z, gn, n_groups):
    yz = y * _silu(z)
    w = yz.shape[-1] // n_groups
    outs = []
    for g in range(n_groups):
        v = yz[:, g * w:(g + 1) * w]
        outs.append(_rms_scale(v) * gn[:, g * w:(g + 1) * w])
    return jnp.concatenate(outs, axis=1)


def _rope(x, c, s1, s2):
    w = x.shape[-1]
    return x * c + pltpu.roll(x, w - ROT_DIM // 2, 1) * s1 + pltpu.roll(x, ROT_DIM // 2, 1) * s2


def _tile_lanes(t, reps):
    return t if reps == 1 else jnp.concatenate([t] * reps, axis=1)


def _inproj_kernel(xp_ref, xs_ref, g_ref, wz_ref, wx_ref, wd_ref, z_ref, xbc_ref, dt_ref, *, n_p):
    i = pl.program_id(0)
    x = jnp.where(i < n_p, xp_ref[...], xs_ref[...])
    xn = (_rms_scale(x) * g_ref[...]).astype(BF16)
    z_ref[...] = jnp.dot(xn, wz_ref[...], preferred_element_type=F32)
    xbc_ref[...] = jnp.dot(xn, wx_ref[...], preferred_element_type=F32)
    dt_ref[...] = jnp.dot(xn, wd_ref[...], preferred_element_type=F32)


def _inproj(xp2, xs2, g, wz, wx, wd):
    tp, d = xp2.shape
    ts = xs2.shape[0]
    tm = TOKEN_TILE
    n_p, n_s = tp // tm, ts // tm
    t = tp + ts
    return pl.pallas_call(
        functools.partial(_inproj_kernel, n_p=n_p),
        grid=(n_p + n_s,),
        in_specs=[
            pl.BlockSpec((tm, d), lambda i: (jnp.minimum(i, n_p - 1), 0)),
            pl.BlockSpec((tm, d), lambda i: (jnp.maximum(i - n_p, 0), 0)),
            _const_spec(g.shape), _const_spec(wz.shape), _const_spec(wx.shape), _const_spec(wd.shape),
        ],
        out_specs=[
            pl.BlockSpec((tm, wz.shape[1]), lambda i: (i, 0)),
            pl.BlockSpec((tm, wx.shape[1]), lambda i: (i, 0)),
            pl.BlockSpec((tm, wd.shape[1]), lambda i: (i, 0)),
        ],
        out_shape=[jax.ShapeDtypeStruct((t, wz.shape[1]), F32),
                   jax.ShapeDtypeStruct((t, wx.shape[1]), F32),
                   jax.ShapeDtypeStruct((t, wd.shape[1]), F32)],
        compiler_params=_cparams(("arbitrary",)),
        name="inproj",
    )(xp2, xs2, g, wz, wx, wd)


def _proj_res_kernel(*refs, n_p, two_a, two_r):
    refs = list(refs)
    o_ref = refs.pop()
    i = pl.program_id(0)
    a_p = refs.pop(0)
    a = jnp.where(i < n_p, a_p[...], refs.pop(0)[...]) if two_a else a_p[...]
    w_ref = refs.pop(0)
    r_p = refs.pop(0)
    r = jnp.where(i < n_p, r_p[...], refs.pop(0)[...]) if two_r else r_p[...]
    o_ref[...] = r + jnp.dot(a.astype(BF16), w_ref[...], preferred_element_type=F32)


def _proj_res(a_list, w, r_list, tp, ts):
    tm = TOKEN_TILE
    n_p, n_s = tp // tm, ts // tm
    k, n = w.shape

    def specs(lst, width):
        if len(lst) == 2:
            return [pl.BlockSpec((tm, width), lambda i: (jnp.minimum(i, n_p - 1), 0)),
                    pl.BlockSpec((tm, width), lambda i: (jnp.maximum(i - n_p, 0), 0))]
        return [pl.BlockSpec((tm, width), lambda i: (i, 0))]

    return pl.pallas_call(
        functools.partial(_proj_res_kernel, n_p=n_p, two_a=len(a_list) == 2, two_r=len(r_list) == 2),
        grid=(n_p + n_s,),
        in_specs=specs(a_list, k) + [_const_spec(w.shape)] + specs(r_list, n),
        out_specs=pl.BlockSpec((tm, n), lambda i: (i, 0)),
        out_shape=jax.ShapeDtypeStruct((tp + ts, n), F32),
        compiler_params=_cparams(("arbitrary",)),
        name="proj_res",
    )(*a_list, w, *r_list)


def _ssd_prompt_kernel(z_ref, xbc_ref, dt_ref, cw_ref, cb_ref, dtb_ref, alog_ref, dsk_ref, gn_ref,
                       tril_ref, e_ref, yg_ref, sfin_ref, cfin_ref, xpad_sc, st_sc):
    c = pl.program_id(1)
    q = SSM_CHUNK
    cd = xbc_ref.shape[1]
    di = z_ref.shape[1]
    gn_w = SSM_GROUPS * SSM_STATE
    hpg = di // SSM_GROUPS
    pad = SUBLANES

    @pl.when(c == 0)
    def _():
        xpad_sc[0:pad, :] = jnp.zeros((pad, cd), F32)
        st_sc[...] = jnp.zeros(st_sc.shape, F32)

    @pl.when(c > 0)
    def _():
        xpad_sc[0:pad, :] = xpad_sc[q:q + pad, :]

    xpad_sc[pad:pad + q, :] = xbc_ref[...]
    acc = cb_ref[...]
    for k in range(SSM_CONV):
        off = pad - (SSM_CONV - 1) + k
        acc = acc + xpad_sc[off:off + q, :] * cw_ref[k:k + 1, :]
    xc = _silu(acc)
    xs = xc[:, :di]
    bm = xc[:, di:di + gn_w]
    cm = xc[:, di + gn_w:]

    dt = _softplus(dt_ref[...] + dtb_ref[...])
    a = -jnp.exp(alog_ref[...])
    act = _mm01_left(tril_ref[...], dt * a)
    act_t = act.T
    act_last = act[q - 1:q, :]
    e01 = e_ref[...]
    xdt = xs * _mm01(dt, e01)
    xd = xdt * _mm01(jnp.exp(act_last - act), e01)
    eax = _mm01(jnp.exp(act), e01)
    cdx = _mm01(jnp.exp(act[q - SUBLANES:q, :]), e01)[SUBLANES - 1:SUBLANES, :]

    row = lax.broadcasted_iota(I32, (q, q), 0)
    col = lax.broadcasted_iota(I32, (q, q), 1)
    causal = row >= col
    lane = lax.broadcasted_iota(I32, (q, LANES), 1)
    lo_half = lane < SSM_HEAD_DIM

    y_parts = []
    heads_per_group = hpg // SSM_HEAD_DIM
    for g in range(SSM_GROUPS):
        cg = cm[:, g * SSM_STATE:(g + 1) * SSM_STATE].astype(BF16)
        bg = bm[:, g * SSM_STATE:(g + 1) * SSM_STATE]
        cb = _dot_nt(cg, bg.astype(BF16))
        st_g = st_sc[:, g * hpg:(g + 1) * hpg]
        y_off = jnp.dot(cg, st_g.astype(BF16), preferred_element_type=F32)
        for pr in range(heads_per_group // 2):
            h0 = g * heads_per_group + 2 * pr
            ms = []
            for h in (h0, h0 + 1):
                seg = act[:, h:h + 1] - act_t[h:h + 1, :]
                lm = jnp.exp(jnp.where(causal, seg, -jnp.inf))
                ms.append((cb * lm).astype(BF16))
            m2 = jnp.concatenate(ms, axis=1)
            xpair = xdt[:, h0 * SSM_HEAD_DIM:(h0 + 2) * SSM_HEAD_DIM]
            rhs = jnp.concatenate([jnp.where(lo_half, xpair, 0.0),
                                   jnp.where(lo_half, 0.0, xpair)], axis=0).astype(BF16)
            y_d = jnp.dot(m2, rhs, preferred_element_type=F32)
            lo = 2 * pr * SSM_HEAD_DIM
            y_parts.append(y_d + y_off[:, lo:lo + LANES] * eax[:, g * hpg + lo:g * hpg + lo + LANES])
        upd = jnp.dot(bg.T.astype(BF16), xd[:, g * hpg:(g + 1) * hpg].astype(BF16),
                      preferred_element_type=F32)
        st_sc[:, g * hpg:(g + 1) * hpg] = st_g * cdx[:, g * hpg:(g + 1) * hpg] + upd

    y = jnp.concatenate(y_parts, axis=1) + xs * dsk_ref[...]
    yg_ref[...] = _gate_norm(y, z_ref[...], gn_ref[...], SSM_GROUPS)

    @pl.when(c == pl.num_programs(1) - 1)
    def _():
        sfin_ref[0] = st_sc[...].T
        cfin_ref[0] = xpad_sc[q:q + pad, :]


def _ssd_prompt(z, xbc, dt, cw, cb, dtb, alog, dsk, gnw, tril, e01, bp, seq):
    nc = seq // SSM_CHUNK
    q = SSM_CHUNK
    di, cd = z.shape[1], xbc.shape[1]
    rows = lambda b, c: (b * nc + c, 0)
    return pl.pallas_call(
        _ssd_prompt_kernel,
        grid=(bp, nc),
        in_specs=[
            pl.BlockSpec((q, di), rows), pl.BlockSpec((q, cd), rows), pl.BlockSpec((q, LANES), rows),
            _const_spec(cw.shape), _const_spec(cb.shape), _const_spec(dtb.shape), _const_spec(alog.shape),
            _const_spec(dsk.shape), _const_spec(gnw.shape), _const_spec(tril.shape), _const_spec(e01.shape),
        ],
        out_specs=[
            pl.BlockSpec((q, di), rows),
            pl.BlockSpec((1, di, SSM_STATE), lambda b, c: (b, 0, 0)),
            pl.BlockSpec((1, SUBLANES, cd), lambda b, c: (b, 0, 0)),
        ],
        out_shape=[jax.ShapeDtypeStruct((bp * seq, di), F32),
                   jax.ShapeDtypeStruct((bp, di, SSM_STATE), F32),
                   jax.ShapeDtypeStruct((bp, SUBLANES, cd), F32)],
        scratch_shapes=[pltpu.VMEM((q + 2 * SUBLANES, cd), F32), pltpu.VMEM((SSM_STATE, di), F32)],
        compiler_params=_cparams(("arbitrary", "arbitrary")),
        name="ssd_prompt",
    )(z, xbc, dt, cw, cb, dtb, alog, dsk, gnw, tril, e01)


def _ssd_sample_kernel(z_ref, xp_ref, dt_ref, s0_ref, cw_ref, cb_ref, dtb_ref, alog_ref, dsk_ref, gn_ref,
                       e_ref, g1_ref, yg_ref, sn_ref, yoff_sc, *, n_tok):
    hf = pl.program_id(1)
    nb = SEQ_PER_STEP
    half = nb // 2
    q = n_tok * nb
    di = z_ref.shape[1]
    gn_w = SSM_GROUPS * SSM_STATE
    hpg = di // SSM_GROUPS

    taps = [xp_ref[0, m] for m in range(n_tok + SSM_CONV - 1)]
    slabs = []
    for t in range(n_tok):
        acc = cb_ref[...]
        for k in range(SSM_CONV):
            acc = acc + taps[t + k] * cw_ref[k:k + 1, :]
        slabs.append(_silu(acc))
    xc = jnp.concatenate(slabs, axis=0)
    xs = xc[:, :di]
    bm = xc[:, di:di + gn_w]
    cm = xc[:, di + gn_w:]

    dt = _softplus(dt_ref[...] + dtb_ref[...])
    da = dt * (-jnp.exp(alog_ref[...]))
    acts = []
    run = None
    for t in range(n_tok):
        d = da[t * nb:(t + 1) * nb, :]
        run = d if run is None else run + d
        acts.append(run)
    act = jnp.concatenate(acts, axis=0)
    act_last = jnp.concatenate([acts[-1]] * n_tok, axis=0)
    e01 = e_ref[...]
    xdt = xs * _mm01(dt, e01)
    xd = xdt * _mm01(jnp.exp(act_last - act), e01)
    eax = _mm01(jnp.exp(act), e01)
    cdx = _mm01(jnp.exp(acts[-1]), e01)

    pairs = [(t, u) for t in range(n_tok) for u in range(t + 1)]
    cbp = jnp.concatenate([cm[t * nb:(t + 1) * nb, :] * bm[u * nb:(u + 1) * nb, :] for t, u in pairs], axis=0)
    seg = jnp.concatenate([acts[t] - acts[u] for t, u in pairs], axis=0)
    coef = _mm01(_mm01(cbp, g1_ref[...]) * jnp.exp(seg), e01)
    y_slabs = []
    for t in range(n_tok):
        acc = None
        for pi, (tt, u) in enumerate(pairs):
            if tt != t:
                continue
            term = coef[pi * nb:(pi + 1) * nb, :] * xdt[u * nb:(u + 1) * nb, :]
            acc = term if acc is None else acc + term
        y_slabs.append(acc)
    y_diag = jnp.concatenate(y_slabs, axis=0)

    zpad = jnp.concatenate([xd,
                            jnp.where(hf == 0, cdx[0:half, :], cdx[half:nb, :]),
                            jnp.zeros((LANES - q - half, di), F32)], axis=0)
    zt = zpad.T
    row_seq = lax.broadcasted_iota(I32, (q, 1), 0) % nb
    cm_b = cm.astype(BF16)
    y_off_g = [None] * SSM_GROUPS
    for sl in range(half):
        in_seq = row_seq == hf * half + sl
        for g in range(SSM_GROUPS):
            s_old = s0_ref[sl, g * hpg:(g + 1) * hpg, :]
            c_g = jnp.where(in_seq, cm_b[:, g * SSM_STATE:(g + 1) * SSM_STATE], jnp.zeros((), BF16))
            yo = _dot_nt(c_g, s_old.astype(BF16))
            y_off_g[g] = yo if y_off_g[g] is None else y_off_g[g] + yo
            b_g = jnp.where(in_seq, bm[:, g * SSM_STATE:(g + 1) * SSM_STATE], 0.0)
            b_pad = jnp.concatenate([b_g, jnp.zeros((LANES - q, SSM_STATE), F32)], axis=0).astype(BF16)
            zt_g = zt[g * hpg:(g + 1) * hpg, :]
            upd = jnp.dot(zt_g.astype(BF16), b_pad, preferred_element_type=F32)
            decay = zt_g[:, q + sl:q + sl + 1]
            sn_ref[sl, g * hpg:(g + 1) * hpg, :] = s_old * decay + upd
    y_off = jnp.concatenate(y_off_g, axis=1)

    @pl.when(hf == 0)
    def _():
        yoff_sc[...] = y_off

    @pl.when(hf == 1)
    def _():
        y = y_diag + (yoff_sc[...] + y_off) * eax + xs * dsk_ref[...]
        yg_ref[...] = _gate_norm(y, z_ref[...], gn_ref[...], SSM_GROUPS)


def _ssd_sample(z, xp7, dt, s0, cw, cb, dtb, alog, dsk, gnw, e01, g1, tp, n_tok):
    nsteps = xp7.shape[0]
    nb = SEQ_PER_STEP
    half = nb // 2
    q = n_tok * nb
    di = z.shape[1]
    cd = xp7.shape[3]
    base = tp // q
    rows = lambda s, hf: (base + s, 0)
    return pl.pallas_call(
        functools.partial(_ssd_sample_kernel, n_tok=n_tok),
        grid=(nsteps, 2),
        in_specs=[
            pl.BlockSpec((q, di), rows),
            pl.BlockSpec((1, n_tok + SSM_CONV - 1, nb, cd), lambda s, hf: (s, 0, 0, 0)),
            pl.BlockSpec((q, LANES), rows),
            pl.BlockSpec((half, di, SSM_STATE), lambda s, hf: (2 * s + hf, 0, 0)),
            _const_spec(cw.shape), _const_spec(cb.shape), _const_spec(dtb.shape), _const_spec(alog.shape),
            _const_spec(dsk.shape), _const_spec(gnw.shape), _const_spec(e01.shape), _const_spec(g1.shape),
        ],
        out_specs=[
            pl.BlockSpec((q, di), lambda s, hf: (s, 0)),
            pl.BlockSpec((half, di, SSM_STATE), lambda s, hf: (2 * s + hf, 0, 0)),
        ],
        out_shape=[jax.ShapeDtypeStruct((nsteps * q, di), F32),
                   jax.ShapeDtypeStruct(s0.shape, F32)],
        scratch_shapes=[pltpu.VMEM((q, di), F32)],
        compiler_params=_cparams(("arbitrary", "arbitrary")),
        name="ssd_sample",
    )(z, xp7, dt, s0, cw, cb, dtb, alog, dsk, gnw, e01, g1)


def _qkv_kernel(h_ref, kvn_ref, an_ref, wkv_ref, wq_ref, kn_ref, qn_ref, rc_ref, rs1_ref, rs2_ref,
                hsum_ref, rq_ref, eq_ref, q_ref, k_ref, v_ref):
    hn = _rms_scale(h_ref[...])
    kvw = k_ref.shape[1]
    kv = jnp.dot((hn * kvn_ref[...]).astype(BF16), wkv_ref[...], preferred_element_type=F32)
    k = kv[:, :kvw]
    v_ref[...] = kv[:, kvw:]
    inv_hd = 1.0 / ATT_HEAD_DIM
    k = k * lax.rsqrt(_mm01(k * k, hsum_ref[...]) * inv_hd + EPS) * kn_ref[...]
    rc, rs1, rs2 = rc_ref[...], rs1_ref[...], rs2_ref[...]
    rk = kvw // LANES
    k_ref[...] = _rope(k, _tile_lanes(rc, rk), _tile_lanes(rs1, rk), _tile_lanes(rs2, rk))
    q = jnp.dot((hn * an_ref[...]).astype(BF16), wq_ref[...], preferred_element_type=F32)
    rsq = lax.rsqrt(_mm01(q * q, rq_ref[...]) * inv_hd + EPS)
    q = q * _mm01(rsq, eq_ref[...]) * qn_ref[...]
    rq = q.shape[1] // LANES
    q_ref[...] = _rope(q, _tile_lanes(rc, rq), _tile_lanes(rs1, rq), _tile_lanes(rs2, rq))


def _qkv(h, kvn, an, wkv, wq, knt, qnt, rc, rs1, rs2, hsum, rq, eq):
    t, d = h.shape
    tm = TOKEN_TILE
    kvw = wkv.shape[1] // 2
    qw = wq.shape[1]
    rows = lambda i: (i, 0)
    return pl.pallas_call(
        _qkv_kernel,
        grid=(t // tm,),
        in_specs=[pl.BlockSpec((tm, d), rows)] + [_const_spec(a.shape) for a in (kvn, an, wkv, wq, knt, qnt)]
        + [pl.BlockSpec((tm, LANES), rows)] * 3 + [_const_spec(a.shape) for a in (hsum, rq, eq)],
        out_specs=[pl.BlockSpec((tm, qw), rows), pl.BlockSpec((tm, kvw), rows), pl.BlockSpec((tm, kvw), rows)],
        out_shape=[jax.ShapeDtypeStruct((t, qw), F32), jax.ShapeDtypeStruct((t, kvw), F32),
                   jax.ShapeDtypeStruct((t, kvw), F32)],
        compiler_params=_cparams(("arbitrary",)),
        name="qkv",
    )(h, kvn, an, wkv, wq, knt, qnt, rc, rs1, rs2, hsum, rq, eq)


def _attn_prompt_kernel(sink_ref, q_ref, kc_ref, kp_ref, vc_ref, vp_ref, o_ref):
    i = pl.program_id(1)
    w = WINDOW
    hd = ATT_HEAD_DIM
    n_q = q_ref.shape[1] // hd
    grp = n_q // ATT_KV_HEADS
    row = lax.broadcasted_iota(I32, (w, 2 * w), 0)
    col = lax.broadcasted_iota(I32, (w, 2 * w), 1)
    dist = row + w - col
    mask = (dist >= 0) & (dist < w) & ((col >= w) | (i > 0))
    q = q_ref[...]
    outs = []
    for g in range(ATT_KV_HEADS):
        sl = slice(g * hd, (g + 1) * hd)
        kk = jnp.concatenate([kp_ref[:, sl], kc_ref[:, sl]], axis=0).astype(BF16)
        vv = jnp.concatenate([vp_ref[:, sl], vc_ref[:, sl]], axis=0).astype(BF16)
        for hq in range(grp):
            h = g * grp + hq
            s = _dot_nt(q[:, h * hd:(h + 1) * hd].astype(BF16), kk)
            s = jnp.where(mask, s * (hd ** -0.5), -jnp.inf)
            sink = sink_ref[h]
            m = jnp.maximum(jnp.max(s, axis=-1, keepdims=True), sink)
            p = jnp.exp(s - m)
            denom = jnp.sum(p, axis=-1, keepdims=True) + jnp.exp(sink - m)
            outs.append(jnp.dot((p / denom).astype(BF16), vv, preferred_element_type=F32))
    o_ref[...] = jnp.concatenate(outs, axis=1)


def _attn_prompt(sinks, q, k, v, bp, seq):
    w = WINDOW
    nb = seq // w
    qw, kvw = q.shape[1], k.shape[1]
    cur = lambda b, i: (b * nb + i, 0)
    prev = lambda b, i: (b * nb + jnp.maximum(i - 1, 0), 0)
    return pl.pallas_call(
        _attn_prompt_kernel,
        grid=(bp, nb),
        in_specs=[pl.BlockSpec(memory_space=pltpu.SMEM),
                  pl.BlockSpec((w, qw), cur), pl.BlockSpec((w, kvw), cur), pl.BlockSpec((w, kvw), prev),
                  pl.BlockSpec((w, kvw), cur), pl.BlockSpec((w, kvw), prev)],
        out_specs=pl.BlockSpec((w, qw), cur),
        out_shape=jax.ShapeDtypeStruct((bp * seq, qw), F32),
        compiler_params=_cparams(("arbitrary", "arbitrary")),
        name="attn_prompt",
    )(sinks, q, k, k, v, v)


def _attn_sample_kernel(sink_ref, q_ref, kn_ref, vn_ref, kc_ref, vc_ref, o_ref, *, n_tok):
    nb = SEQ_PER_STEP
    qn = n_tok * nb
    hd = ATT_HEAD_DIM
    lc = kc_ref.shape[1]
    n_q = q_ref.shape[1] // hd
    grp = n_q // ATT_KV_HEADS
    rows = grp * qn
    r = lax.broadcasted_iota(I32, (rows, 1), 0)
    r_seq = r % nb
    r_tok = (r % qn) // nb
    ccol = lax.broadcasted_iota(I32, (rows, lc), 1)
    mask_c = ccol >= r_tok + 1 + (lc - WINDOW)
    ncol = lax.broadcasted_iota(I32, (rows, LANES), 1)
    mask_n = (ncol < qn) & (ncol % nb == r_seq) & (ncol // nb <= r_tok)
    q = q_ref[...]
    zpad = jnp.zeros((LANES - qn, hd), F32)
    outs = [None] * n_q
    for g in range(ATT_KV_HEADS):
        sl = slice(g * hd, (g + 1) * hd)
        qs = jnp.concatenate([q[:, (g * grp + hq) * hd:(g * grp + hq + 1) * hd] for hq in range(grp)],
                             axis=0).astype(BF16)
        k_new = jnp.concatenate([kn_ref[:, sl], zpad], axis=0).astype(BF16)
        v_new = jnp.concatenate([vn_ref[:, sl], zpad], axis=0).astype(BF16)
        s_n = jnp.where(mask_n, _dot_nt(qs, k_new) * (hd ** -0.5), -jnp.inf)
        s_c = jnp.zeros((rows, lc), F32)
        for sq in range(nb):
            s_sq = _dot_nt(qs, kc_ref[sq, :, sl].astype(BF16))
            s_c = jnp.where(r_seq == sq, s_sq, s_c)
        s_c = jnp.where(mask_c, s_c * (hd ** -0.5), -jnp.inf)
        sink = jnp.zeros((rows, 1), F32)
        for hq in range(grp):
            sink = jnp.where(r // qn == hq, sink_ref[g * grp + hq], sink)
        m = jnp.maximum(jnp.maximum(jnp.max(s_c, axis=-1, keepdims=True),
                                    jnp.max(s_n, axis=-1, keepdims=True)), sink)
        p_c = jnp.exp(s_c - m)
        p_n = jnp.exp(s_n - m)
        denom = (jnp.sum(p_c, axis=-1, keepdims=True) + jnp.sum(p_n, axis=-1, keepdims=True)
                 + jnp.exp(sink - m))
        p_c = p_c / denom
        o = jnp.dot((p_n / denom).astype(BF16), v_new, preferred_element_type=F32)
        for sq in range(nb):
            o = o + jnp.dot(jnp.where(r_seq == sq, p_c, 0.0).astype(BF16), vc_ref[sq, :, sl].astype(BF16),
                            preferred_element_type=F32)
        for hq in range(grp):
            outs[g * grp + hq] = o[hq * qn:(hq + 1) * qn, :]
    o_ref[...] = jnp.concatenate(outs, axis=1)


def _attn_sample(sinks, q, k, v, kc, vc, tp, n_tok):
    nb = SEQ_PER_STEP
    qn = n_tok * nb
    nsteps = kc.shape[0] // nb
    lc, kvw = kc.shape[1], kc.shape[2]
    qw = q.shape[1]
    base = tp // qn
    rows = lambda s: (base + s, 0)
    return pl.pallas_call(
        functools.partial(_attn_sample_kernel, n_tok=n_tok),
        grid=(nsteps,),
        in_specs=[pl.BlockSpec(memory_space=pltpu.SMEM),
                  pl.BlockSpec((qn, qw), rows), pl.BlockSpec((qn, kvw), rows), pl.BlockSpec((qn, kvw), rows),
                  pl.BlockSpec((nb, lc, kvw), lambda s: (s, 0, 0)), pl.BlockSpec((nb, lc, kvw), lambda s: (s, 0, 0))],
        out_specs=pl.BlockSpec((qn, qw), lambda s: (s, 0)),
        out_shape=jax.ShapeDtypeStruct((nsteps * qn, qw), F32),
        compiler_params=_cparams(("arbitrary",)),
        name="attn_sample",
    )(sinks, q, k, v, kc, vc)


def _router_kernel(h_ref, fn_ref, wh_ref, wl_ref, b_ref, tril_ref, xn_ref, info_ref, info_t_ref, cnt_ref, carry_sc):
    i = pl.program_id(0)

    @pl.when(i == 0)
    def _():
        carry_sc[...] = jnp.zeros(carry_sc.shape, F32)

    xn = _rms_scale(h_ref[...]) * fn_ref[...]
    _store_token_major(xn_ref, xn)
    x_hi, x_lo = _split_bf16(xn, 2)
    wh, wl = wh_ref[...], wl_ref[...]
    logits = (jnp.dot(x_hi, wh, preferred_element_type=F32) + jnp.dot(x_hi, wl, preferred_element_type=F32)
              + jnp.dot(x_lo, wh, preferred_element_type=F32)) + b_ref[...]
    tm = logits.shape[0]
    lane = lax.broadcasted_iota(I32, (tm, LANES), 1).astype(F32)
    big = float(LANES)
    neg = -jnp.inf

    is_grp = (lane >= MOE_EXPERTS) & (lane < MOE_EXPERTS + MOE_GROUPS)
    lg = jnp.where(is_grp, logits, neg)
    mg = jnp.max(lg, axis=-1, keepdims=True)
    gp = 1.0 / jnp.sum(jnp.exp(lg - mg), axis=-1, keepdims=True)
    gi = jnp.min(jnp.where(lg == mg, lane, big), axis=-1, keepdims=True) - MOE_EXPERTS

    lo = gi * MOE_EXPERTS_PER_GROUP
    le = jnp.where((lane >= lo) & (lane < lo + MOE_EXPERTS_PER_GROUP), logits, neg)
    m1 = jnp.max(le, axis=-1, keepdims=True)
    i1 = jnp.min(jnp.where(le == m1, lane, big), axis=-1, keepdims=True)
    le2 = jnp.where(lane == i1, neg, le)
    m2 = jnp.max(le2, axis=-1, keepdims=True)
    i2 = jnp.min(jnp.where(le2 == m2, lane, big), axis=-1, keepdims=True)
    e2 = jnp.exp(m2 - m1)
    g1 = gp * (1.0 / (1.0 + e2))
    g2 = gp * (e2 / (1.0 + e2))

    a1 = lane == i1
    a2 = lane == i2
    onehot = jnp.where(a1 | a2, 1.0, 0.0)
    before = jnp.dot(tril_ref[...], onehot.astype(BF16), preferred_element_type=F32) + carry_sc[...]
    r1 = jnp.sum(jnp.where(a1, before, 0.0), axis=-1, keepdims=True)
    r2 = jnp.sum(jnp.where(a2, before, 0.0), axis=-1, keepdims=True)
    carry_sc[...] = carry_sc[...] + jnp.sum(onehot, axis=0, keepdims=True)
    cnt_ref[...] = carry_sc[...]

    cols = (i1, i2, g1, g2, r1, r2)
    info = jnp.zeros((tm, LANES), F32)
    for k, cval in enumerate(cols):
        info = jnp.where(lane == k, cval, info)
    info_ref[...] = info
    info_t_ref[...] = info.T[0:SUBLANES, :]


def _router(h, fn, wh, wl, b, tril):
    t, d = h.shape
    tm = TOKEN_TILE
    rows = lambda i: (i, 0)
    return pl.pallas_call(
        _router_kernel,
        grid=(t // tm,),
        in_specs=[pl.BlockSpec((tm, d), rows)] + [_const_spec(a.shape) for a in (fn, wh, wl, b, tril)],
        out_specs=[pl.BlockSpec((tm * ROW_TILES, LANES), rows), pl.BlockSpec((tm, LANES), rows),
                   pl.BlockSpec((SUBLANES, tm), lambda i: (0, i)), _const_spec((1, LANES))],
        out_shape=[jax.ShapeDtypeStruct((t * ROW_TILES, LANES), F32), jax.ShapeDtypeStruct((t, LANES), F32),
                   jax.ShapeDtypeStruct((SUBLANES, t), F32), jax.ShapeDtypeStruct((1, LANES), F32)],
        scratch_shapes=[pltpu.VMEM((1, LANES), F32)],
        compiler_params=_cparams(("arbitrary",)),
        name="moe_router",
    )(h, fn, wh, wl, b, tril)


def _store_token_major(ref, x):
    n = x.shape[0]
    for j in range(ROW_TILES):
        ref[pl.ds(j, n, stride=ROW_TILES), :] = x[:, j * LANES:(j + 1) * LANES]


def _load_token_major(ref, n):
    return jnp.concatenate([ref[pl.ds(j, n, stride=ROW_TILES), :] for j in range(ROW_TILES)], axis=1)


def _dest_kernel(pst_ref, info_ref, dest_ref):
    info = info_ref[...]
    e = info[0:2, :]
    start = jnp.zeros(e.shape, F32)
    for k in range(MOE_EXPERTS):
        start = jnp.where(e == k, pst_ref[k].astype(F32), start)
    dest = (start + info[4:6, :]).astype(I32)
    dest_ref[...] = jnp.concatenate([dest, jnp.zeros((SUBLANES - 2, dest.shape[1]), I32)], axis=0)


def _dest(pstarts, info_t):
    return pl.pallas_call(
        _dest_kernel,
        in_specs=[pl.BlockSpec(memory_space=pltpu.SMEM), pl.BlockSpec(memory_space=pltpu.VMEM)],
        out_specs=pl.BlockSpec(memory_space=pltpu.VMEM),
        out_shape=jax.ShapeDtypeStruct(info_t.shape, I32),
        name="moe_dest",
    )(pstarts, info_t)


def _tile_copy(src, src_row, dst, dst_row, sem):
    return pltpu.make_async_copy(src.at[pl.ds(pl.multiple_of(src_row * ROW_TILES, ROW_TILES), ROW_TILES)],
                                 dst.at[pl.ds(pl.multiple_of(dst_row * ROW_TILES, ROW_TILES), ROW_TILES)], sem)


def _wait_tiles(ref, n_tokens, sem):
    blk = ref.at[pl.ds(0, n_tokens * ROW_TILES)]
    pltpu.make_async_copy(blk, blk, sem).wait()


def _dispatch_kernel(d1_ref, d2_ref, pend_ref, pcnt_ref, xn_hbm, xbuf_hbm, zero_sc, sem_z, sem, *, n_tok, chunk):
    zero_sc[...] = jnp.zeros(zero_sc.shape, F32)
    blk_rows = zero_sc.shape[0]

    def zero_copy(e):
        start = pl.multiple_of(pend_ref[e] * ROW_TILES - blk_rows, ROW_TILES)
        return pltpu.make_async_copy(zero_sc, xbuf_hbm.at[pl.ds(start, blk_rows)], sem_z)

    for e in range(MOE_EXPERTS):
        @pl.when(pcnt_ref[e] > 0)
        def _():
            zero_copy(e).start()
    first_unused = pend_ref[MOE_EXPERTS - 1] * ROW_TILES // blk_rows
    n_blocks = xbuf_hbm.shape[0] // blk_rows

    def tail_copy(b):
        return pltpu.make_async_copy(zero_sc, xbuf_hbm.at[pl.ds(pl.multiple_of(b * blk_rows, blk_rows), blk_rows)], sem_z)

    def tail_start(b, carry):
        tail_copy(b).start()
        return carry

    def tail_wait(b, carry):
        tail_copy(b).wait()
        return carry
    lax.fori_loop(first_unused, n_blocks, tail_start, 0)
    for e in range(MOE_EXPERTS):
        @pl.when(pcnt_ref[e] > 0)
        def _():
            zero_copy(e).wait()
    lax.fori_loop(first_unused, n_blocks, tail_wait, 0)

    n_chunks = n_tok // chunk

    def chunk_body(c, carry):
        def body(r, carry2):
            t = c * chunk + r
            _tile_copy(xn_hbm, t, xbuf_hbm, d1_ref[t], sem).start()
            _tile_copy(xn_hbm, t, xbuf_hbm, d2_ref[t], sem).start()
            return carry2
        lax.fori_loop(0, chunk, body, 0, unroll=8)

        @pl.when(c > 0)
        def _():
            _wait_tiles(xn_hbm, 2 * chunk, sem)
        return carry
    lax.fori_loop(0, n_chunks, chunk_body, 0)
    _wait_tiles(xn_hbm, 2 * chunk, sem)


def _dispatch(dest1, dest2, pends, pcounts, xn_tm, n_rows):
    n_tok = dest1.shape[0]
    chunk = TOKEN_TILE
    grid_spec = pltpu.PrefetchScalarGridSpec(
        num_scalar_prefetch=4,
        grid=(1,),
        in_specs=[pl.BlockSpec(memory_space=pl.ANY)],
        out_specs=pl.BlockSpec(memory_space=pl.ANY),
        scratch_shapes=[pltpu.VMEM((EXPERT_ROWS * ROW_TILES, LANES), F32), pltpu.SemaphoreType.DMA(()),
                        pltpu.SemaphoreType.DMA(())],
    )
    return pl.pallas_call(
        functools.partial(_dispatch_kernel, n_tok=n_tok, chunk=chunk),
        grid_spec=grid_spec,
        out_shape=jax.ShapeDtypeStruct((n_rows * ROW_TILES, LANES), F32),
        compiler_params=_cparams(("arbitrary",)),
        name="moe_dispatch",
    )(dest1, dest2, pends, pcounts, xn_tm)


def _expert_kernel(blk_e_ref, nvalid_ref, x_ref, wg_ref, wu_ref, wd_ref, y_ref, wg_sc, wu_sc, wd_sc):
    i = pl.program_id(0)
    nv = nvalid_ref[0]
    rows = x_ref.shape[0] // ROW_TILES

    @pl.when(i < nv)
    def _():
        e = blk_e_ref[i]
        e_prev = blk_e_ref[jnp.maximum(i - 1, 0)]

        @pl.when((i == 0) | (e != e_prev))
        def _():
            wg_sc[...] = wg_ref[0, 0].astype(BF16)
            wu_sc[...] = wu_ref[0, 0].astype(BF16)
            wd_sc[...] = wd_ref[0, 0].astype(BF16)

        x = _load_token_major(x_ref, rows).astype(BF16)
        hid = _silu(jnp.dot(x, wg_sc[...], preferred_element_type=F32)) * jnp.dot(
            x, wu_sc[...], preferred_element_type=F32)
        _store_token_major(y_ref, jnp.dot(hid.astype(BF16), wd_sc[...], preferred_element_type=F32))

    @pl.when(i >= nv)
    def _():
        y_ref[...] = jnp.zeros(y_ref.shape, F32)


def _experts(blk_e, nvalid, xbuf, wg, wu, wd, layer):
    nblk = blk_e.shape[0]
    d, hdim = wg.shape[2], wg.shape[3]
    rows = EXPERT_ROWS * ROW_TILES
    wsel = lambda i, be, nv: (layer, be[i], 0, 0)
    grid_spec = pltpu.PrefetchScalarGridSpec(
        num_scalar_prefetch=2,
        grid=(nblk,),
        in_specs=[pl.BlockSpec((rows, LANES), lambda i, be, nv: (jnp.minimum(i, nv[0] - 1), 0)),
                  pl.BlockSpec((1, 1, d, hdim), wsel), pl.BlockSpec((1, 1, d, hdim), wsel),
                  pl.BlockSpec((1, 1, hdim, d), wsel)],
        out_specs=pl.BlockSpec((rows, LANES), lambda i, be, nv: (i, 0)),
        scratch_shapes=[pltpu.VMEM((d, hdim), BF16), pltpu.VMEM((d, hdim), BF16), pltpu.VMEM((hdim, d), BF16)],
    )
    return pl.pallas_call(
        _expert_kernel,
        grid_spec=grid_spec,
        out_shape=jax.ShapeDtypeStruct((nblk * rows, LANES), F32),
        compiler_params=_cparams(("arbitrary",)),
        name="moe_experts",
    )(blk_e, nvalid, xbuf, wg, wu, wd)


def _combine_kernel(d1_ref, d2_ref, h_ref, info_ref, y_hbm, o_ref, r_sc, sem, *, base_tile):
    i = pl.program_id(0)
    n = pl.num_programs(0)
    tm = h_ref.shape[0]

    def start(step):
        slot = step % 2
        base = (base_tile + step) * tm

        def body(r, carry):
            _tile_copy(y_hbm, d1_ref[base + r], r_sc.at[slot, 0], r, sem.at[slot]).start()
            _tile_copy(y_hbm, d2_ref[base + r], r_sc.at[slot, 1], r, sem.at[slot]).start()
            return carry
        lax.fori_loop(0, tm, body, 0, unroll=8)

    @pl.when(i == 0)
    def _():
        start(i)

    @pl.when(i + 1 < n)
    def _():
        start(i + 1)

    slot = i % 2
    _wait_tiles(r_sc.at[slot, 0], tm, sem.at[slot])
    _wait_tiles(r_sc.at[slot, 1], tm, sem.at[slot])
    info = info_ref[...]
    y1 = _load_token_major(r_sc.at[slot, 0], tm)
    y2 = _load_token_major(r_sc.at[slot, 1], tm)
    o_ref[...] = h_ref[...] + (y1 * info[:, 2:3] + y2 * info[:, 3:4])


def _combine(dest1, dest2, h, info, ybuf, row0, nrows):
    d = h.shape[1]
    tm = TOKEN_TILE
    base_tile = row0 // tm
    rows = lambda i, a, b: (base_tile + i, 0)
    grid_spec = pltpu.PrefetchScalarGridSpec(
        num_scalar_prefetch=2,
        grid=(nrows // tm,),
        in_specs=[pl.BlockSpec((tm, d), rows), pl.BlockSpec((tm, LANES), rows), pl.BlockSpec(memory_space=pl.ANY)],
        out_specs=pl.BlockSpec((tm, d), lambda i, a, b: (i, 0)),
        scratch_shapes=[pltpu.VMEM((2, 2, tm * ROW_TILES, LANES), F32), pltpu.SemaphoreType.DMA((2,))],
    )
    return pl.pallas_call(
        functools.partial(_combine_kernel, base_tile=base_tile),
        grid_spec=grid_spec,
        out_shape=jax.ShapeDtypeStruct((nrows, d), F32),
        compiler_params=_cparams(("arbitrary",)),
        name="moe_combine",
    )(dest1, dest2, h, info, ybuf)


def _moe(h, fn, w_grp, b_grp, w_rt, b_rt, wg, wu, wd, layer, tril, out_ranges):
    t, d = h.shape
    w_cat = jnp.zeros((d, LANES), F32).at[:, :MOE_EXPERTS].set(w_rt).at[:, MOE_EXPERTS:MOE_EXPERTS + MOE_GROUPS].set(w_grp)
    b_cat = jnp.zeros((1, LANES), F32).at[0, :MOE_EXPERTS].set(b_rt).at[0, MOE_EXPERTS:MOE_EXPERTS + MOE_GROUPS].set(b_grp)
    w_hi = w_cat.astype(BF16)
    w_lo = (w_cat - w_hi.astype(F32)).astype(BF16)
    xn_tm, info, info_t, cnt = _router(h, fn.reshape(1, d), w_hi, w_lo, b_cat, tril)

    counts = cnt[0, :MOE_EXPERTS].astype(I32)
    pcounts = (counts + EXPERT_ROWS - 1) // EXPERT_ROWS * EXPERT_ROWS
    pends = jnp.cumsum(pcounts)
    pstarts = pends - pcounts
    nblk = -(-(2 * t + MOE_EXPERTS * (EXPERT_ROWS - 1)) // EXPERT_ROWS)
    blk_start = jnp.arange(nblk, dtype=I32) * EXPERT_ROWS
    blk_e = jnp.minimum(jnp.sum((pends[None, :] <= blk_start[:, None]).astype(I32), axis=1), MOE_EXPERTS - 1)
    nvalid = pends[-1:] // EXPERT_ROWS

    dest = _dest(pstarts, info_t)
    dest1, dest2 = dest[0], dest[1]
    xbuf = _dispatch(dest1, dest2, pends, pcounts, xn_tm, nblk * EXPERT_ROWS)
    ybuf = _experts(blk_e, nvalid, xbuf, wg, wu, wd, layer)
    return [_combine(dest1, dest2, h, info, ybuf, r0, nr) for r0, nr in out_ranges]


def _rope_tables(pos):
    half = ROT_DIM // 2
    inv = ROPE_THETA ** (-jnp.arange(0, ROT_DIM, 2, dtype=F32) / ROT_DIM)
    ang = pos.astype(F32)[:, None] * inv[None, :]
    cos, sin = jnp.cos(ang), jnp.sin(ang)
    n = pos.shape[0]
    ones = jnp.ones((n, ATT_HEAD_DIM - ROT_DIM), F32)
    zeros_r = jnp.zeros((n, ATT_HEAD_DIM - ROT_DIM), F32)
    zeros_h = jnp.zeros((n, half), F32)
    c = jnp.concatenate([cos, cos, ones], axis=1)
    s1 = jnp.concatenate([-sin, zeros_h, zeros_r], axis=1)
    s2 = jnp.concatenate([zeros_h, sin, zeros_r], axis=1)
    reps = LANES // ATT_HEAD_DIM
    return tuple(jnp.tile(a, (1, reps)) for a in (c, s1, s2))


def _to_step_order(a, nsteps, n_tok):
    c = a.shape[-1]
    return a.reshape(nsteps, SEQ_PER_STEP, n_tok, c).transpose(0, 2, 1, 3).reshape(nsteps * n_tok * SEQ_PER_STEP, c)


def _from_step_order(a, nsteps, n_tok):
    c = a.shape[-1]
    return a.reshape(nsteps, n_tok, SEQ_PER_STEP, c).transpose(0, 2, 1, 3).reshape(nsteps * SEQ_PER_STEP, n_tok, c)


def kernel(x_prompt, x_sample, state_ssm, state_conv, cache_k_win, cache_v_win, ssm_norm, ssm_w_in, ssm_conv_w, ssm_conv_b, ssm_dt_bias, ssm_a_log, ssm_d, ssm_gate_norm, ssm_w_out, kv_norm, w_kv, k_norm, attn_norm, w_q, q_norm, sinks, w_o, ffn_norm, moe_w_group, moe_b_group, moe_w_router, moe_b_router, moe_w_gate, moe_w_up, moe_w_down):
    bp, seq, d = x_prompt.shape
    bs, n_tok, _ = x_sample.shape
    tp, ts = bp * seq, bs * n_tok
    nsteps = bs // SEQ_PER_STEP
    n_heads = ssm_d.shape[1]
    di = n_heads * SSM_HEAD_DIM
    gn_w = SSM_GROUPS * SSM_STATE
    cdim = di + 2 * gn_w
    n_q = sinks.shape[1]
    kvw = ATT_KV_HEADS * ATT_HEAD_DIM

    xp2 = x_prompt.reshape(tp, d)
    xs2 = _to_step_order(x_sample, nsteps, n_tok)

    lane_i = jnp.arange(LANES)
    e01 = (lane_i[:, None] == (jnp.arange(di) // SSM_HEAD_DIM)[None, :]).astype(BF16)
    hpg = di // SSM_GROUPS
    g1 = ((jnp.arange(gn_w) // SSM_STATE)[:, None] == (lane_i // (hpg // SSM_HEAD_DIM))[None, :])
    g1 = (g1 & (lane_i < n_heads)[None, :]).astype(BF16)
    tril_c = (jnp.arange(SSM_CHUNK)[:, None] >= jnp.arange(SSM_CHUNK)[None, :]).astype(BF16)
    tril_x = (jnp.arange(TOKEN_TILE)[:, None] > jnp.arange(TOKEN_TILE)[None, :]).astype(BF16)
    hsum = ((jnp.arange(kvw) // ATT_HEAD_DIM)[:, None] == (jnp.arange(kvw) // ATT_HEAD_DIM)[None, :]).astype(BF16)
    qw = n_q * ATT_HEAD_DIM
    rq = ((jnp.arange(qw) // ATT_HEAD_DIM)[:, None] == lane_i[None, :]).astype(BF16)
    eq = rq.T

    w_in = ssm_w_in[0]
    wz = w_in[:, :di].astype(BF16)
    wx = w_in[:, di:di + cdim].astype(BF16)
    wd = jnp.zeros((d, LANES), F32).at[:, :n_heads].set(w_in[:, di + cdim:]).astype(BF16)
    z, xbc, dt = _inproj(xp2, xs2, ssm_norm[0].reshape(1, d), wz, wx, wd)

    pad_h = lambda v: jnp.zeros((1, LANES), F32).at[0, :n_heads].set(v)
    cw, cb = ssm_conv_w[0], ssm_conv_b[0].reshape(1, cdim)
    dtb, alog = pad_h(ssm_dt_bias[0]), pad_h(ssm_a_log[0])
    dsk = jnp.repeat(ssm_d[0], SSM_HEAD_DIM).reshape(1, di)
    gnw = ssm_gate_norm[0].reshape(1, di)

    yg_p, s_fin, c_fin = _ssd_prompt(z, xbc, dt, cw, cb, dtb, alog, dsk, gnw, tril_c, e01, bp, seq)
    ssm_p = s_fin.reshape(1, bp, n_heads, SSM_HEAD_DIM, SSM_STATE)
    conv_p = c_fin[:, SUBLANES - (SSM_CONV - 1):, :].reshape(1, bp, SSM_CONV - 1, cdim)

    xbc_s = xbc[tp:].reshape(nsteps, n_tok, SEQ_PER_STEP, cdim)
    conv_in = state_conv[0].reshape(nsteps, SEQ_PER_STEP, SSM_CONV - 1, cdim).transpose(0, 2, 1, 3)
    xp7 = jnp.concatenate([conv_in, xbc_s], axis=1)
    s0 = state_ssm[0].reshape(bs, di, SSM_STATE)
    yg_s, s_new = _ssd_sample(z, xp7, dt, s0, cw, cb, dtb, alog, dsk, gnw, e01, g1, tp, n_tok)
    ssm_s = s_new.reshape(1, bs, n_heads, SSM_HEAD_DIM, SSM_STATE)
    conv_s = _from_step_order(xbc[tp:], nsteps, n_tok)[:, n_tok - (SSM_CONV - 1):, :].reshape(
        1, bs, SSM_CONV - 1, cdim)

    h = _proj_res([yg_p, yg_s], ssm_w_out[0].astype(BF16), [xp2, xs2], tp, ts)
    (h,) = _moe(h, ffn_norm[0], moe_w_group[0], moe_b_group[0], moe_w_router[0], moe_b_router[0],
                moe_w_gate, moe_w_up, moe_w_down, 0, tril_x, [(0, tp + ts)])

    pos = jnp.concatenate([jnp.tile(jnp.arange(seq, dtype=I32), bp),
                           jnp.tile(jnp.repeat(PAST_LEN + jnp.arange(n_tok, dtype=I32), SEQ_PER_STEP), nsteps)])
    rc, rs1, rs2 = _rope_tables(pos)
    q, k, v = _qkv(h, kv_norm.reshape(1, d), attn_norm[0].reshape(1, d), w_kv.astype(BF16), w_q[0].astype(BF16),
                   jnp.tile(k_norm, ATT_KV_HEADS).reshape(1, kvw), jnp.tile(q_norm[0], n_q).reshape(1, qw),
                   rc, rs1, rs2, hsum, rq, eq)
    sk = sinks[0]
    o_p = _attn_prompt(sk, q, k, v, bp, seq)
    lc = cache_k_win.shape[1]
    kc = cache_k_win.reshape(bs, lc, kvw)
    vc = cache_v_win.reshape(bs, lc, kvw)
    o_s = _attn_sample(sk, q, k, v, kc, vc, tp, n_tok)
    h = _proj_res([o_p, o_s], w_o[0].astype(BF16), [h], tp, ts)
    y_p, y_s = _moe(h, ffn_norm[1], moe_w_group[1], moe_b_group[1], moe_w_router[1], moe_b_router[1],
                    moe_w_gate, moe_w_up, moe_w_down, 1, tril_x, [(0, tp), (tp, ts)])

    wl = min(WINDOW, seq)
    k_p = k[:tp].reshape(bp, seq, ATT_KV_HEADS, ATT_HEAD_DIM)[:, seq - wl:]
    v_p = v[:tp].reshape(bp, seq, ATT_KV_HEADS, ATT_HEAD_DIM)[:, seq - wl:]
    k_new = _from_step_order(k[tp:], nsteps, n_tok)
    v_new = _from_step_order(v[tp:], nsteps, n_tok)
    k_s = jnp.concatenate([kc, k_new], axis=1)[:, n_tok:].reshape(bs, lc, ATT_KV_HEADS, ATT_HEAD_DIM)
    v_s = jnp.concatenate([vc, v_new], axis=1)[:, n_tok:].reshape(bs, lc, ATT_KV_HEADS, ATT_HEAD_DIM)
    return (y_p.reshape(bp, seq, d), _from_step_order(y_s, nsteps, n_tok),
            ssm_p, conv_p, k_p, v_p, ssm_s, conv_s, k_s, v_s)
```

```python
import functools

import jax
import jax.numpy as jnp
from jax import lax
from jax.experimental import pallas as pl
from jax.experimental.pallas import tpu as pltpu

F32 = jnp.float32
BF16 = jnp.bfloat16
I32 = jnp.int32

EPS = 1e-6
SSM_HEAD_DIM = 64
SSM_GROUPS = 4
SSM_STATE = 128
SSM_CONV = 4
SSM_CHUNK = 128
ATT_HEAD_DIM = 64
ATT_KV_HEADS = 4
WINDOW = 128
ROT_DIM = ATT_HEAD_DIM // 4
ROPE_THETA = 500000.0
MOE_GROUPS = 4
MOE_EXPERTS_PER_GROUP = 8
MOE_EXPERTS = MOE_GROUPS * MOE_EXPERTS_PER_GROUP
MOE_BLOCK = 128
PAST_LEN = 16384

LANES = 128
SUBLANES = 8
SEQ_PER_STEP = SUBLANES
TOKEN_TILE = 256
ROW_TILES = 8
DISPATCH_TILE = 512
EXPERT_ROWS = 256
VMEM_LIMIT = 56 * 1024 * 1024


def _cparams(sem):
    return pltpu.CompilerParams(dimension_semantics=sem, vmem_limit_bytes=VMEM_LIMIT)


def _const_spec(shape):
    nd = len(shape)
    return pl.BlockSpec(shape, lambda *_: (0,) * nd)


def _split_bf16(v, n):
    parts = []
    r = v
    for k in range(n):
        p = r.astype(BF16)
        parts.append(p)
        if k + 1 < n:
            r = r - p.astype(F32)
    return parts


def _mm01(v, m01, n=3):
    acc = None
    for p in _split_bf16(v, n):
        d = jnp.dot(p, m01, preferred_element_type=F32)
        acc = d if acc is None else acc + d
    return acc


def _mm01_left(m01, v, n=3):
    acc = None
    for p in _split_bf16(v, n):
        d = jnp.dot(m01, p, preferred_element_type=F32)
        acc = d if acc is None else acc + d
    return acc


def _dot_nt(a, b):
    return lax.dot_general(a, b, (((1,), (1,)), ((), ())), preferred_element_type=F32)


def _sigmoid(x):
    return 0.5 * jnp.tanh(0.5 * x) + 0.5


def _silu(x):
    return x * _sigmoid(x)


def _softplus(x):
    return jnp.maximum(x, 0.0) + jnp.log1p(jnp.exp(-jnp.abs(x)))


def _rms_scale(x):
    return x * lax.rsqrt(jnp.mean(x * x, axis=-1, keepdims=True) + EPS)


def _gate_norm(y, z, gn, n_groups):
    yz = y * _silu(z)
    w = yz.shape[-1] // n_groups
    outs = []
    for g in range(n_groups):
        v = yz[:, g * w:(g + 1) * w]
        outs.append(_rms_scale(v) * gn[:, g * w:(g + 1) * w])
    return jnp.concatenate(outs, axis=1)


def _rope(x, c, s1, s2):
    w = x.shape[-1]
    return x * c + pltpu.roll(x, w - ROT_DIM // 2, 1) * s1 + pltpu.roll(x, ROT_DIM // 2, 1) * s2


def _tile_lanes(t, reps):
    return t if reps == 1 else jnp.concatenate([t] * reps, axis=1)


def _inproj_kernel(xp_ref, xs_ref, g_ref, wz_ref, wx_ref, wd_ref, z_ref, xbc_ref, dt_ref, *, n_p):
    i = pl.program_id(0)
    x = jnp.where(i < n_p, xp_ref[...], xs_ref[...])
    xn = (_rms_scale(x) * g_ref[...]).astype(BF16)
    z_ref[...] = jnp.dot(xn, wz_ref[...], preferred_element_type=F32)
    xbc_ref[...] = jnp.dot(xn, wx_ref[...], preferred_element_type=F32)
    dt_ref[...] = jnp.dot(xn, wd_ref[...], preferred_element_type=F32)


def _inproj(xp2, xs2, g, wz, wx, wd):
    tp, d = xp2.shape
    ts = xs2.shape[0]
    tm = TOKEN_TILE
    n_p, n_s = tp // tm, ts // tm
    t = tp + ts
    return pl.pallas_call(
        functools.partial(_inproj_kernel, n_p=n_p),
        grid=(n_p + n_s,),
        in_specs=[
            pl.BlockSpec((tm, d), lambda i: (jnp.minimum(i, n_p - 1), 0)),
            pl.BlockSpec((tm, d), lambda i: (jnp.maximum(i - n_p, 0), 0)),
            _const_spec(g.shape), _const_spec(wz.shape), _const_spec(wx.shape), _const_spec(wd.shape),
        ],
        out_specs=[
            pl.BlockSpec((tm, wz.shape[1]), lambda i: (i, 0)),
            pl.BlockSpec((tm, wx.shape[1]), lambda i: (i, 0)),
            pl.BlockSpec((tm, wd.shape[1]), lambda i: (i, 0)),
        ],
        out_shape=[jax.ShapeDtypeStruct((t, wz.shape[1]), F32),
                   jax.ShapeDtypeStruct((t, wx.shape[1]), F32),
                   jax.ShapeDtypeStruct((t, wd.shape[1]), F32)],
        compiler_params=_cparams(("arbitrary",)),
        name="inproj",
    )(xp2, xs2, g, wz, wx, wd)


def _proj_res_kernel(*refs, n_p, two_a, two_r):
    refs = list(refs)
    o_ref = refs.pop()
    i = pl.program_id(0)
    a_p = refs.pop(0)
    a = jnp.where(i < n_p, a_p[...], refs.pop(0)[...]) if two_a else a_p[...]
    w_ref = refs.pop(0)
    r_p = refs.pop(0)
    r = jnp.where(i < n_p, r_p[...], refs.pop(0)[...]) if two_r else r_p[...]
    o_ref[...] = r + jnp.dot(a.astype(BF16), w_ref[...], preferred_element_type=F32)


def _proj_res(a_list, w, r_list, tp, ts):
    tm = TOKEN_TILE
    n_p, n_s = tp // tm, ts // tm
    k, n = w.shape

    def specs(lst, width):
        if len(lst) == 2:
            return [pl.BlockSpec((tm, width), lambda i: (jnp.minimum(i, n_p - 1), 0)),
                    pl.BlockSpec((tm, width), lambda i: (jnp.maximum(i - n_p, 0), 0))]
        return [pl.BlockSpec((tm, width), lambda i: (i, 0))]

    return pl.pallas_call(
        functools.partial(_proj_res_kernel, n_p=n_p, two_a=len(a_list) == 2, two_r=len(r_list) == 2),
        grid=(n_p + n_s,),
        in_specs=specs(a_list, k) + [_const_spec(w.shape)] + specs(r_list, n),
        out_specs=pl.BlockSpec((tm, n), lambda i: (i, 0)),
        out_shape=jax.ShapeDtypeStruct((tp + ts, n), F32),
        compiler_params=_cparams(("arbitrary",)),
        name="proj_res",
    )(*a_list, w, *r_list)


def _ssd_prompt_kernel(z_ref, xbc_ref, dt_ref, cw_ref, cb_ref, dtb_ref, alog_ref, dsk_ref, gn_ref,
                       tril_ref, e_ref, yg_ref, sfin_ref, cfin_ref, xpad_sc, st_sc):
    c = pl.program_id(1)
    q = SSM_CHUNK
    cd = xbc_ref.shape[1]
    di = z_ref.shape[1]
    gn_w = SSM_GROUPS * SSM_STATE
    hpg = di // SSM_GROUPS
    pad = SUBLANES

    @pl.when(c == 0)
    def _():
        xpad_sc[0:pad, :] = jnp.zeros((pad, cd), F32)
        st_sc[...] = jnp.zeros(st_sc.shape, F32)

    @pl.when(c > 0)
    def _():
        xpad_sc[0:pad, :] = xpad_sc[q:q + pad, :]

    xpad_sc[pad:pad + q, :] = xbc_ref[...]
    acc = cb_ref[...]
    for k in range(SSM_CONV):
        off = pad - (SSM_CONV - 1) + k
        acc = acc + xpad_sc[off:off + q, :] * cw_ref[k:k + 1, :]
    xc = _silu(acc)
    xs = xc[:, :di]
    bm = xc[:, di:di + gn_w]
    cm = xc[:, di + gn_w:]

    dt = _softplus(dt_ref[...] + dtb_ref[...])
    a = -jnp.exp(alog_ref[...])
    act = _mm01_left(tril_ref[...], dt * a)
    act_t = act.T
    act_last = act[q - 1:q, :]
    e01 = e_ref[...]
    xdt = xs * _mm01(dt, e01, 2)
    xd = xdt * _mm01(jnp.exp(act_last - act), e01, 2)
    eax = _mm01(jnp.exp(act), e01, 2)
    cdx = _mm01(jnp.exp(act[q - SUBLANES:q, :]), e01)[SUBLANES - 1:SUBLANES, :]

    row = lax.broadcasted_iota(I32, (q, q), 0)
    col = lax.broadcasted_iota(I32, (q, q), 1)
    causal = row >= col
    lane = lax.broadcasted_iota(I32, (q, LANES), 1)
    lo_half = lane < SSM_HEAD_DIM

    y_parts = []
    heads_per_group = hpg // SSM_HEAD_DIM
    for g in range(SSM_GROUPS):
        cg = cm[:, g * SSM_STATE:(g + 1) * SSM_STATE].astype(BF16)
        bg = bm[:, g * SSM_STATE:(g + 1) * SSM_STATE]
        cb = _dot_nt(cg, bg.astype(BF16))
        st_g = st_sc[:, g * hpg:(g + 1) * hpg]
        y_off = jnp.dot(cg, st_g.astype(BF16), preferred_element_type=F32)
        for pr in range(heads_per_group // 2):
            h0 = g * heads_per_group + 2 * pr
            ms = []
            for h in (h0, h0 + 1):
                seg = act[:, h:h + 1] - act_t[h:h + 1, :]
                lm = jnp.exp(jnp.where(causal, seg, -jnp.inf))
                ms.append((cb * lm).astype(BF16))
            m2 = jnp.concatenate(ms, axis=1)
            xpair = xdt[:, h0 * SSM_HEAD_DIM:(h0 + 2) * SSM_HEAD_DIM]
            rhs = jnp.concatenate([jnp.where(lo_half, xpair, 0.0),
                                   jnp.where(lo_half, 0.0, xpair)], axis=0).astype(BF16)
            y_d = jnp.dot(m2, rhs, preferred_element_type=F32)
            lo = 2 * pr * SSM_HEAD_DIM
            y_parts.append(y_d + y_off[:, lo:lo + LANES] * eax[:, g * hpg + lo:g * hpg + lo + LANES])
        upd = jnp.dot(bg.T.astype(BF16), xd[:, g * hpg:(g + 1) * hpg].astype(BF16),
                      preferred_element_type=F32)
        st_sc[:, g * hpg:(g + 1) * hpg] = st_g * cdx[:, g * hpg:(g + 1) * hpg] + upd

    y = jnp.concatenate(y_parts, axis=1) + xs * dsk_ref[...]
    yg_ref[...] = _gate_norm(y, z_ref[...], gn_ref[...], SSM_GROUPS)

    @pl.when(c == pl.num_programs(1) - 1)
    def _():
        sfin_ref[0] = st_sc[...].T
        cfin_ref[0] = xpad_sc[q:q + pad, :]


def _ssd_prompt(z, xbc, dt, cw, cb, dtb, alog, dsk, gnw, tril, e01, bp, seq):
    nc = seq // SSM_CHUNK
    q = SSM_CHUNK
    di, cd = z.shape[1], xbc.shape[1]
    rows = lambda b, c: (b * nc + c, 0)
    return pl.pallas_call(
        _ssd_prompt_kernel,
        grid=(bp, nc),
        in_specs=[
            pl.BlockSpec((q, di), rows), pl.BlockSpec((q, cd), rows), pl.BlockSpec((q, LANES), rows),
            _const_spec(cw.shape), _const_spec(cb.shape), _const_spec(dtb.shape), _const_spec(alog.shape),
            _const_spec(dsk.shape), _const_spec(gnw.shape), _const_spec(tril.shape), _const_spec(e01.shape),
        ],
        out_specs=[
            pl.BlockSpec((q, di), rows),
            pl.BlockSpec((1, di, SSM_STATE), lambda b, c: (b, 0, 0)),
            pl.BlockSpec((1, SUBLANES, cd), lambda b, c: (b, 0, 0)),
        ],
        out_shape=[jax.ShapeDtypeStruct((bp * seq, di), F32),
                   jax.ShapeDtypeStruct((bp, di, SSM_STATE), F32),
                   jax.ShapeDtypeStruct((bp, SUBLANES, cd), F32)],
        scratch_shapes=[pltpu.VMEM((q + 2 * SUBLANES, cd), F32), pltpu.VMEM((SSM_STATE, di), F32)],
        compiler_params=_cparams(("arbitrary", "arbitrary")),
        name="ssd_prompt",
    )(z, xbc, dt, cw, cb, dtb, alog, dsk, gnw, tril, e01)


def _ssd_sample_kernel(z_ref, xp_ref, dt_ref, s0_ref, cw_ref, cb_ref, dtb_ref, alog_ref, dsk_ref, gn_ref,
                       e_ref, g1_ref, yg_ref, sn_ref, yoff_sc, *, n_tok):
    hf = pl.program_id(1)
    nb = SEQ_PER_STEP
    half = nb // 2
    q = n_tok * nb
    di = z_ref.shape[1]
    gn_w = SSM_GROUPS * SSM_STATE
    hpg = di // SSM_GROUPS

    taps = [xp_ref[0, m] for m in range(n_tok + SSM_CONV - 1)]
    slabs = []
    for t in range(n_tok):
        acc = cb_ref[...]
        for k in range(SSM_CONV):
            acc = acc + taps[t + k] * cw_ref[k:k + 1, :]
        slabs.append(_silu(acc))
    xc = jnp.concatenate(slabs, axis=0)
    xs = xc[:, :di]
    bm = xc[:, di:di + gn_w]
    cm = xc[:, di + gn_w:]

    dt = _softplus(dt_ref[...] + dtb_ref[...])
    da = dt * (-jnp.exp(alog_ref[...]))
    acts = []
    run = None
    for t in range(n_tok):
        d = da[t * nb:(t + 1) * nb, :]
        run = d if run is None else run + d
        acts.append(run)
    act = jnp.concatenate(acts, axis=0)
    act_last = jnp.concatenate([acts[-1]] * n_tok, axis=0)
    e01 = e_ref[...]
    xdt = xs * _mm01(dt, e01)
    xd = xdt * _mm01(jnp.exp(act_last - act), e01)
    eax = _mm01(jnp.exp(act), e01)
    cdx = _mm01(jnp.exp(acts[-1]), e01)

    pairs = [(t, u) for t in range(n_tok) for u in range(t + 1)]
    cbp = jnp.concatenate([cm[t * nb:(t + 1) * nb, :] * bm[u * nb:(u + 1) * nb, :] for t, u in pairs], axis=0)
    seg = jnp.concatenate([acts[t] - acts[u] for t, u in pairs], axis=0)
    coef = _mm01(_mm01(cbp, g1_ref[...]) * jnp.exp(seg), e01)
    y_slabs = []
    for t in range(n_tok):
        acc = None
        for pi, (tt, u) in enumerate(pairs):
            if tt != t:
                continue
            term = coef[pi * nb:(pi + 1) * nb, :] * xdt[u * nb:(u + 1) * nb, :]
            acc = term if acc is None else acc + term
        y_slabs.append(acc)
    y_diag = jnp.concatenate(y_slabs, axis=0)

    zpad = jnp.concatenate([xd,
                            jnp.where(hf == 0, cdx[0:half, :], cdx[half:nb, :]),
                            jnp.zeros((LANES - q - half, di), F32)], axis=0)
    zt = zpad.T
    row_seq = lax.broadcasted_iota(I32, (q, 1), 0) % nb
    cm_b = cm.astype(BF16)
    y_off_g = [None] * SSM_GROUPS
    for sl in range(half):
        in_seq = row_seq == hf * half + sl
        for g in range(SSM_GROUPS):
            s_old = s0_ref[sl, g * hpg:(g + 1) * hpg, :]
            c_g = jnp.where(in_seq, cm_b[:, g * SSM_STATE:(g + 1) * SSM_STATE], jnp.zeros((), BF16))
            yo = _dot_nt(c_g, s_old.astype(BF16))
            y_off_g[g] = yo if y_off_g[g] is None else y_off_g[g] + yo
            b_g = jnp.where(in_seq, bm[:, g * SSM_STATE:(g + 1) * SSM_STATE], 0.0)
            b_pad = jnp.concatenate([b_g, jnp.zeros((LANES - q, SSM_STATE), F32)], axis=0).astype(BF16)
            zt_g = zt[g * hpg:(g + 1) * hpg, :]
            upd = jnp.dot(zt_g.astype(BF16), b_pad, preferred_element_type=F32)
            decay = zt_g[:, q + sl:q + sl + 1]
            sn_ref[sl, g * hpg:(g + 1) * hpg, :] = s_old * decay + upd
    y_off = jnp.concatenate(y_off_g, axis=1)

    @pl.when(hf == 0)
    def _():
        yoff_sc[...] = y_off

    @pl.when(hf == 1)
    def _():
        y = y_diag + (yoff_sc[...] + y_off) * eax + xs * dsk_ref[...]
        yg_ref[...] = _gate_norm(y, z_ref[...], gn_ref[...], SSM_GROUPS)


def _ssd_sample(z, xp7, dt, s0, cw, cb, dtb, alog, dsk, gnw, e01, g1, tp, n_tok):
    nsteps = xp7.shape[0]
    nb = SEQ_PER_STEP
    half = nb // 2
    q = n_tok * nb
    di = z.shape[1]
    cd = xp7.shape[3]
    base = tp // q
    rows = lambda s, hf: (base + s, 0)
    return pl.pallas_call(
        functools.partial(_ssd_sample_kernel, n_tok=n_tok),
        grid=(nsteps, 2),
        in_specs=[
            pl.BlockSpec((q, di), rows),
            pl.BlockSpec((1, n_tok + SSM_CONV - 1, nb, cd), lambda s, hf: (s, 0, 0, 0)),
            pl.BlockSpec((q, LANES), rows),
            pl.BlockSpec((half, di, SSM_STATE), lambda s, hf: (2 * s + hf, 0, 0)),
            _const_spec(cw.shape), _const_spec(cb.shape), _const_spec(dtb.shape), _const_spec(alog.shape),
            _const_spec(dsk.shape), _const_spec(gnw.shape), _const_spec(e01.shape), _const_spec(g1.shape),
        ],
        out_specs=[
            pl.BlockSpec((q, di), lambda s, hf: (s, 0)),
            pl.BlockSpec((half, di, SSM_STATE), lambda s, hf: (2 * s + hf, 0, 0)),
        ],
        out_shape=[jax.ShapeDtypeStruct((nsteps * q, di), F32),
                   jax.ShapeDtypeStruct(s0.shape, F32)],
        scratch_shapes=[pltpu.VMEM((q, di), F32)],
        compiler_params=_cparams(("arbitrary", "arbitrary")),
        name="ssd_sample",
    )(z, xp7, dt, s0, cw, cb, dtb, alog, dsk, gnw, e01, g1)


def _qkv_kernel(h_ref, kvn_ref, an_ref, wkv_ref, wq_ref, kn_ref, qn_ref, rc_ref, rs1_ref, rs2_ref,
                hsum_ref, rq_ref, eq_ref, q_ref, k_ref, v_ref):
    hn = _rms_scale(h_ref[...])
    kvw = k_ref.shape[1]
    kv = jnp.dot((hn * kvn_ref[...]).astype(BF16), wkv_ref[...], preferred_element_type=F32)
    k = kv[:, :kvw]
    v_ref[...] = kv[:, kvw:]
    inv_hd = 1.0 / ATT_HEAD_DIM
    k = k * lax.rsqrt(_mm01(k * k, hsum_ref[...], 2) * inv_hd + EPS) * kn_ref[...]
    rc, rs1, rs2 = rc_ref[...], rs1_ref[...], rs2_ref[...]
    rk = kvw // LANES
    k_ref[...] = _rope(k, _tile_lanes(rc, rk), _tile_lanes(rs1, rk), _tile_lanes(rs2, rk))
    q = jnp.dot((hn * an_ref[...]).astype(BF16), wq_ref[...], preferred_element_type=F32)
    rsq = lax.rsqrt(_mm01(q * q, rq_ref[...], 2) * inv_hd + EPS)
    q = q * _mm01(rsq, eq_ref[...], 2) * qn_ref[...]
    rq = q.shape[1] // LANES
    q_ref[...] = _rope(q, _tile_lanes(rc, rq), _tile_lanes(rs1, rq), _tile_lanes(rs2, rq))


def _qkv(h, kvn, an, wkv, wq, knt, qnt, rc, rs1, rs2, hsum, rq, eq):
    t, d = h.shape
    tm = TOKEN_TILE
    kvw = wkv.shape[1] // 2
    qw = wq.shape[1]
    rows = lambda i: (i, 0)
    return pl.pallas_call(
        _qkv_kernel,
        grid=(t // tm,),
        in_specs=[pl.BlockSpec((tm, d), rows)] + [_const_spec(a.shape) for a in (kvn, an, wkv, wq, knt, qnt)]
        + [pl.BlockSpec((tm, LANES), rows)] * 3 + [_const_spec(a.shape) for a in (hsum, rq, eq)],
        out_specs=[pl.BlockSpec((tm, qw), rows), pl.BlockSpec((tm, kvw), rows), pl.BlockSpec((tm, kvw), rows)],
        out_shape=[jax.ShapeDtypeStruct((t, qw), F32), jax.ShapeDtypeStruct((t, kvw), F32),
                   jax.ShapeDtypeStruct((t, kvw), F32)],
        compiler_params=_cparams(("arbitrary",)),
        name="qkv",
    )(h, kvn, an, wkv, wq, knt, qnt, rc, rs1, rs2, hsum, rq, eq)


def _attn_prompt_kernel(sink_ref, q_ref, kc_ref, kp_ref, vc_ref, vp_ref, o_ref):
    i = pl.program_id(1)
    w = WINDOW
    hd = ATT_HEAD_DIM
    n_q = q_ref.shape[1] // hd
    grp = n_q // ATT_KV_HEADS
    rows = grp * w
    row = lax.broadcasted_iota(I32, (rows, 2 * w), 0) % w
    col = lax.broadcasted_iota(I32, (rows, 2 * w), 1)
    dist = row + w - col
    mask = (dist >= 0) & (dist < w) & ((col >= w) | (i > 0))
    r_head = lax.broadcasted_iota(I32, (rows, 1), 0) // w
    q = q_ref[...]
    outs = [None] * n_q
    for g in range(ATT_KV_HEADS):
        sl = slice(g * hd, (g + 1) * hd)
        kk = jnp.concatenate([kp_ref[:, sl], kc_ref[:, sl]], axis=0).astype(BF16)
        vv = jnp.concatenate([vp_ref[:, sl], vc_ref[:, sl]], axis=0).astype(BF16)
        qs = jnp.concatenate([q[:, (g * grp + hq) * hd:(g * grp + hq + 1) * hd] for hq in range(grp)],
                             axis=0).astype(BF16)
        s = jnp.where(mask, _dot_nt(qs, kk) * (hd ** -0.5), -jnp.inf)
        sink = jnp.zeros((rows, 1), F32)
        for hq in range(grp):
            sink = jnp.where(r_head == hq, sink_ref[g * grp + hq], sink)
        m = jnp.maximum(jnp.max(s, axis=-1, keepdims=True), sink)
        p = jnp.exp(s - m)
        denom = jnp.sum(p, axis=-1, keepdims=True) + jnp.exp(sink - m)
        o = jnp.dot(p.astype(BF16), vv, preferred_element_type=F32) * (1.0 / denom)
        for hq in range(grp):
            outs[g * grp + hq] = o[hq * w:(hq + 1) * w, :]
    o_ref[...] = jnp.concatenate(outs, axis=1)


def _attn_prompt(sinks, q, k, v, bp, seq):
    w = WINDOW
    nb = seq // w
    qw, kvw = q.shape[1], k.shape[1]
    cur = lambda b, i: (b * nb + i, 0)
    prev = lambda b, i: (b * nb + jnp.maximum(i - 1, 0), 0)
    return pl.pallas_call(
        _attn_prompt_kernel,
        grid=(bp, nb),
        in_specs=[pl.BlockSpec(memory_space=pltpu.SMEM),
                  pl.BlockSpec((w, qw), cur), pl.BlockSpec((w, kvw), cur), pl.BlockSpec((w, kvw), prev),
                  pl.BlockSpec((w, kvw), cur), pl.BlockSpec((w, kvw), prev)],
        out_specs=pl.BlockSpec((w, qw), cur),
        out_shape=jax.ShapeDtypeStruct((bp * seq, qw), F32),
        compiler_params=_cparams(("arbitrary", "arbitrary")),
        name="attn_prompt",
    )(sinks, q, k, k, v, v)


def _attn_sample_kernel(sink_ref, q_ref, kn_ref, vn_ref, kc_ref, vc_ref, o_ref, *, n_tok):
    nb = SEQ_PER_STEP
    qn = n_tok * nb
    hd = ATT_HEAD_DIM
    lc = kc_ref.shape[1]
    n_q = q_ref.shape[1] // hd
    grp = n_q // ATT_KV_HEADS
    rows = grp * qn
    r = lax.broadcasted_iota(I32, (rows, 1), 0)
    r_seq = r % nb
    r_tok = (r % qn) // nb
    ccol = lax.broadcasted_iota(I32, (rows, lc), 1)
    mask_c = ccol >= r_tok + 1 + (lc - WINDOW)
    ncol = lax.broadcasted_iota(I32, (rows, LANES), 1)
    mask_n = (ncol < qn) & (ncol % nb == r_seq) & (ncol // nb <= r_tok)
    q = q_ref[...]
    zpad = jnp.zeros((LANES - qn, hd), F32)
    outs = [None] * n_q
    for g in range(ATT_KV_HEADS):
        sl = slice(g * hd, (g + 1) * hd)
        qs = jnp.concatenate([q[:, (g * grp + hq) * hd:(g * grp + hq + 1) * hd] for hq in range(grp)],
                             axis=0).astype(BF16)
        k_new = jnp.concatenate([kn_ref[:, sl], zpad], axis=0).astype(BF16)
        v_new = jnp.concatenate([vn_ref[:, sl], zpad], axis=0).astype(BF16)
        s_n = jnp.where(mask_n, _dot_nt(qs, k_new) * (hd ** -0.5), -jnp.inf)
        s_c = jnp.zeros((rows, lc), F32)
        for sq in range(nb):
            s_sq = _dot_nt(qs, kc_ref[sq, :, sl].astype(BF16))
            s_c = jnp.where(r_seq == sq, s_sq, s_c)
        s_c = jnp.where(mask_c, s_c * (hd ** -0.5), -jnp.inf)
        sink = jnp.zeros((rows, 1), F32)
        for hq in range(grp):
            sink = jnp.where(r // qn == hq, sink_ref[g * grp + hq], sink)
        m = jnp.maximum(jnp.maximum(jnp.max(s_c, axis=-1, keepdims=True),
                                    jnp.max(s_n, axis=-1, keepdims=True)), sink)
        p_c = jnp.exp(s_c - m)
        p_n = jnp.exp(s_n - m)
        denom = (jnp.sum(p_c, axis=-1, keepdims=True) + jnp.sum(p_n, axis=-1, keepdims=True)
                 + jnp.exp(sink - m))
        p_c = p_c / denom
        o = jnp.dot((p_n / denom).astype(BF16), v_new, preferred_element_type=F32)
        for sq in range(nb):
            o = o + jnp.dot(jnp.where(r_seq == sq, p_c, 0.0).astype(BF16), vc_ref[sq, :, sl].astype(BF16),
                            preferred_element_type=F32)
        for hq in range(grp):
            outs[g * grp + hq] = o[hq * qn:(hq + 1) * qn, :]
    o_ref[...] = jnp.concatenate(outs, axis=1)


def _attn_sample(sinks, q, k, v, kc, vc, tp, n_tok):
    nb = SEQ_PER_STEP
    qn = n_tok * nb
    nsteps = kc.shape[0] // nb
    lc, kvw = kc.shape[1], kc.shape[2]
    qw = q.shape[1]
    base = tp // qn
    rows = lambda s: (base + s, 0)
    return pl.pallas_call(
        functools.partial(_attn_sample_kernel, n_tok=n_tok),
        grid=(nsteps,),
        in_specs=[pl.BlockSpec(memory_space=pltpu.SMEM),
                  pl.BlockSpec((qn, qw), rows), pl.BlockSpec((qn, kvw), rows), pl.BlockSpec((qn, kvw), rows),
                  pl.BlockSpec((nb, lc, kvw), lambda s: (s, 0, 0)), pl.BlockSpec((nb, lc, kvw), lambda s: (s, 0, 0))],
        out_specs=pl.BlockSpec((qn, qw), lambda s: (s, 0)),
        out_shape=jax.ShapeDtypeStruct((nsteps * qn, qw), F32),
        compiler_params=_cparams(("arbitrary",)),
        name="attn_sample",
    )(sinks, q, k, v, kc, vc)


def _router_kernel(h_ref, fn_ref, wh_ref, wl_ref, b_ref, tril_ref, xn_ref, info_ref, info_t_ref, cnt_ref, carry_sc):
    i = pl.program_id(0)

    @pl.when(i == 0)
    def _():
        carry_sc[...] = jnp.zeros(carry_sc.shape, F32)

    xn = _rms_scale(h_ref[...]) * fn_ref[...]
    _store_token_major(xn_ref, xn)
    x_hi, x_lo = _split_bf16(xn, 2)
    wh, wl = wh_ref[...], wl_ref[...]
    logits = (jnp.dot(x_hi, wh, preferred_element_type=F32) + jnp.dot(x_hi, wl, preferred_element_type=F32)
              + jnp.dot(x_lo, wh, preferred_element_type=F32)) + b_ref[...]
    tm = logits.shape[0]
    lane = lax.broadcasted_iota(I32, (tm, LANES), 1).astype(F32)
    big = float(LANES)
    neg = -jnp.inf

    is_grp = (lane >= MOE_EXPERTS) & (lane < MOE_EXPERTS + MOE_GROUPS)
    lg = jnp.where(is_grp, logits, neg)
    mg = jnp.max(lg, axis=-1, keepdims=True)
    gp = 1.0 / jnp.sum(jnp.exp(lg - mg), axis=-1, keepdims=True)
    gi = jnp.min(jnp.where(lg == mg, lane, big), axis=-1, keepdims=True) - MOE_EXPERTS

    lo = gi * MOE_EXPERTS_PER_GROUP
    le = jnp.where((lane >= lo) & (lane < lo + MOE_EXPERTS_PER_GROUP), logits, neg)
    m1 = jnp.max(le, axis=-1, keepdims=True)
    i1 = jnp.min(jnp.where(le == m1, lane, big), axis=-1, keepdims=True)
    le2 = jnp.where(lane == i1, neg, le)
    m2 = jnp.max(le2, axis=-1, keepdims=True)
    i2 = jnp.min(jnp.where(le2 == m2, lane, big), axis=-1, keepdims=True)
    e2 = jnp.exp(m2 - m1)
    g1 = gp * (1.0 / (1.0 + e2))
    g2 = gp * (e2 / (1.0 + e2))

    a1 = lane == i1
    a2 = lane == i2
    onehot = jnp.where(a1 | a2, 1.0, 0.0)
    before = jnp.dot(tril_ref[...], onehot.astype(BF16), preferred_element_type=F32) + carry_sc[...]
    r1 = jnp.sum(jnp.where(a1, before, 0.0), axis=-1, keepdims=True)
    r2 = jnp.sum(jnp.where(a2, before, 0.0), axis=-1, keepdims=True)
    carry_sc[...] = carry_sc[...] + jnp.sum(onehot, axis=0, keepdims=True)
    cnt_ref[...] = carry_sc[...]

    cols = (i1, i2, g1, g2, r1, r2)
    info = jnp.zeros((tm, LANES), F32)
    for k, cval in enumerate(cols):
        info = jnp.where(lane == k, cval, info)
    info_ref[...] = info
    info_t_ref[...] = info.T[0:SUBLANES, :]


def _router(h, fn, wh, wl, b, tril):
    t, d = h.shape
    tm = TOKEN_TILE
    rows = lambda i: (i, 0)
    return pl.pallas_call(
        _router_kernel,
        grid=(t // tm,),
        in_specs=[pl.BlockSpec((tm, d), rows)] + [_const_spec(a.shape) for a in (fn, wh, wl, b, tril)],
        out_specs=[pl.BlockSpec((tm * ROW_TILES, LANES), rows), pl.BlockSpec((tm, LANES), rows),
                   pl.BlockSpec((SUBLANES, tm), lambda i: (0, i)), _const_spec((1, LANES))],
        out_shape=[jax.ShapeDtypeStruct((t * ROW_TILES, LANES), F32), jax.ShapeDtypeStruct((t, LANES), F32),
                   jax.ShapeDtypeStruct((SUBLANES, t), F32), jax.ShapeDtypeStruct((1, LANES), F32)],
        scratch_shapes=[pltpu.VMEM((1, LANES), F32)],
        compiler_params=_cparams(("arbitrary",)),
        name="moe_router",
    )(h, fn, wh, wl, b, tril)


def _store_token_major(ref, x):
    n = x.shape[0]
    for j in range(ROW_TILES):
        ref[pl.ds(j, n, stride=ROW_TILES), :] = x[:, j * LANES:(j + 1) * LANES]


def _load_token_major(ref, n):
    return jnp.concatenate([ref[pl.ds(j, n, stride=ROW_TILES), :] for j in range(ROW_TILES)], axis=1)


def _dest_kernel(pst_ref, info_ref, dest_ref):
    info = info_ref[...]
    e = info[0:2, :]
    start = jnp.zeros(e.shape, F32)
    for k in range(MOE_EXPERTS):
        start = jnp.where(e == k, pst_ref[k].astype(F32), start)
    dest = (start + info[4:6, :]).astype(I32)
    dest_ref[...] = jnp.concatenate([dest, jnp.zeros((SUBLANES - 2, dest.shape[1]), I32)], axis=0)


def _dest(pstarts, info_t):
    return pl.pallas_call(
        _dest_kernel,
        in_specs=[pl.BlockSpec(memory_space=pltpu.SMEM), pl.BlockSpec(memory_space=pltpu.VMEM)],
        out_specs=pl.BlockSpec(memory_space=pltpu.VMEM),
        out_shape=jax.ShapeDtypeStruct(info_t.shape, I32),
        name="moe_dest",
    )(pstarts, info_t)


def _tile_copy(src, src_row, dst, dst_row, sem):
    return pltpu.make_async_copy(src.at[pl.ds(pl.multiple_of(src_row * ROW_TILES, ROW_TILES), ROW_TILES)],
                                 dst.at[pl.ds(pl.multiple_of(dst_row * ROW_TILES, ROW_TILES), ROW_TILES)], sem)


def _wait_tiles(ref, n_tokens, sem):
    blk = ref.at[pl.ds(0, n_tokens * ROW_TILES)]
    pltpu.make_async_copy(blk, blk, sem).wait()


def _dispatch_kernel(d1_ref, d2_ref, pend_ref, pcnt_ref, xn_ref, xbuf_hbm, zero_sc, sem_z, sem):
    i = pl.program_id(0)
    tm = xn_ref.shape[0] // ROW_TILES
    blk_rows = zero_sc.shape[0]

    @pl.when(i == 0)
    def _():
        zero_sc[...] = jnp.zeros(zero_sc.shape, F32)

        def zero_copy(e):
            start = pl.multiple_of(pend_ref[e] * ROW_TILES - blk_rows, ROW_TILES)
            return pltpu.make_async_copy(zero_sc, xbuf_hbm.at[pl.ds(start, blk_rows)], sem_z)

        for e in range(MOE_EXPERTS):
            @pl.when(pcnt_ref[e] > 0)
            def _():
                zero_copy(e).start()
        first_unused = pend_ref[MOE_EXPERTS - 1] * ROW_TILES // blk_rows
        n_blocks = xbuf_hbm.shape[0] // blk_rows

        def tail_copy(b):
            dst = xbuf_hbm.at[pl.ds(pl.multiple_of(b * blk_rows, blk_rows), blk_rows)]
            return pltpu.make_async_copy(zero_sc, dst, sem_z)

        def tail_start(b, carry):
            tail_copy(b).start()
            return carry

        def tail_wait(b, carry):
            tail_copy(b).wait()
            return carry
        lax.fori_loop(first_unused, n_blocks, tail_start, 0)
        for e in range(MOE_EXPERTS):
            @pl.when(pcnt_ref[e] > 0)
            def _():
                zero_copy(e).wait()
        lax.fori_loop(first_unused, n_blocks, tail_wait, 0)

    def body(r, carry):
        t = i * tm + r
        _tile_copy(xn_ref, r, xbuf_hbm, d1_ref[t], sem).start()
        _tile_copy(xn_ref, r, xbuf_hbm, d2_ref[t], sem).start()
        return carry
    lax.fori_loop(0, tm, body, 0, unroll=8)
    _wait_tiles(xn_ref, tm, sem)
    _wait_tiles(xn_ref, tm, sem)


def _dispatch(dest1, dest2, pends, pcounts, xn_tm, n_rows):
    n_tok = dest1.shape[0]
    tm = DISPATCH_TILE
    grid_spec = pltpu.PrefetchScalarGridSpec(
        num_scalar_prefetch=4,
        grid=(n_tok // tm,),
        in_specs=[pl.BlockSpec((tm * ROW_TILES, LANES), lambda i, *_: (i, 0))],
        out_specs=pl.BlockSpec(memory_space=pl.ANY),
        scratch_shapes=[pltpu.VMEM((EXPERT_ROWS * ROW_TILES, LANES), F32), pltpu.SemaphoreType.DMA(()),
                        pltpu.SemaphoreType.DMA(())],
    )
    return pl.pallas_call(
        _dispatch_kernel,
        grid_spec=grid_spec,
        out_shape=jax.ShapeDtypeStruct((n_rows * ROW_TILES, LANES), F32),
        compiler_params=_cparams(("arbitrary",)),
        name="moe_dispatch",
    )(dest1, dest2, pends, pcounts, xn_tm)


def _expert_kernel(blk_e_ref, nvalid_ref, next_e_ref, x_ref, wg_hbm, wu_hbm, wd_hbm, y_ref,
                   wg_sc, wu_sc, wd_sc, wg_st, wu_st, wd_st, sem, *, layer):
    i = pl.program_id(0)
    nv = nvalid_ref[0]
    rows = x_ref.shape[0] // ROW_TILES

    def weight_copies(e):
        return (pltpu.make_async_copy(wg_hbm.at[layer, e], wg_st, sem.at[0]),
                pltpu.make_async_copy(wu_hbm.at[layer, e], wu_st, sem.at[1]),
                pltpu.make_async_copy(wd_hbm.at[layer, e], wd_st, sem.at[2]))

    @pl.when((i == 0) & (nv > 0))
    def _():
        for c in weight_copies(blk_e_ref[0]):
            c.start()

    @pl.when(i < nv)
    def _():
        e = blk_e_ref[i]
        e_prev = blk_e_ref[jnp.maximum(i - 1, 0)]

        @pl.when((i == 0) | (e != e_prev))
        def _():
            for c in weight_copies(e):
                c.wait()
            wg_sc[...] = wg_st[...].astype(BF16)
            wu_sc[...] = wu_st[...].astype(BF16)
            wd_sc[...] = wd_st[...].astype(BF16)
            nxt = next_e_ref[e]

            @pl.when(nxt < MOE_EXPERTS)
            def _():
                for c in weight_copies(nxt):
                    c.start()

        x = _load_token_major(x_ref, rows).astype(BF16)
        hid = _silu(jnp.dot(x, wg_sc[...], preferred_element_type=F32)) * jnp.dot(
            x, wu_sc[...], preferred_element_type=F32)
        _store_token_major(y_ref, jnp.dot(hid.astype(BF16), wd_sc[...], preferred_element_type=F32))

    @pl.when(i >= nv)
    def _():
        y_ref[...] = jnp.zeros(y_ref.shape, F32)


def _experts(blk_e, nvalid, next_e, xbuf, wg, wu, wd, layer):
    nblk = blk_e.shape[0]
    d, hdim = wg.shape[2], wg.shape[3]
    rows = EXPERT_ROWS * ROW_TILES
    grid_spec = pltpu.PrefetchScalarGridSpec(
        num_scalar_prefetch=3,
        grid=(nblk,),
        in_specs=[pl.BlockSpec((rows, LANES), lambda i, be, nv, ne: (jnp.minimum(i, nv[0] - 1), 0)),
                  pl.BlockSpec(memory_space=pl.ANY), pl.BlockSpec(memory_space=pl.ANY),
                  pl.BlockSpec(memory_space=pl.ANY)],
        out_specs=pl.BlockSpec((rows, LANES), lambda i, be, nv, ne: (i, 0)),
        scratch_shapes=[pltpu.VMEM((d, hdim), BF16), pltpu.VMEM((d, hdim), BF16), pltpu.VMEM((hdim, d), BF16),
                        pltpu.VMEM((d, hdim), F32), pltpu.VMEM((d, hdim), F32), pltpu.VMEM((hdim, d), F32),
                        pltpu.SemaphoreType.DMA((3,))],
    )
    return pl.pallas_call(
        functools.partial(_expert_kernel, layer=layer),
        grid_spec=grid_spec,
        out_shape=jax.ShapeDtypeStruct((nblk * rows, LANES), F32),
        compiler_params=_cparams(("arbitrary",)),
        name="moe_experts",
    )(blk_e, nvalid, next_e, xbuf, wg, wu, wd)


def _combine_kernel(d1_ref, d2_ref, h_ref, info_ref, y_hbm, o_ref, r_sc, sem, *, base_tile):
    i = pl.program_id(0)
    n = pl.num_programs(0)
    tm = h_ref.shape[0]

    def start(step):
        slot = step % 2
        base = (base_tile + step) * tm

        def body(r, carry):
            _tile_copy(y_hbm, d1_ref[base + r], r_sc.at[slot, 0], r, sem.at[slot]).start()
            _tile_copy(y_hbm, d2_ref[base + r], r_sc.at[slot, 1], r, sem.at[slot]).start()
            return carry
        lax.fori_loop(0, tm, body, 0, unroll=8)

    @pl.when(i == 0)
    def _():
        start(i)

    @pl.when(i + 1 < n)
    def _():
        start(i + 1)

    slot = i % 2
    _wait_tiles(r_sc.at[slot, 0], tm, sem.at[slot])
    _wait_tiles(r_sc.at[slot, 1], tm, sem.at[slot])
    info = info_ref[...]
    y1 = _load_token_major(r_sc.at[slot, 0], tm)
    y2 = _load_token_major(r_sc.at[slot, 1], tm)
    o_ref[...] = h_ref[...] + (y1 * info[:, 2:3] + y2 * info[:, 3:4])


def _combine(dest1, dest2, h, info, ybuf, row0, nrows):
    d = h.shape[1]
    tm = TOKEN_TILE
    base_tile = row0 // tm
    rows = lambda i, a, b: (base_tile + i, 0)
    grid_spec = pltpu.PrefetchScalarGridSpec(
        num_scalar_prefetch=2,
        grid=(nrows // tm,),
        in_specs=[pl.BlockSpec((tm, d), rows), pl.BlockSpec((tm, LANES), rows), pl.BlockSpec(memory_space=pl.ANY)],
        out_specs=pl.BlockSpec((tm, d), lambda i, a, b: (i, 0)),
        scratch_shapes=[pltpu.VMEM((2, 2, tm * ROW_TILES, LANES), F32), pltpu.SemaphoreType.DMA((2,))],
    )
    return pl.pallas_call(
        functools.partial(_combine_kernel, base_tile=base_tile),
        grid_spec=grid_spec,
        out_shape=jax.ShapeDtypeStruct((nrows, d), F32),
        compiler_params=_cparams(("arbitrary",)),
        name="moe_combine",
    )(dest1, dest2, h, info, ybuf)


def _moe(h, fn, w_grp, b_grp, w_rt, b_rt, wg, wu, wd, layer, tril, out_ranges):
    t, d = h.shape
    w_cat = jnp.zeros((d, LANES), F32).at[:, :MOE_EXPERTS].set(w_rt).at[:, MOE_EXPERTS:MOE_EXPERTS + MOE_GROUPS].set(w_grp)
    b_cat = jnp.zeros((1, LANES), F32).at[0, :MOE_EXPERTS].set(b_rt).at[0, MOE_EXPERTS:MOE_EXPERTS + MOE_GROUPS].set(b_grp)
    w_hi = w_cat.astype(BF16)
    w_lo = (w_cat - w_hi.astype(F32)).astype(BF16)
    xn_tm, info, info_t, cnt = _router(h, fn.reshape(1, d), w_hi, w_lo, b_cat, tril)

    counts = cnt[0, :MOE_EXPERTS].astype(I32)
    pcounts = (counts + EXPERT_ROWS - 1) // EXPERT_ROWS * EXPERT_ROWS
    pends = jnp.cumsum(pcounts)
    pstarts = pends - pcounts
    nblk = -(-(2 * t + MOE_EXPERTS * (EXPERT_ROWS - 1)) // EXPERT_ROWS)
    blk_start = jnp.arange(nblk, dtype=I32) * EXPERT_ROWS
    blk_e = jnp.minimum(jnp.sum((pends[None, :] <= blk_start[:, None]).astype(I32), axis=1), MOE_EXPERTS - 1)
    nvalid = pends[-1:] // EXPERT_ROWS
    eid = jnp.arange(MOE_EXPERTS, dtype=I32)
    later = (eid[None, :] > eid[:, None]) & (pcounts[None, :] > 0)
    next_e = jnp.min(jnp.where(later, eid[None, :], MOE_EXPERTS), axis=1).astype(I32)

    dest = _dest(pstarts, info_t)
    dest1, dest2 = dest[0], dest[1]
    xbuf = _dispatch(dest1, dest2, pends, pcounts, xn_tm, nblk * EXPERT_ROWS)
    ybuf = _experts(blk_e, nvalid, next_e, xbuf, wg, wu, wd, layer)
    return [_combine(dest1, dest2, h, info, ybuf, r0, nr) for r0, nr in out_ranges]


def _rope_tables(pos):
    half = ROT_DIM // 2
    inv = ROPE_THETA ** (-jnp.arange(0, ROT_DIM, 2, dtype=F32) / ROT_DIM)
    ang = pos.astype(F32)[:, None] * inv[None, :]
    cos, sin = jnp.cos(ang), jnp.sin(ang)
    n = pos.shape[0]
    ones = jnp.ones((n, ATT_HEAD_DIM - ROT_DIM), F32)
    zeros_r = jnp.zeros((n, ATT_HEAD_DIM - ROT_DIM), F32)
    zeros_h = jnp.zeros((n, half), F32)
    c = jnp.concatenate([cos, cos, ones], axis=1)
    s1 = jnp.concatenate([-sin, zeros_h, zeros_r], axis=1)
    s2 = jnp.concatenate([zeros_h, sin, zeros_r], axis=1)
    reps = LANES // ATT_HEAD_DIM
    return tuple(jnp.tile(a, (1, reps)) for a in (c, s1, s2))


def _to_step_order(a, nsteps, n_tok):
    c = a.shape[-1]
    return a.reshape(nsteps, SEQ_PER_STEP, n_tok, c).transpose(0, 2, 1, 3).reshape(nsteps * n_tok * SEQ_PER_STEP, c)


def _from_step_order(a, nsteps, n_tok):
    c = a.shape[-1]
    return a.reshape(nsteps, n_tok, SEQ_PER_STEP, c).transpose(0, 2, 1, 3).reshape(nsteps * SEQ_PER_STEP, n_tok, c)


def kernel(x_prompt, x_sample, state_ssm, state_conv, cache_k_win, cache_v_win, ssm_norm, ssm_w_in, ssm_conv_w, ssm_conv_b, ssm_dt_bias, ssm_a_log, ssm_d, ssm_gate_norm, ssm_w_out, kv_norm, w_kv, k_norm, attn_norm, w_q, q_norm, sinks, w_o, ffn_norm, moe_w_group, moe_b_group, moe_w_router, moe_b_router, moe_w_gate, moe_w_up, moe_w_down):
    bp, seq, d = x_prompt.shape
    bs, n_tok, _ = x_sample.shape
    tp, ts = bp * seq, bs * n_tok
    nsteps = bs // SEQ_PER_STEP
    n_heads = ssm_d.shape[1]
    di = n_heads * SSM_HEAD_DIM
    gn_w = SSM_GROUPS * SSM_STATE
    cdim = di + 2 * gn_w
    n_q = sinks.shape[1]
    kvw = ATT_KV_HEADS * ATT_HEAD_DIM

    xp2 = x_prompt.reshape(tp, d)
    xs2 = _to_step_order(x_sample, nsteps, n_tok)

    lane_i = jnp.arange(LANES)
    e01 = (lane_i[:, None] == (jnp.arange(di) // SSM_HEAD_DIM)[None, :]).astype(BF16)
    hpg = di // SSM_GROUPS
    g1 = ((jnp.arange(gn_w) // SSM_STATE)[:, None] == (lane_i // (hpg // SSM_HEAD_DIM))[None, :])
    g1 = (g1 & (lane_i < n_heads)[None, :]).astype(BF16)
    tril_c = (jnp.arange(SSM_CHUNK)[:, None] >= jnp.arange(SSM_CHUNK)[None, :]).astype(BF16)
    tril_x = (jnp.arange(TOKEN_TILE)[:, None] > jnp.arange(TOKEN_TILE)[None, :]).astype(BF16)
    hsum = ((jnp.arange(kvw) // ATT_HEAD_DIM)[:, None] == (jnp.arange(kvw) // ATT_HEAD_DIM)[None, :]).astype(BF16)
    qw = n_q * ATT_HEAD_DIM
    rq = ((jnp.arange(qw) // ATT_HEAD_DIM)[:, None] == lane_i[None, :]).astype(BF16)
    eq = rq.T

    w_in = ssm_w_in[0]
    wz = w_in[:, :di].astype(BF16)
    wx = w_in[:, di:di + cdim].astype(BF16)
    wd = jnp.zeros((d, LANES), F32).at[:, :n_heads].set(w_in[:, di + cdim:]).astype(BF16)
    z, xbc, dt = _inproj(xp2, xs2, ssm_norm[0].reshape(1, d), wz, wx, wd)

    pad_h = lambda v: jnp.zeros((1, LANES), F32).at[0, :n_heads].set(v)
    cw, cb = ssm_conv_w[0], ssm_conv_b[0].reshape(1, cdim)
    dtb, alog = pad_h(ssm_dt_bias[0]), pad_h(ssm_a_log[0])
    dsk = jnp.repeat(ssm_d[0], SSM_HEAD_DIM).reshape(1, di)
    gnw = ssm_gate_norm[0].reshape(1, di)

    yg_p, s_fin, c_fin = _ssd_prompt(z, xbc, dt, cw, cb, dtb, alog, dsk, gnw, tril_c, e01, bp, seq)
    ssm_p = s_fin.reshape(1, bp, n_heads, SSM_HEAD_DIM, SSM_STATE)
    conv_p = c_fin[:, SUBLANES - (SSM_CONV - 1):, :].reshape(1, bp, SSM_CONV - 1, cdim)

    xbc_s = xbc[tp:].reshape(nsteps, n_tok, SEQ_PER_STEP, cdim)
    conv_in = state_conv[0].reshape(nsteps, SEQ_PER_STEP, SSM_CONV - 1, cdim).transpose(0, 2, 1, 3)
    xp7 = jnp.concatenate([conv_in, xbc_s], axis=1)
    s0 = state_ssm[0].reshape(bs, di, SSM_STATE)
    yg_s, s_new = _ssd_sample(z, xp7, dt, s0, cw, cb, dtb, alog, dsk, gnw, e01, g1, tp, n_tok)
    ssm_s = s_new.reshape(1, bs, n_heads, SSM_HEAD_DIM, SSM_STATE)
    conv_s = _from_step_order(xbc[tp:], nsteps, n_tok)[:, n_tok - (SSM_CONV - 1):, :].reshape(
        1, bs, SSM_CONV - 1, cdim)

    h = _proj_res([yg_p, yg_s], ssm_w_out[0].astype(BF16), [xp2, xs2], tp, ts)
    (h,) = _moe(h, ffn_norm[0], moe_w_group[0], moe_b_group[0], moe_w_router[0], moe_b_router[0],
                moe_w_gate, moe_w_up, moe_w_down, 0, tril_x, [(0, tp + ts)])

    pos = jnp.concatenate([jnp.tile(jnp.arange(seq, dtype=I32), bp),
                           jnp.tile(jnp.repeat(PAST_LEN + jnp.arange(n_tok, dtype=I32), SEQ_PER_STEP), nsteps)])
    rc, rs1, rs2 = _rope_tables(pos)
    q, k, v = _qkv(h, kv_norm.reshape(1, d), attn_norm[0].reshape(1, d), w_kv.astype(BF16), w_q[0].astype(BF16),
                   jnp.tile(k_norm, ATT_KV_HEADS).reshape(1, kvw), jnp.tile(q_norm[0], n_q).reshape(1, qw),
                   rc, rs1, rs2, hsum, rq, eq)
    sk = sinks[0]
    o_p = _attn_prompt(sk, q, k, v, bp, seq)
    lc = cache_k_win.shape[1]
    kc = cache_k_win.reshape(bs, lc, kvw)
    vc = cache_v_win.reshape(bs, lc, kvw)
    o_s = _attn_sample(sk, q, k, v, kc, vc, tp, n_tok)
    h = _proj_res([o_p, o_s], w_o[0].astype(BF16), [h], tp, ts)
    y_p, y_s = _moe(h, ffn_norm[1], moe_w_group[1], moe_b_group[1], moe_w_router[1], moe_b_router[1],
                    moe_w_gate, moe_w_up, moe_w_down, 1, tril_x, [(0, tp), (tp, ts)])

    wl = min(WINDOW, seq)
    k_p = k[:tp].reshape(bp, seq, kvw)[:, seq - wl:].reshape(bp, wl, ATT_KV_HEADS, ATT_HEAD_DIM)
    v_p = v[:tp].reshape(bp, seq, kvw)[:, seq - wl:].reshape(bp, wl, ATT_KV_HEADS, ATT_HEAD_DIM)
    k_new = _from_step_order(k[tp:], nsteps, n_tok)
    v_new = _from_step_order(v[tp:], nsteps, n_tok)
    k_s = jnp.concatenate([kc, k_new], axis=1)[:, n_tok:].reshape(bs, lc, ATT_KV_HEADS, ATT_HEAD_DIM)
    v_s = jnp.concatenate([vc, v_new], axis=1)[:, n_tok:].reshape(bs, lc, ATT_KV_HEADS, ATT_HEAD_DIM)
    return (y_p.reshape(bp, seq, d), _from_step_order(y_s, nsteps, n_tok),
            ssm_p, conv_p, k_p, v_p, ssm_s, conv_s, k_s, v_s)
```

```python
import functools

import jax
import jax.numpy as jnp
from jax import lax
from jax.experimental import pallas as pl
from jax.experimental.pallas import tpu as pltpu

F32 = jnp.float32
BF16 = jnp.bfloat16
I32 = jnp.int32

EPS = 1e-6
SSM_HEAD_DIM = 64
SSM_GROUPS = 4
SSM_STATE = 128
SSM_CONV = 4
SSM_CHUNK = 128
ATT_HEAD_DIM = 64
ATT_KV_HEADS = 4
WINDOW = 128
ROT_DIM = ATT_HEAD_DIM // 4
ROPE_THETA = 500000.0
MOE_GROUPS = 4
MOE_EXPERTS_PER_GROUP = 8
MOE_EXPERTS = MOE_GROUPS * MOE_EXPERTS_PER_GROUP
MOE_BLOCK = 128
PAST_LEN = 16384

LANES = 128
SUBLANES = 8
SEQ_PER_STEP = SUBLANES
TOKEN_TILE = 256
ROW_TILES = 8
PROJ_TILE = 512
DISPATCH_TILE = 512
EXPERT_ROWS = 256
VMEM_LIMIT = 56 * 1024 * 1024


def _cparams(sem):
    return pltpu.CompilerParams(dimension_semantics=sem, vmem_limit_bytes=VMEM_LIMIT)


def _const_spec(shape):
    nd = len(shape)
    return pl.BlockSpec(shape, lambda *_: (0,) * nd)


def _split_bf16(v, n):
    parts = []
    r = v
    for k in range(n):
        p = r.astype(BF16)
        parts.append(p)
        if k + 1 < n:
            r = r - p.astype(F32)
    return parts


def _mm01(v, m01, n=3):
    acc = None
    for p in _split_bf16(v, n):
        d = jnp.dot(p, m01, preferred_element_type=F32)
        acc = d if acc is None else acc + d
    return acc


def _mm01_left(m01, v, n=3):
    acc = None
    for p in _split_bf16(v, n):
        d = jnp.dot(m01, p, preferred_element_type=F32)
        acc = d if acc is None else acc + d
    return acc


def _dot_nt(a, b):
    return lax.dot_general(a, b, (((1,), (1,)), ((), ())), preferred_element_type=F32)


def _sigmoid(x):
    return 0.5 * jnp.tanh(0.5 * x) + 0.5


def _silu(x):
    return x * _sigmoid(x)


def _softplus(x):
    return jnp.maximum(x, 0.0) + jnp.log1p(jnp.exp(-jnp.abs(x)))


def _rms_scale(x):
    return x * lax.rsqrt(jnp.mean(x * x, axis=-1, keepdims=True) + EPS)


def _gate_norm(y, z, gn, n_groups):
    yz = y * _silu(z)
    w = yz.shape[-1] // n_groups
    outs = []
    for g in range(n_groups):
        v = yz[:, g * w:(g + 1) * w]
        outs.append(_rms_scale(v) * gn[:, g * w:(g + 1) * w])
    return jnp.concatenate(outs, axis=1)


def _rope(x, c, s1, s2):
    w = x.shape[-1]
    return x * c + pltpu.roll(x, w - ROT_DIM // 2, 1) * s1 + pltpu.roll(x, ROT_DIM // 2, 1) * s2


def _tile_lanes(t, reps):
    return t if reps == 1 else jnp.concatenate([t] * reps, axis=1)


def _inproj_kernel(xp_ref, xs_ref, g_ref, wz_ref, wx_ref, wd_ref, z_ref, xbc_ref, dt_ref, *, n_p):
    i = pl.program_id(0)
    x = jnp.where(i < n_p, xp_ref[...], xs_ref[...])
    xn = (_rms_scale(x) * g_ref[...]).astype(BF16)
    z_ref[...] = jnp.dot(xn, wz_ref[...], preferred_element_type=F32)
    xbc_ref[...] = jnp.dot(xn, wx_ref[...], preferred_element_type=F32)
    dt_ref[...] = jnp.dot(xn, wd_ref[...], preferred_element_type=F32)


def _inproj(xp2, xs2, g, wz, wx, wd):
    tp, d = xp2.shape
    ts = xs2.shape[0]
    tm = TOKEN_TILE
    n_p, n_s = tp // tm, ts // tm
    t = tp + ts
    return pl.pallas_call(
        functools.partial(_inproj_kernel, n_p=n_p),
        grid=(n_p + n_s,),
        in_specs=[
            pl.BlockSpec((tm, d), lambda i: (jnp.minimum(i, n_p - 1), 0)),
            pl.BlockSpec((tm, d), lambda i: (jnp.maximum(i - n_p, 0), 0)),
            _const_spec(g.shape), _const_spec(wz.shape), _const_spec(wx.shape), _const_spec(wd.shape),
        ],
        out_specs=[
            pl.BlockSpec((tm, wz.shape[1]), lambda i: (i, 0)),
            pl.BlockSpec((tm, wx.shape[1]), lambda i: (i, 0)),
            pl.BlockSpec((tm, wd.shape[1]), lambda i: (i, 0)),
        ],
        out_shape=[jax.ShapeDtypeStruct((t, wz.shape[1]), F32),
                   jax.ShapeDtypeStruct((t, wx.shape[1]), F32),
                   jax.ShapeDtypeStruct((t, wd.shape[1]), F32)],
        compiler_params=_cparams(("arbitrary",)),
        name="inproj",
    )(xp2, xs2, g, wz, wx, wd)


def _ssd_prompt_kernel(z_ref, xbc_ref, dt_ref, x_ref, cw_ref, cb_ref, dtb_ref, alog_ref, dsk_ref, gn_ref,
                       tril_ref, e_ref, wout_ref, h_ref, sfin_ref, cfin_ref, xpad_sc, st_sc):
    c = pl.program_id(1)
    q = SSM_CHUNK
    cd = xbc_ref.shape[1]
    di = z_ref.shape[1]
    gn_w = SSM_GROUPS * SSM_STATE
    hpg = di // SSM_GROUPS
    pad = SUBLANES

    @pl.when(c == 0)
    def _():
        xpad_sc[0:pad, :] = jnp.zeros((pad, cd), F32)
        st_sc[...] = jnp.zeros(st_sc.shape, F32)

    @pl.when(c > 0)
    def _():
        xpad_sc[0:pad, :] = xpad_sc[q:q + pad, :]

    xpad_sc[pad:pad + q, :] = xbc_ref[...]
    acc = cb_ref[...]
    for k in range(SSM_CONV):
        off = pad - (SSM_CONV - 1) + k
        acc = acc + xpad_sc[off:off + q, :] * cw_ref[k:k + 1, :]
    xc = _silu(acc)
    xs = xc[:, :di]
    bm = xc[:, di:di + gn_w]
    cm = xc[:, di + gn_w:]

    dt = _softplus(dt_ref[...] + dtb_ref[...])
    a = -jnp.exp(alog_ref[...])
    act = _mm01_left(tril_ref[...], dt * a)
    act_t = act.T
    act_last = act[q - 1:q, :]
    e01 = e_ref[...]
    xdt = xs * _mm01(dt, e01, 2)
    xd = xdt * _mm01(jnp.exp(act_last - act), e01, 2)
    eax = _mm01(jnp.exp(act), e01, 2)
    cdx = _mm01(jnp.exp(act[q - SUBLANES:q, :]), e01)[SUBLANES - 1:SUBLANES, :]

    row = lax.broadcasted_iota(I32, (q, q), 0)
    col = lax.broadcasted_iota(I32, (q, q), 1)
    causal = row >= col
    lane = lax.broadcasted_iota(I32, (q, LANES), 1)
    lo_half = lane < SSM_HEAD_DIM

    y_parts = []
    heads_per_group = hpg // SSM_HEAD_DIM
    for g in range(SSM_GROUPS):
        cg = cm[:, g * SSM_STATE:(g + 1) * SSM_STATE].astype(BF16)
        bg = bm[:, g * SSM_STATE:(g + 1) * SSM_STATE]
        cb = _dot_nt(cg, bg.astype(BF16))
        st_g = st_sc[:, g * hpg:(g + 1) * hpg]
        y_off = jnp.dot(cg, st_g.astype(BF16), preferred_element_type=F32)
        for pr in range(heads_per_group // 2):
            h0 = g * heads_per_group + 2 * pr
            ms = []
            for h in (h0, h0 + 1):
                seg = act[:, h:h + 1] - act_t[h:h + 1, :]
                lm = jnp.exp(jnp.where(causal, seg, -jnp.inf))
                ms.append((cb * lm).astype(BF16))
            m2 = jnp.concatenate(ms, axis=1)
            xpair = xdt[:, h0 * SSM_HEAD_DIM:(h0 + 2) * SSM_HEAD_DIM]
            rhs = jnp.concatenate([jnp.where(lo_half, xpair, 0.0),
                                   jnp.where(lo_half, 0.0, xpair)], axis=0).astype(BF16)
            y_d = jnp.dot(m2, rhs, preferred_element_type=F32)
            lo = 2 * pr * SSM_HEAD_DIM
            y_parts.append(y_d + y_off[:, lo:lo + LANES] * eax[:, g * hpg + lo:g * hpg + lo + LANES])
        upd = jnp.dot(bg.T.astype(BF16), xd[:, g * hpg:(g + 1) * hpg].astype(BF16),
                      preferred_element_type=F32)
        st_sc[:, g * hpg:(g + 1) * hpg] = st_g * cdx[:, g * hpg:(g + 1) * hpg] + upd

    y = jnp.concatenate(y_parts, axis=1) + xs * dsk_ref[...]
    yg = _gate_norm(y, z_ref[...], gn_ref[...], SSM_GROUPS)
    h_ref[...] = x_ref[...] + jnp.dot(yg.astype(BF16), wout_ref[...], preferred_element_type=F32)

    @pl.when(c == pl.num_programs(1) - 1)
    def _():
        sfin_ref[0] = st_sc[...].T
        cfin_ref[0] = xpad_sc[q:q + pad, :]


def _ssd_prompt(z, xbc, dt, x, cw, cb, dtb, alog, dsk, gnw, tril, e01, wout, bp, seq):
    nc = seq // SSM_CHUNK
    q = SSM_CHUNK
    di, cd, d = z.shape[1], xbc.shape[1], x.shape[1]
    rows = lambda b, c: (b * nc + c, 0)
    return pl.pallas_call(
        _ssd_prompt_kernel,
        grid=(bp, nc),
        in_specs=[
            pl.BlockSpec((q, di), rows), pl.BlockSpec((q, cd), rows), pl.BlockSpec((q, LANES), rows),
            pl.BlockSpec((q, d), rows),
            _const_spec(cw.shape), _const_spec(cb.shape), _const_spec(dtb.shape), _const_spec(alog.shape),
            _const_spec(dsk.shape), _const_spec(gnw.shape), _const_spec(tril.shape), _const_spec(e01.shape),
            _const_spec(wout.shape),
        ],
        out_specs=[
            pl.BlockSpec((q, d), rows),
            pl.BlockSpec((1, di, SSM_STATE), lambda b, c: (b, 0, 0)),
            pl.BlockSpec((1, SUBLANES, cd), lambda b, c: (b, 0, 0)),
        ],
        out_shape=[jax.ShapeDtypeStruct((bp * seq, d), F32),
                   jax.ShapeDtypeStruct((bp, di, SSM_STATE), F32),
                   jax.ShapeDtypeStruct((bp, SUBLANES, cd), F32)],
        scratch_shapes=[pltpu.VMEM((q + 2 * SUBLANES, cd), F32), pltpu.VMEM((SSM_STATE, di), F32)],
        compiler_params=_cparams(("arbitrary", "arbitrary")),
        name="ssd_prompt",
    )(z, xbc, dt, x, cw, cb, dtb, alog, dsk, gnw, tril, e01, wout)


def _ssd_sample_kernel(z_ref, xp_ref, dt_ref, s0_ref, x_ref, cw_ref, cb_ref, dtb_ref, alog_ref, dsk_ref, gn_ref,
                       e_ref, g1_ref, wout_ref, h_ref, sn_ref, yoff_sc, *, n_tok):
    hf = pl.program_id(1)
    nb = SEQ_PER_STEP
    half = nb // 2
    q = n_tok * nb
    di = z_ref.shape[1]
    gn_w = SSM_GROUPS * SSM_STATE
    hpg = di // SSM_GROUPS

    taps = [xp_ref[0, m] for m in range(n_tok + SSM_CONV - 1)]
    slabs = []
    for t in range(n_tok):
        acc = cb_ref[...]
        for k in range(SSM_CONV):
            acc = acc + taps[t + k] * cw_ref[k:k + 1, :]
        slabs.append(_silu(acc))
    xc = jnp.concatenate(slabs, axis=0)
    xs = xc[:, :di]
    bm = xc[:, di:di + gn_w]
    cm = xc[:, di + gn_w:]

    dt = _softplus(dt_ref[...] + dtb_ref[...])
    da = dt * (-jnp.exp(alog_ref[...]))
    acts = []
    run = None
    for t in range(n_tok):
        d = da[t * nb:(t + 1) * nb, :]
        run = d if run is None else run + d
        acts.append(run)
    act = jnp.concatenate(acts, axis=0)
    act_last = jnp.concatenate([acts[-1]] * n_tok, axis=0)
    e01 = e_ref[...]
    xdt = xs * _mm01(dt, e01)
    xd = xdt * _mm01(jnp.exp(act_last - act), e01)
    eax = _mm01(jnp.exp(act), e01)
    cdx = _mm01(jnp.exp(acts[-1]), e01)

    pairs = [(t, u) for t in range(n_tok) for u in range(t + 1)]
    cbp = jnp.concatenate([cm[t * nb:(t + 1) * nb, :] * bm[u * nb:(u + 1) * nb, :] for t, u in pairs], axis=0)
    seg = jnp.concatenate([acts[t] - acts[u] for t, u in pairs], axis=0)
    coef = _mm01(_mm01(cbp, g1_ref[...]) * jnp.exp(seg), e01)
    y_slabs = []
    for t in range(n_tok):
        acc = None
        for pi, (tt, u) in enumerate(pairs):
            if tt != t:
                continue
            term = coef[pi * nb:(pi + 1) * nb, :] * xdt[u * nb:(u + 1) * nb, :]
            acc = term if acc is None else acc + term
        y_slabs.append(acc)
    y_diag = jnp.concatenate(y_slabs, axis=0)

    zpad = jnp.concatenate([xd,
                            jnp.where(hf == 0, cdx[0:half, :], cdx[half:nb, :]),
                            jnp.zeros((LANES - q - half, di), F32)], axis=0)
    zt = zpad.T
    row_seq = lax.broadcasted_iota(I32, (q, 1), 0) % nb
    cm_b = cm.astype(BF16)
    y_off_g = [None] * SSM_GROUPS
    for sl in range(half):
        in_seq = row_seq == hf * half + sl
        for g in range(SSM_GROUPS):
            s_old = s0_ref[sl, g * hpg:(g + 1) * hpg, :]
            c_g = jnp.where(in_seq, cm_b[:, g * SSM_STATE:(g + 1) * SSM_STATE], jnp.zeros((), BF16))
            yo = _dot_nt(c_g, s_old.astype(BF16))
            y_off_g[g] = yo if y_off_g[g] is None else y_off_g[g] + yo
            b_g = jnp.where(in_seq, bm[:, g * SSM_STATE:(g + 1) * SSM_STATE], 0.0)
            b_pad = jnp.concatenate([b_g, jnp.zeros((LANES - q, SSM_STATE), F32)], axis=0).astype(BF16)
            zt_g = zt[g * hpg:(g + 1) * hpg, :]
            upd = jnp.dot(zt_g.astype(BF16), b_pad, preferred_element_type=F32)
            decay = zt_g[:, q + sl:q + sl + 1]
            sn_ref[sl, g * hpg:(g + 1) * hpg, :] = s_old * decay + upd
    y_off = jnp.concatenate(y_off_g, axis=1)

    @pl.when(hf == 0)
    def _():
        yoff_sc[...] = y_off

    @pl.when(hf == 1)
    def _():
        y = y_diag + (yoff_sc[...] + y_off) * eax + xs * dsk_ref[...]
        yg = _gate_norm(y, z_ref[...], gn_ref[...], SSM_GROUPS)
        h_ref[...] = x_ref[...] + jnp.dot(yg.astype(BF16), wout_ref[...], preferred_element_type=F32)


def _ssd_sample(z, xp7, dt, s0, x, cw, cb, dtb, alog, dsk, gnw, e01, g1, wout, tp, n_tok):
    nsteps = xp7.shape[0]
    nb = SEQ_PER_STEP
    half = nb // 2
    q = n_tok * nb
    di, d = z.shape[1], x.shape[1]
    cd = xp7.shape[3]
    base = tp // q
    rows = lambda s, hf: (base + s, 0)
    return pl.pallas_call(
        functools.partial(_ssd_sample_kernel, n_tok=n_tok),
        grid=(nsteps, 2),
        in_specs=[
            pl.BlockSpec((q, di), rows),
            pl.BlockSpec((1, n_tok + SSM_CONV - 1, nb, cd), lambda s, hf: (s, 0, 0, 0)),
            pl.BlockSpec((q, LANES), rows),
            pl.BlockSpec((half, di, SSM_STATE), lambda s, hf: (2 * s + hf, 0, 0)),
            pl.BlockSpec((q, d), lambda s, hf: (s, 0)),
            _const_spec(cw.shape), _const_spec(cb.shape), _const_spec(dtb.shape), _const_spec(alog.shape),
            _const_spec(dsk.shape), _const_spec(gnw.shape), _const_spec(e01.shape), _const_spec(g1.shape),
            _const_spec(wout.shape),
        ],
        out_specs=[
            pl.BlockSpec((q, d), lambda s, hf: (s, 0)),
            pl.BlockSpec((half, di, SSM_STATE), lambda s, hf: (2 * s + hf, 0, 0)),
        ],
        out_shape=[jax.ShapeDtypeStruct((nsteps * q, d), F32),
                   jax.ShapeDtypeStruct(s0.shape, F32)],
        scratch_shapes=[pltpu.VMEM((q, di), F32)],
        compiler_params=_cparams(("arbitrary", "arbitrary")),
        name="ssd_sample",
    )(z, xp7, dt, s0, x, cw, cb, dtb, alog, dsk, gnw, e01, g1, wout)


def _qkv_kernel(hp_ref, hs_ref, kvn_ref, an_ref, wkv_ref, wq_ref, kn_ref, qn_ref, rc_ref, rs1_ref, rs2_ref,
                hsum_ref, rq_ref, eq_ref, q_ref, k_ref, v_ref, *, n_p):
    hn = _rms_scale(jnp.where(pl.program_id(0) < n_p, hp_ref[...], hs_ref[...]))
    kvw = k_ref.shape[1]
    kv = jnp.dot((hn * kvn_ref[...]).astype(BF16), wkv_ref[...], preferred_element_type=F32)
    k = kv[:, :kvw]
    v_ref[...] = kv[:, kvw:]
    inv_hd = 1.0 / ATT_HEAD_DIM
    k = k * lax.rsqrt(_mm01(k * k, hsum_ref[...], 2) * inv_hd + EPS) * kn_ref[...]
    rc, rs1, rs2 = rc_ref[...], rs1_ref[...], rs2_ref[...]
    rk = kvw // LANES
    k_ref[...] = _rope(k, _tile_lanes(rc, rk), _tile_lanes(rs1, rk), _tile_lanes(rs2, rk))
    q = jnp.dot((hn * an_ref[...]).astype(BF16), wq_ref[...], preferred_element_type=F32)
    rsq = lax.rsqrt(_mm01(q * q, rq_ref[...], 2) * inv_hd + EPS)
    q = q * _mm01(rsq, eq_ref[...], 2) * qn_ref[...]
    rq = q.shape[1] // LANES
    q_ref[...] = _rope(q, _tile_lanes(rc, rq), _tile_lanes(rs1, rq), _tile_lanes(rs2, rq))


def _qkv(h_p, h_s, kvn, an, wkv, wq, knt, qnt, rc, rs1, rs2, hsum, rq, eq):
    d = h_p.shape[1]
    tm = TOKEN_TILE
    n_p = h_p.shape[0] // tm
    t = h_p.shape[0] + h_s.shape[0]
    kvw = wkv.shape[1] // 2
    qw = wq.shape[1]
    rows = lambda i: (i, 0)
    return pl.pallas_call(
        functools.partial(_qkv_kernel, n_p=n_p),
        grid=(t // tm,),
        in_specs=_two_source_specs(tm, d, n_p) + [_const_spec(a.shape) for a in (kvn, an, wkv, wq, knt, qnt)]
        + [pl.BlockSpec((tm, LANES), rows)] * 3 + [_const_spec(a.shape) for a in (hsum, rq, eq)],
        out_specs=[pl.BlockSpec((tm, qw), rows), pl.BlockSpec((tm, kvw), rows), pl.BlockSpec((tm, kvw), rows)],
        out_shape=[jax.ShapeDtypeStruct((t, qw), F32), jax.ShapeDtypeStruct((t, kvw), F32),
                   jax.ShapeDtypeStruct((t, kvw), F32)],
        compiler_params=_cparams(("arbitrary",)),
        name="qkv",
    )(h_p, h_s, kvn, an, wkv, wq, knt, qnt, rc, rs1, rs2, hsum, rq, eq)


def _attn_prompt_kernel(sink_ref, q_ref, kc_ref, kp_ref, vc_ref, vp_ref, o_ref):
    i = pl.program_id(1)
    w = WINDOW
    hd = ATT_HEAD_DIM
    n_q = q_ref.shape[1] // hd
    grp = n_q // ATT_KV_HEADS
    rows = grp * w
    row = lax.broadcasted_iota(I32, (rows, 2 * w), 0) % w
    col = lax.broadcasted_iota(I32, (rows, 2 * w), 1)
    dist = row + w - col
    mask = (dist >= 0) & (dist < w) & ((col >= w) | (i > 0))
    r_head = lax.broadcasted_iota(I32, (rows, 1), 0) // w
    q = q_ref[...]
    outs = [None] * n_q
    for g in range(ATT_KV_HEADS):
        sl = slice(g * hd, (g + 1) * hd)
        kk = jnp.concatenate([kp_ref[:, sl], kc_ref[:, sl]], axis=0).astype(BF16)
        vv = jnp.concatenate([vp_ref[:, sl], vc_ref[:, sl]], axis=0).astype(BF16)
        qs = jnp.concatenate([q[:, (g * grp + hq) * hd:(g * grp + hq + 1) * hd] for hq in range(grp)],
                             axis=0).astype(BF16)
        s = jnp.where(mask, _dot_nt(qs, kk) * (hd ** -0.5), -jnp.inf)
        sink = jnp.zeros((rows, 1), F32)
        for hq in range(grp):
            sink = jnp.where(r_head == hq, sink_ref[g * grp + hq], sink)
        m = jnp.maximum(jnp.max(s, axis=-1, keepdims=True), sink)
        p = jnp.exp(s - m)
        denom = jnp.sum(p, axis=-1, keepdims=True) + jnp.exp(sink - m)
        o = jnp.dot(p.astype(BF16), vv, preferred_element_type=F32) * (1.0 / denom)
        for hq in range(grp):
            outs[g * grp + hq] = o[hq * w:(hq + 1) * w, :]
    o_ref[...] = jnp.concatenate(outs, axis=1)


def _attn_prompt(sinks, q, k, v, bp, seq):
    w = WINDOW
    nb = seq // w
    qw, kvw = q.shape[1], k.shape[1]
    cur = lambda b, i: (b * nb + i, 0)
    prev = lambda b, i: (b * nb + jnp.maximum(i - 1, 0), 0)
    return pl.pallas_call(
        _attn_prompt_kernel,
        grid=(bp, nb),
        in_specs=[pl.BlockSpec(memory_space=pltpu.SMEM),
                  pl.BlockSpec((w, qw), cur), pl.BlockSpec((w, kvw), cur), pl.BlockSpec((w, kvw), prev),
                  pl.BlockSpec((w, kvw), cur), pl.BlockSpec((w, kvw), prev)],
        out_specs=pl.BlockSpec((w, qw), cur),
        out_shape=jax.ShapeDtypeStruct((bp * seq, qw), F32),
        compiler_params=_cparams(("arbitrary", "arbitrary")),
        name="attn_prompt",
    )(sinks, q, k, k, v, v)


def _attn_sample_kernel(sink_ref, q_ref, kn_ref, vn_ref, kc_ref, vc_ref, o_ref, *, n_tok):
    nb = SEQ_PER_STEP
    qn = n_tok * nb
    hd = ATT_HEAD_DIM
    lc = kc_ref.shape[1]
    n_q = q_ref.shape[1] // hd
    grp = n_q // ATT_KV_HEADS
    rows = grp * qn
    r = lax.broadcasted_iota(I32, (rows, 1), 0)
    r_seq = r % nb
    r_tok = (r % qn) // nb
    ccol = lax.broadcasted_iota(I32, (rows, lc), 1)
    mask_c = ccol >= r_tok + 1 + (lc - WINDOW)
    ncol = lax.broadcasted_iota(I32, (rows, LANES), 1)
    mask_n = (ncol < qn) & (ncol % nb == r_seq) & (ncol // nb <= r_tok)
    q = q_ref[...]
    zpad = jnp.zeros((LANES - qn, hd), F32)
    outs = [None] * n_q
    for g in range(ATT_KV_HEADS):
        sl = slice(g * hd, (g + 1) * hd)
        qs = jnp.concatenate([q[:, (g * grp + hq) * hd:(g * grp + hq + 1) * hd] for hq in range(grp)],
                             axis=0).astype(BF16)
        k_new = jnp.concatenate([kn_ref[:, sl], zpad], axis=0).astype(BF16)
        v_new = jnp.concatenate([vn_ref[:, sl], zpad], axis=0).astype(BF16)
        s_n = jnp.where(mask_n, _dot_nt(qs, k_new) * (hd ** -0.5), -jnp.inf)
        s_c = jnp.zeros((rows, lc), F32)
        for sq in range(nb):
            s_sq = _dot_nt(qs, kc_ref[sq, :, sl].astype(BF16))
            s_c = jnp.where(r_seq == sq, s_sq, s_c)
        s_c = jnp.where(mask_c, s_c * (hd ** -0.5), -jnp.inf)
        sink = jnp.zeros((rows, 1), F32)
        for hq in range(grp):
            sink = jnp.where(r // qn == hq, sink_ref[g * grp + hq], sink)
        m = jnp.maximum(jnp.maximum(jnp.max(s_c, axis=-1, keepdims=True),
                                    jnp.max(s_n, axis=-1, keepdims=True)), sink)
        p_c = jnp.exp(s_c - m)
        p_n = jnp.exp(s_n - m)
        denom = (jnp.sum(p_c, axis=-1, keepdims=True) + jnp.sum(p_n, axis=-1, keepdims=True)
                 + jnp.exp(sink - m))
        p_c = p_c / denom
        o = jnp.dot((p_n / denom).astype(BF16), v_new, preferred_element_type=F32)
        for sq in range(nb):
            o = o + jnp.dot(jnp.where(r_seq == sq, p_c, 0.0).astype(BF16), vc_ref[sq, :, sl].astype(BF16),
                            preferred_element_type=F32)
        for hq in range(grp):
            outs[g * grp + hq] = o[hq * qn:(hq + 1) * qn, :]
    o_ref[...] = jnp.concatenate(outs, axis=1)


def _attn_sample(sinks, q, k, v, kc, vc, tp, n_tok):
    nb = SEQ_PER_STEP
    qn = n_tok * nb
    nsteps = kc.shape[0] // nb
    lc, kvw = kc.shape[1], kc.shape[2]
    qw = q.shape[1]
    base = tp // qn
    rows = lambda s: (base + s, 0)
    return pl.pallas_call(
        functools.partial(_attn_sample_kernel, n_tok=n_tok),
        grid=(nsteps,),
        in_specs=[pl.BlockSpec(memory_space=pltpu.SMEM),
                  pl.BlockSpec((qn, qw), rows), pl.BlockSpec((qn, kvw), rows), pl.BlockSpec((qn, kvw), rows),
                  pl.BlockSpec((nb, lc, kvw), lambda s: (s, 0, 0)), pl.BlockSpec((nb, lc, kvw), lambda s: (s, 0, 0))],
        out_specs=pl.BlockSpec((qn, qw), lambda s: (s, 0)),
        out_shape=jax.ShapeDtypeStruct((nsteps * qn, qw), F32),
        compiler_params=_cparams(("arbitrary",)),
        name="attn_sample",
    )(sinks, q, k, v, kc, vc)


def _proj_res_kernel(a_ref, w_ref, r_ref, o_ref):
    o_ref[...] = r_ref[...] + jnp.dot(a_ref[...].astype(BF16), w_ref[...], preferred_element_type=F32)


def _proj_res(a, w, res):
    t, k = a.shape
    n = w.shape[1]
    tm = min(PROJ_TILE, t)
    rows = lambda i: (i, 0)
    return pl.pallas_call(
        _proj_res_kernel,
        grid=(t // tm,),
        in_specs=[pl.BlockSpec((tm, k), rows), _const_spec(w.shape), pl.BlockSpec((tm, n), rows)],
        out_specs=pl.BlockSpec((tm, n), rows),
        out_shape=jax.ShapeDtypeStruct((t, n), F32),
        compiler_params=_cparams(("arbitrary",)),
        name="proj_res",
    )(a, w, res)


def _router_kernel(hp_ref, hs_ref, fn_ref, wh_ref, wl_ref, b_ref, tril_ref, xn_ref, info_ref, info_t_ref, cnt_ref,
                   carry_sc, *, n_p):
    i = pl.program_id(0)

    @pl.when(i == 0)
    def _():
        carry_sc[...] = jnp.zeros(carry_sc.shape, F32)

    xn = _rms_scale(jnp.where(i < n_p, hp_ref[...], hs_ref[...])) * fn_ref[...]
    _store_token_major(xn_ref, xn)
    x_hi, x_lo = _split_bf16(xn, 2)
    wh, wl = wh_ref[...], wl_ref[...]
    logits = (jnp.dot(x_hi, wh, preferred_element_type=F32) + jnp.dot(x_hi, wl, preferred_element_type=F32)
              + jnp.dot(x_lo, wh, preferred_element_type=F32)) + b_ref[...]
    tm = logits.shape[0]
    lane = lax.broadcasted_iota(I32, (tm, LANES), 1).astype(F32)
    big = float(LANES)
    neg = -jnp.inf

    is_grp = (lane >= MOE_EXPERTS) & (lane < MOE_EXPERTS + MOE_GROUPS)
    lg = jnp.where(is_grp, logits, neg)
    mg = jnp.max(lg, axis=-1, keepdims=True)
    gp = 1.0 / jnp.sum(jnp.exp(lg - mg), axis=-1, keepdims=True)
    gi = jnp.min(jnp.where(lg == mg, lane, big), axis=-1, keepdims=True) - MOE_EXPERTS

    lo = gi * MOE_EXPERTS_PER_GROUP
    le = jnp.where((lane >= lo) & (lane < lo + MOE_EXPERTS_PER_GROUP), logits, neg)
    m1 = jnp.max(le, axis=-1, keepdims=True)
    i1 = jnp.min(jnp.where(le == m1, lane, big), axis=-1, keepdims=True)
    le2 = jnp.where(lane == i1, neg, le)
    m2 = jnp.max(le2, axis=-1, keepdims=True)
    i2 = jnp.min(jnp.where(le2 == m2, lane, big), axis=-1, keepdims=True)
    e2 = jnp.exp(m2 - m1)
    g1 = gp * (1.0 / (1.0 + e2))
    g2 = gp * (e2 / (1.0 + e2))

    a1 = lane == i1
    a2 = lane == i2
    onehot = jnp.where(a1 | a2, 1.0, 0.0)
    before = jnp.dot(tril_ref[...], onehot.astype(BF16), preferred_element_type=F32) + carry_sc[...]
    r1 = jnp.sum(jnp.where(a1, before, 0.0), axis=-1, keepdims=True)
    r2 = jnp.sum(jnp.where(a2, before, 0.0), axis=-1, keepdims=True)
    carry_sc[...] = carry_sc[...] + jnp.sum(onehot, axis=0, keepdims=True)
    cnt_ref[...] = carry_sc[...]

    cols = (i1, i2, g1, g2, r1, r2)
    info = jnp.zeros((tm, LANES), F32)
    for k, cval in enumerate(cols):
        info = jnp.where(lane == k, cval, info)
    info_ref[...] = info
    info_t_ref[...] = info.T[0:SUBLANES, :]


def _two_source_specs(tm, width, n_p):
    return [pl.BlockSpec((tm, width), lambda i: (jnp.minimum(i, n_p - 1), 0)),
            pl.BlockSpec((tm, width), lambda i: (jnp.maximum(i - n_p, 0), 0))]


def _router(h_p, h_s, fn, wh, wl, b, tril):
    d = h_p.shape[1]
    tm = TOKEN_TILE
    n_p = h_p.shape[0] // tm
    t = h_p.shape[0] + h_s.shape[0]
    rows = lambda i: (i, 0)
    return pl.pallas_call(
        functools.partial(_router_kernel, n_p=n_p),
        grid=(t // tm,),
        in_specs=_two_source_specs(tm, d, n_p) + [_const_spec(a.shape) for a in (fn, wh, wl, b, tril)],
        out_specs=[pl.BlockSpec((tm * ROW_TILES, LANES), rows), pl.BlockSpec((tm, LANES), rows),
                   pl.BlockSpec((SUBLANES, tm), lambda i: (0, i)), _const_spec((1, LANES))],
        out_shape=[jax.ShapeDtypeStruct((t * ROW_TILES, LANES), F32), jax.ShapeDtypeStruct((t, LANES), F32),
                   jax.ShapeDtypeStruct((SUBLANES, t), F32), jax.ShapeDtypeStruct((1, LANES), F32)],
        scratch_shapes=[pltpu.VMEM((1, LANES), F32)],
        compiler_params=_cparams(("arbitrary",)),
        name="moe_router",
    )(h_p, h_s, fn, wh, wl, b, tril)


def _store_token_major(ref, x):
    n = x.shape[0]
    for j in range(ROW_TILES):
        ref[pl.ds(j, n, stride=ROW_TILES), :] = x[:, j * LANES:(j + 1) * LANES]


def _load_token_major(ref, n):
    return jnp.concatenate([ref[pl.ds(j, n, stride=ROW_TILES), :] for j in range(ROW_TILES)], axis=1)


def _dest_kernel(pst_ref, info_ref, dest_ref):
    info = info_ref[...]
    e = info[0:2, :]
    start = jnp.zeros(e.shape, F32)
    for k in range(MOE_EXPERTS):
        start = jnp.where(e == k, pst_ref[k].astype(F32), start)
    dest = (start + info[4:6, :]).astype(I32)
    dest_ref[...] = jnp.concatenate([dest, jnp.zeros((SUBLANES - 2, dest.shape[1]), I32)], axis=0)


def _dest(pstarts, info_t):
    return pl.pallas_call(
        _dest_kernel,
        in_specs=[pl.BlockSpec(memory_space=pltpu.SMEM), pl.BlockSpec(memory_space=pltpu.VMEM)],
        out_specs=pl.BlockSpec(memory_space=pltpu.VMEM),
        out_shape=jax.ShapeDtypeStruct(info_t.shape, I32),
        name="moe_dest",
    )(pstarts, info_t)


def _tile_copy(src, src_row, dst, dst_row, sem):
    return pltpu.make_async_copy(src.at[pl.ds(pl.multiple_of(src_row * ROW_TILES, ROW_TILES), ROW_TILES)],
                                 dst.at[pl.ds(pl.multiple_of(dst_row * ROW_TILES, ROW_TILES), ROW_TILES)], sem)


def _wait_tiles(ref, n_tokens, sem):
    blk = ref.at[pl.ds(0, n_tokens * ROW_TILES)]
    pltpu.make_async_copy(blk, blk, sem).wait()


def _dispatch_kernel(d1_ref, d2_ref, pend_ref, pcnt_ref, xn_ref, xbuf_hbm, zero_sc, sem_z, sem):
    i = pl.program_id(0)
    tm = xn_ref.shape[0] // ROW_TILES
    blk_rows = zero_sc.shape[0]

    @pl.when(i == 0)
    def _():
        zero_sc[...] = jnp.zeros(zero_sc.shape, F32)

        def zero_copy(e):
            start = pl.multiple_of(pend_ref[e] * ROW_TILES - blk_rows, ROW_TILES)
            return pltpu.make_async_copy(zero_sc, xbuf_hbm.at[pl.ds(start, blk_rows)], sem_z)

        for e in range(MOE_EXPERTS):
            @pl.when(pcnt_ref[e] > 0)
            def _():
                zero_copy(e).start()
        first_unused = pend_ref[MOE_EXPERTS - 1] * ROW_TILES // blk_rows
        n_blocks = xbuf_hbm.shape[0] // blk_rows

        def tail_copy(b):
            dst = xbuf_hbm.at[pl.ds(pl.multiple_of(b * blk_rows, blk_rows), blk_rows)]
            return pltpu.make_async_copy(zero_sc, dst, sem_z)

        def tail_start(b, carry):
            tail_copy(b).start()
            return carry

        def tail_wait(b, carry):
            tail_copy(b).wait()
            return carry
        lax.fori_loop(first_unused, n_blocks, tail_start, 0)
        for e in range(MOE_EXPERTS):
            @pl.when(pcnt_ref[e] > 0)
            def _():
                zero_copy(e).wait()
        lax.fori_loop(first_unused, n_blocks, tail_wait, 0)

    def body(r, carry):
        t = i * tm + r
        _tile_copy(xn_ref, r, xbuf_hbm, d1_ref[t], sem).start(priority=0)
        _tile_copy(xn_ref, r, xbuf_hbm, d2_ref[t], sem).start(priority=1)
        return carry
    lax.fori_loop(0, tm, body, 0, unroll=8)
    _wait_tiles(xn_ref, tm, sem)
    _wait_tiles(xn_ref, tm, sem)


def _dispatch(dest1, dest2, pends, pcounts, xn_tm, n_rows):
    n_tok = dest1.shape[0]
    tm = DISPATCH_TILE
    grid_spec = pltpu.PrefetchScalarGridSpec(
        num_scalar_prefetch=4,
        grid=(n_tok // tm,),
        in_specs=[pl.BlockSpec((tm * ROW_TILES, LANES), lambda i, *_: (i, 0))],
        out_specs=pl.BlockSpec(memory_space=pl.ANY),
        scratch_shapes=[pltpu.VMEM((EXPERT_ROWS * ROW_TILES, LANES), F32), pltpu.SemaphoreType.DMA(()),
                        pltpu.SemaphoreType.DMA(())],
    )
    return pl.pallas_call(
        _dispatch_kernel,
        grid_spec=grid_spec,
        out_shape=jax.ShapeDtypeStruct((n_rows * ROW_TILES, LANES), F32),
        compiler_params=_cparams(("arbitrary",)),
        name="moe_dispatch",
    )(dest1, dest2, pends, pcounts, xn_tm)


def _expert_kernel(blk_e_ref, nvalid_ref, next_e_ref, x_ref, wg_hbm, wu_hbm, wd_hbm, y_ref,
                   wg_sc, wu_sc, wd_sc, wg_st, wu_st, wd_st, sem, *, layer):
    i = pl.program_id(0)
    nv = nvalid_ref[0]
    rows = x_ref.shape[0] // ROW_TILES

    def weight_copies(e):
        return (pltpu.make_async_copy(wg_hbm.at[layer, e], wg_st, sem.at[0]),
                pltpu.make_async_copy(wu_hbm.at[layer, e], wu_st, sem.at[1]),
                pltpu.make_async_copy(wd_hbm.at[layer, e], wd_st, sem.at[2]))

    @pl.when((i == 0) & (nv > 0))
    def _():
        for c in weight_copies(blk_e_ref[0]):
            c.start()

    @pl.when(i < nv)
    def _():
        e = blk_e_ref[i]
        e_prev = blk_e_ref[jnp.maximum(i - 1, 0)]

        @pl.when((i == 0) | (e != e_prev))
        def _():
            for c in weight_copies(e):
                c.wait()
            wg_sc[...] = wg_st[...].astype(BF16)
            wu_sc[...] = wu_st[...].astype(BF16)
            wd_sc[...] = wd_st[...].astype(BF16)
            nxt = next_e_ref[e]

            @pl.when(nxt < MOE_EXPERTS)
            def _():
                for c in weight_copies(nxt):
                    c.start()

        x = _load_token_major(x_ref, rows).astype(BF16)
        hid = _silu(jnp.dot(x, wg_sc[...], preferred_element_type=F32)) * jnp.dot(
            x, wu_sc[...], preferred_element_type=F32)
        _store_token_major(y_ref, jnp.dot(hid.astype(BF16), wd_sc[...], preferred_element_type=F32))

    @pl.when(i >= nv)
    def _():
        y_ref[...] = jnp.zeros(y_ref.shape, F32)


def _experts(blk_e, nvalid, next_e, xbuf, wg, wu, wd, layer):
    nblk = blk_e.shape[0]
    d, hdim = wg.shape[2], wg.shape[3]
    rows = EXPERT_ROWS * ROW_TILES
    grid_spec = pltpu.PrefetchScalarGridSpec(
        num_scalar_prefetch=3,
        grid=(nblk,),
        in_specs=[pl.BlockSpec((rows, LANES), lambda i, be, nv, ne: (jnp.minimum(i, nv[0] - 1), 0)),
                  pl.BlockSpec(memory_space=pl.ANY), pl.BlockSpec(memory_space=pl.ANY),
                  pl.BlockSpec(memory_space=pl.ANY)],
        out_specs=pl.BlockSpec((rows, LANES), lambda i, be, nv, ne: (i, 0)),
        scratch_shapes=[pltpu.VMEM((d, hdim), BF16), pltpu.VMEM((d, hdim), BF16), pltpu.VMEM((hdim, d), BF16),
                        pltpu.VMEM((d, hdim), F32), pltpu.VMEM((d, hdim), F32), pltpu.VMEM((hdim, d), F32),
                        pltpu.SemaphoreType.DMA((3,))],
    )
    return pl.pallas_call(
        functools.partial(_expert_kernel, layer=layer),
        grid_spec=grid_spec,
        out_shape=jax.ShapeDtypeStruct((nblk * rows, LANES), F32),
        compiler_params=_cparams(("arbitrary",)),
        name="moe_experts",
    )(blk_e, nvalid, next_e, xbuf, wg, wu, wd)


def _combine_kernel(d1_ref, d2_ref, h_ref, info_ref, y_hbm, o_ref, r_sc, sem, *, base_tile):
    i = pl.program_id(0)
    n = pl.num_programs(0)
    tm = h_ref.shape[0]

    def start(step):
        slot = step % 2
        base = (base_tile + step) * tm

        def body(r, carry):
            _tile_copy(y_hbm, d1_ref[base + r], r_sc.at[slot, 0], r, sem.at[slot]).start(priority=0)
            _tile_copy(y_hbm, d2_ref[base + r], r_sc.at[slot, 1], r, sem.at[slot]).start(priority=1)
            return carry
        lax.fori_loop(0, tm, body, 0, unroll=8)

    @pl.when(i == 0)
    def _():
        start(i)

    @pl.when(i + 1 < n)
    def _():
        start(i + 1)

    slot = i % 2
    _wait_tiles(r_sc.at[slot, 0], tm, sem.at[slot])
    _wait_tiles(r_sc.at[slot, 1], tm, sem.at[slot])
    info = info_ref[...]
    y1 = _load_token_major(r_sc.at[slot, 0], tm)
    y2 = _load_token_major(r_sc.at[slot, 1], tm)
    o_ref[...] = h_ref[...] + (y1 * info[:, 2:3] + y2 * info[:, 3:4])


def _combine(dest1, dest2, h, info, ybuf, row0):
    nrows, d = h.shape
    tm = TOKEN_TILE
    base_tile = row0 // tm
    own = lambda i, a, b: (i, 0)
    grid_spec = pltpu.PrefetchScalarGridSpec(
        num_scalar_prefetch=2,
        grid=(nrows // tm,),
        in_specs=[pl.BlockSpec((tm, d), own), pl.BlockSpec((tm, LANES), lambda i, a, b: (base_tile + i, 0)),
                  pl.BlockSpec(memory_space=pl.ANY)],
        out_specs=pl.BlockSpec((tm, d), own),
        scratch_shapes=[pltpu.VMEM((2, 2, tm * ROW_TILES, LANES), F32), pltpu.SemaphoreType.DMA((2,))],
    )
    return pl.pallas_call(
        functools.partial(_combine_kernel, base_tile=base_tile),
        grid_spec=grid_spec,
        out_shape=jax.ShapeDtypeStruct((nrows, d), F32),
        compiler_params=_cparams(("arbitrary",)),
        name="moe_combine",
    )(dest1, dest2, h, info, ybuf)


def _moe(h_p, h_s, fn, w_grp, b_grp, w_rt, b_rt, wg, wu, wd, layer, tril):
    d = h_p.shape[1]
    t = h_p.shape[0] + h_s.shape[0]
    w_cat = jnp.zeros((d, LANES), F32).at[:, :MOE_EXPERTS].set(w_rt).at[:, MOE_EXPERTS:MOE_EXPERTS + MOE_GROUPS].set(w_grp)
    b_cat = jnp.zeros((1, LANES), F32).at[0, :MOE_EXPERTS].set(b_rt).at[0, MOE_EXPERTS:MOE_EXPERTS + MOE_GROUPS].set(b_grp)
    w_hi = w_cat.astype(BF16)
    w_lo = (w_cat - w_hi.astype(F32)).astype(BF16)
    xn_tm, info, info_t, cnt = _router(h_p, h_s, fn.reshape(1, d), w_hi, w_lo, b_cat, tril)

    counts = cnt[0, :MOE_EXPERTS].astype(I32)
    pcounts = (counts + EXPERT_ROWS - 1) // EXPERT_ROWS * EXPERT_ROWS
    pends = jnp.cumsum(pcounts)
    pstarts = pends - pcounts
    nblk = -(-(2 * t + MOE_EXPERTS * (EXPERT_ROWS - 1)) // EXPERT_ROWS)
    blk_start = jnp.arange(nblk, dtype=I32) * EXPERT_ROWS
    blk_e = jnp.minimum(jnp.sum((pends[None, :] <= blk_start[:, None]).astype(I32), axis=1), MOE_EXPERTS - 1)
    nvalid = pends[-1:] // EXPERT_ROWS
    eid = jnp.arange(MOE_EXPERTS, dtype=I32)
    later = (eid[None, :] > eid[:, None]) & (pcounts[None, :] > 0)
    next_e = jnp.min(jnp.where(later, eid[None, :], MOE_EXPERTS), axis=1).astype(I32)

    dest = _dest(pstarts, info_t)
    dest1, dest2 = dest[0], dest[1]
    xbuf = _dispatch(dest1, dest2, pends, pcounts, xn_tm, nblk * EXPERT_ROWS)
    ybuf = _experts(blk_e, nvalid, next_e, xbuf, wg, wu, wd, layer)
    return (_combine(dest1, dest2, h_p, info, ybuf, 0), _combine(dest1, dest2, h_s, info, ybuf, h_p.shape[0]))


def _rope_tables(pos):
    half = ROT_DIM // 2
    inv = ROPE_THETA ** (-jnp.arange(0, ROT_DIM, 2, dtype=F32) / ROT_DIM)
    ang = pos.astype(F32)[:, None] * inv[None, :]
    cos, sin = jnp.cos(ang), jnp.sin(ang)
    n = pos.shape[0]
    ones = jnp.ones((n, ATT_HEAD_DIM - ROT_DIM), F32)
    zeros_r = jnp.zeros((n, ATT_HEAD_DIM - ROT_DIM), F32)
    zeros_h = jnp.zeros((n, half), F32)
    c = jnp.concatenate([cos, cos, ones], axis=1)
    s1 = jnp.concatenate([-sin, zeros_h, zeros_r], axis=1)
    s2 = jnp.concatenate([zeros_h, sin, zeros_r], axis=1)
    reps = LANES // ATT_HEAD_DIM
    return tuple(jnp.tile(a, (1, reps)) for a in (c, s1, s2))


def _to_step_order(a, nsteps, n_tok):
    c = a.shape[-1]
    return a.reshape(nsteps, SEQ_PER_STEP, n_tok, c).transpose(0, 2, 1, 3).reshape(nsteps * n_tok * SEQ_PER_STEP, c)


def _from_step_order(a, nsteps, n_tok):
    c = a.shape[-1]
    return a.reshape(nsteps, n_tok, SEQ_PER_STEP, c).transpose(0, 2, 1, 3).reshape(nsteps * SEQ_PER_STEP, n_tok, c)


def kernel(x_prompt, x_sample, state_ssm, state_conv, cache_k_win, cache_v_win, ssm_norm, ssm_w_in, ssm_conv_w, ssm_conv_b, ssm_dt_bias, ssm_a_log, ssm_d, ssm_gate_norm, ssm_w_out, kv_norm, w_kv, k_norm, attn_norm, w_q, q_norm, sinks, w_o, ffn_norm, moe_w_group, moe_b_group, moe_w_router, moe_b_router, moe_w_gate, moe_w_up, moe_w_down):
    bp, seq, d = x_prompt.shape
    bs, n_tok, _ = x_sample.shape
    tp, ts = bp * seq, bs * n_tok
    nsteps = bs // SEQ_PER_STEP
    n_heads = ssm_d.shape[1]
    di = n_heads * SSM_HEAD_DIM
    gn_w = SSM_GROUPS * SSM_STATE
    cdim = di + 2 * gn_w
    n_q = sinks.shape[1]
    kvw = ATT_KV_HEADS * ATT_HEAD_DIM

    xp2 = x_prompt.reshape(tp, d)
    xs2 = _to_step_order(x_sample, nsteps, n_tok)

    lane_i = jnp.arange(LANES)
    e01 = (lane_i[:, None] == (jnp.arange(di) // SSM_HEAD_DIM)[None, :]).astype(BF16)
    hpg = di // SSM_GROUPS
    g1 = ((jnp.arange(gn_w) // SSM_STATE)[:, None] == (lane_i // (hpg // SSM_HEAD_DIM))[None, :])
    g1 = (g1 & (lane_i < n_heads)[None, :]).astype(BF16)
    tril_c = (jnp.arange(SSM_CHUNK)[:, None] >= jnp.arange(SSM_CHUNK)[None, :]).astype(BF16)
    tril_x = (jnp.arange(TOKEN_TILE)[:, None] > jnp.arange(TOKEN_TILE)[None, :]).astype(BF16)
    hsum = ((jnp.arange(kvw) // ATT_HEAD_DIM)[:, None] == (jnp.arange(kvw) // ATT_HEAD_DIM)[None, :]).astype(BF16)
    qw = n_q * ATT_HEAD_DIM
    rq = ((jnp.arange(qw) // ATT_HEAD_DIM)[:, None] == lane_i[None, :]).astype(BF16)
    eq = rq.T

    w_in = ssm_w_in[0]
    wz = w_in[:, :di].astype(BF16)
    wx = w_in[:, di:di + cdim].astype(BF16)
    wd = jnp.zeros((d, LANES), F32).at[:, :n_heads].set(w_in[:, di + cdim:]).astype(BF16)
    z, xbc, dt = _inproj(xp2, xs2, ssm_norm[0].reshape(1, d), wz, wx, wd)

    pad_h = lambda v: jnp.zeros((1, LANES), F32).at[0, :n_heads].set(v)
    cw, cb = ssm_conv_w[0], ssm_conv_b[0].reshape(1, cdim)
    dtb, alog = pad_h(ssm_dt_bias[0]), pad_h(ssm_a_log[0])
    dsk = jnp.repeat(ssm_d[0], SSM_HEAD_DIM).reshape(1, di)
    gnw = ssm_gate_norm[0].reshape(1, di)

    w_out = ssm_w_out[0].astype(BF16)
    h_p, s_fin, c_fin = _ssd_prompt(z, xbc, dt, xp2, cw, cb, dtb, alog, dsk, gnw, tril_c, e01, w_out, bp, seq)
    ssm_p = s_fin.reshape(1, bp, n_heads, SSM_HEAD_DIM, SSM_STATE)
    conv_p = c_fin[:, SUBLANES - (SSM_CONV - 1):, :].reshape(1, bp, SSM_CONV - 1, cdim)

    xbc_s = xbc[tp:].reshape(nsteps, n_tok, SEQ_PER_STEP, cdim)
    conv_in = state_conv[0].reshape(nsteps, SEQ_PER_STEP, SSM_CONV - 1, cdim).transpose(0, 2, 1, 3)
    xp7 = jnp.concatenate([conv_in, xbc_s], axis=1)
    s0 = state_ssm[0].reshape(bs, di, SSM_STATE)
    h_s, s_new = _ssd_sample(z, xp7, dt, s0, xs2, cw, cb, dtb, alog, dsk, gnw, e01, g1, w_out, tp, n_tok)
    ssm_s = s_new.reshape(1, bs, n_heads, SSM_HEAD_DIM, SSM_STATE)
    conv_s = _from_step_order(xbc[tp:], nsteps, n_tok)[:, n_tok - (SSM_CONV - 1):, :].reshape(
        1, bs, SSM_CONV - 1, cdim)

    h_p, h_s = _moe(h_p, h_s, ffn_norm[0], moe_w_group[0], moe_b_group[0], moe_w_router[0], moe_b_router[0],
                    moe_w_gate, moe_w_up, moe_w_down, 0, tril_x)

    pos = jnp.concatenate([jnp.tile(jnp.arange(seq, dtype=I32), bp),
                           jnp.tile(jnp.repeat(PAST_LEN + jnp.arange(n_tok, dtype=I32), SEQ_PER_STEP), nsteps)])
    rc, rs1, rs2 = _rope_tables(pos)
    q, k, v = _qkv(h_p, h_s, kv_norm.reshape(1, d), attn_norm[0].reshape(1, d), w_kv.astype(BF16), w_q[0].astype(BF16),
                   jnp.tile(k_norm, ATT_KV_HEADS).reshape(1, kvw), jnp.tile(q_norm[0], n_q).reshape(1, qw),
                   rc, rs1, rs2, hsum, rq, eq)
    sk = sinks[0]
    wo = w_o[0].astype(BF16)
    lc = cache_k_win.shape[1]
    kc = cache_k_win.reshape(bs, lc, kvw)
    vc = cache_v_win.reshape(bs, lc, kvw)
    h_p = _proj_res(_attn_prompt(sk, q, k, v, bp, seq), wo, h_p)
    h_s = _proj_res(_attn_sample(sk, q, k, v, kc, vc, tp, n_tok), wo, h_s)
    y_p, y_s = _moe(h_p, h_s, ffn_norm[1], moe_w_group[1], moe_b_group[1], moe_w_router[1], moe_b_router[1],
                    moe_w_gate, moe_w_up, moe_w_down, 1, tril_x)

    wl = min(WINDOW, seq)
    k_p = k[:tp].reshape(bp, seq, kvw)[:, seq - wl:].reshape(bp, wl, ATT_KV_HEADS, ATT_HEAD_DIM)
    v_p = v[:tp].reshape(bp, seq, kvw)[:, seq - wl:].reshape(bp, wl, ATT_KV_HEADS, ATT_HEAD_DIM)
    k_new = _from_step_order(k[tp:], nsteps, n_tok)
    v_new = _from_step_order(v[tp:], nsteps, n_tok)
    k_s = jnp.concatenate([kc, k_new], axis=1)[:, n_tok:].reshape(bs, lc, ATT_KV_HEADS, ATT_HEAD_DIM)
    v_s = jnp.concatenate([vc, v_new], axis=1)[:, n_tok:].reshape(bs, lc, ATT_KV_HEADS, ATT_HEAD_DIM)
    return (y_p.reshape(bp, seq, d), _from_step_order(y_s, nsteps, n_tok),
            ssm_p, conv_p, k_p, v_p, ssm_s, conv_s, k_s, v_s)
```

```python
import functools

import jax
import jax.numpy as jnp
from jax import lax
from jax.experimental import pallas as pl
from jax.experimental.pallas import tpu as pltpu

F32 = jnp.float32
BF16 = jnp.bfloat16
I32 = jnp.int32

EPS = 1e-6
SSM_HEAD_DIM = 64
SSM_GROUPS = 4
SSM_STATE = 128
SSM_CONV = 4
SSM_CHUNK = 128
ATT_HEAD_DIM = 64
ATT_KV_HEADS = 4
WINDOW = 128
ROT_DIM = ATT_HEAD_DIM // 4
ROPE_THETA = 500000.0
MOE_GROUPS = 4
MOE_EXPERTS_PER_GROUP = 8
MOE_EXPERTS = MOE_GROUPS * MOE_EXPERTS_PER_GROUP
MOE_BLOCK = 128
PAST_LEN = 16384

LANES = 128
SUBLANES = 8
SEQ_PER_STEP = SUBLANES
TOKEN_TILE = 256
ROW_TILES = 8
ATTN_STACK = 2
PROJ_TILE = 512
DISPATCH_TILE = 512
EXPERT_ROWS = 512
VMEM_LIMIT = 56 * 1024 * 1024


def _cparams(sem):
    return pltpu.CompilerParams(dimension_semantics=sem, vmem_limit_bytes=VMEM_LIMIT)


def _const_spec(shape):
    nd = len(shape)
    return pl.BlockSpec(shape, lambda *_: (0,) * nd)


def _split_bf16(v, n):
    parts = []
    r = v
    for k in range(n):
        p = r.astype(BF16)
        parts.append(p)
        if k + 1 < n:
            r = r - p.astype(F32)
    return parts


def _mm01(v, m01, n=3):
    acc = None
    for p in _split_bf16(v, n):
        d = jnp.dot(p, m01, preferred_element_type=F32)
        acc = d if acc is None else acc + d
    return acc


def _mm01_left(m01, v, n=3):
    acc = None
    for p in _split_bf16(v, n):
        d = jnp.dot(m01, p, preferred_element_type=F32)
        acc = d if acc is None else acc + d
    return acc


def _dot_nt(a, b):
    return lax.dot_general(a, b, (((1,), (1,)), ((), ())), preferred_element_type=F32)


def _sigmoid(x):
    return 0.5 * jnp.tanh(0.5 * x) + 0.5


def _silu(x):
    return x * _sigmoid(x)


def _softplus(x):
    return jnp.maximum(x, 0.0) + jnp.log1p(jnp.exp(-jnp.abs(x)))


def _rms_scale(x):
    return x * lax.rsqrt(jnp.mean(x * x, axis=-1, keepdims=True) + EPS)


def _gate_norm(y, z, gn, n_groups):
    yz = y * _silu(z)
    w = yz.shape[-1] // n_groups
    outs = []
    for g in range(n_groups):
        v = yz[:, g * w:(g + 1) * w]
        outs.append(_rms_scale(v) * gn[:, g * w:(g + 1) * w])
    return jnp.concatenate(outs, axis=1)


def _rope(x, c, s1, s2):
    w = x.shape[-1]
    return x * c + pltpu.roll(x, w - ROT_DIM // 2, 1) * s1 + pltpu.roll(x, ROT_DIM // 2, 1) * s2


def _tile_lanes(t, reps):
    return t if reps == 1 else jnp.concatenate([t] * reps, axis=1)


def _inproj_kernel(xp_ref, xs_ref, g_ref, wz_ref, wx_ref, wd_ref, cw_ref, cb_ref, z_ref, xbc_ref, dt_ref, tail_ref,
                   xpad_sc, *, n_p, tiles_per_seq):
    i = pl.program_id(0)
    tm = z_ref.shape[0]
    pad = SUBLANES
    @pl.when(i == 0)
    def _():
        xpad_sc[...] = jnp.zeros(xpad_sc.shape, F32)

    x = jnp.where(i < n_p, xp_ref[...], xs_ref[...])
    xn = (_rms_scale(x) * g_ref[...]).astype(BF16)
    xbc = jnp.dot(xn, wx_ref[...], preferred_element_type=F32)
    first = i % tiles_per_seq == 0
    xpad_sc[0:pad, :] = jnp.where(first, 0.0, xpad_sc[tm:tm + pad, :])
    xpad_sc[pad:pad + tm, :] = xbc
    acc = cb_ref[...]
    for k in range(SSM_CONV):
        off = pad - (SSM_CONV - 1) + k
        acc = acc + xpad_sc[off:off + tm, :] * cw_ref[k:k + 1, :]
    xbc_ref[...] = jnp.where(i < n_p, _silu(acc), xbc)
    z_ref[...] = jnp.dot(xn, wz_ref[...], preferred_element_type=F32)
    dt_ref[...] = jnp.dot(xn, wd_ref[...], preferred_element_type=F32)

    @pl.when((i < n_p) & (i % tiles_per_seq == tiles_per_seq - 1))
    def _():
        tail_ref[0] = xbc[tm - pad:tm, :]


def _inproj(xp2, xs2, g, wz, wx, wd, cw, cb, seq):
    tp, d = xp2.shape
    ts = xs2.shape[0]
    tm = TOKEN_TILE
    n_p, n_s = tp // tm, ts // tm
    t = tp + ts
    tiles_per_seq = seq // tm
    n_seq = tp // seq
    cdim = wx.shape[1]
    return pl.pallas_call(
        functools.partial(_inproj_kernel, n_p=n_p, tiles_per_seq=tiles_per_seq),
        grid=(n_p + n_s,),
        in_specs=_two_source_specs(tm, d, n_p) + [_const_spec(a.shape) for a in (g, wz, wx, wd, cw, cb)],
        out_specs=[
            pl.BlockSpec((tm, wz.shape[1]), lambda i: (i, 0)),
            pl.BlockSpec((tm, cdim), lambda i: (i, 0)),
            pl.BlockSpec((tm, wd.shape[1]), lambda i: (i, 0)),
            pl.BlockSpec((1, SUBLANES, cdim), lambda i: (jnp.minimum(i // tiles_per_seq, n_seq - 1), 0, 0)),
        ],
        out_shape=[jax.ShapeDtypeStruct((t, wz.shape[1]), F32),
                   jax.ShapeDtypeStruct((t, cdim), F32),
                   jax.ShapeDtypeStruct((t, wd.shape[1]), F32),
                   jax.ShapeDtypeStruct((n_seq, SUBLANES, cdim), F32)],
        scratch_shapes=[pltpu.VMEM((tm + 2 * SUBLANES, cdim), F32)],
        compiler_params=_cparams(("arbitrary",)),
        name="inproj",
    )(xp2, xs2, g, wz, wx, wd, cw, cb)


def _ssd_prompt_kernel(z_ref, xc_ref, dt_ref, x_ref, dtb_ref, alog_ref, dsk_ref, gn_ref,
                       tril_ref, e_ref, wout_ref, h_ref, sfin_ref, st_sc):
    c = pl.program_id(1)
    q = SSM_CHUNK
    di = z_ref.shape[1]
    gn_w = SSM_GROUPS * SSM_STATE
    hpg = di // SSM_GROUPS

    @pl.when(c == 0)
    def _():
        st_sc[...] = jnp.zeros(st_sc.shape, F32)

    xs = xc_ref[:, :di]
    bm = xc_ref[:, di:di + gn_w]
    cm = xc_ref[:, di + gn_w:]

    dt = _softplus(dt_ref[...] + dtb_ref[...])
    a = -jnp.exp(alog_ref[...])
    act = _mm01_left(tril_ref[...], dt * a)
    act_t = act.T
    act_last = act[q - 1:q, :]
    e01 = e_ref[...]
    xdt = xs * _mm01(dt, e01, 2)
    xd = xdt * _mm01(jnp.exp(act_last - act), e01, 2)
    eax = _mm01(jnp.exp(act), e01, 2)
    cdx = _mm01(jnp.exp(act[q - SUBLANES:q, :]), e01)[SUBLANES - 1:SUBLANES, :]

    row = lax.broadcasted_iota(I32, (q, q), 0)
    col = lax.broadcasted_iota(I32, (q, q), 1)
    causal = row >= col
    lane = lax.broadcasted_iota(I32, (q, LANES), 1)
    lo_half = lane < SSM_HEAD_DIM

    y_parts = []
    heads_per_group = hpg // SSM_HEAD_DIM
    for g in range(SSM_GROUPS):
        cg = cm[:, g * SSM_STATE:(g + 1) * SSM_STATE].astype(BF16)
        bg = bm[:, g * SSM_STATE:(g + 1) * SSM_STATE]
        cb = _dot_nt(cg, bg.astype(BF16))
        st_g = st_sc[:, g * hpg:(g + 1) * hpg]
        y_off = jnp.dot(cg, st_g.astype(BF16), preferred_element_type=F32)
        for pr in range(heads_per_group // 2):
            h0 = g * heads_per_group + 2 * pr
            ms = []
            for h in (h0, h0 + 1):
                seg = act[:, h:h + 1] - act_t[h:h + 1, :]
                lm = jnp.exp(jnp.where(causal, seg, -jnp.inf))
                ms.append((cb * lm).astype(BF16))
            m2 = jnp.concatenate(ms, axis=1)
            xpair = xdt[:, h0 * SSM_HEAD_DIM:(h0 + 2) * SSM_HEAD_DIM]
            rhs = jnp.concatenate([jnp.where(lo_half, xpair, 0.0),
                                   jnp.where(lo_half, 0.0, xpair)], axis=0).astype(BF16)
            y_d = jnp.dot(m2, rhs, preferred_element_type=F32)
            lo = 2 * pr * SSM_HEAD_DIM
            y_parts.append(y_d + y_off[:, lo:lo + LANES] * eax[:, g * hpg + lo:g * hpg + lo + LANES])
        upd = jnp.dot(bg.T.astype(BF16), xd[:, g * hpg:(g + 1) * hpg].astype(BF16),
                      preferred_element_type=F32)
        st_sc[:, g * hpg:(g + 1) * hpg] = st_g * cdx[:, g * hpg:(g + 1) * hpg] + upd

    y = jnp.concatenate(y_parts, axis=1) + xs * dsk_ref[...]
    yg = _gate_norm(y, z_ref[...], gn_ref[...], SSM_GROUPS)
    h_ref[...] = x_ref[...] + jnp.dot(yg.astype(BF16), wout_ref[...], preferred_element_type=F32)

    @pl.when(c == pl.num_programs(1) - 1)
    def _():
        sfin_ref[0] = st_sc[...].T


def _ssd_prompt(z, xc, dt, x, dtb, alog, dsk, gnw, tril, e01, wout, bp, seq):
    nc = seq // SSM_CHUNK
    q = SSM_CHUNK
    di, cd, d = z.shape[1], xc.shape[1], x.shape[1]
    rows = lambda b, c: (b * nc + c, 0)
    return pl.pallas_call(
        _ssd_prompt_kernel,
        grid=(bp, nc),
        in_specs=[
            pl.BlockSpec((q, di), rows), pl.BlockSpec((q, cd), rows), pl.BlockSpec((q, LANES), rows),
            pl.BlockSpec((q, d), rows),
            _const_spec(dtb.shape), _const_spec(alog.shape),
            _const_spec(dsk.shape), _const_spec(gnw.shape), _const_spec(tril.shape), _const_spec(e01.shape),
            _const_spec(wout.shape),
        ],
        out_specs=[
            pl.BlockSpec((q, d), rows),
            pl.BlockSpec((1, di, SSM_STATE), lambda b, c: (b, 0, 0)),
        ],
        out_shape=[jax.ShapeDtypeStruct((bp * seq, d), F32),
                   jax.ShapeDtypeStruct((bp, di, SSM_STATE), F32)],
        scratch_shapes=[pltpu.VMEM((SSM_STATE, di), F32)],
        compiler_params=_cparams(("arbitrary", "arbitrary")),
        name="ssd_prompt",
    )(z, xc, dt, x, dtb, alog, dsk, gnw, tril, e01, wout)


def _ssd_sample_kernel(z_ref, xp_ref, dt_ref, s0_ref, x_ref, cw_ref, cb_ref, dtb_ref, alog_ref, dsk_ref, gn_ref,
                       e_ref, g1_ref, wout_ref, h_ref, sn_ref, yoff_sc, *, n_tok):
    hf = pl.program_id(1)
    nb = SEQ_PER_STEP
    half = nb // 2
    q = n_tok * nb
    di = z_ref.shape[1]
    gn_w = SSM_GROUPS * SSM_STATE
    hpg = di // SSM_GROUPS

    taps = [xp_ref[0, m] for m in range(n_tok + SSM_CONV - 1)]
    slabs = []
    for t in range(n_tok):
        acc = cb_ref[...]
        for k in range(SSM_CONV):
            acc = acc + taps[t + k] * cw_ref[k:k + 1, :]
        slabs.append(_silu(acc))
    xc = jnp.concatenate(slabs, axis=0)
    xs = xc[:, :di]
    bm = xc[:, di:di + gn_w]
    cm = xc[:, di + gn_w:]

    dt = _softplus(dt_ref[...] + dtb_ref[...])
    da = dt * (-jnp.exp(alog_ref[...]))
    acts = []
    run = None
    for t in range(n_tok):
        d = da[t * nb:(t + 1) * nb, :]
        run = d if run is None else run + d
        acts.append(run)
    act = jnp.concatenate(acts, axis=0)
    act_last = jnp.concatenate([acts[-1]] * n_tok, axis=0)
    e01 = e_ref[...]
    xdt = xs * _mm01(dt, e01)
    xd = xdt * _mm01(jnp.exp(act_last - act), e01)
    eax = _mm01(jnp.exp(act), e01)
    cdx = _mm01(jnp.exp(acts[-1]), e01)

    pairs = [(t, u) for t in range(n_tok) for u in range(t + 1)]
    cbp = jnp.concatenate([cm[t * nb:(t + 1) * nb, :] * bm[u * nb:(u + 1) * nb, :] for t, u in pairs], axis=0)
    seg = jnp.concatenate([acts[t] - acts[u] for t, u in pairs], axis=0)
    coef = _mm01(_mm01(cbp, g1_ref[...]) * jnp.exp(seg), e01)
    y_slabs = []
    for t in range(n_tok):
        acc = None
        for pi, (tt, u) in enumerate(pairs):
            if tt != t:
                continue
            term = coef[pi * nb:(pi + 1) * nb, :] * xdt[u * nb:(u + 1) * nb, :]
            acc = term if acc is None else acc + term
        y_slabs.append(acc)
    y_diag = jnp.concatenate(y_slabs, axis=0)

    zpad = jnp.concatenate([xd,
                            jnp.where(hf == 0, cdx[0:half, :], cdx[half:nb, :]),
                            jnp.zeros((LANES - q - half, di), F32)], axis=0)
    zt = zpad.T
    row_seq = lax.broadcasted_iota(I32, (q, 1), 0) % nb
    cm_b = cm.astype(BF16)
    y_off_g = [None] * SSM_GROUPS
    for sl in range(half):
        in_seq = row_seq == hf * half + sl
        for g in range(SSM_GROUPS):
            s_old = s0_ref[sl, g * hpg:(g + 1) * hpg, :]
            c_g = jnp.where(in_seq, cm_b[:, g * SSM_STATE:(g + 1) * SSM_STATE], jnp.zeros((), BF16))
            yo = _dot_nt(c_g, s_old.astype(BF16))
            y_off_g[g] = yo if y_off_g[g] is None else y_off_g[g] + yo
            b_g = jnp.where(in_seq, bm[:, g * SSM_STATE:(g + 1) * SSM_STATE], 0.0)
            b_pad = jnp.concatenate([b_g, jnp.zeros((LANES - q, SSM_STATE), F32)], axis=0).astype(BF16)
            zt_g = zt[g * hpg:(g + 1) * hpg, :]
            upd = jnp.dot(zt_g.astype(BF16), b_pad, preferred_element_type=F32)
            decay = zt_g[:, q + sl:q + sl + 1]
            sn_ref[sl, g * hpg:(g + 1) * hpg, :] = s_old * decay + upd
    y_off = jnp.concatenate(y_off_g, axis=1)

    @pl.when(hf == 0)
    def _():
        yoff_sc[...] = y_off

    @pl.when(hf == 1)
    def _():
        y = y_diag + (yoff_sc[...] + y_off) * eax + xs * dsk_ref[...]
        yg = _gate_norm(y, z_ref[...], gn_ref[...], SSM_GROUPS)
        h_ref[...] = x_ref[...] + jnp.dot(yg.astype(BF16), wout_ref[...], preferred_element_type=F32)


def _ssd_sample(z, xp7, dt, s0, x, cw, cb, dtb, alog, dsk, gnw, e01, g1, wout, tp, n_tok):
    nsteps = xp7.shape[0]
    nb = SEQ_PER_STEP
    half = nb // 2
    q = n_tok * nb
    di, d = z.shape[1], x.shape[1]
    cd = xp7.shape[3]
    base = tp // q
    rows = lambda s, hf: (base + s, 0)
    return pl.pallas_call(
        functools.partial(_ssd_sample_kernel, n_tok=n_tok),
        grid=(nsteps, 2),
        in_specs=[
            pl.BlockSpec((q, di), rows),
            pl.BlockSpec((1, n_tok + SSM_CONV - 1, nb, cd), lambda s, hf: (s, 0, 0, 0)),
            pl.BlockSpec((q, LANES), rows),
            pl.BlockSpec((half, di, SSM_STATE), lambda s, hf: (2 * s + hf, 0, 0)),
            pl.BlockSpec((q, d), lambda s, hf: (s, 0)),
            _const_spec(cw.shape), _const_spec(cb.shape), _const_spec(dtb.shape), _const_spec(alog.shape),
            _const_spec(dsk.shape), _const_spec(gnw.shape), _const_spec(e01.shape), _const_spec(g1.shape),
            _const_spec(wout.shape),
        ],
        out_specs=[
            pl.BlockSpec((q, d), lambda s, hf: (s, 0)),
            pl.BlockSpec((half, di, SSM_STATE), lambda s, hf: (2 * s + hf, 0, 0)),
        ],
        out_shape=[jax.ShapeDtypeStruct((nsteps * q, d), F32),
                   jax.ShapeDtypeStruct(s0.shape, F32)],
        scratch_shapes=[pltpu.VMEM((q, di), F32)],
        compiler_params=_cparams(("arbitrary", "arbitrary")),
        name="ssd_sample",
    )(z, xp7, dt, s0, x, cw, cb, dtb, alog, dsk, gnw, e01, g1, wout)


def _qkv_kernel(hp_ref, hs_ref, kvn_ref, an_ref, wkv_ref, wq_ref, kn_ref, qn_ref, rc_ref, rs1_ref, rs2_ref,
                hsum_ref, rq_ref, eq_ref, q_ref, k_ref, v_ref, *, n_p):
    hn = _rms_scale(jnp.where(pl.program_id(0) < n_p, hp_ref[...], hs_ref[...]))
    kvw = k_ref.shape[1]
    kv = jnp.dot((hn * kvn_ref[...]).astype(BF16), wkv_ref[...], preferred_element_type=F32)
    k = kv[:, :kvw]
    v_ref[...] = kv[:, kvw:]
    inv_hd = 1.0 / ATT_HEAD_DIM
    k = k * lax.rsqrt(_mm01(k * k, hsum_ref[...], 2) * inv_hd + EPS) * kn_ref[...]
    rc, rs1, rs2 = rc_ref[...], rs1_ref[...], rs2_ref[...]
    rk = kvw // LANES
    k_ref[...] = _rope(k, _tile_lanes(rc, rk), _tile_lanes(rs1, rk), _tile_lanes(rs2, rk))
    q = jnp.dot((hn * an_ref[...]).astype(BF16), wq_ref[...], preferred_element_type=F32)
    rsq = lax.rsqrt(_mm01(q * q, rq_ref[...], 2) * inv_hd + EPS)
    q = q * _mm01(rsq, eq_ref[...], 2) * qn_ref[...]
    rq = q.shape[1] // LANES
    q_ref[...] = _rope(q, _tile_lanes(rc, rq), _tile_lanes(rs1, rq), _tile_lanes(rs2, rq))


def _qkv(h_p, h_s, kvn, an, wkv, wq, knt, qnt, rc, rs1, rs2, hsum, rq, eq):
    d = h_p.shape[1]
    tm = TOKEN_TILE
    n_p = h_p.shape[0] // tm
    t = h_p.shape[0] + h_s.shape[0]
    kvw = wkv.shape[1] // 2
    qw = wq.shape[1]
    rows = lambda i: (i, 0)
    return pl.pallas_call(
        functools.partial(_qkv_kernel, n_p=n_p),
        grid=(t // tm,),
        in_specs=_two_source_specs(tm, d, n_p) + [_const_spec(a.shape) for a in (kvn, an, wkv, wq, knt, qnt)]
        + [pl.BlockSpec((tm, LANES), rows)] * 3 + [_const_spec(a.shape) for a in (hsum, rq, eq)],
        out_specs=[pl.BlockSpec((tm, qw), rows), pl.BlockSpec((tm, kvw), rows), pl.BlockSpec((tm, kvw), rows)],
        out_shape=[jax.ShapeDtypeStruct((t, qw), F32), jax.ShapeDtypeStruct((t, kvw), F32),
                   jax.ShapeDtypeStruct((t, kvw), F32)],
        compiler_params=_cparams(("arbitrary",)),
        name="qkv",
    )(h_p, h_s, kvn, an, wkv, wq, knt, qnt, rc, rs1, rs2, hsum, rq, eq)


def _attn_prompt_kernel(sink_ref, q_ref, kc_ref, kp_ref, vc_ref, vp_ref, o_ref):
    i = pl.program_id(1)
    w = WINDOW
    hd = ATT_HEAD_DIM
    n_q = q_ref.shape[1] // hd
    grp = n_q // ATT_KV_HEADS
    stack = ATTN_STACK
    rows = stack * w
    row = lax.broadcasted_iota(I32, (rows, 2 * w), 0) % w
    col = lax.broadcasted_iota(I32, (rows, 2 * w), 1)
    dist = row + w - col
    mask = (dist >= 0) & (dist < w) & ((col >= w) | (i > 0))
    r_head = lax.broadcasted_iota(I32, (rows, 1), 0) // w
    q = q_ref[...] * (hd ** -0.5)
    outs = [None] * n_q
    for g in range(ATT_KV_HEADS):
        sl = slice(g * hd, (g + 1) * hd)
        kk = jnp.concatenate([kp_ref[:, sl], kc_ref[:, sl]], axis=0).astype(BF16)
        vv = jnp.concatenate([vp_ref[:, sl], vc_ref[:, sl]], axis=0).astype(BF16)
        for h0 in range(g * grp, (g + 1) * grp, stack):
            qs = jnp.concatenate([q[:, h * hd:(h + 1) * hd] for h in range(h0, h0 + stack)],
                                 axis=0).astype(BF16)
            s = jnp.where(mask, _dot_nt(qs, kk), -jnp.inf)
            sink = jnp.zeros((rows, 1), F32)
            for j in range(stack):
                sink = jnp.where(r_head == j, sink_ref[h0 + j], sink)
            m = jnp.maximum(jnp.max(s, axis=-1, keepdims=True), sink)
            p = jnp.exp(s - m)
            denom = jnp.sum(p, axis=-1, keepdims=True) + jnp.exp(sink - m)
            o = jnp.dot(p.astype(BF16), vv, preferred_element_type=F32) * (1.0 / denom)
            for j in range(stack):
                outs[h0 + j] = o[j * w:(j + 1) * w, :]
    o_ref[...] = jnp.concatenate(outs, axis=1)


def _attn_prompt(sinks, q, k, v, bp, seq):
    w = WINDOW
    nb = seq // w
    qw, kvw = q.shape[1], k.shape[1]
    cur = lambda b, i: (b * nb + i, 0)
    prev = lambda b, i: (b * nb + jnp.maximum(i - 1, 0), 0)
    return pl.pallas_call(
        _attn_prompt_kernel,
        grid=(bp, nb),
        in_specs=[pl.BlockSpec(memory_space=pltpu.SMEM),
                  pl.BlockSpec((w, qw), cur), pl.BlockSpec((w, kvw), cur), pl.BlockSpec((w, kvw), prev),
                  pl.BlockSpec((w, kvw), cur), pl.BlockSpec((w, kvw), prev)],
        out_specs=pl.BlockSpec((w, qw), cur),
        out_shape=jax.ShapeDtypeStruct((bp * seq, qw), F32),
        compiler_params=_cparams(("arbitrary", "arbitrary")),
        name="attn_prompt",
    )(sinks, q, k, k, v, v)


def _attn_sample_kernel(sink_ref, q_ref, kn_ref, vn_ref, kc_ref, vc_ref, o_ref, *, n_tok):
    nb = SEQ_PER_STEP
    qn = n_tok * nb
    hd = ATT_HEAD_DIM
    lc = kc_ref.shape[1]
    n_q = q_ref.shape[1] // hd
    grp = n_q // ATT_KV_HEADS
    rows = grp * qn
    r = lax.broadcasted_iota(I32, (rows, 1), 0)
    r_seq = r % nb
    r_tok = (r % qn) // nb
    ccol = lax.broadcasted_iota(I32, (rows, lc), 1)
    mask_c = ccol >= r_tok + 1 + (lc - WINDOW)
    ncol = lax.broadcasted_iota(I32, (rows, LANES), 1)
    mask_n = (ncol < qn) & (ncol % nb == r_seq) & (ncol // nb <= r_tok)
    q = q_ref[...]
    zpad = jnp.zeros((LANES - qn, hd), F32)
    outs = [None] * n_q
    for g in range(ATT_KV_HEADS):
        sl = slice(g * hd, (g + 1) * hd)
        qs = jnp.concatenate([q[:, (g * grp + hq) * hd:(g * grp + hq + 1) * hd] for hq in range(grp)],
                             axis=0).astype(BF16)
        k_new = jnp.concatenate([kn_ref[:, sl], zpad], axis=0).astype(BF16)
        v_new = jnp.concatenate([vn_ref[:, sl], zpad], axis=0).astype(BF16)
        s_n = jnp.where(mask_n, _dot_nt(qs, k_new) * (hd ** -0.5), -jnp.inf)
        s_c = jnp.zeros((rows, lc), F32)
        for sq in range(nb):
            s_sq = _dot_nt(qs, kc_ref[sq, :, sl].astype(BF16))
            s_c = jnp.where(r_seq == sq, s_sq, s_c)
        s_c = jnp.where(mask_c, s_c * (hd ** -0.5), -jnp.inf)
        sink = jnp.zeros((rows, 1), F32)
        for hq in range(grp):
            sink = jnp.where(r // qn == hq, sink_ref[g * grp + hq], sink)
        m = jnp.maximum(jnp.maximum(jnp.max(s_c, axis=-1, keepdims=True),
                                    jnp.max(s_n, axis=-1, keepdims=True)), sink)
        p_c = jnp.exp(s_c - m)
        p_n = jnp.exp(s_n - m)
        denom = (jnp.sum(p_c, axis=-1, keepdims=True) + jnp.sum(p_n, axis=-1, keepdims=True)
                 + jnp.exp(sink - m))
        p_c = p_c / denom
        o = jnp.dot((p_n / denom).astype(BF16), v_new, preferred_element_type=F32)
        for sq in range(nb):
            o = o + jnp.dot(jnp.where(r_seq == sq, p_c, 0.0).astype(BF16), vc_ref[sq, :, sl].astype(BF16),
                            preferred_element_type=F32)
        for hq in range(grp):
            outs[g * grp + hq] = o[hq * qn:(hq + 1) * qn, :]
    o_ref[...] = jnp.concatenate(outs, axis=1)


def _attn_sample(sinks, q, k, v, kc, vc, tp, n_tok):
    nb = SEQ_PER_STEP
    qn = n_tok * nb
    nsteps = kc.shape[0] // nb
    lc, kvw = kc.shape[1], kc.shape[2]
    qw = q.shape[1]
    base = tp // qn
    rows = lambda s: (base + s, 0)
    return pl.pallas_call(
        functools.partial(_attn_sample_kernel, n_tok=n_tok),
        grid=(nsteps,),
        in_specs=[pl.BlockSpec(memory_space=pltpu.SMEM),
                  pl.BlockSpec((qn, qw), rows), pl.BlockSpec((qn, kvw), rows), pl.BlockSpec((qn, kvw), rows),
                  pl.BlockSpec((nb, lc, kvw), lambda s: (s, 0, 0)), pl.BlockSpec((nb, lc, kvw), lambda s: (s, 0, 0))],
        out_specs=pl.BlockSpec((qn, qw), lambda s: (s, 0)),
        out_shape=jax.ShapeDtypeStruct((nsteps * qn, qw), F32),
        compiler_params=_cparams(("arbitrary",)),
        name="attn_sample",
    )(sinks, q, k, v, kc, vc)


def _proj_res_kernel(a_ref, w_ref, r_ref, o_ref):
    o_ref[...] = r_ref[...] + jnp.dot(a_ref[...].astype(BF16), w_ref[...], preferred_element_type=F32)


def _proj_res(a, w, res):
    t, k = a.shape
    n = w.shape[1]
    tm = min(PROJ_TILE, t)
    rows = lambda i: (i, 0)
    return pl.pallas_call(
        _proj_res_kernel,
        grid=(t // tm,),
        in_specs=[pl.BlockSpec((tm, k), rows), _const_spec(w.shape), pl.BlockSpec((tm, n), rows)],
        out_specs=pl.BlockSpec((tm, n), rows),
        out_shape=jax.ShapeDtypeStruct((t, n), F32),
        compiler_params=_cparams(("arbitrary",)),
        name="proj_res",
    )(a, w, res)


def _router_kernel(hp_ref, hs_ref, fn_ref, wh_ref, wl_ref, b_ref, tril_ref, xn_ref, info_ref, info_t_ref, cnt_ref,
                   carry_sc, *, n_p):
    i = pl.program_id(0)

    @pl.when(i == 0)
    def _():
        carry_sc[...] = jnp.zeros(carry_sc.shape, F32)

    xn = _rms_scale(jnp.where(i < n_p, hp_ref[...], hs_ref[...])) * fn_ref[...]
    _store_token_major(xn_ref, xn)
    x_hi, x_lo = _split_bf16(xn, 2)
    wh, wl = wh_ref[...], wl_ref[...]
    logits = (jnp.dot(x_hi, wh, preferred_element_type=F32) + jnp.dot(x_hi, wl, preferred_element_type=F32)
              + jnp.dot(x_lo, wh, preferred_element_type=F32)) + b_ref[...]
    tm = logits.shape[0]
    lane = lax.broadcasted_iota(I32, (tm, LANES), 1).astype(F32)
    big = float(LANES)
    neg = -jnp.inf

    is_grp = (lane >= MOE_EXPERTS) & (lane < MOE_EXPERTS + MOE_GROUPS)
    lg = jnp.where(is_grp, logits, neg)
    mg = jnp.max(lg, axis=-1, keepdims=True)
    gp = 1.0 / jnp.sum(jnp.exp(lg - mg), axis=-1, keepdims=True)
    gi = jnp.min(jnp.where(lg == mg, lane, big), axis=-1, keepdims=True) - MOE_EXPERTS

    lo = gi * MOE_EXPERTS_PER_GROUP
    le = jnp.where((lane >= lo) & (lane < lo + MOE_EXPERTS_PER_GROUP), logits, neg)
    m1 = jnp.max(le, axis=-1, keepdims=True)
    i1 = jnp.min(jnp.where(le == m1, lane, big), axis=-1, keepdims=True)
    le2 = jnp.where(lane == i1, neg, le)
    m2 = jnp.max(le2, axis=-1, keepdims=True)
    i2 = jnp.min(jnp.where(le2 == m2, lane, big), axis=-1, keepdims=True)
    e2 = jnp.exp(m2 - m1)
    g1 = gp * (1.0 / (1.0 + e2))
    g2 = gp * (e2 / (1.0 + e2))

    a1 = lane == i1
    a2 = lane == i2
    onehot = jnp.where(a1 | a2, 1.0, 0.0)
    before = jnp.dot(tril_ref[...], onehot.astype(BF16), preferred_element_type=F32) + carry_sc[...]
    r1 = jnp.sum(jnp.where(a1, before, 0.0), axis=-1, keepdims=True)
    r2 = jnp.sum(jnp.where(a2, before, 0.0), axis=-1, keepdims=True)
    carry_sc[...] = carry_sc[...] + jnp.sum(onehot, axis=0, keepdims=True)
    cnt_ref[...] = carry_sc[...]

    cols = (i1, i2, g1, g2, r1, r2)
    info = jnp.zeros((tm, LANES), F32)
    for k, cval in enumerate(cols):
        info = jnp.where(lane == k, cval, info)
    info_ref[...] = info
    info_t_ref[...] = info.T[0:SUBLANES, :]


def _two_source_specs(tm, width, n_p):
    return [pl.BlockSpec((tm, width), lambda i: (jnp.minimum(i, n_p - 1), 0)),
            pl.BlockSpec((tm, width), lambda i: (jnp.maximum(i - n_p, 0), 0))]


def _router(h_p, h_s, fn, wh, wl, b, tril):
    d = h_p.shape[1]
    tm = TOKEN_TILE
    n_p = h_p.shape[0] // tm
    t = h_p.shape[0] + h_s.shape[0]
    rows = lambda i: (i, 0)
    return pl.pallas_call(
        functools.partial(_router_kernel, n_p=n_p),
        grid=(t // tm,),
        in_specs=_two_source_specs(tm, d, n_p) + [_const_spec(a.shape) for a in (fn, wh, wl, b, tril)],
        out_specs=[pl.BlockSpec((tm * ROW_TILES, LANES), rows), pl.BlockSpec((tm, LANES), rows),
                   pl.BlockSpec((SUBLANES, tm), lambda i: (0, i)), _const_spec((1, LANES))],
        out_shape=[jax.ShapeDtypeStruct((t * ROW_TILES, LANES), F32), jax.ShapeDtypeStruct((t, LANES), F32),
                   jax.ShapeDtypeStruct((SUBLANES, t), F32), jax.ShapeDtypeStruct((1, LANES), F32)],
        scratch_shapes=[pltpu.VMEM((1, LANES), F32)],
        compiler_params=_cparams(("arbitrary",)),
        name="moe_router",
    )(h_p, h_s, fn, wh, wl, b, tril)


def _store_token_major(ref, x):
    n = x.shape[0]
    for j in range(ROW_TILES):
        ref[pl.ds(j, n, stride=ROW_TILES), :] = x[:, j * LANES:(j + 1) * LANES]


def _load_token_major(ref, n):
    return jnp.concatenate([ref[pl.ds(j, n, stride=ROW_TILES), :] for j in range(ROW_TILES)], axis=1)


def _dest_kernel(pst_ref, info_ref, dest_ref):
    info = info_ref[...]
    e = info[0:2, :]
    start = jnp.zeros(e.shape, F32)
    for k in range(MOE_EXPERTS):
        start = jnp.where(e == k, pst_ref[k].astype(F32), start)
    dest = (start + info[4:6, :]).astype(I32)
    dest_ref[...] = jnp.concatenate([dest, jnp.zeros((SUBLANES - 2, dest.shape[1]), I32)], axis=0)


def _dest(pstarts, info_t):
    return pl.pallas_call(
        _dest_kernel,
        in_specs=[pl.BlockSpec(memory_space=pltpu.SMEM), pl.BlockSpec(memory_space=pltpu.VMEM)],
        out_specs=pl.BlockSpec(memory_space=pltpu.VMEM),
        out_shape=jax.ShapeDtypeStruct(info_t.shape, I32),
        name="moe_dest",
    )(pstarts, info_t)


def _tile_copy(src, src_row, dst, dst_row, sem):
    return pltpu.make_async_copy(src.at[pl.ds(pl.multiple_of(src_row * ROW_TILES, ROW_TILES), ROW_TILES)],
                                 dst.at[pl.ds(pl.multiple_of(dst_row * ROW_TILES, ROW_TILES), ROW_TILES)], sem)


def _wait_tiles(ref, n_tokens, sem):
    blk = ref.at[pl.ds(0, n_tokens * ROW_TILES)]
    pltpu.make_async_copy(blk, blk, sem).wait()


def _dispatch_kernel(d1_ref, d2_ref, pend_ref, pcnt_ref, xn_ref, xbuf_hbm, zero_sc, sem_z, sem):
    i = pl.program_id(0)
    tm = xn_ref.shape[0] // ROW_TILES
    blk_rows = zero_sc.shape[0]

    @pl.when(i == 0)
    def _():
        zero_sc[...] = jnp.zeros(zero_sc.shape, F32)

        def zero_copy(e):
            start = pl.multiple_of(pend_ref[e] * ROW_TILES - blk_rows, ROW_TILES)
            return pltpu.make_async_copy(zero_sc, xbuf_hbm.at[pl.ds(start, blk_rows)], sem_z)

        for e in range(MOE_EXPERTS):
            @pl.when(pcnt_ref[e] > 0)
            def _():
                zero_copy(e).start()
        first_unused = pend_ref[MOE_EXPERTS - 1] * ROW_TILES // blk_rows
        n_blocks = xbuf_hbm.shape[0] // blk_rows

        def tail_copy(b):
            dst = xbuf_hbm.at[pl.ds(pl.multiple_of(b * blk_rows, blk_rows), blk_rows)]
            return pltpu.make_async_copy(zero_sc, dst, sem_z)

        def tail_start(b, carry):
            tail_copy(b).start()
            return carry

        def tail_wait(b, carry):
            tail_copy(b).wait()
            return carry
        lax.fori_loop(first_unused, n_blocks, tail_start, 0)
        for e in range(MOE_EXPERTS):
            @pl.when(pcnt_ref[e] > 0)
            def _():
                zero_copy(e).wait()
        lax.fori_loop(first_unused, n_blocks, tail_wait, 0)

    def body(r, carry):
        t = i * tm + r
        _tile_copy(xn_ref, r, xbuf_hbm, d1_ref[t], sem).start(priority=0)
        _tile_copy(xn_ref, r, xbuf_hbm, d2_ref[t], sem).start(priority=1)
        return carry
    lax.fori_loop(0, tm, body, 0, unroll=8)
    _wait_tiles(xn_ref, tm, sem)
    _wait_tiles(xn_ref, tm, sem)


def _dispatch(dest1, dest2, pends, pcounts, xn_tm, n_rows):
    n_tok = dest1.shape[0]
    tm = DISPATCH_TILE
    grid_spec = pltpu.PrefetchScalarGridSpec(
        num_scalar_prefetch=4,
        grid=(n_tok // tm,),
        in_specs=[pl.BlockSpec((tm * ROW_TILES, LANES), lambda i, *_: (i, 0))],
        out_specs=pl.BlockSpec(memory_space=pl.ANY),
        scratch_shapes=[pltpu.VMEM((EXPERT_ROWS * ROW_TILES, LANES), F32), pltpu.SemaphoreType.DMA(()),
                        pltpu.SemaphoreType.DMA(())],
    )
    return pl.pallas_call(
        _dispatch_kernel,
        grid_spec=grid_spec,
        out_shape=jax.ShapeDtypeStruct((n_rows * ROW_TILES, LANES), F32),
        compiler_params=_cparams(("arbitrary",)),
        name="moe_dispatch",
    )(dest1, dest2, pends, pcounts, xn_tm)


def _expert_kernel(blk_e_ref, nvalid_ref, next_e_ref, x_ref, wg_hbm, wu_hbm, wd_hbm, y_ref,
                   wg_sc, wu_sc, wd_sc, wg_st, wu_st, wd_st, sem, *, layer):
    i = pl.program_id(0)
    nv = nvalid_ref[0]
    rows = x_ref.shape[0] // ROW_TILES

    def weight_copies(e):
        return (pltpu.make_async_copy(wg_hbm.at[layer, e], wg_st, sem.at[0]),
                pltpu.make_async_copy(wu_hbm.at[layer, e], wu_st, sem.at[1]),
                pltpu.make_async_copy(wd_hbm.at[layer, e], wd_st, sem.at[2]))

    @pl.when((i == 0) & (nv > 0))
    def _():
        for c in weight_copies(blk_e_ref[0]):
            c.start()

    @pl.when(i < nv)
    def _():
        e = blk_e_ref[i]
        e_prev = blk_e_ref[jnp.maximum(i - 1, 0)]

        @pl.when((i == 0) | (e != e_prev))
        def _():
            for c in weight_copies(e):
                c.wait()
            wg_sc[...] = wg_st[...].astype(BF16)
            wu_sc[...] = wu_st[...].astype(BF16)
            wd_sc[...] = wd_st[...].astype(BF16)
            nxt = next_e_ref[e]

            @pl.when(nxt < MOE_EXPERTS)
            def _():
                for c in weight_copies(nxt):
                    c.start()

        x = _load_token_major(x_ref, rows).astype(BF16)
        hid = _silu(jnp.dot(x, wg_sc[...], preferred_element_type=F32)) * jnp.dot(
            x, wu_sc[...], preferred_element_type=F32)
        _store_token_major(y_ref, jnp.dot(hid.astype(BF16), wd_sc[...], preferred_element_type=F32))

    @pl.when(i >= nv)
    def _():
        y_ref[...] = jnp.zeros(y_ref.shape, F32)


def _experts(blk_e, nvalid, next_e, xbuf, wg, wu, wd, layer):
    nblk = blk_e.shape[0]
    d, hdim = wg.shape[2], wg.shape[3]
    rows = EXPERT_ROWS * ROW_TILES
    grid_spec = pltpu.PrefetchScalarGridSpec(
        num_scalar_prefetch=3,
        grid=(nblk,),
        in_specs=[pl.BlockSpec((rows, LANES), lambda i, be, nv, ne: (jnp.minimum(i, nv[0] - 1), 0)),
                  pl.BlockSpec(memory_space=pl.ANY), pl.BlockSpec(memory_space=pl.ANY),
                  pl.BlockSpec(memory_space=pl.ANY)],
        out_specs=pl.BlockSpec((rows, LANES), lambda i, be, nv, ne: (i, 0)),
        scratch_shapes=[pltpu.VMEM((d, hdim), BF16), pltpu.VMEM((d, hdim), BF16), pltpu.VMEM((hdim, d), BF16),
                        pltpu.VMEM((d, hdim), F32), pltpu.VMEM((d, hdim), F32), pltpu.VMEM((hdim, d), F32),
                        pltpu.SemaphoreType.DMA((3,))],
    )
    return pl.pallas_call(
        functools.partial(_expert_kernel, layer=layer),
        grid_spec=grid_spec,
        out_shape=jax.ShapeDtypeStruct((nblk * rows, LANES), F32),
        compiler_params=_cparams(("arbitrary",)),
        name="moe_experts",
    )(blk_e, nvalid, next_e, xbuf, wg, wu, wd)


def _combine_kernel(d1_ref, d2_ref, h_ref, info_ref, y_hbm, o_ref, r_sc, sem, *, base_tile):
    i = pl.program_id(0)
    n = pl.num_programs(0)
    tm = h_ref.shape[0]

    def start(step):
        slot = step % 2
        base = (base_tile + step) * tm

        def body(r, carry):
            _tile_copy(y_hbm, d1_ref[base + r], r_sc.at[slot, 0], r, sem.at[slot]).start(priority=0)
            _tile_copy(y_hbm, d2_ref[base + r], r_sc.at[slot, 1], r, sem.at[slot]).start(priority=1)
            return carry
        lax.fori_loop(0, tm, body, 0, unroll=8)

    @pl.when(i == 0)
    def _():
        start(i)

    @pl.when(i + 1 < n)
    def _():
        start(i + 1)

    slot = i % 2
    _wait_tiles(r_sc.at[slot, 0], tm, sem.at[slot])
    _wait_tiles(r_sc.at[slot, 1], tm, sem.at[slot])
    info = info_ref[...]
    y1 = _load_token_major(r_sc.at[slot, 0], tm)
    y2 = _load_token_major(r_sc.at[slot, 1], tm)
    o_ref[...] = h_ref[...] + (y1 * info[:, 2:3] + y2 * info[:, 3:4])


def _combine(dest1, dest2, h, info, ybuf, row0):
    nrows, d = h.shape
    tm = TOKEN_TILE
    base_tile = row0 // tm
    own = lambda i, a, b: (i, 0)
    grid_spec = pltpu.PrefetchScalarGridSpec(
        num_scalar_prefetch=2,
        grid=(nrows // tm,),
        in_specs=[pl.BlockSpec((tm, d), own), pl.BlockSpec((tm, LANES), lambda i, a, b: (base_tile + i, 0)),
                  pl.BlockSpec(memory_space=pl.ANY)],
        out_specs=pl.BlockSpec((tm, d), own),
        scratch_shapes=[pltpu.VMEM((2, 2, tm * ROW_TILES, LANES), F32), pltpu.SemaphoreType.DMA((2,))],
    )
    return pl.pallas_call(
        functools.partial(_combine_kernel, base_tile=base_tile),
        grid_spec=grid_spec,
        out_shape=jax.ShapeDtypeStruct((nrows, d), F32),
        compiler_params=_cparams(("arbitrary",)),
        name="moe_combine",
    )(dest1, dest2, h, info, ybuf)


def _moe(h_p, h_s, fn, w_grp, b_grp, w_rt, b_rt, wg, wu, wd, layer, tril):
    d = h_p.shape[1]
    t = h_p.shape[0] + h_s.shape[0]
    w_cat = jnp.zeros((d, LANES), F32).at[:, :MOE_EXPERTS].set(w_rt).at[:, MOE_EXPERTS:MOE_EXPERTS + MOE_GROUPS].set(w_grp)
    b_cat = jnp.zeros((1, LANES), F32).at[0, :MOE_EXPERTS].set(b_rt).at[0, MOE_EXPERTS:MOE_EXPERTS + MOE_GROUPS].set(b_grp)
    w_hi = w_cat.astype(BF16)
    w_lo = (w_cat - w_hi.astype(F32)).astype(BF16)
    xn_tm, info, info_t, cnt = _router(h_p, h_s, fn.reshape(1, d), w_hi, w_lo, b_cat, tril)

    counts = cnt[0, :MOE_EXPERTS].astype(I32)
    pcounts = (counts + EXPERT_ROWS - 1) // EXPERT_ROWS * EXPERT_ROWS
    pends = jnp.cumsum(pcounts)
    pstarts = pends - pcounts
    nblk = -(-(2 * t + MOE_EXPERTS * (EXPERT_ROWS - 1)) // EXPERT_ROWS)
    blk_start = jnp.arange(nblk, dtype=I32) * EXPERT_ROWS
    blk_e = jnp.minimum(jnp.sum((pends[None, :] <= blk_start[:, None]).astype(I32), axis=1), MOE_EXPERTS - 1)
    nvalid = pends[-1:] // EXPERT_ROWS
    eid = jnp.arange(MOE_EXPERTS, dtype=I32)
    later = (eid[None, :] > eid[:, None]) & (pcounts[None, :] > 0)
    next_e = jnp.min(jnp.where(later, eid[None, :], MOE_EXPERTS), axis=1).astype(I32)

    dest = _dest(pstarts, info_t)
    dest1, dest2 = dest[0], dest[1]
    xbuf = _dispatch(dest1, dest2, pends, pcounts, xn_tm, nblk * EXPERT_ROWS)
    ybuf = _experts(blk_e, nvalid, next_e, xbuf, wg, wu, wd, layer)
    return (_combine(dest1, dest2, h_p, info, ybuf, 0), _combine(dest1, dest2, h_s, info, ybuf, h_p.shape[0]))


def _rope_tables(pos):
    half = ROT_DIM // 2
    inv = ROPE_THETA ** (-jnp.arange(0, ROT_DIM, 2, dtype=F32) / ROT_DIM)
    ang = pos.astype(F32)[:, None] * inv[None, :]
    cos, sin = jnp.cos(ang), jnp.sin(ang)
    n = pos.shape[0]
    ones = jnp.ones((n, ATT_HEAD_DIM - ROT_DIM), F32)
    zeros_r = jnp.zeros((n, ATT_HEAD_DIM - ROT_DIM), F32)
    zeros_h = jnp.zeros((n, half), F32)
    c = jnp.concatenate([cos, cos, ones], axis=1)
    s1 = jnp.concatenate([-sin, zeros_h, zeros_r], axis=1)
    s2 = jnp.concatenate([zeros_h, sin, zeros_r], axis=1)
    reps = LANES // ATT_HEAD_DIM
    return tuple(jnp.tile(a, (1, reps)) for a in (c, s1, s2))


def _to_step_order(a, nsteps, n_tok):
    c = a.shape[-1]
    return a.reshape(nsteps, SEQ_PER_STEP, n_tok, c).transpose(0, 2, 1, 3).reshape(nsteps * n_tok * SEQ_PER_STEP, c)


def _from_step_order(a, nsteps, n_tok):
    c = a.shape[-1]
    return a.reshape(nsteps, n_tok, SEQ_PER_STEP, c).transpose(0, 2, 1, 3).reshape(nsteps * SEQ_PER_STEP, n_tok, c)


def kernel(x_prompt, x_sample, state_ssm, state_conv, cache_k_win, cache_v_win, ssm_norm, ssm_w_in, ssm_conv_w, ssm_conv_b, ssm_dt_bias, ssm_a_log, ssm_d, ssm_gate_norm, ssm_w_out, kv_norm, w_kv, k_norm, attn_norm, w_q, q_norm, sinks, w_o, ffn_norm, moe_w_group, moe_b_group, moe_w_router, moe_b_router, moe_w_gate, moe_w_up, moe_w_down):
    bp, seq, d = x_prompt.shape
    bs, n_tok, _ = x_sample.shape
    tp, ts = bp * seq, bs * n_tok
    nsteps = bs // SEQ_PER_STEP
    n_heads = ssm_d.shape[1]
    di = n_heads * SSM_HEAD_DIM
    gn_w = SSM_GROUPS * SSM_STATE
    cdim = di + 2 * gn_w
    n_q = sinks.shape[1]
    kvw = ATT_KV_HEADS * ATT_HEAD_DIM

    xp2 = x_prompt.reshape(tp, d)
    xs2 = _to_step_order(x_sample, nsteps, n_tok)

    lane_i = jnp.arange(LANES)
    e01 = (lane_i[:, None] == (jnp.arange(di) // SSM_HEAD_DIM)[None, :]).astype(BF16)
    hpg = di // SSM_GROUPS
    g1 = ((jnp.arange(gn_w) // SSM_STATE)[:, None] == (lane_i // (hpg // SSM_HEAD_DIM))[None, :])
    g1 = (g1 & (lane_i < n_heads)[None, :]).astype(BF16)
    tril_c = (jnp.arange(SSM_CHUNK)[:, None] >= jnp.arange(SSM_CHUNK)[None, :]).astype(BF16)
    tril_x = (jnp.arange(TOKEN_TILE)[:, None] > jnp.arange(TOKEN_TILE)[None, :]).astype(BF16)
    hsum = ((jnp.arange(kvw) // ATT_HEAD_DIM)[:, None] == (jnp.arange(kvw) // ATT_HEAD_DIM)[None, :]).astype(BF16)
    qw = n_q * ATT_HEAD_DIM
    rq = ((jnp.arange(qw) // ATT_HEAD_DIM)[:, None] == lane_i[None, :]).astype(BF16)
    eq = rq.T

    w_in = ssm_w_in[0]
    wz = w_in[:, :di].astype(BF16)
    wx = w_in[:, di:di + cdim].astype(BF16)
    wd = jnp.zeros((d, LANES), F32).at[:, :n_heads].set(w_in[:, di + cdim:]).astype(BF16)
    cw, cb = ssm_conv_w[0], ssm_conv_b[0].reshape(1, cdim)
    z, xbc, dt, c_fin = _inproj(xp2, xs2, ssm_norm[0].reshape(1, d), wz, wx, wd, cw, cb, seq)

    pad_h = lambda v: jnp.zeros((1, LANES), F32).at[0, :n_heads].set(v)
    dtb, alog = pad_h(ssm_dt_bias[0]), pad_h(ssm_a_log[0])
    dsk = jnp.repeat(ssm_d[0], SSM_HEAD_DIM).reshape(1, di)
    gnw = ssm_gate_norm[0].reshape(1, di)

    w_out = ssm_w_out[0].astype(BF16)
    h_p, s_fin = _ssd_prompt(z, xbc, dt, xp2, dtb, alog, dsk, gnw, tril_c, e01, w_out, bp, seq)
    ssm_p = s_fin.reshape(1, bp, n_heads, SSM_HEAD_DIM, SSM_STATE)
    conv_p = c_fin[:, SUBLANES - (SSM_CONV - 1):, :].reshape(1, bp, SSM_CONV - 1, cdim)

    xbc_s = xbc[tp:].reshape(nsteps, n_tok, SEQ_PER_STEP, cdim)
    conv_in = state_conv[0].reshape(nsteps, SEQ_PER_STEP, SSM_CONV - 1, cdim).transpose(0, 2, 1, 3)
    xp7 = jnp.concatenate([conv_in, xbc_s], axis=1)
    s0 = state_ssm[0].reshape(bs, di, SSM_STATE)
    h_s, s_new = _ssd_sample(z, xp7, dt, s0, xs2, cw, cb, dtb, alog, dsk, gnw, e01, g1, w_out, tp, n_tok)
    ssm_s = s_new.reshape(1, bs, n_heads, SSM_HEAD_DIM, SSM_STATE)
    conv_s = _from_step_order(xbc[tp:], nsteps, n_tok)[:, n_tok - (SSM_CONV - 1):, :].reshape(
        1, bs, SSM_CONV - 1, cdim)

    h_p, h_s = _moe(h_p, h_s, ffn_norm[0], moe_w_group[0], moe_b_group[0], moe_w_router[0], moe_b_router[0],
                    moe_w_gate, moe_w_up, moe_w_down, 0, tril_x)

    pos = jnp.concatenate([jnp.tile(jnp.arange(seq, dtype=I32), bp),
                           jnp.tile(jnp.repeat(PAST_LEN + jnp.arange(n_tok, dtype=I32), SEQ_PER_STEP), nsteps)])
    rc, rs1, rs2 = _rope_tables(pos)
    q, k, v = _qkv(h_p, h_s, kv_norm.reshape(1, d), attn_norm[0].reshape(1, d), w_kv.astype(BF16), w_q[0].astype(BF16),
                   jnp.tile(k_norm, ATT_KV_HEADS).reshape(1, kvw), jnp.tile(q_norm[0], n_q).reshape(1, qw),
                   rc, rs1, rs2, hsum, rq, eq)
    sk = sinks[0]
    wo = w_o[0].astype(BF16)
    lc = cache_k_win.shape[1]
    kc = cache_k_win.reshape(bs, lc, kvw)
    vc = cache_v_win.reshape(bs, lc, kvw)
    h_p = _proj_res(_attn_prompt(sk, q, k, v, bp, seq), wo, h_p)
    h_s = _proj_res(_attn_sample(sk, q, k, v, kc, vc, tp, n_tok), wo, h_s)
    y_p, y_s = _moe(h_p, h_s, ffn_norm[1], moe_w_group[1], moe_b_group[1], moe_w_router[1], moe_b_router[1],
                    moe_w_gate, moe_w_up, moe_w_down, 1, tril_x)

    wl = min(WINDOW, seq)
    k_p = k[:tp].reshape(bp, seq, kvw)[:, seq - wl:].reshape(bp, wl, ATT_KV_HEADS, ATT_HEAD_DIM)
    v_p = v[:tp].reshape(bp, seq, kvw)[:, seq - wl:].reshape(bp, wl, ATT_KV_HEADS, ATT_HEAD_DIM)
    k_new = _from_step_order(k[tp:], nsteps, n_tok)
    v_new = _from_step_order(v[tp:], nsteps, n_tok)
    k_s = jnp.concatenate([kc, k_new], axis=1)[:, n_tok:].reshape(bs, lc, ATT_KV_HEADS, ATT_HEAD_DIM)
    v_s = jnp.concatenate([vc, v_new], axis=1)[:, n_tok:].reshape(bs, lc, ATT_KV_HEADS, ATT_HEAD_DIM)
    return (y_p.reshape(bp, seq, d), _from_step_order(y_s, nsteps, n_tok),
            ssm_p, conv_p, k_p, v_p, ssm_s, conv_s, k_s, v_s)
```

```python
import functools

import jax
import jax.numpy as jnp
from jax import lax
from jax.experimental import pallas as pl
from jax.experimental.pallas import tpu as pltpu

F32 = jnp.float32
BF16 = jnp.bfloat16
I32 = jnp.int32

EPS = 1e-6
SSM_HEAD_DIM = 64
SSM_GROUPS = 4
SSM_STATE = 128
SSM_CONV = 4
SSM_CHUNK = 128
ATT_HEAD_DIM = 64
ATT_KV_HEADS = 4
WINDOW = 128
ROT_DIM = ATT_HEAD_DIM // 4
ROPE_THETA = 500000.0
MOE_GROUPS = 4
MOE_EXPERTS_PER_GROUP = 8
MOE_EXPERTS = MOE_GROUPS * MOE_EXPERTS_PER_GROUP
MOE_BLOCK = 128
PAST_LEN = 16384

LANES = 128
SUBLANES = 8
SEQ_PER_STEP = SUBLANES
TOKEN_TILE = 256
ROW_TILES = 8
ATTN_STACK = 2
DISPATCH_TILE = 512
EXPERT_ROWS = 512
VMEM_LIMIT = 56 * 1024 * 1024


def _cparams(sem):
    return pltpu.CompilerParams(dimension_semantics=sem, vmem_limit_bytes=VMEM_LIMIT)


def _const_spec(shape):
    nd = len(shape)
    return pl.BlockSpec(shape, lambda *_: (0,) * nd)


def _split_bf16(v, n):
    parts = []
    r = v
    for k in range(n):
        p = r.astype(BF16)
        parts.append(p)
        if k + 1 < n:
            r = r - p.astype(F32)
    return parts


def _mm01(v, m01, n=3):
    acc = None
    for p in _split_bf16(v, n):
        d = jnp.dot(p, m01, preferred_element_type=F32)
        acc = d if acc is None else acc + d
    return acc


def _mm01_left(m01, v, n=3):
    acc = None
    for p in _split_bf16(v, n):
        d = jnp.dot(m01, p, preferred_element_type=F32)
        acc = d if acc is None else acc + d
    return acc


def _dot_nt(a, b):
    return lax.dot_general(a, b, (((1,), (1,)), ((), ())), preferred_element_type=F32)


def _sigmoid(x):
    return 0.5 * jnp.tanh(0.5 * x) + 0.5


def _silu(x):
    return x * _sigmoid(x)


def _softplus(x):
    return jnp.maximum(x, 0.0) + jnp.log1p(jnp.exp(-jnp.abs(x)))


def _rms_scale(x):
    return x * lax.rsqrt(jnp.mean(x * x, axis=-1, keepdims=True) + EPS)


def _gate_norm(y, z, gn, n_groups):
    yz = y * _silu(z)
    w = yz.shape[-1] // n_groups
    outs = []
    for g in range(n_groups):
        v = yz[:, g * w:(g + 1) * w]
        outs.append(_rms_scale(v) * gn[:, g * w:(g + 1) * w])
    return jnp.concatenate(outs, axis=1)


def _rope(x, c, s1, s2):
    w = x.shape[-1]
    return x * c + pltpu.roll(x, w - ROT_DIM // 2, 1) * s1 + pltpu.roll(x, ROT_DIM // 2, 1) * s2


def _tile_lanes(t, reps):
    return t if reps == 1 else jnp.concatenate([t] * reps, axis=1)


def _inproj_kernel(xp_ref, xs_ref, g_ref, wz_ref, wx_ref, wd_ref, z_ref, xbc_ref, dt_ref, *, n_p):
    i = pl.program_id(0)
    x = jnp.where(i < n_p, xp_ref[...], xs_ref[...])
    xn = (_rms_scale(x) * g_ref[...]).astype(BF16)
    z_ref[...] = jnp.dot(xn, wz_ref[...], preferred_element_type=F32)
    xbc_ref[...] = jnp.dot(xn, wx_ref[...], preferred_element_type=F32)
    dt_ref[...] = jnp.dot(xn, wd_ref[...], preferred_element_type=F32)


def _inproj(xp2, xs2, g, wz, wx, wd):
    tp, d = xp2.shape
    ts = xs2.shape[0]
    tm = TOKEN_TILE
    n_p, n_s = tp // tm, ts // tm
    t = tp + ts
    return pl.pallas_call(
        functools.partial(_inproj_kernel, n_p=n_p),
        grid=(n_p + n_s,),
        in_specs=_two_source_specs(tm, d, n_p) + [_const_spec(a.shape) for a in (g, wz, wx, wd)],
        out_specs=[
            pl.BlockSpec((tm, wz.shape[1]), lambda i: (i, 0)),
            pl.BlockSpec((tm, wx.shape[1]), lambda i: (i, 0)),
            pl.BlockSpec((tm, wd.shape[1]), lambda i: (i, 0)),
        ],
        out_shape=[jax.ShapeDtypeStruct((t, wz.shape[1]), F32),
                   jax.ShapeDtypeStruct((t, wx.shape[1]), F32),
                   jax.ShapeDtypeStruct((t, wd.shape[1]), F32)],
        compiler_params=_cparams(("arbitrary",)),
        name="inproj",
    )(xp2, xs2, g, wz, wx, wd)


def _ssd_prompt_kernel(z_ref, xbc_ref, dt_ref, x_ref, cw_ref, cb_ref, dtb_ref, alog_ref, dsk_ref, gn_ref,
                       tril_ref, e_ref, wout_ref, h_ref, sfin_ref, cfin_ref, xpad_sc, st_sc):
    c = pl.program_id(1)
    q = SSM_CHUNK
    cd = xbc_ref.shape[1]
    di = z_ref.shape[1]
    gn_w = SSM_GROUPS * SSM_STATE
    hpg = di // SSM_GROUPS
    pad = SUBLANES

    @pl.when(c == 0)
    def _():
        xpad_sc[0:pad, :] = jnp.zeros((pad, cd), F32)
        st_sc[...] = jnp.zeros(st_sc.shape, F32)

    @pl.when(c > 0)
    def _():
        xpad_sc[0:pad, :] = xpad_sc[q:q + pad, :]

    xpad_sc[pad:pad + q, :] = xbc_ref[...]
    acc = cb_ref[...]
    for k in range(SSM_CONV):
        off = pad - (SSM_CONV - 1) + k
        acc = acc + xpad_sc[off:off + q, :] * cw_ref[k:k + 1, :]
    xc = _silu(acc)
    xs = xc[:, :di]
    bm = xc[:, di:di + gn_w]
    cm = xc[:, di + gn_w:]

    dt = _softplus(dt_ref[...] + dtb_ref[...])
    a = -jnp.exp(alog_ref[...])
    act = _mm01_left(tril_ref[...], dt * a)
    act_t = act.T
    act_last = act[q - 1:q, :]
    e01 = e_ref[...]
    xdt = xs * _mm01(dt, e01, 2)
    xd = xdt * _mm01(jnp.exp(act_last - act), e01, 2)
    eax = _mm01(jnp.exp(act), e01, 2)
    cdx = _mm01(jnp.exp(act[q - SUBLANES:q, :]), e01)[SUBLANES - 1:SUBLANES, :]

    row = lax.broadcasted_iota(I32, (q, q), 0)
    col = lax.broadcasted_iota(I32, (q, q), 1)
    causal = row >= col
    lane = lax.broadcasted_iota(I32, (q, LANES), 1)
    lo_half = lane < SSM_HEAD_DIM

    y_parts = []
    heads_per_group = hpg // SSM_HEAD_DIM
    for g in range(SSM_GROUPS):
        cg = cm[:, g * SSM_STATE:(g + 1) * SSM_STATE].astype(BF16)
        bg = bm[:, g * SSM_STATE:(g + 1) * SSM_STATE]
        cb = _dot_nt(cg, bg.astype(BF16))
        st_g = st_sc[:, g * hpg:(g + 1) * hpg]
        y_off = jnp.dot(cg, st_g.astype(BF16), preferred_element_type=F32)
        for pr in range(heads_per_group // 2):
            h0 = g * heads_per_group + 2 * pr
            ms = []
            for h in (h0, h0 + 1):
                seg = act[:, h:h + 1] - act_t[h:h + 1, :]
                lm = jnp.exp(jnp.where(causal, seg, -jnp.inf))
                ms.append((cb * lm).astype(BF16))
            m2 = jnp.concatenate(ms, axis=1)
            xpair = xdt[:, h0 * SSM_HEAD_DIM:(h0 + 2) * SSM_HEAD_DIM]
            rhs = jnp.concatenate([jnp.where(lo_half, xpair, 0.0),
                                   jnp.where(lo_half, 0.0, xpair)], axis=0).astype(BF16)
            y_d = jnp.dot(m2, rhs, preferred_element_type=F32)
            lo = 2 * pr * SSM_HEAD_DIM
            y_parts.append(y_d + y_off[:, lo:lo + LANES] * eax[:, g * hpg + lo:g * hpg + lo + LANES])
        upd = jnp.dot(bg.T.astype(BF16), xd[:, g * hpg:(g + 1) * hpg].astype(BF16),
                      preferred_element_type=F32)
        st_sc[:, g * hpg:(g + 1) * hpg] = st_g * cdx[:, g * hpg:(g + 1) * hpg] + upd

    y = jnp.concatenate(y_parts, axis=1) + xs * dsk_ref[...]
    yg = _gate_norm(y, z_ref[...], gn_ref[...], SSM_GROUPS)
    h_ref[...] = x_ref[...] + jnp.dot(yg.astype(BF16), wout_ref[...], preferred_element_type=F32)

    @pl.when(c == pl.num_programs(1) - 1)
    def _():
        sfin_ref[0] = st_sc[...].T
        cfin_ref[0] = xpad_sc[q:q + pad, :]


def _ssd_prompt(z, xbc, dt, x, cw, cb, dtb, alog, dsk, gnw, tril, e01, wout, bp, seq):
    nc = seq // SSM_CHUNK
    q = SSM_CHUNK
    di, cd, d = z.shape[1], xbc.shape[1], x.shape[1]
    rows = lambda b, c: (b * nc + c, 0)
    return pl.pallas_call(
        _ssd_prompt_kernel,
        grid=(bp, nc),
        in_specs=[
            pl.BlockSpec((q, di), rows), pl.BlockSpec((q, cd), rows), pl.BlockSpec((q, LANES), rows),
            pl.BlockSpec((q, d), rows),
            _const_spec(cw.shape), _const_spec(cb.shape), _const_spec(dtb.shape), _const_spec(alog.shape),
            _const_spec(dsk.shape), _const_spec(gnw.shape), _const_spec(tril.shape), _const_spec(e01.shape),
            _const_spec(wout.shape),
        ],
        out_specs=[
            pl.BlockSpec((q, d), rows),
            pl.BlockSpec((1, di, SSM_STATE), lambda b, c: (b, 0, 0)),
            pl.BlockSpec((1, SUBLANES, cd), lambda b, c: (b, 0, 0)),
        ],
        out_shape=[jax.ShapeDtypeStruct((bp * seq, d), F32),
                   jax.ShapeDtypeStruct((bp, di, SSM_STATE), F32),
                   jax.ShapeDtypeStruct((bp, SUBLANES, cd), F32)],
        scratch_shapes=[pltpu.VMEM((q + 2 * SUBLANES, cd), F32), pltpu.VMEM((SSM_STATE, di), F32)],
        compiler_params=_cparams(("arbitrary", "arbitrary")),
        name="ssd_prompt",
    )(z, xbc, dt, x, cw, cb, dtb, alog, dsk, gnw, tril, e01, wout)


def _ssd_sample_kernel(z_ref, xp_ref, dt_ref, s0_ref, x_ref, cw_ref, cb_ref, dtb_ref, alog_ref, dsk_ref, gn_ref,
                       e_ref, g1_ref, wout_ref, h_ref, sn_ref, yoff_sc, *, n_tok):
    hf = pl.program_id(1)
    nb = SEQ_PER_STEP
    half = nb // 2
    q = n_tok * nb
    di = z_ref.shape[1]
    gn_w = SSM_GROUPS * SSM_STATE
    hpg = di // SSM_GROUPS

    taps = [xp_ref[0, m] for m in range(n_tok + SSM_CONV - 1)]
    slabs = []
    for t in range(n_tok):
        acc = cb_ref[...]
        for k in range(SSM_CONV):
            acc = acc + taps[t + k] * cw_ref[k:k + 1, :]
        slabs.append(_silu(acc))
    xc = jnp.concatenate(slabs, axis=0)
    xs = xc[:, :di]
    bm = xc[:, di:di + gn_w]
    cm = xc[:, di + gn_w:]

    dt = _softplus(dt_ref[...] + dtb_ref[...])
    da = dt * (-jnp.exp(alog_ref[...]))
    acts = []
    run = None
    for t in range(n_tok):
        d = da[t * nb:(t + 1) * nb, :]
        run = d if run is None else run + d
        acts.append(run)
    act = jnp.concatenate(acts, axis=0)
    act_last = jnp.concatenate([acts[-1]] * n_tok, axis=0)
    e01 = e_ref[...]
    xdt = xs * _mm01(dt, e01)
    xd = xdt * _mm01(jnp.exp(act_last - act), e01)
    eax = _mm01(jnp.exp(act), e01)
    cdx = _mm01(jnp.exp(acts[-1]), e01)

    pairs = [(t, u) for t in range(n_tok) for u in range(t + 1)]
    cbp = jnp.concatenate([cm[t * nb:(t + 1) * nb, :] * bm[u * nb:(u + 1) * nb, :] for t, u in pairs], axis=0)
    seg = jnp.concatenate([acts[t] - acts[u] for t, u in pairs], axis=0)
    coef = _mm01(_mm01(cbp, g1_ref[...]) * jnp.exp(seg), e01)
    y_slabs = []
    for t in range(n_tok):
        acc = None
        for pi, (tt, u) in enumerate(pairs):
            if tt != t:
                continue
            term = coef[pi * nb:(pi + 1) * nb, :] * xdt[u * nb:(u + 1) * nb, :]
            acc = term if acc is None else acc + term
        y_slabs.append(acc)
    y_diag = jnp.concatenate(y_slabs, axis=0)

    zpad = jnp.concatenate([xd,
                            jnp.where(hf == 0, cdx[0:half, :], cdx[half:nb, :]),
                            jnp.zeros((LANES - q - half, di), F32)], axis=0)
    zt = zpad.T
    row_seq = lax.broadcasted_iota(I32, (q, 1), 0) % nb
    cm_b = cm.astype(BF16)
    y_off_g = [None] * SSM_GROUPS
    for sl in range(half):
        in_seq = row_seq == hf * half + sl
        for g in range(SSM_GROUPS):
            s_old = s0_ref[sl, g * hpg:(g + 1) * hpg, :]
            c_g = jnp.where(in_seq, cm_b[:, g * SSM_STATE:(g + 1) * SSM_STATE], jnp.zeros((), BF16))
            yo = _dot_nt(c_g, s_old.astype(BF16))
            y_off_g[g] = yo if y_off_g[g] is None else y_off_g[g] + yo
            b_g = jnp.where(in_seq, bm[:, g * SSM_STATE:(g + 1) * SSM_STATE], 0.0)
            b_pad = jnp.concatenate([b_g, jnp.zeros((LANES - q, SSM_STATE), F32)], axis=0).astype(BF16)
            zt_g = zt[g * hpg:(g + 1) * hpg, :]
            upd = jnp.dot(zt_g.astype(BF16), b_pad, preferred_element_type=F32)
            decay = zt_g[:, q + sl:q + sl + 1]
            sn_ref[sl, g * hpg:(g + 1) * hpg, :] = s_old * decay + upd
    y_off = jnp.concatenate(y_off_g, axis=1)

    @pl.when(hf == 0)
    def _():
        yoff_sc[...] = y_off

    @pl.when(hf == 1)
    def _():
        y = y_diag + (yoff_sc[...] + y_off) * eax + xs * dsk_ref[...]
        yg = _gate_norm(y, z_ref[...], gn_ref[...], SSM_GROUPS)
        h_ref[...] = x_ref[...] + jnp.dot(yg.astype(BF16), wout_ref[...], preferred_element_type=F32)


def _ssd_sample(z, xp7, dt, s0, x, cw, cb, dtb, alog, dsk, gnw, e01, g1, wout, tp, n_tok):
    nsteps = xp7.shape[0]
    nb = SEQ_PER_STEP
    half = nb // 2
    q = n_tok * nb
    di, d = z.shape[1], x.shape[1]
    cd = xp7.shape[3]
    base = tp // q
    rows = lambda s, hf: (base + s, 0)
    return pl.pallas_call(
        functools.partial(_ssd_sample_kernel, n_tok=n_tok),
        grid=(nsteps, 2),
        in_specs=[
            pl.BlockSpec((q, di), rows),
            pl.BlockSpec((1, n_tok + SSM_CONV - 1, nb, cd), lambda s, hf: (s, 0, 0, 0)),
            pl.BlockSpec((q, LANES), rows),
            pl.BlockSpec((half, di, SSM_STATE), lambda s, hf: (2 * s + hf, 0, 0)),
            pl.BlockSpec((q, d), lambda s, hf: (s, 0)),
            _const_spec(cw.shape), _const_spec(cb.shape), _const_spec(dtb.shape), _const_spec(alog.shape),
            _const_spec(dsk.shape), _const_spec(gnw.shape), _const_spec(e01.shape), _const_spec(g1.shape),
            _const_spec(wout.shape),
        ],
        out_specs=[
            pl.BlockSpec((q, d), lambda s, hf: (s, 0)),
            pl.BlockSpec((half, di, SSM_STATE), lambda s, hf: (2 * s + hf, 0, 0)),
        ],
        out_shape=[jax.ShapeDtypeStruct((nsteps * q, d), F32),
                   jax.ShapeDtypeStruct(s0.shape, F32)],
        scratch_shapes=[pltpu.VMEM((q, di), F32)],
        compiler_params=_cparams(("arbitrary", "arbitrary")),
        name="ssd_sample",
    )(z, xp7, dt, s0, x, cw, cb, dtb, alog, dsk, gnw, e01, g1, wout)


def _qkv_math(h, kvn_ref, an_ref, wkv_ref, wq_ref, kn_ref, qn_ref, rc_ref, rs1_ref, rs2_ref,
              hsum_ref, rq_ref, eq_ref, q_ref, k_ref, v_ref):
    hn = _rms_scale(h)
    kvw = k_ref.shape[1]
    kv = jnp.dot((hn * kvn_ref[...]).astype(BF16), wkv_ref[...], preferred_element_type=F32)
    k = kv[:, :kvw]
    v_ref[...] = kv[:, kvw:]
    inv_hd = 1.0 / ATT_HEAD_DIM
    k = k * lax.rsqrt(_mm01(k * k, hsum_ref[...], 2) * inv_hd + EPS) * kn_ref[...]
    rc, rs1, rs2 = rc_ref[...], rs1_ref[...], rs2_ref[...]
    rk = kvw // LANES
    k_ref[...] = _rope(k, _tile_lanes(rc, rk), _tile_lanes(rs1, rk), _tile_lanes(rs2, rk))
    q = jnp.dot((hn * an_ref[...]).astype(BF16), wq_ref[...], preferred_element_type=F32)
    rsq = lax.rsqrt(_mm01(q * q, rq_ref[...], 2) * inv_hd + EPS)
    q = q * _mm01(rsq, eq_ref[...], 2) * qn_ref[...]
    rq = q.shape[1] // LANES
    q_ref[...] = _rope(q, _tile_lanes(rc, rq), _tile_lanes(rs1, rq), _tile_lanes(rs2, rq))


def _attn_prompt_kernel(sink_ref, q_ref, kc_ref, kp_ref, vc_ref, vp_ref, o_ref):
    i = pl.program_id(1)
    w = WINDOW
    hd = ATT_HEAD_DIM
    n_q = q_ref.shape[1] // hd
    grp = n_q // ATT_KV_HEADS
    stack = ATTN_STACK
    rows = stack * w
    row = lax.broadcasted_iota(I32, (rows, 2 * w), 0) % w
    col = lax.broadcasted_iota(I32, (rows, 2 * w), 1)
    dist = row + w - col
    mask = (dist >= 0) & (dist < w) & ((col >= w) | (i > 0))
    r_head = lax.broadcasted_iota(I32, (rows, 1), 0) // w
    q = q_ref[...] * (hd ** -0.5)
    outs = [None] * n_q
    for g in range(ATT_KV_HEADS):
        sl = slice(g * hd, (g + 1) * hd)
        kk = jnp.concatenate([kp_ref[:, sl], kc_ref[:, sl]], axis=0).astype(BF16)
        vv = jnp.concatenate([vp_ref[:, sl], vc_ref[:, sl]], axis=0).astype(BF16)
        for h0 in range(g * grp, (g + 1) * grp, stack):
            qs = jnp.concatenate([q[:, h * hd:(h + 1) * hd] for h in range(h0, h0 + stack)],
                                 axis=0).astype(BF16)
            s = jnp.where(mask, _dot_nt(qs, kk), -jnp.inf)
            sink = jnp.zeros((rows, 1), F32)
            for j in range(stack):
                sink = jnp.where(r_head == j, sink_ref[h0 + j], sink)
            m = jnp.maximum(jnp.max(s, axis=-1, keepdims=True), sink)
            p = jnp.exp(s - m)
            denom = jnp.sum(p, axis=-1, keepdims=True) + jnp.exp(sink - m)
            o = jnp.dot(p.astype(BF16), vv, preferred_element_type=F32) * (1.0 / denom)
            for j in range(stack):
                outs[h0 + j] = o[j * w:(j + 1) * w, :]
    o_ref[...] = jnp.concatenate(outs, axis=1)


def _attn_prompt(sinks, q, k, v, bp, seq):
    w = WINDOW
    nb = seq // w
    qw, kvw = q.shape[1], k.shape[1]
    cur = lambda b, i: (b * nb + i, 0)
    prev = lambda b, i: (b * nb + jnp.maximum(i - 1, 0), 0)
    return pl.pallas_call(
        _attn_prompt_kernel,
        grid=(bp, nb),
        in_specs=[pl.BlockSpec(memory_space=pltpu.SMEM),
                  pl.BlockSpec((w, qw), cur), pl.BlockSpec((w, kvw), cur), pl.BlockSpec((w, kvw), prev),
                  pl.BlockSpec((w, kvw), cur), pl.BlockSpec((w, kvw), prev)],
        out_specs=pl.BlockSpec((w, qw), cur),
        out_shape=jax.ShapeDtypeStruct((bp * seq, qw), F32),
        compiler_params=_cparams(("arbitrary", "arbitrary")),
        name="attn_prompt",
    )(sinks, q, k, k, v, v)


def _attn_sample_kernel(sink_ref, q_ref, kn_ref, vn_ref, kc_ref, vc_ref, o_ref, *, n_tok):
    nb = SEQ_PER_STEP
    qn = n_tok * nb
    hd = ATT_HEAD_DIM
    lc = kc_ref.shape[1]
    n_q = q_ref.shape[1] // hd
    grp = n_q // ATT_KV_HEADS
    rows = grp * qn
    r = lax.broadcasted_iota(I32, (rows, 1), 0)
    r_seq = r % nb
    r_tok = (r % qn) // nb
    ccol = lax.broadcasted_iota(I32, (rows, lc), 1)
    mask_c = ccol >= r_tok + 1 + (lc - WINDOW)
    ncol = lax.broadcasted_iota(I32, (rows, LANES), 1)
    mask_n = (ncol < qn) & (ncol % nb == r_seq) & (ncol // nb <= r_tok)
    q = q_ref[...]
    zpad = jnp.zeros((LANES - qn, hd), F32)
    outs = [None] * n_q
    for g in range(ATT_KV_HEADS):
        sl = slice(g * hd, (g + 1) * hd)
        qs = jnp.concatenate([q[:, (g * grp + hq) * hd:(g * grp + hq + 1) * hd] for hq in range(grp)],
                             axis=0).astype(BF16)
        k_new = jnp.concatenate([kn_ref[:, sl], zpad], axis=0).astype(BF16)
        v_new = jnp.concatenate([vn_ref[:, sl], zpad], axis=0).astype(BF16)
        s_n = jnp.where(mask_n, _dot_nt(qs, k_new) * (hd ** -0.5), -jnp.inf)
        s_c = jnp.zeros((rows, lc), F32)
        for sq in range(nb):
            s_sq = _dot_nt(qs, kc_ref[sq, :, sl].astype(BF16))
            s_c = jnp.where(r_seq == sq, s_sq, s_c)
        s_c = jnp.where(mask_c, s_c * (hd ** -0.5), -jnp.inf)
        sink = jnp.zeros((rows, 1), F32)
        for hq in range(grp):
            sink = jnp.where(r // qn == hq, sink_ref[g * grp + hq], sink)
        m = jnp.maximum(jnp.maximum(jnp.max(s_c, axis=-1, keepdims=True),
                                    jnp.max(s_n, axis=-1, keepdims=True)), sink)
        p_c = jnp.exp(s_c - m)
        p_n = jnp.exp(s_n - m)
        denom = (jnp.sum(p_c, axis=-1, keepdims=True) + jnp.sum(p_n, axis=-1, keepdims=True)
                 + jnp.exp(sink - m))
        p_c = p_c / denom
        o = jnp.dot((p_n / denom).astype(BF16), v_new, preferred_element_type=F32)
        for sq in range(nb):
            o = o + jnp.dot(jnp.where(r_seq == sq, p_c, 0.0).astype(BF16), vc_ref[sq, :, sl].astype(BF16),
                            preferred_element_type=F32)
        for hq in range(grp):
            outs[g * grp + hq] = o[hq * qn:(hq + 1) * qn, :]
    o_ref[...] = jnp.concatenate(outs, axis=1)


def _attn_sample(sinks, q, k, v, kc, vc, tp, n_tok):
    nb = SEQ_PER_STEP
    qn = n_tok * nb
    nsteps = kc.shape[0] // nb
    lc, kvw = kc.shape[1], kc.shape[2]
    qw = q.shape[1]
    base = tp // qn
    rows = lambda s: (base + s, 0)
    return pl.pallas_call(
        functools.partial(_attn_sample_kernel, n_tok=n_tok),
        grid=(nsteps,),
        in_specs=[pl.BlockSpec(memory_space=pltpu.SMEM),
                  pl.BlockSpec((qn, qw), rows), pl.BlockSpec((qn, kvw), rows), pl.BlockSpec((qn, kvw), rows),
                  pl.BlockSpec((nb, lc, kvw), lambda s: (s, 0, 0)), pl.BlockSpec((nb, lc, kvw), lambda s: (s, 0, 0))],
        out_specs=pl.BlockSpec((qn, qw), lambda s: (s, 0)),
        out_shape=jax.ShapeDtypeStruct((nsteps * qn, qw), F32),
        compiler_params=_cparams(("arbitrary",)),
        name="attn_sample",
    )(sinks, q, k, v, kc, vc)


def _router_kernel(hp_ref, hs_ref, fn_ref, wh_ref, wl_ref, b_ref, tril_ref, xn_ref, info_ref, info_t_ref, cnt_ref,
                   carry_sc, *, n_p):
    h = jnp.where(pl.program_id(0) < n_p, hp_ref[...], hs_ref[...])
    _router_math(h, fn_ref, wh_ref, wl_ref, b_ref, tril_ref, xn_ref, info_ref, info_t_ref, cnt_ref, carry_sc)


def _wo_router_kernel(op_ref, os_ref, res_ref, wo_ref, fn_ref, wh_ref, wl_ref, b_ref, tril_ref,
                      h_ref, xn_ref, info_ref, info_t_ref, cnt_ref, carry_sc, *, n_p):
    o = jnp.where(pl.program_id(0) < n_p, op_ref[...], os_ref[...])
    h = res_ref[...] + jnp.dot(o.astype(BF16), wo_ref[...], preferred_element_type=F32)
    h_ref[...] = h
    _router_math(h, fn_ref, wh_ref, wl_ref, b_ref, tril_ref, xn_ref, info_ref, info_t_ref, cnt_ref, carry_sc)


def _router_math(h, fn_ref, wh_ref, wl_ref, b_ref, tril_ref, xn_ref, info_ref, info_t_ref, cnt_ref, carry_sc):
    i = pl.program_id(0)

    @pl.when(i == 0)
    def _():
        carry_sc[...] = jnp.zeros(carry_sc.shape, F32)

    xn = _rms_scale(h) * fn_ref[...]
    _store_token_major(xn_ref, xn)
    x_hi, x_lo = _split_bf16(xn, 2)
    wh, wl = wh_ref[...], wl_ref[...]
    logits = (jnp.dot(x_hi, wh, preferred_element_type=F32) + jnp.dot(x_hi, wl, preferred_element_type=F32)
              + jnp.dot(x_lo, wh, preferred_element_type=F32)) + b_ref[...]
    tm = logits.shape[0]
    lane = lax.broadcasted_iota(I32, (tm, LANES), 1).astype(F32)
    big = float(LANES)
    neg = -jnp.inf

    is_grp = (lane >= MOE_EXPERTS) & (lane < MOE_EXPERTS + MOE_GROUPS)
    lg = jnp.where(is_grp, logits, neg)
    mg = jnp.max(lg, axis=-1, keepdims=True)
    gp = 1.0 / jnp.sum(jnp.exp(lg - mg), axis=-1, keepdims=True)
    gi = jnp.min(jnp.where(lg == mg, lane, big), axis=-1, keepdims=True) - MOE_EXPERTS

    lo = gi * MOE_EXPERTS_PER_GROUP
    le = jnp.where((lane >= lo) & (lane < lo + MOE_EXPERTS_PER_GROUP), logits, neg)
    m1 = jnp.max(le, axis=-1, keepdims=True)
    i1 = jnp.min(jnp.where(le == m1, lane, big), axis=-1, keepdims=True)
    le2 = jnp.where(lane == i1, neg, le)
    m2 = jnp.max(le2, axis=-1, keepdims=True)
    i2 = jnp.min(jnp.where(le2 == m2, lane, big), axis=-1, keepdims=True)
    e2 = jnp.exp(m2 - m1)
    g1 = gp * (1.0 / (1.0 + e2))
    g2 = gp * (e2 / (1.0 + e2))

    a1 = lane == i1
    a2 = lane == i2
    onehot = jnp.where(a1 | a2, 1.0, 0.0)
    before = jnp.dot(tril_ref[...], onehot.astype(BF16), preferred_element_type=F32) + carry_sc[...]
    r1 = jnp.sum(jnp.where(a1, before, 0.0), axis=-1, keepdims=True)
    r2 = jnp.sum(jnp.where(a2, before, 0.0), axis=-1, keepdims=True)
    carry_sc[...] = carry_sc[...] + jnp.sum(onehot, axis=0, keepdims=True)
    cnt_ref[...] = carry_sc[...]

    cols = (i1, i2, g1, g2, r1, r2)
    info = jnp.zeros((tm, LANES), F32)
    for k, cval in enumerate(cols):
        info = jnp.where(lane == k, cval, info)
    info_ref[...] = info
    info_t_ref[...] = info.T[0:SUBLANES, :]


def _two_source_specs(tm, width, n_p):
    return [pl.BlockSpec((tm, width), lambda i: (jnp.minimum(i, n_p - 1), 0)),
            pl.BlockSpec((tm, width), lambda i: (jnp.maximum(i - n_p, 0), 0))]


def _router(h_p, h_s, fn, wh, wl, b, tril):
    d = h_p.shape[1]
    tm = TOKEN_TILE
    n_p = h_p.shape[0] // tm
    t = h_p.shape[0] + h_s.shape[0]
    rows = lambda i: (i, 0)
    return pl.pallas_call(
        functools.partial(_router_kernel, n_p=n_p),
        grid=(t // tm,),
        in_specs=_two_source_specs(tm, d, n_p) + [_const_spec(a.shape) for a in (fn, wh, wl, b, tril)],
        out_specs=[pl.BlockSpec((tm * ROW_TILES, LANES), rows), pl.BlockSpec((tm, LANES), rows),
                   pl.BlockSpec((SUBLANES, tm), lambda i: (0, i)), _const_spec((1, LANES))],
        out_shape=[jax.ShapeDtypeStruct((t * ROW_TILES, LANES), F32), jax.ShapeDtypeStruct((t, LANES), F32),
                   jax.ShapeDtypeStruct((SUBLANES, t), F32), jax.ShapeDtypeStruct((1, LANES), F32)],
        scratch_shapes=[pltpu.VMEM((1, LANES), F32)],
        compiler_params=_cparams(("arbitrary",)),
        name="moe_router",
    )(h_p, h_s, fn, wh, wl, b, tril)


def _wo_router(o_p, o_s, res, wo, fn, wh, wl, b, tril):
    t, d = res.shape
    tm = TOKEN_TILE
    n_p = o_p.shape[0] // tm
    rows = lambda i: (i, 0)
    return pl.pallas_call(
        functools.partial(_wo_router_kernel, n_p=n_p),
        grid=(t // tm,),
        in_specs=_two_source_specs(tm, o_p.shape[1], n_p) + [pl.BlockSpec((tm, d), rows)]
        + [_const_spec(a.shape) for a in (wo, fn, wh, wl, b, tril)],
        out_specs=[pl.BlockSpec((tm, d), rows), pl.BlockSpec((tm * ROW_TILES, LANES), rows),
                   pl.BlockSpec((tm, LANES), rows), pl.BlockSpec((SUBLANES, tm), lambda i: (0, i)),
                   _const_spec((1, LANES))],
        out_shape=[jax.ShapeDtypeStruct((t, d), F32), jax.ShapeDtypeStruct((t * ROW_TILES, LANES), F32),
                   jax.ShapeDtypeStruct((t, LANES), F32), jax.ShapeDtypeStruct((SUBLANES, t), F32),
                   jax.ShapeDtypeStruct((1, LANES), F32)],
        scratch_shapes=[pltpu.VMEM((1, LANES), F32)],
        compiler_params=_cparams(("arbitrary",)),
        name="wo_router",
    )(o_p, o_s, res, wo, fn, wh, wl, b, tril)


def _store_token_major(ref, x):
    n = x.shape[0]
    for j in range(ROW_TILES):
        ref[pl.ds(j, n, stride=ROW_TILES), :] = x[:, j * LANES:(j + 1) * LANES]


def _load_token_major(ref, n):
    return jnp.concatenate([ref[pl.ds(j, n, stride=ROW_TILES), :] for j in range(ROW_TILES)], axis=1)


def _dest_kernel(pst_ref, info_ref, dest_ref):
    info = info_ref[...]
    e = info[0:2, :]
    start = jnp.zeros(e.shape, F32)
    for k in range(MOE_EXPERTS):
        start = jnp.where(e == k, pst_ref[k].astype(F32), start)
    dest = (start + info[4:6, :]).astype(I32)
    dest_ref[...] = jnp.concatenate([dest, jnp.zeros((SUBLANES - 2, dest.shape[1]), I32)], axis=0)


def _dest(pstarts, info_t):
    return pl.pallas_call(
        _dest_kernel,
        in_specs=[pl.BlockSpec(memory_space=pltpu.SMEM), pl.BlockSpec(memory_space=pltpu.VMEM)],
        out_specs=pl.BlockSpec(memory_space=pltpu.VMEM),
        out_shape=jax.ShapeDtypeStruct(info_t.shape, I32),
        name="moe_dest",
    )(pstarts, info_t)


def _tile_copy(src, src_row, dst, dst_row, sem):
    return pltpu.make_async_copy(src.at[pl.ds(pl.multiple_of(src_row * ROW_TILES, ROW_TILES), ROW_TILES)],
                                 dst.at[pl.ds(pl.multiple_of(dst_row * ROW_TILES, ROW_TILES), ROW_TILES)], sem)


def _wait_tiles(ref, n_tokens, sem):
    blk = ref.at[pl.ds(0, n_tokens * ROW_TILES)]
    pltpu.make_async_copy(blk, blk, sem).wait()


def _dispatch_kernel(d1_ref, d2_ref, pend_ref, pcnt_ref, xn_ref, xbuf_hbm, zero_sc, sem_z, sem):
    i = pl.program_id(0)
    tm = xn_ref.shape[0] // ROW_TILES
    blk_rows = zero_sc.shape[0]

    @pl.when(i == 0)
    def _():
        zero_sc[...] = jnp.zeros(zero_sc.shape, F32)

        def zero_copy(e):
            start = pl.multiple_of(pend_ref[e] * ROW_TILES - blk_rows, ROW_TILES)
            return pltpu.make_async_copy(zero_sc, xbuf_hbm.at[pl.ds(start, blk_rows)], sem_z)

        for e in range(MOE_EXPERTS):
            @pl.when(pcnt_ref[e] > 0)
            def _():
                zero_copy(e).start()
        first_unused = pend_ref[MOE_EXPERTS - 1] * ROW_TILES // blk_rows
        n_blocks = xbuf_hbm.shape[0] // blk_rows

        def tail_copy(b):
            dst = xbuf_hbm.at[pl.ds(pl.multiple_of(b * blk_rows, blk_rows), blk_rows)]
            return pltpu.make_async_copy(zero_sc, dst, sem_z)

        def tail_start(b, carry):
            tail_copy(b).start()
            return carry

        def tail_wait(b, carry):
            tail_copy(b).wait()
            return carry
        lax.fori_loop(first_unused, n_blocks, tail_start, 0)
        for e in range(MOE_EXPERTS):
            @pl.when(pcnt_ref[e] > 0)
            def _():
                zero_copy(e).wait()
        lax.fori_loop(first_unused, n_blocks, tail_wait, 0)

    def body(r, carry):
        t = i * tm + r
        _tile_copy(xn_ref, r, xbuf_hbm, d1_ref[t], sem).start(priority=0)
        _tile_copy(xn_ref, r, xbuf_hbm, d2_ref[t], sem).start(priority=1)
        return carry
    lax.fori_loop(0, tm, body, 0, unroll=8)
    _wait_tiles(xn_ref, tm, sem)
    _wait_tiles(xn_ref, tm, sem)


def _dispatch(dest1, dest2, pends, pcounts, xn_tm, n_rows):
    n_tok = dest1.shape[0]
    tm = DISPATCH_TILE
    grid_spec = pltpu.PrefetchScalarGridSpec(
        num_scalar_prefetch=4,
        grid=(n_tok // tm,),
        in_specs=[pl.BlockSpec((tm * ROW_TILES, LANES), lambda i, *_: (i, 0))],
        out_specs=pl.BlockSpec(memory_space=pl.ANY),
        scratch_shapes=[pltpu.VMEM((EXPERT_ROWS * ROW_TILES, LANES), F32), pltpu.SemaphoreType.DMA(()),
                        pltpu.SemaphoreType.DMA(())],
    )
    return pl.pallas_call(
        _dispatch_kernel,
        grid_spec=grid_spec,
        out_shape=jax.ShapeDtypeStruct((n_rows * ROW_TILES, LANES), F32),
        compiler_params=_cparams(("arbitrary",)),
        name="moe_dispatch",
    )(dest1, dest2, pends, pcounts, xn_tm)


def _expert_kernel(blk_e_ref, nvalid_ref, next_e_ref, x_ref, wg_hbm, wu_hbm, wd_hbm, y_ref,
                   wg_sc, wu_sc, wd_sc, wg_st, wu_st, wd_st, sem, *, layer):
    i = pl.program_id(0)
    nv = nvalid_ref[0]
    rows = x_ref.shape[0] // ROW_TILES

    def weight_copies(e):
        return (pltpu.make_async_copy(wg_hbm.at[layer, e], wg_st, sem.at[0]),
                pltpu.make_async_copy(wu_hbm.at[layer, e], wu_st, sem.at[1]),
                pltpu.make_async_copy(wd_hbm.at[layer, e], wd_st, sem.at[2]))

    @pl.when((i == 0) & (nv > 0))
    def _():
        for c in weight_copies(blk_e_ref[0]):
            c.start()

    @pl.when(i < nv)
    def _():
        e = blk_e_ref[i]
        e_prev = blk_e_ref[jnp.maximum(i - 1, 0)]

        @pl.when((i == 0) | (e != e_prev))
        def _():
            for c in weight_copies(e):
                c.wait()
            wg_sc[...] = wg_st[...].astype(BF16)
            wu_sc[...] = wu_st[...].astype(BF16)
            wd_sc[...] = wd_st[...].astype(BF16)
            nxt = next_e_ref[e]

            @pl.when(nxt < MOE_EXPERTS)
            def _():
                for c in weight_copies(nxt):
                    c.start()

        x = _load_token_major(x_ref, rows).astype(BF16)
        hid = _silu(jnp.dot(x, wg_sc[...], preferred_element_type=F32)) * jnp.dot(
            x, wu_sc[...], preferred_element_type=F32)
        _store_token_major(y_ref, jnp.dot(hid.astype(BF16), wd_sc[...], preferred_element_type=F32))

    @pl.when(i >= nv)
    def _():
        y_ref[...] = jnp.zeros(y_ref.shape, F32)


def _experts(blk_e, nvalid, next_e, xbuf, wg, wu, wd, layer):
    nblk = blk_e.shape[0]
    d, hdim = wg.shape[2], wg.shape[3]
    rows = EXPERT_ROWS * ROW_TILES
    grid_spec = pltpu.PrefetchScalarGridSpec(
        num_scalar_prefetch=3,
        grid=(nblk,),
        in_specs=[pl.BlockSpec((rows, LANES), lambda i, be, nv, ne: (jnp.minimum(i, nv[0] - 1), 0)),
                  pl.BlockSpec(memory_space=pl.ANY), pl.BlockSpec(memory_space=pl.ANY),
                  pl.BlockSpec(memory_space=pl.ANY)],
        out_specs=pl.BlockSpec((rows, LANES), lambda i, be, nv, ne: (i, 0)),
        scratch_shapes=[pltpu.VMEM((d, hdim), BF16), pltpu.VMEM((d, hdim), BF16), pltpu.VMEM((hdim, d), BF16),
                        pltpu.VMEM((d, hdim), F32), pltpu.VMEM((d, hdim), F32), pltpu.VMEM((hdim, d), F32),
                        pltpu.SemaphoreType.DMA((3,))],
    )
    return pl.pallas_call(
        functools.partial(_expert_kernel, layer=layer),
        grid_spec=grid_spec,
        out_shape=jax.ShapeDtypeStruct((nblk * rows, LANES), F32),
        compiler_params=_cparams(("arbitrary",)),
        name="moe_experts",
    )(blk_e, nvalid, next_e, xbuf, wg, wu, wd)


def _gather_moe_rows(d1_ref, d2_ref, y_hbm, r_sc, sem, tm, row0=0):
    i = pl.program_id(0)
    n = pl.num_programs(0)

    def start(step):
        slot = step % 2
        base = row0 + step * tm

        def body(r, carry):
            _tile_copy(y_hbm, d1_ref[base + r], r_sc.at[slot, 0], r, sem.at[slot]).start(priority=0)
            _tile_copy(y_hbm, d2_ref[base + r], r_sc.at[slot, 1], r, sem.at[slot]).start(priority=1)
            return carry
        lax.fori_loop(0, tm, body, 0, unroll=8)

    @pl.when(i == 0)
    def _():
        start(i)

    @pl.when(i + 1 < n)
    def _():
        start(i + 1)

    slot = i % 2
    _wait_tiles(r_sc.at[slot, 0], tm, sem.at[slot])
    _wait_tiles(r_sc.at[slot, 1], tm, sem.at[slot])
    return _load_token_major(r_sc.at[slot, 0], tm), _load_token_major(r_sc.at[slot, 1], tm)


def _combine_kernel(d1_ref, d2_ref, h_ref, info_ref, y_hbm, o_ref, r_sc, sem, *, row0):
    y1, y2 = _gather_moe_rows(d1_ref, d2_ref, y_hbm, r_sc, sem, h_ref.shape[0], row0)
    info = info_ref[...]
    o_ref[...] = h_ref[...] + (y1 * info[:, 2:3] + y2 * info[:, 3:4])


def _combine(dest1, dest2, h, info, ybuf, row0, nrows):
    d = h.shape[1]
    tm = TOKEN_TILE
    base_tile = row0 // tm
    rows = lambda i, a, b: (base_tile + i, 0)
    grid_spec = pltpu.PrefetchScalarGridSpec(
        num_scalar_prefetch=2,
        grid=(nrows // tm,),
        in_specs=[pl.BlockSpec((tm, d), rows), pl.BlockSpec((tm, LANES), rows), pl.BlockSpec(memory_space=pl.ANY)],
        out_specs=pl.BlockSpec((tm, d), lambda i, a, b: (i, 0)),
        scratch_shapes=[pltpu.VMEM((2, 2, tm * ROW_TILES, LANES), F32), pltpu.SemaphoreType.DMA((2,))],
    )
    return pl.pallas_call(
        functools.partial(_combine_kernel, row0=row0),
        grid_spec=grid_spec,
        out_shape=jax.ShapeDtypeStruct((nrows, d), F32),
        compiler_params=_cparams(("arbitrary",)),
        name="moe_combine",
    )(dest1, dest2, h, info, ybuf)


def _combine_qkv_kernel(d1_ref, d2_ref, hp_ref, hs_ref, info_ref, y_hbm, kvn_ref, an_ref, wkv_ref, wq_ref, kn_ref,
                        qn_ref, rc_ref, rs1_ref, rs2_ref, hsum_ref, rq_ref, eq_ref,
                        h_ref, q_ref, k_ref, v_ref, r_sc, sem, *, n_p):
    y1, y2 = _gather_moe_rows(d1_ref, d2_ref, y_hbm, r_sc, sem, h_ref.shape[0])
    info = info_ref[...]
    h = jnp.where(pl.program_id(0) < n_p, hp_ref[...], hs_ref[...]) + (y1 * info[:, 2:3] + y2 * info[:, 3:4])
    h_ref[...] = h
    _qkv_math(h, kvn_ref, an_ref, wkv_ref, wq_ref, kn_ref, qn_ref, rc_ref, rs1_ref, rs2_ref,
              hsum_ref, rq_ref, eq_ref, q_ref, k_ref, v_ref)


def _combine_qkv(dest1, dest2, h_p, h_s, info, ybuf, kvn, an, wkv, wq, knt, qnt, rc, rs1, rs2, hsum, rq, eq):
    d = h_p.shape[1]
    tm = TOKEN_TILE
    n_p = h_p.shape[0] // tm
    t = h_p.shape[0] + h_s.shape[0]
    kvw = wkv.shape[1] // 2
    qw = wq.shape[1]
    rows = lambda i, a, b: (i, 0)
    const = lambda arr: pl.BlockSpec(arr.shape, lambda i, a, b: (0,) * arr.ndim)
    grid_spec = pltpu.PrefetchScalarGridSpec(
        num_scalar_prefetch=2,
        grid=(t // tm,),
        in_specs=[pl.BlockSpec((tm, d), lambda i, a, b: (jnp.minimum(i, n_p - 1), 0)),
                  pl.BlockSpec((tm, d), lambda i, a, b: (jnp.maximum(i - n_p, 0), 0)),
                  pl.BlockSpec((tm, LANES), rows), pl.BlockSpec(memory_space=pl.ANY)]
        + [const(a) for a in (kvn, an, wkv, wq, knt, qnt)] + [pl.BlockSpec((tm, LANES), rows)] * 3
        + [const(a) for a in (hsum, rq, eq)],
        out_specs=[pl.BlockSpec((tm, d), rows), pl.BlockSpec((tm, qw), rows), pl.BlockSpec((tm, kvw), rows),
                   pl.BlockSpec((tm, kvw), rows)],
        scratch_shapes=[pltpu.VMEM((2, 2, tm * ROW_TILES, LANES), F32), pltpu.SemaphoreType.DMA((2,))],
    )
    return pl.pallas_call(
        functools.partial(_combine_qkv_kernel, n_p=n_p),
        grid_spec=grid_spec,
        out_shape=[jax.ShapeDtypeStruct((t, d), F32), jax.ShapeDtypeStruct((t, qw), F32),
                   jax.ShapeDtypeStruct((t, kvw), F32), jax.ShapeDtypeStruct((t, kvw), F32)],
        compiler_params=_cparams(("arbitrary",)),
        name="combine_qkv",
    )(dest1, dest2, h_p, h_s, info, ybuf, kvn, an, wkv, wq, knt, qnt, rc, rs1, rs2, hsum, rq, eq)


def _router_weights(w_grp, b_grp, w_rt, b_rt):
    d = w_rt.shape[0]
    w_cat = jnp.zeros((d, LANES), F32).at[:, :MOE_EXPERTS].set(w_rt).at[:, MOE_EXPERTS:MOE_EXPERTS + MOE_GROUPS].set(w_grp)
    b_cat = jnp.zeros((1, LANES), F32).at[0, :MOE_EXPERTS].set(b_rt).at[0, MOE_EXPERTS:MOE_EXPERTS + MOE_GROUPS].set(b_grp)
    w_hi = w_cat.astype(BF16)
    w_lo = (w_cat - w_hi.astype(F32)).astype(BF16)
    return w_hi, w_lo, b_cat


def _moe_experts(xn_tm, info_t, cnt, wg, wu, wd, layer):
    t = info_t.shape[1]
    counts = cnt[0, :MOE_EXPERTS].astype(I32)
    pcounts = (counts + EXPERT_ROWS - 1) // EXPERT_ROWS * EXPERT_ROWS
    pends = jnp.cumsum(pcounts)
    pstarts = pends - pcounts
    nblk = -(-(2 * t + MOE_EXPERTS * (EXPERT_ROWS - 1)) // EXPERT_ROWS)
    blk_start = jnp.arange(nblk, dtype=I32) * EXPERT_ROWS
    blk_e = jnp.minimum(jnp.sum((pends[None, :] <= blk_start[:, None]).astype(I32), axis=1), MOE_EXPERTS - 1)
    nvalid = pends[-1:] // EXPERT_ROWS
    eid = jnp.arange(MOE_EXPERTS, dtype=I32)
    later = (eid[None, :] > eid[:, None]) & (pcounts[None, :] > 0)
    next_e = jnp.min(jnp.where(later, eid[None, :], MOE_EXPERTS), axis=1).astype(I32)

    dest = _dest(pstarts, info_t)
    dest1, dest2 = dest[0], dest[1]
    xbuf = _dispatch(dest1, dest2, pends, pcounts, xn_tm, nblk * EXPERT_ROWS)
    return dest1, dest2, _experts(blk_e, nvalid, next_e, xbuf, wg, wu, wd, layer)


def _rope_tables(pos):
    half = ROT_DIM // 2
    inv = ROPE_THETA ** (-jnp.arange(0, ROT_DIM, 2, dtype=F32) / ROT_DIM)
    ang = pos.astype(F32)[:, None] * inv[None, :]
    cos, sin = jnp.cos(ang), jnp.sin(ang)
    n = pos.shape[0]
    ones = jnp.ones((n, ATT_HEAD_DIM - ROT_DIM), F32)
    zeros_r = jnp.zeros((n, ATT_HEAD_DIM - ROT_DIM), F32)
    zeros_h = jnp.zeros((n, half), F32)
    c = jnp.concatenate([cos, cos, ones], axis=1)
    s1 = jnp.concatenate([-sin, zeros_h, zeros_r], axis=1)
    s2 = jnp.concatenate([zeros_h, sin, zeros_r], axis=1)
    reps = LANES // ATT_HEAD_DIM
    return tuple(jnp.tile(a, (1, reps)) for a in (c, s1, s2))


def _to_step_order(a, nsteps, n_tok):
    c = a.shape[-1]
    return a.reshape(nsteps, SEQ_PER_STEP, n_tok, c).transpose(0, 2, 1, 3).reshape(nsteps * n_tok * SEQ_PER_STEP, c)


def _from_step_order(a, nsteps, n_tok):
    c = a.shape[-1]
    return a.reshape(nsteps, n_tok, SEQ_PER_STEP, c).transpose(0, 2, 1, 3).reshape(nsteps * SEQ_PER_STEP, n_tok, c)


def kernel(x_prompt, x_sample, state_ssm, state_conv, cache_k_win, cache_v_win, ssm_norm, ssm_w_in, ssm_conv_w, ssm_conv_b, ssm_dt_bias, ssm_a_log, ssm_d, ssm_gate_norm, ssm_w_out, kv_norm, w_kv, k_norm, attn_norm, w_q, q_norm, sinks, w_o, ffn_norm, moe_w_group, moe_b_group, moe_w_router, moe_b_router, moe_w_gate, moe_w_up, moe_w_down):
    bp, seq, d = x_prompt.shape
    bs, n_tok, _ = x_sample.shape
    tp, ts = bp * seq, bs * n_tok
    nsteps = bs // SEQ_PER_STEP
    n_heads = ssm_d.shape[1]
    di = n_heads * SSM_HEAD_DIM
    gn_w = SSM_GROUPS * SSM_STATE
    cdim = di + 2 * gn_w
    n_q = sinks.shape[1]
    kvw = ATT_KV_HEADS * ATT_HEAD_DIM

    xp2 = x_prompt.reshape(tp, d)
    xs2 = _to_step_order(x_sample, nsteps, n_tok)

    lane_i = jnp.arange(LANES)
    e01 = (lane_i[:, None] == (jnp.arange(di) // SSM_HEAD_DIM)[None, :]).astype(BF16)
    hpg = di // SSM_GROUPS
    g1 = ((jnp.arange(gn_w) // SSM_STATE)[:, None] == (lane_i // (hpg // SSM_HEAD_DIM))[None, :])
    g1 = (g1 & (lane_i < n_heads)[None, :]).astype(BF16)
    tril_c = (jnp.arange(SSM_CHUNK)[:, None] >= jnp.arange(SSM_CHUNK)[None, :]).astype(BF16)
    tril_x = (jnp.arange(TOKEN_TILE)[:, None] > jnp.arange(TOKEN_TILE)[None, :]).astype(BF16)
    hsum = ((jnp.arange(kvw) // ATT_HEAD_DIM)[:, None] == (jnp.arange(kvw) // ATT_HEAD_DIM)[None, :]).astype(BF16)
    qw = n_q * ATT_HEAD_DIM
    rq = ((jnp.arange(qw) // ATT_HEAD_DIM)[:, None] == lane_i[None, :]).astype(BF16)
    eq = rq.T

    w_in = ssm_w_in[0]
    wz = w_in[:, :di].astype(BF16)
    wx = w_in[:, di:di + cdim].astype(BF16)
    wd = jnp.zeros((d, LANES), F32).at[:, :n_heads].set(w_in[:, di + cdim:]).astype(BF16)
    cw, cb = ssm_conv_w[0], ssm_conv_b[0].reshape(1, cdim)
    z, xbc, dt = _inproj(xp2, xs2, ssm_norm[0].reshape(1, d), wz, wx, wd)

    pad_h = lambda v: jnp.zeros((1, LANES), F32).at[0, :n_heads].set(v)
    dtb, alog = pad_h(ssm_dt_bias[0]), pad_h(ssm_a_log[0])
    dsk = jnp.repeat(ssm_d[0], SSM_HEAD_DIM).reshape(1, di)
    gnw = ssm_gate_norm[0].reshape(1, di)

    w_out = ssm_w_out[0].astype(BF16)
    h_p, s_fin, c_fin = _ssd_prompt(z, xbc, dt, xp2, cw, cb, dtb, alog, dsk, gnw, tril_c, e01, w_out, bp, seq)
    ssm_p = s_fin.reshape(1, bp, n_heads, SSM_HEAD_DIM, SSM_STATE)
    conv_p = c_fin[:, SUBLANES - (SSM_CONV - 1):, :].reshape(1, bp, SSM_CONV - 1, cdim)

    xbc_s = xbc[tp:].reshape(nsteps, n_tok, SEQ_PER_STEP, cdim)
    conv_in = state_conv[0].reshape(nsteps, SEQ_PER_STEP, SSM_CONV - 1, cdim).transpose(0, 2, 1, 3)
    xp7 = jnp.concatenate([conv_in, xbc_s], axis=1)
    s0 = state_ssm[0].reshape(bs, di, SSM_STATE)
    h_s, s_new = _ssd_sample(z, xp7, dt, s0, xs2, cw, cb, dtb, alog, dsk, gnw, e01, g1, w_out, tp, n_tok)
    ssm_s = s_new.reshape(1, bs, n_heads, SSM_HEAD_DIM, SSM_STATE)
    conv_s = _from_step_order(xbc[tp:], nsteps, n_tok)[:, n_tok - (SSM_CONV - 1):, :].reshape(
        1, bs, SSM_CONV - 1, cdim)

    w_hi, w_lo, b_cat = _router_weights(moe_w_group[0], moe_b_group[0], moe_w_router[0], moe_b_router[0])
    xn_tm, info, info_t, cnt = _router(h_p, h_s, ffn_norm[0].reshape(1, d), w_hi, w_lo, b_cat, tril_x)
    dest1, dest2, ybuf = _moe_experts(xn_tm, info_t, cnt, moe_w_gate, moe_w_up, moe_w_down, 0)

    pos = jnp.concatenate([jnp.tile(jnp.arange(seq, dtype=I32), bp),
                           jnp.tile(jnp.repeat(PAST_LEN + jnp.arange(n_tok, dtype=I32), SEQ_PER_STEP), nsteps)])
    rc, rs1, rs2 = _rope_tables(pos)
    h1, q, k, v = _combine_qkv(dest1, dest2, h_p, h_s, info, ybuf, kv_norm.reshape(1, d), attn_norm[0].reshape(1, d),
                               w_kv.astype(BF16), w_q[0].astype(BF16),
                               jnp.tile(k_norm, ATT_KV_HEADS).reshape(1, kvw), jnp.tile(q_norm[0], n_q).reshape(1, qw),
                               rc, rs1, rs2, hsum, rq, eq)
    sk = sinks[0]
    lc = cache_k_win.shape[1]
    kc = cache_k_win.reshape(bs, lc, kvw)
    vc = cache_v_win.reshape(bs, lc, kvw)
    o_p = _attn_prompt(sk, q, k, v, bp, seq)
    o_s = _attn_sample(sk, q, k, v, kc, vc, tp, n_tok)
    w_hi, w_lo, b_cat = _router_weights(moe_w_group[1], moe_b_group[1], moe_w_router[1], moe_b_router[1])
    h2, xn_tm, info, info_t, cnt = _wo_router(o_p, o_s, h1, w_o[0].astype(BF16), ffn_norm[1].reshape(1, d),
                                             w_hi, w_lo, b_cat, tril_x)
    dest1, dest2, ybuf = _moe_experts(xn_tm, info_t, cnt, moe_w_gate, moe_w_up, moe_w_down, 1)
    y_p = _combine(dest1, dest2, h2, info, ybuf, 0, tp)
    y_s = _combine(dest1, dest2, h2, info, ybuf, tp, ts)

    wl = min(WINDOW, seq)
    k_p = k[:tp].reshape(bp, seq, kvw)[:, seq - wl:].reshape(bp, wl, ATT_KV_HEADS, ATT_HEAD_DIM)
    v_p = v[:tp].reshape(bp, seq, kvw)[:, seq - wl:].reshape(bp, wl, ATT_KV_HEADS, ATT_HEAD_DIM)
    k_new = _from_step_order(k[tp:], nsteps, n_tok)
    v_new = _from_step_order(v[tp:], nsteps, n_tok)
    k_s = jnp.concatenate([kc, k_new], axis=1)[:, n_tok:].reshape(bs, lc, ATT_KV_HEADS, ATT_HEAD_DIM)
    v_s = jnp.concatenate([vc, v_new], axis=1)[:, n_tok:].reshape(bs, lc, ATT_KV_HEADS, ATT_HEAD_DIM)
    return (y_p.reshape(bp, seq, d), _from_step_order(y_s, nsteps, n_tok),
            ssm_p, conv_p, k_p, v_p, ssm_s, conv_s, k_s, v_s)
```

```python
import functools

import jax
import jax.numpy as jnp
from jax import lax
from jax.experimental import pallas as pl
from jax.experimental.pallas import tpu as pltpu

F32 = jnp.float32
BF16 = jnp.bfloat16
I32 = jnp.int32

EPS = 1e-6
SSM_HEAD_DIM = 64
SSM_GROUPS = 4
SSM_STATE = 128
SSM_CONV = 4
SSM_CHUNK = 128
ATT_HEAD_DIM = 64
ATT_KV_HEADS = 4
WINDOW = 128
ROT_DIM = ATT_HEAD_DIM // 4
ROPE_THETA = 500000.0
MOE_GROUPS = 4
MOE_EXPERTS_PER_GROUP = 8
MOE_EXPERTS = MOE_GROUPS * MOE_EXPERTS_PER_GROUP
MOE_BLOCK = 128
PAST_LEN = 16384

LANES = 128
SUBLANES = 8
BF16_ROWS = 16
SEQ_PER_STEP = SUBLANES
TOKEN_TILE = 256
ROW_TILES = 8
ATTN_STACK = 2
DISPATCH_TILE = 512
EXPERT_ROWS = 512
VMEM_LIMIT = 56 * 1024 * 1024


def _cparams(sem):
    return pltpu.CompilerParams(dimension_semantics=sem, vmem_limit_bytes=VMEM_LIMIT)


def _const_spec(shape):
    nd = len(shape)
    return pl.BlockSpec(shape, lambda *_: (0,) * nd)


def _split_bf16(v, n):
    parts = []
    r = v
    for k in range(n):
        p = r.astype(BF16)
        parts.append(p)
        if k + 1 < n:
            r = r - p.astype(F32)
    return parts


def _mm01(v, m01, n=3):
    acc = None
    for p in _split_bf16(v, n):
        d = jnp.dot(p, m01, preferred_element_type=F32)
        acc = d if acc is None else acc + d
    return acc


def _mm01_left(m01, v, n=3):
    acc = None
    for p in _split_bf16(v, n):
        d = jnp.dot(m01, p, preferred_element_type=F32)
        acc = d if acc is None else acc + d
    return acc


def _dot_nt(a, b):
    return lax.dot_general(a, b, (((1,), (1,)), ((), ())), preferred_element_type=F32)


def _sigmoid(x):
    return 0.5 * jnp.tanh(0.5 * x) + 0.5


def _silu(x):
    return x * _sigmoid(x)


def _softplus(x):
    return jnp.maximum(x, 0.0) + jnp.log1p(jnp.exp(-jnp.abs(x)))


def _rms_scale(x):
    return x * lax.rsqrt(jnp.mean(x * x, axis=-1, keepdims=True) + EPS)


def _gate_norm(y, z, gn, n_groups):
    yz = y * _silu(z)
    w = yz.shape[-1] // n_groups
    outs = []
    for g in range(n_groups):
        v = yz[:, g * w:(g + 1) * w]
        outs.append(_rms_scale(v) * gn[:, g * w:(g + 1) * w])
    return jnp.concatenate(outs, axis=1)


def _rope(x, c, s1, s2):
    w = x.shape[-1]
    return x * c + pltpu.roll(x, w - ROT_DIM // 2, 1) * s1 + pltpu.roll(x, ROT_DIM // 2, 1) * s2


def _tile_lanes(t, reps):
    return t if reps == 1 else jnp.concatenate([t] * reps, axis=1)


def _inproj_kernel(xp_ref, xs_ref, g_ref, wz_ref, wx_ref, wd_ref, z_ref, xbc_ref, dt_ref, *, n_p):
    i = pl.program_id(0)
    x = jnp.where(i < n_p, xp_ref[...], xs_ref[...])
    xn = (_rms_scale(x) * g_ref[...]).astype(BF16)
    z_ref[...] = jnp.dot(xn, wz_ref[...], preferred_element_type=F32)
    xbc_ref[...] = jnp.dot(xn, wx_ref[...], preferred_element_type=F32)
    dt_ref[...] = jnp.dot(xn, wd_ref[...], preferred_element_type=F32)


def _inproj(xp2, xs2, g, wz, wx, wd):
    tp, d = xp2.shape
    ts = xs2.shape[0]
    tm = TOKEN_TILE
    n_p, n_s = tp // tm, ts // tm
    t = tp + ts
    return pl.pallas_call(
        functools.partial(_inproj_kernel, n_p=n_p),
        grid=(n_p + n_s,),
        in_specs=_two_source_specs(tm, d, n_p) + [_const_spec(a.shape) for a in (g, wz, wx, wd)],
        out_specs=[
            pl.BlockSpec((tm, wz.shape[1]), lambda i: (i, 0)),
            pl.BlockSpec((tm, wx.shape[1]), lambda i: (i, 0)),
            pl.BlockSpec((tm, wd.shape[1]), lambda i: (i, 0)),
        ],
        out_shape=[jax.ShapeDtypeStruct((t, wz.shape[1]), F32),
                   jax.ShapeDtypeStruct((t, wx.shape[1]), F32),
                   jax.ShapeDtypeStruct((t, wd.shape[1]), F32)],
        compiler_params=_cparams(("arbitrary",)),
        name="inproj",
    )(xp2, xs2, g, wz, wx, wd)


def _ssd_prompt_kernel(z_ref, xbc_ref, dt_ref, x_ref, cw_ref, cb_ref, dtb_ref, alog_ref, dsk_ref, gn_ref,
                       tril_ref, e_ref, wout_ref, h_ref, sfin_ref, cfin_ref, xpad_sc, st_sc):
    c = pl.program_id(1)
    q = SSM_CHUNK
    cd = xbc_ref.shape[1]
    di = z_ref.shape[1]
    gn_w = SSM_GROUPS * SSM_STATE
    hpg = di // SSM_GROUPS
    pad = SUBLANES

    n_slab = cd // LANES

    @pl.when(c == 0)
    def _():
        xpad_sc[:, 0:pad, :] = jnp.zeros((n_slab, pad, LANES), F32)
        st_sc[...] = jnp.zeros(st_sc.shape, F32)

    @pl.when(c > 0)
    def _():
        xpad_sc[:, 0:pad, :] = xpad_sc[:, q:q + pad, :]

    slabs = []
    for j in range(n_slab):
        ls = slice(j * LANES, (j + 1) * LANES)
        xpad_sc[j, pad:pad + q, :] = xbc_ref[:, ls]
        acc = cb_ref[:, ls]
        for k in range(SSM_CONV):
            off = pad - (SSM_CONV - 1) + k
            acc = acc + xpad_sc[j, off:off + q, :] * cw_ref[k:k + 1, ls]
        slabs.append(_silu(acc))
    xc = jnp.concatenate(slabs, axis=1)
    xs = xc[:, :di]
    bm = xc[:, di:di + gn_w]
    cm = xc[:, di + gn_w:]

    dt = _softplus(dt_ref[...] + dtb_ref[...])
    a = -jnp.exp(alog_ref[...])
    act = _mm01_left(tril_ref[...], dt * a)
    act_t = act.T
    act_last = act[q - 1:q, :]
    pieces = (_split_bf16(dt, 2) + _split_bf16(jnp.exp(act_last - act), 2) + _split_bf16(jnp.exp(act), 2)
              + _split_bf16(jnp.exp(act[q - BF16_ROWS:q, :]), 3))
    ex = jnp.dot(jnp.concatenate(pieces, axis=0), e_ref[...], preferred_element_type=F32)
    xdt = xs * (ex[0:q] + ex[q:2 * q])
    xd = xdt * (ex[2 * q:3 * q] + ex[3 * q:4 * q])
    eax = ex[4 * q:5 * q] + ex[5 * q:6 * q]
    tail = ex[6 * q:]
    cd = (tail[0:BF16_ROWS] + tail[BF16_ROWS:2 * BF16_ROWS]) + tail[2 * BF16_ROWS:]
    cdx = cd[BF16_ROWS - 1:BF16_ROWS, :]

    row = lax.broadcasted_iota(I32, (q, q), 0)
    col = lax.broadcasted_iota(I32, (q, q), 1)
    causal = row >= col
    lane = lax.broadcasted_iota(I32, (q, LANES), 1)
    lo_half = lane < SSM_HEAD_DIM

    y_parts = []
    heads_per_group = hpg // SSM_HEAD_DIM
    for g in range(SSM_GROUPS):
        cg = cm[:, g * SSM_STATE:(g + 1) * SSM_STATE].astype(BF16)
        bg = bm[:, g * SSM_STATE:(g + 1) * SSM_STATE]
        cb = _dot_nt(cg, bg.astype(BF16))
        st_g = st_sc[:, g * hpg:(g + 1) * hpg]
        y_off = jnp.dot(cg, st_g.astype(BF16), preferred_element_type=F32)
        for pr in range(heads_per_group // 2):
            h0 = g * heads_per_group + 2 * pr
            ms = []
            for h in (h0, h0 + 1):
                seg = act[:, h:h + 1] - act_t[h:h + 1, :]
                lm = jnp.exp(jnp.where(causal, seg, -jnp.inf))
                ms.append((cb * lm).astype(BF16))
            m2 = jnp.concatenate(ms, axis=1)
            xpair = xdt[:, h0 * SSM_HEAD_DIM:(h0 + 2) * SSM_HEAD_DIM]
            rhs = jnp.concatenate([jnp.where(lo_half, xpair, 0.0),
                                   jnp.where(lo_half, 0.0, xpair)], axis=0).astype(BF16)
            y_d = jnp.dot(m2, rhs, preferred_element_type=F32)
            lo = 2 * pr * SSM_HEAD_DIM
            y_parts.append(y_d + y_off[:, lo:lo + LANES] * eax[:, g * hpg + lo:g * hpg + lo + LANES])
        upd = jnp.dot(bg.T.astype(BF16), xd[:, g * hpg:(g + 1) * hpg].astype(BF16),
                      preferred_element_type=F32)
        st_sc[:, g * hpg:(g + 1) * hpg] = st_g * cdx[:, g * hpg:(g + 1) * hpg] + upd

    y = jnp.concatenate(y_parts, axis=1) + xs * dsk_ref[...]
    yg = _gate_norm(y, z_ref[...], gn_ref[...], SSM_GROUPS)
    h_ref[...] = x_ref[...] + jnp.dot(yg.astype(BF16), wout_ref[...], preferred_element_type=F32)

    @pl.when(c == pl.num_programs(1) - 1)
    def _():
        sfin_ref[0] = st_sc[...].T
        cfin_ref[0] = jnp.concatenate([xpad_sc[j, q:q + pad, :] for j in range(n_slab)], axis=1)


def _ssd_prompt(z, xbc, dt, x, cw, cb, dtb, alog, dsk, gnw, tril, e01, wout, bp, seq):
    nc = seq // SSM_CHUNK
    q = SSM_CHUNK
    di, cd, d = z.shape[1], xbc.shape[1], x.shape[1]
    rows = lambda b, c: (b * nc + c, 0)
    return pl.pallas_call(
        _ssd_prompt_kernel,
        grid=(bp, nc),
        in_specs=[
            pl.BlockSpec((q, di), rows), pl.BlockSpec((q, cd), rows), pl.BlockSpec((q, LANES), rows),
            pl.BlockSpec((q, d), rows),
            _const_spec(cw.shape), _const_spec(cb.shape), _const_spec(dtb.shape), _const_spec(alog.shape),
            _const_spec(dsk.shape), _const_spec(gnw.shape), _const_spec(tril.shape), _const_spec(e01.shape),
            _const_spec(wout.shape),
        ],
        out_specs=[
            pl.BlockSpec((q, d), rows),
            pl.BlockSpec((1, di, SSM_STATE), lambda b, c: (b, 0, 0)),
            pl.BlockSpec((1, SUBLANES, cd), lambda b, c: (b, 0, 0)),
        ],
        out_shape=[jax.ShapeDtypeStruct((bp * seq, d), F32),
                   jax.ShapeDtypeStruct((bp, di, SSM_STATE), F32),
                   jax.ShapeDtypeStruct((bp, SUBLANES, cd), F32)],
        scratch_shapes=[pltpu.VMEM((cd // LANES, q + 2 * SUBLANES, LANES), F32), pltpu.VMEM((SSM_STATE, di), F32)],
        compiler_params=_cparams(("arbitrary", "arbitrary")),
        name="ssd_prompt",
    )(z, xbc, dt, x, cw, cb, dtb, alog, dsk, gnw, tril, e01, wout)


def _ssd_sample_kernel(z_ref, xp_ref, dt_ref, s0_ref, x_ref, cw_ref, cb_ref, dtb_ref, alog_ref, dsk_ref, gn_ref,
                       e_ref, g1_ref, wout_ref, h_ref, sn_ref, yoff_sc, *, n_tok):
    hf = pl.program_id(1)
    nb = SEQ_PER_STEP
    half = nb // 2
    q = n_tok * nb
    di = z_ref.shape[1]
    gn_w = SSM_GROUPS * SSM_STATE
    hpg = di // SSM_GROUPS

    taps = [xp_ref[0, m] for m in range(n_tok + SSM_CONV - 1)]
    slabs = []
    for t in range(n_tok):
        acc = cb_ref[...]
        for k in range(SSM_CONV):
            acc = acc + taps[t + k] * cw_ref[k:k + 1, :]
        slabs.append(_silu(acc))
    xc = jnp.concatenate(slabs, axis=0)
    xs = xc[:, :di]
    bm = xc[:, di:di + gn_w]
    cm = xc[:, di + gn_w:]

    dt = _softplus(dt_ref[...] + dtb_ref[...])
    da = dt * (-jnp.exp(alog_ref[...]))
    acts = []
    run = None
    for t in range(n_tok):
        d = da[t * nb:(t + 1) * nb, :]
        run = d if run is None else run + d
        acts.append(run)
    act = jnp.concatenate(acts, axis=0)
    act_last = jnp.concatenate([acts[-1]] * n_tok, axis=0)
    e01 = e_ref[...]
    xdt = xs * _mm01(dt, e01)
    xd = xdt * _mm01(jnp.exp(act_last - act), e01)
    eax = _mm01(jnp.exp(act), e01)
    cdx = _mm01(jnp.exp(acts[-1]), e01)

    pairs = [(t, u) for t in range(n_tok) for u in range(t + 1)]
    cbp = jnp.concatenate([cm[t * nb:(t + 1) * nb, :] * bm[u * nb:(u + 1) * nb, :] for t, u in pairs], axis=0)
    seg = jnp.concatenate([acts[t] - acts[u] for t, u in pairs], axis=0)
    coef = _mm01(_mm01(cbp, g1_ref[...]) * jnp.exp(seg), e01)
    y_slabs = []
    for t in range(n_tok):
        acc = None
        for pi, (tt, u) in enumerate(pairs):
            if tt != t:
                continue
            term = coef[pi * nb:(pi + 1) * nb, :] * xdt[u * nb:(u + 1) * nb, :]
            acc = term if acc is None else acc + term
        y_slabs.append(acc)
    y_diag = jnp.concatenate(y_slabs, axis=0)

    zpad = jnp.concatenate([xd,
                            jnp.where(hf == 0, cdx[0:half, :], cdx[half:nb, :]),
                            jnp.zeros((LANES - q - half, di), F32)], axis=0)
    zt = zpad.T
    row_seq = lax.broadcasted_iota(I32, (q, 1), 0) % nb
    cm_b = cm.astype(BF16)
    y_off_g = [None] * SSM_GROUPS
    for sl in range(half):
        in_seq = row_seq == hf * half + sl
        for g in range(SSM_GROUPS):
            s_old = s0_ref[sl, g * hpg:(g + 1) * hpg, :]
            c_g = jnp.where(in_seq, cm_b[:, g * SSM_STATE:(g + 1) * SSM_STATE], jnp.zeros((), BF16))
            yo = _dot_nt(c_g, s_old.astype(BF16))
            y_off_g[g] = yo if y_off_g[g] is None else y_off_g[g] + yo
            b_g = jnp.where(in_seq, bm[:, g * SSM_STATE:(g + 1) * SSM_STATE], 0.0)
            b_pad = jnp.concatenate([b_g, jnp.zeros((LANES - q, SSM_STATE), F32)], axis=0).astype(BF16)
            zt_g = zt[g * hpg:(g + 1) * hpg, :]
            upd = jnp.dot(zt_g.astype(BF16), b_pad, preferred_element_type=F32)
            decay = zt_g[:, q + sl:q + sl + 1]
            sn_ref[sl, g * hpg:(g + 1) * hpg, :] = s_old * decay + upd
    y_off = jnp.concatenate(y_off_g, axis=1)

    @pl.when(hf == 0)
    def _():
        yoff_sc[...] = y_off

    @pl.when(hf == 1)
    def _():
        y = y_diag + (yoff_sc[...] + y_off) * eax + xs * dsk_ref[...]
        yg = _gate_norm(y, z_ref[...], gn_ref[...], SSM_GROUPS)
        h_ref[...] = x_ref[...] + jnp.dot(yg.astype(BF16), wout_ref[...], preferred_element_type=F32)


def _ssd_sample(z, xp7, dt, s0, x, cw, cb, dtb, alog, dsk, gnw, e01, g1, wout, tp, n_tok):
    nsteps = xp7.shape[0]
    nb = SEQ_PER_STEP
    half = nb // 2
    q = n_tok * nb
    di, d = z.shape[1], x.shape[1]
    cd = xp7.shape[3]
    base = tp // q
    rows = lambda s, hf: (base + s, 0)
    return pl.pallas_call(
        functools.partial(_ssd_sample_kernel, n_tok=n_tok),
        grid=(nsteps, 2),
        in_specs=[
            pl.BlockSpec((q, di), rows),
            pl.BlockSpec((1, n_tok + SSM_CONV - 1, nb, cd), lambda s, hf: (s, 0, 0, 0)),
            pl.BlockSpec((q, LANES), rows),
            pl.BlockSpec((half, di, SSM_STATE), lambda s, hf: (2 * s + hf, 0, 0)),
            pl.BlockSpec((q, d), lambda s, hf: (s, 0)),
            _const_spec(cw.shape), _const_spec(cb.shape), _const_spec(dtb.shape), _const_spec(alog.shape),
            _const_spec(dsk.shape), _const_spec(gnw.shape), _const_spec(e01.shape), _const_spec(g1.shape),
            _const_spec(wout.shape),
        ],
        out_specs=[
            pl.BlockSpec((q, d), lambda s, hf: (s, 0)),
            pl.BlockSpec((half, di, SSM_STATE), lambda s, hf: (2 * s + hf, 0, 0)),
        ],
        out_shape=[jax.ShapeDtypeStruct((nsteps * q, d), F32),
                   jax.ShapeDtypeStruct(s0.shape, F32)],
        scratch_shapes=[pltpu.VMEM((q, di), F32)],
        compiler_params=_cparams(("arbitrary", "arbitrary")),
        name="ssd_sample",
    )(z, xp7, dt, s0, x, cw, cb, dtb, alog, dsk, gnw, e01, g1, wout)


def _qkv_math(h, kvn_ref, an_ref, wkv_ref, wq_ref, kn_ref, qn_ref, rc_ref, rs1_ref, rs2_ref,
              hsum_ref, rq_ref, eq_ref, q_ref, k_ref, v_ref):
    hn = _rms_scale(h)
    kvw = k_ref.shape[1]
    kv = jnp.dot((hn * kvn_ref[...]).astype(BF16), wkv_ref[...], preferred_element_type=F32)
    k = kv[:, :kvw]
    v_ref[...] = kv[:, kvw:]
    inv_hd = 1.0 / ATT_HEAD_DIM
    k = k * lax.rsqrt(_mm01(k * k, hsum_ref[...], 2) * inv_hd + EPS) * kn_ref[...]
    rc, rs1, rs2 = rc_ref[...], rs1_ref[...], rs2_ref[...]
    rk = kvw // LANES
    k_ref[...] = _rope(k, _tile_lanes(rc, rk), _tile_lanes(rs1, rk), _tile_lanes(rs2, rk))
    q = jnp.dot((hn * an_ref[...]).astype(BF16), wq_ref[...], preferred_element_type=F32)
    rsq = lax.rsqrt(_mm01(q * q, rq_ref[...], 2) * inv_hd + EPS)
    q = q * _mm01(rsq, eq_ref[...], 2) * qn_ref[...]
    rq = q.shape[1] // LANES
    q_ref[...] = _rope(q, _tile_lanes(rc, rq), _tile_lanes(rs1, rq), _tile_lanes(rs2, rq))


def _attn_prompt_kernel(sink_ref, q_ref, kc_ref, kp_ref, vc_ref, vp_ref, o_ref):
    i = pl.program_id(1)
    w = WINDOW
    hd = ATT_HEAD_DIM
    n_q = q_ref.shape[1] // hd
    grp = n_q // ATT_KV_HEADS
    stack = ATTN_STACK
    rows = stack * w
    row = lax.broadcasted_iota(I32, (rows, 2 * w), 0) % w
    col = lax.broadcasted_iota(I32, (rows, 2 * w), 1)
    dist = row + w - col
    mask = (dist >= 0) & (dist < w) & ((col >= w) | (i > 0))
    r_head = lax.broadcasted_iota(I32, (rows, 1), 0) // w
    q = q_ref[...] * (hd ** -0.5)
    outs = [None] * n_q
    for g in range(ATT_KV_HEADS):
        sl = slice(g * hd, (g + 1) * hd)
        kk = jnp.concatenate([kp_ref[:, sl], kc_ref[:, sl]], axis=0).astype(BF16)
        vv = jnp.concatenate([vp_ref[:, sl], vc_ref[:, sl]], axis=0).astype(BF16)
        for h0 in range(g * grp, (g + 1) * grp, stack):
            qs = jnp.concatenate([q[:, h * hd:(h + 1) * hd] for h in range(h0, h0 + stack)],
                                 axis=0).astype(BF16)
            s = jnp.where(mask, _dot_nt(qs, kk), -jnp.inf)
            sink = jnp.zeros((rows, 1), F32)
            for j in range(stack):
                sink = jnp.where(r_head == j, sink_ref[h0 + j], sink)
            m = jnp.maximum(jnp.max(s, axis=-1, keepdims=True), sink)
            p = jnp.exp(s - m)
            denom = jnp.sum(p, axis=-1, keepdims=True) + jnp.exp(sink - m)
            o = jnp.dot(p.astype(BF16), vv, preferred_element_type=F32) * (1.0 / denom)
            for j in range(stack):
                outs[h0 + j] = o[j * w:(j + 1) * w, :]
    o_ref[...] = jnp.concatenate(outs, axis=1)


def _attn_prompt(sinks, q, k, v, bp, seq):
    w = WINDOW
    nb = seq // w
    qw, kvw = q.shape[1], k.shape[1]
    cur = lambda b, i: (b * nb + i, 0)
    prev = lambda b, i: (b * nb + jnp.maximum(i - 1, 0), 0)
    return pl.pallas_call(
        _attn_prompt_kernel,
        grid=(bp, nb),
        in_specs=[pl.BlockSpec(memory_space=pltpu.SMEM),
                  pl.BlockSpec((w, qw), cur), pl.BlockSpec((w, kvw), cur), pl.BlockSpec((w, kvw), prev),
                  pl.BlockSpec((w, kvw), cur), pl.BlockSpec((w, kvw), prev)],
        out_specs=pl.BlockSpec((w, qw), cur),
        out_shape=jax.ShapeDtypeStruct((bp * seq, qw), F32),
        compiler_params=_cparams(("arbitrary", "arbitrary")),
        name="attn_prompt",
    )(sinks, q, k, k, v, v)


def _attn_sample_kernel(sink_ref, q_ref, kn_ref, vn_ref, kc_ref, vc_ref, o_ref, *, n_tok):
    nb = SEQ_PER_STEP
    qn = n_tok * nb
    hd = ATT_HEAD_DIM
    lc = kc_ref.shape[1]
    n_q = q_ref.shape[1] // hd
    grp = n_q // ATT_KV_HEADS
    rows = grp * qn
    r = lax.broadcasted_iota(I32, (rows, 1), 0)
    r_seq = r % nb
    r_tok = (r % qn) // nb
    ccol = lax.broadcasted_iota(I32, (rows, lc), 1)
    mask_c = ccol >= r_tok + 1 + (lc - WINDOW)
    ncol = lax.broadcasted_iota(I32, (rows, LANES), 1)
    mask_n = (ncol < qn) & (ncol % nb == r_seq) & (ncol // nb <= r_tok)
    q = q_ref[...]
    zpad = jnp.zeros((LANES - qn, hd), F32)
    outs = [None] * n_q
    for g in range(ATT_KV_HEADS):
        sl = slice(g * hd, (g + 1) * hd)
        qs = jnp.concatenate([q[:, (g * grp + hq) * hd:(g * grp + hq + 1) * hd] for hq in range(grp)],
                             axis=0).astype(BF16)
        k_new = jnp.concatenate([kn_ref[:, sl], zpad], axis=0).astype(BF16)
        v_new = jnp.concatenate([vn_ref[:, sl], zpad], axis=0).astype(BF16)
        s_n = jnp.where(mask_n, _dot_nt(qs, k_new) * (hd ** -0.5), -jnp.inf)
        s_c = jnp.zeros((rows, lc), F32)
        for sq in range(nb):
            s_sq = _dot_nt(qs, kc_ref[sq, :, sl].astype(BF16))
            s_c = jnp.where(r_seq == sq, s_sq, s_c)
        s_c = jnp.where(mask_c, s_c * (hd ** -0.5), -jnp.inf)
        sink = jnp.zeros((rows, 1), F32)
        for hq in range(grp):
            sink = jnp.where(r // qn == hq, sink_ref[g * grp + hq], sink)
        m = jnp.maximum(jnp.maximum(jnp.max(s_c, axis=-1, keepdims=True),
                                    jnp.max(s_n, axis=-1, keepdims=True)), sink)
        p_c = jnp.exp(s_c - m)
        p_n = jnp.exp(s_n - m)
        denom = (jnp.sum(p_c, axis=-1, keepdims=True) + jnp.sum(p_n, axis=-1, keepdims=True)
                 + jnp.exp(sink - m))
        p_c = p_c / denom
        o = jnp.dot((p_n / denom).astype(BF16), v_new, preferred_element_type=F32)
        for sq in range(nb):
            o = o + jnp.dot(jnp.where(r_seq == sq, p_c, 0.0).astype(BF16), vc_ref[sq, :, sl].astype(BF16),
                            preferred_element_type=F32)
        for hq in range(grp):
            outs[g * grp + hq] = o[hq * qn:(hq + 1) * qn, :]
    o_ref[...] = jnp.concatenate(outs, axis=1)


def _attn_sample(sinks, q, k, v, kc, vc, tp, n_tok):
    nb = SEQ_PER_STEP
    qn = n_tok * nb
    nsteps = kc.shape[0] // nb
    lc, kvw = kc.shape[1], kc.shape[2]
    qw = q.shape[1]
    base = tp // qn
    rows = lambda s: (base + s, 0)
    return pl.pallas_call(
        functools.partial(_attn_sample_kernel, n_tok=n_tok),
        grid=(nsteps,),
        in_specs=[pl.BlockSpec(memory_space=pltpu.SMEM),
                  pl.BlockSpec((qn, qw), rows), pl.BlockSpec((qn, kvw), rows), pl.BlockSpec((qn, kvw), rows),
                  pl.BlockSpec((nb, lc, kvw), lambda s: (s, 0, 0)), pl.BlockSpec((nb, lc, kvw), lambda s: (s, 0, 0))],
        out_specs=pl.BlockSpec((qn, qw), lambda s: (s, 0)),
        out_shape=jax.ShapeDtypeStruct((nsteps * qn, qw), F32),
        compiler_params=_cparams(("arbitrary",)),
        name="attn_sample",
    )(sinks, q, k, v, kc, vc)


def _router_kernel(hp_ref, hs_ref, fn_ref, wh_ref, wl_ref, b_ref, tril_ref, xn_ref, info_ref, info_t_ref, cnt_ref,
                   carry_sc, *, n_p):
    h = jnp.where(pl.program_id(0) < n_p, hp_ref[...], hs_ref[...])
    _router_math(h, fn_ref, wh_ref, wl_ref, b_ref, tril_ref, xn_ref, info_ref, info_t_ref, cnt_ref, carry_sc)


def _wo_router_kernel(op_ref, os_ref, res_ref, wo_ref, fn_ref, wh_ref, wl_ref, b_ref, tril_ref,
                      h_ref, xn_ref, info_ref, info_t_ref, cnt_ref, carry_sc, *, n_p):
    o = jnp.where(pl.program_id(0) < n_p, op_ref[...], os_ref[...])
    h = res_ref[...] + jnp.dot(o.astype(BF16), wo_ref[...], preferred_element_type=F32)
    h_ref[...] = h
    _router_math(h, fn_ref, wh_ref, wl_ref, b_ref, tril_ref, xn_ref, info_ref, info_t_ref, cnt_ref, carry_sc)


def _router_math(h, fn_ref, wh_ref, wl_ref, b_ref, tril_ref, xn_ref, info_ref, info_t_ref, cnt_ref, carry_sc):
    i = pl.program_id(0)

    @pl.when(i == 0)
    def _():
        carry_sc[...] = jnp.zeros(carry_sc.shape, F32)

    xn = _rms_scale(h) * fn_ref[...]
    _store_token_major(xn_ref, xn)
    x_hi, x_lo = _split_bf16(xn, 2)
    wh, wl = wh_ref[...], wl_ref[...]
    logits = (jnp.dot(x_hi, wh, preferred_element_type=F32) + jnp.dot(x_hi, wl, preferred_element_type=F32)
              + jnp.dot(x_lo, wh, preferred_element_type=F32)) + b_ref[...]
    tm = logits.shape[0]
    lane = lax.broadcasted_iota(I32, (tm, LANES), 1).astype(F32)
    big = float(LANES)
    neg = -jnp.inf

    is_grp = (lane >= MOE_EXPERTS) & (lane < MOE_EXPERTS + MOE_GROUPS)
    lg = jnp.where(is_grp, logits, neg)
    mg = jnp.max(lg, axis=-1, keepdims=True)
    gp = 1.0 / jnp.sum(jnp.exp(lg - mg), axis=-1, keepdims=True)
    gi = jnp.min(jnp.where(lg == mg, lane, big), axis=-1, keepdims=True) - MOE_EXPERTS

    lo = gi * MOE_EXPERTS_PER_GROUP
    le = jnp.where((lane >= lo) & (lane < lo + MOE_EXPERTS_PER_GROUP), logits, neg)
    m1 = jnp.max(le, axis=-1, keepdims=True)
    i1 = jnp.min(jnp.where(le == m1, lane, big), axis=-1, keepdims=True)
    le2 = jnp.where(lane == i1, neg, le)
    m2 = jnp.max(le2, axis=-1, keepdims=True)
    i2 = jnp.min(jnp.where(le2 == m2, lane, big), axis=-1, keepdims=True)
    e2 = jnp.exp(m2 - m1)
    g1 = gp * (1.0 / (1.0 + e2))
    g2 = gp * (e2 / (1.0 + e2))

    a1 = lane == i1
    a2 = lane == i2
    onehot = jnp.where(a1 | a2, 1.0, 0.0)
    before = jnp.dot(tril_ref[...], onehot.astype(BF16), preferred_element_type=F32) + carry_sc[...]
    r1 = jnp.sum(jnp.where(a1, before, 0.0), axis=-1, keepdims=True)
    r2 = jnp.sum(jnp.where(a2, before, 0.0), axis=-1, keepdims=True)
    carry_sc[...] = carry_sc[...] + jnp.sum(onehot, axis=0, keepdims=True)
    cnt_ref[...] = carry_sc[...]

    cols = (i1, i2, g1, g2, r1, r2)
    info = jnp.zeros((tm, LANES), F32)
    for k, cval in enumerate(cols):
        info = jnp.where(lane == k, cval, info)
    info_ref[...] = info
    info_t_ref[...] = info.T[0:SUBLANES, :]


def _two_source_specs(tm, width, n_p):
    return [pl.BlockSpec((tm, width), lambda i: (jnp.minimum(i, n_p - 1), 0)),
            pl.BlockSpec((tm, width), lambda i: (jnp.maximum(i - n_p, 0), 0))]


def _router(h_p, h_s, fn, wh, wl, b, tril):
    d = h_p.shape[1]
    tm = TOKEN_TILE
    n_p = h_p.shape[0] // tm
    t = h_p.shape[0] + h_s.shape[0]
    rows = lambda i: (i, 0)
    return pl.pallas_call(
        functools.partial(_router_kernel, n_p=n_p),
        grid=(t // tm,),
        in_specs=_two_source_specs(tm, d, n_p) + [_const_spec(a.shape) for a in (fn, wh, wl, b, tril)],
        out_specs=[pl.BlockSpec((tm * ROW_TILES, LANES), rows), pl.BlockSpec((tm, LANES), rows),
                   pl.BlockSpec((SUBLANES, tm), lambda i: (0, i)), _const_spec((1, LANES))],
        out_shape=[jax.ShapeDtypeStruct((t * ROW_TILES, LANES), F32), jax.ShapeDtypeStruct((t, LANES), F32),
                   jax.ShapeDtypeStruct((SUBLANES, t), F32), jax.ShapeDtypeStruct((1, LANES), F32)],
        scratch_shapes=[pltpu.VMEM((1, LANES), F32)],
        compiler_params=_cparams(("arbitrary",)),
        name="moe_router",
    )(h_p, h_s, fn, wh, wl, b, tril)


def _wo_router(o_p, o_s, res, wo, fn, wh, wl, b, tril):
    t, d = res.shape
    tm = TOKEN_TILE
    n_p = o_p.shape[0] // tm
    rows = lambda i: (i, 0)
    return pl.pallas_call(
        functools.partial(_wo_router_kernel, n_p=n_p),
        grid=(t // tm,),
        in_specs=_two_source_specs(tm, o_p.shape[1], n_p) + [pl.BlockSpec((tm, d), rows)]
        + [_const_spec(a.shape) for a in (wo, fn, wh, wl, b, tril)],
        out_specs=[pl.BlockSpec((tm, d), rows), pl.BlockSpec((tm * ROW_TILES, LANES), rows),
                   pl.BlockSpec((tm, LANES), rows), pl.BlockSpec((SUBLANES, tm), lambda i: (0, i)),
                   _const_spec((1, LANES))],
        out_shape=[jax.ShapeDtypeStruct((t, d), F32), jax.ShapeDtypeStruct((t * ROW_TILES, LANES), F32),
                   jax.ShapeDtypeStruct((t, LANES), F32), jax.ShapeDtypeStruct((SUBLANES, t), F32),
                   jax.ShapeDtypeStruct((1, LANES), F32)],
        scratch_shapes=[pltpu.VMEM((1, LANES), F32)],
        compiler_params=_cparams(("arbitrary",)),
        name="wo_router",
    )(o_p, o_s, res, wo, fn, wh, wl, b, tril)


def _store_token_major(ref, x):
    n = x.shape[0]
    for j in range(ROW_TILES):
        ref[pl.ds(j, n, stride=ROW_TILES), :] = x[:, j * LANES:(j + 1) * LANES]


def _load_token_major(ref, n):
    return jnp.concatenate([ref[pl.ds(j, n, stride=ROW_TILES), :] for j in range(ROW_TILES)], axis=1)


def _dest_kernel(pst_ref, info_ref, dest_ref):
    info = info_ref[...]
    e = info[0:2, :]
    start = jnp.zeros(e.shape, F32)
    for k in range(MOE_EXPERTS):
        start = jnp.where(e == k, pst_ref[k].astype(F32), start)
    dest = (start + info[4:6, :]).astype(I32)
    dest_ref[...] = jnp.concatenate([dest, jnp.zeros((SUBLANES - 2, dest.shape[1]), I32)], axis=0)


def _dest(pstarts, info_t):
    return pl.pallas_call(
        _dest_kernel,
        in_specs=[pl.BlockSpec(memory_space=pltpu.SMEM), pl.BlockSpec(memory_space=pltpu.VMEM)],
        out_specs=pl.BlockSpec(memory_space=pltpu.VMEM),
        out_shape=jax.ShapeDtypeStruct(info_t.shape, I32),
        name="moe_dest",
    )(pstarts, info_t)


def _tile_copy(src, src_row, dst, dst_row, sem):
    return pltpu.make_async_copy(src.at[pl.ds(pl.multiple_of(src_row * ROW_TILES, ROW_TILES), ROW_TILES)],
                                 dst.at[pl.ds(pl.multiple_of(dst_row * ROW_TILES, ROW_TILES), ROW_TILES)], sem)


def _wait_tiles(ref, n_tokens, sem):
    blk = ref.at[pl.ds(0, n_tokens * ROW_TILES)]
    pltpu.make_async_copy(blk, blk, sem).wait()


def _dispatch_kernel(d1_ref, d2_ref, pend_ref, pcnt_ref, xn_ref, xbuf_hbm, zero_sc, sem_z, sem):
    i = pl.program_id(0)
    tm = xn_ref.shape[0] // ROW_TILES
    blk_rows = zero_sc.shape[0]

    @pl.when(i == 0)
    def _():
        zero_sc[...] = jnp.zeros(zero_sc.shape, F32)

        def zero_copy(e):
            start = pl.multiple_of(pend_ref[e] * ROW_TILES - blk_rows, ROW_TILES)
            return pltpu.make_async_copy(zero_sc, xbuf_hbm.at[pl.ds(start, blk_rows)], sem_z)

        for e in range(MOE_EXPERTS):
            @pl.when(pcnt_ref[e] > 0)
            def _():
                zero_copy(e).start()
        first_unused = pend_ref[MOE_EXPERTS - 1] * ROW_TILES // blk_rows
        n_blocks = xbuf_hbm.shape[0] // blk_rows

        def tail_copy(b):
            dst = xbuf_hbm.at[pl.ds(pl.multiple_of(b * blk_rows, blk_rows), blk_rows)]
            return pltpu.make_async_copy(zero_sc, dst, sem_z)

        def tail_start(b, carry):
            tail_copy(b).start()
            return carry

        def tail_wait(b, carry):
            tail_copy(b).wait()
            return carry
        lax.fori_loop(first_unused, n_blocks, tail_start, 0)
        for e in range(MOE_EXPERTS):
            @pl.when(pcnt_ref[e] > 0)
            def _():
                zero_copy(e).wait()
        lax.fori_loop(first_unused, n_blocks, tail_wait, 0)

    def body(r, carry):
        t = i * tm + r
        _tile_copy(xn_ref, r, xbuf_hbm, d1_ref[t], sem).start(priority=0)
        _tile_copy(xn_ref, r, xbuf_hbm, d2_ref[t], sem).start(priority=1)
        return carry
    lax.fori_loop(0, tm, body, 0, unroll=8)
    _wait_tiles(xn_ref, tm, sem)
    _wait_tiles(xn_ref, tm, sem)


def _dispatch(dest1, dest2, pends, pcounts, xn_tm, n_rows):
    n_tok = dest1.shape[0]
    tm = DISPATCH_TILE
    grid_spec = pltpu.PrefetchScalarGridSpec(
        num_scalar_prefetch=4,
        grid=(n_tok // tm,),
        in_specs=[pl.BlockSpec((tm * ROW_TILES, LANES), lambda i, *_: (i, 0))],
        out_specs=pl.BlockSpec(memory_space=pl.ANY),
        scratch_shapes=[pltpu.VMEM((EXPERT_ROWS * ROW_TILES, LANES), F32), pltpu.SemaphoreType.DMA(()),
                        pltpu.SemaphoreType.DMA(())],
    )
    return pl.pallas_call(
        _dispatch_kernel,
        grid_spec=grid_spec,
        out_shape=jax.ShapeDtypeStruct((n_rows * ROW_TILES, LANES), F32),
        compiler_params=_cparams(("arbitrary",)),
        name="moe_dispatch",
    )(dest1, dest2, pends, pcounts, xn_tm)


def _expert_kernel(blk_e_ref, nvalid_ref, next_e_ref, x_ref, wg_hbm, wu_hbm, wd_hbm, y_ref,
                   wg_sc, wu_sc, wd_sc, wg_st, wu_st, wd_st, sem, *, layer):
    i = pl.program_id(0)
    nv = nvalid_ref[0]
    rows = x_ref.shape[0] // ROW_TILES

    def weight_copies(e):
        return (pltpu.make_async_copy(wg_hbm.at[layer, e], wg_st, sem.at[0]),
                pltpu.make_async_copy(wu_hbm.at[layer, e], wu_st, sem.at[1]),
                pltpu.make_async_copy(wd_hbm.at[layer, e], wd_st, sem.at[2]))

    @pl.when((i == 0) & (nv > 0))
    def _():
        for c in weight_copies(blk_e_ref[0]):
            c.start()

    @pl.when(i < nv)
    def _():
        e = blk_e_ref[i]
        e_prev = blk_e_ref[jnp.maximum(i - 1, 0)]

        @pl.when((i == 0) | (e != e_prev))
        def _():
            for c in weight_copies(e):
                c.wait()
            wg_sc[...] = wg_st[...].astype(BF16)
            wu_sc[...] = wu_st[...].astype(BF16)
            wd_sc[...] = wd_st[...].astype(BF16)
            nxt = next_e_ref[e]

            @pl.when(nxt < MOE_EXPERTS)
            def _():
                for c in weight_copies(nxt):
                    c.start()

        x = _load_token_major(x_ref, rows).astype(BF16)
        hid = _silu(jnp.dot(x, wg_sc[...], preferred_element_type=F32)) * jnp.dot(
            x, wu_sc[...], preferred_element_type=F32)
        _store_token_major(y_ref, jnp.dot(hid.astype(BF16), wd_sc[...], preferred_element_type=F32))

    @pl.when(i >= nv)
    def _():
        y_ref[...] = jnp.zeros(y_ref.shape, F32)


def _experts(blk_e, nvalid, next_e, xbuf, wg, wu, wd, layer):
    nblk = blk_e.shape[0]
    d, hdim = wg.shape[2], wg.shape[3]
    rows = EXPERT_ROWS * ROW_TILES
    grid_spec = pltpu.PrefetchScalarGridSpec(
        num_scalar_prefetch=3,
        grid=(nblk,),
        in_specs=[pl.BlockSpec((rows, LANES), lambda i, be, nv, ne: (jnp.minimum(i, nv[0] - 1), 0)),
                  pl.BlockSpec(memory_space=pl.ANY), pl.BlockSpec(memory_space=pl.ANY),
                  pl.BlockSpec(memory_space=pl.ANY)],
        out_specs=pl.BlockSpec((rows, LANES), lambda i, be, nv, ne: (i, 0)),
        scratch_shapes=[pltpu.VMEM((d, hdim), BF16), pltpu.VMEM((d, hdim), BF16), pltpu.VMEM((hdim, d), BF16),
                        pltpu.VMEM((d, hdim), F32), pltpu.VMEM((d, hdim), F32), pltpu.VMEM((hdim, d), F32),
                        pltpu.SemaphoreType.DMA((3,))],
    )
    return pl.pallas_call(
        functools.partial(_expert_kernel, layer=layer),
        grid_spec=grid_spec,
        out_shape=jax.ShapeDtypeStruct((nblk * rows, LANES), F32),
        compiler_params=_cparams(("arbitrary",)),
        name="moe_experts",
    )(blk_e, nvalid, next_e, xbuf, wg, wu, wd)


def _gather_moe_rows(d1_ref, d2_ref, y_hbm, r_sc, sem, tm, row0=0):
    i = pl.program_id(0)
    n = pl.num_programs(0)

    def start(step):
        slot = step % 2
        base = row0 + step * tm

        def body(r, carry):
            _tile_copy(y_hbm, d1_ref[base + r], r_sc.at[slot, 0], r, sem.at[slot]).start(priority=0)
            _tile_copy(y_hbm, d2_ref[base + r], r_sc.at[slot, 1], r, sem.at[slot]).start(priority=1)
            return carry
        lax.fori_loop(0, tm, body, 0, unroll=8)

    @pl.when(i == 0)
    def _():
        start(i)

    @pl.when(i + 1 < n)
    def _():
        start(i + 1)

    slot = i % 2
    _wait_tiles(r_sc.at[slot, 0], tm, sem.at[slot])
    _wait_tiles(r_sc.at[slot, 1], tm, sem.at[slot])
    return _load_token_major(r_sc.at[slot, 0], tm), _load_token_major(r_sc.at[slot, 1], tm)


def _combine_kernel(d1_ref, d2_ref, h_ref, info_ref, y_hbm, o_ref, r_sc, sem, *, row0):
    y1, y2 = _gather_moe_rows(d1_ref, d2_ref, y_hbm, r_sc, sem, h_ref.shape[0], row0)
    info = info_ref[...]
    o_ref[...] = h_ref[...] + (y1 * info[:, 2:3] + y2 * info[:, 3:4])


def _combine(dest1, dest2, h, info, ybuf, row0, nrows):
    d = h.shape[1]
    tm = TOKEN_TILE
    base_tile = row0 // tm
    rows = lambda i, a, b: (base_tile + i, 0)
    grid_spec = pltpu.PrefetchScalarGridSpec(
        num_scalar_prefetch=2,
        grid=(nrows // tm,),
        in_specs=[pl.BlockSpec((tm, d), rows), pl.BlockSpec((tm, LANES), rows), pl.BlockSpec(memory_space=pl.ANY)],
        out_specs=pl.BlockSpec((tm, d), lambda i, a, b: (i, 0)),
        scratch_shapes=[pltpu.VMEM((2, 2, tm * ROW_TILES, LANES), F32), pltpu.SemaphoreType.DMA((2,))],
    )
    return pl.pallas_call(
        functools.partial(_combine_kernel, row0=row0),
        grid_spec=grid_spec,
        out_shape=jax.ShapeDtypeStruct((nrows, d), F32),
        compiler_params=_cparams(("arbitrary",)),
        name="moe_combine",
    )(dest1, dest2, h, info, ybuf)


def _combine_qkv_kernel(d1_ref, d2_ref, hp_ref, hs_ref, info_ref, y_hbm, kvn_ref, an_ref, wkv_ref, wq_ref, kn_ref,
                        qn_ref, rc_ref, rs1_ref, rs2_ref, hsum_ref, rq_ref, eq_ref,
                        h_ref, q_ref, k_ref, v_ref, r_sc, sem, *, n_p):
    y1, y2 = _gather_moe_rows(d1_ref, d2_ref, y_hbm, r_sc, sem, h_ref.shape[0])
    info = info_ref[...]
    h = jnp.where(pl.program_id(0) < n_p, hp_ref[...], hs_ref[...]) + (y1 * info[:, 2:3] + y2 * info[:, 3:4])
    h_ref[...] = h
    _qkv_math(h, kvn_ref, an_ref, wkv_ref, wq_ref, kn_ref, qn_ref, rc_ref, rs1_ref, rs2_ref,
              hsum_ref, rq_ref, eq_ref, q_ref, k_ref, v_ref)


def _combine_qkv(dest1, dest2, h_p, h_s, info, ybuf, kvn, an, wkv, wq, knt, qnt, rc, rs1, rs2, hsum, rq, eq):
    d = h_p.shape[1]
    tm = TOKEN_TILE
    n_p = h_p.shape[0] // tm
    t = h_p.shape[0] + h_s.shape[0]
    kvw = wkv.shape[1] // 2
    qw = wq.shape[1]
    rows = lambda i, a, b: (i, 0)
    const = lambda arr: pl.BlockSpec(arr.shape, lambda i, a, b: (0,) * arr.ndim)
    grid_spec = pltpu.PrefetchScalarGridSpec(
        num_scalar_prefetch=2,
        grid=(t // tm,),
        in_specs=[pl.BlockSpec((tm, d), lambda i, a, b: (jnp.minimum(i, n_p - 1), 0)),
                  pl.BlockSpec((tm, d), lambda i, a, b: (jnp.maximum(i - n_p, 0), 0)),
                  pl.BlockSpec((tm, LANES), rows), pl.BlockSpec(memory_space=pl.ANY)]
        + [const(a) for a in (kvn, an, wkv, wq, knt, qnt)] + [pl.BlockSpec((tm, LANES), rows)] * 3
        + [const(a) for a in (hsum, rq, eq)],
        out_specs=[pl.BlockSpec((tm, d), rows), pl.BlockSpec((tm, qw), rows), pl.BlockSpec((tm, kvw), rows),
                   pl.BlockSpec((tm, kvw), rows)],
        scratch_shapes=[pltpu.VMEM((2, 2, tm * ROW_TILES, LANES), F32), pltpu.SemaphoreType.DMA((2,))],
    )
    return pl.pallas_call(
        functools.partial(_combine_qkv_kernel, n_p=n_p),
        grid_spec=grid_spec,
        out_shape=[jax.ShapeDtypeStruct((t, d), F32), jax.ShapeDtypeStruct((t, qw), F32),
                   jax.ShapeDtypeStruct((t, kvw), F32), jax.ShapeDtypeStruct((t, kvw), F32)],
        compiler_params=_cparams(("arbitrary",)),
        name="combine_qkv",
    )(dest1, dest2, h_p, h_s, info, ybuf, kvn, an, wkv, wq, knt, qnt, rc, rs1, rs2, hsum, rq, eq)


def _router_weights(w_grp, b_grp, w_rt, b_rt):
    d = w_rt.shape[0]
    w_cat = jnp.zeros((d, LANES), F32).at[:, :MOE_EXPERTS].set(w_rt).at[:, MOE_EXPERTS:MOE_EXPERTS + MOE_GROUPS].set(w_grp)
    b_cat = jnp.zeros((1, LANES), F32).at[0, :MOE_EXPERTS].set(b_rt).at[0, MOE_EXPERTS:MOE_EXPERTS + MOE_GROUPS].set(b_grp)
    w_hi = w_cat.astype(BF16)
    w_lo = (w_cat - w_hi.astype(F32)).astype(BF16)
    return w_hi, w_lo, b_cat


def _moe_experts(xn_tm, info_t, cnt, wg, wu, wd, layer):
    t = info_t.shape[1]
    counts = cnt[0, :MOE_EXPERTS].astype(I32)
    pcounts = (counts + EXPERT_ROWS - 1) // EXPERT_ROWS * EXPERT_ROWS
    pends = jnp.cumsum(pcounts)
    pstarts = pends - pcounts
    nblk = -(-(2 * t + MOE_EXPERTS * (EXPERT_ROWS - 1)) // EXPERT_ROWS)
    blk_start = jnp.arange(nblk, dtype=I32) * EXPERT_ROWS
    blk_e = jnp.minimum(jnp.sum((pends[None, :] <= blk_start[:, None]).astype(I32), axis=1), MOE_EXPERTS - 1)
    nvalid = pends[-1:] // EXPERT_ROWS
    eid = jnp.arange(MOE_EXPERTS, dtype=I32)
    later = (eid[None, :] > eid[:, None]) & (pcounts[None, :] > 0)
    next_e = jnp.min(jnp.where(later, eid[None, :], MOE_EXPERTS), axis=1).astype(I32)

    dest = _dest(pstarts, info_t)
    dest1, dest2 = dest[0], dest[1]
    xbuf = _dispatch(dest1, dest2, pends, pcounts, xn_tm, nblk * EXPERT_ROWS)
    return dest1, dest2, _experts(blk_e, nvalid, next_e, xbuf, wg, wu, wd, layer)


def _rope_tables(pos):
    half = ROT_DIM // 2
    inv = ROPE_THETA ** (-jnp.arange(0, ROT_DIM, 2, dtype=F32) / ROT_DIM)
    ang = pos.astype(F32)[:, None] * inv[None, :]
    cos, sin = jnp.cos(ang), jnp.sin(ang)
    n = pos.shape[0]
    ones = jnp.ones((n, ATT_HEAD_DIM - ROT_DIM), F32)
    zeros_r = jnp.zeros((n, ATT_HEAD_DIM - ROT_DIM), F32)
    zeros_h = jnp.zeros((n, half), F32)
    c = jnp.concatenate([cos, cos, ones], axis=1)
    s1 = jnp.concatenate([-sin, zeros_h, zeros_r], axis=1)
    s2 = jnp.concatenate([zeros_h, sin, zeros_r], axis=1)
    reps = LANES // ATT_HEAD_DIM
    return tuple(jnp.tile(a, (1, reps)) for a in (c, s1, s2))


def _to_step_order(a, nsteps, n_tok):
    c = a.shape[-1]
    return a.reshape(nsteps, SEQ_PER_STEP, n_tok, c).transpose(0, 2, 1, 3).reshape(nsteps * n_tok * SEQ_PER_STEP, c)


def _from_step_order(a, nsteps, n_tok):
    c = a.shape[-1]
    return a.reshape(nsteps, n_tok, SEQ_PER_STEP, c).transpose(0, 2, 1, 3).reshape(nsteps * SEQ_PER_STEP, n_tok, c)


def kernel(x_prompt, x_sample, state_ssm, state_conv, cache_k_win, cache_v_win, ssm_norm, ssm_w_in, ssm_conv_w, ssm_conv_b, ssm_dt_bias, ssm_a_log, ssm_d, ssm_gate_norm, ssm_w_out, kv_norm, w_kv, k_norm, attn_norm, w_q, q_norm, sinks, w_o, ffn_norm, moe_w_group, moe_b_group, moe_w_router, moe_b_router, moe_w_gate, moe_w_up, moe_w_down):
    bp, seq, d = x_prompt.shape
    bs, n_tok, _ = x_sample.shape
    tp, ts = bp * seq, bs * n_tok
    nsteps = bs // SEQ_PER_STEP
    n_heads = ssm_d.shape[1]
    di = n_heads * SSM_HEAD_DIM
    gn_w = SSM_GROUPS * SSM_STATE
    cdim = di + 2 * gn_w
    n_q = sinks.shape[1]
    kvw = ATT_KV_HEADS * ATT_HEAD_DIM

    xp2 = x_prompt.reshape(tp, d)
    xs2 = _to_step_order(x_sample, nsteps, n_tok)

    lane_i = jnp.arange(LANES)
    e01 = (lane_i[:, None] == (jnp.arange(di) // SSM_HEAD_DIM)[None, :]).astype(BF16)
    hpg = di // SSM_GROUPS
    g1 = ((jnp.arange(gn_w) // SSM_STATE)[:, None] == (lane_i // (hpg // SSM_HEAD_DIM))[None, :])
    g1 = (g1 & (lane_i < n_heads)[None, :]).astype(BF16)
    tril_c = (jnp.arange(SSM_CHUNK)[:, None] >= jnp.arange(SSM_CHUNK)[None, :]).astype(BF16)
    tril_x = (jnp.arange(TOKEN_TILE)[:, None] > jnp.arange(TOKEN_TILE)[None, :]).astype(BF16)
    hsum = ((jnp.arange(kvw) // ATT_HEAD_DIM)[:, None] == (jnp.arange(kvw) // ATT_HEAD_DIM)[None, :]).astype(BF16)
    qw = n_q * ATT_HEAD_DIM
    rq = ((jnp.arange(qw) // ATT_HEAD_DIM)[:, None] == lane_i[None, :]).astype(BF16)
    eq = rq.T

    w_in = ssm_w_in[0]
    wz = w_in[:, :di].astype(BF16)
    wx = w_in[:, di:di + cdim].astype(BF16)
    wd = jnp.zeros((d, LANES), F32).at[:, :n_heads].set(w_in[:, di + cdim:]).astype(BF16)
    cw, cb = ssm_conv_w[0], ssm_conv_b[0].reshape(1, cdim)
    z, xbc, dt = _inproj(xp2, xs2, ssm_norm[0].reshape(1, d), wz, wx, wd)

    pad_h = lambda v: jnp.zeros((1, LANES), F32).at[0, :n_heads].set(v)
    dtb, alog = pad_h(ssm_dt_bias[0]), pad_h(ssm_a_log[0])
    dsk = jnp.repeat(ssm_d[0], SSM_HEAD_DIM).reshape(1, di)
    gnw = ssm_gate_norm[0].reshape(1, di)

    w_out = ssm_w_out[0].astype(BF16)
    h_p, s_fin, c_fin = _ssd_prompt(z, xbc, dt, xp2, cw, cb, dtb, alog, dsk, gnw, tril_c, e01, w_out, bp, seq)
    ssm_p = s_fin.reshape(1, bp, n_heads, SSM_HEAD_DIM, SSM_STATE)
    conv_p = c_fin[:, SUBLANES - (SSM_CONV - 1):, :].reshape(1, bp, SSM_CONV - 1, cdim)

    xbc_s = xbc[tp:].reshape(nsteps, n_tok, SEQ_PER_STEP, cdim)
    conv_in = state_conv[0].reshape(nsteps, SEQ_PER_STEP, SSM_CONV - 1, cdim).transpose(0, 2, 1, 3)
    xp7 = jnp.concatenate([conv_in, xbc_s], axis=1)
    s0 = state_ssm[0].reshape(bs, di, SSM_STATE)
    h_s, s_new = _ssd_sample(z, xp7, dt, s0, xs2, cw, cb, dtb, alog, dsk, gnw, e01, g1, w_out, tp, n_tok)
    ssm_s = s_new.reshape(1, bs, n_heads, SSM_HEAD_DIM, SSM_STATE)
    conv_s = _from_step_order(xbc[tp:], nsteps, n_tok)[:, n_tok - (SSM_CONV - 1):, :].reshape(
        1, bs, SSM_CONV - 1, cdim)

    w_hi, w_lo, b_cat = _router_weights(moe_w_group[0], moe_b_group[0], moe_w_router[0], moe_b_router[0])
    xn_tm, info, info_t, cnt = _router(h_p, h_s, ffn_norm[0].reshape(1, d), w_hi, w_lo, b_cat, tril_x)
    dest1, dest2, ybuf = _moe_experts(xn_tm, info_t, cnt, moe_w_gate, moe_w_up, moe_w_down, 0)

    pos = jnp.concatenate([jnp.tile(jnp.arange(seq, dtype=I32), bp),
                           jnp.tile(jnp.repeat(PAST_LEN + jnp.arange(n_tok, dtype=I32), SEQ_PER_STEP), nsteps)])
    rc, rs1, rs2 = _rope_tables(pos)
    h1, q, k, v = _combine_qkv(dest1, dest2, h_p, h_s, info, ybuf, kv_norm.reshape(1, d), attn_norm[0].reshape(1, d),
                               w_kv.astype(BF16), w_q[0].astype(BF16),
                               jnp.tile(k_norm, ATT_KV_HEADS).reshape(1, kvw), jnp.tile(q_norm[0], n_q).reshape(1, qw),
                               rc, rs1, rs2, hsum, rq, eq)
    sk = sinks[0]
    lc = cache_k_win.shape[1]
    kc = cache_k_win.reshape(bs, lc, kvw)
    vc = cache_v_win.reshape(bs, lc, kvw)
    o_p = _attn_prompt(sk, q, k, v, bp, seq)
    o_s = _attn_sample(sk, q, k, v, kc, vc, tp, n_tok)
    w_hi, w_lo, b_cat = _router_weights(moe_w_group[1], moe_b_group[1], moe_w_router[1], moe_b_router[1])
    h2, xn_tm, info, info_t, cnt = _wo_router(o_p, o_s, h1, w_o[0].astype(BF16), ffn_norm[1].reshape(1, d),
                                             w_hi, w_lo, b_cat, tril_x)
    dest1, dest2, ybuf = _moe_experts(xn_tm, info_t, cnt, moe_w_gate, moe_w_up, moe_w_down, 1)
    y_p = _combine(dest1, dest2, h2, info, ybuf, 0, tp)
    y_s = _combine(dest1, dest2, h2, info, ybuf, tp, ts)

    wl = min(WINDOW, seq)
    k_p = k[:tp].reshape(bp, seq, kvw)[:, seq - wl:].reshape(bp, wl, ATT_KV_HEADS, ATT_HEAD_DIM)
    v_p = v[:tp].reshape(bp, seq, kvw)[:, seq - wl:].reshape(bp, wl, ATT_KV_HEADS, ATT_HEAD_DIM)
    k_new = _from_step_order(k[tp:], nsteps, n_tok)
    v_new = _from_step_order(v[tp:], nsteps, n_tok)
    k_s = jnp.concatenate([kc, k_new], axis=1)[:, n_tok:].reshape(bs, lc, ATT_KV_HEADS, ATT_HEAD_DIM)
    v_s = jnp.concatenate([vc, v_new], axis=1)[:, n_tok:].reshape(bs, lc, ATT_KV_HEADS, ATT_HEAD_DIM)
    return (y_p.reshape(bp, seq, d), _from_step_order(y_s, nsteps, n_tok),
            ssm_p, conv_p, k_p, v_p, ssm_s, conv_s, k_s, v_s)
```

```python
import functools

import jax
import jax.numpy as jnp
from jax import lax
from jax.experimental import pallas as pl
from jax.experimental.pallas import tpu as pltpu

F32 = jnp.float32
BF16 = jnp.bfloat16
I32 = jnp.int32

EPS = 1e-6
SSM_HEAD_DIM = 64
SSM_GROUPS = 4
SSM_STATE = 128
SSM_CONV = 4
SSM_CHUNK = 128
ATT_HEAD_DIM = 64
ATT_KV_HEADS = 4
WINDOW = 128
ROT_DIM = ATT_HEAD_DIM // 4
ROPE_THETA = 500000.0
MOE_GROUPS = 4
MOE_EXPERTS_PER_GROUP = 8
MOE_EXPERTS = MOE_GROUPS * MOE_EXPERTS_PER_GROUP
MOE_BLOCK = 128
PAST_LEN = 16384

LANES = 128
SUBLANES = 8
BF16_ROWS = 16
SEQ_PER_STEP = SUBLANES
TOKEN_TILE = 256
ROW_TILES = 8
ATTN_STACK = 2
DISPATCH_TILE = 512
EXPERT_ROWS = 512
VMEM_LIMIT = 56 * 1024 * 1024


def _cparams(sem):
    return pltpu.CompilerParams(dimension_semantics=sem, vmem_limit_bytes=VMEM_LIMIT)


def _const_spec(shape):
    nd = len(shape)
    return pl.BlockSpec(shape, lambda *_: (0,) * nd)


def _split_bf16(v, n):
    parts = []
    r = v
    for k in range(n):
        p = r.astype(BF16)
        parts.append(p)
        if k + 1 < n:
            r = r - p.astype(F32)
    return parts


def _mm01(v, m01, n=3):
    acc = None
    for p in _split_bf16(v, n):
        d = jnp.dot(p, m01, preferred_element_type=F32)
        acc = d if acc is None else acc + d
    return acc


def _mm01_left(m01, v, n=3):
    acc = None
    for p in _split_bf16(v, n):
        d = jnp.dot(m01, p, preferred_element_type=F32)
        acc = d if acc is None else acc + d
    return acc


def _dot_nt(a, b):
    return lax.dot_general(a, b, (((1,), (1,)), ((), ())), preferred_element_type=F32)


def _sigmoid(x):
    return 0.5 * jnp.tanh(0.5 * x) + 0.5


def _silu(x):
    return x * _sigmoid(x)


def _softplus(x):
    return jnp.maximum(x, 0.0) + jnp.log1p(jnp.exp(-jnp.abs(x)))


def _rms_scale(x):
    return x * lax.rsqrt(jnp.mean(x * x, axis=-1, keepdims=True) + EPS)


def _gate_norm(y, z, gn, n_groups):
    yz = y * _silu(z)
    w = yz.shape[-1] // n_groups
    outs = []
    for g in range(n_groups):
        v = yz[:, g * w:(g + 1) * w]
        outs.append(_rms_scale(v) * gn[:, g * w:(g + 1) * w])
    return jnp.concatenate(outs, axis=1)


def _rope(x, c, s1, s2):
    w = x.shape[-1]
    return x * c + pltpu.roll(x, w - ROT_DIM // 2, 1) * s1 + pltpu.roll(x, ROT_DIM // 2, 1) * s2


def _tile_lanes(t, reps):
    return t if reps == 1 else jnp.concatenate([t] * reps, axis=1)


def _inproj_kernel(xp_ref, xs_ref, g_ref, wz_ref, wx_ref, wd_ref, z_ref, xbc_ref, dt_ref, *, n_p):
    i = pl.program_id(0)
    x = jnp.where(i < n_p, xp_ref[...], xs_ref[...])
    xn = (_rms_scale(x) * g_ref[...]).astype(BF16)
    z_ref[...] = jnp.dot(xn, wz_ref[...], preferred_element_type=F32)
    xbc_ref[...] = jnp.dot(xn, wx_ref[...], preferred_element_type=F32)
    dt_ref[...] = jnp.dot(xn, wd_ref[...], preferred_element_type=F32)


def _inproj(xp2, xs2, g, wz, wx, wd):
    tp, d = xp2.shape
    ts = xs2.shape[0]
    tm = TOKEN_TILE
    n_p, n_s = tp // tm, ts // tm
    t = tp + ts
    return pl.pallas_call(
        functools.partial(_inproj_kernel, n_p=n_p),
        grid=(n_p + n_s,),
        in_specs=_two_source_specs(tm, d, n_p) + [_const_spec(a.shape) for a in (g, wz, wx, wd)],
        out_specs=[
            pl.BlockSpec((tm, wz.shape[1]), lambda i: (i, 0)),
            pl.BlockSpec((tm, wx.shape[1]), lambda i: (i, 0)),
            pl.BlockSpec((tm, wd.shape[1]), lambda i: (i, 0)),
        ],
        out_shape=[jax.ShapeDtypeStruct((t, wz.shape[1]), F32),
                   jax.ShapeDtypeStruct((t, wx.shape[1]), F32),
                   jax.ShapeDtypeStruct((t, wd.shape[1]), F32)],
        compiler_params=_cparams(("arbitrary",)),
        name="inproj",
    )(xp2, xs2, g, wz, wx, wd)


def _ssd_prompt_kernel(z_ref, xbc_ref, dt_ref, x_ref, cw_ref, cb_ref, dtb_ref, alog_ref, dsk_ref, gn_ref,
                       tril_ref, e_ref, wout_ref, h_ref, sfin_ref, cfin_ref, xpad_sc, st_sc):
    c = pl.program_id(1)
    q = SSM_CHUNK
    cd = xbc_ref.shape[1]
    di = z_ref.shape[1]
    gn_w = SSM_GROUPS * SSM_STATE
    hpg = di // SSM_GROUPS
    pad = SUBLANES

    n_slab = cd // LANES

    @pl.when(c == 0)
    def _():
        xpad_sc[:, 0:pad, :] = jnp.zeros((n_slab, pad, LANES), F32)
        st_sc[...] = jnp.zeros(st_sc.shape, F32)

    @pl.when(c > 0)
    def _():
        xpad_sc[:, 0:pad, :] = xpad_sc[:, q:q + pad, :]

    slabs = []
    for j in range(n_slab):
        ls = slice(j * LANES, (j + 1) * LANES)
        xpad_sc[j, pad:pad + q, :] = xbc_ref[:, ls]
        acc = cb_ref[:, ls]
        for k in range(SSM_CONV):
            off = pad - (SSM_CONV - 1) + k
            acc = acc + xpad_sc[j, off:off + q, :] * cw_ref[k:k + 1, ls]
        slabs.append(_silu(acc))
    xc = jnp.concatenate(slabs, axis=1)
    xs = xc[:, :di]
    bm = xc[:, di:di + gn_w]
    cm = xc[:, di + gn_w:]

    dt = _softplus(dt_ref[...] + dtb_ref[...])
    a = -jnp.exp(alog_ref[...])
    act = _mm01_left(tril_ref[...], dt * a)
    act_t = act.T
    act_last = act[q - 1:q, :]
    pieces = (_split_bf16(dt, 2) + _split_bf16(jnp.exp(act_last - act), 2) + _split_bf16(jnp.exp(act), 2)
              + _split_bf16(jnp.exp(act[q - BF16_ROWS:q, :]), 3))
    ex = jnp.dot(jnp.concatenate(pieces, axis=0), e_ref[...], preferred_element_type=F32)
    xdt = xs * (ex[0:q] + ex[q:2 * q])
    xd = xdt * (ex[2 * q:3 * q] + ex[3 * q:4 * q])
    eax = ex[4 * q:5 * q] + ex[5 * q:6 * q]
    tail = ex[6 * q:]
    cd = (tail[0:BF16_ROWS] + tail[BF16_ROWS:2 * BF16_ROWS]) + tail[2 * BF16_ROWS:]
    cdx = cd[BF16_ROWS - 1:BF16_ROWS, :]

    row = lax.broadcasted_iota(I32, (q, q), 0)
    col = lax.broadcasted_iota(I32, (q, q), 1)
    causal = row >= col
    lane = lax.broadcasted_iota(I32, (q, LANES), 1)
    lo_half = lane < SSM_HEAD_DIM

    y_parts = []
    heads_per_group = hpg // SSM_HEAD_DIM
    for g in range(SSM_GROUPS):
        cg = cm[:, g * SSM_STATE:(g + 1) * SSM_STATE].astype(BF16)
        bg = bm[:, g * SSM_STATE:(g + 1) * SSM_STATE]
        cb = _dot_nt(cg, bg.astype(BF16))
        st_g = st_sc[:, g * hpg:(g + 1) * hpg]
        y_off = jnp.dot(cg, st_g.astype(BF16), preferred_element_type=F32)
        for pr in range(heads_per_group // 2):
            h0 = g * heads_per_group + 2 * pr
            ms = []
            for h in (h0, h0 + 1):
                seg = act[:, h:h + 1] - act_t[h:h + 1, :]
                lm = jnp.exp(jnp.where(causal, seg, -jnp.inf))
                ms.append((cb * lm).astype(BF16))
            m2 = jnp.concatenate(ms, axis=1)
            xpair = xdt[:, h0 * SSM_HEAD_DIM:(h0 + 2) * SSM_HEAD_DIM]
            rhs = jnp.concatenate([jnp.where(lo_half, xpair, 0.0),
                                   jnp.where(lo_half, 0.0, xpair)], axis=0).astype(BF16)
            y_d = jnp.dot(m2, rhs, preferred_element_type=F32)
            lo = 2 * pr * SSM_HEAD_DIM
            y_parts.append(y_d + y_off[:, lo:lo + LANES] * eax[:, g * hpg + lo:g * hpg + lo + LANES])
        upd = jnp.dot(bg.T.astype(BF16), xd[:, g * hpg:(g + 1) * hpg].astype(BF16),
                      preferred_element_type=F32)
        st_sc[:, g * hpg:(g + 1) * hpg] = st_g * cdx[:, g * hpg:(g + 1) * hpg] + upd

    y = jnp.concatenate(y_parts, axis=1) + xs * dsk_ref[...]
    yg = _gate_norm(y, z_ref[...], gn_ref[...], SSM_GROUPS)
    h_ref[...] = x_ref[...] + jnp.dot(yg.astype(BF16), wout_ref[...], preferred_element_type=F32)

    @pl.when(c == pl.num_programs(1) - 1)
    def _():
        sfin_ref[0] = st_sc[...].T
        cfin_ref[0] = jnp.concatenate([xpad_sc[j, q:q + pad, :] for j in range(n_slab)], axis=1)


def _ssd_prompt(z, xbc, dt, x, cw, cb, dtb, alog, dsk, gnw, tril, e01, wout, bp, seq):
    nc = seq // SSM_CHUNK
    q = SSM_CHUNK
    di, cd, d = z.shape[1], xbc.shape[1], x.shape[1]
    rows = lambda b, c: (b * nc + c, 0)
    return pl.pallas_call(
        _ssd_prompt_kernel,
        grid=(bp, nc),
        in_specs=[
            pl.BlockSpec((q, di), rows), pl.BlockSpec((q, cd), rows), pl.BlockSpec((q, LANES), rows),
            pl.BlockSpec((q, d), rows),
            _const_spec(cw.shape), _const_spec(cb.shape), _const_spec(dtb.shape), _const_spec(alog.shape),
            _const_spec(dsk.shape), _const_spec(gnw.shape), _const_spec(tril.shape), _const_spec(e01.shape),
            _const_spec(wout.shape),
        ],
        out_specs=[
            pl.BlockSpec((q, d), rows),
            pl.BlockSpec((1, di, SSM_STATE), lambda b, c: (b, 0, 0)),
            pl.BlockSpec((1, SUBLANES, cd), lambda b, c: (b, 0, 0)),
        ],
        out_shape=[jax.ShapeDtypeStruct((bp * seq, d), F32),
                   jax.ShapeDtypeStruct((bp, di, SSM_STATE), F32),
                   jax.ShapeDtypeStruct((bp, SUBLANES, cd), F32)],
        scratch_shapes=[pltpu.VMEM((cd // LANES, q + 2 * SUBLANES, LANES), F32), pltpu.VMEM((SSM_STATE, di), F32)],
        compiler_params=_cparams(("arbitrary", "arbitrary")),
        name="ssd_prompt",
    )(z, xbc, dt, x, cw, cb, dtb, alog, dsk, gnw, tril, e01, wout)


def _ssd_sample_kernel(z_ref, xp_ref, dt_ref, s0_ref, x_ref, cw_ref, cb_ref, dtb_ref, alog_ref, dsk_ref, gn_ref,
                       e_ref, g1_ref, wout_ref, h_ref, sn_ref, yoff_sc, *, n_tok):
    hf = pl.program_id(1)
    nb = SEQ_PER_STEP
    half = nb // 2
    q = n_tok * nb
    di = z_ref.shape[1]
    gn_w = SSM_GROUPS * SSM_STATE
    hpg = di // SSM_GROUPS

    taps = [xp_ref[0, m] for m in range(n_tok + SSM_CONV - 1)]
    slabs = []
    for t in range(n_tok):
        acc = cb_ref[...]
        for k in range(SSM_CONV):
            acc = acc + taps[t + k] * cw_ref[k:k + 1, :]
        slabs.append(_silu(acc))
    xc = jnp.concatenate(slabs, axis=0)
    xs = xc[:, :di]
    bm = xc[:, di:di + gn_w]
    cm = xc[:, di + gn_w:]

    dt = _softplus(dt_ref[...] + dtb_ref[...])
    da = dt * (-jnp.exp(alog_ref[...]))
    acts = []
    run = None
    for t in range(n_tok):
        d = da[t * nb:(t + 1) * nb, :]
        run = d if run is None else run + d
        acts.append(run)
    act = jnp.concatenate(acts, axis=0)
    act_last = jnp.concatenate([acts[-1]] * n_tok, axis=0)
    e01 = e_ref[...]
    xdt = xs * _mm01(dt, e01)
    xd = xdt * _mm01(jnp.exp(act_last - act), e01)
    eax = _mm01(jnp.exp(act), e01)
    cdx = _mm01(jnp.exp(acts[-1]), e01)

    pairs = [(t, u) for t in range(n_tok) for u in range(t + 1)]
    cbp = jnp.concatenate([cm[t * nb:(t + 1) * nb, :] * bm[u * nb:(u + 1) * nb, :] for t, u in pairs], axis=0)
    seg = jnp.concatenate([acts[t] - acts[u] for t, u in pairs], axis=0)
    coef = _mm01(_mm01(cbp, g1_ref[...]) * jnp.exp(seg), e01)
    y_slabs = []
    for t in range(n_tok):
        acc = None
        for pi, (tt, u) in enumerate(pairs):
            if tt != t:
                continue
            term = coef[pi * nb:(pi + 1) * nb, :] * xdt[u * nb:(u + 1) * nb, :]
            acc = term if acc is None else acc + term
        y_slabs.append(acc)
    y_diag = jnp.concatenate(y_slabs, axis=0)

    zpad = jnp.concatenate([xd,
                            jnp.where(hf == 0, cdx[0:half, :], cdx[half:nb, :]),
                            jnp.zeros((LANES - q - half, di), F32)], axis=0)
    zt = zpad.T
    row_seq = lax.broadcasted_iota(I32, (q, 1), 0) % nb
    cm_b = cm.astype(BF16)
    y_off_g = [None] * SSM_GROUPS
    for sl in range(half):
        in_seq = row_seq == hf * half + sl
        for g in range(SSM_GROUPS):
            s_old = s0_ref[sl, g * hpg:(g + 1) * hpg, :]
            c_g = jnp.where(in_seq, cm_b[:, g * SSM_STATE:(g + 1) * SSM_STATE], jnp.zeros((), BF16))
            yo = _dot_nt(c_g, s_old.astype(BF16))
            y_off_g[g] = yo if y_off_g[g] is None else y_off_g[g] + yo
            b_g = jnp.where(in_seq, bm[:, g * SSM_STATE:(g + 1) * SSM_STATE], 0.0)
            b_pad = jnp.concatenate([b_g, jnp.zeros((LANES - q, SSM_STATE), F32)], axis=0).astype(BF16)
            zt_g = zt[g * hpg:(g + 1) * hpg, :]
            upd = jnp.dot(zt_g.astype(BF16), b_pad, preferred_element_type=F32)
            decay = zt_g[:, q + sl:q + sl + 1]
            sn_ref[sl, g * hpg:(g + 1) * hpg, :] = s_old * decay + upd
    y_off = jnp.concatenate(y_off_g, axis=1)

    @pl.when(hf == 0)
    def _():
        yoff_sc[...] = y_off

    @pl.when(hf == 1)
    def _():
        y = y_diag + (yoff_sc[...] + y_off) * eax + xs * dsk_ref[...]
        yg = _gate_norm(y, z_ref[...], gn_ref[...], SSM_GROUPS)
        h_ref[...] = x_ref[...] + jnp.dot(yg.astype(BF16), wout_ref[...], preferred_element_type=F32)


def _ssd_sample(z, xp7, dt, s0, x, cw, cb, dtb, alog, dsk, gnw, e01, g1, wout, tp, n_tok):
    nsteps = xp7.shape[0]
    nb = SEQ_PER_STEP
    half = nb // 2
    q = n_tok * nb
    di, d = z.shape[1], x.shape[1]
    cd = xp7.shape[3]
    base = tp // q
    rows = lambda s, hf: (base + s, 0)
    return pl.pallas_call(
        functools.partial(_ssd_sample_kernel, n_tok=n_tok),
        grid=(nsteps, 2),
        in_specs=[
            pl.BlockSpec((q, di), rows),
            pl.BlockSpec((1, n_tok + SSM_CONV - 1, nb, cd), lambda s, hf: (s, 0, 0, 0)),
            pl.BlockSpec((q, LANES), rows),
            pl.BlockSpec((half, di, SSM_STATE), lambda s, hf: (2 * s + hf, 0, 0)),
            pl.BlockSpec((q, d), lambda s, hf: (s, 0)),
            _const_spec(cw.shape), _const_spec(cb.shape), _const_spec(dtb.shape), _const_spec(alog.shape),
            _const_spec(dsk.shape), _const_spec(gnw.shape), _const_spec(e01.shape), _const_spec(g1.shape),
            _const_spec(wout.shape),
        ],
        out_specs=[
            pl.BlockSpec((q, d), lambda s, hf: (s, 0)),
            pl.BlockSpec((half, di, SSM_STATE), lambda s, hf: (2 * s + hf, 0, 0)),
        ],
        out_shape=[jax.ShapeDtypeStruct((nsteps * q, d), F32),
                   jax.ShapeDtypeStruct(s0.shape, F32)],
        scratch_shapes=[pltpu.VMEM((q, di), F32)],
        compiler_params=_cparams(("arbitrary", "arbitrary")),
        name="ssd_sample",
    )(z, xp7, dt, s0, x, cw, cb, dtb, alog, dsk, gnw, e01, g1, wout)


QKV_PHASES = 8


def _qkv_math(h, kvn_ref, an_ref, wkv_ref, wq_ref, kn_ref, qn_ref, rc_ref, rs1_ref, rs2_ref,
              hsum_ref, rq_ref, eq_ref, q_ref, k_ref, v_ref, between):
    hn = _rms_scale(h)
    between(0)
    kvw = k_ref.shape[1]
    kv = jnp.dot((hn * kvn_ref[...]).astype(BF16), wkv_ref[...], preferred_element_type=F32)
    k = kv[:, :kvw]
    v_ref[...] = kv[:, kvw:]
    between(1)
    inv_hd = 1.0 / ATT_HEAD_DIM
    k = k * lax.rsqrt(_mm01(k * k, hsum_ref[...], 2) * inv_hd + EPS) * kn_ref[...]
    between(2)
    rc, rs1, rs2 = rc_ref[...], rs1_ref[...], rs2_ref[...]
    rk = kvw // LANES
    k_ref[...] = _rope(k, _tile_lanes(rc, rk), _tile_lanes(rs1, rk), _tile_lanes(rs2, rk))
    between(3)
    q = jnp.dot((hn * an_ref[...]).astype(BF16), wq_ref[...], preferred_element_type=F32)
    between(4)
    rsq = lax.rsqrt(_mm01(q * q, rq_ref[...], 2) * inv_hd + EPS)
    between(5)
    q = q * _mm01(rsq, eq_ref[...], 2) * qn_ref[...]
    between(6)
    rq = q.shape[1] // LANES
    q_ref[...] = _rope(q, _tile_lanes(rc, rq), _tile_lanes(rs1, rq), _tile_lanes(rs2, rq))
    between(7)


def _attn_prompt_kernel(sink_ref, q_ref, kc_ref, kp_ref, vc_ref, vp_ref, o_ref):
    i = pl.program_id(1)
    w = WINDOW
    hd = ATT_HEAD_DIM
    n_q = q_ref.shape[1] // hd
    grp = n_q // ATT_KV_HEADS
    stack = ATTN_STACK
    rows = stack * w
    row = lax.broadcasted_iota(I32, (rows, 2 * w), 0) % w
    col = lax.broadcasted_iota(I32, (rows, 2 * w), 1)
    dist = row + w - col
    mask = (dist >= 0) & (dist < w) & ((col >= w) | (i > 0))
    r_head = lax.broadcasted_iota(I32, (rows, 1), 0) // w
    q = q_ref[...] * (hd ** -0.5)
    outs = [None] * n_q
    for g in range(ATT_KV_HEADS):
        sl = slice(g * hd, (g + 1) * hd)
        kk = jnp.concatenate([kp_ref[:, sl], kc_ref[:, sl]], axis=0).astype(BF16)
        vv = jnp.concatenate([vp_ref[:, sl], vc_ref[:, sl]], axis=0).astype(BF16)
        for h0 in range(g * grp, (g + 1) * grp, stack):
            qs = jnp.concatenate([q[:, h * hd:(h + 1) * hd] for h in range(h0, h0 + stack)],
                                 axis=0).astype(BF16)
            s = jnp.where(mask, _dot_nt(qs, kk), -jnp.inf)
            sink = jnp.zeros((rows, 1), F32)
            for j in range(stack):
                sink = jnp.where(r_head == j, sink_ref[h0 + j], sink)
            m = jnp.maximum(jnp.max(s, axis=-1, keepdims=True), sink)
            p = jnp.exp(s - m)
            denom = jnp.sum(p, axis=-1, keepdims=True) + jnp.exp(sink - m)
            o = jnp.dot(p.astype(BF16), vv, preferred_element_type=F32) * (1.0 / denom)
            for j in range(stack):
                outs[h0 + j] = o[j * w:(j + 1) * w, :]
    o_ref[...] = jnp.concatenate(outs, axis=1)


def _attn_prompt(sinks, q, k, v, bp, seq):
    w = WINDOW
    nb = seq // w
    qw, kvw = q.shape[1], k.shape[1]
    cur = lambda b, i: (b * nb + i, 0)
    prev = lambda b, i: (b * nb + jnp.maximum(i - 1, 0), 0)
    return pl.pallas_call(
        _attn_prompt_kernel,
        grid=(bp, nb),
        in_specs=[pl.BlockSpec(memory_space=pltpu.SMEM),
                  pl.BlockSpec((w, qw), cur), pl.BlockSpec((w, kvw), cur), pl.BlockSpec((w, kvw), prev),
                  pl.BlockSpec((w, kvw), cur), pl.BlockSpec((w, kvw), prev)],
        out_specs=pl.BlockSpec((w, qw), cur),
        out_shape=jax.ShapeDtypeStruct((bp * seq, qw), F32),
        compiler_params=_cparams(("arbitrary", "arbitrary")),
        name="attn_prompt",
    )(sinks, q, k, k, v, v)


def _attn_sample_kernel(sink_ref, q_ref, kn_ref, vn_ref, kc_ref, vc_ref, o_ref, *, n_tok):
    nb = SEQ_PER_STEP
    qn = n_tok * nb
    hd = ATT_HEAD_DIM
    lc = kc_ref.shape[1]
    n_q = q_ref.shape[1] // hd
    grp = n_q // ATT_KV_HEADS
    rows = grp * qn
    r = lax.broadcasted_iota(I32, (rows, 1), 0)
    r_seq = r % nb
    r_tok = (r % qn) // nb
    ccol = lax.broadcasted_iota(I32, (rows, lc), 1)
    mask_c = ccol >= r_tok + 1 + (lc - WINDOW)
    ncol = lax.broadcasted_iota(I32, (rows, LANES), 1)
    mask_n = (ncol < qn) & (ncol % nb == r_seq) & (ncol // nb <= r_tok)
    q = q_ref[...]
    zpad = jnp.zeros((LANES - qn, hd), F32)
    outs = [None] * n_q
    for g in range(ATT_KV_HEADS):
        sl = slice(g * hd, (g + 1) * hd)
        qs = jnp.concatenate([q[:, (g * grp + hq) * hd:(g * grp + hq + 1) * hd] for hq in range(grp)],
                             axis=0).astype(BF16)
        k_new = jnp.concatenate([kn_ref[:, sl], zpad], axis=0).astype(BF16)
        v_new = jnp.concatenate([vn_ref[:, sl], zpad], axis=0).astype(BF16)
        s_n = jnp.where(mask_n, _dot_nt(qs, k_new) * (hd ** -0.5), -jnp.inf)
        s_c = jnp.zeros((rows, lc), F32)
        for sq in range(nb):
            s_sq = _dot_nt(qs, kc_ref[sq, :, sl].astype(BF16))
            s_c = jnp.where(r_seq == sq, s_sq, s_c)
        s_c = jnp.where(mask_c, s_c * (hd ** -0.5), -jnp.inf)
        sink = jnp.zeros((rows, 1), F32)
        for hq in range(grp):
            sink = jnp.where(r // qn == hq, sink_ref[g * grp + hq], sink)
        m = jnp.maximum(jnp.maximum(jnp.max(s_c, axis=-1, keepdims=True),
                                    jnp.max(s_n, axis=-1, keepdims=True)), sink)
        p_c = jnp.exp(s_c - m)
        p_n = jnp.exp(s_n - m)
        denom = (jnp.sum(p_c, axis=-1, keepdims=True) + jnp.sum(p_n, axis=-1, keepdims=True)
                 + jnp.exp(sink - m))
        p_c = p_c / denom
        o = jnp.dot((p_n / denom).astype(BF16), v_new, preferred_element_type=F32)
        for sq in range(nb):
            o = o + jnp.dot(jnp.where(r_seq == sq, p_c, 0.0).astype(BF16), vc_ref[sq, :, sl].astype(BF16),
                            preferred_element_type=F32)
        for hq in range(grp):
            outs[g * grp + hq] = o[hq * qn:(hq + 1) * qn, :]
    o_ref[...] = jnp.concatenate(outs, axis=1)


def _attn_sample(sinks, q, k, v, kc, vc, tp, n_tok):
    nb = SEQ_PER_STEP
    qn = n_tok * nb
    nsteps = kc.shape[0] // nb
    lc, kvw = kc.shape[1], kc.shape[2]
    qw = q.shape[1]
    base = tp // qn
    rows = lambda s: (base + s, 0)
    return pl.pallas_call(
        functools.partial(_attn_sample_kernel, n_tok=n_tok),
        grid=(nsteps,),
        in_specs=[pl.BlockSpec(memory_space=pltpu.SMEM),
                  pl.BlockSpec((qn, qw), rows), pl.BlockSpec((qn, kvw), rows), pl.BlockSpec((qn, kvw), rows),
                  pl.BlockSpec((nb, lc, kvw), lambda s: (s, 0, 0)), pl.BlockSpec((nb, lc, kvw), lambda s: (s, 0, 0))],
        out_specs=pl.BlockSpec((qn, qw), lambda s: (s, 0)),
        out_shape=jax.ShapeDtypeStruct((nsteps * qn, qw), F32),
        compiler_params=_cparams(("arbitrary",)),
        name="attn_sample",
    )(sinks, q, k, v, kc, vc)


def _router_kernel(hp_ref, hs_ref, fn_ref, wh_ref, wl_ref, b_ref, tril_ref, xn_ref, info_ref, info_t_ref, cnt_ref,
                   carry_sc, *, n_p):
    h = jnp.where(pl.program_id(0) < n_p, hp_ref[...], hs_ref[...])
    _router_math(h, fn_ref, wh_ref, wl_ref, b_ref, tril_ref, xn_ref, info_ref, info_t_ref, cnt_ref, carry_sc)


def _wo_router_kernel(op_ref, os_ref, res_ref, wo_ref, fn_ref, wh_ref, wl_ref, b_ref, tril_ref,
                      h_ref, xn_ref, info_ref, info_t_ref, cnt_ref, carry_sc, *, n_p):
    o = jnp.where(pl.program_id(0) < n_p, op_ref[...], os_ref[...])
    h = res_ref[...] + jnp.dot(o.astype(BF16), wo_ref[...], preferred_element_type=F32)
    h_ref[...] = h
    _router_math(h, fn_ref, wh_ref, wl_ref, b_ref, tril_ref, xn_ref, info_ref, info_t_ref, cnt_ref, carry_sc)


def _router_math(h, fn_ref, wh_ref, wl_ref, b_ref, tril_ref, xn_ref, info_ref, info_t_ref, cnt_ref, carry_sc):
    i = pl.program_id(0)

    @pl.when(i == 0)
    def _():
        carry_sc[...] = jnp.zeros(carry_sc.shape, F32)

    xn = _rms_scale(h) * fn_ref[...]
    _store_token_major(xn_ref, xn)
    x_hi, x_lo = _split_bf16(xn, 2)
    wh, wl = wh_ref[...], wl_ref[...]
    logits = (jnp.dot(x_hi, wh, preferred_element_type=F32) + jnp.dot(x_hi, wl, preferred_element_type=F32)
              + jnp.dot(x_lo, wh, preferred_element_type=F32)) + b_ref[...]
    tm = logits.shape[0]
    lane = lax.broadcasted_iota(I32, (tm, LANES), 1).astype(F32)
    big = float(LANES)
    neg = -jnp.inf

    is_grp = (lane >= MOE_EXPERTS) & (lane < MOE_EXPERTS + MOE_GROUPS)
    lg = jnp.where(is_grp, logits, neg)
    mg = jnp.max(lg, axis=-1, keepdims=True)
    gp = 1.0 / jnp.sum(jnp.exp(lg - mg), axis=-1, keepdims=True)
    gi = jnp.min(jnp.where(lg == mg, lane, big), axis=-1, keepdims=True) - MOE_EXPERTS

    lo = gi * MOE_EXPERTS_PER_GROUP
    le = jnp.where((lane >= lo) & (lane < lo + MOE_EXPERTS_PER_GROUP), logits, neg)
    m1 = jnp.max(le, axis=-1, keepdims=True)
    i1 = jnp.min(jnp.where(le == m1, lane, big), axis=-1, keepdims=True)
    le2 = jnp.where(lane == i1, neg, le)
    m2 = jnp.max(le2, axis=-1, keepdims=True)
    i2 = jnp.min(jnp.where(le2 == m2, lane, big), axis=-1, keepdims=True)
    e2 = jnp.exp(m2 - m1)
    g1 = gp * (1.0 / (1.0 + e2))
    g2 = gp * (e2 / (1.0 + e2))

    a1 = lane == i1
    a2 = lane == i2
    onehot = jnp.where(a1 | a2, 1.0, 0.0)
    before = jnp.dot(tril_ref[...], onehot.astype(BF16), preferred_element_type=F32) + carry_sc[...]
    r1 = jnp.sum(jnp.where(a1, before, 0.0), axis=-1, keepdims=True)
    r2 = jnp.sum(jnp.where(a2, before, 0.0), axis=-1, keepdims=True)
    carry_sc[...] = carry_sc[...] + jnp.sum(onehot, axis=0, keepdims=True)
    cnt_ref[...] = carry_sc[...]

    cols = (i1, i2, g1, g2, r1, r2)
    info = jnp.zeros((tm, LANES), F32)
    for k, cval in enumerate(cols):
        info = jnp.where(lane == k, cval, info)
    info_ref[...] = info
    info_t_ref[...] = info.T[0:SUBLANES, :]


def _two_source_specs(tm, width, n_p):
    return [pl.BlockSpec((tm, width), lambda i: (jnp.minimum(i, n_p - 1), 0)),
            pl.BlockSpec((tm, width), lambda i: (jnp.maximum(i - n_p, 0), 0))]


def _router(h_p, h_s, fn, wh, wl, b, tril):
    d = h_p.shape[1]
    tm = TOKEN_TILE
    n_p = h_p.shape[0] // tm
    t = h_p.shape[0] + h_s.shape[0]
    rows = lambda i: (i, 0)
    return pl.pallas_call(
        functools.partial(_router_kernel, n_p=n_p),
        grid=(t // tm,),
        in_specs=_two_source_specs(tm, d, n_p) + [_const_spec(a.shape) for a in (fn, wh, wl, b, tril)],
        out_specs=[pl.BlockSpec((tm * ROW_TILES, LANES), rows), pl.BlockSpec((tm, LANES), rows),
                   pl.BlockSpec((SUBLANES, tm), lambda i: (0, i)), _const_spec((1, LANES))],
        out_shape=[jax.ShapeDtypeStruct((t * ROW_TILES, LANES), F32), jax.ShapeDtypeStruct((t, LANES), F32),
                   jax.ShapeDtypeStruct((SUBLANES, t), F32), jax.ShapeDtypeStruct((1, LANES), F32)],
        scratch_shapes=[pltpu.VMEM((1, LANES), F32)],
        compiler_params=_cparams(("arbitrary",)),
        name="moe_router",
    )(h_p, h_s, fn, wh, wl, b, tril)


def _wo_router(o_p, o_s, res, wo, fn, wh, wl, b, tril):
    t, d = res.shape
    tm = TOKEN_TILE
    n_p = o_p.shape[0] // tm
    rows = lambda i: (i, 0)
    return pl.pallas_call(
        functools.partial(_wo_router_kernel, n_p=n_p),
        grid=(t // tm,),
        in_specs=_two_source_specs(tm, o_p.shape[1], n_p) + [pl.BlockSpec((tm, d), rows)]
        + [_const_spec(a.shape) for a in (wo, fn, wh, wl, b, tril)],
        out_specs=[pl.BlockSpec((tm, d), rows), pl.BlockSpec((tm * ROW_TILES, LANES), rows),
                   pl.BlockSpec((tm, LANES), rows), pl.BlockSpec((SUBLANES, tm), lambda i: (0, i)),
                   _const_spec((1, LANES))],
        out_shape=[jax.ShapeDtypeStruct((t, d), F32), jax.ShapeDtypeStruct((t * ROW_TILES, LANES), F32),
                   jax.ShapeDtypeStruct((t, LANES), F32), jax.ShapeDtypeStruct((SUBLANES, t), F32),
                   jax.ShapeDtypeStruct((1, LANES), F32)],
        scratch_shapes=[pltpu.VMEM((1, LANES), F32)],
        compiler_params=_cparams(("arbitrary",)),
        name="wo_router",
    )(o_p, o_s, res, wo, fn, wh, wl, b, tril)


def _store_token_major(ref, x):
    n = x.shape[0]
    for j in range(ROW_TILES):
        ref[pl.ds(j, n, stride=ROW_TILES), :] = x[:, j * LANES:(j + 1) * LANES]


def _load_token_major(ref, n):
    return jnp.concatenate([ref[pl.ds(j, n, stride=ROW_TILES), :] for j in range(ROW_TILES)], axis=1)


def _dest_kernel(pst_ref, info_ref, dest_ref):
    info = info_ref[...]
    e = info[0:2, :]
    start = jnp.zeros(e.shape, F32)
    for k in range(MOE_EXPERTS):
        start = jnp.where(e == k, pst_ref[k].astype(F32), start)
    dest = (start + info[4:6, :]).astype(I32)
    dest_ref[...] = jnp.concatenate([dest, jnp.zeros((SUBLANES - 2, dest.shape[1]), I32)], axis=0)


def _dest(pstarts, info_t):
    return pl.pallas_call(
        _dest_kernel,
        in_specs=[pl.BlockSpec(memory_space=pltpu.SMEM), pl.BlockSpec(memory_space=pltpu.VMEM)],
        out_specs=pl.BlockSpec(memory_space=pltpu.VMEM),
        out_shape=jax.ShapeDtypeStruct(info_t.shape, I32),
        name="moe_dest",
    )(pstarts, info_t)


def _tile_copy(src, src_row, dst, dst_row, sem):
    return pltpu.make_async_copy(src.at[pl.ds(pl.multiple_of(src_row * ROW_TILES, ROW_TILES), ROW_TILES)],
                                 dst.at[pl.ds(pl.multiple_of(dst_row * ROW_TILES, ROW_TILES), ROW_TILES)], sem)


def _wait_tiles(ref, n_tokens, sem):
    blk = ref.at[pl.ds(0, n_tokens * ROW_TILES)]
    pltpu.make_async_copy(blk, blk, sem).wait()


def _dispatch_kernel(d1_ref, d2_ref, pend_ref, pcnt_ref, xn_ref, xbuf_hbm, zero_sc, sem_z, sem):
    i = pl.program_id(0)
    tm = xn_ref.shape[0] // ROW_TILES
    blk_rows = zero_sc.shape[0]

    @pl.when(i == 0)
    def _():
        zero_sc[...] = jnp.zeros(zero_sc.shape, F32)

        def zero_copy(e):
            start = pl.multiple_of(pend_ref[e] * ROW_TILES - blk_rows, ROW_TILES)
            return pltpu.make_async_copy(zero_sc, xbuf_hbm.at[pl.ds(start, blk_rows)], sem_z)

        for e in range(MOE_EXPERTS):
            @pl.when(pcnt_ref[e] > 0)
            def _():
                zero_copy(e).start()
        first_unused = pend_ref[MOE_EXPERTS - 1] * ROW_TILES // blk_rows
        n_blocks = xbuf_hbm.shape[0] // blk_rows

        def tail_copy(b):
            dst = xbuf_hbm.at[pl.ds(pl.multiple_of(b * blk_rows, blk_rows), blk_rows)]
            return pltpu.make_async_copy(zero_sc, dst, sem_z)

        def tail_start(b, carry):
            tail_copy(b).start()
            return carry

        def tail_wait(b, carry):
            tail_copy(b).wait()
            return carry
        lax.fori_loop(first_unused, n_blocks, tail_start, 0)
        for e in range(MOE_EXPERTS):
            @pl.when(pcnt_ref[e] > 0)
            def _():
                zero_copy(e).wait()
        lax.fori_loop(first_unused, n_blocks, tail_wait, 0)

    def body(r, carry):
        t = i * tm + r
        _tile_copy(xn_ref, r, xbuf_hbm, d1_ref[t], sem).start(priority=0)
        _tile_copy(xn_ref, r, xbuf_hbm, d2_ref[t], sem).start(priority=1)
        return carry
    lax.fori_loop(0, tm, body, 0, unroll=8)
    _wait_tiles(xn_ref, tm, sem)
    _wait_tiles(xn_ref, tm, sem)


def _dispatch(dest1, dest2, pends, pcounts, xn_tm, n_rows):
    n_tok = dest1.shape[0]
    tm = DISPATCH_TILE
    grid_spec = pltpu.PrefetchScalarGridSpec(
        num_scalar_prefetch=4,
        grid=(n_tok // tm,),
        in_specs=[pl.BlockSpec((tm * ROW_TILES, LANES), lambda i, *_: (i, 0))],
        out_specs=pl.BlockSpec(memory_space=pl.ANY),
        scratch_shapes=[pltpu.VMEM((EXPERT_ROWS * ROW_TILES, LANES), F32), pltpu.SemaphoreType.DMA(()),
                        pltpu.SemaphoreType.DMA(())],
    )
    return pl.pallas_call(
        _dispatch_kernel,
        grid_spec=grid_spec,
        out_shape=jax.ShapeDtypeStruct((n_rows * ROW_TILES, LANES), F32),
        compiler_params=_cparams(("arbitrary",)),
        name="moe_dispatch",
    )(dest1, dest2, pends, pcounts, xn_tm)


def _expert_kernel(blk_e_ref, nvalid_ref, next_e_ref, x_ref, wg_hbm, wu_hbm, wd_hbm, y_ref,
                   wg_sc, wu_sc, wd_sc, wg_st, wu_st, wd_st, sem, *, layer):
    i = pl.program_id(0)
    nv = nvalid_ref[0]
    rows = x_ref.shape[0] // ROW_TILES

    def weight_copies(e):
        return (pltpu.make_async_copy(wg_hbm.at[layer, e], wg_st, sem.at[0]),
                pltpu.make_async_copy(wu_hbm.at[layer, e], wu_st, sem.at[1]),
                pltpu.make_async_copy(wd_hbm.at[layer, e], wd_st, sem.at[2]))

    @pl.when((i == 0) & (nv > 0))
    def _():
        for c in weight_copies(blk_e_ref[0]):
            c.start()

    @pl.when(i < nv)
    def _():
        e = blk_e_ref[i]
        e_prev = blk_e_ref[jnp.maximum(i - 1, 0)]

        @pl.when((i == 0) | (e != e_prev))
        def _():
            for c in weight_copies(e):
                c.wait()
            wg_sc[...] = wg_st[...].astype(BF16)
            wu_sc[...] = wu_st[...].astype(BF16)
            wd_sc[...] = wd_st[...].astype(BF16)
            nxt = next_e_ref[e]

            @pl.when(nxt < MOE_EXPERTS)
            def _():
                for c in weight_copies(nxt):
                    c.start()

        x = _load_token_major(x_ref, rows).astype(BF16)
        hid = _silu(jnp.dot(x, wg_sc[...], preferred_element_type=F32)) * jnp.dot(
            x, wu_sc[...], preferred_element_type=F32)
        _store_token_major(y_ref, jnp.dot(hid.astype(BF16), wd_sc[...], preferred_element_type=F32))

    @pl.when(i >= nv)
    def _():
        y_ref[...] = jnp.zeros(y_ref.shape, F32)


def _experts(blk_e, nvalid, next_e, xbuf, wg, wu, wd, layer):
    nblk = blk_e.shape[0]
    d, hdim = wg.shape[2], wg.shape[3]
    rows = EXPERT_ROWS * ROW_TILES
    grid_spec = pltpu.PrefetchScalarGridSpec(
        num_scalar_prefetch=3,
        grid=(nblk,),
        in_specs=[pl.BlockSpec((rows, LANES), lambda i, be, nv, ne: (jnp.minimum(i, nv[0] - 1), 0)),
                  pl.BlockSpec(memory_space=pl.ANY), pl.BlockSpec(memory_space=pl.ANY),
                  pl.BlockSpec(memory_space=pl.ANY)],
        out_specs=pl.BlockSpec((rows, LANES), lambda i, be, nv, ne: (i, 0)),
        scratch_shapes=[pltpu.VMEM((d, hdim), BF16), pltpu.VMEM((d, hdim), BF16), pltpu.VMEM((hdim, d), BF16),
                        pltpu.VMEM((d, hdim), F32), pltpu.VMEM((d, hdim), F32), pltpu.VMEM((hdim, d), F32),
                        pltpu.SemaphoreType.DMA((3,))],
    )
    return pl.pallas_call(
        functools.partial(_expert_kernel, layer=layer),
        grid_spec=grid_spec,
        out_shape=jax.ShapeDtypeStruct((nblk * rows, LANES), F32),
        compiler_params=_cparams(("arbitrary",)),
        name="moe_experts",
    )(blk_e, nvalid, next_e, xbuf, wg, wu, wd)


def _gather_moe_rows(d1_ref, d2_ref, y_hbm, r_sc, sem, tm, row0=0, n_parts=1):
    i = pl.program_id(0)
    n = pl.num_programs(0)
    per = tm // n_parts

    def start(step, part):
        slot = step % 2
        base = row0 + step * tm + part * per

        def body(r, carry):
            _tile_copy(y_hbm, d1_ref[base + r], r_sc.at[slot, 0], part * per + r, sem.at[slot]).start(priority=0)
            _tile_copy(y_hbm, d2_ref[base + r], r_sc.at[slot, 1], part * per + r, sem.at[slot]).start(priority=1)
            return carry
        lax.fori_loop(0, per, body, 0, unroll=8)

    @pl.when(i == 0)
    def _():
        for part in range(n_parts):
            start(i, part)

    def issue_next(part):
        @pl.when(i + 1 < n)
        def _():
            start(i + 1, part)

    if n_parts == 1:
        issue_next(0)
    slot = i % 2
    _wait_tiles(r_sc.at[slot, 0], tm, sem.at[slot])
    _wait_tiles(r_sc.at[slot, 1], tm, sem.at[slot])
    return _load_token_major(r_sc.at[slot, 0], tm), _load_token_major(r_sc.at[slot, 1], tm), issue_next


def _combine_kernel(d1_ref, d2_ref, h_ref, info_ref, y_hbm, o_ref, r_sc, sem, *, row0):
    y1, y2, _ = _gather_moe_rows(d1_ref, d2_ref, y_hbm, r_sc, sem, h_ref.shape[0], row0)
    info = info_ref[...]
    o_ref[...] = h_ref[...] + (y1 * info[:, 2:3] + y2 * info[:, 3:4])


def _combine(dest1, dest2, h, info, ybuf, row0, nrows):
    d = h.shape[1]
    tm = TOKEN_TILE
    base_tile = row0 // tm
    rows = lambda i, a, b: (base_tile + i, 0)
    grid_spec = pltpu.PrefetchScalarGridSpec(
        num_scalar_prefetch=2,
        grid=(nrows // tm,),
        in_specs=[pl.BlockSpec((tm, d), rows), pl.BlockSpec((tm, LANES), rows), pl.BlockSpec(memory_space=pl.ANY)],
        out_specs=pl.BlockSpec((tm, d), lambda i, a, b: (i, 0)),
        scratch_shapes=[pltpu.VMEM((2, 2, tm * ROW_TILES, LANES), F32), pltpu.SemaphoreType.DMA((2,))],
    )
    return pl.pallas_call(
        functools.partial(_combine_kernel, row0=row0),
        grid_spec=grid_spec,
        out_shape=jax.ShapeDtypeStruct((nrows, d), F32),
        compiler_params=_cparams(("arbitrary",)),
        name="moe_combine",
    )(dest1, dest2, h, info, ybuf)


def _combine_qkv_kernel(d1_ref, d2_ref, hp_ref, hs_ref, info_ref, y_hbm, kvn_ref, an_ref, wkv_ref, wq_ref, kn_ref,
                        qn_ref, rc_ref, rs1_ref, rs2_ref, hsum_ref, rq_ref, eq_ref,
                        h_ref, q_ref, k_ref, v_ref, r_sc, sem, *, n_p):
    y1, y2, issue_next = _gather_moe_rows(d1_ref, d2_ref, y_hbm, r_sc, sem, h_ref.shape[0], n_parts=QKV_PHASES)
    info = info_ref[...]
    h = jnp.where(pl.program_id(0) < n_p, hp_ref[...], hs_ref[...]) + (y1 * info[:, 2:3] + y2 * info[:, 3:4])
    h_ref[...] = h
    _qkv_math(h, kvn_ref, an_ref, wkv_ref, wq_ref, kn_ref, qn_ref, rc_ref, rs1_ref, rs2_ref,
              hsum_ref, rq_ref, eq_ref, q_ref, k_ref, v_ref, issue_next)


def _combine_qkv(dest1, dest2, h_p, h_s, info, ybuf, kvn, an, wkv, wq, knt, qnt, rc, rs1, rs2, hsum, rq, eq):
    d = h_p.shape[1]
    tm = TOKEN_TILE
    n_p = h_p.shape[0] // tm
    t = h_p.shape[0] + h_s.shape[0]
    kvw = wkv.shape[1] // 2
    qw = wq.shape[1]
    rows = lambda i, a, b: (i, 0)
    const = lambda arr: pl.BlockSpec(arr.shape, lambda i, a, b: (0,) * arr.ndim)
    grid_spec = pltpu.PrefetchScalarGridSpec(
        num_scalar_prefetch=2,
        grid=(t // tm,),
        in_specs=[pl.BlockSpec((tm, d), lambda i, a, b: (jnp.minimum(i, n_p - 1), 0)),
                  pl.BlockSpec((tm, d), lambda i, a, b: (jnp.maximum(i - n_p, 0), 0)),
                  pl.BlockSpec((tm, LANES), rows), pl.BlockSpec(memory_space=pl.ANY)]
        + [const(a) for a in (kvn, an, wkv, wq, knt, qnt)] + [pl.BlockSpec((tm, LANES), rows)] * 3
        + [const(a) for a in (hsum, rq, eq)],
        out_specs=[pl.BlockSpec((tm, d), rows), pl.BlockSpec((tm, qw), rows), pl.BlockSpec((tm, kvw), rows),
                   pl.BlockSpec((tm, kvw), rows)],
        scratch_shapes=[pltpu.VMEM((2, 2, tm * ROW_TILES, LANES), F32), pltpu.SemaphoreType.DMA((2,))],
    )
    return pl.pallas_call(
        functools.partial(_combine_qkv_kernel, n_p=n_p),
        grid_spec=grid_spec,
        out_shape=[jax.ShapeDtypeStruct((t, d), F32), jax.ShapeDtypeStruct((t, qw), F32),
                   jax.ShapeDtypeStruct((t, kvw), F32), jax.ShapeDtypeStruct((t, kvw), F32)],
        compiler_params=_cparams(("arbitrary",)),
        name="combine_qkv",
    )(dest1, dest2, h_p, h_s, info, ybuf, kvn, an, wkv, wq, knt, qnt, rc, rs1, rs2, hsum, rq, eq)


def _router_weights(w_grp, b_grp, w_rt, b_rt):
    d = w_rt.shape[0]
    w_cat = jnp.zeros((d, LANES), F32).at[:, :MOE_EXPERTS].set(w_rt).at[:, MOE_EXPERTS:MOE_EXPERTS + MOE_GROUPS].set(w_grp)
    b_cat = jnp.zeros((1, LANES), F32).at[0, :MOE_EXPERTS].set(b_rt).at[0, MOE_EXPERTS:MOE_EXPERTS + MOE_GROUPS].set(b_grp)
    w_hi = w_cat.astype(BF16)
    w_lo = (w_cat - w_hi.astype(F32)).astype(BF16)
    return w_hi, w_lo, b_cat


def _moe_experts(xn_tm, info_t, cnt, wg, wu, wd, layer):
    t = info_t.shape[1]
    counts = cnt[0, :MOE_EXPERTS].astype(I32)
    pcounts = (counts + EXPERT_ROWS - 1) // EXPERT_ROWS * EXPERT_ROWS
    pends = jnp.cumsum(pcounts)
    pstarts = pends - pcounts
    nblk = -(-(2 * t + MOE_EXPERTS * (EXPERT_ROWS - 1)) // EXPERT_ROWS)
    blk_start = jnp.arange(nblk, dtype=I32) * EXPERT_ROWS
    blk_e = jnp.minimum(jnp.sum((pends[None, :] <= blk_start[:, None]).astype(I32), axis=1), MOE_EXPERTS - 1)
    nvalid = pends[-1:] // EXPERT_ROWS
    eid = jnp.arange(MOE_EXPERTS, dtype=I32)
    later = (eid[None, :] > eid[:, None]) & (pcounts[None, :] > 0)
    next_e = jnp.min(jnp.where(later, eid[None, :], MOE_EXPERTS), axis=1).astype(I32)

    dest = _dest(pstarts, info_t)
    dest1, dest2 = dest[0], dest[1]
    xbuf = _dispatch(dest1, dest2, pends, pcounts, xn_tm, nblk * EXPERT_ROWS)
    return dest1, dest2, _experts(blk_e, nvalid, next_e, xbuf, wg, wu, wd, layer)


def _rope_tables(pos):
    half = ROT_DIM // 2
    inv = ROPE_THETA ** (-jnp.arange(0, ROT_DIM, 2, dtype=F32) / ROT_DIM)
    ang = pos.astype(F32)[:, None] * inv[None, :]
    cos, sin = jnp.cos(ang), jnp.sin(ang)
    n = pos.shape[0]
    ones = jnp.ones((n, ATT_HEAD_DIM - ROT_DIM), F32)
    zeros_r = jnp.zeros((n, ATT_HEAD_DIM - ROT_DIM), F32)
    zeros_h = jnp.zeros((n, half), F32)
    c = jnp.concatenate([cos, cos, ones], axis=1)
    s1 = jnp.concatenate([-sin, zeros_h, zeros_r], axis=1)
    s2 = jnp.concatenate([zeros_h, sin, zeros_r], axis=1)
    reps = LANES // ATT_HEAD_DIM
    return tuple(jnp.tile(a, (1, reps)) for a in (c, s1, s2))


def _to_step_order(a, nsteps, n_tok):
    c = a.shape[-1]
    return a.reshape(nsteps, SEQ_PER_STEP, n_tok, c).transpose(0, 2, 1, 3).reshape(nsteps * n_tok * SEQ_PER_STEP, c)


def _from_step_order(a, nsteps, n_tok):
    c = a.shape[-1]
    return a.reshape(nsteps, n_tok, SEQ_PER_STEP, c).transpose(0, 2, 1, 3).reshape(nsteps * SEQ_PER_STEP, n_tok, c)


def kernel(x_prompt, x_sample, state_ssm, state_conv, cache_k_win, cache_v_win, ssm_norm, ssm_w_in, ssm_conv_w, ssm_conv_b, ssm_dt_bias, ssm_a_log, ssm_d, ssm_gate_norm, ssm_w_out, kv_norm, w_kv, k_norm, attn_norm, w_q, q_norm, sinks, w_o, ffn_norm, moe_w_group, moe_b_group, moe_w_router, moe_b_router, moe_w_gate, moe_w_up, moe_w_down):
    bp, seq, d = x_prompt.shape
    bs, n_tok, _ = x_sample.shape
    tp, ts = bp * seq, bs * n_tok
    nsteps = bs // SEQ_PER_STEP
    n_heads = ssm_d.shape[1]
    di = n_heads * SSM_HEAD_DIM
    gn_w = SSM_GROUPS * SSM_STATE
    cdim = di + 2 * gn_w
    n_q = sinks.shape[1]
    kvw = ATT_KV_HEADS * ATT_HEAD_DIM

    xp2 = x_prompt.reshape(tp, d)
    xs2 = _to_step_order(x_sample, nsteps, n_tok)

    lane_i = jnp.arange(LANES)
    e01 = (lane_i[:, None] == (jnp.arange(di) // SSM_HEAD_DIM)[None, :]).astype(BF16)
    hpg = di // SSM_GROUPS
    g1 = ((jnp.arange(gn_w) // SSM_STATE)[:, None] == (lane_i // (hpg // SSM_HEAD_DIM))[None, :])
    g1 = (g1 & (lane_i < n_heads)[None, :]).astype(BF16)
    tril_c = (jnp.arange(SSM_CHUNK)[:, None] >= jnp.arange(SSM_CHUNK)[None, :]).astype(BF16)
    tril_x = (jnp.arange(TOKEN_TILE)[:, None] > jnp.arange(TOKEN_TILE)[None, :]).astype(BF16)
    hsum = ((jnp.arange(kvw) // ATT_HEAD_DIM)[:, None] == (jnp.arange(kvw) // ATT_HEAD_DIM)[None, :]).astype(BF16)
    qw = n_q * ATT_HEAD_DIM
    rq = ((jnp.arange(qw) // ATT_HEAD_DIM)[:, None] == lane_i[None, :]).astype(BF16)
    eq = rq.T

    w_in = ssm_w_in[0]
    wz = w_in[:, :di].astype(BF16)
    wx = w_in[:, di:di + cdim].astype(BF16)
    wd = jnp.zeros((d, LANES), F32).at[:, :n_heads].set(w_in[:, di + cdim:]).astype(BF16)
    cw, cb = ssm_conv_w[0], ssm_conv_b[0].reshape(1, cdim)
    z, xbc, dt = _inproj(xp2, xs2, ssm_norm[0].reshape(1, d), wz, wx, wd)

    pad_h = lambda v: jnp.zeros((1, LANES), F32).at[0, :n_heads].set(v)
    dtb, alog = pad_h(ssm_dt_bias[0]), pad_h(ssm_a_log[0])
    dsk = jnp.repeat(ssm_d[0], SSM_HEAD_DIM).reshape(1, di)
    gnw = ssm_gate_norm[0].reshape(1, di)

    w_out = ssm_w_out[0].astype(BF16)
    h_p, s_fin, c_fin = _ssd_prompt(z, xbc, dt, xp2, cw, cb, dtb, alog, dsk, gnw, tril_c, e01, w_out, bp, seq)
    ssm_p = s_fin.reshape(1, bp, n_heads, SSM_HEAD_DIM, SSM_STATE)
    conv_p = c_fin[:, SUBLANES - (SSM_CONV - 1):, :].reshape(1, bp, SSM_CONV - 1, cdim)

    xbc_s = xbc[tp:].reshape(nsteps, n_tok, SEQ_PER_STEP, cdim)
    conv_in = state_conv[0].reshape(nsteps, SEQ_PER_STEP, SSM_CONV - 1, cdim).transpose(0, 2, 1, 3)
    xp7 = jnp.concatenate([conv_in, xbc_s], axis=1)
    s0 = state_ssm[0].reshape(bs, di, SSM_STATE)
    h_s, s_new = _ssd_sample(z, xp7, dt, s0, xs2, cw, cb, dtb, alog, dsk, gnw, e01, g1, w_out, tp, n_tok)
    ssm_s = s_new.reshape(1, bs, n_heads, SSM_HEAD_DIM, SSM_STATE)
    conv_s = _from_step_order(xbc[tp:], nsteps, n_tok)[:, n_tok - (SSM_CONV - 1):, :].reshape(
        1, bs, SSM_CONV - 1, cdim)

    w_hi, w_lo, b_cat = _router_weights(moe_w_group[0], moe_b_group[0], moe_w_router[0], moe_b_router[0])
    xn_tm, info, info_t, cnt = _router(h_p, h_s, ffn_norm[0].reshape(1, d), w_hi, w_lo, b_cat, tril_x)
    dest1, dest2, ybuf = _moe_experts(xn_tm, info_t, cnt, moe_w_gate, moe_w_up, moe_w_down, 0)

    pos = jnp.concatenate([jnp.tile(jnp.arange(seq, dtype=I32), bp),
                           jnp.tile(jnp.repeat(PAST_LEN + jnp.arange(n_tok, dtype=I32), SEQ_PER_STEP), nsteps)])
    rc, rs1, rs2 = _rope_tables(pos)
    h1, q, k, v = _combine_qkv(dest1, dest2, h_p, h_s, info, ybuf, kv_norm.reshape(1, d), attn_norm[0].reshape(1, d),
                               w_kv.astype(BF16), w_q[0].astype(BF16),
                               jnp.tile(k_norm, ATT_KV_HEADS).reshape(1, kvw), jnp.tile(q_norm[0], n_q).reshape(1, qw),
                               rc, rs1, rs2, hsum, rq, eq)
    sk = sinks[0]
    lc = cache_k_win.shape[1]
    kc = cache_k_win.reshape(bs, lc, kvw)
    vc = cache_v_win.reshape(bs, lc, kvw)
    o_p = _attn_prompt(sk, q, k, v, bp, seq)
    o_s = _attn_sample(sk, q, k, v, kc, vc, tp, n_tok)
    w_hi, w_lo, b_cat = _router_weights(moe_w_group[1], moe_b_group[1], moe_w_router[1], moe_b_router[1])
    h2, xn_tm, info, info_t, cnt = _wo_router(o_p, o_s, h1, w_o[0].astype(BF16), ffn_norm[1].reshape(1, d),
                                             w_hi, w_lo, b_cat, tril_x)
    dest1, dest2, ybuf = _moe_experts(xn_tm, info_t, cnt, moe_w_gate, moe_w_up, moe_w_down, 1)
    y_p = _combine(dest1, dest2, h2, info, ybuf, 0, tp)
    y_s = _combine(dest1, dest2, h2, info, ybuf, tp, ts)

    wl = min(WINDOW, seq)
    k_p = k[:tp].reshape(bp, seq, kvw)[:, seq - wl:].reshape(bp, wl, ATT_KV_HEADS, ATT_HEAD_DIM)
    v_p = v[:tp].reshape(bp, seq, kvw)[:, seq - wl:].reshape(bp, wl, ATT_KV_HEADS, ATT_HEAD_DIM)
    k_new = _from_step_order(k[tp:], nsteps, n_tok)
    v_new = _from_step_order(v[tp:], nsteps, n_tok)
    k_s = jnp.concatenate([kc, k_new], axis=1)[:, n_tok:].reshape(bs, lc, ATT_KV_HEADS, ATT_HEAD_DIM)
    v_s = jnp.concatenate([vc, v_new], axis=1)[:, n_tok:].reshape(bs, lc, ATT_KV_HEADS, ATT_HEAD_DIM)
    return (y_p.reshape(bp, seq, d), _from_step_order(y_s, nsteps, n_tok),
            ssm_p, conv_p, k_p, v_p, ssm_s, conv_s, k_s, v_s)
```

```python
import functools

import jax
import jax.numpy as jnp
from jax import lax
from jax.experimental import pallas as pl
from jax.experimental.pallas import tpu as pltpu

F32 = jnp.float32
BF16 = jnp.bfloat16
I32 = jnp.int32

EPS = 1e-6
SSM_HEAD_DIM = 64
SSM_GROUPS = 4
SSM_STATE = 128
SSM_CONV = 4
SSM_CHUNK = 128
ATT_HEAD_DIM = 64
ATT_KV_HEADS = 4
WINDOW = 128
ROT_DIM = ATT_HEAD_DIM // 4
ROPE_THETA = 500000.0
MOE_GROUPS = 4
MOE_EXPERTS_PER_GROUP = 8
MOE_EXPERTS = MOE_GROUPS * MOE_EXPERTS_PER_GROUP
MOE_BLOCK = 128
PAST_LEN = 16384

LANES = 128
SUBLANES = 8
BF16_ROWS = 16
SEQ_PER_STEP = SUBLANES
TOKEN_TILE = 256
ROW_TILES = 8
ATTN_STACK = 2
DISPATCH_TILE = 512
EXPERT_RING = 3
EXPERT_ROWS = 512
VMEM_LIMIT = 56 * 1024 * 1024


def _cparams(sem):
    return pltpu.CompilerParams(dimension_semantics=sem, vmem_limit_bytes=VMEM_LIMIT)


def _const_spec(shape):
    nd = len(shape)
    return pl.BlockSpec(shape, lambda *_: (0,) * nd)


def _split_bf16(v, n):
    parts = []
    r = v
    for k in range(n):
        p = r.astype(BF16)
        parts.append(p)
        if k + 1 < n:
            r = r - p.astype(F32)
    return parts


def _mm01(v, m01, n=3):
    acc = None
    for p in _split_bf16(v, n):
        d = jnp.dot(p, m01, preferred_element_type=F32)
        acc = d if acc is None else acc + d
    return acc


def _mm01_left(m01, v, n=3):
    acc = None
    for p in _split_bf16(v, n):
        d = jnp.dot(m01, p, preferred_element_type=F32)
        acc = d if acc is None else acc + d
    return acc


def _dot_nt(a, b):
    return lax.dot_general(a, b, (((1,), (1,)), ((), ())), preferred_element_type=F32)


def _sigmoid(x):
    return 0.5 * jnp.tanh(0.5 * x) + 0.5


def _silu(x):
    return x * _sigmoid(x)


def _softplus(x):
    return jnp.maximum(x, 0.0) + jnp.log1p(jnp.exp(-jnp.abs(x)))


def _rms_scale(x):
    return x * lax.rsqrt(jnp.mean(x * x, axis=-1, keepdims=True) + EPS)


def _gate_norm(y, z, gn, n_groups):
    yz = y * _silu(z)
    w = yz.shape[-1] // n_groups
    outs = []
    for g in range(n_groups):
        v = yz[:, g * w:(g + 1) * w]
        outs.append(_rms_scale(v) * gn[:, g * w:(g + 1) * w])
    return jnp.concatenate(outs, axis=1)


def _rope(x, c, s1, s2):
    w = x.shape[-1]
    return x * c + pltpu.roll(x, w - ROT_DIM // 2, 1) * s1 + pltpu.roll(x, ROT_DIM // 2, 1) * s2


def _tile_lanes(t, reps):
    return t if reps == 1 else jnp.concatenate([t] * reps, axis=1)


def _inproj_kernel(xp_ref, xs_ref, g_ref, wz_ref, wx_ref, wd_ref, z_ref, xbc_ref, dt_ref, *, n_p):
    i = pl.program_id(0)
    x = jnp.where(i < n_p, xp_ref[...], xs_ref[...])
    xn = (_rms_scale(x) * g_ref[...]).astype(BF16)
    z_ref[...] = jnp.dot(xn, wz_ref[...], preferred_element_type=F32)
    xbc_ref[...] = jnp.dot(xn, wx_ref[...], preferred_element_type=F32)
    dt_ref[...] = jnp.dot(xn, wd_ref[...], preferred_element_type=F32)


def _inproj(xp2, xs2, g, wz, wx, wd):
    tp, d = xp2.shape
    ts = xs2.shape[0]
    tm = TOKEN_TILE
    n_p, n_s = tp // tm, ts // tm
    t = tp + ts
    return pl.pallas_call(
        functools.partial(_inproj_kernel, n_p=n_p),
        grid=(n_p + n_s,),
        in_specs=_two_source_specs(tm, d, n_p) + [_const_spec(a.shape) for a in (g, wz, wx, wd)],
        out_specs=[
            pl.BlockSpec((tm, wz.shape[1]), lambda i: (i, 0)),
            pl.BlockSpec((tm, wx.shape[1]), lambda i: (i, 0)),
            pl.BlockSpec((tm, wd.shape[1]), lambda i: (i, 0)),
        ],
        out_shape=[jax.ShapeDtypeStruct((t, wz.shape[1]), F32),
                   jax.ShapeDtypeStruct((t, wx.shape[1]), F32),
                   jax.ShapeDtypeStruct((t, wd.shape[1]), F32)],
        compiler_params=_cparams(("arbitrary",)),
        name="inproj",
    )(xp2, xs2, g, wz, wx, wd)


def _ssd_prompt_kernel(z_ref, xbc_ref, dt_ref, x_ref, cw_ref, cb_ref, dtb_ref, alog_ref, dsk_ref, gn_ref,
                       tril_ref, e_ref, wout_ref, h_ref, sfin_ref, cfin_ref, xpad_sc, st_sc):
    c = pl.program_id(1)
    q = SSM_CHUNK
    cd = xbc_ref.shape[1]
    di = z_ref.shape[1]
    gn_w = SSM_GROUPS * SSM_STATE
    hpg = di // SSM_GROUPS
    pad = SUBLANES

    n_slab = cd // LANES

    @pl.when(c == 0)
    def _():
        xpad_sc[:, 0:pad, :] = jnp.zeros((n_slab, pad, LANES), F32)
        st_sc[...] = jnp.zeros(st_sc.shape, F32)

    @pl.when(c > 0)
    def _():
        xpad_sc[:, 0:pad, :] = xpad_sc[:, q:q + pad, :]

    slabs = []
    for j in range(n_slab):
        ls = slice(j * LANES, (j + 1) * LANES)
        xpad_sc[j, pad:pad + q, :] = xbc_ref[:, ls]
        acc = cb_ref[:, ls]
        for k in range(SSM_CONV):
            off = pad - (SSM_CONV - 1) + k
            acc = acc + xpad_sc[j, off:off + q, :] * cw_ref[k:k + 1, ls]
        slabs.append(_silu(acc))
    xc = jnp.concatenate(slabs, axis=1)
    xs = xc[:, :di]
    bm = xc[:, di:di + gn_w]
    cm = xc[:, di + gn_w:]

    dt = _softplus(dt_ref[...] + dtb_ref[...])
    a = -jnp.exp(alog_ref[...])
    act = _mm01_left(tril_ref[...], dt * a)
    act_t = act.T
    act_last = act[q - 1:q, :]
    pieces = (_split_bf16(dt, 2) + _split_bf16(jnp.exp(act_last - act), 2) + _split_bf16(jnp.exp(act), 2)
              + _split_bf16(jnp.exp(act[q - BF16_ROWS:q, :]), 3))
    ex = jnp.dot(jnp.concatenate(pieces, axis=0), e_ref[...], preferred_element_type=F32)
    xdt = xs * (ex[0:q] + ex[q:2 * q])
    xd = xdt * (ex[2 * q:3 * q] + ex[3 * q:4 * q])
    eax = ex[4 * q:5 * q] + ex[5 * q:6 * q]
    tail = ex[6 * q:]
    cd = (tail[0:BF16_ROWS] + tail[BF16_ROWS:2 * BF16_ROWS]) + tail[2 * BF16_ROWS:]
    cdx = cd[BF16_ROWS - 1:BF16_ROWS, :]

    row = lax.broadcasted_iota(I32, (q, q), 0)
    col = lax.broadcasted_iota(I32, (q, q), 1)
    causal = row >= col
    lane = lax.broadcasted_iota(I32, (q, LANES), 1)
    lo_half = lane < SSM_HEAD_DIM

    y_parts = []
    heads_per_group = hpg // SSM_HEAD_DIM
    for g in range(SSM_GROUPS):
        cg = cm[:, g * SSM_STATE:(g + 1) * SSM_STATE].astype(BF16)
        bg = bm[:, g * SSM_STATE:(g + 1) * SSM_STATE]
        cb = _dot_nt(cg, bg.astype(BF16))
        st_g = st_sc[:, g * hpg:(g + 1) * hpg]
        y_off = jnp.dot(cg, st_g.astype(BF16), preferred_element_type=F32)
        for pr in range(heads_per_group // 2):
            h0 = g * heads_per_group + 2 * pr
            ms = []
            for h in (h0, h0 + 1):
                seg = act[:, h:h + 1] - act_t[h:h + 1, :]
                lm = jnp.exp(jnp.where(causal, seg, -jnp.inf))
                ms.append((cb * lm).astype(BF16))
            m2 = jnp.concatenate(ms, axis=1)
            xpair = xdt[:, h0 * SSM_HEAD_DIM:(h0 + 2) * SSM_HEAD_DIM]
            rhs = jnp.concatenate([jnp.where(lo_half, xpair, 0.0),
                                   jnp.where(lo_half, 0.0, xpair)], axis=0).astype(BF16)
            y_d = jnp.dot(m2, rhs, preferred_element_type=F32)
            lo = 2 * pr * SSM_HEAD_DIM
            y_parts.append(y_d + y_off[:, lo:lo + LANES] * eax[:, g * hpg + lo:g * hpg + lo + LANES])
        upd = jnp.dot(bg.T.astype(BF16), xd[:, g * hpg:(g + 1) * hpg].astype(BF16),
                      preferred_element_type=F32)
        st_sc[:, g * hpg:(g + 1) * hpg] = st_g * cdx[:, g * hpg:(g + 1) * hpg] + upd

    y = jnp.concatenate(y_parts, axis=1) + xs * dsk_ref[...]
    yg = _gate_norm(y, z_ref[...], gn_ref[...], SSM_GROUPS)
    h_ref[...] = x_ref[...] + jnp.dot(yg.astype(BF16), wout_ref[...], preferred_element_type=F32)

    @pl.when(c == pl.num_programs(1) - 1)
    def _():
        sfin_ref[0] = st_sc[...].T
        cfin_ref[0] = jnp.concatenate([xpad_sc[j, q:q + pad, :] for j in range(n_slab)], axis=1)


def _ssd_prompt(z, xbc, dt, x, cw, cb, dtb, alog, dsk, gnw, tril, e01, wout, bp, seq):
    nc = seq // SSM_CHUNK
    q = SSM_CHUNK
    di, cd, d = z.shape[1], xbc.shape[1], x.shape[1]
    rows = lambda b, c: (b * nc + c, 0)
    return pl.pallas_call(
        _ssd_prompt_kernel,
        grid=(bp, nc),
        in_specs=[
            pl.BlockSpec((q, di), rows), pl.BlockSpec((q, cd), rows), pl.BlockSpec((q, LANES), rows),
            pl.BlockSpec((q, d), rows),
            _const_spec(cw.shape), _const_spec(cb.shape), _const_spec(dtb.shape), _const_spec(alog.shape),
            _const_spec(dsk.shape), _const_spec(gnw.shape), _const_spec(tril.shape), _const_spec(e01.shape),
            _const_spec(wout.shape),
        ],
        out_specs=[
            pl.BlockSpec((q, d), rows),
            pl.BlockSpec((1, di, SSM_STATE), lambda b, c: (b, 0, 0)),
            pl.BlockSpec((1, SUBLANES, cd), lambda b, c: (b, 0, 0)),
        ],
        out_shape=[jax.ShapeDtypeStruct((bp * seq, d), F32),
                   jax.ShapeDtypeStruct((bp, di, SSM_STATE), F32),
                   jax.ShapeDtypeStruct((bp, SUBLANES, cd), F32)],
        scratch_shapes=[pltpu.VMEM((cd // LANES, q + 2 * SUBLANES, LANES), F32), pltpu.VMEM((SSM_STATE, di), F32)],
        compiler_params=_cparams(("arbitrary", "arbitrary")),
        name="ssd_prompt",
    )(z, xbc, dt, x, cw, cb, dtb, alog, dsk, gnw, tril, e01, wout)


def _ssd_sample_kernel(z_ref, xp_ref, dt_ref, s0_ref, x_ref, cw_ref, cb_ref, dtb_ref, alog_ref, dsk_ref, gn_ref,
                       e_ref, g1_ref, wout_ref, h_ref, sn_ref, yoff_sc, *, n_tok):
    hf = pl.program_id(1)
    nb = SEQ_PER_STEP
    half = nb // 2
    q = n_tok * nb
    di = z_ref.shape[1]
    gn_w = SSM_GROUPS * SSM_STATE
    hpg = di // SSM_GROUPS

    taps = [xp_ref[0, m] for m in range(n_tok + SSM_CONV - 1)]
    slabs = []
    for t in range(n_tok):
        acc = cb_ref[...]
        for k in range(SSM_CONV):
            acc = acc + taps[t + k] * cw_ref[k:k + 1, :]
        slabs.append(_silu(acc))
    xc = jnp.concatenate(slabs, axis=0)
    xs = xc[:, :di]
    bm = xc[:, di:di + gn_w]
    cm = xc[:, di + gn_w:]

    dt = _softplus(dt_ref[...] + dtb_ref[...])
    da = dt * (-jnp.exp(alog_ref[...]))
    acts = []
    run = None
    for t in range(n_tok):
        d = da[t * nb:(t + 1) * nb, :]
        run = d if run is None else run + d
        acts.append(run)
    act = jnp.concatenate(acts, axis=0)
    act_last = jnp.concatenate([acts[-1]] * n_tok, axis=0)
    e01 = e_ref[...]
    xdt = xs * _mm01(dt, e01)
    xd = xdt * _mm01(jnp.exp(act_last - act), e01)
    eax = _mm01(jnp.exp(act), e01)
    cdx = _mm01(jnp.exp(acts[-1]), e01)

    pairs = [(t, u) for t in range(n_tok) for u in range(t + 1)]
    cbp = jnp.concatenate([cm[t * nb:(t + 1) * nb, :] * bm[u * nb:(u + 1) * nb, :] for t, u in pairs], axis=0)
    seg = jnp.concatenate([acts[t] - acts[u] for t, u in pairs], axis=0)
    coef = _mm01(_mm01(cbp, g1_ref[...]) * jnp.exp(seg), e01)
    y_slabs = []
    for t in range(n_tok):
        acc = None
        for pi, (tt, u) in enumerate(pairs):
            if tt != t:
                continue
            term = coef[pi * nb:(pi + 1) * nb, :] * xdt[u * nb:(u + 1) * nb, :]
            acc = term if acc is None else acc + term
        y_slabs.append(acc)
    y_diag = jnp.concatenate(y_slabs, axis=0)

    zpad = jnp.concatenate([xd,
                            jnp.where(hf == 0, cdx[0:half, :], cdx[half:nb, :]),
                            jnp.zeros((LANES - q - half, di), F32)], axis=0)
    zt = zpad.T
    row_seq = lax.broadcasted_iota(I32, (q, 1), 0) % nb
    cm_b = cm.astype(BF16)
    y_off_g = [None] * SSM_GROUPS
    for sl in range(half):
        in_seq = row_seq == hf * half + sl
        for g in range(SSM_GROUPS):
            s_old = s0_ref[sl, g * hpg:(g + 1) * hpg, :]
            c_g = jnp.where(in_seq, cm_b[:, g * SSM_STATE:(g + 1) * SSM_STATE], jnp.zeros((), BF16))
            yo = _dot_nt(c_g, s_old.astype(BF16))
            y_off_g[g] = yo if y_off_g[g] is None else y_off_g[g] + yo
            b_g = jnp.where(in_seq, bm[:, g * SSM_STATE:(g + 1) * SSM_STATE], 0.0)
            b_pad = jnp.concatenate([b_g, jnp.zeros((LANES - q, SSM_STATE), F32)], axis=0).astype(BF16)
            zt_g = zt[g * hpg:(g + 1) * hpg, :]
            upd = jnp.dot(zt_g.astype(BF16), b_pad, preferred_element_type=F32)
            decay = zt_g[:, q + sl:q + sl + 1]
            sn_ref[sl, g * hpg:(g + 1) * hpg, :] = s_old * decay + upd
    y_off = jnp.concatenate(y_off_g, axis=1)

    @pl.when(hf == 0)
    def _():
        yoff_sc[...] = y_off

    @pl.when(hf == 1)
    def _():
        y = y_diag + (yoff_sc[...] + y_off) * eax + xs * dsk_ref[...]
        yg = _gate_norm(y, z_ref[...], gn_ref[...], SSM_GROUPS)
        h_ref[...] = x_ref[...] + jnp.dot(yg.astype(BF16), wout_ref[...], preferred_element_type=F32)


def _ssd_sample(z, xp7, dt, s0, x, cw, cb, dtb, alog, dsk, gnw, e01, g1, wout, tp, n_tok):
    nsteps = xp7.shape[0]
    nb = SEQ_PER_STEP
    half = nb // 2
    q = n_tok * nb
    di, d = z.shape[1], x.shape[1]
    cd = xp7.shape[3]
    base = tp // q
    rows = lambda s, hf: (base + s, 0)
    return pl.pallas_call(
        functools.partial(_ssd_sample_kernel, n_tok=n_tok),
        grid=(nsteps, 2),
        in_specs=[
            pl.BlockSpec((q, di), rows),
            pl.BlockSpec((1, n_tok + SSM_CONV - 1, nb, cd), lambda s, hf: (s, 0, 0, 0)),
            pl.BlockSpec((q, LANES), rows),
            pl.BlockSpec((half, di, SSM_STATE), lambda s, hf: (2 * s + hf, 0, 0)),
            pl.BlockSpec((q, d), lambda s, hf: (s, 0)),
            _const_spec(cw.shape), _const_spec(cb.shape), _const_spec(dtb.shape), _const_spec(alog.shape),
            _const_spec(dsk.shape), _const_spec(gnw.shape), _const_spec(e01.shape), _const_spec(g1.shape),
            _const_spec(wout.shape),
        ],
        out_specs=[
            pl.BlockSpec((q, d), lambda s, hf: (s, 0)),
            pl.BlockSpec((half, di, SSM_STATE), lambda s, hf: (2 * s + hf, 0, 0)),
        ],
        out_shape=[jax.ShapeDtypeStruct((nsteps * q, d), F32),
                   jax.ShapeDtypeStruct(s0.shape, F32)],
        scratch_shapes=[pltpu.VMEM((q, di), F32)],
        compiler_params=_cparams(("arbitrary", "arbitrary")),
        name="ssd_sample",
    )(z, xp7, dt, s0, x, cw, cb, dtb, alog, dsk, gnw, e01, g1, wout)


def _qkv_math(h, kvn_ref, an_ref, wkv_ref, wq_ref, kn_ref, qn_ref, rc_ref, rs1_ref, rs2_ref,
              hsum_ref, rq_ref, eq_ref, q_ref, k_ref, v_ref):
    hn = _rms_scale(h)
    kvw = k_ref.shape[1]
    kv = jnp.dot((hn * kvn_ref[...]).astype(BF16), wkv_ref[...], preferred_element_type=F32)
    k = kv[:, :kvw]
    v_ref[...] = kv[:, kvw:]
    inv_hd = 1.0 / ATT_HEAD_DIM
    k = k * lax.rsqrt(_mm01(k * k, hsum_ref[...], 2) * inv_hd + EPS) * kn_ref[...]
    rc, rs1, rs2 = rc_ref[...], rs1_ref[...], rs2_ref[...]
    rk = kvw // LANES
    k_ref[...] = _rope(k, _tile_lanes(rc, rk), _tile_lanes(rs1, rk), _tile_lanes(rs2, rk))
    q = jnp.dot((hn * an_ref[...]).astype(BF16), wq_ref[...], preferred_element_type=F32)
    rsq = lax.rsqrt(_mm01(q * q, rq_ref[...], 2) * inv_hd + EPS)
    q = q * _mm01(rsq, eq_ref[...], 2) * qn_ref[...]
    rq = q.shape[1] // LANES
    q_ref[...] = _rope(q, _tile_lanes(rc, rq), _tile_lanes(rs1, rq), _tile_lanes(rs2, rq))


def _attn_prompt_kernel(sink_ref, q_ref, kc_ref, kp_ref, vc_ref, vp_ref, o_ref):
    i = pl.program_id(1)
    w = WINDOW
    hd = ATT_HEAD_DIM
    n_q = q_ref.shape[1] // hd
    grp = n_q // ATT_KV_HEADS
    stack = ATTN_STACK
    rows = stack * w
    row = lax.broadcasted_iota(I32, (rows, 2 * w), 0) % w
    col = lax.broadcasted_iota(I32, (rows, 2 * w), 1)
    dist = row + w - col
    mask = (dist >= 0) & (dist < w) & ((col >= w) | (i > 0))
    r_head = lax.broadcasted_iota(I32, (rows, 1), 0) // w
    q = q_ref[...] * (hd ** -0.5)
    outs = [None] * n_q
    for g in range(ATT_KV_HEADS):
        sl = slice(g * hd, (g + 1) * hd)
        kk = jnp.concatenate([kp_ref[:, sl], kc_ref[:, sl]], axis=0).astype(BF16)
        vv = jnp.concatenate([vp_ref[:, sl], vc_ref[:, sl]], axis=0).astype(BF16)
        for h0 in range(g * grp, (g + 1) * grp, stack):
            qs = jnp.concatenate([q[:, h * hd:(h + 1) * hd] for h in range(h0, h0 + stack)],
                                 axis=0).astype(BF16)
            s = jnp.where(mask, _dot_nt(qs, kk), -jnp.inf)
            sink = jnp.zeros((rows, 1), F32)
            for j in range(stack):
                sink = jnp.where(r_head == j, sink_ref[h0 + j], sink)
            m = jnp.maximum(jnp.max(s, axis=-1, keepdims=True), sink)
            p = jnp.exp(s - m)
            denom = jnp.sum(p, axis=-1, keepdims=True) + jnp.exp(sink - m)
            o = jnp.dot(p.astype(BF16), vv, preferred_element_type=F32) * (1.0 / denom)
            for j in range(stack):
                outs[h0 + j] = o[j * w:(j + 1) * w, :]
    o_ref[...] = jnp.concatenate(outs, axis=1)


def _attn_prompt(sinks, q, k, v, bp, seq):
    w = WINDOW
    nb = seq // w
    qw, kvw = q.shape[1], k.shape[1]
    cur = lambda b, i: (b * nb + i, 0)
    prev = lambda b, i: (b * nb + jnp.maximum(i - 1, 0), 0)
    return pl.pallas_call(
        _attn_prompt_kernel,
        grid=(bp, nb),
        in_specs=[pl.BlockSpec(memory_space=pltpu.SMEM),
                  pl.BlockSpec((w, qw), cur), pl.BlockSpec((w, kvw), cur), pl.BlockSpec((w, kvw), prev),
                  pl.BlockSpec((w, kvw), cur), pl.BlockSpec((w, kvw), prev)],
        out_specs=pl.BlockSpec((w, qw), cur),
        out_shape=jax.ShapeDtypeStruct((bp * seq, qw), F32),
        compiler_params=_cparams(("arbitrary", "arbitrary")),
        name="attn_prompt",
    )(sinks, q, k, k, v, v)


def _attn_sample_kernel(sink_ref, q_ref, kn_ref, vn_ref, kc_ref, vc_ref, o_ref, kw_ref, vw_ref, *, n_tok):
    nb = SEQ_PER_STEP
    qn = n_tok * nb
    hd = ATT_HEAD_DIM
    lc = kc_ref.shape[1]
    for new_ref, old_ref, win_ref in ((kn_ref, kc_ref, kw_ref), (vn_ref, vc_ref, vw_ref)):
        new = new_ref[...]
        for sq in range(nb):
            win_ref[sq, 0:lc - n_tok, :] = old_ref[sq, n_tok:lc, :]
            win_ref[sq, lc - n_tok:lc, :] = jnp.concatenate(
                [new[t * nb + sq:t * nb + sq + 1, :] for t in range(n_tok)], axis=0)
    n_q = q_ref.shape[1] // hd
    grp = n_q // ATT_KV_HEADS
    rows = grp * qn
    r = lax.broadcasted_iota(I32, (rows, 1), 0)
    r_seq = r % nb
    r_tok = (r % qn) // nb
    ccol = lax.broadcasted_iota(I32, (rows, lc), 1)
    mask_c = ccol >= r_tok + 1 + (lc - WINDOW)
    ncol = lax.broadcasted_iota(I32, (rows, LANES), 1)
    mask_n = (ncol < qn) & (ncol % nb == r_seq) & (ncol // nb <= r_tok)
    q = q_ref[...]
    zpad = jnp.zeros((LANES - qn, hd), F32)
    outs = [None] * n_q
    for g in range(ATT_KV_HEADS):
        sl = slice(g * hd, (g + 1) * hd)
        qs = jnp.concatenate([q[:, (g * grp + hq) * hd:(g * grp + hq + 1) * hd] for hq in range(grp)],
                             axis=0).astype(BF16)
        k_new = jnp.concatenate([kn_ref[:, sl], zpad], axis=0).astype(BF16)
        v_new = jnp.concatenate([vn_ref[:, sl], zpad], axis=0).astype(BF16)
        s_n = jnp.where(mask_n, _dot_nt(qs, k_new) * (hd ** -0.5), -jnp.inf)
        s_c = jnp.zeros((rows, lc), F32)
        for sq in range(nb):
            s_sq = _dot_nt(qs, kc_ref[sq, :, sl].astype(BF16))
            s_c = jnp.where(r_seq == sq, s_sq, s_c)
        s_c = jnp.where(mask_c, s_c * (hd ** -0.5), -jnp.inf)
        sink = jnp.zeros((rows, 1), F32)
        for hq in range(grp):
            sink = jnp.where(r // qn == hq, sink_ref[g * grp + hq], sink)
        m = jnp.maximum(jnp.maximum(jnp.max(s_c, axis=-1, keepdims=True),
                                    jnp.max(s_n, axis=-1, keepdims=True)), sink)
        p_c = jnp.exp(s_c - m)
        p_n = jnp.exp(s_n - m)
        denom = (jnp.sum(p_c, axis=-1, keepdims=True) + jnp.sum(p_n, axis=-1, keepdims=True)
                 + jnp.exp(sink - m))
        p_c = p_c / denom
        o = jnp.dot((p_n / denom).astype(BF16), v_new, preferred_element_type=F32)
        for sq in range(nb):
            o = o + jnp.dot(jnp.where(r_seq == sq, p_c, 0.0).astype(BF16), vc_ref[sq, :, sl].astype(BF16),
                            preferred_element_type=F32)
        for hq in range(grp):
            outs[g * grp + hq] = o[hq * qn:(hq + 1) * qn, :]
    o_ref[...] = jnp.concatenate(outs, axis=1)


def _attn_sample(sinks, q, k, v, kc, vc, tp, n_tok):
    nb = SEQ_PER_STEP
    qn = n_tok * nb
    nsteps = kc.shape[0] // nb
    lc, kvw = kc.shape[1], kc.shape[2]
    qw = q.shape[1]
    base = tp // qn
    rows = lambda s: (base + s, 0)
    return pl.pallas_call(
        functools.partial(_attn_sample_kernel, n_tok=n_tok),
        grid=(nsteps,),
        in_specs=[pl.BlockSpec(memory_space=pltpu.SMEM),
                  pl.BlockSpec((qn, qw), rows), pl.BlockSpec((qn, kvw), rows), pl.BlockSpec((qn, kvw), rows),
                  pl.BlockSpec((nb, lc, kvw), lambda s: (s, 0, 0)), pl.BlockSpec((nb, lc, kvw), lambda s: (s, 0, 0))],
        out_specs=[pl.BlockSpec((qn, qw), lambda s: (s, 0)),
                   pl.BlockSpec((nb, lc, kvw), lambda s: (s, 0, 0)), pl.BlockSpec((nb, lc, kvw), lambda s: (s, 0, 0))],
        out_shape=[jax.ShapeDtypeStruct((nsteps * qn, qw), F32),
                   jax.ShapeDtypeStruct(kc.shape, F32), jax.ShapeDtypeStruct(vc.shape, F32)],
        compiler_params=_cparams(("arbitrary",)),
        name="attn_sample",
    )(sinks, q, k, v, kc, vc)


def _router_kernel(hp_ref, hs_ref, fn_ref, wh_ref, wl_ref, b_ref, tril_ref, xn_ref, info_ref, info_t_ref, cnt_ref,
                   carry_sc, *, n_p):
    h = jnp.where(pl.program_id(0) < n_p, hp_ref[...], hs_ref[...])
    _router_math(h, fn_ref, wh_ref, wl_ref, b_ref, tril_ref, xn_ref, info_ref, info_t_ref, cnt_ref, carry_sc)


def _wo_router_kernel(op_ref, os_ref, res_ref, wo_ref, fn_ref, wh_ref, wl_ref, b_ref, tril_ref,
                      h_ref, xn_ref, info_ref, info_t_ref, cnt_ref, carry_sc, *, n_p):
    o = jnp.where(pl.program_id(0) < n_p, op_ref[...], os_ref[...])
    h = res_ref[...] + jnp.dot(o.astype(BF16), wo_ref[...], preferred_element_type=F32)
    h_ref[...] = h
    _router_math(h, fn_ref, wh_ref, wl_ref, b_ref, tril_ref, xn_ref, info_ref, info_t_ref, cnt_ref, carry_sc)


def _router_math(h, fn_ref, wh_ref, wl_ref, b_ref, tril_ref, xn_ref, info_ref, info_t_ref, cnt_ref, carry_sc):
    i = pl.program_id(0)

    @pl.when(i == 0)
    def _():
        carry_sc[...] = jnp.zeros(carry_sc.shape, F32)

    xn = _rms_scale(h) * fn_ref[...]
    _store_token_major(xn_ref, xn)
    x_hi, x_lo = _split_bf16(xn, 2)
    wh, wl = wh_ref[...], wl_ref[...]
    logits = (jnp.dot(x_hi, wh, preferred_element_type=F32) + jnp.dot(x_hi, wl, preferred_element_type=F32)
              + jnp.dot(x_lo, wh, preferred_element_type=F32)) + b_ref[...]
    tm = logits.shape[0]
    lane = lax.broadcasted_iota(I32, (tm, LANES), 1).astype(F32)
    big = float(LANES)
    neg = -jnp.inf

    is_grp = (lane >= MOE_EXPERTS) & (lane < MOE_EXPERTS + MOE_GROUPS)
    lg = jnp.where(is_grp, logits, neg)
    mg = jnp.max(lg, axis=-1, keepdims=True)
    gp = 1.0 / jnp.sum(jnp.exp(lg - mg), axis=-1, keepdims=True)
    gi = jnp.min(jnp.where(lg == mg, lane, big), axis=-1, keepdims=True) - MOE_EXPERTS

    lo = gi * MOE_EXPERTS_PER_GROUP
    le = jnp.where((lane >= lo) & (lane < lo + MOE_EXPERTS_PER_GROUP), logits, neg)
    m1 = jnp.max(le, axis=-1, keepdims=True)
    i1 = jnp.min(jnp.where(le == m1, lane, big), axis=-1, keepdims=True)
    le2 = jnp.where(lane == i1, neg, le)
    m2 = jnp.max(le2, axis=-1, keepdims=True)
    i2 = jnp.min(jnp.where(le2 == m2, lane, big), axis=-1, keepdims=True)
    e2 = jnp.exp(m2 - m1)
    g1 = gp * (1.0 / (1.0 + e2))
    g2 = gp * (e2 / (1.0 + e2))

    a1 = lane == i1
    a2 = lane == i2
    onehot = jnp.where(a1 | a2, 1.0, 0.0)
    before = jnp.dot(tril_ref[...], onehot.astype(BF16), preferred_element_type=F32) + carry_sc[...]
    r1 = jnp.sum(jnp.where(a1, before, 0.0), axis=-1, keepdims=True)
    r2 = jnp.sum(jnp.where(a2, before, 0.0), axis=-1, keepdims=True)
    carry_sc[...] = carry_sc[...] + jnp.sum(onehot, axis=0, keepdims=True)
    cnt_ref[...] = carry_sc[...]

    cols = (i1, i2, g1, g2, r1, r2)
    info = jnp.zeros((tm, LANES), F32)
    for k, cval in enumerate(cols):
        info = jnp.where(lane == k, cval, info)
    info_ref[...] = info
    info_t_ref[...] = info.T[0:SUBLANES, :]


def _two_source_specs(tm, width, n_p):
    return [pl.BlockSpec((tm, width), lambda i: (jnp.minimum(i, n_p - 1), 0)),
            pl.BlockSpec((tm, width), lambda i: (jnp.maximum(i - n_p, 0), 0))]


def _router(h_p, h_s, fn, wh, wl, b, tril):
    d = h_p.shape[1]
    tm = TOKEN_TILE
    n_p = h_p.shape[0] // tm
    t = h_p.shape[0] + h_s.shape[0]
    rows = lambda i: (i, 0)
    return pl.pallas_call(
        functools.partial(_router_kernel, n_p=n_p),
        grid=(t // tm,),
        in_specs=_two_source_specs(tm, d, n_p) + [_const_spec(a.shape) for a in (fn, wh, wl, b, tril)],
        out_specs=[pl.BlockSpec((tm * ROW_TILES, LANES), rows), pl.BlockSpec((tm, LANES), rows),
                   pl.BlockSpec((SUBLANES, tm), lambda i: (0, i)), _const_spec((1, LANES))],
        out_shape=[jax.ShapeDtypeStruct((t * ROW_TILES, LANES), F32), jax.ShapeDtypeStruct((t, LANES), F32),
                   jax.ShapeDtypeStruct((SUBLANES, t), F32), jax.ShapeDtypeStruct((1, LANES), F32)],
        scratch_shapes=[pltpu.VMEM((1, LANES), F32)],
        compiler_params=_cparams(("arbitrary",)),
        name="moe_router",
    )(h_p, h_s, fn, wh, wl, b, tril)


def _wo_router(o_p, o_s, res, wo, fn, wh, wl, b, tril):
    t, d = res.shape
    tm = TOKEN_TILE
    n_p = o_p.shape[0] // tm
    rows = lambda i: (i, 0)
    return pl.pallas_call(
        functools.partial(_wo_router_kernel, n_p=n_p),
        grid=(t // tm,),
        in_specs=_two_source_specs(tm, o_p.shape[1], n_p) + [pl.BlockSpec((tm, d), rows)]
        + [_const_spec(a.shape) for a in (wo, fn, wh, wl, b, tril)],
        out_specs=[pl.BlockSpec((tm, d), rows), pl.BlockSpec((tm * ROW_TILES, LANES), rows),
                   pl.BlockSpec((tm, LANES), rows), pl.BlockSpec((SUBLANES, tm), lambda i: (0, i)),
                   _const_spec((1, LANES))],
        out_shape=[jax.ShapeDtypeStruct((t, d), F32), jax.ShapeDtypeStruct((t * ROW_TILES, LANES), F32),
                   jax.ShapeDtypeStruct((t, LANES), F32), jax.ShapeDtypeStruct((SUBLANES, t), F32),
                   jax.ShapeDtypeStruct((1, LANES), F32)],
        scratch_shapes=[pltpu.VMEM((1, LANES), F32)],
        compiler_params=_cparams(("arbitrary",)),
        name="wo_router",
    )(o_p, o_s, res, wo, fn, wh, wl, b, tril)


def _store_token_major(ref, x):
    n = x.shape[0]
    for j in range(ROW_TILES):
        ref[pl.ds(j, n, stride=ROW_TILES), :] = x[:, j * LANES:(j + 1) * LANES]


def _load_token_major(ref, n):
    return jnp.concatenate([ref[pl.ds(j, n, stride=ROW_TILES), :] for j in range(ROW_TILES)], axis=1)


def _dest_kernel(pst_ref, info_ref, dest_ref):
    info = info_ref[...]
    e = info[0:2, :]
    start = jnp.zeros(e.shape, F32)
    for k in range(MOE_EXPERTS):
        start = jnp.where(e == k, pst_ref[k].astype(F32), start)
    dest = (start + info[4:6, :]).astype(I32)
    dest_ref[...] = jnp.concatenate([dest, jnp.zeros((SUBLANES - 2, dest.shape[1]), I32)], axis=0)


def _dest(pstarts, info_t):
    return pl.pallas_call(
        _dest_kernel,
        in_specs=[pl.BlockSpec(memory_space=pltpu.SMEM), pl.BlockSpec(memory_space=pltpu.VMEM)],
        out_specs=pl.BlockSpec(memory_space=pltpu.VMEM),
        out_shape=jax.ShapeDtypeStruct(info_t.shape, I32),
        name="moe_dest",
    )(pstarts, info_t)


def _tile_copy(src, src_row, dst, dst_row, sem):
    return pltpu.make_async_copy(src.at[pl.ds(pl.multiple_of(src_row * ROW_TILES, ROW_TILES), ROW_TILES)],
                                 dst.at[pl.ds(pl.multiple_of(dst_row * ROW_TILES, ROW_TILES), ROW_TILES)], sem)


def _wait_tiles(ref, n_tokens, sem):
    blk = ref.at[pl.ds(0, n_tokens * ROW_TILES)]
    pltpu.make_async_copy(blk, blk, sem).wait()


def _dispatch_kernel(d1_ref, d2_ref, pend_ref, pcnt_ref, xn_ref, xbuf_hbm, zero_sc, sem_z, sem):
    i = pl.program_id(0)
    tm = xn_ref.shape[0] // ROW_TILES
    blk_rows = zero_sc.shape[0]

    @pl.when(i == 0)
    def _():
        zero_sc[...] = jnp.zeros(zero_sc.shape, F32)

        def zero_copy(e):
            start = pl.multiple_of(pend_ref[e] * ROW_TILES - blk_rows, ROW_TILES)
            return pltpu.make_async_copy(zero_sc, xbuf_hbm.at[pl.ds(start, blk_rows)], sem_z)

        for e in range(MOE_EXPERTS):
            @pl.when(pcnt_ref[e] > 0)
            def _():
                zero_copy(e).start()
        first_unused = pend_ref[MOE_EXPERTS - 1] * ROW_TILES // blk_rows
        n_blocks = xbuf_hbm.shape[0] // blk_rows

        def tail_copy(b):
            dst = xbuf_hbm.at[pl.ds(pl.multiple_of(b * blk_rows, blk_rows), blk_rows)]
            return pltpu.make_async_copy(zero_sc, dst, sem_z)

        def tail_start(b, carry):
            tail_copy(b).start()
            return carry

        def tail_wait(b, carry):
            tail_copy(b).wait()
            return carry
        lax.fori_loop(first_unused, n_blocks, tail_start, 0)
        for e in range(MOE_EXPERTS):
            @pl.when(pcnt_ref[e] > 0)
            def _():
                zero_copy(e).wait()
        lax.fori_loop(first_unused, n_blocks, tail_wait, 0)

    def body(r, carry):
        t = i * tm + r
        _tile_copy(xn_ref, r, xbuf_hbm, d1_ref[t], sem).start(priority=0)
        _tile_copy(xn_ref, r, xbuf_hbm, d2_ref[t], sem).start(priority=1)
        return carry
    lax.fori_loop(0, tm, body, 0, unroll=8)
    _wait_tiles(xn_ref, tm, sem)
    _wait_tiles(xn_ref, tm, sem)


def _dispatch(dest1, dest2, pends, pcounts, xn_tm, n_rows):
    n_tok = dest1.shape[0]
    tm = DISPATCH_TILE
    grid_spec = pltpu.PrefetchScalarGridSpec(
        num_scalar_prefetch=4,
        grid=(n_tok // tm,),
        in_specs=[pl.BlockSpec((tm * ROW_TILES, LANES), lambda i, *_: (i, 0))],
        out_specs=pl.BlockSpec(memory_space=pl.ANY),
        scratch_shapes=[pltpu.VMEM((EXPERT_ROWS * ROW_TILES, LANES), F32), pltpu.SemaphoreType.DMA(()),
                        pltpu.SemaphoreType.DMA(())],
    )
    return pl.pallas_call(
        _dispatch_kernel,
        grid_spec=grid_spec,
        out_shape=jax.ShapeDtypeStruct((n_rows * ROW_TILES, LANES), F32),
        compiler_params=_cparams(("arbitrary",)),
        name="moe_dispatch",
    )(dest1, dest2, pends, pcounts, xn_tm)


def _expert_kernel(blk_e_ref, nvalid_ref, next_e_ref, x_hbm, wg_hbm, wu_hbm, wd_hbm, y_ref,
                   wg_sc, wu_sc, wd_sc, wg_st, wu_st, wd_st, x_sc, sem, xsem, *, layer):
    i = pl.program_id(0)
    nv = nvalid_ref[0]
    blk_rows = x_sc.shape[1]
    rows = blk_rows // ROW_TILES

    def weight_copies(e):
        return (pltpu.make_async_copy(wg_hbm.at[layer, e], wg_st, sem.at[0]),
                pltpu.make_async_copy(wu_hbm.at[layer, e], wu_st, sem.at[1]),
                pltpu.make_async_copy(wd_hbm.at[layer, e], wd_st, sem.at[2]))

    def row_copy(step):
        slot = step % EXPERT_RING
        src = x_hbm.at[pl.ds(pl.multiple_of(step * blk_rows, blk_rows), blk_rows)]
        return pltpu.make_async_copy(src, x_sc.at[slot], xsem.at[slot])

    @pl.when((i == 0) & (nv > 0))
    def _():
        for c in weight_copies(blk_e_ref[0]):
            c.start()
        for s in range(EXPERT_RING - 1):
            @pl.when(s < nv)
            def _():
                row_copy(s).start()

    @pl.when(i + EXPERT_RING - 1 < nv)
    def _():
        row_copy(i + EXPERT_RING - 1).start()

    @pl.when(i < nv)
    def _():
        e = blk_e_ref[i]
        e_prev = blk_e_ref[jnp.maximum(i - 1, 0)]
        row_copy(i).wait()
        x_ref = x_sc.at[i % EXPERT_RING]

        @pl.when((i == 0) | (e != e_prev))
        def _():
            for c in weight_copies(e):
                c.wait()
            wg_sc[...] = wg_st[...].astype(BF16)
            wu_sc[...] = wu_st[...].astype(BF16)
            wd_sc[...] = wd_st[...].astype(BF16)
            nxt = next_e_ref[e]

            @pl.when(nxt < MOE_EXPERTS)
            def _():
                for c in weight_copies(nxt):
                    c.start()

        x = _load_token_major(x_ref, rows).astype(BF16)
        hid = _silu(jnp.dot(x, wg_sc[...], preferred_element_type=F32)) * jnp.dot(
            x, wu_sc[...], preferred_element_type=F32)
        _store_token_major(y_ref, jnp.dot(hid.astype(BF16), wd_sc[...], preferred_element_type=F32))

    @pl.when(i >= nv)
    def _():
        y_ref[...] = jnp.zeros(y_ref.shape, F32)


def _experts(blk_e, nvalid, next_e, xbuf, wg, wu, wd, layer):
    nblk = blk_e.shape[0]
    d, hdim = wg.shape[2], wg.shape[3]
    rows = EXPERT_ROWS * ROW_TILES
    grid_spec = pltpu.PrefetchScalarGridSpec(
        num_scalar_prefetch=3,
        grid=(nblk,),
        in_specs=[pl.BlockSpec(memory_space=pl.ANY),
                  pl.BlockSpec(memory_space=pl.ANY), pl.BlockSpec(memory_space=pl.ANY),
                  pl.BlockSpec(memory_space=pl.ANY)],
        out_specs=pl.BlockSpec((rows, LANES), lambda i, be, nv, ne: (i, 0)),
        scratch_shapes=[pltpu.VMEM((d, hdim), BF16), pltpu.VMEM((d, hdim), BF16), pltpu.VMEM((hdim, d), BF16),
                        pltpu.VMEM((d, hdim), F32), pltpu.VMEM((d, hdim), F32), pltpu.VMEM((hdim, d), F32),
                        pltpu.VMEM((EXPERT_RING, rows, LANES), F32),
                        pltpu.SemaphoreType.DMA((3,)), pltpu.SemaphoreType.DMA((EXPERT_RING,))],
    )
    return pl.pallas_call(
        functools.partial(_expert_kernel, layer=layer),
        grid_spec=grid_spec,
        out_shape=jax.ShapeDtypeStruct((nblk * rows, LANES), F32),
        compiler_params=_cparams(("arbitrary",)),
        name="moe_experts",
    )(blk_e, nvalid, next_e, xbuf, wg, wu, wd)


def _gather_moe_rows(d1_ref, d2_ref, y_hbm, r_sc, sem, tm, row0=0):
    i = pl.program_id(0)
    n = pl.num_programs(0)

    def start(step):
        slot = step % 2
        base = row0 + step * tm

        def body(r, carry):
            _tile_copy(y_hbm, d1_ref[base + r], r_sc.at[slot, 0], r, sem.at[slot]).start(priority=0)
            _tile_copy(y_hbm, d2_ref[base + r], r_sc.at[slot, 1], r, sem.at[slot]).start(priority=1)
            return carry
        lax.fori_loop(0, tm, body, 0, unroll=8)

    @pl.when(i == 0)
    def _():
        start(i)

    @pl.when(i + 1 < n)
    def _():
        start(i + 1)

    slot = i % 2
    _wait_tiles(r_sc.at[slot, 0], tm, sem.at[slot])
    _wait_tiles(r_sc.at[slot, 1], tm, sem.at[slot])
    return _load_token_major(r_sc.at[slot, 0], tm), _load_token_major(r_sc.at[slot, 1], tm)


def _combine_kernel(d1_ref, d2_ref, h_ref, info_ref, y_hbm, o_ref, r_sc, sem, *, row0):
    y1, y2 = _gather_moe_rows(d1_ref, d2_ref, y_hbm, r_sc, sem, h_ref.shape[0], row0)
    info = info_ref[...]
    o_ref[...] = h_ref[...] + (y1 * info[:, 2:3] + y2 * info[:, 3:4])


def _combine(dest1, dest2, h, info, ybuf, row0, nrows):
    d = h.shape[1]
    tm = TOKEN_TILE
    base_tile = row0 // tm
    rows = lambda i, a, b: (base_tile + i, 0)
    grid_spec = pltpu.PrefetchScalarGridSpec(
        num_scalar_prefetch=2,
        grid=(nrows // tm,),
        in_specs=[pl.BlockSpec((tm, d), rows), pl.BlockSpec((tm, LANES), rows), pl.BlockSpec(memory_space=pl.ANY)],
        out_specs=pl.BlockSpec((tm, d), lambda i, a, b: (i, 0)),
        scratch_shapes=[pltpu.VMEM((2, 2, tm * ROW_TILES, LANES), F32), pltpu.SemaphoreType.DMA((2,))],
    )
    return pl.pallas_call(
        functools.partial(_combine_kernel, row0=row0),
        grid_spec=grid_spec,
        out_shape=jax.ShapeDtypeStruct((nrows, d), F32),
        compiler_params=_cparams(("arbitrary",)),
        name="moe_combine",
    )(dest1, dest2, h, info, ybuf)


def _combine_qkv_kernel(d1_ref, d2_ref, hp_ref, hs_ref, info_ref, y_hbm, kvn_ref, an_ref, wkv_ref, wq_ref, kn_ref,
                        qn_ref, rc_ref, rs1_ref, rs2_ref, hsum_ref, rq_ref, eq_ref,
                        h_ref, q_ref, k_ref, v_ref, r_sc, sem, *, n_p):
    y1, y2 = _gather_moe_rows(d1_ref, d2_ref, y_hbm, r_sc, sem, h_ref.shape[0])
    info = info_ref[...]
    h = jnp.where(pl.program_id(0) < n_p, hp_ref[...], hs_ref[...]) + (y1 * info[:, 2:3] + y2 * info[:, 3:4])
    h_ref[...] = h
    _qkv_math(h, kvn_ref, an_ref, wkv_ref, wq_ref, kn_ref, qn_ref, rc_ref, rs1_ref, rs2_ref,
              hsum_ref, rq_ref, eq_ref, q_ref, k_ref, v_ref)


def _combine_qkv(dest1, dest2, h_p, h_s, info, ybuf, kvn, an, wkv, wq, knt, qnt, rc, rs1, rs2, hsum, rq, eq):
    d = h_p.shape[1]
    tm = TOKEN_TILE
    n_p = h_p.shape[0] // tm
    t = h_p.shape[0] + h_s.shape[0]
    kvw = wkv.shape[1] // 2
    qw = wq.shape[1]
    rows = lambda i, a, b: (i, 0)
    const = lambda arr: pl.BlockSpec(arr.shape, lambda i, a, b: (0,) * arr.ndim)
    grid_spec = pltpu.PrefetchScalarGridSpec(
        num_scalar_prefetch=2,
        grid=(t // tm,),
        in_specs=[pl.BlockSpec((tm, d), lambda i, a, b: (jnp.minimum(i, n_p - 1), 0)),
                  pl.BlockSpec((tm, d), lambda i, a, b: (jnp.maximum(i - n_p, 0), 0)),
                  pl.BlockSpec((tm, LANES), rows), pl.BlockSpec(memory_space=pl.ANY)]
        + [const(a) for a in (kvn, an, wkv, wq, knt, qnt)] + [pl.BlockSpec((tm, LANES), rows)] * 3
        + [const(a) for a in (hsum, rq, eq)],
        out_specs=[pl.BlockSpec((tm, d), rows), pl.BlockSpec((tm, qw), rows), pl.BlockSpec((tm, kvw), rows),
                   pl.BlockSpec((tm, kvw), rows)],
        scratch_shapes=[pltpu.VMEM((2, 2, tm * ROW_TILES, LANES), F32), pltpu.SemaphoreType.DMA((2,))],
    )
    return pl.pallas_call(
        functools.partial(_combine_qkv_kernel, n_p=n_p),
        grid_spec=grid_spec,
        out_shape=[jax.ShapeDtypeStruct((t, d), F32), jax.ShapeDtypeStruct((t, qw), F32),
                   jax.ShapeDtypeStruct((t, kvw), F32), jax.ShapeDtypeStruct((t, kvw), F32)],
        compiler_params=_cparams(("arbitrary",)),
        name="combine_qkv",
    )(dest1, dest2, h_p, h_s, info, ybuf, kvn, an, wkv, wq, knt, qnt, rc, rs1, rs2, hsum, rq, eq)


def _router_weights(w_grp, b_grp, w_rt, b_rt):
    d = w_rt.shape[0]
    w_cat = jnp.zeros((d, LANES), F32).at[:, :MOE_EXPERTS].set(w_rt).at[:, MOE_EXPERTS:MOE_EXPERTS + MOE_GROUPS].set(w_grp)
    b_cat = jnp.zeros((1, LANES), F32).at[0, :MOE_EXPERTS].set(b_rt).at[0, MOE_EXPERTS:MOE_EXPERTS + MOE_GROUPS].set(b_grp)
    w_hi = w_cat.astype(BF16)
    w_lo = (w_cat - w_hi.astype(F32)).astype(BF16)
    return w_hi, w_lo, b_cat


def _moe_experts(xn_tm, info_t, cnt, wg, wu, wd, layer):
    t = info_t.shape[1]
    counts = cnt[0, :MOE_EXPERTS].astype(I32)
    pcounts = (counts + EXPERT_ROWS - 1) // EXPERT_ROWS * EXPERT_ROWS
    pends = jnp.cumsum(pcounts)
    pstarts = pends - pcounts
    nblk = -(-(2 * t + MOE_EXPERTS * (EXPERT_ROWS - 1)) // EXPERT_ROWS)
    blk_start = jnp.arange(nblk, dtype=I32) * EXPERT_ROWS
    blk_e = jnp.minimum(jnp.sum((pends[None, :] <= blk_start[:, None]).astype(I32), axis=1), MOE_EXPERTS - 1)
    nvalid = pends[-1:] // EXPERT_ROWS
    eid = jnp.arange(MOE_EXPERTS, dtype=I32)
    later = (eid[None, :] > eid[:, None]) & (pcounts[None, :] > 0)
    next_e = jnp.min(jnp.where(later, eid[None, :], MOE_EXPERTS), axis=1).astype(I32)

    dest = _dest(pstarts, info_t)
    dest1, dest2 = dest[0], dest[1]
    xbuf = _dispatch(dest1, dest2, pends, pcounts, xn_tm, nblk * EXPERT_ROWS)
    return dest1, dest2, _experts(blk_e, nvalid, next_e, xbuf, wg, wu, wd, layer)


def _rope_tables(pos):
    half = ROT_DIM // 2
    inv = ROPE_THETA ** (-jnp.arange(0, ROT_DIM, 2, dtype=F32) / ROT_DIM)
    ang = pos.astype(F32)[:, None] * inv[None, :]
    cos, sin = jnp.cos(ang), jnp.sin(ang)
    n = pos.shape[0]
    ones = jnp.ones((n, ATT_HEAD_DIM - ROT_DIM), F32)
    zeros_r = jnp.zeros((n, ATT_HEAD_DIM - ROT_DIM), F32)
    zeros_h = jnp.zeros((n, half), F32)
    c = jnp.concatenate([cos, cos, ones], axis=1)
    s1 = jnp.concatenate([-sin, zeros_h, zeros_r], axis=1)
    s2 = jnp.concatenate([zeros_h, sin, zeros_r], axis=1)
    reps = LANES // ATT_HEAD_DIM
    return tuple(jnp.tile(a, (1, reps)) for a in (c, s1, s2))


def _to_step_order(a, nsteps, n_tok):
    c = a.shape[-1]
    return a.reshape(nsteps, SEQ_PER_STEP, n_tok, c).transpose(0, 2, 1, 3).reshape(nsteps * n_tok * SEQ_PER_STEP, c)


def _from_step_order(a, nsteps, n_tok):
    c = a.shape[-1]
    return a.reshape(nsteps, n_tok, SEQ_PER_STEP, c).transpose(0, 2, 1, 3).reshape(nsteps * SEQ_PER_STEP, n_tok, c)


def kernel(x_prompt, x_sample, state_ssm, state_conv, cache_k_win, cache_v_win, ssm_norm, ssm_w_in, ssm_conv_w, ssm_conv_b, ssm_dt_bias, ssm_a_log, ssm_d, ssm_gate_norm, ssm_w_out, kv_norm, w_kv, k_norm, attn_norm, w_q, q_norm, sinks, w_o, ffn_norm, moe_w_group, moe_b_group, moe_w_router, moe_b_router, moe_w_gate, moe_w_up, moe_w_down):
    bp, seq, d = x_prompt.shape
    bs, n_tok, _ = x_sample.shape
    tp, ts = bp * seq, bs * n_tok
    nsteps = bs // SEQ_PER_STEP
    n_heads = ssm_d.shape[1]
    di = n_heads * SSM_HEAD_DIM
    gn_w = SSM_GROUPS * SSM_STATE
    cdim = di + 2 * gn_w
    n_q = sinks.shape[1]
    kvw = ATT_KV_HEADS * ATT_HEAD_DIM

    xp2 = x_prompt.reshape(tp, d)
    xs2 = _to_step_order(x_sample, nsteps, n_tok)

    lane_i = jnp.arange(LANES)
    e01 = (lane_i[:, None] == (jnp.arange(di) // SSM_HEAD_DIM)[None, :]).astype(BF16)
    hpg = di // SSM_GROUPS
    g1 = ((jnp.arange(gn_w) // SSM_STATE)[:, None] == (lane_i // (hpg // SSM_HEAD_DIM))[None, :])
    g1 = (g1 & (lane_i < n_heads)[None, :]).astype(BF16)
    tril_c = (jnp.arange(SSM_CHUNK)[:, None] >= jnp.arange(SSM_CHUNK)[None, :]).astype(BF16)
    tril_x = (jnp.arange(TOKEN_TILE)[:, None] > jnp.arange(TOKEN_TILE)[None, :]).astype(BF16)
    hsum = ((jnp.arange(kvw) // ATT_HEAD_DIM)[:, None] == (jnp.arange(kvw) // ATT_HEAD_DIM)[None, :]).astype(BF16)
    qw = n_q * ATT_HEAD_DIM
    rq = ((jnp.arange(qw) // ATT_HEAD_DIM)[:, None] == lane_i[None, :]).astype(BF16)
    eq = rq.T

    w_in = ssm_w_in[0]
    wz = w_in[:, :di].astype(BF16)
    wx = w_in[:, di:di + cdim].astype(BF16)
    wd = jnp.zeros((d, LANES), F32).at[:, :n_heads].set(w_in[:, di + cdim:]).astype(BF16)
    cw, cb = ssm_conv_w[0], ssm_conv_b[0].reshape(1, cdim)
    z, xbc, dt = _inproj(xp2, xs2, ssm_norm[0].reshape(1, d), wz, wx, wd)

    pad_h = lambda v: jnp.zeros((1, LANES), F32).at[0, :n_heads].set(v)
    dtb, alog = pad_h(ssm_dt_bias[0]), pad_h(ssm_a_log[0])
    dsk = jnp.repeat(ssm_d[0], SSM_HEAD_DIM).reshape(1, di)
    gnw = ssm_gate_norm[0].reshape(1, di)

    w_out = ssm_w_out[0].astype(BF16)
    h_p, s_fin, c_fin = _ssd_prompt(z, xbc, dt, xp2, cw, cb, dtb, alog, dsk, gnw, tril_c, e01, w_out, bp, seq)
    ssm_p = s_fin.reshape(1, bp, n_heads, SSM_HEAD_DIM, SSM_STATE)
    conv_p = c_fin[:, SUBLANES - (SSM_CONV - 1):, :].reshape(1, bp, SSM_CONV - 1, cdim)

    xbc_s = xbc[tp:].reshape(nsteps, n_tok, SEQ_PER_STEP, cdim)
    conv_in = state_conv[0].reshape(nsteps, SEQ_PER_STEP, SSM_CONV - 1, cdim).transpose(0, 2, 1, 3)
    xp7 = jnp.concatenate([conv_in, xbc_s], axis=1)
    s0 = state_ssm[0].reshape(bs, di, SSM_STATE)
    h_s, s_new = _ssd_sample(z, xp7, dt, s0, xs2, cw, cb, dtb, alog, dsk, gnw, e01, g1, w_out, tp, n_tok)
    ssm_s = s_new.reshape(1, bs, n_heads, SSM_HEAD_DIM, SSM_STATE)
    conv_s = _from_step_order(xbc[tp:], nsteps, n_tok)[:, n_tok - (SSM_CONV - 1):, :].reshape(
        1, bs, SSM_CONV - 1, cdim)

    w_hi, w_lo, b_cat = _router_weights(moe_w_group[0], moe_b_group[0], moe_w_router[0], moe_b_router[0])
    xn_tm, info, info_t, cnt = _router(h_p, h_s, ffn_norm[0].reshape(1, d), w_hi, w_lo, b_cat, tril_x)
    dest1, dest2, ybuf = _moe_experts(xn_tm, info_t, cnt, moe_w_gate, moe_w_up, moe_w_down, 0)

    pos = jnp.concatenate([jnp.tile(jnp.arange(seq, dtype=I32), bp),
                           jnp.tile(jnp.repeat(PAST_LEN + jnp.arange(n_tok, dtype=I32), SEQ_PER_STEP), nsteps)])
    rc, rs1, rs2 = _rope_tables(pos)
    h1, q, k, v = _combine_qkv(dest1, dest2, h_p, h_s, info, ybuf, kv_norm.reshape(1, d), attn_norm[0].reshape(1, d),
                               w_kv.astype(BF16), w_q[0].astype(BF16),
                               jnp.tile(k_norm, ATT_KV_HEADS).reshape(1, kvw), jnp.tile(q_norm[0], n_q).reshape(1, qw),
                               rc, rs1, rs2, hsum, rq, eq)
    sk = sinks[0]
    lc = cache_k_win.shape[1]
    kc = cache_k_win.reshape(bs, lc, kvw)
    vc = cache_v_win.reshape(bs, lc, kvw)
    o_p = _attn_prompt(sk, q, k, v, bp, seq)
    o_s, k_win, v_win = _attn_sample(sk, q, k, v, kc, vc, tp, n_tok)
    w_hi, w_lo, b_cat = _router_weights(moe_w_group[1], moe_b_group[1], moe_w_router[1], moe_b_router[1])
    h2, xn_tm, info, info_t, cnt = _wo_router(o_p, o_s, h1, w_o[0].astype(BF16), ffn_norm[1].reshape(1, d),
                                             w_hi, w_lo, b_cat, tril_x)
    dest1, dest2, ybuf = _moe_experts(xn_tm, info_t, cnt, moe_w_gate, moe_w_up, moe_w_down, 1)
    y_p = _combine(dest1, dest2, h2, info, ybuf, 0, tp)
    y_s = _combine(dest1, dest2, h2, info, ybuf, tp, ts)

    wl = min(WINDOW, seq)
    k_p = k[:tp].reshape(bp, seq, kvw)[:, seq - wl:].reshape(bp, wl, ATT_KV_HEADS, ATT_HEAD_DIM)
    v_p = v[:tp].reshape(bp, seq, kvw)[:, seq - wl:].reshape(bp, wl, ATT_KV_HEADS, ATT_HEAD_DIM)
    k_s = k_win.reshape(bs, lc, ATT_KV_HEADS, ATT_HEAD_DIM)
    v_s = v_win.reshape(bs, lc, ATT_KV_HEADS, ATT_HEAD_DIM)
    return (y_p.reshape(bp, seq, d), _from_step_order(y_s, nsteps, n_tok),
            ssm_p, conv_p, k_p, v_p, ssm_s, conv_s, k_s, v_s)
```

```python
import functools

import jax
import jax.numpy as jnp
from jax import lax
from jax.experimental import pallas as pl
from jax.experimental.pallas import tpu as pltpu

F32 = jnp.float32
BF16 = jnp.bfloat16
I32 = jnp.int32

EPS = 1e-6
SSM_HEAD_DIM = 64
SSM_GROUPS = 4
SSM_STATE = 128
SSM_CONV = 4
SSM_CHUNK = 128
ATT_HEAD_DIM = 64
ATT_KV_HEADS = 4
WINDOW = 128
ROT_DIM = ATT_HEAD_DIM // 4
ROPE_THETA = 500000.0
MOE_GROUPS = 4
MOE_EXPERTS_PER_GROUP = 8
MOE_EXPERTS = MOE_GROUPS * MOE_EXPERTS_PER_GROUP
MOE_BLOCK = 128
PAST_LEN = 16384

LANES = 128
SUBLANES = 8
BF16_ROWS = 16
SEQ_PER_STEP = SUBLANES
TOKEN_TILE = 256
ROW_TILES = 8
ATTN_STACK = 2
DISPATCH_TILE = 512
EXPERT_RING = 3
EXPERT_ROWS = 512
VMEM_LIMIT = 56 * 1024 * 1024


def _cparams(sem):
    return pltpu.CompilerParams(dimension_semantics=sem, vmem_limit_bytes=VMEM_LIMIT)


def _const_spec(shape):
    nd = len(shape)
    return pl.BlockSpec(shape, lambda *_: (0,) * nd)


def _split_bf16(v, n):
    parts = []
    r = v
    for k in range(n):
        p = r.astype(BF16)
        parts.append(p)
        if k + 1 < n:
            r = r - p.astype(F32)
    return parts


def _mm01(v, m01, n=3):
    acc = None
    for p in _split_bf16(v, n):
        d = jnp.dot(p, m01, preferred_element_type=F32)
        acc = d if acc is None else acc + d
    return acc


def _mm01_left(m01, v, n=3):
    acc = None
    for p in _split_bf16(v, n):
        d = jnp.dot(m01, p, preferred_element_type=F32)
        acc = d if acc is None else acc + d
    return acc


def _dot_nt(a, b):
    return lax.dot_general(a, b, (((1,), (1,)), ((), ())), preferred_element_type=F32)


def _sigmoid(x):
    return 0.5 * jnp.tanh(0.5 * x) + 0.5


def _silu(x):
    return x * _sigmoid(x)


def _softplus(x):
    return jnp.maximum(x, 0.0) + jnp.log1p(jnp.exp(-jnp.abs(x)))


def _rms_scale(x):
    return x * lax.rsqrt(jnp.mean(x * x, axis=-1, keepdims=True) + EPS)


def _gate_norm(y, z, gn, n_groups):
    yz = y * _silu(z)
    w = yz.shape[-1] // n_groups
    outs = []
    for g in range(n_groups):
        v = yz[:, g * w:(g + 1) * w]
        outs.append(_rms_scale(v) * gn[:, g * w:(g + 1) * w])
    return jnp.concatenate(outs, axis=1)


def _rope(x, c, s1, s2):
    w = x.shape[-1]
    return x * c + pltpu.roll(x, w - ROT_DIM // 2, 1) * s1 + pltpu.roll(x, ROT_DIM // 2, 1) * s2


def _tile_lanes(t, reps):
    return t if reps == 1 else jnp.concatenate([t] * reps, axis=1)


def _inproj_kernel(xp_ref, xs_ref, g_ref, wz_ref, wx_ref, wd_ref, z_ref, xbc_ref, dt_ref, *, n_p):
    i = pl.program_id(0)
    x = jnp.where(i < n_p, xp_ref[...], xs_ref[...])
    xn = (_rms_scale(x) * g_ref[...]).astype(BF16)
    z_ref[...] = jnp.dot(xn, wz_ref[...], preferred_element_type=F32)
    xbc_ref[...] = jnp.dot(xn, wx_ref[...], preferred_element_type=F32)
    dt_ref[...] = jnp.dot(xn, wd_ref[...], preferred_element_type=F32)


def _inproj(xp2, xs2, g, wz, wx, wd):
    tp, d = xp2.shape
    ts = xs2.shape[0]
    tm = TOKEN_TILE
    n_p, n_s = tp // tm, ts // tm
    t = tp + ts
    return pl.pallas_call(
        functools.partial(_inproj_kernel, n_p=n_p),
        grid=(n_p + n_s,),
        in_specs=_two_source_specs(tm, d, n_p) + [_const_spec(a.shape) for a in (g, wz, wx, wd)],
        out_specs=[
            pl.BlockSpec((tm, wz.shape[1]), lambda i: (i, 0)),
            pl.BlockSpec((tm, wx.shape[1]), lambda i: (i, 0)),
            pl.BlockSpec((tm, wd.shape[1]), lambda i: (i, 0)),
        ],
        out_shape=[jax.ShapeDtypeStruct((t, wz.shape[1]), F32),
                   jax.ShapeDtypeStruct((t, wx.shape[1]), F32),
                   jax.ShapeDtypeStruct((t, wd.shape[1]), F32)],
        compiler_params=_cparams(("arbitrary",)),
        name="inproj",
    )(xp2, xs2, g, wz, wx, wd)


def _ssd_prompt_kernel(z_ref, xbc_ref, dt_ref, x_ref, cw_ref, cb_ref, dtb_ref, alog_ref, dsk_ref, gn_ref,
                       tril_ref, e_ref, wout_ref, h_ref, sfin_ref, cfin_ref, xpad_sc, st_sc):
    c = pl.program_id(1)
    q = SSM_CHUNK
    cd = xbc_ref.shape[1]
    di = z_ref.shape[1]
    gn_w = SSM_GROUPS * SSM_STATE
    hpg = di // SSM_GROUPS
    pad = SUBLANES

    n_slab = cd // LANES

    @pl.when(c == 0)
    def _():
        xpad_sc[:, 0:pad, :] = jnp.zeros((n_slab, pad, LANES), F32)
        st_sc[...] = jnp.zeros(st_sc.shape, F32)

    @pl.when(c > 0)
    def _():
        xpad_sc[:, 0:pad, :] = xpad_sc[:, q:q + pad, :]

    slabs = []
    for j in range(n_slab):
        ls = slice(j * LANES, (j + 1) * LANES)
        xpad_sc[j, pad:pad + q, :] = xbc_ref[:, ls]
        acc = cb_ref[:, ls]
        for k in range(SSM_CONV):
            off = pad - (SSM_CONV - 1) + k
            acc = acc + xpad_sc[j, off:off + q, :] * cw_ref[k:k + 1, ls]
        slabs.append(_silu(acc))
    xc = jnp.concatenate(slabs, axis=1)
    xs = xc[:, :di]
    bm = xc[:, di:di + gn_w]
    cm = xc[:, di + gn_w:]

    dt = _softplus(dt_ref[...] + dtb_ref[...])
    a = -jnp.exp(alog_ref[...])
    act = _mm01_left(tril_ref[...], dt * a)
    act_t = act.T
    act_last = act[q - 1:q, :]
    pieces = (_split_bf16(dt, 2) + _split_bf16(jnp.exp(act_last - act), 2) + _split_bf16(jnp.exp(act), 2)
              + _split_bf16(jnp.exp(act[q - BF16_ROWS:q, :]), 3))
    ex = jnp.dot(jnp.concatenate(pieces, axis=0), e_ref[...], preferred_element_type=F32)
    xdt = xs * (ex[0:q] + ex[q:2 * q])
    xd = xdt * (ex[2 * q:3 * q] + ex[3 * q:4 * q])
    eax = ex[4 * q:5 * q] + ex[5 * q:6 * q]
    tail = ex[6 * q:]
    cd = (tail[0:BF16_ROWS] + tail[BF16_ROWS:2 * BF16_ROWS]) + tail[2 * BF16_ROWS:]
    cdx = cd[BF16_ROWS - 1:BF16_ROWS, :]

    row = lax.broadcasted_iota(I32, (q, q), 0)
    col = lax.broadcasted_iota(I32, (q, q), 1)
    causal = row >= col
    lane = lax.broadcasted_iota(I32, (q, LANES), 1)
    lo_half = lane < SSM_HEAD_DIM

    y_parts = []
    heads_per_group = hpg // SSM_HEAD_DIM
    for g in range(SSM_GROUPS):
        cg = cm[:, g * SSM_STATE:(g + 1) * SSM_STATE].astype(BF16)
        bg = bm[:, g * SSM_STATE:(g + 1) * SSM_STATE]
        cb = _dot_nt(cg, bg.astype(BF16))
        st_g = st_sc[:, g * hpg:(g + 1) * hpg]
        y_off = jnp.dot(cg, st_g.astype(BF16), preferred_element_type=F32)
        for pr in range(heads_per_group // 2):
            h0 = g * heads_per_group + 2 * pr
            ms = []
            for h in (h0, h0 + 1):
                seg = act[:, h:h + 1] - act_t[h:h + 1, :]
                lm = jnp.exp(jnp.where(causal, seg, -jnp.inf))
                ms.append((cb * lm).astype(BF16))
            m2 = jnp.concatenate(ms, axis=1)
            xpair = xdt[:, h0 * SSM_HEAD_DIM:(h0 + 2) * SSM_HEAD_DIM]
            rhs = jnp.concatenate([jnp.where(lo_half, xpair, 0.0),
                                   jnp.where(lo_half, 0.0, xpair)], axis=0).astype(BF16)
            y_d = jnp.dot(m2, rhs, preferred_element_type=F32)
            lo = 2 * pr * SSM_HEAD_DIM
            y_parts.append(y_d + y_off[:, lo:lo + LANES] * eax[:, g * hpg + lo:g * hpg + lo + LANES])
        upd = jnp.dot(bg.T.astype(BF16), xd[:, g * hpg:(g + 1) * hpg].astype(BF16),
                      preferred_element_type=F32)
        st_sc[:, g * hpg:(g + 1) * hpg] = st_g * cdx[:, g * hpg:(g + 1) * hpg] + upd

    y = jnp.concatenate(y_parts, axis=1) + xs * dsk_ref[...]
    yg = _gate_norm(y, z_ref[...], gn_ref[...], SSM_GROUPS)
    h_ref[...] = x_ref[...] + jnp.dot(yg.astype(BF16), wout_ref[...], preferred_element_type=F32)

    @pl.when(c == pl.num_programs(1) - 1)
    def _():
        sfin_ref[0] = st_sc[...].T
        cfin_ref[0] = jnp.concatenate([xpad_sc[j, q:q + pad, :] for j in range(n_slab)], axis=1)


def _ssd_prompt(z, xbc, dt, x, cw, cb, dtb, alog, dsk, gnw, tril, e01, wout, bp, seq):
    nc = seq // SSM_CHUNK
    q = SSM_CHUNK
    di, cd, d = z.shape[1], xbc.shape[1], x.shape[1]
    rows = lambda b, c: (b * nc + c, 0)
    return pl.pallas_call(
        _ssd_prompt_kernel,
        grid=(bp, nc),
        in_specs=[
            pl.BlockSpec((q, di), rows), pl.BlockSpec((q, cd), rows), pl.BlockSpec((q, LANES), rows),
            pl.BlockSpec((q, d), rows),
            _const_spec(cw.shape), _const_spec(cb.shape), _const_spec(dtb.shape), _const_spec(alog.shape),
            _const_spec(dsk.shape), _const_spec(gnw.shape), _const_spec(tril.shape), _const_spec(e01.shape),
            _const_spec(wout.shape),
        ],
        out_specs=[
            pl.BlockSpec((q, d), rows),
            pl.BlockSpec((1, di, SSM_STATE), lambda b, c: (b, 0, 0)),
            pl.BlockSpec((1, SUBLANES, cd), lambda b, c: (b, 0, 0)),
        ],
        out_shape=[jax.ShapeDtypeStruct((bp * seq, d), F32),
                   jax.ShapeDtypeStruct((bp, di, SSM_STATE), F32),
                   jax.ShapeDtypeStruct((bp, SUBLANES, cd), F32)],
        scratch_shapes=[pltpu.VMEM((cd // LANES, q + 2 * SUBLANES, LANES), F32), pltpu.VMEM((SSM_STATE, di), F32)],
        compiler_params=_cparams(("arbitrary", "arbitrary")),
        name="ssd_prompt",
    )(z, xbc, dt, x, cw, cb, dtb, alog, dsk, gnw, tril, e01, wout)


def _ssd_sample_kernel(z_ref, xp_ref, dt_ref, s0_ref, x_ref, cw_ref, cb_ref, dtb_ref, alog_ref, dsk_ref, gn_ref,
                       e_ref, g1_ref, wout_ref, h_ref, sn_ref, yoff_sc, *, n_tok):
    hf = pl.program_id(1)
    nb = SEQ_PER_STEP
    half = nb // 2
    q = n_tok * nb
    di = z_ref.shape[1]
    gn_w = SSM_GROUPS * SSM_STATE
    hpg = di // SSM_GROUPS

    taps = [xp_ref[0, m] for m in range(n_tok + SSM_CONV - 1)]
    slabs = []
    for t in range(n_tok):
        acc = cb_ref[...]
        for k in range(SSM_CONV):
            acc = acc + taps[t + k] * cw_ref[k:k + 1, :]
        slabs.append(_silu(acc))
    xc = jnp.concatenate(slabs, axis=0)
    xs = xc[:, :di]
    bm = xc[:, di:di + gn_w]
    cm = xc[:, di + gn_w:]

    dt = _softplus(dt_ref[...] + dtb_ref[...])
    da = dt * (-jnp.exp(alog_ref[...]))
    acts = []
    run = None
    for t in range(n_tok):
        d = da[t * nb:(t + 1) * nb, :]
        run = d if run is None else run + d
        acts.append(run)
    act = jnp.concatenate(acts, axis=0)
    act_last = jnp.concatenate([acts[-1]] * n_tok, axis=0)
    e01 = e_ref[...]
    xdt = xs * _mm01(dt, e01)
    xd = xdt * _mm01(jnp.exp(act_last - act), e01)
    eax = _mm01(jnp.exp(act), e01)
    cdx = _mm01(jnp.exp(acts[-1]), e01)

    pairs = [(t, u) for t in range(n_tok) for u in range(t + 1)]
    cbp = jnp.concatenate([cm[t * nb:(t + 1) * nb, :] * bm[u * nb:(u + 1) * nb, :] for t, u in pairs], axis=0)
    seg = jnp.concatenate([acts[t] - acts[u] for t, u in pairs], axis=0)
    coef = _mm01(_mm01(cbp, g1_ref[...]) * jnp.exp(seg), e01)
    y_slabs = []
    for t in range(n_tok):
        acc = None
        for pi, (tt, u) in enumerate(pairs):
            if tt != t:
                continue
            term = coef[pi * nb:(pi + 1) * nb, :] * xdt[u * nb:(u + 1) * nb, :]
            acc = term if acc is None else acc + term
        y_slabs.append(acc)
    y_diag = jnp.concatenate(y_slabs, axis=0)

    zpad = jnp.concatenate([xd,
                            jnp.where(hf == 0, cdx[0:half, :], cdx[half:nb, :]),
                            jnp.zeros((LANES - q - half, di), F32)], axis=0)
    zt = zpad.T
    row_seq = lax.broadcasted_iota(I32, (q, 1), 0) % nb
    cm_b = cm.astype(BF16)
    y_off_g = [None] * SSM_GROUPS
    for sl in range(half):
        in_seq = row_seq == hf * half + sl
        for g in range(SSM_GROUPS):
            s_old = s0_ref[sl, g * hpg:(g + 1) * hpg, :]
            c_g = jnp.where(in_seq, cm_b[:, g * SSM_STATE:(g + 1) * SSM_STATE], jnp.zeros((), BF16))
            yo = _dot_nt(c_g, s_old.astype(BF16))
            y_off_g[g] = yo if y_off_g[g] is None else y_off_g[g] + yo
            b_g = jnp.where(in_seq, bm[:, g * SSM_STATE:(g + 1) * SSM_STATE], 0.0)
            b_pad = jnp.concatenate([b_g, jnp.zeros((LANES - q, SSM_STATE), F32)], axis=0).astype(BF16)
            zt_g = zt[g * hpg:(g + 1) * hpg, :]
            upd = jnp.dot(zt_g.astype(BF16), b_pad, preferred_element_type=F32)
            decay = zt_g[:, q + sl:q + sl + 1]
            sn_ref[sl, g * hpg:(g + 1) * hpg, :] = s_old * decay + upd
    y_off = jnp.concatenate(y_off_g, axis=1)

    @pl.when(hf == 0)
    def _():
        yoff_sc[...] = y_off

    @pl.when(hf == 1)
    def _():
        y = y_diag + (yoff_sc[...] + y_off) * eax + xs * dsk_ref[...]
        yg = _gate_norm(y, z_ref[...], gn_ref[...], SSM_GROUPS)
        h_ref[...] = x_ref[...] + jnp.dot(yg.astype(BF16), wout_ref[...], preferred_element_type=F32)


def _ssd_sample(z, xp7, dt, s0, x, cw, cb, dtb, alog, dsk, gnw, e01, g1, wout, tp, n_tok):
    nsteps = xp7.shape[0]
    nb = SEQ_PER_STEP
    half = nb // 2
    q = n_tok * nb
    di, d = z.shape[1], x.shape[1]
    cd = xp7.shape[3]
    base = tp // q
    rows = lambda s, hf: (base + s, 0)
    return pl.pallas_call(
        functools.partial(_ssd_sample_kernel, n_tok=n_tok),
        grid=(nsteps, 2),
        in_specs=[
            pl.BlockSpec((q, di), rows),
            pl.BlockSpec((1, n_tok + SSM_CONV - 1, nb, cd), lambda s, hf: (s, 0, 0, 0)),
            pl.BlockSpec((q, LANES), rows),
            pl.BlockSpec((half, di, SSM_STATE), lambda s, hf: (2 * s + hf, 0, 0)),
            pl.BlockSpec((q, d), lambda s, hf: (s, 0)),
            _const_spec(cw.shape), _const_spec(cb.shape), _const_spec(dtb.shape), _const_spec(alog.shape),
            _const_spec(dsk.shape), _const_spec(gnw.shape), _const_spec(e01.shape), _const_spec(g1.shape),
            _const_spec(wout.shape),
        ],
        out_specs=[
            pl.BlockSpec((q, d), lambda s, hf: (s, 0)),
            pl.BlockSpec((half, di, SSM_STATE), lambda s, hf: (2 * s + hf, 0, 0)),
        ],
        out_shape=[jax.ShapeDtypeStruct((nsteps * q, d), F32),
                   jax.ShapeDtypeStruct(s0.shape, F32)],
        scratch_shapes=[pltpu.VMEM((q, di), F32)],
        compiler_params=_cparams(("arbitrary", "arbitrary")),
        name="ssd_sample",
    )(z, xp7, dt, s0, x, cw, cb, dtb, alog, dsk, gnw, e01, g1, wout)


def _qkv_math(h, kvn_ref, an_ref, wkv_ref, wq_ref, kn_ref, qn_ref, rc_ref, rs1_ref, rs2_ref,
              hsum_ref, rq_ref, eq_ref, q_ref, k_ref, v_ref):
    hn = _rms_scale(h)
    kvw = k_ref.shape[1]
    kv = jnp.dot((hn * kvn_ref[...]).astype(BF16), wkv_ref[...], preferred_element_type=F32)
    k = kv[:, :kvw]
    v_ref[...] = kv[:, kvw:]
    inv_hd = 1.0 / ATT_HEAD_DIM
    k = k * lax.rsqrt(_mm01(k * k, hsum_ref[...], 2) * inv_hd + EPS) * kn_ref[...]
    rc, rs1, rs2 = rc_ref[...], rs1_ref[...], rs2_ref[...]
    rk = kvw // LANES
    k_ref[...] = _rope(k, _tile_lanes(rc, rk), _tile_lanes(rs1, rk), _tile_lanes(rs2, rk))
    q = jnp.dot((hn * an_ref[...]).astype(BF16), wq_ref[...], preferred_element_type=F32)
    rsq = lax.rsqrt(_mm01(q * q, rq_ref[...], 2) * inv_hd + EPS)
    q = q * _mm01(rsq, eq_ref[...], 2) * qn_ref[...]
    rq = q.shape[1] // LANES
    q_ref[...] = _rope(q, _tile_lanes(rc, rq), _tile_lanes(rs1, rq), _tile_lanes(rs2, rq))


def _attn_prompt_kernel(sink_ref, q_ref, kc_ref, kp_ref, vc_ref, vp_ref, o_ref):
    i = pl.program_id(1)
    w = WINDOW
    hd = ATT_HEAD_DIM
    n_q = q_ref.shape[1] // hd
    grp = n_q // ATT_KV_HEADS
    stack = ATTN_STACK
    rows = stack * w
    row = lax.broadcasted_iota(I32, (rows, 2 * w), 0) % w
    col = lax.broadcasted_iota(I32, (rows, 2 * w), 1)
    dist = row + w - col
    mask = (dist >= 0) & (dist < w) & ((col >= w) | (i > 0))
    r_head = lax.broadcasted_iota(I32, (rows, 1), 0) // w
    q = q_ref[...] * (hd ** -0.5)
    outs = [None] * n_q
    for g in range(ATT_KV_HEADS):
        sl = slice(g * hd, (g + 1) * hd)
        kk = jnp.concatenate([kp_ref[:, sl], kc_ref[:, sl]], axis=0).astype(BF16)
        vv = jnp.concatenate([vp_ref[:, sl], vc_ref[:, sl]], axis=0).astype(BF16)
        for h0 in range(g * grp, (g + 1) * grp, stack):
            qs = jnp.concatenate([q[:, h * hd:(h + 1) * hd] for h in range(h0, h0 + stack)],
                                 axis=0).astype(BF16)
            s = jnp.where(mask, _dot_nt(qs, kk), -jnp.inf)
            sink = jnp.zeros((rows, 1), F32)
            for j in range(stack):
                sink = jnp.where(r_head == j, sink_ref[h0 + j], sink)
            m = jnp.maximum(jnp.max(s, axis=-1, keepdims=True), sink)
            p = jnp.exp(s - m)
            denom = jnp.sum(p, axis=-1, keepdims=True) + jnp.exp(sink - m)
            o = jnp.dot(p.astype(BF16), vv, preferred_element_type=F32) * (1.0 / denom)
            for j in range(stack):
                outs[h0 + j] = o[j * w:(j + 1) * w, :]
    o_ref[...] = jnp.concatenate(outs, axis=1)


def _attn_prompt(sinks, q, k, v, bp, seq):
    w = WINDOW
    nb = seq // w
    qw, kvw = q.shape[1], k.shape[1]
    cur = lambda b, i: (b * nb + i, 0)
    prev = lambda b, i: (b * nb + jnp.maximum(i - 1, 0), 0)
    return pl.pallas_call(
        _attn_prompt_kernel,
        grid=(bp, nb),
        in_specs=[pl.BlockSpec(memory_space=pltpu.SMEM),
                  pl.BlockSpec((w, qw), cur), pl.BlockSpec((w, kvw), cur), pl.BlockSpec((w, kvw), prev),
                  pl.BlockSpec((w, kvw), cur), pl.BlockSpec((w, kvw), prev)],
        out_specs=pl.BlockSpec((w, qw), cur),
        out_shape=jax.ShapeDtypeStruct((bp * seq, qw), F32),
        compiler_params=_cparams(("arbitrary", "arbitrary")),
        name="attn_prompt",
    )(sinks, q, k, k, v, v)


def _attn_sample_kernel(sink_ref, q_ref, kn_ref, vn_ref, kc_ref, vc_ref, o_ref, kw_ref, vw_ref, *, n_tok):
    nb = SEQ_PER_STEP
    qn = n_tok * nb
    hd = ATT_HEAD_DIM
    lc = kc_ref.shape[1]
    for new_ref, old_ref, win_ref in ((kn_ref, kc_ref, kw_ref), (vn_ref, vc_ref, vw_ref)):
        new = new_ref[...]
        for sq in range(nb):
            win_ref[sq, 0:lc - n_tok, :] = old_ref[sq, n_tok:lc, :]
            win_ref[sq, lc - n_tok:lc, :] = jnp.concatenate(
                [new[t * nb + sq:t * nb + sq + 1, :] for t in range(n_tok)], axis=0)
    n_q = q_ref.shape[1] // hd
    grp = n_q // ATT_KV_HEADS
    rows = grp * qn
    r = lax.broadcasted_iota(I32, (rows, 1), 0)
    r_seq = r % nb
    r_tok = (r % qn) // nb
    ccol = lax.broadcasted_iota(I32, (rows, lc), 1)
    mask_c = ccol >= r_tok + 1 + (lc - WINDOW)
    ncol = lax.broadcasted_iota(I32, (rows, LANES), 1)
    mask_n = (ncol < qn) & (ncol % nb == r_seq) & (ncol // nb <= r_tok)
    q = q_ref[...]
    zpad = jnp.zeros((LANES - qn, hd), F32)
    outs = [None] * n_q
    for g in range(ATT_KV_HEADS):
        sl = slice(g * hd, (g + 1) * hd)
        qs = jnp.concatenate([q[:, (g * grp + hq) * hd:(g * grp + hq + 1) * hd] for hq in range(grp)],
                             axis=0).astype(BF16)
        k_new = jnp.concatenate([kn_ref[:, sl], zpad], axis=0).astype(BF16)
        v_new = jnp.concatenate([vn_ref[:, sl], zpad], axis=0).astype(BF16)
        s_n = jnp.where(mask_n, _dot_nt(qs, k_new) * (hd ** -0.5), -jnp.inf)
        s_c = jnp.zeros((rows, lc), F32)
        for sq in range(nb):
            s_sq = _dot_nt(qs, kc_ref[sq, :, sl].astype(BF16))
            s_c = jnp.where(r_seq == sq, s_sq, s_c)
        s_c = jnp.where(mask_c, s_c * (hd ** -0.5), -jnp.inf)
        sink = jnp.zeros((rows, 1), F32)
        for hq in range(grp):
            sink = jnp.where(r // qn == hq, sink_ref[g * grp + hq], sink)
        m = jnp.maximum(jnp.maximum(jnp.max(s_c, axis=-1, keepdims=True),
                                    jnp.max(s_n, axis=-1, keepdims=True)), sink)
        p_c = jnp.exp(s_c - m)
        p_n = jnp.exp(s_n - m)
        denom = (jnp.sum(p_c, axis=-1, keepdims=True) + jnp.sum(p_n, axis=-1, keepdims=True)
                 + jnp.exp(sink - m))
        p_c = p_c / denom
        o = jnp.dot((p_n / denom).astype(BF16), v_new, preferred_element_type=F32)
        for sq in range(nb):
            o = o + jnp.dot(jnp.where(r_seq == sq, p_c, 0.0).astype(BF16), vc_ref[sq, :, sl].astype(BF16),
                            preferred_element_type=F32)
        for hq in range(grp):
            outs[g * grp + hq] = o[hq * qn:(hq + 1) * qn, :]
    o_ref[...] = jnp.concatenate(outs, axis=1)


def _attn_sample(sinks, q, k, v, kc, vc, tp, n_tok):
    nb = SEQ_PER_STEP
    qn = n_tok * nb
    nsteps = kc.shape[0] // nb
    lc, kvw = kc.shape[1], kc.shape[2]
    qw = q.shape[1]
    base = tp // qn
    rows = lambda s: (base + s, 0)
    return pl.pallas_call(
        functools.partial(_attn_sample_kernel, n_tok=n_tok),
        grid=(nsteps,),
        in_specs=[pl.BlockSpec(memory_space=pltpu.SMEM),
                  pl.BlockSpec((qn, qw), rows), pl.BlockSpec((qn, kvw), rows), pl.BlockSpec((qn, kvw), rows),
                  pl.BlockSpec((nb, lc, kvw), lambda s: (s, 0, 0)), pl.BlockSpec((nb, lc, kvw), lambda s: (s, 0, 0))],
        out_specs=[pl.BlockSpec((qn, qw), lambda s: (s, 0)),
                   pl.BlockSpec((nb, lc, kvw), lambda s: (s, 0, 0)), pl.BlockSpec((nb, lc, kvw), lambda s: (s, 0, 0))],
        out_shape=[jax.ShapeDtypeStruct((nsteps * qn, qw), F32),
                   jax.ShapeDtypeStruct(kc.shape, F32), jax.ShapeDtypeStruct(vc.shape, F32)],
        compiler_params=_cparams(("arbitrary",)),
        name="attn_sample",
    )(sinks, q, k, v, kc, vc)


def _router_kernel(hp_ref, hs_ref, fn_ref, wh_ref, wl_ref, b_ref, triu_ref, xn_ref, info_ref, info_t_ref, cnt_ref,
                   carry_sc, *, n_p):
    h = jnp.where(pl.program_id(0) < n_p, hp_ref[...], hs_ref[...])
    _router_math(h, fn_ref, wh_ref, wl_ref, b_ref, triu_ref, xn_ref, info_ref, info_t_ref, cnt_ref, carry_sc)


def _wo_router_kernel(op_ref, os_ref, res_ref, wo_ref, fn_ref, wh_ref, wl_ref, b_ref, triu_ref,
                      h_ref, xn_ref, info_ref, info_t_ref, cnt_ref, carry_sc, *, n_p):
    o = jnp.where(pl.program_id(0) < n_p, op_ref[...], os_ref[...])
    h = res_ref[...] + jnp.dot(o.astype(BF16), wo_ref[...], preferred_element_type=F32)
    h_ref[...] = h
    _router_math(h, fn_ref, wh_ref, wl_ref, b_ref, triu_ref, xn_ref, info_ref, info_t_ref, cnt_ref, carry_sc)


def _router_math(h, fn_ref, wh_ref, wl_ref, b_ref, triu_ref, xn_ref, info_ref, info_t_ref, cnt_ref, carry_sc):
    i = pl.program_id(0)

    @pl.when(i == 0)
    def _():
        carry_sc[...] = jnp.zeros(carry_sc.shape, F32)

    xn = _rms_scale(h) * fn_ref[...]
    _store_token_major(xn_ref, xn)
    x_hi, x_lo = _split_bf16(xn, 2)
    wh, wl = wh_ref[...], wl_ref[...]
    logits = (jnp.dot(x_hi, wh, preferred_element_type=F32) + jnp.dot(x_hi, wl, preferred_element_type=F32)
              + jnp.dot(x_lo, wh, preferred_element_type=F32)) + b_ref[...]
    tm = logits.shape[0]
    lt = logits.T
    per = MOE_EXPERTS_PER_GROUP
    row = lax.broadcasted_iota(I32, (per, tm), 0).astype(F32)
    big = float(LANES)
    neg = -jnp.inf

    lg = jnp.where(row < MOE_GROUPS, lt[MOE_EXPERTS:MOE_EXPERTS + per, :], neg)
    mg = jnp.max(lg, axis=0, keepdims=True)
    gp = 1.0 / jnp.sum(jnp.exp(lg - mg), axis=0, keepdims=True)
    gi = jnp.min(jnp.where(lg == mg, row, big), axis=0, keepdims=True)

    le = lt[0:per, :]
    for g in range(1, MOE_GROUPS):
        le = jnp.where(gi == g, lt[g * per:(g + 1) * per, :], le)
    m1 = jnp.max(le, axis=0, keepdims=True)
    i1 = jnp.min(jnp.where(le == m1, row, big), axis=0, keepdims=True)
    le2 = jnp.where(row == i1, neg, le)
    m2 = jnp.max(le2, axis=0, keepdims=True)
    i2 = jnp.min(jnp.where(le2 == m2, row, big), axis=0, keepdims=True)
    e2 = jnp.exp(m2 - m1)
    g1 = gp * (1.0 / (1.0 + e2))
    g2 = gp * (e2 / (1.0 + e2))
    x1 = gi * per + i1
    x2 = gi * per + i2

    row_e = lax.broadcasted_iota(I32, (LANES, tm), 0).astype(F32)
    a1 = row_e == x1
    a2 = row_e == x2
    onehot = jnp.where(a1 | a2, 1.0, 0.0)
    before = jnp.dot(onehot.astype(BF16), triu_ref[...], preferred_element_type=F32) + carry_sc[:, 0:1]
    r1 = jnp.sum(jnp.where(a1, before, 0.0), axis=0, keepdims=True)
    r2 = jnp.sum(jnp.where(a2, before, 0.0), axis=0, keepdims=True)
    carry_sc[...] = carry_sc[...] + jnp.sum(onehot, axis=1, keepdims=True)
    cnt_ref[...] = carry_sc[...]

    info_t = jnp.zeros((SUBLANES, tm), F32)
    field = lax.broadcasted_iota(I32, (SUBLANES, tm), 0)
    for k, val in enumerate((x1, x2, g1, g2, r1, r2)):
        info_t = jnp.where(field == k, val, info_t)
    info_t_ref[...] = info_t
    info_ref[...] = jnp.concatenate([info_t, jnp.zeros((LANES - SUBLANES, tm), F32)], axis=0).T


def _two_source_specs(tm, width, n_p):
    return [pl.BlockSpec((tm, width), lambda i: (jnp.minimum(i, n_p - 1), 0)),
            pl.BlockSpec((tm, width), lambda i: (jnp.maximum(i - n_p, 0), 0))]


def _router(h_p, h_s, fn, wh, wl, b, tril):
    d = h_p.shape[1]
    tm = TOKEN_TILE
    n_p = h_p.shape[0] // tm
    t = h_p.shape[0] + h_s.shape[0]
    rows = lambda i: (i, 0)
    return pl.pallas_call(
        functools.partial(_router_kernel, n_p=n_p),
        grid=(t // tm,),
        in_specs=_two_source_specs(tm, d, n_p) + [_const_spec(a.shape) for a in (fn, wh, wl, b, tril)],
        out_specs=[pl.BlockSpec((tm * ROW_TILES, LANES), rows), pl.BlockSpec((tm, LANES), rows),
                   pl.BlockSpec((SUBLANES, tm), lambda i: (0, i)), _const_spec((LANES, LANES))],
        out_shape=[jax.ShapeDtypeStruct((t * ROW_TILES, LANES), F32), jax.ShapeDtypeStruct((t, LANES), F32),
                   jax.ShapeDtypeStruct((SUBLANES, t), F32), jax.ShapeDtypeStruct((LANES, LANES), F32)],
        scratch_shapes=[pltpu.VMEM((LANES, LANES), F32)],
        compiler_params=_cparams(("arbitrary",)),
        name="moe_router",
    )(h_p, h_s, fn, wh, wl, b, tril)


def _wo_router(o_p, o_s, res, wo, fn, wh, wl, b, tril):
    t, d = res.shape
    tm = TOKEN_TILE
    n_p = o_p.shape[0] // tm
    rows = lambda i: (i, 0)
    return pl.pallas_call(
        functools.partial(_wo_router_kernel, n_p=n_p),
        grid=(t // tm,),
        in_specs=_two_source_specs(tm, o_p.shape[1], n_p) + [pl.BlockSpec((tm, d), rows)]
        + [_const_spec(a.shape) for a in (wo, fn, wh, wl, b, tril)],
        out_specs=[pl.BlockSpec((tm, d), rows), pl.BlockSpec((tm * ROW_TILES, LANES), rows),
                   pl.BlockSpec((tm, LANES), rows), pl.BlockSpec((SUBLANES, tm), lambda i: (0, i)),
                   _const_spec((LANES, LANES))],
        out_shape=[jax.ShapeDtypeStruct((t, d), F32), jax.ShapeDtypeStruct((t * ROW_TILES, LANES), F32),
                   jax.ShapeDtypeStruct((t, LANES), F32), jax.ShapeDtypeStruct((SUBLANES, t), F32),
                   jax.ShapeDtypeStruct((LANES, LANES), F32)],
        scratch_shapes=[pltpu.VMEM((LANES, LANES), F32)],
        compiler_params=_cparams(("arbitrary",)),
        name="wo_router",
    )(o_p, o_s, res, wo, fn, wh, wl, b, tril)


def _store_token_major(ref, x):
    n = x.shape[0]
    for j in range(ROW_TILES):
        ref[pl.ds(j, n, stride=ROW_TILES), :] = x[:, j * LANES:(j + 1) * LANES]


def _load_token_major(ref, n):
    return jnp.concatenate([ref[pl.ds(j, n, stride=ROW_TILES), :] for j in range(ROW_TILES)], axis=1)


def _dest_kernel(pst_ref, info_ref, dest_ref):
    info = info_ref[...]
    e = info[0:2, :]
    start = jnp.zeros(e.shape, F32)
    for k in range(MOE_EXPERTS):
        start = jnp.where(e == k, pst_ref[k].astype(F32), start)
    dest = (start + info[4:6, :]).astype(I32)
    dest_ref[...] = jnp.concatenate([dest, jnp.zeros((SUBLANES - 2, dest.shape[1]), I32)], axis=0)


def _dest(pstarts, info_t):
    return pl.pallas_call(
        _dest_kernel,
        in_specs=[pl.BlockSpec(memory_space=pltpu.SMEM), pl.BlockSpec(memory_space=pltpu.VMEM)],
        out_specs=pl.BlockSpec(memory_space=pltpu.VMEM),
        out_shape=jax.ShapeDtypeStruct(info_t.shape, I32),
        name="moe_dest",
    )(pstarts, info_t)


def _tile_copy(src, src_row, dst, dst_row, sem):
    return pltpu.make_async_copy(src.at[pl.ds(pl.multiple_of(src_row * ROW_TILES, ROW_TILES), ROW_TILES)],
                                 dst.at[pl.ds(pl.multiple_of(dst_row * ROW_TILES, ROW_TILES), ROW_TILES)], sem)


def _wait_tiles(ref, n_tokens, sem):
    blk = ref.at[pl.ds(0, n_tokens * ROW_TILES)]
    pltpu.make_async_copy(blk, blk, sem).wait()


def _dispatch_kernel(d1_ref, d2_ref, pend_ref, pcnt_ref, xn_ref, xbuf_hbm, zero_sc, sem_z, sem):
    i = pl.program_id(0)
    tm = xn_ref.shape[0] // ROW_TILES
    blk_rows = zero_sc.shape[0]

    @pl.when(i == 0)
    def _():
        zero_sc[...] = jnp.zeros(zero_sc.shape, F32)

        def zero_copy(e):
            start = pl.multiple_of(pend_ref[e] * ROW_TILES - blk_rows, ROW_TILES)
            return pltpu.make_async_copy(zero_sc, xbuf_hbm.at[pl.ds(start, blk_rows)], sem_z)

        for e in range(MOE_EXPERTS):
            @pl.when(pcnt_ref[e] > 0)
            def _():
                zero_copy(e).start()
        first_unused = pend_ref[MOE_EXPERTS - 1] * ROW_TILES // blk_rows
        n_blocks = xbuf_hbm.shape[0] // blk_rows

        def tail_copy(b):
            dst = xbuf_hbm.at[pl.ds(pl.multiple_of(b * blk_rows, blk_rows), blk_rows)]
            return pltpu.make_async_copy(zero_sc, dst, sem_z)

        def tail_start(b, carry):
            tail_copy(b).start()
            return carry

        def tail_wait(b, carry):
            tail_copy(b).wait()
            return carry
        lax.fori_loop(first_unused, n_blocks, tail_start, 0)
        for e in range(MOE_EXPERTS):
            @pl.when(pcnt_ref[e] > 0)
            def _():
                zero_copy(e).wait()
        lax.fori_loop(first_unused, n_blocks, tail_wait, 0)

    def body(r, carry):
        t = i * tm + r
        _tile_copy(xn_ref, r, xbuf_hbm, d1_ref[t], sem).start(priority=0)
        _tile_copy(xn_ref, r, xbuf_hbm, d2_ref[t], sem).start(priority=1)
        return carry
    lax.fori_loop(0, tm, body, 0, unroll=8)
    _wait_tiles(xn_ref, tm, sem)
    _wait_tiles(xn_ref, tm, sem)


def _dispatch(dest1, dest2, pends, pcounts, xn_tm, n_rows):
    n_tok = dest1.shape[0]
    tm = DISPATCH_TILE
    grid_spec = pltpu.PrefetchScalarGridSpec(
        num_scalar_prefetch=4,
        grid=(n_tok // tm,),
        in_specs=[pl.BlockSpec((tm * ROW_TILES, LANES), lambda i, *_: (i, 0))],
        out_specs=pl.BlockSpec(memory_space=pl.ANY),
        scratch_shapes=[pltpu.VMEM((EXPERT_ROWS * ROW_TILES, LANES), F32), pltpu.SemaphoreType.DMA(()),
                        pltpu.SemaphoreType.DMA(())],
    )
    return pl.pallas_call(
        _dispatch_kernel,
        grid_spec=grid_spec,
        out_shape=jax.ShapeDtypeStruct((n_rows * ROW_TILES, LANES), F32),
        compiler_params=_cparams(("arbitrary",)),
        name="moe_dispatch",
    )(dest1, dest2, pends, pcounts, xn_tm)


def _expert_kernel(blk_e_ref, nvalid_ref, next_e_ref, x_hbm, wg_hbm, wu_hbm, wd_hbm, y_ref,
                   wg_sc, wu_sc, wd_sc, wg_st, wu_st, wd_st, x_sc, sem, xsem, *, layer):
    i = pl.program_id(0)
    nv = nvalid_ref[0]
    blk_rows = x_sc.shape[1]
    rows = blk_rows // ROW_TILES

    def weight_copies(e):
        return (pltpu.make_async_copy(wg_hbm.at[layer, e], wg_st, sem.at[0]),
                pltpu.make_async_copy(wu_hbm.at[layer, e], wu_st, sem.at[1]),
                pltpu.make_async_copy(wd_hbm.at[layer, e], wd_st, sem.at[2]))

    def row_copy(step):
        slot = step % EXPERT_RING
        src = x_hbm.at[pl.ds(pl.multiple_of(step * blk_rows, blk_rows), blk_rows)]
        return pltpu.make_async_copy(src, x_sc.at[slot], xsem.at[slot])

    @pl.when((i == 0) & (nv > 0))
    def _():
        for c in weight_copies(blk_e_ref[0]):
            c.start()
        for s in range(EXPERT_RING - 1):
            @pl.when(s < nv)
            def _():
                row_copy(s).start()

    @pl.when(i + EXPERT_RING - 1 < nv)
    def _():
        row_copy(i + EXPERT_RING - 1).start()

    @pl.when(i < nv)
    def _():
        e = blk_e_ref[i]
        e_prev = blk_e_ref[jnp.maximum(i - 1, 0)]
        row_copy(i).wait()
        x_ref = x_sc.at[i % EXPERT_RING]

        @pl.when((i == 0) | (e != e_prev))
        def _():
            for c in weight_copies(e):
                c.wait()
            wg_sc[...] = wg_st[...].astype(BF16)
            wu_sc[...] = wu_st[...].astype(BF16)
            wd_sc[...] = wd_st[...].astype(BF16)
            nxt = next_e_ref[e]

            @pl.when(nxt < MOE_EXPERTS)
            def _():
                for c in weight_copies(nxt):
                    c.start()

        x = _load_token_major(x_ref, rows).astype(BF16)
        hid = _silu(jnp.dot(x, wg_sc[...], preferred_element_type=F32)) * jnp.dot(
            x, wu_sc[...], preferred_element_type=F32)
        _store_token_major(y_ref, jnp.dot(hid.astype(BF16), wd_sc[...], preferred_element_type=F32))

    @pl.when(i >= nv)
    def _():
        y_ref[...] = jnp.zeros(y_ref.shape, F32)


def _experts(blk_e, nvalid, next_e, xbuf, wg, wu, wd, layer):
    nblk = blk_e.shape[0]
    d, hdim = wg.shape[2], wg.shape[3]
    rows = EXPERT_ROWS * ROW_TILES
    grid_spec = pltpu.PrefetchScalarGridSpec(
        num_scalar_prefetch=3,
        grid=(nblk,),
        in_specs=[pl.BlockSpec(memory_space=pl.ANY),
                  pl.BlockSpec(memory_space=pl.ANY), pl.BlockSpec(memory_space=pl.ANY),
                  pl.BlockSpec(memory_space=pl.ANY)],
        out_specs=pl.BlockSpec((rows, LANES), lambda i, be, nv, ne: (i, 0)),
        scratch_shapes=[pltpu.VMEM((d, hdim), BF16), pltpu.VMEM((d, hdim), BF16), pltpu.VMEM((hdim, d), BF16),
                        pltpu.VMEM((d, hdim), F32), pltpu.VMEM((d, hdim), F32), pltpu.VMEM((hdim, d), F32),
                        pltpu.VMEM((EXPERT_RING, rows, LANES), F32),
                        pltpu.SemaphoreType.DMA((3,)), pltpu.SemaphoreType.DMA((EXPERT_RING,))],
    )
    return pl.pallas_call(
        functools.partial(_expert_kernel, layer=layer),
        grid_spec=grid_spec,
        out_shape=jax.ShapeDtypeStruct((nblk * rows, LANES), F32),
        compiler_params=_cparams(("arbitrary",)),
        name="moe_experts",
    )(blk_e, nvalid, next_e, xbuf, wg, wu, wd)


def _gather_moe_rows(d1_ref, d2_ref, y_hbm, r_sc, sem, tm, row0=0):
    i = pl.program_id(0)
    n = pl.num_programs(0)

    def start(step):
        slot = step % 2
        base = row0 + step * tm

        def body(r, carry):
            _tile_copy(y_hbm, d1_ref[base + r], r_sc.at[slot, 0], r, sem.at[slot]).start(priority=0)
            _tile_copy(y_hbm, d2_ref[base + r], r_sc.at[slot, 1], r, sem.at[slot]).start(priority=1)
            return carry
        lax.fori_loop(0, tm, body, 0, unroll=8)

    @pl.when(i == 0)
    def _():
        start(i)

    @pl.when(i + 1 < n)
    def _():
        start(i + 1)

    slot = i % 2
    _wait_tiles(r_sc.at[slot, 0], tm, sem.at[slot])
    _wait_tiles(r_sc.at[slot, 1], tm, sem.at[slot])
    return _load_token_major(r_sc.at[slot, 0], tm), _load_token_major(r_sc.at[slot, 1], tm)


def _combine_kernel(d1_ref, d2_ref, h_ref, info_ref, y_hbm, o_ref, r_sc, sem, *, row0):
    y1, y2 = _gather_moe_rows(d1_ref, d2_ref, y_hbm, r_sc, sem, h_ref.shape[0], row0)
    info = info_ref[...]
    o_ref[...] = h_ref[...] + (y1 * info[:, 2:3] + y2 * info[:, 3:4])


def _combine(dest1, dest2, h, info, ybuf, row0, nrows):
    d = h.shape[1]
    tm = TOKEN_TILE
    base_tile = row0 // tm
    rows = lambda i, a, b: (base_tile + i, 0)
    grid_spec = pltpu.PrefetchScalarGridSpec(
        num_scalar_prefetch=2,
        grid=(nrows // tm,),
        in_specs=[pl.BlockSpec((tm, d), rows), pl.BlockSpec((tm, LANES), rows), pl.BlockSpec(memory_space=pl.ANY)],
        out_specs=pl.BlockSpec((tm, d), lambda i, a, b: (i, 0)),
        scratch_shapes=[pltpu.VMEM((2, 2, tm * ROW_TILES, LANES), F32), pltpu.SemaphoreType.DMA((2,))],
    )
    return pl.pallas_call(
        functools.partial(_combine_kernel, row0=row0),
        grid_spec=grid_spec,
        out_shape=jax.ShapeDtypeStruct((nrows, d), F32),
        compiler_params=_cparams(("arbitrary",)),
        name="moe_combine",
    )(dest1, dest2, h, info, ybuf)


def _combine_qkv_kernel(d1_ref, d2_ref, hp_ref, hs_ref, info_ref, y_hbm, kvn_ref, an_ref, wkv_ref, wq_ref, kn_ref,
                        qn_ref, rc_ref, rs1_ref, rs2_ref, hsum_ref, rq_ref, eq_ref,
                        h_ref, q_ref, k_ref, v_ref, r_sc, sem, *, n_p):
    y1, y2 = _gather_moe_rows(d1_ref, d2_ref, y_hbm, r_sc, sem, h_ref.shape[0])
    info = info_ref[...]
    h = jnp.where(pl.program_id(0) < n_p, hp_ref[...], hs_ref[...]) + (y1 * info[:, 2:3] + y2 * info[:, 3:4])
    h_ref[...] = h
    _qkv_math(h, kvn_ref, an_ref, wkv_ref, wq_ref, kn_ref, qn_ref, rc_ref, rs1_ref, rs2_ref,
              hsum_ref, rq_ref, eq_ref, q_ref, k_ref, v_ref)


def _combine_qkv(dest1, dest2, h_p, h_s, info, ybuf, kvn, an, wkv, wq, knt, qnt, rc, rs1, rs2, hsum, rq, eq):
    d = h_p.shape[1]
    tm = TOKEN_TILE
    n_p = h_p.shape[0] // tm
    t = h_p.shape[0] + h_s.shape[0]
    kvw = wkv.shape[1] // 2
    qw = wq.shape[1]
    rows = lambda i, a, b: (i, 0)
    const = lambda arr: pl.BlockSpec(arr.shape, lambda i, a, b: (0,) * arr.ndim)
    grid_spec = pltpu.PrefetchScalarGridSpec(
        num_scalar_prefetch=2,
        grid=(t // tm,),
        in_specs=[pl.BlockSpec((tm, d), lambda i, a, b: (jnp.minimum(i, n_p - 1), 0)),
                  pl.BlockSpec((tm, d), lambda i, a, b: (jnp.maximum(i - n_p, 0), 0)),
                  pl.BlockSpec((tm, LANES), rows), pl.BlockSpec(memory_space=pl.ANY)]
        + [const(a) for a in (kvn, an, wkv, wq, knt, qnt)] + [pl.BlockSpec((tm, LANES), rows)] * 3
        + [const(a) for a in (hsum, rq, eq)],
        out_specs=[pl.BlockSpec((tm, d), rows), pl.BlockSpec((tm, qw), rows), pl.BlockSpec((tm, kvw), rows),
                   pl.BlockSpec((tm, kvw), rows)],
        scratch_shapes=[pltpu.VMEM((2, 2, tm * ROW_TILES, LANES), F32), pltpu.SemaphoreType.DMA((2,))],
    )
    return pl.pallas_call(
        functools.partial(_combine_qkv_kernel, n_p=n_p),
        grid_spec=grid_spec,
        out_shape=[jax.ShapeDtypeStruct((t, d), F32), jax.ShapeDtypeStruct((t, qw), F32),
                   jax.ShapeDtypeStruct((t, kvw), F32), jax.ShapeDtypeStruct((t, kvw), F32)],
        compiler_params=_cparams(("arbitrary",)),
        name="combine_qkv",
    )(dest1, dest2, h_p, h_s, info, ybuf, kvn, an, wkv, wq, knt, qnt, rc, rs1, rs2, hsum, rq, eq)


def _router_weights(w_grp, b_grp, w_rt, b_rt):
    d = w_rt.shape[0]
    w_cat = jnp.zeros((d, LANES), F32).at[:, :MOE_EXPERTS].set(w_rt).at[:, MOE_EXPERTS:MOE_EXPERTS + MOE_GROUPS].set(w_grp)
    b_cat = jnp.zeros((1, LANES), F32).at[0, :MOE_EXPERTS].set(b_rt).at[0, MOE_EXPERTS:MOE_EXPERTS + MOE_GROUPS].set(b_grp)
    w_hi = w_cat.astype(BF16)
    w_lo = (w_cat - w_hi.astype(F32)).astype(BF16)
    return w_hi, w_lo, b_cat


def _moe_experts(xn_tm, info_t, cnt, wg, wu, wd, layer):
    t = info_t.shape[1]
    counts = cnt[:MOE_EXPERTS, 0].astype(I32)
    pcounts = (counts + EXPERT_ROWS - 1) // EXPERT_ROWS * EXPERT_ROWS
    pends = jnp.cumsum(pcounts)
    pstarts = pends - pcounts
    nblk = -(-(2 * t + MOE_EXPERTS * (EXPERT_ROWS - 1)) // EXPERT_ROWS)
    blk_start = jnp.arange(nblk, dtype=I32) * EXPERT_ROWS
    blk_e = jnp.minimum(jnp.sum((pends[None, :] <= blk_start[:, None]).astype(I32), axis=1), MOE_EXPERTS - 1)
    nvalid = pends[-1:] // EXPERT_ROWS
    eid = jnp.arange(MOE_EXPERTS, dtype=I32)
    later = (eid[None, :] > eid[:, None]) & (pcounts[None, :] > 0)
    next_e = jnp.min(jnp.where(later, eid[None, :], MOE_EXPERTS), axis=1).astype(I32)

    dest = _dest(pstarts, info_t)
    dest1, dest2 = dest[0], dest[1]
    xbuf = _dispatch(dest1, dest2, pends, pcounts, xn_tm, nblk * EXPERT_ROWS)
    return dest1, dest2, _experts(blk_e, nvalid, next_e, xbuf, wg, wu, wd, layer)


def _rope_tables(pos):
    half = ROT_DIM // 2
    inv = ROPE_THETA ** (-jnp.arange(0, ROT_DIM, 2, dtype=F32) / ROT_DIM)
    ang = pos.astype(F32)[:, None] * inv[None, :]
    cos, sin = jnp.cos(ang), jnp.sin(ang)
    n = pos.shape[0]
    ones = jnp.ones((n, ATT_HEAD_DIM - ROT_DIM), F32)
    zeros_r = jnp.zeros((n, ATT_HEAD_DIM - ROT_DIM), F32)
    zeros_h = jnp.zeros((n, half), F32)
    c = jnp.concatenate([cos, cos, ones], axis=1)
    s1 = jnp.concatenate([-sin, zeros_h, zeros_r], axis=1)
    s2 = jnp.concatenate([zeros_h, sin, zeros_r], axis=1)
    reps = LANES // ATT_HEAD_DIM
    return tuple(jnp.tile(a, (1, reps)) for a in (c, s1, s2))


def _to_step_order(a, nsteps, n_tok):
    c = a.shape[-1]
    return a.reshape(nsteps, SEQ_PER_STEP, n_tok, c).transpose(0, 2, 1, 3).reshape(nsteps * n_tok * SEQ_PER_STEP, c)


def _from_step_order(a, nsteps, n_tok):
    c = a.shape[-1]
    return a.reshape(nsteps, n_tok, SEQ_PER_STEP, c).transpose(0, 2, 1, 3).reshape(nsteps * SEQ_PER_STEP, n_tok, c)


def kernel(x_prompt, x_sample, state_ssm, state_conv, cache_k_win, cache_v_win, ssm_norm, ssm_w_in, ssm_conv_w, ssm_conv_b, ssm_dt_bias, ssm_a_log, ssm_d, ssm_gate_norm, ssm_w_out, kv_norm, w_kv, k_norm, attn_norm, w_q, q_norm, sinks, w_o, ffn_norm, moe_w_group, moe_b_group, moe_w_router, moe_b_router, moe_w_gate, moe_w_up, moe_w_down):
    bp, seq, d = x_prompt.shape
    bs, n_tok, _ = x_sample.shape
    tp, ts = bp * seq, bs * n_tok
    nsteps = bs // SEQ_PER_STEP
    n_heads = ssm_d.shape[1]
    di = n_heads * SSM_HEAD_DIM
    gn_w = SSM_GROUPS * SSM_STATE
    cdim = di + 2 * gn_w
    n_q = sinks.shape[1]
    kvw = ATT_KV_HEADS * ATT_HEAD_DIM

    xp2 = x_prompt.reshape(tp, d)
    xs2 = _to_step_order(x_sample, nsteps, n_tok)

    lane_i = jnp.arange(LANES)
    e01 = (lane_i[:, None] == (jnp.arange(di) // SSM_HEAD_DIM)[None, :]).astype(BF16)
    hpg = di // SSM_GROUPS
    g1 = ((jnp.arange(gn_w) // SSM_STATE)[:, None] == (lane_i // (hpg // SSM_HEAD_DIM))[None, :])
    g1 = (g1 & (lane_i < n_heads)[None, :]).astype(BF16)
    tril_c = (jnp.arange(SSM_CHUNK)[:, None] >= jnp.arange(SSM_CHUNK)[None, :]).astype(BF16)
    earlier_x = (jnp.arange(TOKEN_TILE)[:, None] < jnp.arange(TOKEN_TILE)[None, :]).astype(BF16)
    hsum = ((jnp.arange(kvw) // ATT_HEAD_DIM)[:, None] == (jnp.arange(kvw) // ATT_HEAD_DIM)[None, :]).astype(BF16)
    qw = n_q * ATT_HEAD_DIM
    rq = ((jnp.arange(qw) // ATT_HEAD_DIM)[:, None] == lane_i[None, :]).astype(BF16)
    eq = rq.T

    w_in = ssm_w_in[0]
    wz = w_in[:, :di].astype(BF16)
    wx = w_in[:, di:di + cdim].astype(BF16)
    wd = jnp.zeros((d, LANES), F32).at[:, :n_heads].set(w_in[:, di + cdim:]).astype(BF16)
    cw, cb = ssm_conv_w[0], ssm_conv_b[0].reshape(1, cdim)
    z, xbc, dt = _inproj(xp2, xs2, ssm_norm[0].reshape(1, d), wz, wx, wd)

    pad_h = lambda v: jnp.zeros((1, LANES), F32).at[0, :n_heads].set(v)
    dtb, alog = pad_h(ssm_dt_bias[0]), pad_h(ssm_a_log[0])
    dsk = jnp.repeat(ssm_d[0], SSM_HEAD_DIM).reshape(1, di)
    gnw = ssm_gate_norm[0].reshape(1, di)

    w_out = ssm_w_out[0].astype(BF16)
    h_p, s_fin, c_fin = _ssd_prompt(z, xbc, dt, xp2, cw, cb, dtb, alog, dsk, gnw, tril_c, e01, w_out, bp, seq)
    ssm_p = s_fin.reshape(1, bp, n_heads, SSM_HEAD_DIM, SSM_STATE)
    conv_p = c_fin[:, SUBLANES - (SSM_CONV - 1):, :].reshape(1, bp, SSM_CONV - 1, cdim)

    xbc_s = xbc[tp:].reshape(nsteps, n_tok, SEQ_PER_STEP, cdim)
    conv_in = state_conv[0].reshape(nsteps, SEQ_PER_STEP, SSM_CONV - 1, cdim).transpose(0, 2, 1, 3)
    xp7 = jnp.concatenate([conv_in, xbc_s], axis=1)
    s0 = state_ssm[0].reshape(bs, di, SSM_STATE)
    h_s, s_new = _ssd_sample(z, xp7, dt, s0, xs2, cw, cb, dtb, alog, dsk, gnw, e01, g1, w_out, tp, n_tok)
    ssm_s = s_new.reshape(1, bs, n_heads, SSM_HEAD_DIM, SSM_STATE)
    conv_s = _from_step_order(xbc[tp:], nsteps, n_tok)[:, n_tok - (SSM_CONV - 1):, :].reshape(
        1, bs, SSM_CONV - 1, cdim)

    w_hi, w_lo, b_cat = _router_weights(moe_w_group[0], moe_b_group[0], moe_w_router[0], moe_b_router[0])
    xn_tm, info, info_t, cnt = _router(h_p, h_s, ffn_norm[0].reshape(1, d), w_hi, w_lo, b_cat, earlier_x)
    dest1, dest2, ybuf = _moe_experts(xn_tm, info_t, cnt, moe_w_gate, moe_w_up, moe_w_down, 0)

    pos = jnp.concatenate([jnp.tile(jnp.arange(seq, dtype=I32), bp),
                           jnp.tile(jnp.repeat(PAST_LEN + jnp.arange(n_tok, dtype=I32), SEQ_PER_STEP), nsteps)])
    rc, rs1, rs2 = _rope_tables(pos)
    h1, q, k, v = _combine_qkv(dest1, dest2, h_p, h_s, info, ybuf, kv_norm.reshape(1, d), attn_norm[0].reshape(1, d),
                               w_kv.astype(BF16), w_q[0].astype(BF16),
                               jnp.tile(k_norm, ATT_KV_HEADS).reshape(1, kvw), jnp.tile(q_norm[0], n_q).reshape(1, qw),
                               rc, rs1, rs2, hsum, rq, eq)
    sk = sinks[0]
    lc = cache_k_win.shape[1]
    kc = cache_k_win.reshape(bs, lc, kvw)
    vc = cache_v_win.reshape(bs, lc, kvw)
    o_p = _attn_prompt(sk, q, k, v, bp, seq)
    o_s, k_win, v_win = _attn_sample(sk, q, k, v, kc, vc, tp, n_tok)
    w_hi, w_lo, b_cat = _router_weights(moe_w_group[1], moe_b_group[1], moe_w_router[1], moe_b_router[1])
    h2, xn_tm, info, info_t, cnt = _wo_router(o_p, o_s, h1, w_o[0].astype(BF16), ffn_norm[1].reshape(1, d),
                                             w_hi, w_lo, b_cat, earlier_x)
    dest1, dest2, ybuf = _moe_experts(xn_tm, info_t, cnt, moe_w_gate, moe_w_up, moe_w_down, 1)
    y_p = _combine(dest1, dest2, h2, info, ybuf, 0, tp)
    y_s = _combine(dest1, dest2, h2, info, ybuf, tp, ts)

    wl = min(WINDOW, seq)
    k_p = k[:tp].reshape(bp, seq, kvw)[:, seq - wl:].reshape(bp, wl, ATT_KV_HEADS, ATT_HEAD_DIM)
    v_p = v[:tp].reshape(bp, seq, kvw)[:, seq - wl:].reshape(bp, wl, ATT_KV_HEADS, ATT_HEAD_DIM)
    k_s = k_win.reshape(bs, lc, ATT_KV_HEADS, ATT_HEAD_DIM)
    v_s = v_win.reshape(bs, lc, ATT_KV_HEADS, ATT_HEAD_DIM)
    return (y_p.reshape(bp, seq, d), _from_step_order(y_s, nsteps, n_tok),
            ssm_p, conv_p, k_p, v_p, ssm_s, conv_s, k_s, v_s)
```

```python
import functools

import numpy as np
import jax
import jax.numpy as jnp
from jax import lax
from jax.experimental import pallas as pl
from jax.experimental.pallas import tpu as pltpu

F32 = jnp.float32
BF16 = jnp.bfloat16
I32 = jnp.int32

EPS = 1e-6
SSM_HEAD_DIM = 64
SSM_GROUPS = 4
SSM_STATE = 128
SSM_CONV = 4
SSM_CHUNK = 128
ATT_HEAD_DIM = 64
ATT_KV_HEADS = 4
WINDOW = 128
ROT_DIM = ATT_HEAD_DIM // 4
ROPE_THETA = 500000.0
MOE_GROUPS = 4
MOE_EXPERTS_PER_GROUP = 8
MOE_EXPERTS = MOE_GROUPS * MOE_EXPERTS_PER_GROUP
MOE_BLOCK = 128
PAST_LEN = 16384

LANES = 128
SUBLANES = 8
BF16_ROWS = 16
SEQ_PER_STEP = SUBLANES
TOKEN_TILE = 256
ROW_TILES = 8
ATTN_STACK = 2
DISPATCH_TILE = 512
EXPERT_RING = 3
EXPERT_ROWS = 512
VMEM_LIMIT = 56 * 1024 * 1024


def _cparams(sem):
    return pltpu.CompilerParams(dimension_semantics=sem, vmem_limit_bytes=VMEM_LIMIT)


def _const_spec(shape):
    nd = len(shape)
    return pl.BlockSpec(shape, lambda *_: (0,) * nd)


def _split_bf16(v, n):
    parts = []
    r = v
    for k in range(n):
        p = r.astype(BF16)
        parts.append(p)
        if k + 1 < n:
            r = r - p.astype(F32)
    return parts


def _mm01(v, m01, n=3):
    acc = None
    for p in _split_bf16(v, n):
        d = jnp.dot(p, m01, preferred_element_type=F32)
        acc = d if acc is None else acc + d
    return acc


def _mm01_left(m01, v, n=3):
    acc = None
    for p in _split_bf16(v, n):
        d = jnp.dot(m01, p, preferred_element_type=F32)
        acc = d if acc is None else acc + d
    return acc


def _dot_nt(a, b):
    return lax.dot_general(a, b, (((1,), (1,)), ((), ())), preferred_element_type=F32)


def _sigmoid(x):
    return 0.5 * jnp.tanh(0.5 * x) + 0.5


def _silu(x):
    return x * _sigmoid(x)


def _softplus(x):
    return jnp.maximum(x, 0.0) + jnp.log1p(jnp.exp(-jnp.abs(x)))


def _rms_scale(x):
    return x * lax.rsqrt(jnp.mean(x * x, axis=-1, keepdims=True) + EPS)


def _gate_norm(y, z, gn, n_groups):
    yz = y * _silu(z)
    w = yz.shape[-1] // n_groups
    outs = []
    for g in range(n_groups):
        v = yz[:, g * w:(g + 1) * w]
        outs.append(_rms_scale(v) * gn[:, g * w:(g + 1) * w])
    return jnp.concatenate(outs, axis=1)


def _rope(x, c, s1, s2):
    w = x.shape[-1]
    return x * c + pltpu.roll(x, w - ROT_DIM // 2, 1) * s1 + pltpu.roll(x, ROT_DIM // 2, 1) * s2


def _tile_lanes(t, reps):
    return t if reps == 1 else jnp.concatenate([t] * reps, axis=1)


def _inproj_kernel(xp_ref, xs_ref, g_ref, wz_ref, wx_ref, wd_ref, z_ref, xbc_ref, dt_ref, *, n_p):
    i = pl.program_id(0)
    x = jnp.where(i < n_p, xp_ref[...], xs_ref[...])
    xn = (_rms_scale(x) * g_ref[...]).astype(BF16)
    z_ref[...] = jnp.dot(xn, wz_ref[...], preferred_element_type=F32)
    xbc_ref[...] = jnp.dot(xn, wx_ref[...], preferred_element_type=F32)
    dt_ref[...] = jnp.dot(xn, wd_ref[...], preferred_element_type=F32)


def _inproj(xp2, xs2, g, wz, wx, wd):
    tp, d = xp2.shape
    ts = xs2.shape[0]
    tm = TOKEN_TILE
    n_p, n_s = tp // tm, ts // tm
    t = tp + ts
    return pl.pallas_call(
        functools.partial(_inproj_kernel, n_p=n_p),
        grid=(n_p + n_s,),
        in_specs=_two_source_specs(tm, d, n_p) + [_const_spec(a.shape) for a in (g, wz, wx, wd)],
        out_specs=[
            pl.BlockSpec((tm, wz.shape[1]), lambda i: (i, 0)),
            pl.BlockSpec((tm, wx.shape[1]), lambda i: (i, 0)),
            pl.BlockSpec((tm, wd.shape[1]), lambda i: (i, 0)),
        ],
        out_shape=[jax.ShapeDtypeStruct((t, wz.shape[1]), F32),
                   jax.ShapeDtypeStruct((t, wx.shape[1]), F32),
                   jax.ShapeDtypeStruct((t, wd.shape[1]), F32)],
        compiler_params=_cparams(("arbitrary",)),
        name="inproj",
    )(xp2, xs2, g, wz, wx, wd)


def _ssd_prompt_kernel(z_ref, xbc_ref, dt_ref, x_ref, cw_ref, cb_ref, dtb_ref, alog_ref, dsk_ref, gn_ref,
                       tril_ref, e_ref, wout_ref, h_ref, sfin_ref, cfin_ref, xpad_sc, st_sc):
    c = pl.program_id(1)
    q = SSM_CHUNK
    cd = xbc_ref.shape[1]
    di = z_ref.shape[1]
    gn_w = SSM_GROUPS * SSM_STATE
    hpg = di // SSM_GROUPS
    pad = SUBLANES

    n_slab = cd // LANES

    @pl.when(c == 0)
    def _():
        xpad_sc[:, 0:pad, :] = jnp.zeros((n_slab, pad, LANES), F32)
        st_sc[...] = jnp.zeros(st_sc.shape, F32)

    @pl.when(c > 0)
    def _():
        xpad_sc[:, 0:pad, :] = xpad_sc[:, q:q + pad, :]

    slabs = []
    for j in range(n_slab):
        ls = slice(j * LANES, (j + 1) * LANES)
        xpad_sc[j, pad:pad + q, :] = xbc_ref[:, ls]
        acc = cb_ref[:, ls]
        for k in range(SSM_CONV):
            off = pad - (SSM_CONV - 1) + k
            acc = acc + xpad_sc[j, off:off + q, :] * cw_ref[k:k + 1, ls]
        slabs.append(_silu(acc))
    xc = jnp.concatenate(slabs, axis=1)
    xs = xc[:, :di]
    bm = xc[:, di:di + gn_w]
    cm = xc[:, di + gn_w:]

    dt = _softplus(dt_ref[...] + dtb_ref[...])
    a = -jnp.exp(alog_ref[...])
    act = _mm01_left(tril_ref[...], dt * a)
    act_t = act.T
    act_last = act[q - 1:q, :]
    pieces = (_split_bf16(dt, 2) + _split_bf16(jnp.exp(act_last - act), 2) + _split_bf16(jnp.exp(act), 2)
              + _split_bf16(jnp.exp(act[q - BF16_ROWS:q, :]), 3))
    ex = jnp.dot(jnp.concatenate(pieces, axis=0), e_ref[...], preferred_element_type=F32)
    xdt = xs * (ex[0:q] + ex[q:2 * q])
    xd = xdt * (ex[2 * q:3 * q] + ex[3 * q:4 * q])
    eax = ex[4 * q:5 * q] + ex[5 * q:6 * q]
    tail = ex[6 * q:]
    cd = (tail[0:BF16_ROWS] + tail[BF16_ROWS:2 * BF16_ROWS]) + tail[2 * BF16_ROWS:]
    cdx = cd[BF16_ROWS - 1:BF16_ROWS, :]

    row = lax.broadcasted_iota(I32, (q, q), 0)
    col = lax.broadcasted_iota(I32, (q, q), 1)
    causal = row >= col
    lane = lax.broadcasted_iota(I32, (q, LANES), 1)
    lo_half = lane < SSM_HEAD_DIM

    y_parts = []
    heads_per_group = hpg // SSM_HEAD_DIM
    for g in range(SSM_GROUPS):
        cg = cm[:, g * SSM_STATE:(g + 1) * SSM_STATE].astype(BF16)
        bg = bm[:, g * SSM_STATE:(g + 1) * SSM_STATE]
        cb = _dot_nt(cg, bg.astype(BF16))
        st_g = st_sc[:, g * hpg:(g + 1) * hpg]
        y_off = jnp.dot(cg, st_g.astype(BF16), preferred_element_type=F32)
        for pr in range(heads_per_group // 2):
            h0 = g * heads_per_group + 2 * pr
            ms = []
            for h in (h0, h0 + 1):
                seg = act[:, h:h + 1] - act_t[h:h + 1, :]
                lm = jnp.exp(jnp.where(causal, seg, -jnp.inf))
                ms.append((cb * lm).astype(BF16))
            m2 = jnp.concatenate(ms, axis=1)
            xpair = xdt[:, h0 * SSM_HEAD_DIM:(h0 + 2) * SSM_HEAD_DIM]
            rhs = jnp.concatenate([jnp.where(lo_half, xpair, 0.0),
                                   jnp.where(lo_half, 0.0, xpair)], axis=0).astype(BF16)
            y_d = jnp.dot(m2, rhs, preferred_element_type=F32)
            lo = 2 * pr * SSM_HEAD_DIM
            y_parts.append(y_d + y_off[:, lo:lo + LANES] * eax[:, g * hpg + lo:g * hpg + lo + LANES])
        upd = jnp.dot(bg.T.astype(BF16), xd[:, g * hpg:(g + 1) * hpg].astype(BF16),
                      preferred_element_type=F32)
        st_sc[:, g * hpg:(g + 1) * hpg] = st_g * cdx[:, g * hpg:(g + 1) * hpg] + upd

    y = jnp.concatenate(y_parts, axis=1) + xs * dsk_ref[...]
    yg = _gate_norm(y, z_ref[...], gn_ref[...], SSM_GROUPS)
    h_ref[...] = x_ref[...] + jnp.dot(yg.astype(BF16), wout_ref[...], preferred_element_type=F32)

    @pl.when(c == pl.num_programs(1) - 1)
    def _():
        sfin_ref[0] = st_sc[...].T
        cfin_ref[0] = jnp.concatenate([xpad_sc[j, q:q + pad, :] for j in range(n_slab)], axis=1)


def _ssd_prompt(z, xbc, dt, x, cw, cb, dtb, alog, dsk, gnw, tril, e01, wout, bp, seq):
    nc = seq // SSM_CHUNK
    q = SSM_CHUNK
    di, cd, d = z.shape[1], xbc.shape[1], x.shape[1]
    rows = lambda b, c: (b * nc + c, 0)
    return pl.pallas_call(
        _ssd_prompt_kernel,
        grid=(bp, nc),
        in_specs=[
            pl.BlockSpec((q, di), rows), pl.BlockSpec((q, cd), rows), pl.BlockSpec((q, LANES), rows),
            pl.BlockSpec((q, d), rows),
            _const_spec(cw.shape), _const_spec(cb.shape), _const_spec(dtb.shape), _const_spec(alog.shape),
            _const_spec(dsk.shape), _const_spec(gnw.shape), _const_spec(tril.shape), _const_spec(e01.shape),
            _const_spec(wout.shape),
        ],
        out_specs=[
            pl.BlockSpec((q, d), rows),
            pl.BlockSpec((1, di, SSM_STATE), lambda b, c: (b, 0, 0)),
            pl.BlockSpec((1, SUBLANES, cd), lambda b, c: (b, 0, 0)),
        ],
        out_shape=[jax.ShapeDtypeStruct((bp * seq, d), F32),
                   jax.ShapeDtypeStruct((bp, di, SSM_STATE), F32),
                   jax.ShapeDtypeStruct((bp, SUBLANES, cd), F32)],
        scratch_shapes=[pltpu.VMEM((cd // LANES, q + 2 * SUBLANES, LANES), F32), pltpu.VMEM((SSM_STATE, di), F32)],
        compiler_params=_cparams(("arbitrary", "arbitrary")),
        name="ssd_prompt",
    )(z, xbc, dt, x, cw, cb, dtb, alog, dsk, gnw, tril, e01, wout)


def _ssd_sample_kernel(z_ref, xp_ref, dt_ref, s0_ref, x_ref, cw_ref, cb_ref, dtb_ref, alog_ref, dsk_ref, gn_ref,
                       e_ref, g1_ref, wout_ref, h_ref, sn_ref, yoff_sc, *, n_tok):
    hf = pl.program_id(1)
    nb = SEQ_PER_STEP
    half = nb // 2
    q = n_tok * nb
    di = z_ref.shape[1]
    gn_w = SSM_GROUPS * SSM_STATE
    hpg = di // SSM_GROUPS

    taps = [xp_ref[0, m] for m in range(n_tok + SSM_CONV - 1)]
    slabs = []
    for t in range(n_tok):
        acc = cb_ref[...]
        for k in range(SSM_CONV):
            acc = acc + taps[t + k] * cw_ref[k:k + 1, :]
        slabs.append(_silu(acc))
    xc = jnp.concatenate(slabs, axis=0)
    xs = xc[:, :di]
    bm = xc[:, di:di + gn_w]
    cm = xc[:, di + gn_w:]

    dt = _softplus(dt_ref[...] + dtb_ref[...])
    da = dt * (-jnp.exp(alog_ref[...]))
    acts = []
    run = None
    for t in range(n_tok):
        d = da[t * nb:(t + 1) * nb, :]
        run = d if run is None else run + d
        acts.append(run)
    act = jnp.concatenate(acts, axis=0)
    act_last = jnp.concatenate([acts[-1]] * n_tok, axis=0)
    e01 = e_ref[...]
    xdt = xs * _mm01(dt, e01)
    xd = xdt * _mm01(jnp.exp(act_last - act), e01)
    eax = _mm01(jnp.exp(act), e01)
    cdx = _mm01(jnp.exp(acts[-1]), e01)

    pairs = [(t, u) for t in range(n_tok) for u in range(t + 1)]
    cbp = jnp.concatenate([cm[t * nb:(t + 1) * nb, :] * bm[u * nb:(u + 1) * nb, :] for t, u in pairs], axis=0)
    seg = jnp.concatenate([acts[t] - acts[u] for t, u in pairs], axis=0)
    coef = _mm01(_mm01(cbp, g1_ref[...]) * jnp.exp(seg), e01)
    y_slabs = []
    for t in range(n_tok):
        acc = None
        for pi, (tt, u) in enumerate(pairs):
            if tt != t:
                continue
            term = coef[pi * nb:(pi + 1) * nb, :] * xdt[u * nb:(u + 1) * nb, :]
            acc = term if acc is None else acc + term
        y_slabs.append(acc)
    y_diag = jnp.concatenate(y_slabs, axis=0)

    zpad = jnp.concatenate([xd,
                            jnp.where(hf == 0, cdx[0:half, :], cdx[half:nb, :]),
                            jnp.zeros((LANES - q - half, di), F32)], axis=0)
    zt = zpad.T
    row_seq = lax.broadcasted_iota(I32, (q, 1), 0) % nb
    cm_b = cm.astype(BF16)
    y_off_g = [None] * SSM_GROUPS
    for sl in range(half):
        in_seq = row_seq == hf * half + sl
        for g in range(SSM_GROUPS):
            s_old = s0_ref[sl, g * hpg:(g + 1) * hpg, :]
            c_g = jnp.where(in_seq, cm_b[:, g * SSM_STATE:(g + 1) * SSM_STATE], jnp.zeros((), BF16))
            yo = _dot_nt(c_g, s_old.astype(BF16))
            y_off_g[g] = yo if y_off_g[g] is None else y_off_g[g] + yo
            b_g = jnp.where(in_seq, bm[:, g * SSM_STATE:(g + 1) * SSM_STATE], 0.0)
            b_pad = jnp.concatenate([b_g, jnp.zeros((LANES - q, SSM_STATE), F32)], axis=0).astype(BF16)
            zt_g = zt[g * hpg:(g + 1) * hpg, :]
            upd = jnp.dot(zt_g.astype(BF16), b_pad, preferred_element_type=F32)
            decay = zt_g[:, q + sl:q + sl + 1]
            sn_ref[sl, g * hpg:(g + 1) * hpg, :] = s_old * decay + upd
    y_off = jnp.concatenate(y_off_g, axis=1)

    @pl.when(hf == 0)
    def _():
        yoff_sc[...] = y_off

    @pl.when(hf == 1)
    def _():
        y = y_diag + (yoff_sc[...] + y_off) * eax + xs * dsk_ref[...]
        yg = _gate_norm(y, z_ref[...], gn_ref[...], SSM_GROUPS)
        h_ref[...] = x_ref[...] + jnp.dot(yg.astype(BF16), wout_ref[...], preferred_element_type=F32)


def _ssd_sample(z, xp7, dt, s0, x, cw, cb, dtb, alog, dsk, gnw, e01, g1, wout, tp, n_tok):
    nsteps = xp7.shape[0]
    nb = SEQ_PER_STEP
    half = nb // 2
    q = n_tok * nb
    di, d = z.shape[1], x.shape[1]
    cd = xp7.shape[3]
    base = tp // q
    rows = lambda s, hf: (base + s, 0)
    return pl.pallas_call(
        functools.partial(_ssd_sample_kernel, n_tok=n_tok),
        grid=(nsteps, 2),
        in_specs=[
            pl.BlockSpec((q, di), rows),
            pl.BlockSpec((1, n_tok + SSM_CONV - 1, nb, cd), lambda s, hf: (s, 0, 0, 0)),
            pl.BlockSpec((q, LANES), rows),
            pl.BlockSpec((half, di, SSM_STATE), lambda s, hf: (2 * s + hf, 0, 0)),
            pl.BlockSpec((q, d), lambda s, hf: (s, 0)),
            _const_spec(cw.shape), _const_spec(cb.shape), _const_spec(dtb.shape), _const_spec(alog.shape),
            _const_spec(dsk.shape), _const_spec(gnw.shape), _const_spec(e01.shape), _const_spec(g1.shape),
            _const_spec(wout.shape),
        ],
        out_specs=[
            pl.BlockSpec((q, d), lambda s, hf: (s, 0)),
            pl.BlockSpec((half, di, SSM_STATE), lambda s, hf: (2 * s + hf, 0, 0)),
        ],
        out_shape=[jax.ShapeDtypeStruct((nsteps * q, d), F32),
                   jax.ShapeDtypeStruct(s0.shape, F32)],
        scratch_shapes=[pltpu.VMEM((q, di), F32)],
        compiler_params=_cparams(("arbitrary", "arbitrary")),
        name="ssd_sample",
    )(z, xp7, dt, s0, x, cw, cb, dtb, alog, dsk, gnw, e01, g1, wout)


def _qkv_math(h, kvn_ref, an_ref, wkv_ref, wq_ref, kn_ref, qn_ref, rc_ref, rs1_ref, rs2_ref,
              hsum_ref, rq_ref, eq_ref, q_ref, k_ref, v_ref):
    hn = _rms_scale(h)
    kvw = k_ref.shape[1]
    kv = jnp.dot((hn * kvn_ref[...]).astype(BF16), wkv_ref[...], preferred_element_type=F32)
    k = kv[:, :kvw]
    v_ref[...] = kv[:, kvw:]
    inv_hd = 1.0 / ATT_HEAD_DIM
    k = k * lax.rsqrt(_mm01(k * k, hsum_ref[...], 2) * inv_hd + EPS) * kn_ref[...]
    rc, rs1, rs2 = rc_ref[...], rs1_ref[...], rs2_ref[...]
    rk = kvw // LANES
    k_ref[...] = _rope(k, _tile_lanes(rc, rk), _tile_lanes(rs1, rk), _tile_lanes(rs2, rk))
    q = jnp.dot((hn * an_ref[...]).astype(BF16), wq_ref[...], preferred_element_type=F32)
    rsq = lax.rsqrt(_mm01(q * q, rq_ref[...], 2) * inv_hd + EPS)
    q = q * _mm01(rsq, eq_ref[...], 2) * qn_ref[...]
    rq = q.shape[1] // LANES
    q_ref[...] = _rope(q, _tile_lanes(rc, rq), _tile_lanes(rs1, rq), _tile_lanes(rs2, rq))


def _attn_prompt_kernel(sink_ref, q_ref, kc_ref, kp_ref, vc_ref, vp_ref, o_ref):
    i = pl.program_id(1)
    w = WINDOW
    hd = ATT_HEAD_DIM
    n_q = q_ref.shape[1] // hd
    grp = n_q // ATT_KV_HEADS
    stack = ATTN_STACK
    rows = stack * w
    row = lax.broadcasted_iota(I32, (rows, 2 * w), 0) % w
    col = lax.broadcasted_iota(I32, (rows, 2 * w), 1)
    dist = row + w - col
    mask = (dist >= 0) & (dist < w) & ((col >= w) | (i > 0))
    r_head = lax.broadcasted_iota(I32, (rows, 1), 0) // w
    q = q_ref[...] * (hd ** -0.5)
    outs = [None] * n_q
    for g in range(ATT_KV_HEADS):
        sl = slice(g * hd, (g + 1) * hd)
        kk = jnp.concatenate([kp_ref[:, sl], kc_ref[:, sl]], axis=0).astype(BF16)
        vv = jnp.concatenate([vp_ref[:, sl], vc_ref[:, sl]], axis=0).astype(BF16)
        for h0 in range(g * grp, (g + 1) * grp, stack):
            qs = jnp.concatenate([q[:, h * hd:(h + 1) * hd] for h in range(h0, h0 + stack)],
                                 axis=0).astype(BF16)
            s = jnp.where(mask, _dot_nt(qs, kk), -jnp.inf)
            sink = jnp.zeros((rows, 1), F32)
            for j in range(stack):
                sink = jnp.where(r_head == j, sink_ref[h0 + j], sink)
            m = jnp.maximum(jnp.max(s, axis=-1, keepdims=True), sink)
            p = jnp.exp(s - m)
            denom = jnp.sum(p, axis=-1, keepdims=True) + jnp.exp(sink - m)
            o = jnp.dot(p.astype(BF16), vv, preferred_element_type=F32) * (1.0 / denom)
            for j in range(stack):
                outs[h0 + j] = o[j * w:(j + 1) * w, :]
    o_ref[...] = jnp.concatenate(outs, axis=1)


def _attn_prompt(sinks, q, k, v, bp, seq):
    w = WINDOW
    nb = seq // w
    qw, kvw = q.shape[1], k.shape[1]
    cur = lambda b, i: (b * nb + i, 0)
    prev = lambda b, i: (b * nb + jnp.maximum(i - 1, 0), 0)
    return pl.pallas_call(
        _attn_prompt_kernel,
        grid=(bp, nb),
        in_specs=[pl.BlockSpec(memory_space=pltpu.SMEM),
                  pl.BlockSpec((w, qw), cur), pl.BlockSpec((w, kvw), cur), pl.BlockSpec((w, kvw), prev),
                  pl.BlockSpec((w, kvw), cur), pl.BlockSpec((w, kvw), prev)],
        out_specs=pl.BlockSpec((w, qw), cur),
        out_shape=jax.ShapeDtypeStruct((bp * seq, qw), F32),
        compiler_params=_cparams(("arbitrary", "arbitrary")),
        name="attn_prompt",
    )(sinks, q, k, k, v, v)


def _attn_sample_kernel(sink_ref, q_ref, kn_ref, vn_ref, kc_ref, vc_ref, o_ref, kw_ref, vw_ref, *, n_tok):
    nb = SEQ_PER_STEP
    qn = n_tok * nb
    hd = ATT_HEAD_DIM
    lc = kc_ref.shape[1]
    for new_ref, old_ref, win_ref in ((kn_ref, kc_ref, kw_ref), (vn_ref, vc_ref, vw_ref)):
        new = new_ref[...]
        for sq in range(nb):
            win_ref[sq, 0:lc - n_tok, :] = old_ref[sq, n_tok:lc, :]
            win_ref[sq, lc - n_tok:lc, :] = jnp.concatenate(
                [new[t * nb + sq:t * nb + sq + 1, :] for t in range(n_tok)], axis=0)
    n_q = q_ref.shape[1] // hd
    grp = n_q // ATT_KV_HEADS
    rows = grp * qn
    r = lax.broadcasted_iota(I32, (rows, 1), 0)
    r_seq = r % nb
    r_tok = (r % qn) // nb
    ccol = lax.broadcasted_iota(I32, (rows, lc), 1)
    mask_c = ccol >= r_tok + 1 + (lc - WINDOW)
    ncol = lax.broadcasted_iota(I32, (rows, LANES), 1)
    mask_n = (ncol < qn) & (ncol % nb == r_seq) & (ncol // nb <= r_tok)
    q = q_ref[...]
    zpad = jnp.zeros((LANES - qn, hd), F32)
    outs = [None] * n_q
    for g in range(ATT_KV_HEADS):
        sl = slice(g * hd, (g + 1) * hd)
        qs = jnp.concatenate([q[:, (g * grp + hq) * hd:(g * grp + hq + 1) * hd] for hq in range(grp)],
                             axis=0).astype(BF16)
        k_new = jnp.concatenate([kn_ref[:, sl], zpad], axis=0).astype(BF16)
        v_new = jnp.concatenate([vn_ref[:, sl], zpad], axis=0).astype(BF16)
        s_n = jnp.where(mask_n, _dot_nt(qs, k_new) * (hd ** -0.5), -jnp.inf)
        s_c = jnp.zeros((rows, lc), F32)
        for sq in range(nb):
            s_sq = _dot_nt(qs, kc_ref[sq, :, sl].astype(BF16))
            s_c = jnp.where(r_seq == sq, s_sq, s_c)
        s_c = jnp.where(mask_c, s_c * (hd ** -0.5), -jnp.inf)
        sink = jnp.zeros((rows, 1), F32)
        for hq in range(grp):
            sink = jnp.where(r // qn == hq, sink_ref[g * grp + hq], sink)
        m = jnp.maximum(jnp.maximum(jnp.max(s_c, axis=-1, keepdims=True),
                                    jnp.max(s_n, axis=-1, keepdims=True)), sink)
        p_c = jnp.exp(s_c - m)
        p_n = jnp.exp(s_n - m)
        denom = (jnp.sum(p_c, axis=-1, keepdims=True) + jnp.sum(p_n, axis=-1, keepdims=True)
                 + jnp.exp(sink - m))
        p_c = p_c / denom
        o = jnp.dot((p_n / denom).astype(BF16), v_new, preferred_element_type=F32)
        for sq in range(nb):
            o = o + jnp.dot(jnp.where(r_seq == sq, p_c, 0.0).astype(BF16), vc_ref[sq, :, sl].astype(BF16),
                            preferred_element_type=F32)
        for hq in range(grp):
            outs[g * grp + hq] = o[hq * qn:(hq + 1) * qn, :]
    o_ref[...] = jnp.concatenate(outs, axis=1)


def _attn_sample(sinks, q, k, v, kc, vc, tp, n_tok):
    nb = SEQ_PER_STEP
    qn = n_tok * nb
    nsteps = kc.shape[0] // nb
    lc, kvw = kc.shape[1], kc.shape[2]
    qw = q.shape[1]
    base = tp // qn
    rows = lambda s: (base + s, 0)
    return pl.pallas_call(
        functools.partial(_attn_sample_kernel, n_tok=n_tok),
        grid=(nsteps,),
        in_specs=[pl.BlockSpec(memory_space=pltpu.SMEM),
                  pl.BlockSpec((qn, qw), rows), pl.BlockSpec((qn, kvw), rows), pl.BlockSpec((qn, kvw), rows),
                  pl.BlockSpec((nb, lc, kvw), lambda s: (s, 0, 0)), pl.BlockSpec((nb, lc, kvw), lambda s: (s, 0, 0))],
        out_specs=[pl.BlockSpec((qn, qw), lambda s: (s, 0)),
                   pl.BlockSpec((nb, lc, kvw), lambda s: (s, 0, 0)), pl.BlockSpec((nb, lc, kvw), lambda s: (s, 0, 0))],
        out_shape=[jax.ShapeDtypeStruct((nsteps * qn, qw), F32),
                   jax.ShapeDtypeStruct(kc.shape, F32), jax.ShapeDtypeStruct(vc.shape, F32)],
        compiler_params=_cparams(("arbitrary",)),
        name="attn_sample",
    )(sinks, q, k, v, kc, vc)


def _router_kernel(hp_ref, hs_ref, fn_ref, wh_ref, wl_ref, b_ref, triu_ref, xn_ref, info_ref, info_t_ref, cnt_ref,
                   carry_sc, *, n_p):
    h = jnp.where(pl.program_id(0) < n_p, hp_ref[...], hs_ref[...])
    _router_math(h, fn_ref, wh_ref, wl_ref, b_ref, triu_ref, xn_ref, info_ref, info_t_ref, cnt_ref, carry_sc)


def _wo_router_kernel(op_ref, os_ref, res_ref, wo_ref, fn_ref, wh_ref, wl_ref, b_ref, triu_ref,
                      h_ref, xn_ref, info_ref, info_t_ref, cnt_ref, carry_sc, *, n_p):
    o = jnp.where(pl.program_id(0) < n_p, op_ref[...], os_ref[...])
    h = res_ref[...] + jnp.dot(o.astype(BF16), wo_ref[...], preferred_element_type=F32)
    h_ref[...] = h
    _router_math(h, fn_ref, wh_ref, wl_ref, b_ref, triu_ref, xn_ref, info_ref, info_t_ref, cnt_ref, carry_sc)


def _router_math(h, fn_ref, wh_ref, wl_ref, b_ref, triu_ref, xn_ref, info_ref, info_t_ref, cnt_ref, carry_sc):
    i = pl.program_id(0)

    @pl.when(i == 0)
    def _():
        carry_sc[...] = jnp.zeros(carry_sc.shape, F32)

    xn = _rms_scale(h) * fn_ref[...]
    _store_token_major(xn_ref, xn)
    x_hi, x_lo = _split_bf16(xn, 2)
    wh, wl = wh_ref[...], wl_ref[...]
    logits = (jnp.dot(x_hi, wh, preferred_element_type=F32) + jnp.dot(x_hi, wl, preferred_element_type=F32)
              + jnp.dot(x_lo, wh, preferred_element_type=F32)) + b_ref[...]
    tm = logits.shape[0]
    lt = logits.T
    per = MOE_EXPERTS_PER_GROUP
    row = lax.broadcasted_iota(I32, (per, tm), 0).astype(F32)
    big = float(LANES)
    neg = -jnp.inf

    lg = jnp.where(row < MOE_GROUPS, lt[MOE_EXPERTS:MOE_EXPERTS + per, :], neg)
    mg = jnp.max(lg, axis=0, keepdims=True)
    gp = 1.0 / jnp.sum(jnp.exp(lg - mg), axis=0, keepdims=True)
    gi = jnp.min(jnp.where(lg == mg, row, big), axis=0, keepdims=True)

    le = lt[0:per, :]
    for g in range(1, MOE_GROUPS):
        le = jnp.where(gi == g, lt[g * per:(g + 1) * per, :], le)
    m1 = jnp.max(le, axis=0, keepdims=True)
    i1 = jnp.min(jnp.where(le == m1, row, big), axis=0, keepdims=True)
    le2 = jnp.where(row == i1, neg, le)
    m2 = jnp.max(le2, axis=0, keepdims=True)
    i2 = jnp.min(jnp.where(le2 == m2, row, big), axis=0, keepdims=True)
    e2 = jnp.exp(m2 - m1)
    g1 = gp * (1.0 / (1.0 + e2))
    g2 = gp * (e2 / (1.0 + e2))
    x1 = gi * per + i1
    x2 = gi * per + i2

    row_e = lax.broadcasted_iota(I32, (LANES, tm), 0).astype(F32)
    a1 = row_e == x1
    a2 = row_e == x2
    onehot = jnp.where(a1 | a2, 1.0, 0.0)
    before = jnp.dot(onehot.astype(BF16), triu_ref[...], preferred_element_type=F32) + carry_sc[:, 0:1]
    r1 = jnp.sum(jnp.where(a1, before, 0.0), axis=0, keepdims=True)
    r2 = jnp.sum(jnp.where(a2, before, 0.0), axis=0, keepdims=True)
    carry_sc[...] = carry_sc[...] + jnp.sum(onehot, axis=1, keepdims=True)
    cnt_ref[...] = carry_sc[...]

    info_t = jnp.zeros((SUBLANES, tm), F32)
    field = lax.broadcasted_iota(I32, (SUBLANES, tm), 0)
    for k, val in enumerate((x1, x2, g1, g2, r1, r2)):
        info_t = jnp.where(field == k, val, info_t)
    info_t_ref[...] = info_t
    info_ref[...] = jnp.concatenate([info_t, jnp.zeros((LANES - SUBLANES, tm), F32)], axis=0).T


def _two_source_specs(tm, width, n_p):
    return [pl.BlockSpec((tm, width), lambda i: (jnp.minimum(i, n_p - 1), 0)),
            pl.BlockSpec((tm, width), lambda i: (jnp.maximum(i - n_p, 0), 0))]


def _router(h_p, h_s, fn, wh, wl, b, tril):
    d = h_p.shape[1]
    tm = TOKEN_TILE
    n_p = h_p.shape[0] // tm
    t = h_p.shape[0] + h_s.shape[0]
    rows = lambda i: (i, 0)
    return pl.pallas_call(
        functools.partial(_router_kernel, n_p=n_p),
        grid=(t // tm,),
        in_specs=_two_source_specs(tm, d, n_p) + [_const_spec(a.shape) for a in (fn, wh, wl, b, tril)],
        out_specs=[pl.BlockSpec((tm * ROW_TILES, LANES), rows), pl.BlockSpec((tm, LANES), rows),
                   pl.BlockSpec((SUBLANES, tm), lambda i: (0, i)), _const_spec((LANES, LANES))],
        out_shape=[jax.ShapeDtypeStruct((t * ROW_TILES, LANES), F32), jax.ShapeDtypeStruct((t, LANES), F32),
                   jax.ShapeDtypeStruct((SUBLANES, t), F32), jax.ShapeDtypeStruct((LANES, LANES), F32)],
        scratch_shapes=[pltpu.VMEM((LANES, LANES), F32)],
        compiler_params=_cparams(("arbitrary",)),
        name="moe_router",
    )(h_p, h_s, fn, wh, wl, b, tril)


def _wo_router(o_p, o_s, res, wo, fn, wh, wl, b, tril):
    t, d = res.shape
    tm = TOKEN_TILE
    n_p = o_p.shape[0] // tm
    rows = lambda i: (i, 0)
    return pl.pallas_call(
        functools.partial(_wo_router_kernel, n_p=n_p),
        grid=(t // tm,),
        in_specs=_two_source_specs(tm, o_p.shape[1], n_p) + [pl.BlockSpec((tm, d), rows)]
        + [_const_spec(a.shape) for a in (wo, fn, wh, wl, b, tril)],
        out_specs=[pl.BlockSpec((tm, d), rows), pl.BlockSpec((tm * ROW_TILES, LANES), rows),
                   pl.BlockSpec((tm, LANES), rows), pl.BlockSpec((SUBLANES, tm), lambda i: (0, i)),
                   _const_spec((LANES, LANES))],
        out_shape=[jax.ShapeDtypeStruct((t, d), F32), jax.ShapeDtypeStruct((t * ROW_TILES, LANES), F32),
                   jax.ShapeDtypeStruct((t, LANES), F32), jax.ShapeDtypeStruct((SUBLANES, t), F32),
                   jax.ShapeDtypeStruct((LANES, LANES), F32)],
        scratch_shapes=[pltpu.VMEM((LANES, LANES), F32)],
        compiler_params=_cparams(("arbitrary",)),
        name="wo_router",
    )(o_p, o_s, res, wo, fn, wh, wl, b, tril)


def _store_token_major(ref, x):
    n = x.shape[0]
    for j in range(ROW_TILES):
        ref[pl.ds(j, n, stride=ROW_TILES), :] = x[:, j * LANES:(j + 1) * LANES]


def _load_token_major(ref, n):
    return jnp.concatenate([ref[pl.ds(j, n, stride=ROW_TILES), :] for j in range(ROW_TILES)], axis=1)


def _dest_kernel(pst_ref, info_ref, dest_ref):
    info = info_ref[...]
    e = info[0:2, :]
    start = jnp.zeros(e.shape, F32)
    for k in range(MOE_EXPERTS):
        start = jnp.where(e == k, pst_ref[k].astype(F32), start)
    dest = (start + info[4:6, :]).astype(I32)
    dest_ref[...] = jnp.concatenate([dest, jnp.zeros((SUBLANES - 2, dest.shape[1]), I32)], axis=0)


def _dest(pstarts, info_t):
    return pl.pallas_call(
        _dest_kernel,
        in_specs=[pl.BlockSpec(memory_space=pltpu.SMEM), pl.BlockSpec(memory_space=pltpu.VMEM)],
        out_specs=pl.BlockSpec(memory_space=pltpu.VMEM),
        out_shape=jax.ShapeDtypeStruct(info_t.shape, I32),
        name="moe_dest",
    )(pstarts, info_t)


def _tile_copy(src, src_row, dst, dst_row, sem):
    return pltpu.make_async_copy(src.at[pl.ds(pl.multiple_of(src_row * ROW_TILES, ROW_TILES), ROW_TILES)],
                                 dst.at[pl.ds(pl.multiple_of(dst_row * ROW_TILES, ROW_TILES), ROW_TILES)], sem)


def _wait_tiles(ref, n_tokens, sem):
    blk = ref.at[pl.ds(0, n_tokens * ROW_TILES)]
    pltpu.make_async_copy(blk, blk, sem).wait()


def _dispatch_kernel(d1_ref, d2_ref, pend_ref, pcnt_ref, xn_ref, xbuf_hbm, zero_sc, sem_z, sem):
    i = pl.program_id(0)
    tm = xn_ref.shape[0] // ROW_TILES
    blk_rows = zero_sc.shape[0]

    @pl.when(i == 0)
    def _():
        zero_sc[...] = jnp.zeros(zero_sc.shape, F32)

        def zero_copy(e):
            start = pl.multiple_of(pend_ref[e] * ROW_TILES - blk_rows, ROW_TILES)
            return pltpu.make_async_copy(zero_sc, xbuf_hbm.at[pl.ds(start, blk_rows)], sem_z)

        for e in range(MOE_EXPERTS):
            @pl.when(pcnt_ref[e] > 0)
            def _():
                zero_copy(e).start()
        first_unused = pend_ref[MOE_EXPERTS - 1] * ROW_TILES // blk_rows
        n_blocks = xbuf_hbm.shape[0] // blk_rows

        def tail_copy(b):
            dst = xbuf_hbm.at[pl.ds(pl.multiple_of(b * blk_rows, blk_rows), blk_rows)]
            return pltpu.make_async_copy(zero_sc, dst, sem_z)

        def tail_start(b, carry):
            tail_copy(b).start()
            return carry

        def tail_wait(b, carry):
            tail_copy(b).wait()
            return carry
        lax.fori_loop(first_unused, n_blocks, tail_start, 0)
        for e in range(MOE_EXPERTS):
            @pl.when(pcnt_ref[e] > 0)
            def _():
                zero_copy(e).wait()
        lax.fori_loop(first_unused, n_blocks, tail_wait, 0)

    def body(r, carry):
        t = i * tm + r
        _tile_copy(xn_ref, r, xbuf_hbm, d1_ref[t], sem).start(priority=0)
        _tile_copy(xn_ref, r, xbuf_hbm, d2_ref[t], sem).start(priority=1)
        return carry
    lax.fori_loop(0, tm, body, 0, unroll=8)
    _wait_tiles(xn_ref, tm, sem)
    _wait_tiles(xn_ref, tm, sem)


def _dispatch(dest1, dest2, pends, pcounts, xn_tm, n_rows):
    n_tok = dest1.shape[0]
    tm = DISPATCH_TILE
    grid_spec = pltpu.PrefetchScalarGridSpec(
        num_scalar_prefetch=4,
        grid=(n_tok // tm,),
        in_specs=[pl.BlockSpec((tm * ROW_TILES, LANES), lambda i, *_: (i, 0))],
        out_specs=pl.BlockSpec(memory_space=pl.ANY),
        scratch_shapes=[pltpu.VMEM((EXPERT_ROWS * ROW_TILES, LANES), F32), pltpu.SemaphoreType.DMA(()),
                        pltpu.SemaphoreType.DMA(())],
    )
    return pl.pallas_call(
        _dispatch_kernel,
        grid_spec=grid_spec,
        out_shape=jax.ShapeDtypeStruct((n_rows * ROW_TILES, LANES), F32),
        compiler_params=_cparams(("arbitrary",)),
        name="moe_dispatch",
    )(dest1, dest2, pends, pcounts, xn_tm)


def _expert_kernel(blk_e_ref, nvalid_ref, next_e_ref, x_hbm, wg_hbm, wu_hbm, wd_hbm, y_ref,
                   wg_sc, wu_sc, wd_sc, wg_st, wu_st, wd_st, x_sc, sem, xsem, *, layer):
    i = pl.program_id(0)
    nv = nvalid_ref[0]
    blk_rows = x_sc.shape[1]
    rows = blk_rows // ROW_TILES

    def weight_copies(e):
        return (pltpu.make_async_copy(wg_hbm.at[layer, e], wg_st, sem.at[0]),
                pltpu.make_async_copy(wu_hbm.at[layer, e], wu_st, sem.at[1]),
                pltpu.make_async_copy(wd_hbm.at[layer, e], wd_st, sem.at[2]))

    def row_copy(step):
        slot = step % EXPERT_RING
        src = x_hbm.at[pl.ds(pl.multiple_of(step * blk_rows, blk_rows), blk_rows)]
        return pltpu.make_async_copy(src, x_sc.at[slot], xsem.at[slot])

    @pl.when((i == 0) & (nv > 0))
    def _():
        for c in weight_copies(blk_e_ref[0]):
            c.start()
        for s in range(EXPERT_RING - 1):
            @pl.when(s < nv)
            def _():
                row_copy(s).start()

    @pl.when(i + EXPERT_RING - 1 < nv)
    def _():
        row_copy(i + EXPERT_RING - 1).start()

    @pl.when(i < nv)
    def _():
        e = blk_e_ref[i]
        e_prev = blk_e_ref[jnp.maximum(i - 1, 0)]
        row_copy(i).wait()
        x_ref = x_sc.at[i % EXPERT_RING]

        @pl.when((i == 0) | (e != e_prev))
        def _():
            for c in weight_copies(e):
                c.wait()
            wg_sc[...] = wg_st[...].astype(BF16)
            wu_sc[...] = wu_st[...].astype(BF16)
            wd_sc[...] = wd_st[...].astype(BF16)
            nxt = next_e_ref[e]

            @pl.when(nxt < MOE_EXPERTS)
            def _():
                for c in weight_copies(nxt):
                    c.start()

        x = _load_token_major(x_ref, rows).astype(BF16)
        hid = _silu(jnp.dot(x, wg_sc[...], preferred_element_type=F32)) * jnp.dot(
            x, wu_sc[...], preferred_element_type=F32)
        _store_token_major(y_ref, jnp.dot(hid.astype(BF16), wd_sc[...], preferred_element_type=F32))

    @pl.when(i >= nv)
    def _():
        y_ref[...] = jnp.zeros(y_ref.shape, F32)


def _experts(blk_e, nvalid, next_e, xbuf, wg, wu, wd, layer):
    nblk = blk_e.shape[0]
    d, hdim = wg.shape[2], wg.shape[3]
    rows = EXPERT_ROWS * ROW_TILES
    grid_spec = pltpu.PrefetchScalarGridSpec(
        num_scalar_prefetch=3,
        grid=(nblk,),
        in_specs=[pl.BlockSpec(memory_space=pl.ANY),
                  pl.BlockSpec(memory_space=pl.ANY), pl.BlockSpec(memory_space=pl.ANY),
                  pl.BlockSpec(memory_space=pl.ANY)],
        out_specs=pl.BlockSpec((rows, LANES), lambda i, be, nv, ne: (i, 0)),
        scratch_shapes=[pltpu.VMEM((d, hdim), BF16), pltpu.VMEM((d, hdim), BF16), pltpu.VMEM((hdim, d), BF16),
                        pltpu.VMEM((d, hdim), F32), pltpu.VMEM((d, hdim), F32), pltpu.VMEM((hdim, d), F32),
                        pltpu.VMEM((EXPERT_RING, rows, LANES), F32),
                        pltpu.SemaphoreType.DMA((3,)), pltpu.SemaphoreType.DMA((EXPERT_RING,))],
    )
    return pl.pallas_call(
        functools.partial(_expert_kernel, layer=layer),
        grid_spec=grid_spec,
        out_shape=jax.ShapeDtypeStruct((nblk * rows, LANES), F32),
        compiler_params=_cparams(("arbitrary",)),
        name="moe_experts",
    )(blk_e, nvalid, next_e, xbuf, wg, wu, wd)


def _gather_moe_rows(d1_ref, d2_ref, y_hbm, r_sc, sem, tm, row0=0):
    i = pl.program_id(0)
    n = pl.num_programs(0)

    def start(step):
        slot = step % 2
        base = row0 + step * tm

        def body(r, carry):
            _tile_copy(y_hbm, d1_ref[base + r], r_sc.at[slot, 0], r, sem.at[slot]).start(priority=0)
            _tile_copy(y_hbm, d2_ref[base + r], r_sc.at[slot, 1], r, sem.at[slot]).start(priority=1)
            return carry
        lax.fori_loop(0, tm, body, 0, unroll=8)

    @pl.when(i == 0)
    def _():
        start(i)

    @pl.when(i + 1 < n)
    def _():
        start(i + 1)

    slot = i % 2
    _wait_tiles(r_sc.at[slot, 0], tm, sem.at[slot])
    _wait_tiles(r_sc.at[slot, 1], tm, sem.at[slot])
    return _load_token_major(r_sc.at[slot, 0], tm), _load_token_major(r_sc.at[slot, 1], tm)


def _combine_kernel(d1_ref, d2_ref, h_ref, info_ref, y_hbm, o_ref, r_sc, sem, *, row0):
    y1, y2 = _gather_moe_rows(d1_ref, d2_ref, y_hbm, r_sc, sem, h_ref.shape[0], row0)
    info = info_ref[...]
    o_ref[...] = h_ref[...] + (y1 * info[:, 2:3] + y2 * info[:, 3:4])


def _combine(dest1, dest2, h, info, ybuf, row0, nrows):
    d = h.shape[1]
    tm = TOKEN_TILE
    base_tile = row0 // tm
    rows = lambda i, a, b: (base_tile + i, 0)
    grid_spec = pltpu.PrefetchScalarGridSpec(
        num_scalar_prefetch=2,
        grid=(nrows // tm,),
        in_specs=[pl.BlockSpec((tm, d), rows), pl.BlockSpec((tm, LANES), rows), pl.BlockSpec(memory_space=pl.ANY)],
        out_specs=pl.BlockSpec((tm, d), lambda i, a, b: (i, 0)),
        scratch_shapes=[pltpu.VMEM((2, 2, tm * ROW_TILES, LANES), F32), pltpu.SemaphoreType.DMA((2,))],
    )
    return pl.pallas_call(
        functools.partial(_combine_kernel, row0=row0),
        grid_spec=grid_spec,
        out_shape=jax.ShapeDtypeStruct((nrows, d), F32),
        compiler_params=_cparams(("arbitrary",)),
        name="moe_combine",
    )(dest1, dest2, h, info, ybuf)


def _combine_qkv_kernel(d1_ref, d2_ref, hp_ref, hs_ref, info_ref, y_hbm, kvn_ref, an_ref, wkv_ref, wq_ref, kn_ref,
                        qn_ref, rc_ref, rs1_ref, rs2_ref, hsum_ref, rq_ref, eq_ref,
                        h_ref, q_ref, k_ref, v_ref, r_sc, sem, *, n_p):
    y1, y2 = _gather_moe_rows(d1_ref, d2_ref, y_hbm, r_sc, sem, h_ref.shape[0])
    info = info_ref[...]
    h = jnp.where(pl.program_id(0) < n_p, hp_ref[...], hs_ref[...]) + (y1 * info[:, 2:3] + y2 * info[:, 3:4])
    h_ref[...] = h
    _qkv_math(h, kvn_ref, an_ref, wkv_ref, wq_ref, kn_ref, qn_ref, rc_ref, rs1_ref, rs2_ref,
              hsum_ref, rq_ref, eq_ref, q_ref, k_ref, v_ref)


def _combine_qkv(dest1, dest2, h_p, h_s, info, ybuf, kvn, an, wkv, wq, knt, qnt, rc, rs1, rs2, hsum, rq, eq):
    d = h_p.shape[1]
    tm = TOKEN_TILE
    n_p = h_p.shape[0] // tm
    t = h_p.shape[0] + h_s.shape[0]
    kvw = wkv.shape[1] // 2
    qw = wq.shape[1]
    rows = lambda i, a, b: (i, 0)
    const = lambda arr: pl.BlockSpec(arr.shape, lambda i, a, b: (0,) * arr.ndim)
    seq_tiles = (rc.shape[0] - h_s.shape[0]) // tm
    rope_rows = lambda i, a, b: (jnp.where(i < n_p, i % seq_tiles, seq_tiles + i - n_p), 0)
    grid_spec = pltpu.PrefetchScalarGridSpec(
        num_scalar_prefetch=2,
        grid=(t // tm,),
        in_specs=[pl.BlockSpec((tm, d), lambda i, a, b: (jnp.minimum(i, n_p - 1), 0)),
                  pl.BlockSpec((tm, d), lambda i, a, b: (jnp.maximum(i - n_p, 0), 0)),
                  pl.BlockSpec((tm, LANES), rows), pl.BlockSpec(memory_space=pl.ANY)]
        + [const(a) for a in (kvn, an, wkv, wq, knt, qnt)] + [pl.BlockSpec((tm, LANES), rope_rows)] * 3
        + [const(a) for a in (hsum, rq, eq)],
        out_specs=[pl.BlockSpec((tm, d), rows), pl.BlockSpec((tm, qw), rows), pl.BlockSpec((tm, kvw), rows),
                   pl.BlockSpec((tm, kvw), rows)],
        scratch_shapes=[pltpu.VMEM((2, 2, tm * ROW_TILES, LANES), F32), pltpu.SemaphoreType.DMA((2,))],
    )
    return pl.pallas_call(
        functools.partial(_combine_qkv_kernel, n_p=n_p),
        grid_spec=grid_spec,
        out_shape=[jax.ShapeDtypeStruct((t, d), F32), jax.ShapeDtypeStruct((t, qw), F32),
                   jax.ShapeDtypeStruct((t, kvw), F32), jax.ShapeDtypeStruct((t, kvw), F32)],
        compiler_params=_cparams(("arbitrary",)),
        name="combine_qkv",
    )(dest1, dest2, h_p, h_s, info, ybuf, kvn, an, wkv, wq, knt, qnt, rc, rs1, rs2, hsum, rq, eq)


def _router_weights(w_grp, b_grp, w_rt, b_rt):
    d = w_rt.shape[0]
    w_cat = jnp.zeros((d, LANES), F32).at[:, :MOE_EXPERTS].set(w_rt).at[:, MOE_EXPERTS:MOE_EXPERTS + MOE_GROUPS].set(w_grp)
    b_cat = jnp.zeros((1, LANES), F32).at[0, :MOE_EXPERTS].set(b_rt).at[0, MOE_EXPERTS:MOE_EXPERTS + MOE_GROUPS].set(b_grp)
    w_hi = w_cat.astype(BF16)
    w_lo = (w_cat - w_hi.astype(F32)).astype(BF16)
    return w_hi, w_lo, b_cat


def _moe_experts(xn_tm, info_t, cnt, wg, wu, wd, layer):
    t = info_t.shape[1]
    counts = cnt[:MOE_EXPERTS, 0].astype(I32)
    pcounts = (counts + EXPERT_ROWS - 1) // EXPERT_ROWS * EXPERT_ROWS
    pends = jnp.cumsum(pcounts)
    pstarts = pends - pcounts
    nblk = -(-(2 * t + MOE_EXPERTS * (EXPERT_ROWS - 1)) // EXPERT_ROWS)
    blk_start = jnp.arange(nblk, dtype=I32) * EXPERT_ROWS
    blk_e = jnp.minimum(jnp.sum((pends[None, :] <= blk_start[:, None]).astype(I32), axis=1), MOE_EXPERTS - 1)
    nvalid = pends[-1:] // EXPERT_ROWS
    eid = jnp.arange(MOE_EXPERTS, dtype=I32)
    later = (eid[None, :] > eid[:, None]) & (pcounts[None, :] > 0)
    next_e = jnp.min(jnp.where(later, eid[None, :], MOE_EXPERTS), axis=1).astype(I32)

    dest = _dest(pstarts, info_t)
    dest1, dest2 = dest[0], dest[1]
    xbuf = _dispatch(dest1, dest2, pends, pcounts, xn_tm, nblk * EXPERT_ROWS)
    return dest1, dest2, _experts(blk_e, nvalid, next_e, xbuf, wg, wu, wd, layer)


def _rope_tables(pos):
    half = ROT_DIM // 2
    inv = ROPE_THETA ** (-np.arange(0, ROT_DIM, 2, dtype=np.float64) / ROT_DIM)
    ang = pos.astype(np.float64)[:, None] * inv[None, :]
    cos, sin = np.cos(ang), np.sin(ang)
    n = pos.shape[0]
    ones = np.ones((n, ATT_HEAD_DIM - ROT_DIM))
    zeros_r = np.zeros((n, ATT_HEAD_DIM - ROT_DIM))
    zeros_h = np.zeros((n, half))
    c = np.concatenate([cos, cos, ones], axis=1)
    s1 = np.concatenate([-sin, zeros_h, zeros_r], axis=1)
    s2 = np.concatenate([zeros_h, sin, zeros_r], axis=1)
    reps = LANES // ATT_HEAD_DIM
    return tuple(jnp.asarray(np.tile(a, (1, reps)).astype(np.float32)) for a in (c, s1, s2))


def _mask01(m):
    return jnp.asarray(m.astype(np.float32), dtype=BF16)


def _to_step_order(a, nsteps, n_tok):
    c = a.shape[-1]
    return a.reshape(nsteps, SEQ_PER_STEP, n_tok, c).transpose(0, 2, 1, 3).reshape(nsteps * n_tok * SEQ_PER_STEP, c)


def _from_step_order(a, nsteps, n_tok):
    c = a.shape[-1]
    return a.reshape(nsteps, n_tok, SEQ_PER_STEP, c).transpose(0, 2, 1, 3).reshape(nsteps * SEQ_PER_STEP, n_tok, c)


def kernel(x_prompt, x_sample, state_ssm, state_conv, cache_k_win, cache_v_win, ssm_norm, ssm_w_in, ssm_conv_w, ssm_conv_b, ssm_dt_bias, ssm_a_log, ssm_d, ssm_gate_norm, ssm_w_out, kv_norm, w_kv, k_norm, attn_norm, w_q, q_norm, sinks, w_o, ffn_norm, moe_w_group, moe_b_group, moe_w_router, moe_b_router, moe_w_gate, moe_w_up, moe_w_down):
    bp, seq, d = x_prompt.shape
    bs, n_tok, _ = x_sample.shape
    tp, ts = bp * seq, bs * n_tok
    nsteps = bs // SEQ_PER_STEP
    n_heads = ssm_d.shape[1]
    di = n_heads * SSM_HEAD_DIM
    gn_w = SSM_GROUPS * SSM_STATE
    cdim = di + 2 * gn_w
    n_q = sinks.shape[1]
    kvw = ATT_KV_HEADS * ATT_HEAD_DIM

    xp2 = x_prompt.reshape(tp, d)
    xs2 = _to_step_order(x_sample, nsteps, n_tok)

    lane_i = np.arange(LANES)
    e01 = _mask01(lane_i[:, None] == (np.arange(di) // SSM_HEAD_DIM)[None, :])
    hpg = di // SSM_GROUPS
    g1 = ((np.arange(gn_w) // SSM_STATE)[:, None] == (lane_i // (hpg // SSM_HEAD_DIM))[None, :])
    g1 = _mask01(g1 & (lane_i < n_heads)[None, :])
    tril_c = _mask01(np.arange(SSM_CHUNK)[:, None] >= np.arange(SSM_CHUNK)[None, :])
    earlier_x = _mask01(np.arange(TOKEN_TILE)[:, None] < np.arange(TOKEN_TILE)[None, :])
    hsum = _mask01((np.arange(kvw) // ATT_HEAD_DIM)[:, None] == (np.arange(kvw) // ATT_HEAD_DIM)[None, :])
    qw = n_q * ATT_HEAD_DIM
    rq_np = (np.arange(qw) // ATT_HEAD_DIM)[:, None] == lane_i[None, :]
    rq, eq = _mask01(rq_np), _mask01(rq_np.T)

    w_in = ssm_w_in[0]
    wz = w_in[:, :di].astype(BF16)
    wx = w_in[:, di:di + cdim].astype(BF16)
    wd = jnp.zeros((d, LANES), F32).at[:, :n_heads].set(w_in[:, di + cdim:]).astype(BF16)
    cw, cb = ssm_conv_w[0], ssm_conv_b[0].reshape(1, cdim)
    z, xbc, dt = _inproj(xp2, xs2, ssm_norm[0].reshape(1, d), wz, wx, wd)

    pad_h = lambda v: jnp.zeros((1, LANES), F32).at[0, :n_heads].set(v)
    dtb, alog = pad_h(ssm_dt_bias[0]), pad_h(ssm_a_log[0])
    dsk = jnp.repeat(ssm_d[0], SSM_HEAD_DIM).reshape(1, di)
    gnw = ssm_gate_norm[0].reshape(1, di)

    w_out = ssm_w_out[0].astype(BF16)
    h_p, s_fin, c_fin = _ssd_prompt(z, xbc, dt, xp2, cw, cb, dtb, alog, dsk, gnw, tril_c, e01, w_out, bp, seq)
    ssm_p = s_fin.reshape(1, bp, n_heads, SSM_HEAD_DIM, SSM_STATE)
    conv_p = c_fin[:, SUBLANES - (SSM_CONV - 1):, :].reshape(1, bp, SSM_CONV - 1, cdim)

    xbc_s = xbc[tp:].reshape(nsteps, n_tok, SEQ_PER_STEP, cdim)
    conv_in = state_conv[0].reshape(nsteps, SEQ_PER_STEP, SSM_CONV - 1, cdim).transpose(0, 2, 1, 3)
    xp7 = jnp.concatenate([conv_in, xbc_s], axis=1)
    s0 = state_ssm[0].reshape(bs, di, SSM_STATE)
    h_s, s_new = _ssd_sample(z, xp7, dt, s0, xs2, cw, cb, dtb, alog, dsk, gnw, e01, g1, w_out, tp, n_tok)
    ssm_s = s_new.reshape(1, bs, n_heads, SSM_HEAD_DIM, SSM_STATE)
    conv_s = _from_step_order(xbc[tp:], nsteps, n_tok)[:, n_tok - (SSM_CONV - 1):, :].reshape(
        1, bs, SSM_CONV - 1, cdim)

    w_hi, w_lo, b_cat = _router_weights(moe_w_group[0], moe_b_group[0], moe_w_router[0], moe_b_router[0])
    xn_tm, info, info_t, cnt = _router(h_p, h_s, ffn_norm[0].reshape(1, d), w_hi, w_lo, b_cat, earlier_x)
    dest1, dest2, ybuf = _moe_experts(xn_tm, info_t, cnt, moe_w_gate, moe_w_up, moe_w_down, 0)

    pos = np.concatenate([np.arange(seq), np.tile(np.repeat(PAST_LEN + np.arange(n_tok), SEQ_PER_STEP), nsteps)])
    rc, rs1, rs2 = _rope_tables(pos)
    h1, q, k, v = _combine_qkv(dest1, dest2, h_p, h_s, info, ybuf, kv_norm.reshape(1, d), attn_norm[0].reshape(1, d),
                               w_kv.astype(BF16), w_q[0].astype(BF16),
                               jnp.tile(k_norm, ATT_KV_HEADS).reshape(1, kvw), jnp.tile(q_norm[0], n_q).reshape(1, qw),
                               rc, rs1, rs2, hsum, rq, eq)
    sk = sinks[0]
    lc = cache_k_win.shape[1]
    kc = cache_k_win.reshape(bs, lc, kvw)
    vc = cache_v_win.reshape(bs, lc, kvw)
    o_p = _attn_prompt(sk, q, k, v, bp, seq)
    o_s, k_win, v_win = _attn_sample(sk, q, k, v, kc, vc, tp, n_tok)
    w_hi, w_lo, b_cat = _router_weights(moe_w_group[1], moe_b_group[1], moe_w_router[1], moe_b_router[1])
    h2, xn_tm, info, info_t, cnt = _wo_router(o_p, o_s, h1, w_o[0].astype(BF16), ffn_norm[1].reshape(1, d),
                                             w_hi, w_lo, b_cat, earlier_x)
    dest1, dest2, ybuf = _moe_experts(xn_tm, info_t, cnt, moe_w_gate, moe_w_up, moe_w_down, 1)
    y_p = _combine(dest1, dest2, h2, info, ybuf, 0, tp)
    y_s = _combine(dest1, dest2, h2, info, ybuf, tp, ts)

    wl = min(WINDOW, seq)
    k_p = k[:tp].reshape(bp, seq, kvw)[:, seq - wl:].reshape(bp, wl, ATT_KV_HEADS, ATT_HEAD_DIM)
    v_p = v[:tp].reshape(bp, seq, kvw)[:, seq - wl:].reshape(bp, wl, ATT_KV_HEADS, ATT_HEAD_DIM)
    k_s = k_win.reshape(bs, lc, ATT_KV_HEADS, ATT_HEAD_DIM)
    v_s = v_win.reshape(bs, lc, ATT_KV_HEADS, ATT_HEAD_DIM)
    return (y_p.reshape(bp, seq, d), _from_step_order(y_s, nsteps, n_tok),
            ssm_p, conv_p, k_p, v_p, ssm_s, conv_s, k_s, v_s)
```

```python
import functools

import numpy as np
import jax
import jax.numpy as jnp
from jax import lax
from jax.experimental import pallas as pl
from jax.experimental.pallas import tpu as pltpu

F32 = jnp.float32
BF16 = jnp.bfloat16
I32 = jnp.int32

EPS = 1e-6
SSM_HEAD_DIM = 64
SSM_GROUPS = 4
SSM_STATE = 128
SSM_CONV = 4
SSM_CHUNK = 128
ATT_HEAD_DIM = 64
ATT_KV_HEADS = 4
WINDOW = 128
ROT_DIM = ATT_HEAD_DIM // 4
ROPE_THETA = 500000.0
MOE_GROUPS = 4
MOE_EXPERTS_PER_GROUP = 8
MOE_EXPERTS = MOE_GROUPS * MOE_EXPERTS_PER_GROUP
MOE_BLOCK = 128
PAST_LEN = 16384

LANES = 128
SUBLANES = 8
BF16_ROWS = 16
SEQ_PER_STEP = SUBLANES
TOKEN_TILE = 256
INPROJ_TILE = 512
ROW_TILES = 8
DISPATCH_TILE = 512
EXPERT_RING = 3
EXPERT_ROWS = 512
VMEM_LIMIT = 56 * 1024 * 1024


def _cparams(sem):
    return pltpu.CompilerParams(dimension_semantics=sem, vmem_limit_bytes=VMEM_LIMIT)


def _const_spec(shape):
    nd = len(shape)
    return pl.BlockSpec(shape, lambda *_: (0,) * nd)


def _split_bf16(v, n):
    parts = []
    r = v
    for k in range(n):
        p = r.astype(BF16)
        parts.append(p)
        if k + 1 < n:
            r = r - p.astype(F32)
    return parts


def _mm01(v, m01, n=3):
    acc = None
    for p in _split_bf16(v, n):
        d = jnp.dot(p, m01, preferred_element_type=F32)
        acc = d if acc is None else acc + d
    return acc


def _mm01_left(m01, v, n=3):
    acc = None
    for p in _split_bf16(v, n):
        d = jnp.dot(m01, p, preferred_element_type=F32)
        acc = d if acc is None else acc + d
    return acc


def _dot_nt(a, b):
    return lax.dot_general(a, b, (((1,), (1,)), ((), ())), preferred_element_type=F32)


def _sigmoid(x):
    return 0.5 * jnp.tanh(0.5 * x) + 0.5


def _silu(x):
    return x * _sigmoid(x)


def _softplus(x):
    return jnp.maximum(x, 0.0) + jnp.log1p(jnp.exp(-jnp.abs(x)))


def _rms_scale(x):
    return x * lax.rsqrt(jnp.mean(x * x, axis=-1, keepdims=True) + EPS)


def _gate_norm(y, z, gn, n_groups):
    yz = y * _silu(z)
    w = yz.shape[-1] // n_groups
    outs = []
    for g in range(n_groups):
        v = yz[:, g * w:(g + 1) * w]
        outs.append(_rms_scale(v) * gn[:, g * w:(g + 1) * w])
    return jnp.concatenate(outs, axis=1)


def _rope(x, c, s1, s2):
    w = x.shape[-1]
    return x * c + pltpu.roll(x, w - ROT_DIM // 2, 1) * s1 + pltpu.roll(x, ROT_DIM // 2, 1) * s2


def _tile_lanes(t, reps):
    return t if reps == 1 else jnp.concatenate([t] * reps, axis=1)


def _inproj_kernel(xp_ref, xs_ref, g_ref, wz_ref, wx_ref, wd_ref, z_ref, xbc_ref, dt_ref, *, n_p):
    i = pl.program_id(0)
    x = jnp.where(i < n_p, xp_ref[...], xs_ref[...])
    xn = (_rms_scale(x) * g_ref[...]).astype(BF16)
    z_ref[...] = jnp.dot(xn, wz_ref[...], preferred_element_type=F32)
    xbc_ref[...] = jnp.dot(xn, wx_ref[...], preferred_element_type=F32)
    dt_ref[...] = jnp.dot(xn, wd_ref[...], preferred_element_type=F32)


def _inproj(xp2, xs2, g, wz, wx, wd):
    tp, d = xp2.shape
    ts = xs2.shape[0]
    tm = INPROJ_TILE
    n_p, n_s = tp // tm, ts // tm
    t = tp + ts
    return pl.pallas_call(
        functools.partial(_inproj_kernel, n_p=n_p),
        grid=(n_p + n_s,),
        in_specs=_two_source_specs(tm, d, n_p) + [_const_spec(a.shape) for a in (g, wz, wx, wd)],
        out_specs=[
            pl.BlockSpec((tm, wz.shape[1]), lambda i: (i, 0)),
            pl.BlockSpec((tm, wx.shape[1]), lambda i: (i, 0)),
            pl.BlockSpec((tm, wd.shape[1]), lambda i: (i, 0)),
        ],
        out_shape=[jax.ShapeDtypeStruct((t, wz.shape[1]), F32),
                   jax.ShapeDtypeStruct((t, wx.shape[1]), F32),
                   jax.ShapeDtypeStruct((t, wd.shape[1]), F32)],
        compiler_params=_cparams(("arbitrary",)),
        name="inproj",
    )(xp2, xs2, g, wz, wx, wd)


def _ssd_prompt_kernel(z_ref, xbc_ref, dt_ref, x_ref, cw_ref, cb_ref, dtb_ref, alog_ref, dsk_ref, gn_ref,
                       tril_ref, e_ref, wout_ref, h_ref, sfin_ref, cfin_ref, xpad_sc, st_sc):
    c = pl.program_id(1)
    q = SSM_CHUNK
    cd = xbc_ref.shape[1]
    di = z_ref.shape[1]
    gn_w = SSM_GROUPS * SSM_STATE
    hpg = di // SSM_GROUPS
    pad = SUBLANES

    n_slab = cd // LANES

    @pl.when(c == 0)
    def _():
        xpad_sc[:, 0:pad, :] = jnp.zeros((n_slab, pad, LANES), F32)
        st_sc[...] = jnp.zeros(st_sc.shape, F32)

    @pl.when(c > 0)
    def _():
        xpad_sc[:, 0:pad, :] = xpad_sc[:, q:q + pad, :]

    slabs = []
    for j in range(n_slab):
        ls = slice(j * LANES, (j + 1) * LANES)
        xpad_sc[j, pad:pad + q, :] = xbc_ref[:, ls]
        acc = cb_ref[:, ls]
        for k in range(SSM_CONV):
            off = pad - (SSM_CONV - 1) + k
            acc = acc + xpad_sc[j, off:off + q, :] * cw_ref[k:k + 1, ls]
        slabs.append(_silu(acc))
    xc = jnp.concatenate(slabs, axis=1)
    xs = xc[:, :di]
    bm = xc[:, di:di + gn_w]
    cm = xc[:, di + gn_w:]

    dt = _softplus(dt_ref[...] + dtb_ref[...])
    a = -jnp.exp(alog_ref[...])
    act = _mm01_left(tril_ref[...], dt * a)
    act_t = act.T
    act_last = act[q - 1:q, :]
    pieces = (_split_bf16(dt, 2) + _split_bf16(jnp.exp(act_last - act), 2) + _split_bf16(jnp.exp(act), 2)
              + _split_bf16(jnp.exp(act[q - BF16_ROWS:q, :]), 3))
    ex = jnp.dot(jnp.concatenate(pieces, axis=0), e_ref[...], preferred_element_type=F32)
    xdt = xs * (ex[0:q] + ex[q:2 * q])
    xd = xdt * (ex[2 * q:3 * q] + ex[3 * q:4 * q])
    eax = ex[4 * q:5 * q] + ex[5 * q:6 * q]
    tail = ex[6 * q:]
    cd = (tail[0:BF16_ROWS] + tail[BF16_ROWS:2 * BF16_ROWS]) + tail[2 * BF16_ROWS:]
    cdx = cd[BF16_ROWS - 1:BF16_ROWS, :]

    row = lax.broadcasted_iota(I32, (q, q), 0)
    col = lax.broadcasted_iota(I32, (q, q), 1)
    causal = row >= col
    lane = lax.broadcasted_iota(I32, (q, LANES), 1)
    lo_half = lane < SSM_HEAD_DIM

    y_parts = []
    heads_per_group = hpg // SSM_HEAD_DIM
    for g in range(SSM_GROUPS):
        cg = cm[:, g * SSM_STATE:(g + 1) * SSM_STATE].astype(BF16)
        bg = bm[:, g * SSM_STATE:(g + 1) * SSM_STATE]
        cb = _dot_nt(cg, bg.astype(BF16))
        st_g = st_sc[:, g * hpg:(g + 1) * hpg]
        y_off = jnp.dot(cg, st_g.astype(BF16), preferred_element_type=F32)
        for pr in range(heads_per_group // 2):
            h0 = g * heads_per_group + 2 * pr
            ms = []
            for h in (h0, h0 + 1):
                seg = act[:, h:h + 1] - act_t[h:h + 1, :]
                lm = jnp.exp(jnp.where(causal, seg, -jnp.inf))
                ms.append((cb * lm).astype(BF16))
            m2 = jnp.concatenate(ms, axis=1)
            xpair = xdt[:, h0 * SSM_HEAD_DIM:(h0 + 2) * SSM_HEAD_DIM]
            rhs = jnp.concatenate([jnp.where(lo_half, xpair, 0.0),
                                   jnp.where(lo_half, 0.0, xpair)], axis=0).astype(BF16)
            y_d = jnp.dot(m2, rhs, preferred_element_type=F32)
            lo = 2 * pr * SSM_HEAD_DIM
            y_parts.append(y_d + y_off[:, lo:lo + LANES] * eax[:, g * hpg + lo:g * hpg + lo + LANES])
        upd = jnp.dot(bg.T.astype(BF16), xd[:, g * hpg:(g + 1) * hpg].astype(BF16),
                      preferred_element_type=F32)
        st_sc[:, g * hpg:(g + 1) * hpg] = st_g * cdx[:, g * hpg:(g + 1) * hpg] + upd

    y = jnp.concatenate(y_parts, axis=1) + xs * dsk_ref[...]
    yg = _gate_norm(y, z_ref[...], gn_ref[...], SSM_GROUPS)
    h_ref[...] = x_ref[...] + jnp.dot(yg.astype(BF16), wout_ref[...], preferred_element_type=F32)

    @pl.when(c == pl.num_programs(1) - 1)
    def _():
        sfin_ref[0] = st_sc[...].T
        cfin_ref[0] = jnp.concatenate([xpad_sc[j, q:q + pad, :] for j in range(n_slab)], axis=1)


def _ssd_prompt(z, xbc, dt, x, cw, cb, dtb, alog, dsk, gnw, tril, e01, wout, bp, seq):
    nc = seq // SSM_CHUNK
    q = SSM_CHUNK
    di, cd, d = z.shape[1], xbc.shape[1], x.shape[1]
    rows = lambda b, c: (b * nc + c, 0)
    return pl.pallas_call(
        _ssd_prompt_kernel,
        grid=(bp, nc),
        in_specs=[
            pl.BlockSpec((q, di), rows), pl.BlockSpec((q, cd), rows), pl.BlockSpec((q, LANES), rows),
            pl.BlockSpec((q, d), rows),
            _const_spec(cw.shape), _const_spec(cb.shape), _const_spec(dtb.shape), _const_spec(alog.shape),
            _const_spec(dsk.shape), _const_spec(gnw.shape), _const_spec(tril.shape), _const_spec(e01.shape),
            _const_spec(wout.shape),
        ],
        out_specs=[
            pl.BlockSpec((q, d), rows),
            pl.BlockSpec((1, di, SSM_STATE), lambda b, c: (b, 0, 0)),
            pl.BlockSpec((1, SUBLANES, cd), lambda b, c: (b, 0, 0)),
        ],
        out_shape=[jax.ShapeDtypeStruct((bp * seq, d), F32),
                   jax.ShapeDtypeStruct((bp, di, SSM_STATE), F32),
                   jax.ShapeDtypeStruct((bp, SUBLANES, cd), F32)],
        scratch_shapes=[pltpu.VMEM((cd // LANES, q + 2 * SUBLANES, LANES), F32), pltpu.VMEM((SSM_STATE, di), F32)],
        compiler_params=_cparams(("arbitrary", "arbitrary")),
        name="ssd_prompt",
    )(z, xbc, dt, x, cw, cb, dtb, alog, dsk, gnw, tril, e01, wout)


def _ssd_sample_kernel(z_ref, xp_ref, dt_ref, s0_ref, x_ref, cw_ref, cb_ref, dtb_ref, alog_ref, dsk_ref, gn_ref,
                       e_ref, g1_ref, wout_ref, h_ref, sn_ref, yoff_sc, *, n_tok):
    hf = pl.program_id(1)
    nb = SEQ_PER_STEP
    half = nb // 2
    q = n_tok * nb
    di = z_ref.shape[1]
    gn_w = SSM_GROUPS * SSM_STATE
    hpg = di // SSM_GROUPS

    taps = [xp_ref[0, m] for m in range(n_tok + SSM_CONV - 1)]
    slabs = []
    for t in range(n_tok):
        acc = cb_ref[...]
        for k in range(SSM_CONV):
            acc = acc + taps[t + k] * cw_ref[k:k + 1, :]
        slabs.append(_silu(acc))
    xc = jnp.concatenate(slabs, axis=0)
    xs = xc[:, :di]
    bm = xc[:, di:di + gn_w]
    cm = xc[:, di + gn_w:]

    dt = _softplus(dt_ref[...] + dtb_ref[...])
    da = dt * (-jnp.exp(alog_ref[...]))
    acts = []
    run = None
    for t in range(n_tok):
        d = da[t * nb:(t + 1) * nb, :]
        run = d if run is None else run + d
        acts.append(run)
    act = jnp.concatenate(acts, axis=0)
    act_last = jnp.concatenate([acts[-1]] * n_tok, axis=0)
    e01 = e_ref[...]
    xdt = xs * _mm01(dt, e01)
    xd = xdt * _mm01(jnp.exp(act_last - act), e01)
    eax = _mm01(jnp.exp(act), e01)
    cdx = _mm01(jnp.exp(acts[-1]), e01)

    pairs = [(t, u) for t in range(n_tok) for u in range(t + 1)]
    cbp = jnp.concatenate([cm[t * nb:(t + 1) * nb, :] * bm[u * nb:(u + 1) * nb, :] for t, u in pairs], axis=0)
    seg = jnp.concatenate([acts[t] - acts[u] for t, u in pairs], axis=0)
    coef = _mm01(_mm01(cbp, g1_ref[...]) * jnp.exp(seg), e01)
    y_slabs = []
    for t in range(n_tok):
        acc = None
        for pi, (tt, u) in enumerate(pairs):
            if tt != t:
                continue
            term = coef[pi * nb:(pi + 1) * nb, :] * xdt[u * nb:(u + 1) * nb, :]
            acc = term if acc is None else acc + term
        y_slabs.append(acc)
    y_diag = jnp.concatenate(y_slabs, axis=0)

    zpad = jnp.concatenate([xd,
                            jnp.where(hf == 0, cdx[0:half, :], cdx[half:nb, :]),
                            jnp.zeros((LANES - q - half, di), F32)], axis=0)
    zt = zpad.T
    row_seq = lax.broadcasted_iota(I32, (q, 1), 0) % nb
    cm_b = cm.astype(BF16)
    y_off_g = [None] * SSM_GROUPS
    for sl in range(half):
        in_seq = row_seq == hf * half + sl
        for g in range(SSM_GROUPS):
            s_old = s0_ref[sl, g * hpg:(g + 1) * hpg, :]
            c_g = jnp.where(in_seq, cm_b[:, g * SSM_STATE:(g + 1) * SSM_STATE], jnp.zeros((), BF16))
            yo = _dot_nt(c_g, s_old.astype(BF16))
            y_off_g[g] = yo if y_off_g[g] is None else y_off_g[g] + yo
            b_g = jnp.where(in_seq, bm[:, g * SSM_STATE:(g + 1) * SSM_STATE], 0.0)
            b_pad = jnp.concatenate([b_g, jnp.zeros((LANES - q, SSM_STATE), F32)], axis=0).astype(BF16)
            zt_g = zt[g * hpg:(g + 1) * hpg, :]
            upd = jnp.dot(zt_g.astype(BF16), b_pad, preferred_element_type=F32)
            decay = zt_g[:, q + sl:q + sl + 1]
            sn_ref[sl, g * hpg:(g + 1) * hpg, :] = s_old * decay + upd
    y_off = jnp.concatenate(y_off_g, axis=1)

    @pl.when(hf == 0)
    def _():
        yoff_sc[...] = y_off

    @pl.when(hf == 1)
    def _():
        y = y_diag + (yoff_sc[...] + y_off) * eax + xs * dsk_ref[...]
        yg = _gate_norm(y, z_ref[...], gn_ref[...], SSM_GROUPS)
        h_ref[...] = x_ref[...] + jnp.dot(yg.astype(BF16), wout_ref[...], preferred_element_type=F32)


def _ssd_sample(z, xp7, dt, s0, x, cw, cb, dtb, alog, dsk, gnw, e01, g1, wout, tp, n_tok):
    nsteps = xp7.shape[0]
    nb = SEQ_PER_STEP
    half = nb // 2
    q = n_tok * nb
    di, d = z.shape[1], x.shape[1]
    cd = xp7.shape[3]
    base = tp // q
    rows = lambda s, hf: (base + s, 0)
    return pl.pallas_call(
        functools.partial(_ssd_sample_kernel, n_tok=n_tok),
        grid=(nsteps, 2),
        in_specs=[
            pl.BlockSpec((q, di), rows),
            pl.BlockSpec((1, n_tok + SSM_CONV - 1, nb, cd), lambda s, hf: (s, 0, 0, 0)),
            pl.BlockSpec((q, LANES), rows),
            pl.BlockSpec((half, di, SSM_STATE), lambda s, hf: (2 * s + hf, 0, 0)),
            pl.BlockSpec((q, d), lambda s, hf: (s, 0)),
            _const_spec(cw.shape), _const_spec(cb.shape), _const_spec(dtb.shape), _const_spec(alog.shape),
            _const_spec(dsk.shape), _const_spec(gnw.shape), _const_spec(e01.shape), _const_spec(g1.shape),
            _const_spec(wout.shape),
        ],
        out_specs=[
            pl.BlockSpec((q, d), lambda s, hf: (s, 0)),
            pl.BlockSpec((half, di, SSM_STATE), lambda s, hf: (2 * s + hf, 0, 0)),
        ],
        out_shape=[jax.ShapeDtypeStruct((nsteps * q, d), F32),
                   jax.ShapeDtypeStruct(s0.shape, F32)],
        scratch_shapes=[pltpu.VMEM((q, di), F32)],
        compiler_params=_cparams(("arbitrary", "arbitrary")),
        name="ssd_sample",
    )(z, xp7, dt, s0, x, cw, cb, dtb, alog, dsk, gnw, e01, g1, wout)


def _qkv_math(h, kvn_ref, an_ref, wkv_ref, wq_ref, kn_ref, qn_ref, rc_ref, rs1_ref, rs2_ref,
              hsum_ref, rq_ref, eq_ref, q_ref, k_ref, v_ref):
    hn = _rms_scale(h)
    kvw = k_ref.shape[1]
    kv = jnp.dot((hn * kvn_ref[...]).astype(BF16), wkv_ref[...], preferred_element_type=F32)
    k = kv[:, :kvw]
    v_ref[...] = kv[:, kvw:]
    inv_hd = 1.0 / ATT_HEAD_DIM
    k = k * lax.rsqrt(_mm01(k * k, hsum_ref[...], 2) * inv_hd + EPS) * kn_ref[...]
    rc, rs1, rs2 = rc_ref[...], rs1_ref[...], rs2_ref[...]
    rk = kvw // LANES
    k_ref[...] = _rope(k, _tile_lanes(rc, rk), _tile_lanes(rs1, rk), _tile_lanes(rs2, rk))
    q = jnp.dot((hn * an_ref[...]).astype(BF16), wq_ref[...], preferred_element_type=F32)
    rsq = lax.rsqrt(_mm01(q * q, rq_ref[...], 2) * inv_hd + EPS)
    q = q * _mm01(rsq, eq_ref[...], 2) * qn_ref[...]
    rq = q.shape[1] // LANES
    q_ref[...] = _rope(q, _tile_lanes(rc, rq), _tile_lanes(rs1, rq), _tile_lanes(rs2, rq))


def _attn_prompt_kernel(sink_ref, q_ref, kc_ref, kp_ref, vc_ref, vp_ref, o_ref):
    i = pl.program_id(1)
    w = WINDOW
    hd = ATT_HEAD_DIM
    n_q = q_ref.shape[1] // hd
    grp = n_q // ATT_KV_HEADS
    pair = LANES // hd
    rows = pair * w
    row = lax.broadcasted_iota(I32, (rows, 2 * w), 0) % w
    col = lax.broadcasted_iota(I32, (rows, 2 * w), 1)
    dist = row + w - col
    mask = (dist >= 0) & (dist < w) & ((col >= w) | (i > 0))
    first_head = lax.broadcasted_iota(I32, (rows, 1), 0) < w
    lo_lanes = lax.broadcasted_iota(I32, (w, LANES), 1) < hd
    zeros_kv = jnp.zeros((2 * w, hd), BF16)
    outs = []
    for g in range(ATT_KV_HEADS):
        sl = slice(g * hd, (g + 1) * hd)
        kk = jnp.concatenate([kp_ref[:, sl], kc_ref[:, sl]], axis=0).astype(BF16)
        vv = jnp.concatenate([vp_ref[:, sl], vc_ref[:, sl]], axis=0).astype(BF16)
        k2 = jnp.concatenate([kk, kk], axis=1)
        v_lo = jnp.concatenate([vv, zeros_kv], axis=1)
        v_hi = jnp.concatenate([zeros_kv, vv], axis=1)
        for h0 in range(g * grp, (g + 1) * grp, pair):
            qp = q_ref[:, h0 * hd:(h0 + pair) * hd] * (hd ** -0.5)
            qs = jnp.concatenate([jnp.where(lo_lanes, qp, 0.0), jnp.where(lo_lanes, 0.0, qp)],
                                 axis=0).astype(BF16)
            s = jnp.where(mask, _dot_nt(qs, k2), -jnp.inf)
            sink = jnp.where(first_head, sink_ref[h0], sink_ref[h0 + 1])
            m = jnp.maximum(jnp.max(s, axis=-1, keepdims=True), sink)
            p = jnp.exp(s - m)
            inv = 1.0 / (jnp.sum(p, axis=-1, keepdims=True) + jnp.exp(sink - m))
            pb = p.astype(BF16)
            outs.append(jnp.dot(pb[0:w], v_lo, preferred_element_type=F32) * inv[0:w]
                        + jnp.dot(pb[w:2 * w], v_hi, preferred_element_type=F32) * inv[w:2 * w])
    o_ref[...] = jnp.concatenate(outs, axis=1)


def _attn_prompt(sinks, q, k, v, bp, seq):
    w = WINDOW
    nb = seq // w
    qw, kvw = q.shape[1], k.shape[1]
    cur = lambda b, i: (b * nb + i, 0)
    prev = lambda b, i: (b * nb + jnp.maximum(i - 1, 0), 0)
    return pl.pallas_call(
        _attn_prompt_kernel,
        grid=(bp, nb),
        in_specs=[pl.BlockSpec(memory_space=pltpu.SMEM),
                  pl.BlockSpec((w, qw), cur), pl.BlockSpec((w, kvw), cur), pl.BlockSpec((w, kvw), prev),
                  pl.BlockSpec((w, kvw), cur), pl.BlockSpec((w, kvw), prev)],
        out_specs=pl.BlockSpec((w, qw), cur),
        out_shape=jax.ShapeDtypeStruct((bp * seq, qw), F32),
        compiler_params=_cparams(("arbitrary", "arbitrary")),
        name="attn_prompt",
    )(sinks, q, k, k, v, v)


def _attn_sample_kernel(sink_ref, q_ref, kn_ref, vn_ref, kc_ref, vc_ref, o_ref, kw_ref, vw_ref, *, n_tok):
    nb = SEQ_PER_STEP
    qn = n_tok * nb
    hd = ATT_HEAD_DIM
    lc = kc_ref.shape[1]
    for new_ref, old_ref, win_ref in ((kn_ref, kc_ref, kw_ref), (vn_ref, vc_ref, vw_ref)):
        new = new_ref[...]
        for sq in range(nb):
            win_ref[sq, 0:lc - n_tok, :] = old_ref[sq, n_tok:lc, :]
            win_ref[sq, lc - n_tok:lc, :] = jnp.concatenate(
                [new[t * nb + sq:t * nb + sq + 1, :] for t in range(n_tok)], axis=0)
    n_q = q_ref.shape[1] // hd
    grp = n_q // ATT_KV_HEADS
    rows = grp * qn
    r = lax.broadcasted_iota(I32, (rows, 1), 0)
    r_seq = r % nb
    r_tok = (r % qn) // nb
    ccol = lax.broadcasted_iota(I32, (rows, lc), 1)
    mask_c = ccol >= r_tok + 1 + (lc - WINDOW)
    ncol = lax.broadcasted_iota(I32, (rows, LANES), 1)
    mask_n = (ncol < qn) & (ncol % nb == r_seq) & (ncol // nb <= r_tok)
    q = q_ref[...]
    zpad = jnp.zeros((LANES - qn, hd), F32)
    outs = [None] * n_q
    for g in range(ATT_KV_HEADS):
        sl = slice(g * hd, (g + 1) * hd)
        qs = jnp.concatenate([q[:, (g * grp + hq) * hd:(g * grp + hq + 1) * hd] for hq in range(grp)],
                             axis=0).astype(BF16)
        k_new = jnp.concatenate([kn_ref[:, sl], zpad], axis=0).astype(BF16)
        v_new = jnp.concatenate([vn_ref[:, sl], zpad], axis=0).astype(BF16)
        s_n = jnp.where(mask_n, _dot_nt(qs, k_new) * (hd ** -0.5), -jnp.inf)
        s_c = jnp.zeros((rows, lc), F32)
        for sq in range(nb):
            s_sq = _dot_nt(qs, kc_ref[sq, :, sl].astype(BF16))
            s_c = jnp.where(r_seq == sq, s_sq, s_c)
        s_c = jnp.where(mask_c, s_c * (hd ** -0.5), -jnp.inf)
        sink = jnp.zeros((rows, 1), F32)
        for hq in range(grp):
            sink = jnp.where(r // qn == hq, sink_ref[g * grp + hq], sink)
        m = jnp.maximum(jnp.maximum(jnp.max(s_c, axis=-1, keepdims=True),
                                    jnp.max(s_n, axis=-1, keepdims=True)), sink)
        p_c = jnp.exp(s_c - m)
        p_n = jnp.exp(s_n - m)
        denom = (jnp.sum(p_c, axis=-1, keepdims=True) + jnp.sum(p_n, axis=-1, keepdims=True)
                 + jnp.exp(sink - m))
        p_c = p_c / denom
        o = jnp.dot((p_n / denom).astype(BF16), v_new, preferred_element_type=F32)
        for sq in range(nb):
            o = o + jnp.dot(jnp.where(r_seq == sq, p_c, 0.0).astype(BF16), vc_ref[sq, :, sl].astype(BF16),
                            preferred_element_type=F32)
        for hq in range(grp):
            outs[g * grp + hq] = o[hq * qn:(hq + 1) * qn, :]
    o_ref[...] = jnp.concatenate(outs, axis=1)


def _attn_sample(sinks, q, k, v, kc, vc, tp, n_tok):
    nb = SEQ_PER_STEP
    qn = n_tok * nb
    nsteps = kc.shape[0] // nb
    lc, kvw = kc.shape[1], kc.shape[2]
    qw = q.shape[1]
    base = tp // qn
    rows = lambda s: (base + s, 0)
    return pl.pallas_call(
        functools.partial(_attn_sample_kernel, n_tok=n_tok),
        grid=(nsteps,),
        in_specs=[pl.BlockSpec(memory_space=pltpu.SMEM),
                  pl.BlockSpec((qn, qw), rows), pl.BlockSpec((qn, kvw), rows), pl.BlockSpec((qn, kvw), rows),
                  pl.BlockSpec((nb, lc, kvw), lambda s: (s, 0, 0)), pl.BlockSpec((nb, lc, kvw), lambda s: (s, 0, 0))],
        out_specs=[pl.BlockSpec((qn, qw), lambda s: (s, 0)),
                   pl.BlockSpec((nb, lc, kvw), lambda s: (s, 0, 0)), pl.BlockSpec((nb, lc, kvw), lambda s: (s, 0, 0))],
        out_shape=[jax.ShapeDtypeStruct((nsteps * qn, qw), F32),
                   jax.ShapeDtypeStruct(kc.shape, F32), jax.ShapeDtypeStruct(vc.shape, F32)],
        compiler_params=_cparams(("arbitrary",)),
        name="attn_sample",
    )(sinks, q, k, v, kc, vc)


def _router_kernel(hp_ref, hs_ref, fn_ref, wh_ref, wl_ref, b_ref, triu_ref, xn_ref, info_ref, info_t_ref, cnt_ref,
                   carry_sc, *, n_p):
    h = jnp.where(pl.program_id(0) < n_p, hp_ref[...], hs_ref[...])
    _router_math(h, fn_ref, wh_ref, wl_ref, b_ref, triu_ref, xn_ref, info_ref, info_t_ref, cnt_ref, carry_sc)


def _wo_router_kernel(op_ref, os_ref, res_ref, wo_ref, fn_ref, wh_ref, wl_ref, b_ref, triu_ref,
                      h_ref, xn_ref, info_ref, info_t_ref, cnt_ref, carry_sc, *, n_p):
    o = jnp.where(pl.program_id(0) < n_p, op_ref[...], os_ref[...])
    h = res_ref[...] + jnp.dot(o.astype(BF16), wo_ref[...], preferred_element_type=F32)
    h_ref[...] = h
    _router_math(h, fn_ref, wh_ref, wl_ref, b_ref, triu_ref, xn_ref, info_ref, info_t_ref, cnt_ref, carry_sc)


def _router_math(h, fn_ref, wh_ref, wl_ref, b_ref, triu_ref, xn_ref, info_ref, info_t_ref, cnt_ref, carry_sc):
    i = pl.program_id(0)

    @pl.when(i == 0)
    def _():
        carry_sc[...] = jnp.zeros(carry_sc.shape, F32)

    xn = _rms_scale(h) * fn_ref[...]
    _store_token_major(xn_ref, xn)
    x_hi, x_lo = _split_bf16(xn, 2)
    wh, wl = wh_ref[...], wl_ref[...]
    logits = (jnp.dot(x_hi, wh, preferred_element_type=F32) + jnp.dot(x_hi, wl, preferred_element_type=F32)
              + jnp.dot(x_lo, wh, preferred_element_type=F32)) + b_ref[...]
    tm = logits.shape[0]
    lt = logits.T
    per = MOE_EXPERTS_PER_GROUP
    row = lax.broadcasted_iota(I32, (per, tm), 0).astype(F32)
    big = float(LANES)
    neg = -jnp.inf

    lg = jnp.where(row < MOE_GROUPS, lt[MOE_EXPERTS:MOE_EXPERTS + per, :], neg)
    mg = jnp.max(lg, axis=0, keepdims=True)
    gp = 1.0 / jnp.sum(jnp.exp(lg - mg), axis=0, keepdims=True)
    gi = jnp.min(jnp.where(lg == mg, row, big), axis=0, keepdims=True)

    le = lt[0:per, :]
    for g in range(1, MOE_GROUPS):
        le = jnp.where(gi == g, lt[g * per:(g + 1) * per, :], le)
    m1 = jnp.max(le, axis=0, keepdims=True)
    i1 = jnp.min(jnp.where(le == m1, row, big), axis=0, keepdims=True)
    le2 = jnp.where(row == i1, neg, le)
    m2 = jnp.max(le2, axis=0, keepdims=True)
    i2 = jnp.min(jnp.where(le2 == m2, row, big), axis=0, keepdims=True)
    e2 = jnp.exp(m2 - m1)
    g1 = gp * (1.0 / (1.0 + e2))
    g2 = gp * (e2 / (1.0 + e2))
    x1 = gi * per + i1
    x2 = gi * per + i2

    row_e = lax.broadcasted_iota(I32, (LANES, tm), 0).astype(F32)
    a1 = row_e == x1
    a2 = row_e == x2
    onehot = jnp.where(a1 | a2, 1.0, 0.0)
    before = jnp.dot(onehot.astype(BF16), triu_ref[...], preferred_element_type=F32) + carry_sc[:, 0:1]
    r1 = jnp.sum(jnp.where(a1, before, 0.0), axis=0, keepdims=True)
    r2 = jnp.sum(jnp.where(a2, before, 0.0), axis=0, keepdims=True)
    carry_sc[...] = carry_sc[...] + jnp.sum(onehot, axis=1, keepdims=True)
    cnt_ref[...] = carry_sc[...]

    info_t = jnp.zeros((SUBLANES, tm), F32)
    field = lax.broadcasted_iota(I32, (SUBLANES, tm), 0)
    for k, val in enumerate((x1, x2, g1, g2, r1, r2)):
        info_t = jnp.where(field == k, val, info_t)
    info_t_ref[...] = info_t
    info_ref[...] = jnp.concatenate([info_t, jnp.zeros((LANES - SUBLANES, tm), F32)], axis=0).T


def _two_source_specs(tm, width, n_p):
    return [pl.BlockSpec((tm, width), lambda i: (jnp.minimum(i, n_p - 1), 0)),
            pl.BlockSpec((tm, width), lambda i: (jnp.maximum(i - n_p, 0), 0))]


def _router(h_p, h_s, fn, wh, wl, b, tril):
    d = h_p.shape[1]
    tm = TOKEN_TILE
    n_p = h_p.shape[0] // tm
    t = h_p.shape[0] + h_s.shape[0]
    rows = lambda i: (i, 0)
    return pl.pallas_call(
        functools.partial(_router_kernel, n_p=n_p),
        grid=(t // tm,),
        in_specs=_two_source_specs(tm, d, n_p) + [_const_spec(a.shape) for a in (fn, wh, wl, b, tril)],
        out_specs=[pl.BlockSpec((tm * ROW_TILES, LANES), rows), pl.BlockSpec((tm, LANES), rows),
                   pl.BlockSpec((SUBLANES, tm), lambda i: (0, i)), _const_spec((LANES, LANES))],
        out_shape=[jax.ShapeDtypeStruct((t * ROW_TILES, LANES), F32), jax.ShapeDtypeStruct((t, LANES), F32),
                   jax.ShapeDtypeStruct((SUBLANES, t), F32), jax.ShapeDtypeStruct((LANES, LANES), F32)],
        scratch_shapes=[pltpu.VMEM((LANES, LANES), F32)],
        compiler_params=_cparams(("arbitrary",)),
        name="moe_router",
    )(h_p, h_s, fn, wh, wl, b, tril)


def _wo_router(o_p, o_s, res, wo, fn, wh, wl, b, tril):
    t, d = res.shape
    tm = TOKEN_TILE
    n_p = o_p.shape[0] // tm
    rows = lambda i: (i, 0)
    return pl.pallas_call(
        functools.partial(_wo_router_kernel, n_p=n_p),
        grid=(t // tm,),
        in_specs=_two_source_specs(tm, o_p.shape[1], n_p) + [pl.BlockSpec((tm, d), rows)]
        + [_const_spec(a.shape) for a in (wo, fn, wh, wl, b, tril)],
        out_specs=[pl.BlockSpec((tm, d), rows), pl.BlockSpec((tm * ROW_TILES, LANES), rows),
                   pl.BlockSpec((tm, LANES), rows), pl.BlockSpec((SUBLANES, tm), lambda i: (0, i)),
                   _const_spec((LANES, LANES))],
        out_shape=[jax.ShapeDtypeStruct((t, d), F32), jax.ShapeDtypeStruct((t * ROW_TILES, LANES), F32),
                   jax.ShapeDtypeStruct((t, LANES), F32), jax.ShapeDtypeStruct((SUBLANES, t), F32),
                   jax.ShapeDtypeStruct((LANES, LANES), F32)],
        scratch_shapes=[pltpu.VMEM((LANES, LANES), F32)],
        compiler_params=_cparams(("arbitrary",)),
        name="wo_router",
    )(o_p, o_s, res, wo, fn, wh, wl, b, tril)


def _store_token_major(ref, x):
    n = x.shape[0]
    for j in range(ROW_TILES):
        ref[pl.ds(j, n, stride=ROW_TILES), :] = x[:, j * LANES:(j + 1) * LANES]


def _load_token_major(ref, n):
    return jnp.concatenate([ref[pl.ds(j, n, stride=ROW_TILES), :] for j in range(ROW_TILES)], axis=1)


def _dest_kernel(pst_ref, info_ref, dest_ref):
    info = info_ref[...]
    e = info[0:2, :]
    start = jnp.zeros(e.shape, F32)
    for k in range(MOE_EXPERTS):
        start = jnp.where(e == k, pst_ref[k].astype(F32), start)
    dest = (start + info[4:6, :]).astype(I32)
    dest_ref[...] = jnp.concatenate([dest, jnp.zeros((SUBLANES - 2, dest.shape[1]), I32)], axis=0)


def _dest(pstarts, info_t):
    return pl.pallas_call(
        _dest_kernel,
        in_specs=[pl.BlockSpec(memory_space=pltpu.SMEM), pl.BlockSpec(memory_space=pltpu.VMEM)],
        out_specs=pl.BlockSpec(memory_space=pltpu.VMEM),
        out_shape=jax.ShapeDtypeStruct(info_t.shape, I32),
        name="moe_dest",
    )(pstarts, info_t)


def _tile_copy(src, src_row, dst, dst_row, sem):
    return pltpu.make_async_copy(src.at[pl.ds(pl.multiple_of(src_row * ROW_TILES, ROW_TILES), ROW_TILES)],
                                 dst.at[pl.ds(pl.multiple_of(dst_row * ROW_TILES, ROW_TILES), ROW_TILES)], sem)


def _wait_tiles(ref, n_tokens, sem):
    blk = ref.at[pl.ds(0, n_tokens * ROW_TILES)]
    pltpu.make_async_copy(blk, blk, sem).wait()


def _dispatch_kernel(d1_ref, d2_ref, pend_ref, pcnt_ref, xn_ref, xbuf_hbm, zero_sc, sem_z, sem):
    i = pl.program_id(0)
    tm = xn_ref.shape[0] // ROW_TILES
    blk_rows = zero_sc.shape[0]

    @pl.when(i == 0)
    def _():
        zero_sc[...] = jnp.zeros(zero_sc.shape, F32)

        def zero_copy(e):
            start = pl.multiple_of(pend_ref[e] * ROW_TILES - blk_rows, ROW_TILES)
            return pltpu.make_async_copy(zero_sc, xbuf_hbm.at[pl.ds(start, blk_rows)], sem_z)

        for e in range(MOE_EXPERTS):
            @pl.when(pcnt_ref[e] > 0)
            def _():
                zero_copy(e).start()
        first_unused = pend_ref[MOE_EXPERTS - 1] * ROW_TILES // blk_rows
        n_blocks = xbuf_hbm.shape[0] // blk_rows

        def tail_copy(b):
            dst = xbuf_hbm.at[pl.ds(pl.multiple_of(b * blk_rows, blk_rows), blk_rows)]
            return pltpu.make_async_copy(zero_sc, dst, sem_z)

        def tail_start(b, carry):
            tail_copy(b).start()
            return carry

        def tail_wait(b, carry):
            tail_copy(b).wait()
            return carry
        lax.fori_loop(first_unused, n_blocks, tail_start, 0)
        for e in range(MOE_EXPERTS):
            @pl.when(pcnt_ref[e] > 0)
            def _():
                zero_copy(e).wait()
        lax.fori_loop(first_unused, n_blocks, tail_wait, 0)

    def body(r, carry):
        t = i * tm + r
        _tile_copy(xn_ref, r, xbuf_hbm, d1_ref[t], sem).start(priority=0)
        _tile_copy(xn_ref, r, xbuf_hbm, d2_ref[t], sem).start(priority=1)
        return carry
    lax.fori_loop(0, tm, body, 0, unroll=8)
    _wait_tiles(xn_ref, tm, sem)
    _wait_tiles(xn_ref, tm, sem)


def _dispatch(dest1, dest2, pends, pcounts, xn_tm, n_rows):
    n_tok = dest1.shape[0]
    tm = DISPATCH_TILE
    grid_spec = pltpu.PrefetchScalarGridSpec(
        num_scalar_prefetch=4,
        grid=(n_tok // tm,),
        in_specs=[pl.BlockSpec((tm * ROW_TILES, LANES), lambda i, *_: (i, 0))],
        out_specs=pl.BlockSpec(memory_space=pl.ANY),
        scratch_shapes=[pltpu.VMEM((EXPERT_ROWS * ROW_TILES, LANES), F32), pltpu.SemaphoreType.DMA(()),
                        pltpu.SemaphoreType.DMA(())],
    )
    return pl.pallas_call(
        _dispatch_kernel,
        grid_spec=grid_spec,
        out_shape=jax.ShapeDtypeStruct((n_rows * ROW_TILES, LANES), F32),
        compiler_params=_cparams(("arbitrary",)),
        name="moe_dispatch",
    )(dest1, dest2, pends, pcounts, xn_tm)


def _expert_kernel(blk_e_ref, nvalid_ref, next_e_ref, x_hbm, wg_hbm, wu_hbm, wd_hbm, y_ref,
                   wg_sc, wu_sc, wd_sc, wg_st, wu_st, wd_st, x_sc, sem, xsem, *, layer):
    i = pl.program_id(0)
    nv = nvalid_ref[0]
    blk_rows = x_sc.shape[1]
    rows = blk_rows // ROW_TILES

    def weight_copies(e):
        return (pltpu.make_async_copy(wg_hbm.at[layer, e], wg_st, sem.at[0]),
                pltpu.make_async_copy(wu_hbm.at[layer, e], wu_st, sem.at[1]),
                pltpu.make_async_copy(wd_hbm.at[layer, e], wd_st, sem.at[2]))

    def row_copy(step):
        slot = step % EXPERT_RING
        src = x_hbm.at[pl.ds(pl.multiple_of(step * blk_rows, blk_rows), blk_rows)]
        return pltpu.make_async_copy(src, x_sc.at[slot], xsem.at[slot])

    @pl.when((i == 0) & (nv > 0))
    def _():
        for c in weight_copies(blk_e_ref[0]):
            c.start()
        for s in range(EXPERT_RING - 1):
            @pl.when(s < nv)
            def _():
                row_copy(s).start()

    @pl.when(i + EXPERT_RING - 1 < nv)
    def _():
        row_copy(i + EXPERT_RING - 1).start()

    @pl.when(i < nv)
    def _():
        e = blk_e_ref[i]
        e_prev = blk_e_ref[jnp.maximum(i - 1, 0)]
        row_copy(i).wait()
        x_ref = x_sc.at[i % EXPERT_RING]

        @pl.when((i == 0) | (e != e_prev))
        def _():
            for c in weight_copies(e):
                c.wait()
            wg_sc[...] = wg_st[...].astype(BF16)
            wu_sc[...] = wu_st[...].astype(BF16)
            wd_sc[...] = wd_st[...].astype(BF16)
            nxt = next_e_ref[e]

            @pl.when(nxt < MOE_EXPERTS)
            def _():
                for c in weight_copies(nxt):
                    c.start()

        x = _load_token_major(x_ref, rows).astype(BF16)
        hid = _silu(jnp.dot(x, wg_sc[...], preferred_element_type=F32)) * jnp.dot(
            x, wu_sc[...], preferred_element_type=F32)
        _store_token_major(y_ref, jnp.dot(hid.astype(BF16), wd_sc[...], preferred_element_type=F32))

    @pl.when(i >= nv)
    def _():
        y_ref[...] = jnp.zeros(y_ref.shape, F32)


def _experts(blk_e, nvalid, next_e, xbuf, wg, wu, wd, layer):
    nblk = blk_e.shape[0]
    d, hdim = wg.shape[2], wg.shape[3]
    rows = EXPERT_ROWS * ROW_TILES
    grid_spec = pltpu.PrefetchScalarGridSpec(
        num_scalar_prefetch=3,
        grid=(nblk,),
        in_specs=[pl.BlockSpec(memory_space=pl.ANY),
                  pl.BlockSpec(memory_space=pl.ANY), pl.BlockSpec(memory_space=pl.ANY),
                  pl.BlockSpec(memory_space=pl.ANY)],
        out_specs=pl.BlockSpec((rows, LANES), lambda i, be, nv, ne: (i, 0)),
        scratch_shapes=[pltpu.VMEM((d, hdim), BF16), pltpu.VMEM((d, hdim), BF16), pltpu.VMEM((hdim, d), BF16),
                        pltpu.VMEM((d, hdim), F32), pltpu.VMEM((d, hdim), F32), pltpu.VMEM((hdim, d), F32),
                        pltpu.VMEM((EXPERT_RING, rows, LANES), F32),
                        pltpu.SemaphoreType.DMA((3,)), pltpu.SemaphoreType.DMA((EXPERT_RING,))],
    )
    return pl.pallas_call(
        functools.partial(_expert_kernel, layer=layer),
        grid_spec=grid_spec,
        out_shape=jax.ShapeDtypeStruct((nblk * rows, LANES), F32),
        compiler_params=_cparams(("arbitrary",)),
        name="moe_experts",
    )(blk_e, nvalid, next_e, xbuf, wg, wu, wd)


def _gather_moe_rows(d1_ref, d2_ref, y_hbm, r_sc, sem, tm, row0=0):
    i = pl.program_id(0)
    n = pl.num_programs(0)

    def start(step):
        slot = step % 2
        base = row0 + step * tm

        def body(r, carry):
            _tile_copy(y_hbm, d1_ref[base + r], r_sc.at[slot, 0], r, sem.at[slot]).start(priority=0)
            _tile_copy(y_hbm, d2_ref[base + r], r_sc.at[slot, 1], r, sem.at[slot]).start(priority=1)
            return carry
        lax.fori_loop(0, tm, body, 0, unroll=8)

    @pl.when(i == 0)
    def _():
        start(i)

    @pl.when(i + 1 < n)
    def _():
        start(i + 1)

    slot = i % 2
    _wait_tiles(r_sc.at[slot, 0], tm, sem.at[slot])
    _wait_tiles(r_sc.at[slot, 1], tm, sem.at[slot])
    return _load_token_major(r_sc.at[slot, 0], tm), _load_token_major(r_sc.at[slot, 1], tm)


def _combine_kernel(d1_ref, d2_ref, h_ref, info_ref, y_hbm, o_ref, r_sc, sem, *, row0):
    y1, y2 = _gather_moe_rows(d1_ref, d2_ref, y_hbm, r_sc, sem, h_ref.shape[0], row0)
    info = info_ref[...]
    o_ref[...] = h_ref[...] + (y1 * info[:, 2:3] + y2 * info[:, 3:4])


def _combine(dest1, dest2, h, info, ybuf, row0, nrows):
    d = h.shape[1]
    tm = TOKEN_TILE
    base_tile = row0 // tm
    rows = lambda i, a, b: (base_tile + i, 0)
    grid_spec = pltpu.PrefetchScalarGridSpec(
        num_scalar_prefetch=2,
        grid=(nrows // tm,),
        in_specs=[pl.BlockSpec((tm, d), rows), pl.BlockSpec((tm, LANES), rows), pl.BlockSpec(memory_space=pl.ANY)],
        out_specs=pl.BlockSpec((tm, d), lambda i, a, b: (i, 0)),
        scratch_shapes=[pltpu.VMEM((2, 2, tm * ROW_TILES, LANES), F32), pltpu.SemaphoreType.DMA((2,))],
    )
    return pl.pallas_call(
        functools.partial(_combine_kernel, row0=row0),
        grid_spec=grid_spec,
        out_shape=jax.ShapeDtypeStruct((nrows, d), F32),
        compiler_params=_cparams(("arbitrary",)),
        name="moe_combine",
    )(dest1, dest2, h, info, ybuf)


def _combine_qkv_kernel(d1_ref, d2_ref, hp_ref, hs_ref, info_ref, y_hbm, kvn_ref, an_ref, wkv_ref, wq_ref, kn_ref,
                        qn_ref, rc_ref, rs1_ref, rs2_ref, hsum_ref, rq_ref, eq_ref,
                        h_ref, q_ref, k_ref, v_ref, r_sc, sem, *, n_p):
    y1, y2 = _gather_moe_rows(d1_ref, d2_ref, y_hbm, r_sc, sem, h_ref.shape[0])
    info = info_ref[...]
    h = jnp.where(pl.program_id(0) < n_p, hp_ref[...], hs_ref[...]) + (y1 * info[:, 2:3] + y2 * info[:, 3:4])
    h_ref[...] = h
    _qkv_math(h, kvn_ref, an_ref, wkv_ref, wq_ref, kn_ref, qn_ref, rc_ref, rs1_ref, rs2_ref,
              hsum_ref, rq_ref, eq_ref, q_ref, k_ref, v_ref)


def _combine_qkv(dest1, dest2, h_p, h_s, info, ybuf, kvn, an, wkv, wq, knt, qnt, rc, rs1, rs2, hsum, rq, eq):
    d = h_p.shape[1]
    tm = TOKEN_TILE
    n_p = h_p.shape[0] // tm
    t = h_p.shape[0] + h_s.shape[0]
    kvw = wkv.shape[1] // 2
    qw = wq.shape[1]
    rows = lambda i, a, b: (i, 0)
    const = lambda arr: pl.BlockSpec(arr.shape, lambda i, a, b: (0,) * arr.ndim)
    seq_tiles = (rc.shape[0] - h_s.shape[0]) // tm
    rope_rows = lambda i, a, b: (jnp.where(i < n_p, i % seq_tiles, seq_tiles + i - n_p), 0)
    grid_spec = pltpu.PrefetchScalarGridSpec(
        num_scalar_prefetch=2,
        grid=(t // tm,),
        in_specs=[pl.BlockSpec((tm, d), lambda i, a, b: (jnp.minimum(i, n_p - 1), 0)),
                  pl.BlockSpec((tm, d), lambda i, a, b: (jnp.maximum(i - n_p, 0), 0)),
                  pl.BlockSpec((tm, LANES), rows), pl.BlockSpec(memory_space=pl.ANY)]
        + [const(a) for a in (kvn, an, wkv, wq, knt, qnt)] + [pl.BlockSpec((tm, LANES), rope_rows)] * 3
        + [const(a) for a in (hsum, rq, eq)],
        out_specs=[pl.BlockSpec((tm, d), rows), pl.BlockSpec((tm, qw), rows), pl.BlockSpec((tm, kvw), rows),
                   pl.BlockSpec((tm, kvw), rows)],
        scratch_shapes=[pltpu.VMEM((2, 2, tm * ROW_TILES, LANES), F32), pltpu.SemaphoreType.DMA((2,))],
    )
    return pl.pallas_call(
        functools.partial(_combine_qkv_kernel, n_p=n_p),
        grid_spec=grid_spec,
        out_shape=[jax.ShapeDtypeStruct((t, d), F32), jax.ShapeDtypeStruct((t, qw), F32),
                   jax.ShapeDtypeStruct((t, kvw), F32), jax.ShapeDtypeStruct((t, kvw), F32)],
        compiler_params=_cparams(("arbitrary",)),
        name="combine_qkv",
    )(dest1, dest2, h_p, h_s, info, ybuf, kvn, an, wkv, wq, knt, qnt, rc, rs1, rs2, hsum, rq, eq)


def _router_weights(w_grp, b_grp, w_rt, b_rt):
    d = w_rt.shape[0]
    w_cat = jnp.zeros((d, LANES), F32).at[:, :MOE_EXPERTS].set(w_rt).at[:, MOE_EXPERTS:MOE_EXPERTS + MOE_GROUPS].set(w_grp)
    b_cat = jnp.zeros((1, LANES), F32).at[0, :MOE_EXPERTS].set(b_rt).at[0, MOE_EXPERTS:MOE_EXPERTS + MOE_GROUPS].set(b_grp)
    w_hi = w_cat.astype(BF16)
    w_lo = (w_cat - w_hi.astype(F32)).astype(BF16)
    return w_hi, w_lo, b_cat


def _moe_experts(xn_tm, info_t, cnt, wg, wu, wd, layer):
    t = info_t.shape[1]
    counts = cnt[:MOE_EXPERTS, 0].astype(I32)
    pcounts = (counts + EXPERT_ROWS - 1) // EXPERT_ROWS * EXPERT_ROWS
    pends = jnp.cumsum(pcounts)
    pstarts = pends - pcounts
    nblk = -(-(2 * t + MOE_EXPERTS * (EXPERT_ROWS - 1)) // EXPERT_ROWS)
    blk_start = jnp.arange(nblk, dtype=I32) * EXPERT_ROWS
    blk_e = jnp.minimum(jnp.sum((pends[None, :] <= blk_start[:, None]).astype(I32), axis=1), MOE_EXPERTS - 1)
    nvalid = pends[-1:] // EXPERT_ROWS
    eid = jnp.arange(MOE_EXPERTS, dtype=I32)
    later = (eid[None, :] > eid[:, None]) & (pcounts[None, :] > 0)
    next_e = jnp.min(jnp.where(later, eid[None, :], MOE_EXPERTS), axis=1).astype(I32)

    dest = _dest(pstarts, info_t)
    dest1, dest2 = dest[0], dest[1]
    xbuf = _dispatch(dest1, dest2, pends, pcounts, xn_tm, nblk * EXPERT_ROWS)
    return dest1, dest2, _experts(blk_e, nvalid, next_e, xbuf, wg, wu, wd, layer)


def _rope_tables(pos):
    half = ROT_DIM // 2
    inv = ROPE_THETA ** (-np.arange(0, ROT_DIM, 2, dtype=np.float64) / ROT_DIM)
    ang = pos.astype(np.float64)[:, None] * inv[None, :]
    cos, sin = np.cos(ang), np.sin(ang)
    n = pos.shape[0]
    ones = np.ones((n, ATT_HEAD_DIM - ROT_DIM))
    zeros_r = np.zeros((n, ATT_HEAD_DIM - ROT_DIM))
    zeros_h = np.zeros((n, half))
    c = np.concatenate([cos, cos, ones], axis=1)
    s1 = np.concatenate([-sin, zeros_h, zeros_r], axis=1)
    s2 = np.concatenate([zeros_h, sin, zeros_r], axis=1)
    reps = LANES // ATT_HEAD_DIM
    return tuple(jnp.asarray(np.tile(a, (1, reps)).astype(np.float32)) for a in (c, s1, s2))


def _mask01(m):
    return jnp.asarray(m.astype(np.float32), dtype=BF16)


def _to_step_order(a, nsteps, n_tok):
    c = a.shape[-1]
    return a.reshape(nsteps, SEQ_PER_STEP, n_tok, c).transpose(0, 2, 1, 3).reshape(nsteps * n_tok * SEQ_PER_STEP, c)


def _from_step_order(a, nsteps, n_tok):
    c = a.shape[-1]
    return a.reshape(nsteps, n_tok, SEQ_PER_STEP, c).transpose(0, 2, 1, 3).reshape(nsteps * SEQ_PER_STEP, n_tok, c)


def kernel(x_prompt, x_sample, state_ssm, state_conv, cache_k_win, cache_v_win, ssm_norm, ssm_w_in, ssm_conv_w, ssm_conv_b, ssm_dt_bias, ssm_a_log, ssm_d, ssm_gate_norm, ssm_w_out, kv_norm, w_kv, k_norm, attn_norm, w_q, q_norm, sinks, w_o, ffn_norm, moe_w_group, moe_b_group, moe_w_router, moe_b_router, moe_w_gate, moe_w_up, moe_w_down):
    bp, seq, d = x_prompt.shape
    bs, n_tok, _ = x_sample.shape
    tp, ts = bp * seq, bs * n_tok
    nsteps = bs // SEQ_PER_STEP
    n_heads = ssm_d.shape[1]
    di = n_heads * SSM_HEAD_DIM
    gn_w = SSM_GROUPS * SSM_STATE
    cdim = di + 2 * gn_w
    n_q = sinks.shape[1]
    kvw = ATT_KV_HEADS * ATT_HEAD_DIM

    xp2 = x_prompt.reshape(tp, d)
    xs2 = _to_step_order(x_sample, nsteps, n_tok)

    lane_i = np.arange(LANES)
    e01 = _mask01(lane_i[:, None] == (np.arange(di) // SSM_HEAD_DIM)[None, :])
    hpg = di // SSM_GROUPS
    g1 = ((np.arange(gn_w) // SSM_STATE)[:, None] == (lane_i // (hpg // SSM_HEAD_DIM))[None, :])
    g1 = _mask01(g1 & (lane_i < n_heads)[None, :])
    tril_c = _mask01(np.arange(SSM_CHUNK)[:, None] >= np.arange(SSM_CHUNK)[None, :])
    earlier_x = _mask01(np.arange(TOKEN_TILE)[:, None] < np.arange(TOKEN_TILE)[None, :])
    hsum = _mask01((np.arange(kvw) // ATT_HEAD_DIM)[:, None] == (np.arange(kvw) // ATT_HEAD_DIM)[None, :])
    qw = n_q * ATT_HEAD_DIM
    rq_np = (np.arange(qw) // ATT_HEAD_DIM)[:, None] == lane_i[None, :]
    rq, eq = _mask01(rq_np), _mask01(rq_np.T)

    w_in = ssm_w_in[0]
    wz = w_in[:, :di].astype(BF16)
    wx = w_in[:, di:di + cdim].astype(BF16)
    wd = jnp.zeros((d, LANES), F32).at[:, :n_heads].set(w_in[:, di + cdim:]).astype(BF16)
    cw, cb = ssm_conv_w[0], ssm_conv_b[0].reshape(1, cdim)
    z, xbc, dt = _inproj(xp2, xs2, ssm_norm[0].reshape(1, d), wz, wx, wd)

    pad_h = lambda v: jnp.zeros((1, LANES), F32).at[0, :n_heads].set(v)
    dtb, alog = pad_h(ssm_dt_bias[0]), pad_h(ssm_a_log[0])
    dsk = jnp.repeat(ssm_d[0], SSM_HEAD_DIM).reshape(1, di)
    gnw = ssm_gate_norm[0].reshape(1, di)

    w_out = ssm_w_out[0].astype(BF16)
    h_p, s_fin, c_fin = _ssd_prompt(z, xbc, dt, xp2, cw, cb, dtb, alog, dsk, gnw, tril_c, e01, w_out, bp, seq)
    ssm_p = s_fin.reshape(1, bp, n_heads, SSM_HEAD_DIM, SSM_STATE)
    conv_p = c_fin[:, SUBLANES - (SSM_CONV - 1):, :].reshape(1, bp, SSM_CONV - 1, cdim)

    xbc_s = xbc[tp:].reshape(nsteps, n_tok, SEQ_PER_STEP, cdim)
    conv_in = state_conv[0].reshape(nsteps, SEQ_PER_STEP, SSM_CONV - 1, cdim).transpose(0, 2, 1, 3)
    xp7 = jnp.concatenate([conv_in, xbc_s], axis=1)
    s0 = state_ssm[0].reshape(bs, di, SSM_STATE)
    h_s, s_new = _ssd_sample(z, xp7, dt, s0, xs2, cw, cb, dtb, alog, dsk, gnw, e01, g1, w_out, tp, n_tok)
    ssm_s = s_new.reshape(1, bs, n_heads, SSM_HEAD_DIM, SSM_STATE)
    conv_s = _from_step_order(xbc[tp:], nsteps, n_tok)[:, n_tok - (SSM_CONV - 1):, :].reshape(
        1, bs, SSM_CONV - 1, cdim)

    w_hi, w_lo, b_cat = _router_weights(moe_w_group[0], moe_b_group[0], moe_w_router[0], moe_b_router[0])
    xn_tm, info, info_t, cnt = _router(h_p, h_s, ffn_norm[0].reshape(1, d), w_hi, w_lo, b_cat, earlier_x)
    dest1, dest2, ybuf = _moe_experts(xn_tm, info_t, cnt, moe_w_gate, moe_w_up, moe_w_down, 0)

    pos = np.concatenate([np.arange(seq), np.tile(np.repeat(PAST_LEN + np.arange(n_tok), SEQ_PER_STEP), nsteps)])
    rc, rs1, rs2 = _rope_tables(pos)
    h1, q, k, v = _combine_qkv(dest1, dest2, h_p, h_s, info, ybuf, kv_norm.reshape(1, d), attn_norm[0].reshape(1, d),
                               w_kv.astype(BF16), w_q[0].astype(BF16),
                               jnp.tile(k_norm, ATT_KV_HEADS).reshape(1, kvw), jnp.tile(q_norm[0], n_q).reshape(1, qw),
                               rc, rs1, rs2, hsum, rq, eq)
    sk = sinks[0]
    lc = cache_k_win.shape[1]
    kc = cache_k_win.reshape(bs, lc, kvw)
    vc = cache_v_win.reshape(bs, lc, kvw)
    o_p = _attn_prompt(sk, q, k, v, bp, seq)
    o_s, k_win, v_win = _attn_sample(sk, q, k, v, kc, vc, tp, n_tok)
    w_hi, w_lo, b_cat = _router_weights(moe_w_group[1], moe_b_group[1], moe_w_router[1], moe_b_router[1])
    h2, xn_tm, info, info_t, cnt = _wo_router(o_p, o_s, h1, w_o[0].astype(BF16), ffn_norm[1].reshape(1, d),
                                             w_hi, w_lo, b_cat, earlier_x)
    dest1, dest2, ybuf = _moe_experts(xn_tm, info_t, cnt, moe_w_gate, moe_w_up, moe_w_down, 1)
    y_p = _combine(dest1, dest2, h2, info, ybuf, 0, tp)
    y_s = _combine(dest1, dest2, h2, info, ybuf, tp, ts)

    wl = min(WINDOW, seq)
    k_p = k[:tp].reshape(bp, seq, kvw)[:, seq - wl:].reshape(bp, wl, ATT_KV_HEADS, ATT_HEAD_DIM)
    v_p = v[:tp].reshape(bp, seq, kvw)[:, seq - wl:].reshape(bp, wl, ATT_KV_HEADS, ATT_HEAD_DIM)
    k_s = k_win.reshape(bs, lc, ATT_KV_HEADS, ATT_HEAD_DIM)
    v_s = v_win.reshape(bs, lc, ATT_KV_HEADS, ATT_HEAD_DIM)
    return (y_p.reshape(bp, seq, d), _from_step_order(y_s, nsteps, n_tok),
            ssm_p, conv_p, k_p, v_p, ssm_s, conv_s, k_s, v_s)
```

```python
import functools

import numpy as np
import jax
import jax.numpy as jnp
from jax import lax
from jax.experimental import pallas as pl
from jax.experimental.pallas import tpu as pltpu

F32 = jnp.float32
BF16 = jnp.bfloat16
I32 = jnp.int32

EPS = 1e-6
SSM_HEAD_DIM = 64
SSM_GROUPS = 4
SSM_STATE = 128
SSM_CONV = 4
SSM_CHUNK = 128
ATT_HEAD_DIM = 64
ATT_KV_HEADS = 4
WINDOW = 128
ROT_DIM = ATT_HEAD_DIM // 4
ROPE_THETA = 500000.0
MOE_GROUPS = 4
MOE_EXPERTS_PER_GROUP = 8
MOE_EXPERTS = MOE_GROUPS * MOE_EXPERTS_PER_GROUP
MOE_BLOCK = 128
PAST_LEN = 16384

LANES = 128
SUBLANES = 8
BF16_ROWS = 16
SEQ_PER_STEP = SUBLANES
TOKEN_TILE = 256
ROUTER_TILE = 512
INPROJ_TILE = 512
ROW_TILES = 8
DISPATCH_TILE = 512
EXPERT_RING = 3
EXPERT_ROWS = 512
VMEM_LIMIT = 56 * 1024 * 1024


def _cparams(sem):
    return pltpu.CompilerParams(dimension_semantics=sem, vmem_limit_bytes=VMEM_LIMIT)


def _const_spec(shape):
    nd = len(shape)
    return pl.BlockSpec(shape, lambda *_: (0,) * nd)


def _split_bf16(v, n):
    parts = []
    r = v
    for k in range(n):
        p = r.astype(BF16)
        parts.append(p)
        if k + 1 < n:
            r = r - p.astype(F32)
    return parts


def _mm01(v, m01, n=3):
    acc = None
    for p in _split_bf16(v, n):
        d = jnp.dot(p, m01, preferred_element_type=F32)
        acc = d if acc is None else acc + d
    return acc


def _mm01_left(m01, v, n=3):
    acc = None
    for p in _split_bf16(v, n):
        d = jnp.dot(m01, p, preferred_element_type=F32)
        acc = d if acc is None else acc + d
    return acc


def _dot_nt(a, b):
    return lax.dot_general(a, b, (((1,), (1,)), ((), ())), preferred_element_type=F32)


def _sigmoid(x):
    return 0.5 * jnp.tanh(0.5 * x) + 0.5


def _silu(x):
    return x * _sigmoid(x)


def _softplus(x):
    return jnp.maximum(x, 0.0) + jnp.log1p(jnp.exp(-jnp.abs(x)))


def _rms_scale(x):
    return x * lax.rsqrt(jnp.mean(x * x, axis=-1, keepdims=True) + EPS)


def _gate_norm(y, z, gn, n_groups):
    yz = y * _silu(z)
    w = yz.shape[-1] // n_groups
    outs = []
    for g in range(n_groups):
        v = yz[:, g * w:(g + 1) * w]
        outs.append(_rms_scale(v) * gn[:, g * w:(g + 1) * w])
    return jnp.concatenate(outs, axis=1)


def _rope(x, c, s1, s2):
    w = x.shape[-1]
    return x * c + pltpu.roll(x, w - ROT_DIM // 2, 1) * s1 + pltpu.roll(x, ROT_DIM // 2, 1) * s2


def _tile_lanes(t, reps):
    return t if reps == 1 else jnp.concatenate([t] * reps, axis=1)


def _inproj_kernel(xp_ref, xs_ref, g_ref, wz_ref, wx_ref, wd_ref, z_ref, xbc_ref, dt_ref, *, n_p):
    i = pl.program_id(0)
    x = jnp.where(i < n_p, xp_ref[...], xs_ref[...])
    xn = (_rms_scale(x) * g_ref[...]).astype(BF16)
    z_ref[...] = jnp.dot(xn, wz_ref[...], preferred_element_type=F32)
    xbc_ref[...] = jnp.dot(xn, wx_ref[...], preferred_element_type=F32)
    dt_ref[...] = jnp.dot(xn, wd_ref[...], preferred_element_type=F32)


def _inproj(xp2, xs2, g, wz, wx, wd):
    tp, d = xp2.shape
    ts = xs2.shape[0]
    tm = INPROJ_TILE
    n_p, n_s = tp // tm, ts // tm
    t = tp + ts
    return pl.pallas_call(
        functools.partial(_inproj_kernel, n_p=n_p),
        grid=(n_p + n_s,),
        in_specs=_two_source_specs(tm, d, n_p) + [_const_spec(a.shape) for a in (g, wz, wx, wd)],
        out_specs=[
            pl.BlockSpec((tm, wz.shape[1]), lambda i: (i, 0)),
            pl.BlockSpec((tm, wx.shape[1]), lambda i: (i, 0)),
            pl.BlockSpec((tm, wd.shape[1]), lambda i: (i, 0)),
        ],
        out_shape=[jax.ShapeDtypeStruct((t, wz.shape[1]), F32),
                   jax.ShapeDtypeStruct((t, wx.shape[1]), F32),
                   jax.ShapeDtypeStruct((t, wd.shape[1]), F32)],
        compiler_params=_cparams(("arbitrary",)),
        name="inproj",
    )(xp2, xs2, g, wz, wx, wd)


def _ssd_prompt_kernel(z_ref, xbc_ref, dt_ref, x_ref, cw_ref, cb_ref, dtb_ref, alog_ref, dsk_ref, gn_ref,
                       tril_ref, e_ref, wout_ref, h_ref, sfin_ref, cfin_ref, xpad_sc, st_sc):
    c = pl.program_id(1)
    q = SSM_CHUNK
    cd = xbc_ref.shape[1]
    di = z_ref.shape[1]
    gn_w = SSM_GROUPS * SSM_STATE
    hpg = di // SSM_GROUPS
    pad = SUBLANES

    n_slab = cd // LANES

    @pl.when(c == 0)
    def _():
        xpad_sc[:, 0:pad, :] = jnp.zeros((n_slab, pad, LANES), F32)
        st_sc[...] = jnp.zeros(st_sc.shape, F32)

    @pl.when(c > 0)
    def _():
        xpad_sc[:, 0:pad, :] = xpad_sc[:, q:q + pad, :]

    slabs = []
    for j in range(n_slab):
        ls = slice(j * LANES, (j + 1) * LANES)
        xpad_sc[j, pad:pad + q, :] = xbc_ref[:, ls]
        acc = cb_ref[:, ls]
        for k in range(SSM_CONV):
            off = pad - (SSM_CONV - 1) + k
            acc = acc + xpad_sc[j, off:off + q, :] * cw_ref[k:k + 1, ls]
        slabs.append(_silu(acc))
    xc = jnp.concatenate(slabs, axis=1)
    xs = xc[:, :di]
    bm = xc[:, di:di + gn_w]
    cm = xc[:, di + gn_w:]

    dt = _softplus(dt_ref[...] + dtb_ref[...])
    a = -jnp.exp(alog_ref[...])
    act = _mm01_left(tril_ref[...], dt * a)
    act_t = act.T
    act_last = act[q - 1:q, :]
    pieces = (_split_bf16(dt, 2) + _split_bf16(jnp.exp(act_last - act), 2) + _split_bf16(jnp.exp(act), 2)
              + _split_bf16(jnp.exp(act[q - BF16_ROWS:q, :]), 3))
    ex = jnp.dot(jnp.concatenate(pieces, axis=0), e_ref[...], preferred_element_type=F32)
    xdt = xs * (ex[0:q] + ex[q:2 * q])
    xd = xdt * (ex[2 * q:3 * q] + ex[3 * q:4 * q])
    eax = ex[4 * q:5 * q] + ex[5 * q:6 * q]
    tail = ex[6 * q:]
    cd = (tail[0:BF16_ROWS] + tail[BF16_ROWS:2 * BF16_ROWS]) + tail[2 * BF16_ROWS:]
    cdx = cd[BF16_ROWS - 1:BF16_ROWS, :]

    row = lax.broadcasted_iota(I32, (q, q), 0)
    col = lax.broadcasted_iota(I32, (q, q), 1)
    causal = row >= col
    lane = lax.broadcasted_iota(I32, (q, LANES), 1)
    lo_half = lane < SSM_HEAD_DIM

    y_parts = []
    heads_per_group = hpg // SSM_HEAD_DIM
    for g in range(SSM_GROUPS):
        cg = cm[:, g * SSM_STATE:(g + 1) * SSM_STATE].astype(BF16)
        bg = bm[:, g * SSM_STATE:(g + 1) * SSM_STATE]
        cb = _dot_nt(cg, bg.astype(BF16))
        st_g = st_sc[:, g * hpg:(g + 1) * hpg]
        y_off = jnp.dot(cg, st_g.astype(BF16), preferred_element_type=F32)
        for pr in range(heads_per_group // 2):
            h0 = g * heads_per_group + 2 * pr
            ms = []
            for h in (h0, h0 + 1):
                seg = act[:, h:h + 1] - act_t[h:h + 1, :]
                lm = jnp.exp(jnp.where(causal, seg, -jnp.inf))
                ms.append((cb * lm).astype(BF16))
            m2 = jnp.concatenate(ms, axis=1)
            xpair = xdt[:, h0 * SSM_HEAD_DIM:(h0 + 2) * SSM_HEAD_DIM]
            rhs = jnp.concatenate([jnp.where(lo_half, xpair, 0.0),
                                   jnp.where(lo_half, 0.0, xpair)], axis=0).astype(BF16)
            y_d = jnp.dot(m2, rhs, preferred_element_type=F32)
            lo = 2 * pr * SSM_HEAD_DIM
            y_parts.append(y_d + y_off[:, lo:lo + LANES] * eax[:, g * hpg + lo:g * hpg + lo + LANES])
        upd = jnp.dot(bg.T.astype(BF16), xd[:, g * hpg:(g + 1) * hpg].astype(BF16),
                      preferred_element_type=F32)
        st_sc[:, g * hpg:(g + 1) * hpg] = st_g * cdx[:, g * hpg:(g + 1) * hpg] + upd

    y = jnp.concatenate(y_parts, axis=1) + xs * dsk_ref[...]
    yg = _gate_norm(y, z_ref[...], gn_ref[...], SSM_GROUPS)
    h_ref[...] = x_ref[...] + jnp.dot(yg.astype(BF16), wout_ref[...], preferred_element_type=F32)

    @pl.when(c == pl.num_programs(1) - 1)
    def _():
        sfin_ref[0] = st_sc[...].T
        cfin_ref[0] = jnp.concatenate([xpad_sc[j, q:q + pad, :] for j in range(n_slab)], axis=1)


def _ssd_prompt(z, xbc, dt, x, cw, cb, dtb, alog, dsk, gnw, tril, e01, wout, bp, seq):
    nc = seq // SSM_CHUNK
    q = SSM_CHUNK
    di, cd, d = z.shape[1], xbc.shape[1], x.shape[1]
    rows = lambda b, c: (b * nc + c, 0)
    return pl.pallas_call(
        _ssd_prompt_kernel,
        grid=(bp, nc),
        in_specs=[
            pl.BlockSpec((q, di), rows), pl.BlockSpec((q, cd), rows), pl.BlockSpec((q, LANES), rows),
            pl.BlockSpec((q, d), rows),
            _const_spec(cw.shape), _const_spec(cb.shape), _const_spec(dtb.shape), _const_spec(alog.shape),
            _const_spec(dsk.shape), _const_spec(gnw.shape), _const_spec(tril.shape), _const_spec(e01.shape),
            _const_spec(wout.shape),
        ],
        out_specs=[
            pl.BlockSpec((q, d), rows),
            pl.BlockSpec((1, di, SSM_STATE), lambda b, c: (b, 0, 0)),
            pl.BlockSpec((1, SUBLANES, cd), lambda b, c: (b, 0, 0)),
        ],
        out_shape=[jax.ShapeDtypeStruct((bp * seq, d), F32),
                   jax.ShapeDtypeStruct((bp, di, SSM_STATE), F32),
                   jax.ShapeDtypeStruct((bp, SUBLANES, cd), F32)],
        scratch_shapes=[pltpu.VMEM((cd // LANES, q + 2 * SUBLANES, LANES), F32), pltpu.VMEM((SSM_STATE, di), F32)],
        compiler_params=_cparams(("arbitrary", "arbitrary")),
        name="ssd_prompt",
    )(z, xbc, dt, x, cw, cb, dtb, alog, dsk, gnw, tril, e01, wout)


def _ssd_sample_kernel(z_ref, xp_ref, dt_ref, s0_ref, x_ref, cw_ref, cb_ref, dtb_ref, alog_ref, dsk_ref, gn_ref,
                       e_ref, g1_ref, wout_ref, h_ref, sn_ref, yoff_sc, *, n_tok):
    hf = pl.program_id(1)
    nb = SEQ_PER_STEP
    half = nb // 2
    q = n_tok * nb
    di = z_ref.shape[1]
    gn_w = SSM_GROUPS * SSM_STATE
    hpg = di // SSM_GROUPS

    taps = [xp_ref[0, m] for m in range(n_tok + SSM_CONV - 1)]
    slabs = []
    for t in range(n_tok):
        acc = cb_ref[...]
        for k in range(SSM_CONV):
            acc = acc + taps[t + k] * cw_ref[k:k + 1, :]
        slabs.append(_silu(acc))
    xc = jnp.concatenate(slabs, axis=0)
    xs = xc[:, :di]
    bm = xc[:, di:di + gn_w]
    cm = xc[:, di + gn_w:]

    dt = _softplus(dt_ref[...] + dtb_ref[...])
    da = dt * (-jnp.exp(alog_ref[...]))
    acts = []
    run = None
    for t in range(n_tok):
        d = da[t * nb:(t + 1) * nb, :]
        run = d if run is None else run + d
        acts.append(run)
    act = jnp.concatenate(acts, axis=0)
    act_last = jnp.concatenate([acts[-1]] * n_tok, axis=0)
    e01 = e_ref[...]
    xdt = xs * _mm01(dt, e01)
    xd = xdt * _mm01(jnp.exp(act_last - act), e01)
    eax = _mm01(jnp.exp(act), e01)
    cdx = _mm01(jnp.exp(acts[-1]), e01)

    pairs = [(t, u) for t in range(n_tok) for u in range(t + 1)]
    cbp = jnp.concatenate([cm[t * nb:(t + 1) * nb, :] * bm[u * nb:(u + 1) * nb, :] for t, u in pairs], axis=0)
    seg = jnp.concatenate([acts[t] - acts[u] for t, u in pairs], axis=0)
    coef = _mm01(_mm01(cbp, g1_ref[...]) * jnp.exp(seg), e01)
    y_slabs = []
    for t in range(n_tok):
        acc = None
        for pi, (tt, u) in enumerate(pairs):
            if tt != t:
                continue
            term = coef[pi * nb:(pi + 1) * nb, :] * xdt[u * nb:(u + 1) * nb, :]
            acc = term if acc is None else acc + term
        y_slabs.append(acc)
    y_diag = jnp.concatenate(y_slabs, axis=0)

    zpad = jnp.concatenate([xd,
                            jnp.where(hf == 0, cdx[0:half, :], cdx[half:nb, :]),
                            jnp.zeros((LANES - q - half, di), F32)], axis=0)
    zt = zpad.T
    row_seq = lax.broadcasted_iota(I32, (q, 1), 0) % nb
    cm_b = cm.astype(BF16)
    y_off_g = [None] * SSM_GROUPS
    for sl in range(half):
        in_seq = row_seq == hf * half + sl
        for g in range(SSM_GROUPS):
            s_old = s0_ref[sl, g * hpg:(g + 1) * hpg, :]
            c_g = jnp.where(in_seq, cm_b[:, g * SSM_STATE:(g + 1) * SSM_STATE], jnp.zeros((), BF16))
            yo = _dot_nt(c_g, s_old.astype(BF16))
            y_off_g[g] = yo if y_off_g[g] is None else y_off_g[g] + yo
            b_g = jnp.where(in_seq, bm[:, g * SSM_STATE:(g + 1) * SSM_STATE], 0.0)
            b_pad = jnp.concatenate([b_g, jnp.zeros((LANES - q, SSM_STATE), F32)], axis=0).astype(BF16)
            zt_g = zt[g * hpg:(g + 1) * hpg, :]
            upd = jnp.dot(zt_g.astype(BF16), b_pad, preferred_element_type=F32)
            decay = zt_g[:, q + sl:q + sl + 1]
            sn_ref[sl, g * hpg:(g + 1) * hpg, :] = s_old * decay + upd
    y_off = jnp.concatenate(y_off_g, axis=1)

    @pl.when(hf == 0)
    def _():
        yoff_sc[...] = y_off

    @pl.when(hf == 1)
    def _():
        y = y_diag + (yoff_sc[...] + y_off) * eax + xs * dsk_ref[...]
        yg = _gate_norm(y, z_ref[...], gn_ref[...], SSM_GROUPS)
        h_ref[...] = x_ref[...] + jnp.dot(yg.astype(BF16), wout_ref[...], preferred_element_type=F32)


def _ssd_sample(z, xp7, dt, s0, x, cw, cb, dtb, alog, dsk, gnw, e01, g1, wout, tp, n_tok):
    nsteps = xp7.shape[0]
    nb = SEQ_PER_STEP
    half = nb // 2
    q = n_tok * nb
    di, d = z.shape[1], x.shape[1]
    cd = xp7.shape[3]
    base = tp // q
    rows = lambda s, hf: (base + s, 0)
    return pl.pallas_call(
        functools.partial(_ssd_sample_kernel, n_tok=n_tok),
        grid=(nsteps, 2),
        in_specs=[
            pl.BlockSpec((q, di), rows),
            pl.BlockSpec((1, n_tok + SSM_CONV - 1, nb, cd), lambda s, hf: (s, 0, 0, 0)),
            pl.BlockSpec((q, LANES), rows),
            pl.BlockSpec((half, di, SSM_STATE), lambda s, hf: (2 * s + hf, 0, 0)),
            pl.BlockSpec((q, d), lambda s, hf: (s, 0)),
            _const_spec(cw.shape), _const_spec(cb.shape), _const_spec(dtb.shape), _const_spec(alog.shape),
            _const_spec(dsk.shape), _const_spec(gnw.shape), _const_spec(e01.shape), _const_spec(g1.shape),
            _const_spec(wout.shape),
        ],
        out_specs=[
            pl.BlockSpec((q, d), lambda s, hf: (s, 0)),
            pl.BlockSpec((half, di, SSM_STATE), lambda s, hf: (2 * s + hf, 0, 0)),
        ],
        out_shape=[jax.ShapeDtypeStruct((nsteps * q, d), F32),
                   jax.ShapeDtypeStruct(s0.shape, F32)],
        scratch_shapes=[pltpu.VMEM((q, di), F32)],
        compiler_params=_cparams(("arbitrary", "arbitrary")),
        name="ssd_sample",
    )(z, xp7, dt, s0, x, cw, cb, dtb, alog, dsk, gnw, e01, g1, wout)


def _qkv_math(h, kvn_ref, an_ref, wkv_ref, wq_ref, kn_ref, qn_ref, rc_ref, rs1_ref, rs2_ref,
              hsum_ref, rq_ref, eq_ref, q_ref, k_ref, v_ref):
    hn = _rms_scale(h)
    kvw = k_ref.shape[1]
    kv = jnp.dot((hn * kvn_ref[...]).astype(BF16), wkv_ref[...], preferred_element_type=F32)
    k = kv[:, :kvw]
    v_ref[...] = kv[:, kvw:]
    inv_hd = 1.0 / ATT_HEAD_DIM
    k = k * lax.rsqrt(_mm01(k * k, hsum_ref[...], 2) * inv_hd + EPS) * kn_ref[...]
    rc, rs1, rs2 = rc_ref[...], rs1_ref[...], rs2_ref[...]
    rk = kvw // LANES
    k_ref[...] = _rope(k, _tile_lanes(rc, rk), _tile_lanes(rs1, rk), _tile_lanes(rs2, rk))
    q = jnp.dot((hn * an_ref[...]).astype(BF16), wq_ref[...], preferred_element_type=F32)
    rsq = lax.rsqrt(_mm01(q * q, rq_ref[...], 2) * inv_hd + EPS)
    q = q * _mm01(rsq, eq_ref[...], 2) * qn_ref[...]
    rq = q.shape[1] // LANES
    q_ref[...] = _rope(q, _tile_lanes(rc, rq), _tile_lanes(rs1, rq), _tile_lanes(rs2, rq))


def _attn_prompt_kernel(sink_ref, q_ref, kc_ref, kp_ref, vc_ref, vp_ref, o_ref):
    i = pl.program_id(1)
    w = WINDOW
    hd = ATT_HEAD_DIM
    n_q = q_ref.shape[1] // hd
    grp = n_q // ATT_KV_HEADS
    pair = LANES // hd
    rows = pair * w
    row = lax.broadcasted_iota(I32, (rows, 2 * w), 0) % w
    col = lax.broadcasted_iota(I32, (rows, 2 * w), 1)
    dist = row + w - col
    mask = (dist >= 0) & (dist < w) & ((col >= w) | (i > 0))
    first_head = lax.broadcasted_iota(I32, (rows, 1), 0) < w
    lo_lanes = lax.broadcasted_iota(I32, (w, LANES), 1) < hd
    zeros_kv = jnp.zeros((2 * w, hd), BF16)
    outs = []
    for g in range(ATT_KV_HEADS):
        sl = slice(g * hd, (g + 1) * hd)
        kk = jnp.concatenate([kp_ref[:, sl], kc_ref[:, sl]], axis=0).astype(BF16)
        vv = jnp.concatenate([vp_ref[:, sl], vc_ref[:, sl]], axis=0).astype(BF16)
        k2 = jnp.concatenate([kk, kk], axis=1)
        v_lo = jnp.concatenate([vv, zeros_kv], axis=1)
        v_hi = jnp.concatenate([zeros_kv, vv], axis=1)
        for h0 in range(g * grp, (g + 1) * grp, pair):
            qp = q_ref[:, h0 * hd:(h0 + pair) * hd] * (hd ** -0.5)
            qs = jnp.concatenate([jnp.where(lo_lanes, qp, 0.0), jnp.where(lo_lanes, 0.0, qp)],
                                 axis=0).astype(BF16)
            s = jnp.where(mask, _dot_nt(qs, k2), -jnp.inf)
            sink = jnp.where(first_head, sink_ref[h0], sink_ref[h0 + 1])
            m = jnp.maximum(jnp.max(s, axis=-1, keepdims=True), sink)
            p = jnp.exp(s - m)
            inv = 1.0 / (jnp.sum(p, axis=-1, keepdims=True) + jnp.exp(sink - m))
            pb = p.astype(BF16)
            outs.append(jnp.dot(pb[0:w], v_lo, preferred_element_type=F32) * inv[0:w]
                        + jnp.dot(pb[w:2 * w], v_hi, preferred_element_type=F32) * inv[w:2 * w])
    o_ref[...] = jnp.concatenate(outs, axis=1)


def _attn_prompt(sinks, q, k, v, bp, seq):
    w = WINDOW
    nb = seq // w
    qw, kvw = q.shape[1], k.shape[1]
    cur = lambda b, i: (b * nb + i, 0)
    prev = lambda b, i: (b * nb + jnp.maximum(i - 1, 0), 0)
    return pl.pallas_call(
        _attn_prompt_kernel,
        grid=(bp, nb),
        in_specs=[pl.BlockSpec(memory_space=pltpu.SMEM),
                  pl.BlockSpec((w, qw), cur), pl.BlockSpec((w, kvw), cur), pl.BlockSpec((w, kvw), prev),
                  pl.BlockSpec((w, kvw), cur), pl.BlockSpec((w, kvw), prev)],
        out_specs=pl.BlockSpec((w, qw), cur),
        out_shape=jax.ShapeDtypeStruct((bp * seq, qw), F32),
        compiler_params=_cparams(("arbitrary", "arbitrary")),
        name="attn_prompt",
    )(sinks, q, k, k, v, v)


def _attn_sample_kernel(sink_ref, q_ref, kn_ref, vn_ref, kc_ref, vc_ref, o_ref, kw_ref, vw_ref, *, n_tok):
    nb = SEQ_PER_STEP
    qn = n_tok * nb
    hd = ATT_HEAD_DIM
    lc = kc_ref.shape[1]
    for new_ref, old_ref, win_ref in ((kn_ref, kc_ref, kw_ref), (vn_ref, vc_ref, vw_ref)):
        new = new_ref[...]
        for sq in range(nb):
            win_ref[sq, 0:lc - n_tok, :] = old_ref[sq, n_tok:lc, :]
            win_ref[sq, lc - n_tok:lc, :] = jnp.concatenate(
                [new[t * nb + sq:t * nb + sq + 1, :] for t in range(n_tok)], axis=0)
    n_q = q_ref.shape[1] // hd
    grp = n_q // ATT_KV_HEADS
    rows = grp * qn
    r = lax.broadcasted_iota(I32, (rows, 1), 0)
    r_seq = r % nb
    r_tok = (r % qn) // nb
    ccol = lax.broadcasted_iota(I32, (rows, lc), 1)
    mask_c = ccol >= r_tok + 1 + (lc - WINDOW)
    ncol = lax.broadcasted_iota(I32, (rows, LANES), 1)
    mask_n = (ncol < qn) & (ncol % nb == r_seq) & (ncol // nb <= r_tok)
    q = q_ref[...]
    zpad = jnp.zeros((LANES - qn, hd), F32)
    outs = [None] * n_q
    for g in range(ATT_KV_HEADS):
        sl = slice(g * hd, (g + 1) * hd)
        qs = jnp.concatenate([q[:, (g * grp + hq) * hd:(g * grp + hq + 1) * hd] for hq in range(grp)],
                             axis=0).astype(BF16)
        k_new = jnp.concatenate([kn_ref[:, sl], zpad], axis=0).astype(BF16)
        v_new = jnp.concatenate([vn_ref[:, sl], zpad], axis=0).astype(BF16)
        s_n = jnp.where(mask_n, _dot_nt(qs, k_new) * (hd ** -0.5), -jnp.inf)
        s_c = jnp.zeros((rows, lc), F32)
        for sq in range(nb):
            s_sq = _dot_nt(qs, kc_ref[sq, :, sl].astype(BF16))
            s_c = jnp.where(r_seq == sq, s_sq, s_c)
        s_c = jnp.where(mask_c, s_c * (hd ** -0.5), -jnp.inf)
        sink = jnp.zeros((rows, 1), F32)
        for hq in range(grp):
            sink = jnp.where(r // qn == hq, sink_ref[g * grp + hq], sink)
        m = jnp.maximum(jnp.maximum(jnp.max(s_c, axis=-1, keepdims=True),
                                    jnp.max(s_n, axis=-1, keepdims=True)), sink)
        p_c = jnp.exp(s_c - m)
        p_n = jnp.exp(s_n - m)
        denom = (jnp.sum(p_c, axis=-1, keepdims=True) + jnp.sum(p_n, axis=-1, keepdims=True)
                 + jnp.exp(sink - m))
        p_c = p_c / denom
        o = jnp.dot((p_n / denom).astype(BF16), v_new, preferred_element_type=F32)
        for sq in range(nb):
            o = o + jnp.dot(jnp.where(r_seq == sq, p_c, 0.0).astype(BF16), vc_ref[sq, :, sl].astype(BF16),
                            preferred_element_type=F32)
        for hq in range(grp):
            outs[g * grp + hq] = o[hq * qn:(hq + 1) * qn, :]
    o_ref[...] = jnp.concatenate(outs, axis=1)


def _attn_sample(sinks, q, k, v, kc, vc, tp, n_tok):
    nb = SEQ_PER_STEP
    qn = n_tok * nb
    nsteps = kc.shape[0] // nb
    lc, kvw = kc.shape[1], kc.shape[2]
    qw = q.shape[1]
    base = tp // qn
    rows = lambda s: (base + s, 0)
    return pl.pallas_call(
        functools.partial(_attn_sample_kernel, n_tok=n_tok),
        grid=(nsteps,),
        in_specs=[pl.BlockSpec(memory_space=pltpu.SMEM),
                  pl.BlockSpec((qn, qw), rows), pl.BlockSpec((qn, kvw), rows), pl.BlockSpec((qn, kvw), rows),
                  pl.BlockSpec((nb, lc, kvw), lambda s: (s, 0, 0)), pl.BlockSpec((nb, lc, kvw), lambda s: (s, 0, 0))],
        out_specs=[pl.BlockSpec((qn, qw), lambda s: (s, 0)),
                   pl.BlockSpec((nb, lc, kvw), lambda s: (s, 0, 0)), pl.BlockSpec((nb, lc, kvw), lambda s: (s, 0, 0))],
        out_shape=[jax.ShapeDtypeStruct((nsteps * qn, qw), F32),
                   jax.ShapeDtypeStruct(kc.shape, F32), jax.ShapeDtypeStruct(vc.shape, F32)],
        compiler_params=_cparams(("arbitrary",)),
        name="attn_sample",
    )(sinks, q, k, v, kc, vc)


def _router_kernel(hp_ref, hs_ref, fn_ref, wh_ref, wl_ref, b_ref, triu_ref, xn_ref, info_ref, info_t_ref, cnt_ref,
                   carry_sc, *, n_p):
    h = jnp.where(pl.program_id(0) < n_p, hp_ref[...], hs_ref[...])
    _router_math(h, fn_ref, wh_ref, wl_ref, b_ref, triu_ref, xn_ref, info_ref, info_t_ref, cnt_ref, carry_sc)


def _wo_router_kernel(op_ref, os_ref, res_ref, wo_ref, fn_ref, wh_ref, wl_ref, b_ref, triu_ref,
                      h_ref, xn_ref, info_ref, info_t_ref, cnt_ref, carry_sc, *, n_p):
    o = jnp.where(pl.program_id(0) < n_p, op_ref[...], os_ref[...])
    h = res_ref[...] + jnp.dot(o.astype(BF16), wo_ref[...], preferred_element_type=F32)
    h_ref[...] = h
    _router_math(h, fn_ref, wh_ref, wl_ref, b_ref, triu_ref, xn_ref, info_ref, info_t_ref, cnt_ref, carry_sc)


def _router_math(h, fn_ref, wh_ref, wl_ref, b_ref, triu_ref, xn_ref, info_ref, info_t_ref, cnt_ref, carry_sc):
    i = pl.program_id(0)

    @pl.when(i == 0)
    def _():
        carry_sc[...] = jnp.zeros(carry_sc.shape, F32)

    xn = _rms_scale(h) * fn_ref[...]
    _store_token_major(xn_ref, xn)
    x_hi, x_lo = _split_bf16(xn, 2)
    wh, wl = wh_ref[...], wl_ref[...]
    logits = (jnp.dot(x_hi, wh, preferred_element_type=F32) + jnp.dot(x_hi, wl, preferred_element_type=F32)
              + jnp.dot(x_lo, wh, preferred_element_type=F32)) + b_ref[...]
    tm = logits.shape[0]
    lt = logits.T
    per = MOE_EXPERTS_PER_GROUP
    row = lax.broadcasted_iota(I32, (per, tm), 0).astype(F32)
    big = float(LANES)
    neg = -jnp.inf

    lg = jnp.where(row < MOE_GROUPS, lt[MOE_EXPERTS:MOE_EXPERTS + per, :], neg)
    mg = jnp.max(lg, axis=0, keepdims=True)
    gp = 1.0 / jnp.sum(jnp.exp(lg - mg), axis=0, keepdims=True)
    gi = jnp.min(jnp.where(lg == mg, row, big), axis=0, keepdims=True)

    le = lt[0:per, :]
    for g in range(1, MOE_GROUPS):
        le = jnp.where(gi == g, lt[g * per:(g + 1) * per, :], le)
    m1 = jnp.max(le, axis=0, keepdims=True)
    i1 = jnp.min(jnp.where(le == m1, row, big), axis=0, keepdims=True)
    le2 = jnp.where(row == i1, neg, le)
    m2 = jnp.max(le2, axis=0, keepdims=True)
    i2 = jnp.min(jnp.where(le2 == m2, row, big), axis=0, keepdims=True)
    e2 = jnp.exp(m2 - m1)
    g1 = gp * (1.0 / (1.0 + e2))
    g2 = gp * (e2 / (1.0 + e2))
    x1 = gi * per + i1
    x2 = gi * per + i2

    row_e = lax.broadcasted_iota(I32, (LANES, tm), 0).astype(F32)
    a1 = row_e == x1
    a2 = row_e == x2
    onehot = jnp.where(a1 | a2, 1.0, 0.0)
    before = jnp.dot(onehot.astype(BF16), triu_ref[...], preferred_element_type=F32) + carry_sc[:, 0:1]
    r1 = jnp.sum(jnp.where(a1, before, 0.0), axis=0, keepdims=True)
    r2 = jnp.sum(jnp.where(a2, before, 0.0), axis=0, keepdims=True)
    carry_sc[...] = carry_sc[...] + jnp.sum(onehot, axis=1, keepdims=True)
    cnt_ref[...] = carry_sc[...]

    info_t = jnp.zeros((SUBLANES, tm), F32)
    field = lax.broadcasted_iota(I32, (SUBLANES, tm), 0)
    for k, val in enumerate((x1, x2, g1, g2, r1, r2)):
        info_t = jnp.where(field == k, val, info_t)
    info_t_ref[...] = info_t
    info_ref[...] = jnp.concatenate([info_t, jnp.zeros((LANES - SUBLANES, tm), F32)], axis=0).T


def _two_source_specs(tm, width, n_p):
    return [pl.BlockSpec((tm, width), lambda i: (jnp.minimum(i, n_p - 1), 0)),
            pl.BlockSpec((tm, width), lambda i: (jnp.maximum(i - n_p, 0), 0))]


def _router(h_p, h_s, fn, wh, wl, b, tril):
    d = h_p.shape[1]
    tm = ROUTER_TILE
    n_p = h_p.shape[0] // tm
    t = h_p.shape[0] + h_s.shape[0]
    rows = lambda i: (i, 0)
    return pl.pallas_call(
        functools.partial(_router_kernel, n_p=n_p),
        grid=(t // tm,),
        in_specs=_two_source_specs(tm, d, n_p) + [_const_spec(a.shape) for a in (fn, wh, wl, b, tril)],
        out_specs=[pl.BlockSpec((tm * ROW_TILES, LANES), rows), pl.BlockSpec((tm, LANES), rows),
                   pl.BlockSpec((SUBLANES, tm), lambda i: (0, i)), _const_spec((LANES, LANES))],
        out_shape=[jax.ShapeDtypeStruct((t * ROW_TILES, LANES), F32), jax.ShapeDtypeStruct((t, LANES), F32),
                   jax.ShapeDtypeStruct((SUBLANES, t), F32), jax.ShapeDtypeStruct((LANES, LANES), F32)],
        scratch_shapes=[pltpu.VMEM((LANES, LANES), F32)],
        compiler_params=_cparams(("arbitrary",)),
        name="moe_router",
    )(h_p, h_s, fn, wh, wl, b, tril)


def _wo_router(o_p, o_s, res, wo, fn, wh, wl, b, tril):
    t, d = res.shape
    tm = ROUTER_TILE
    n_p = o_p.shape[0] // tm
    rows = lambda i: (i, 0)
    return pl.pallas_call(
        functools.partial(_wo_router_kernel, n_p=n_p),
        grid=(t // tm,),
        in_specs=_two_source_specs(tm, o_p.shape[1], n_p) + [pl.BlockSpec((tm, d), rows)]
        + [_const_spec(a.shape) for a in (wo, fn, wh, wl, b, tril)],
        out_specs=[pl.BlockSpec((tm, d), rows), pl.BlockSpec((tm * ROW_TILES, LANES), rows),
                   pl.BlockSpec((tm, LANES), rows), pl.BlockSpec((SUBLANES, tm), lambda i: (0, i)),
                   _const_spec((LANES, LANES))],
        out_shape=[jax.ShapeDtypeStruct((t, d), F32), jax.ShapeDtypeStruct((t * ROW_TILES, LANES), F32),
                   jax.ShapeDtypeStruct((t, LANES), F32), jax.ShapeDtypeStruct((SUBLANES, t), F32),
                   jax.ShapeDtypeStruct((LANES, LANES), F32)],
        scratch_shapes=[pltpu.VMEM((LANES, LANES), F32)],
        compiler_params=_cparams(("arbitrary",)),
        name="wo_router",
    )(o_p, o_s, res, wo, fn, wh, wl, b, tril)


def _store_token_major(ref, x):
    n = x.shape[0]
    for j in range(ROW_TILES):
        ref[pl.ds(j, n, stride=ROW_TILES), :] = x[:, j * LANES:(j + 1) * LANES]


def _load_token_major(ref, n):
    return jnp.concatenate([ref[pl.ds(j, n, stride=ROW_TILES), :] for j in range(ROW_TILES)], axis=1)


def _dest_kernel(pst_ref, info_ref, dest_ref):
    info = info_ref[...]
    e = info[0:2, :]
    start = jnp.zeros(e.shape, F32)
    for k in range(MOE_EXPERTS):
        start = jnp.where(e == k, pst_ref[k].astype(F32), start)
    dest = (start + info[4:6, :]).astype(I32)
    dest_ref[...] = jnp.concatenate([dest, jnp.zeros((SUBLANES - 2, dest.shape[1]), I32)], axis=0)


def _dest(pstarts, info_t):
    return pl.pallas_call(
        _dest_kernel,
        in_specs=[pl.BlockSpec(memory_space=pltpu.SMEM), pl.BlockSpec(memory_space=pltpu.VMEM)],
        out_specs=pl.BlockSpec(memory_space=pltpu.VMEM),
        out_shape=jax.ShapeDtypeStruct(info_t.shape, I32),
        name="moe_dest",
    )(pstarts, info_t)


def _tile_copy(src, src_row, dst, dst_row, sem):
    return pltpu.make_async_copy(src.at[pl.ds(pl.multiple_of(src_row * ROW_TILES, ROW_TILES), ROW_TILES)],
                                 dst.at[pl.ds(pl.multiple_of(dst_row * ROW_TILES, ROW_TILES), ROW_TILES)], sem)


def _wait_tiles(ref, n_tokens, sem):
    blk = ref.at[pl.ds(0, n_tokens * ROW_TILES)]
    pltpu.make_async_copy(blk, blk, sem).wait()


def _dispatch_kernel(d1_ref, d2_ref, pend_ref, pcnt_ref, xn_ref, xbuf_hbm, zero_sc, sem_z, sem):
    i = pl.program_id(0)
    tm = xn_ref.shape[0] // ROW_TILES
    blk_rows = zero_sc.shape[0]

    @pl.when(i == 0)
    def _():
        zero_sc[...] = jnp.zeros(zero_sc.shape, F32)

        def zero_copy(e):
            start = pl.multiple_of(pend_ref[e] * ROW_TILES - blk_rows, ROW_TILES)
            return pltpu.make_async_copy(zero_sc, xbuf_hbm.at[pl.ds(start, blk_rows)], sem_z)

        for e in range(MOE_EXPERTS):
            @pl.when(pcnt_ref[e] > 0)
            def _():
                zero_copy(e).start()
        first_unused = pend_ref[MOE_EXPERTS - 1] * ROW_TILES // blk_rows
        n_blocks = xbuf_hbm.shape[0] // blk_rows

        def tail_copy(b):
            dst = xbuf_hbm.at[pl.ds(pl.multiple_of(b * blk_rows, blk_rows), blk_rows)]
            return pltpu.make_async_copy(zero_sc, dst, sem_z)

        def tail_start(b, carry):
            tail_copy(b).start()
            return carry

        def tail_wait(b, carry):
            tail_copy(b).wait()
            return carry
        lax.fori_loop(first_unused, n_blocks, tail_start, 0)
        for e in range(MOE_EXPERTS):
            @pl.when(pcnt_ref[e] > 0)
            def _():
                zero_copy(e).wait()
        lax.fori_loop(first_unused, n_blocks, tail_wait, 0)

    def body(r, carry):
        t = i * tm + r
        _tile_copy(xn_ref, r, xbuf_hbm, d1_ref[t], sem).start(priority=0)
        _tile_copy(xn_ref, r, xbuf_hbm, d2_ref[t], sem).start(priority=1)
        return carry
    lax.fori_loop(0, tm, body, 0, unroll=8)
    _wait_tiles(xn_ref, tm, sem)
    _wait_tiles(xn_ref, tm, sem)


def _dispatch(dest1, dest2, pends, pcounts, xn_tm, n_rows):
    n_tok = dest1.shape[0]
    tm = DISPATCH_TILE
    grid_spec = pltpu.PrefetchScalarGridSpec(
        num_scalar_prefetch=4,
        grid=(n_tok // tm,),
        in_specs=[pl.BlockSpec((tm * ROW_TILES, LANES), lambda i, *_: (i, 0))],
        out_specs=pl.BlockSpec(memory_space=pl.ANY),
        scratch_shapes=[pltpu.VMEM((EXPERT_ROWS * ROW_TILES, LANES), F32), pltpu.SemaphoreType.DMA(()),
                        pltpu.SemaphoreType.DMA(())],
    )
    return pl.pallas_call(
        _dispatch_kernel,
        grid_spec=grid_spec,
        out_shape=jax.ShapeDtypeStruct((n_rows * ROW_TILES, LANES), F32),
        compiler_params=_cparams(("arbitrary",)),
        name="moe_dispatch",
    )(dest1, dest2, pends, pcounts, xn_tm)


def _expert_kernel(blk_e_ref, nvalid_ref, next_e_ref, x_hbm, wg_hbm, wu_hbm, wd_hbm, y_ref,
                   wg_sc, wu_sc, wd_sc, wg_st, wu_st, wd_st, x_sc, sem, xsem, *, layer):
    i = pl.program_id(0)
    nv = nvalid_ref[0]
    blk_rows = x_sc.shape[1]
    rows = blk_rows // ROW_TILES

    def weight_copies(e):
        return (pltpu.make_async_copy(wg_hbm.at[layer, e], wg_st, sem.at[0]),
                pltpu.make_async_copy(wu_hbm.at[layer, e], wu_st, sem.at[1]),
                pltpu.make_async_copy(wd_hbm.at[layer, e], wd_st, sem.at[2]))

    def row_copy(step):
        slot = step % EXPERT_RING
        src = x_hbm.at[pl.ds(pl.multiple_of(step * blk_rows, blk_rows), blk_rows)]
        return pltpu.make_async_copy(src, x_sc.at[slot], xsem.at[slot])

    @pl.when((i == 0) & (nv > 0))
    def _():
        for c in weight_copies(blk_e_ref[0]):
            c.start()
        for s in range(EXPERT_RING - 1):
            @pl.when(s < nv)
            def _():
                row_copy(s).start()

    @pl.when(i + EXPERT_RING - 1 < nv)
    def _():
        row_copy(i + EXPERT_RING - 1).start()

    @pl.when(i < nv)
    def _():
        e = blk_e_ref[i]
        e_prev = blk_e_ref[jnp.maximum(i - 1, 0)]
        row_copy(i).wait()
        x_ref = x_sc.at[i % EXPERT_RING]

        @pl.when((i == 0) | (e != e_prev))
        def _():
            for c in weight_copies(e):
                c.wait()
            wg_sc[...] = wg_st[...].astype(BF16)
            wu_sc[...] = wu_st[...].astype(BF16)
            wd_sc[...] = wd_st[...].astype(BF16)
            nxt = next_e_ref[e]

            @pl.when(nxt < MOE_EXPERTS)
            def _():
                for c in weight_copies(nxt):
                    c.start()

        x = _load_token_major(x_ref, rows).astype(BF16)
        hid = _silu(jnp.dot(x, wg_sc[...], preferred_element_type=F32)) * jnp.dot(
            x, wu_sc[...], preferred_element_type=F32)
        _store_token_major(y_ref, jnp.dot(hid.astype(BF16), wd_sc[...], preferred_element_type=F32))

    @pl.when(i >= nv)
    def _():
        y_ref[...] = jnp.zeros(y_ref.shape, F32)


def _experts(blk_e, nvalid, next_e, xbuf, wg, wu, wd, layer):
    nblk = blk_e.shape[0]
    d, hdim = wg.shape[2], wg.shape[3]
    rows = EXPERT_ROWS * ROW_TILES
    grid_spec = pltpu.PrefetchScalarGridSpec(
        num_scalar_prefetch=3,
        grid=(nblk,),
        in_specs=[pl.BlockSpec(memory_space=pl.ANY),
                  pl.BlockSpec(memory_space=pl.ANY), pl.BlockSpec(memory_space=pl.ANY),
                  pl.BlockSpec(memory_space=pl.ANY)],
        out_specs=pl.BlockSpec((rows, LANES), lambda i, be, nv, ne: (i, 0)),
        scratch_shapes=[pltpu.VMEM((d, hdim), BF16), pltpu.VMEM((d, hdim), BF16), pltpu.VMEM((hdim, d), BF16),
                        pltpu.VMEM((d, hdim), F32), pltpu.VMEM((d, hdim), F32), pltpu.VMEM((hdim, d), F32),
                        pltpu.VMEM((EXPERT_RING, rows, LANES), F32),
                        pltpu.SemaphoreType.DMA((3,)), pltpu.SemaphoreType.DMA((EXPERT_RING,))],
    )
    return pl.pallas_call(
        functools.partial(_expert_kernel, layer=layer),
        grid_spec=grid_spec,
        out_shape=jax.ShapeDtypeStruct((nblk * rows, LANES), F32),
        compiler_params=_cparams(("arbitrary",)),
        name="moe_experts",
    )(blk_e, nvalid, next_e, xbuf, wg, wu, wd)


def _gather_moe_rows(d1_ref, d2_ref, y_hbm, r_sc, sem, tm, row0=0):
    i = pl.program_id(0)
    n = pl.num_programs(0)

    def start(step):
        slot = step % 2
        base = row0 + step * tm

        def body(r, carry):
            _tile_copy(y_hbm, d1_ref[base + r], r_sc.at[slot, 0], r, sem.at[slot]).start(priority=0)
            _tile_copy(y_hbm, d2_ref[base + r], r_sc.at[slot, 1], r, sem.at[slot]).start(priority=1)
            return carry
        lax.fori_loop(0, tm, body, 0, unroll=8)

    @pl.when(i == 0)
    def _():
        start(i)

    @pl.when(i + 1 < n)
    def _():
        start(i + 1)

    slot = i % 2
    _wait_tiles(r_sc.at[slot, 0], tm, sem.at[slot])
    _wait_tiles(r_sc.at[slot, 1], tm, sem.at[slot])
    return _load_token_major(r_sc.at[slot, 0], tm), _load_token_major(r_sc.at[slot, 1], tm)


def _combine_kernel(d1_ref, d2_ref, h_ref, info_ref, y_hbm, o_ref, r_sc, sem, *, row0):
    y1, y2 = _gather_moe_rows(d1_ref, d2_ref, y_hbm, r_sc, sem, h_ref.shape[0], row0)
    info = info_ref[...]
    o_ref[...] = h_ref[...] + (y1 * info[:, 2:3] + y2 * info[:, 3:4])


def _combine(dest1, dest2, h, info, ybuf, row0, nrows):
    d = h.shape[1]
    tm = TOKEN_TILE
    base_tile = row0 // tm
    rows = lambda i, a, b: (base_tile + i, 0)
    grid_spec = pltpu.PrefetchScalarGridSpec(
        num_scalar_prefetch=2,
        grid=(nrows // tm,),
        in_specs=[pl.BlockSpec((tm, d), rows), pl.BlockSpec((tm, LANES), rows), pl.BlockSpec(memory_space=pl.ANY)],
        out_specs=pl.BlockSpec((tm, d), lambda i, a, b: (i, 0)),
        scratch_shapes=[pltpu.VMEM((2, 2, tm * ROW_TILES, LANES), F32), pltpu.SemaphoreType.DMA((2,))],
    )
    return pl.pallas_call(
        functools.partial(_combine_kernel, row0=row0),
        grid_spec=grid_spec,
        out_shape=jax.ShapeDtypeStruct((nrows, d), F32),
        compiler_params=_cparams(("arbitrary",)),
        name="moe_combine",
    )(dest1, dest2, h, info, ybuf)


def _combine_qkv_kernel(d1_ref, d2_ref, hp_ref, hs_ref, info_ref, y_hbm, kvn_ref, an_ref, wkv_ref, wq_ref, kn_ref,
                        qn_ref, rc_ref, rs1_ref, rs2_ref, hsum_ref, rq_ref, eq_ref,
                        h_ref, q_ref, k_ref, v_ref, r_sc, sem, *, n_p):
    y1, y2 = _gather_moe_rows(d1_ref, d2_ref, y_hbm, r_sc, sem, h_ref.shape[0])
    info = info_ref[...]
    h = jnp.where(pl.program_id(0) < n_p, hp_ref[...], hs_ref[...]) + (y1 * info[:, 2:3] + y2 * info[:, 3:4])
    h_ref[...] = h
    _qkv_math(h, kvn_ref, an_ref, wkv_ref, wq_ref, kn_ref, qn_ref, rc_ref, rs1_ref, rs2_ref,
              hsum_ref, rq_ref, eq_ref, q_ref, k_ref, v_ref)


def _combine_qkv(dest1, dest2, h_p, h_s, info, ybuf, kvn, an, wkv, wq, knt, qnt, rc, rs1, rs2, hsum, rq, eq):
    d = h_p.shape[1]
    tm = TOKEN_TILE
    n_p = h_p.shape[0] // tm
    t = h_p.shape[0] + h_s.shape[0]
    kvw = wkv.shape[1] // 2
    qw = wq.shape[1]
    rows = lambda i, a, b: (i, 0)
    const = lambda arr: pl.BlockSpec(arr.shape, lambda i, a, b: (0,) * arr.ndim)
    seq_tiles = (rc.shape[0] - h_s.shape[0]) // tm
    rope_rows = lambda i, a, b: (jnp.where(i < n_p, i % seq_tiles, seq_tiles + i - n_p), 0)
    grid_spec = pltpu.PrefetchScalarGridSpec(
        num_scalar_prefetch=2,
        grid=(t // tm,),
        in_specs=[pl.BlockSpec((tm, d), lambda i, a, b: (jnp.minimum(i, n_p - 1), 0)),
                  pl.BlockSpec((tm, d), lambda i, a, b: (jnp.maximum(i - n_p, 0), 0)),
                  pl.BlockSpec((tm, LANES), rows), pl.BlockSpec(memory_space=pl.ANY)]
        + [const(a) for a in (kvn, an, wkv, wq, knt, qnt)] + [pl.BlockSpec((tm, LANES), rope_rows)] * 3
        + [const(a) for a in (hsum, rq, eq)],
        out_specs=[pl.BlockSpec((tm, d), rows), pl.BlockSpec((tm, qw), rows), pl.BlockSpec((tm, kvw), rows),
                   pl.BlockSpec((tm, kvw), rows)],
        scratch_shapes=[pltpu.VMEM((2, 2, tm * ROW_TILES, LANES), F32), pltpu.SemaphoreType.DMA((2,))],
    )
    return pl.pallas_call(
        functools.partial(_combine_qkv_kernel, n_p=n_p),
        grid_spec=grid_spec,
        out_shape=[jax.ShapeDtypeStruct((t, d), F32), jax.ShapeDtypeStruct((t, qw), F32),
                   jax.ShapeDtypeStruct((t, kvw), F32), jax.ShapeDtypeStruct((t, kvw), F32)],
        compiler_params=_cparams(("arbitrary",)),
        name="combine_qkv",
    )(dest1, dest2, h_p, h_s, info, ybuf, kvn, an, wkv, wq, knt, qnt, rc, rs1, rs2, hsum, rq, eq)


def _router_weights(w_grp, b_grp, w_rt, b_rt):
    d = w_rt.shape[0]
    w_cat = jnp.zeros((d, LANES), F32).at[:, :MOE_EXPERTS].set(w_rt).at[:, MOE_EXPERTS:MOE_EXPERTS + MOE_GROUPS].set(w_grp)
    b_cat = jnp.zeros((1, LANES), F32).at[0, :MOE_EXPERTS].set(b_rt).at[0, MOE_EXPERTS:MOE_EXPERTS + MOE_GROUPS].set(b_grp)
    w_hi = w_cat.astype(BF16)
    w_lo = (w_cat - w_hi.astype(F32)).astype(BF16)
    return w_hi, w_lo, b_cat


def _moe_experts(xn_tm, info_t, cnt, wg, wu, wd, layer):
    t = info_t.shape[1]
    counts = cnt[:MOE_EXPERTS, 0].astype(I32)
    pcounts = (counts + EXPERT_ROWS - 1) // EXPERT_ROWS * EXPERT_ROWS
    pends = jnp.cumsum(pcounts)
    pstarts = pends - pcounts
    nblk = -(-(2 * t + MOE_EXPERTS * (EXPERT_ROWS - 1)) // EXPERT_ROWS)
    blk_start = jnp.arange(nblk, dtype=I32) * EXPERT_ROWS
    blk_e = jnp.minimum(jnp.sum((pends[None, :] <= blk_start[:, None]).astype(I32), axis=1), MOE_EXPERTS - 1)
    nvalid = pends[-1:] // EXPERT_ROWS
    eid = jnp.arange(MOE_EXPERTS, dtype=I32)
    later = (eid[None, :] > eid[:, None]) & (pcounts[None, :] > 0)
    next_e = jnp.min(jnp.where(later, eid[None, :], MOE_EXPERTS), axis=1).astype(I32)

    dest = _dest(pstarts, info_t)
    dest1, dest2 = dest[0], dest[1]
    xbuf = _dispatch(dest1, dest2, pends, pcounts, xn_tm, nblk * EXPERT_ROWS)
    return dest1, dest2, _experts(blk_e, nvalid, next_e, xbuf, wg, wu, wd, layer)


def _rope_tables(pos):
    half = ROT_DIM // 2
    inv = ROPE_THETA ** (-np.arange(0, ROT_DIM, 2, dtype=np.float64) / ROT_DIM)
    ang = pos.astype(np.float64)[:, None] * inv[None, :]
    cos, sin = np.cos(ang), np.sin(ang)
    n = pos.shape[0]
    ones = np.ones((n, ATT_HEAD_DIM - ROT_DIM))
    zeros_r = np.zeros((n, ATT_HEAD_DIM - ROT_DIM))
    zeros_h = np.zeros((n, half))
    c = np.concatenate([cos, cos, ones], axis=1)
    s1 = np.concatenate([-sin, zeros_h, zeros_r], axis=1)
    s2 = np.concatenate([zeros_h, sin, zeros_r], axis=1)
    reps = LANES // ATT_HEAD_DIM
    return tuple(jnp.asarray(np.tile(a, (1, reps)).astype(np.float32)) for a in (c, s1, s2))


def _mask01(m):
    return jnp.asarray(m.astype(np.float32), dtype=BF16)


def _to_step_order(a, nsteps, n_tok):
    c = a.shape[-1]
    return a.reshape(nsteps, SEQ_PER_STEP, n_tok, c).transpose(0, 2, 1, 3).reshape(nsteps * n_tok * SEQ_PER_STEP, c)


def _from_step_order(a, nsteps, n_tok):
    c = a.shape[-1]
    return a.reshape(nsteps, n_tok, SEQ_PER_STEP, c).transpose(0, 2, 1, 3).reshape(nsteps * SEQ_PER_STEP, n_tok, c)


def kernel(x_prompt, x_sample, state_ssm, state_conv, cache_k_win, cache_v_win, ssm_norm, ssm_w_in, ssm_conv_w, ssm_conv_b, ssm_dt_bias, ssm_a_log, ssm_d, ssm_gate_norm, ssm_w_out, kv_norm, w_kv, k_norm, attn_norm, w_q, q_norm, sinks, w_o, ffn_norm, moe_w_group, moe_b_group, moe_w_router, moe_b_router, moe_w_gate, moe_w_up, moe_w_down):
    bp, seq, d = x_prompt.shape
    bs, n_tok, _ = x_sample.shape
    tp, ts = bp * seq, bs * n_tok
    nsteps = bs // SEQ_PER_STEP
    n_heads = ssm_d.shape[1]
    di = n_heads * SSM_HEAD_DIM
    gn_w = SSM_GROUPS * SSM_STATE
    cdim = di + 2 * gn_w
    n_q = sinks.shape[1]
    kvw = ATT_KV_HEADS * ATT_HEAD_DIM

    xp2 = x_prompt.reshape(tp, d)
    xs2 = _to_step_order(x_sample, nsteps, n_tok)

    lane_i = np.arange(LANES)
    e01 = _mask01(lane_i[:, None] == (np.arange(di) // SSM_HEAD_DIM)[None, :])
    hpg = di // SSM_GROUPS
    g1 = ((np.arange(gn_w) // SSM_STATE)[:, None] == (lane_i // (hpg // SSM_HEAD_DIM))[None, :])
    g1 = _mask01(g1 & (lane_i < n_heads)[None, :])
    tril_c = _mask01(np.arange(SSM_CHUNK)[:, None] >= np.arange(SSM_CHUNK)[None, :])
    earlier_x = _mask01(np.arange(ROUTER_TILE)[:, None] < np.arange(ROUTER_TILE)[None, :])
    hsum = _mask01((np.arange(kvw) // ATT_HEAD_DIM)[:, None] == (np.arange(kvw) // ATT_HEAD_DIM)[None, :])
    qw = n_q * ATT_HEAD_DIM
    rq_np = (np.arange(qw) // ATT_HEAD_DIM)[:, None] == lane_i[None, :]
    rq, eq = _mask01(rq_np), _mask01(rq_np.T)

    w_in = ssm_w_in[0]
    wz = w_in[:, :di].astype(BF16)
    wx = w_in[:, di:di + cdim].astype(BF16)
    wd = jnp.zeros((d, LANES), F32).at[:, :n_heads].set(w_in[:, di + cdim:]).astype(BF16)
    cw, cb = ssm_conv_w[0], ssm_conv_b[0].reshape(1, cdim)
    z, xbc, dt = _inproj(xp2, xs2, ssm_norm[0].reshape(1, d), wz, wx, wd)

    pad_h = lambda v: jnp.zeros((1, LANES), F32).at[0, :n_heads].set(v)
    dtb, alog = pad_h(ssm_dt_bias[0]), pad_h(ssm_a_log[0])
    dsk = jnp.repeat(ssm_d[0], SSM_HEAD_DIM).reshape(1, di)
    gnw = ssm_gate_norm[0].reshape(1, di)

    w_out = ssm_w_out[0].astype(BF16)
    h_p, s_fin, c_fin = _ssd_prompt(z, xbc, dt, xp2, cw, cb, dtb, alog, dsk, gnw, tril_c, e01, w_out, bp, seq)
    ssm_p = s_fin.reshape(1, bp, n_heads, SSM_HEAD_DIM, SSM_STATE)
    conv_p = c_fin[:, SUBLANES - (SSM_CONV - 1):, :].reshape(1, bp, SSM_CONV - 1, cdim)

    xbc_s = xbc[tp:].reshape(nsteps, n_tok, SEQ_PER_STEP, cdim)
    conv_in = state_conv[0].reshape(nsteps, SEQ_PER_STEP, SSM_CONV - 1, cdim).transpose(0, 2, 1, 3)
    xp7 = jnp.concatenate([conv_in, xbc_s], axis=1)
    s0 = state_ssm[0].reshape(bs, di, SSM_STATE)
    h_s, s_new = _ssd_sample(z, xp7, dt, s0, xs2, cw, cb, dtb, alog, dsk, gnw, e01, g1, w_out, tp, n_tok)
    ssm_s = s_new.reshape(1, bs, n_heads, SSM_HEAD_DIM, SSM_STATE)
    conv_s = _from_step_order(xbc[tp:], nsteps, n_tok)[:, n_tok - (SSM_CONV - 1):, :].reshape(
        1, bs, SSM_CONV - 1, cdim)

    w_hi, w_lo, b_cat = _router_weights(moe_w_group[0], moe_b_group[0], moe_w_router[0], moe_b_router[0])
    xn_tm, info, info_t, cnt = _router(h_p, h_s, ffn_norm[0].reshape(1, d), w_hi, w_lo, b_cat, earlier_x)
    dest1, dest2, ybuf = _moe_experts(xn_tm, info_t, cnt, moe_w_gate, moe_w_up, moe_w_down, 0)

    pos = np.concatenate([np.arange(seq), np.tile(np.repeat(PAST_LEN + np.arange(n_tok), SEQ_PER_STEP), nsteps)])
    rc, rs1, rs2 = _rope_tables(pos)
    h1, q, k, v = _combine_qkv(dest1, dest2, h_p, h_s, info, ybuf, kv_norm.reshape(1, d), attn_norm[0].reshape(1, d),
                               w_kv.astype(BF16), w_q[0].astype(BF16),
                               jnp.tile(k_norm, ATT_KV_HEADS).reshape(1, kvw), jnp.tile(q_norm[0], n_q).reshape(1, qw),
                               rc, rs1, rs2, hsum, rq, eq)
    sk = sinks[0]
    lc = cache_k_win.shape[1]
    kc = cache_k_win.reshape(bs, lc, kvw)
    vc = cache_v_win.reshape(bs, lc, kvw)
    o_p = _attn_prompt(sk, q, k, v, bp, seq)
    o_s, k_win, v_win = _attn_sample(sk, q, k, v, kc, vc, tp, n_tok)
    w_hi, w_lo, b_cat = _router_weights(moe_w_group[1], moe_b_group[1], moe_w_router[1], moe_b_router[1])
    h2, xn_tm, info, info_t, cnt = _wo_router(o_p, o_s, h1, w_o[0].astype(BF16), ffn_norm[1].reshape(1, d),
                                             w_hi, w_lo, b_cat, earlier_x)
    dest1, dest2, ybuf = _moe_experts(xn_tm, info_t, cnt, moe_w_gate, moe_w_up, moe_w_down, 1)
    y_p = _combine(dest1, dest2, h2, info, ybuf, 0, tp)
    y_s = _combine(dest1, dest2, h2, info, ybuf, tp, ts)

    wl = min(WINDOW, seq)
    k_p = k[:tp].reshape(bp, seq, kvw)[:, seq - wl:].reshape(bp, wl, ATT_KV_HEADS, ATT_HEAD_DIM)
    v_p = v[:tp].reshape(bp, seq, kvw)[:, seq - wl:].reshape(bp, wl, ATT_KV_HEADS, ATT_HEAD_DIM)
    k_s = k_win.reshape(bs, lc, ATT_KV_HEADS, ATT_HEAD_DIM)
    v_s = v_win.reshape(bs, lc, ATT_KV_HEADS, ATT_HEAD_DIM)
    return (y_p.reshape(bp, seq, d), _from_step_order(y_s, nsteps, n_tok),
            ssm_p, conv_p, k_p, v_p, ssm_s, conv_s, k_s, v_s)
```

```python
import functools

import numpy as np
import jax
import jax.numpy as jnp
from jax import lax
from jax.experimental import pallas as pl
from jax.experimental.pallas import tpu as pltpu

F32 = jnp.float32
BF16 = jnp.bfloat16
I32 = jnp.int32

EPS = 1e-6
SSM_HEAD_DIM = 64
SSM_GROUPS = 4
SSM_STATE = 128
SSM_CONV = 4
SSM_CHUNK = 128
ATT_HEAD_DIM = 64
ATT_KV_HEADS = 4
WINDOW = 128
ROT_DIM = ATT_HEAD_DIM // 4
ROPE_THETA = 500000.0
MOE_GROUPS = 4
MOE_EXPERTS_PER_GROUP = 8
MOE_EXPERTS = MOE_GROUPS * MOE_EXPERTS_PER_GROUP
MOE_BLOCK = 128
PAST_LEN = 16384

LANES = 128
SUBLANES = 8
BF16_ROWS = 16
SEQ_PER_STEP = SUBLANES
TOKEN_TILE = 256
ROUTER_TILE = 512
INPROJ_TILE = 512
ROW_TILES = 8
DISPATCH_TILE = 1536
EXPERT_RING = 3
EXPERT_ROWS = 512
VMEM_LIMIT = 56 * 1024 * 1024


def _cparams(sem):
    return pltpu.CompilerParams(dimension_semantics=sem, vmem_limit_bytes=VMEM_LIMIT)


def _const_spec(shape):
    nd = len(shape)
    return pl.BlockSpec(shape, lambda *_: (0,) * nd)


def _split_bf16(v, n):
    parts = []
    r = v
    for k in range(n):
        p = r.astype(BF16)
        parts.append(p)
        if k + 1 < n:
            r = r - p.astype(F32)
    return parts


def _mm01(v, m01, n=3):
    acc = None
    for p in _split_bf16(v, n):
        d = jnp.dot(p, m01, preferred_element_type=F32)
        acc = d if acc is None else acc + d
    return acc


def _mm01_left(m01, v, n=3):
    acc = None
    for p in _split_bf16(v, n):
        d = jnp.dot(m01, p, preferred_element_type=F32)
        acc = d if acc is None else acc + d
    return acc


def _dot_nt(a, b):
    return lax.dot_general(a, b, (((1,), (1,)), ((), ())), preferred_element_type=F32)


def _sigmoid(x):
    return 0.5 * jnp.tanh(0.5 * x) + 0.5


def _silu(x):
    return x * _sigmoid(x)


def _softplus(x):
    return jnp.maximum(x, 0.0) + jnp.log1p(jnp.exp(-jnp.abs(x)))


def _rms_scale(x):
    return x * lax.rsqrt(jnp.mean(x * x, axis=-1, keepdims=True) + EPS)


def _gate_norm(y, z, gn, n_groups):
    yz = y * _silu(z)
    w = yz.shape[-1] // n_groups
    outs = []
    for g in range(n_groups):
        v = yz[:, g * w:(g + 1) * w]
        outs.append(_rms_scale(v) * gn[:, g * w:(g + 1) * w])
    return jnp.concatenate(outs, axis=1)


def _rope(x, c, s1, s2):
    w = x.shape[-1]
    return x * c + pltpu.roll(x, w - ROT_DIM // 2, 1) * s1 + pltpu.roll(x, ROT_DIM // 2, 1) * s2


def _tile_lanes(t, reps):
    return t if reps == 1 else jnp.concatenate([t] * reps, axis=1)


def _inproj_kernel(xp_ref, xs_ref, g_ref, wz_ref, wx_ref, wd_ref, z_ref, xbc_ref, dt_ref, *, n_p):
    i = pl.program_id(0)
    x = jnp.where(i < n_p, xp_ref[...], xs_ref[...])
    xn = (_rms_scale(x) * g_ref[...]).astype(BF16)
    z_ref[...] = jnp.dot(xn, wz_ref[...], preferred_element_type=F32)
    xbc_ref[...] = jnp.dot(xn, wx_ref[...], preferred_element_type=F32)
    dt_ref[...] = jnp.dot(xn, wd_ref[...], preferred_element_type=F32)


def _inproj(xp2, xs2, g, wz, wx, wd):
    tp, d = xp2.shape
    ts = xs2.shape[0]
    tm = INPROJ_TILE
    n_p, n_s = tp // tm, ts // tm
    t = tp + ts
    return pl.pallas_call(
        functools.partial(_inproj_kernel, n_p=n_p),
        grid=(n_p + n_s,),
        in_specs=_two_source_specs(tm, d, n_p) + [_const_spec(a.shape) for a in (g, wz, wx, wd)],
        out_specs=[
            pl.BlockSpec((tm, wz.shape[1]), lambda i: (i, 0)),
            pl.BlockSpec((tm, wx.shape[1]), lambda i: (i, 0)),
            pl.BlockSpec((tm, wd.shape[1]), lambda i: (i, 0)),
        ],
        out_shape=[jax.ShapeDtypeStruct((t, wz.shape[1]), F32),
                   jax.ShapeDtypeStruct((t, wx.shape[1]), F32),
                   jax.ShapeDtypeStruct((t, wd.shape[1]), F32)],
        compiler_params=_cparams(("arbitrary",)),
        name="inproj",
    )(xp2, xs2, g, wz, wx, wd)


def _ssd_prompt_kernel(z_ref, xbc_ref, dt_ref, x_ref, cw_ref, cb_ref, dtb_ref, alog_ref, dsk_ref, gn_ref,
                       tril_ref, e_ref, wout_ref, h_ref, sfin_ref, cfin_ref, xpad_sc, st_sc):
    c = pl.program_id(1)
    q = SSM_CHUNK
    cd = xbc_ref.shape[1]
    di = z_ref.shape[1]
    gn_w = SSM_GROUPS * SSM_STATE
    hpg = di // SSM_GROUPS
    pad = SUBLANES

    n_slab = cd // LANES

    @pl.when(c == 0)
    def _():
        xpad_sc[:, 0:pad, :] = jnp.zeros((n_slab, pad, LANES), F32)
        st_sc[...] = jnp.zeros(st_sc.shape, F32)

    @pl.when(c > 0)
    def _():
        xpad_sc[:, 0:pad, :] = xpad_sc[:, q:q + pad, :]

    slabs = []
    for j in range(n_slab):
        ls = slice(j * LANES, (j + 1) * LANES)
        xpad_sc[j, pad:pad + q, :] = xbc_ref[:, ls]
        acc = cb_ref[:, ls]
        for k in range(SSM_CONV):
            off = pad - (SSM_CONV - 1) + k
            acc = acc + xpad_sc[j, off:off + q, :] * cw_ref[k:k + 1, ls]
        slabs.append(_silu(acc))
    xc = jnp.concatenate(slabs, axis=1)
    xs = xc[:, :di]
    bm = xc[:, di:di + gn_w]
    cm = xc[:, di + gn_w:]

    dt = _softplus(dt_ref[...] + dtb_ref[...])
    a = -jnp.exp(alog_ref[...])
    act = _mm01_left(tril_ref[...], dt * a)
    act_t = act.T
    act_last = act[q - 1:q, :]
    pieces = (_split_bf16(dt, 2) + _split_bf16(jnp.exp(act_last - act), 2) + _split_bf16(jnp.exp(act), 2)
              + _split_bf16(jnp.exp(act[q - BF16_ROWS:q, :]), 3))
    ex = jnp.dot(jnp.concatenate(pieces, axis=0), e_ref[...], preferred_element_type=F32)
    xdt = xs * (ex[0:q] + ex[q:2 * q])
    xd = xdt * (ex[2 * q:3 * q] + ex[3 * q:4 * q])
    eax = ex[4 * q:5 * q] + ex[5 * q:6 * q]
    tail = ex[6 * q:]
    cd = (tail[0:BF16_ROWS] + tail[BF16_ROWS:2 * BF16_ROWS]) + tail[2 * BF16_ROWS:]
    cdx = cd[BF16_ROWS - 1:BF16_ROWS, :]

    row = lax.broadcasted_iota(I32, (q, q), 0)
    col = lax.broadcasted_iota(I32, (q, q), 1)
    causal = row >= col
    lane = lax.broadcasted_iota(I32, (q, LANES), 1)
    lo_half = lane < SSM_HEAD_DIM

    y_parts = []
    heads_per_group = hpg // SSM_HEAD_DIM
    for g in range(SSM_GROUPS):
        cg = cm[:, g * SSM_STATE:(g + 1) * SSM_STATE].astype(BF16)
        bg = bm[:, g * SSM_STATE:(g + 1) * SSM_STATE]
        cb = _dot_nt(cg, bg.astype(BF16))
        st_g = st_sc[:, g * hpg:(g + 1) * hpg]
        y_off = jnp.dot(cg, st_g.astype(BF16), preferred_element_type=F32)
        for pr in range(heads_per_group // 2):
            h0 = g * heads_per_group + 2 * pr
            ms = []
            for h in (h0, h0 + 1):
                seg = act[:, h:h + 1] - act_t[h:h + 1, :]
                lm = jnp.exp(jnp.where(causal, seg, -jnp.inf))
                ms.append((cb * lm).astype(BF16))
            m2 = jnp.concatenate(ms, axis=1)
            xpair = xdt[:, h0 * SSM_HEAD_DIM:(h0 + 2) * SSM_HEAD_DIM]
            rhs = jnp.concatenate([jnp.where(lo_half, xpair, 0.0),
                                   jnp.where(lo_half, 0.0, xpair)], axis=0).astype(BF16)
            y_d = jnp.dot(m2, rhs, preferred_element_type=F32)
            lo = 2 * pr * SSM_HEAD_DIM
            y_parts.append(y_d + y_off[:, lo:lo + LANES] * eax[:, g * hpg + lo:g * hpg + lo + LANES])
        upd = jnp.dot(bg.T.astype(BF16), xd[:, g * hpg:(g + 1) * hpg].astype(BF16),
                      preferred_element_type=F32)
        st_sc[:, g * hpg:(g + 1) * hpg] = st_g * cdx[:, g * hpg:(g + 1) * hpg] + upd

    y = jnp.concatenate(y_parts, axis=1) + xs * dsk_ref[...]
    yg = _gate_norm(y, z_ref[...], gn_ref[...], SSM_GROUPS)
    h_ref[...] = x_ref[...] + jnp.dot(yg.astype(BF16), wout_ref[...], preferred_element_type=F32)

    @pl.when(c == pl.num_programs(1) - 1)
    def _():
        sfin_ref[0] = st_sc[...].T
        cfin_ref[0] = jnp.concatenate([xpad_sc[j, q:q + pad, :] for j in range(n_slab)], axis=1)


def _ssd_prompt(z, xbc, dt, x, cw, cb, dtb, alog, dsk, gnw, tril, e01, wout, bp, seq):
    nc = seq // SSM_CHUNK
    q = SSM_CHUNK
    di, cd, d = z.shape[1], xbc.shape[1], x.shape[1]
    rows = lambda b, c: (b * nc + c, 0)
    return pl.pallas_call(
        _ssd_prompt_kernel,
        grid=(bp, nc),
        in_specs=[
            pl.BlockSpec((q, di), rows), pl.BlockSpec((q, cd), rows), pl.BlockSpec((q, LANES), rows),
            pl.BlockSpec((q, d), rows),
            _const_spec(cw.shape), _const_spec(cb.shape), _const_spec(dtb.shape), _const_spec(alog.shape),
            _const_spec(dsk.shape), _const_spec(gnw.shape), _const_spec(tril.shape), _const_spec(e01.shape),
            _const_spec(wout.shape),
        ],
        out_specs=[
            pl.BlockSpec((q, d), rows),
            pl.BlockSpec((1, di, SSM_STATE), lambda b, c: (b, 0, 0)),
            pl.BlockSpec((1, SUBLANES, cd), lambda b, c: (b, 0, 0)),
        ],
        out_shape=[jax.ShapeDtypeStruct((bp * seq, d), F32),
                   jax.ShapeDtypeStruct((bp, di, SSM_STATE), F32),
                   jax.ShapeDtypeStruct((bp, SUBLANES, cd), F32)],
        scratch_shapes=[pltpu.VMEM((cd // LANES, q + 2 * SUBLANES, LANES), F32), pltpu.VMEM((SSM_STATE, di), F32)],
        compiler_params=_cparams(("arbitrary", "arbitrary")),
        name="ssd_prompt",
    )(z, xbc, dt, x, cw, cb, dtb, alog, dsk, gnw, tril, e01, wout)


def _ssd_sample_kernel(z_ref, xp_ref, dt_ref, s0_ref, x_ref, cw_ref, cb_ref, dtb_ref, alog_ref, dsk_ref, gn_ref,
                       e_ref, g1_ref, wout_ref, h_ref, sn_ref, yoff_sc, *, n_tok):
    hf = pl.program_id(1)
    nb = SEQ_PER_STEP
    half = nb // 2
    q = n_tok * nb
    di = z_ref.shape[1]
    gn_w = SSM_GROUPS * SSM_STATE
    hpg = di // SSM_GROUPS

    taps = [xp_ref[0, m] for m in range(n_tok + SSM_CONV - 1)]
    slabs = []
    for t in range(n_tok):
        acc = cb_ref[...]
        for k in range(SSM_CONV):
            acc = acc + taps[t + k] * cw_ref[k:k + 1, :]
        slabs.append(_silu(acc))
    xc = jnp.concatenate(slabs, axis=0)
    xs = xc[:, :di]
    bm = xc[:, di:di + gn_w]
    cm = xc[:, di + gn_w:]

    dt = _softplus(dt_ref[...] + dtb_ref[...])
    da = dt * (-jnp.exp(alog_ref[...]))
    acts = []
    run = None
    for t in range(n_tok):
        d = da[t * nb:(t + 1) * nb, :]
        run = d if run is None else run + d
        acts.append(run)
    act = jnp.concatenate(acts, axis=0)
    act_last = jnp.concatenate([acts[-1]] * n_tok, axis=0)
    e01 = e_ref[...]
    xdt = xs * _mm01(dt, e01)
    xd = xdt * _mm01(jnp.exp(act_last - act), e01)
    eax = _mm01(jnp.exp(act), e01)
    cdx = _mm01(jnp.exp(acts[-1]), e01)

    pairs = [(t, u) for t in range(n_tok) for u in range(t + 1)]
    cbp = jnp.concatenate([cm[t * nb:(t + 1) * nb, :] * bm[u * nb:(u + 1) * nb, :] for t, u in pairs], axis=0)
    seg = jnp.concatenate([acts[t] - acts[u] for t, u in pairs], axis=0)
    coef = _mm01(_mm01(cbp, g1_ref[...]) * jnp.exp(seg), e01)
    y_slabs = []
    for t in range(n_tok):
        acc = None
        for pi, (tt, u) in enumerate(pairs):
            if tt != t:
                continue
            term = coef[pi * nb:(pi + 1) * nb, :] * xdt[u * nb:(u + 1) * nb, :]
            acc = term if acc is None else acc + term
        y_slabs.append(acc)
    y_diag = jnp.concatenate(y_slabs, axis=0)

    zpad = jnp.concatenate([xd,
                            jnp.where(hf == 0, cdx[0:half, :], cdx[half:nb, :]),
                            jnp.zeros((LANES - q - half, di), F32)], axis=0)
    zt = zpad.T
    row_seq = lax.broadcasted_iota(I32, (q, 1), 0) % nb
    cm_b = cm.astype(BF16)
    y_off_g = [None] * SSM_GROUPS
    for sl in range(half):
        in_seq = row_seq == hf * half + sl
        for g in range(SSM_GROUPS):
            s_old = s0_ref[sl, g * hpg:(g + 1) * hpg, :]
            c_g = jnp.where(in_seq, cm_b[:, g * SSM_STATE:(g + 1) * SSM_STATE], jnp.zeros((), BF16))
            yo = _dot_nt(c_g, s_old.astype(BF16))
            y_off_g[g] = yo if y_off_g[g] is None else y_off_g[g] + yo
            b_g = jnp.where(in_seq, bm[:, g * SSM_STATE:(g + 1) * SSM_STATE], 0.0)
            b_pad = jnp.concatenate([b_g, jnp.zeros((LANES - q, SSM_STATE), F32)], axis=0).astype(BF16)
            zt_g = zt[g * hpg:(g + 1) * hpg, :]
            upd = jnp.dot(zt_g.astype(BF16), b_pad, preferred_element_type=F32)
            decay = zt_g[:, q + sl:q + sl + 1]
            sn_ref[sl, g * hpg:(g + 1) * hpg, :] = s_old * decay + upd
    y_off = jnp.concatenate(y_off_g, axis=1)

    @pl.when(hf == 0)
    def _():
        yoff_sc[...] = y_off

    @pl.when(hf == 1)
    def _():
        y = y_diag + (yoff_sc[...] + y_off) * eax + xs * dsk_ref[...]
        yg = _gate_norm(y, z_ref[...], gn_ref[...], SSM_GROUPS)
        h_ref[...] = x_ref[...] + jnp.dot(yg.astype(BF16), wout_ref[...], preferred_element_type=F32)


def _ssd_sample(z, xp7, dt, s0, x, cw, cb, dtb, alog, dsk, gnw, e01, g1, wout, tp, n_tok):
    nsteps = xp7.shape[0]
    nb = SEQ_PER_STEP
    half = nb // 2
    q = n_tok * nb
    di, d = z.shape[1], x.shape[1]
    cd = xp7.shape[3]
    base = tp // q
    rows = lambda s, hf: (base + s, 0)
    return pl.pallas_call(
        functools.partial(_ssd_sample_kernel, n_tok=n_tok),
        grid=(nsteps, 2),
        in_specs=[
            pl.BlockSpec((q, di), rows),
            pl.BlockSpec((1, n_tok + SSM_CONV - 1, nb, cd), lambda s, hf: (s, 0, 0, 0)),
            pl.BlockSpec((q, LANES), rows),
            pl.BlockSpec((half, di, SSM_STATE), lambda s, hf: (2 * s + hf, 0, 0)),
            pl.BlockSpec((q, d), lambda s, hf: (s, 0)),
            _const_spec(cw.shape), _const_spec(cb.shape), _const_spec(dtb.shape), _const_spec(alog.shape),
            _const_spec(dsk.shape), _const_spec(gnw.shape), _const_spec(e01.shape), _const_spec(g1.shape),
            _const_spec(wout.shape),
        ],
        out_specs=[
            pl.BlockSpec((q, d), lambda s, hf: (s, 0)),
            pl.BlockSpec((half, di, SSM_STATE), lambda s, hf: (2 * s + hf, 0, 0)),
        ],
        out_shape=[jax.ShapeDtypeStruct((nsteps * q, d), F32),
                   jax.ShapeDtypeStruct(s0.shape, F32)],
        scratch_shapes=[pltpu.VMEM((q, di), F32)],
        compiler_params=_cparams(("arbitrary", "arbitrary")),
        name="ssd_sample",
    )(z, xp7, dt, s0, x, cw, cb, dtb, alog, dsk, gnw, e01, g1, wout)


def _qkv_math(h, kvn_ref, an_ref, wkv_ref, wq_ref, kn_ref, qn_ref, rc_ref, rs1_ref, rs2_ref,
              hsum_ref, rq_ref, eq_ref, q_ref, k_ref, v_ref):
    hn = _rms_scale(h)
    kvw = k_ref.shape[1]
    kv = jnp.dot((hn * kvn_ref[...]).astype(BF16), wkv_ref[...], preferred_element_type=F32)
    k = kv[:, :kvw]
    v_ref[...] = kv[:, kvw:]
    inv_hd = 1.0 / ATT_HEAD_DIM
    k = k * lax.rsqrt(_mm01(k * k, hsum_ref[...], 2) * inv_hd + EPS) * kn_ref[...]
    rc, rs1, rs2 = rc_ref[...], rs1_ref[...], rs2_ref[...]
    rk = kvw // LANES
    k_ref[...] = _rope(k, _tile_lanes(rc, rk), _tile_lanes(rs1, rk), _tile_lanes(rs2, rk))
    q = jnp.dot((hn * an_ref[...]).astype(BF16), wq_ref[...], preferred_element_type=F32)
    rsq = lax.rsqrt(_mm01(q * q, rq_ref[...], 2) * inv_hd + EPS)
    q = q * _mm01(rsq, eq_ref[...], 2) * qn_ref[...]
    rq = q.shape[1] // LANES
    q_ref[...] = _rope(q, _tile_lanes(rc, rq), _tile_lanes(rs1, rq), _tile_lanes(rs2, rq))


def _attn_prompt_kernel(sink_ref, q_ref, kc_ref, kp_ref, vc_ref, vp_ref, o_ref):
    i = pl.program_id(1)
    w = WINDOW
    hd = ATT_HEAD_DIM
    n_q = q_ref.shape[1] // hd
    grp = n_q // ATT_KV_HEADS
    pair = LANES // hd
    rows = pair * w
    row = lax.broadcasted_iota(I32, (rows, 2 * w), 0) % w
    col = lax.broadcasted_iota(I32, (rows, 2 * w), 1)
    dist = row + w - col
    mask = (dist >= 0) & (dist < w) & ((col >= w) | (i > 0))
    first_head = lax.broadcasted_iota(I32, (rows, 1), 0) < w
    lo_lanes = lax.broadcasted_iota(I32, (w, LANES), 1) < hd
    zeros_kv = jnp.zeros((2 * w, hd), BF16)
    outs = []
    for g in range(ATT_KV_HEADS):
        sl = slice(g * hd, (g + 1) * hd)
        kk = jnp.concatenate([kp_ref[:, sl], kc_ref[:, sl]], axis=0).astype(BF16)
        vv = jnp.concatenate([vp_ref[:, sl], vc_ref[:, sl]], axis=0).astype(BF16)
        k2 = jnp.concatenate([kk, kk], axis=1)
        v_lo = jnp.concatenate([vv, zeros_kv], axis=1)
        v_hi = jnp.concatenate([zeros_kv, vv], axis=1)
        for h0 in range(g * grp, (g + 1) * grp, pair):
            qp = q_ref[:, h0 * hd:(h0 + pair) * hd] * (hd ** -0.5)
            qs = jnp.concatenate([jnp.where(lo_lanes, qp, 0.0), jnp.where(lo_lanes, 0.0, qp)],
                                 axis=0).astype(BF16)
            s = jnp.where(mask, _dot_nt(qs, k2), -jnp.inf)
            sink = jnp.where(first_head, sink_ref[h0], sink_ref[h0 + 1])
            m = jnp.maximum(jnp.max(s, axis=-1, keepdims=True), sink)
            p = jnp.exp(s - m)
            inv = 1.0 / (jnp.sum(p, axis=-1, keepdims=True) + jnp.exp(sink - m))
            pb = p.astype(BF16)
            outs.append(jnp.dot(pb[0:w], v_lo, preferred_element_type=F32) * inv[0:w]
                        + jnp.dot(pb[w:2 * w], v_hi, preferred_element_type=F32) * inv[w:2 * w])
    o_ref[...] = jnp.concatenate(outs, axis=1)


def _attn_prompt(sinks, q, k, v, bp, seq):
    w = WINDOW
    nb = seq // w
    qw, kvw = q.shape[1], k.shape[1]
    cur = lambda b, i: (b * nb + i, 0)
    prev = lambda b, i: (b * nb + jnp.maximum(i - 1, 0), 0)
    return pl.pallas_call(
        _attn_prompt_kernel,
        grid=(bp, nb),
        in_specs=[pl.BlockSpec(memory_space=pltpu.SMEM),
                  pl.BlockSpec((w, qw), cur), pl.BlockSpec((w, kvw), cur), pl.BlockSpec((w, kvw), prev),
                  pl.BlockSpec((w, kvw), cur), pl.BlockSpec((w, kvw), prev)],
        out_specs=pl.BlockSpec((w, qw), cur),
        out_shape=jax.ShapeDtypeStruct((bp * seq, qw), F32),
        compiler_params=_cparams(("arbitrary", "arbitrary")),
        name="attn_prompt",
    )(sinks, q, k, k, v, v)


def _attn_sample_kernel(sink_ref, q_ref, kn_ref, vn_ref, kc_ref, vc_ref, o_ref, kw_ref, vw_ref, *, n_tok):
    nb = SEQ_PER_STEP
    qn = n_tok * nb
    hd = ATT_HEAD_DIM
    lc = kc_ref.shape[1]
    for new_ref, old_ref, win_ref in ((kn_ref, kc_ref, kw_ref), (vn_ref, vc_ref, vw_ref)):
        new = new_ref[...]
        for sq in range(nb):
            win_ref[sq, 0:lc - n_tok, :] = old_ref[sq, n_tok:lc, :]
            win_ref[sq, lc - n_tok:lc, :] = jnp.concatenate(
                [new[t * nb + sq:t * nb + sq + 1, :] for t in range(n_tok)], axis=0)
    n_q = q_ref.shape[1] // hd
    grp = n_q // ATT_KV_HEADS
    rows = grp * qn
    r = lax.broadcasted_iota(I32, (rows, 1), 0)
    r_seq = r % nb
    r_tok = (r % qn) // nb
    ccol = lax.broadcasted_iota(I32, (rows, lc), 1)
    mask_c = ccol >= r_tok + 1 + (lc - WINDOW)
    ncol = lax.broadcasted_iota(I32, (rows, LANES), 1)
    mask_n = (ncol < qn) & (ncol % nb == r_seq) & (ncol // nb <= r_tok)
    q = q_ref[...]
    zpad = jnp.zeros((LANES - qn, hd), F32)
    outs = [None] * n_q
    for g in range(ATT_KV_HEADS):
        sl = slice(g * hd, (g + 1) * hd)
        qs = jnp.concatenate([q[:, (g * grp + hq) * hd:(g * grp + hq + 1) * hd] for hq in range(grp)],
                             axis=0).astype(BF16)
        k_new = jnp.concatenate([kn_ref[:, sl], zpad], axis=0).astype(BF16)
        v_new = jnp.concatenate([vn_ref[:, sl], zpad], axis=0).astype(BF16)
        s_n = jnp.where(mask_n, _dot_nt(qs, k_new) * (hd ** -0.5), -jnp.inf)
        s_c = jnp.zeros((rows, lc), F32)
        for sq in range(nb):
            s_sq = _dot_nt(qs, kc_ref[sq, :, sl].astype(BF16))
            s_c = jnp.where(r_seq == sq, s_sq, s_c)
        s_c = jnp.where(mask_c, s_c * (hd ** -0.5), -jnp.inf)
        sink = jnp.zeros((rows, 1), F32)
        for hq in range(grp):
            sink = jnp.where(r // qn == hq, sink_ref[g * grp + hq], sink)
        m = jnp.maximum(jnp.maximum(jnp.max(s_c, axis=-1, keepdims=True),
                                    jnp.max(s_n, axis=-1, keepdims=True)), sink)
        p_c = jnp.exp(s_c - m)
        p_n = jnp.exp(s_n - m)
        denom = (jnp.sum(p_c, axis=-1, keepdims=True) + jnp.sum(p_n, axis=-1, keepdims=True)
                 + jnp.exp(sink - m))
        p_c = p_c / denom
        o = jnp.dot((p_n / denom).astype(BF16), v_new, preferred_element_type=F32)
        for sq in range(nb):
            o = o + jnp.dot(jnp.where(r_seq == sq, p_c, 0.0).astype(BF16), vc_ref[sq, :, sl].astype(BF16),
                            preferred_element_type=F32)
        for hq in range(grp):
            outs[g * grp + hq] = o[hq * qn:(hq + 1) * qn, :]
    o_ref[...] = jnp.concatenate(outs, axis=1)


def _attn_sample(sinks, q, k, v, kc, vc, tp, n_tok):
    nb = SEQ_PER_STEP
    qn = n_tok * nb
    nsteps = kc.shape[0] // nb
    lc, kvw = kc.shape[1], kc.shape[2]
    qw = q.shape[1]
    base = tp // qn
    rows = lambda s: (base + s, 0)
    return pl.pallas_call(
        functools.partial(_attn_sample_kernel, n_tok=n_tok),
        grid=(nsteps,),
        in_specs=[pl.BlockSpec(memory_space=pltpu.SMEM),
                  pl.BlockSpec((qn, qw), rows), pl.BlockSpec((qn, kvw), rows), pl.BlockSpec((qn, kvw), rows),
                  pl.BlockSpec((nb, lc, kvw), lambda s: (s, 0, 0)), pl.BlockSpec((nb, lc, kvw), lambda s: (s, 0, 0))],
        out_specs=[pl.BlockSpec((qn, qw), lambda s: (s, 0)),
                   pl.BlockSpec((nb, lc, kvw), lambda s: (s, 0, 0)), pl.BlockSpec((nb, lc, kvw), lambda s: (s, 0, 0))],
        out_shape=[jax.ShapeDtypeStruct((nsteps * qn, qw), F32),
                   jax.ShapeDtypeStruct(kc.shape, F32), jax.ShapeDtypeStruct(vc.shape, F32)],
        compiler_params=_cparams(("arbitrary",)),
        name="attn_sample",
    )(sinks, q, k, v, kc, vc)


def _router_kernel(hp_ref, hs_ref, fn_ref, wh_ref, wl_ref, b_ref, triu_ref, xn_ref, info_ref, info_t_ref, cnt_ref,
                   carry_sc, *, n_p):
    h = jnp.where(pl.program_id(0) < n_p, hp_ref[...], hs_ref[...])
    _router_math(h, fn_ref, wh_ref, wl_ref, b_ref, triu_ref, xn_ref, info_ref, info_t_ref, cnt_ref, carry_sc)


def _wo_router_kernel(op_ref, os_ref, res_ref, wo_ref, fn_ref, wh_ref, wl_ref, b_ref, triu_ref,
                      h_ref, xn_ref, info_ref, info_t_ref, cnt_ref, carry_sc, *, n_p):
    o = jnp.where(pl.program_id(0) < n_p, op_ref[...], os_ref[...])
    h = res_ref[...] + jnp.dot(o.astype(BF16), wo_ref[...], preferred_element_type=F32)
    h_ref[...] = h
    _router_math(h, fn_ref, wh_ref, wl_ref, b_ref, triu_ref, xn_ref, info_ref, info_t_ref, cnt_ref, carry_sc)


def _router_math(h, fn_ref, wh_ref, wl_ref, b_ref, triu_ref, xn_ref, info_ref, info_t_ref, cnt_ref, carry_sc):
    i = pl.program_id(0)

    @pl.when(i == 0)
    def _():
        carry_sc[...] = jnp.zeros(carry_sc.shape, F32)

    xn = _rms_scale(h) * fn_ref[...]
    _store_token_major(xn_ref, xn)
    x_hi, x_lo = _split_bf16(xn, 2)
    wh, wl = wh_ref[...], wl_ref[...]
    logits = (jnp.dot(x_hi, wh, preferred_element_type=F32) + jnp.dot(x_hi, wl, preferred_element_type=F32)
              + jnp.dot(x_lo, wh, preferred_element_type=F32)) + b_ref[...]
    tm = logits.shape[0]
    lt = logits.T
    per = MOE_EXPERTS_PER_GROUP
    row = lax.broadcasted_iota(I32, (per, tm), 0).astype(F32)
    big = float(LANES)
    neg = -jnp.inf

    lg = jnp.where(row < MOE_GROUPS, lt[MOE_EXPERTS:MOE_EXPERTS + per, :], neg)
    mg = jnp.max(lg, axis=0, keepdims=True)
    gp = 1.0 / jnp.sum(jnp.exp(lg - mg), axis=0, keepdims=True)
    gi = jnp.min(jnp.where(lg == mg, row, big), axis=0, keepdims=True)

    le = lt[0:per, :]
    for g in range(1, MOE_GROUPS):
        le = jnp.where(gi == g, lt[g * per:(g + 1) * per, :], le)
    m1 = jnp.max(le, axis=0, keepdims=True)
    i1 = jnp.min(jnp.where(le == m1, row, big), axis=0, keepdims=True)
    le2 = jnp.where(row == i1, neg, le)
    m2 = jnp.max(le2, axis=0, keepdims=True)
    i2 = jnp.min(jnp.where(le2 == m2, row, big), axis=0, keepdims=True)
    e2 = jnp.exp(m2 - m1)
    g1 = gp * (1.0 / (1.0 + e2))
    g2 = gp * (e2 / (1.0 + e2))
    x1 = gi * per + i1
    x2 = gi * per + i2

    row_e = lax.broadcasted_iota(I32, (LANES, tm), 0).astype(F32)
    a1 = row_e == x1
    a2 = row_e == x2
    onehot = jnp.where(a1 | a2, 1.0, 0.0)
    before = jnp.dot(onehot.astype(BF16), triu_ref[...], preferred_element_type=F32) + carry_sc[:, 0:1]
    r1 = jnp.sum(jnp.where(a1, before, 0.0), axis=0, keepdims=True)
    r2 = jnp.sum(jnp.where(a2, before, 0.0), axis=0, keepdims=True)
    carry_sc[...] = carry_sc[...] + jnp.sum(onehot, axis=1, keepdims=True)
    cnt_ref[...] = carry_sc[...]

    info_t = jnp.zeros((SUBLANES, tm), F32)
    field = lax.broadcasted_iota(I32, (SUBLANES, tm), 0)
    for k, val in enumerate((x1, x2, g1, g2, r1, r2)):
        info_t = jnp.where(field == k, val, info_t)
    info_t_ref[...] = info_t
    info_ref[...] = jnp.concatenate([info_t, jnp.zeros((LANES - SUBLANES, tm), F32)], axis=0).T


def _two_source_specs(tm, width, n_p):
    return [pl.BlockSpec((tm, width), lambda i: (jnp.minimum(i, n_p - 1), 0)),
            pl.BlockSpec((tm, width), lambda i: (jnp.maximum(i - n_p, 0), 0))]


def _router(h_p, h_s, fn, wh, wl, b, tril):
    d = h_p.shape[1]
    tm = ROUTER_TILE
    n_p = h_p.shape[0] // tm
    t = h_p.shape[0] + h_s.shape[0]
    rows = lambda i: (i, 0)
    return pl.pallas_call(
        functools.partial(_router_kernel, n_p=n_p),
        grid=(t // tm,),
        in_specs=_two_source_specs(tm, d, n_p) + [_const_spec(a.shape) for a in (fn, wh, wl, b, tril)],
        out_specs=[pl.BlockSpec((tm * ROW_TILES, LANES), rows), pl.BlockSpec((tm, LANES), rows),
                   pl.BlockSpec((SUBLANES, tm), lambda i: (0, i)), _const_spec((LANES, LANES))],
        out_shape=[jax.ShapeDtypeStruct((t * ROW_TILES, LANES), F32), jax.ShapeDtypeStruct((t, LANES), F32),
                   jax.ShapeDtypeStruct((SUBLANES, t), F32), jax.ShapeDtypeStruct((LANES, LANES), F32)],
        scratch_shapes=[pltpu.VMEM((LANES, LANES), F32)],
        compiler_params=_cparams(("arbitrary",)),
        name="moe_router",
    )(h_p, h_s, fn, wh, wl, b, tril)


def _wo_router(o_p, o_s, res, wo, fn, wh, wl, b, tril):
    t, d = res.shape
    tm = ROUTER_TILE
    n_p = o_p.shape[0] // tm
    rows = lambda i: (i, 0)
    return pl.pallas_call(
        functools.partial(_wo_router_kernel, n_p=n_p),
        grid=(t // tm,),
        in_specs=_two_source_specs(tm, o_p.shape[1], n_p) + [pl.BlockSpec((tm, d), rows)]
        + [_const_spec(a.shape) for a in (wo, fn, wh, wl, b, tril)],
        out_specs=[pl.BlockSpec((tm, d), rows), pl.BlockSpec((tm * ROW_TILES, LANES), rows),
                   pl.BlockSpec((tm, LANES), rows), pl.BlockSpec((SUBLANES, tm), lambda i: (0, i)),
                   _const_spec((LANES, LANES))],
        out_shape=[jax.ShapeDtypeStruct((t, d), F32), jax.ShapeDtypeStruct((t * ROW_TILES, LANES), F32),
                   jax.ShapeDtypeStruct((t, LANES), F32), jax.ShapeDtypeStruct((SUBLANES, t), F32),
                   jax.ShapeDtypeStruct((LANES, LANES), F32)],
        scratch_shapes=[pltpu.VMEM((LANES, LANES), F32)],
        compiler_params=_cparams(("arbitrary",)),
        name="wo_router",
    )(o_p, o_s, res, wo, fn, wh, wl, b, tril)


def _store_token_major(ref, x):
    n = x.shape[0]
    for j in range(ROW_TILES):
        ref[pl.ds(j, n, stride=ROW_TILES), :] = x[:, j * LANES:(j + 1) * LANES]


def _load_token_major(ref, n):
    return jnp.concatenate([ref[pl.ds(j, n, stride=ROW_TILES), :] for j in range(ROW_TILES)], axis=1)


def _dest_kernel(pst_ref, info_ref, dest_ref):
    info = info_ref[...]
    e = info[0:2, :]
    start = jnp.zeros(e.shape, F32)
    for k in range(MOE_EXPERTS):
        start = jnp.where(e == k, pst_ref[k].astype(F32), start)
    dest = (start + info[4:6, :]).astype(I32)
    dest_ref[...] = jnp.concatenate([dest, jnp.zeros((SUBLANES - 2, dest.shape[1]), I32)], axis=0)


def _dest(pstarts, info_t):
    return pl.pallas_call(
        _dest_kernel,
        in_specs=[pl.BlockSpec(memory_space=pltpu.SMEM), pl.BlockSpec(memory_space=pltpu.VMEM)],
        out_specs=pl.BlockSpec(memory_space=pltpu.VMEM),
        out_shape=jax.ShapeDtypeStruct(info_t.shape, I32),
        name="moe_dest",
    )(pstarts, info_t)


def _tile_copy(src, src_row, dst, dst_row, sem):
    return pltpu.make_async_copy(src.at[pl.ds(pl.multiple_of(src_row * ROW_TILES, ROW_TILES), ROW_TILES)],
                                 dst.at[pl.ds(pl.multiple_of(dst_row * ROW_TILES, ROW_TILES), ROW_TILES)], sem)


def _wait_tiles(ref, n_tokens, sem):
    blk = ref.at[pl.ds(0, n_tokens * ROW_TILES)]
    pltpu.make_async_copy(blk, blk, sem).wait()


def _dispatch_kernel(d1_ref, d2_ref, pend_ref, pcnt_ref, xn_ref, xbuf_hbm, zero_sc, sem_z, sem):
    i = pl.program_id(0)
    tm = xn_ref.shape[0] // ROW_TILES
    blk_rows = zero_sc.shape[0]

    @pl.when(i == 0)
    def _():
        zero_sc[...] = jnp.zeros(zero_sc.shape, F32)

        def zero_copy(e):
            start = pl.multiple_of(pend_ref[e] * ROW_TILES - blk_rows, ROW_TILES)
            return pltpu.make_async_copy(zero_sc, xbuf_hbm.at[pl.ds(start, blk_rows)], sem_z)

        for e in range(MOE_EXPERTS):
            @pl.when(pcnt_ref[e] > 0)
            def _():
                zero_copy(e).start()
        first_unused = pend_ref[MOE_EXPERTS - 1] * ROW_TILES // blk_rows
        n_blocks = xbuf_hbm.shape[0] // blk_rows

        def tail_copy(b):
            dst = xbuf_hbm.at[pl.ds(pl.multiple_of(b * blk_rows, blk_rows), blk_rows)]
            return pltpu.make_async_copy(zero_sc, dst, sem_z)

        def tail_start(b, carry):
            tail_copy(b).start()
            return carry

        def tail_wait(b, carry):
            tail_copy(b).wait()
            return carry
        lax.fori_loop(first_unused, n_blocks, tail_start, 0)
        for e in range(MOE_EXPERTS):
            @pl.when(pcnt_ref[e] > 0)
            def _():
                zero_copy(e).wait()
        lax.fori_loop(first_unused, n_blocks, tail_wait, 0)

    def body(r, carry):
        t = i * tm + r
        _tile_copy(xn_ref, r, xbuf_hbm, d1_ref[t], sem).start(priority=0)
        _tile_copy(xn_ref, r, xbuf_hbm, d2_ref[t], sem).start(priority=1)
        return carry
    lax.fori_loop(0, tm, body, 0, unroll=8)
    _wait_tiles(xn_ref, tm, sem)
    _wait_tiles(xn_ref, tm, sem)


def _dispatch(dest1, dest2, pends, pcounts, xn_tm, n_rows):
    n_tok = dest1.shape[0]
    tm = DISPATCH_TILE
    grid_spec = pltpu.PrefetchScalarGridSpec(
        num_scalar_prefetch=4,
        grid=(n_tok // tm,),
        in_specs=[pl.BlockSpec((tm * ROW_TILES, LANES), lambda i, *_: (i, 0))],
        out_specs=pl.BlockSpec(memory_space=pl.ANY),
        scratch_shapes=[pltpu.VMEM((EXPERT_ROWS * ROW_TILES, LANES), F32), pltpu.SemaphoreType.DMA(()),
                        pltpu.SemaphoreType.DMA(())],
    )
    return pl.pallas_call(
        _dispatch_kernel,
        grid_spec=grid_spec,
        out_shape=jax.ShapeDtypeStruct((n_rows * ROW_TILES, LANES), F32),
        compiler_params=_cparams(("arbitrary",)),
        name="moe_dispatch",
    )(dest1, dest2, pends, pcounts, xn_tm)


def _expert_kernel(blk_e_ref, nvalid_ref, next_e_ref, x_hbm, wg_hbm, wu_hbm, wd_hbm, y_ref,
                   wg_sc, wu_sc, wd_sc, wg_st, wu_st, wd_st, x_sc, sem, xsem, *, layer):
    i = pl.program_id(0)
    nv = nvalid_ref[0]
    blk_rows = x_sc.shape[1]
    rows = blk_rows // ROW_TILES

    def weight_copies(e):
        return (pltpu.make_async_copy(wg_hbm.at[layer, e], wg_st, sem.at[0]),
                pltpu.make_async_copy(wu_hbm.at[layer, e], wu_st, sem.at[1]),
                pltpu.make_async_copy(wd_hbm.at[layer, e], wd_st, sem.at[2]))

    def row_copy(step):
        slot = step % EXPERT_RING
        src = x_hbm.at[pl.ds(pl.multiple_of(step * blk_rows, blk_rows), blk_rows)]
        return pltpu.make_async_copy(src, x_sc.at[slot], xsem.at[slot])

    @pl.when((i == 0) & (nv > 0))
    def _():
        for c in weight_copies(blk_e_ref[0]):
            c.start()
        for s in range(EXPERT_RING - 1):
            @pl.when(s < nv)
            def _():
                row_copy(s).start()

    @pl.when(i + EXPERT_RING - 1 < nv)
    def _():
        row_copy(i + EXPERT_RING - 1).start()

    @pl.when(i < nv)
    def _():
        e = blk_e_ref[i]
        e_prev = blk_e_ref[jnp.maximum(i - 1, 0)]
        row_copy(i).wait()
        x_ref = x_sc.at[i % EXPERT_RING]

        @pl.when((i == 0) | (e != e_prev))
        def _():
            for c in weight_copies(e):
                c.wait()
            wg_sc[...] = wg_st[...].astype(BF16)
            wu_sc[...] = wu_st[...].astype(BF16)
            wd_sc[...] = wd_st[...].astype(BF16)
            nxt = next_e_ref[e]

            @pl.when(nxt < MOE_EXPERTS)
            def _():
                for c in weight_copies(nxt):
                    c.start()

        x = _load_token_major(x_ref, rows).astype(BF16)
        hid = _silu(jnp.dot(x, wg_sc[...], preferred_element_type=F32)) * jnp.dot(
            x, wu_sc[...], preferred_element_type=F32)
        _store_token_major(y_ref, jnp.dot(hid.astype(BF16), wd_sc[...], preferred_element_type=F32))

    @pl.when(i >= nv)
    def _():
        y_ref[...] = jnp.zeros(y_ref.shape, F32)


def _experts(blk_e, nvalid, next_e, xbuf, wg, wu, wd, layer):
    nblk = blk_e.shape[0]
    d, hdim = wg.shape[2], wg.shape[3]
    rows = EXPERT_ROWS * ROW_TILES
    grid_spec = pltpu.PrefetchScalarGridSpec(
        num_scalar_prefetch=3,
        grid=(nblk,),
        in_specs=[pl.BlockSpec(memory_space=pl.ANY),
                  pl.BlockSpec(memory_space=pl.ANY), pl.BlockSpec(memory_space=pl.ANY),
                  pl.BlockSpec(memory_space=pl.ANY)],
        out_specs=pl.BlockSpec((rows, LANES), lambda i, be, nv, ne: (i, 0)),
        scratch_shapes=[pltpu.VMEM((d, hdim), BF16), pltpu.VMEM((d, hdim), BF16), pltpu.VMEM((hdim, d), BF16),
                        pltpu.VMEM((d, hdim), F32), pltpu.VMEM((d, hdim), F32), pltpu.VMEM((hdim, d), F32),
                        pltpu.VMEM((EXPERT_RING, rows, LANES), F32),
                        pltpu.SemaphoreType.DMA((3,)), pltpu.SemaphoreType.DMA((EXPERT_RING,))],
    )
    return pl.pallas_call(
        functools.partial(_expert_kernel, layer=layer),
        grid_spec=grid_spec,
        out_shape=jax.ShapeDtypeStruct((nblk * rows, LANES), F32),
        compiler_params=_cparams(("arbitrary",)),
        name="moe_experts",
    )(blk_e, nvalid, next_e, xbuf, wg, wu, wd)


def _gather_moe_rows(d1_ref, d2_ref, y_hbm, r_sc, sem, tm, row0=0):
    i = pl.program_id(0)
    n = pl.num_programs(0)

    def start(step):
        slot = step % 2
        base = row0 + step * tm

        def body(r, carry):
            _tile_copy(y_hbm, d1_ref[base + r], r_sc.at[slot, 0], r, sem.at[slot]).start(priority=0)
            _tile_copy(y_hbm, d2_ref[base + r], r_sc.at[slot, 1], r, sem.at[slot]).start(priority=1)
            return carry
        lax.fori_loop(0, tm, body, 0, unroll=8)

    @pl.when(i == 0)
    def _():
        start(i)

    @pl.when(i + 1 < n)
    def _():
        start(i + 1)

    slot = i % 2
    _wait_tiles(r_sc.at[slot, 0], tm, sem.at[slot])
    _wait_tiles(r_sc.at[slot, 1], tm, sem.at[slot])
    return _load_token_major(r_sc.at[slot, 0], tm), _load_token_major(r_sc.at[slot, 1], tm)


def _combine_kernel(d1_ref, d2_ref, h_ref, info_ref, y_hbm, o_ref, r_sc, sem, *, row0):
    y1, y2 = _gather_moe_rows(d1_ref, d2_ref, y_hbm, r_sc, sem, h_ref.shape[0], row0)
    info = info_ref[...]
    o_ref[...] = h_ref[...] + (y1 * info[:, 2:3] + y2 * info[:, 3:4])


def _combine(dest1, dest2, h, info, ybuf, row0, nrows):
    d = h.shape[1]
    tm = ROUTER_TILE
    base_tile = row0 // tm
    rows = lambda i, a, b: (base_tile + i, 0)
    grid_spec = pltpu.PrefetchScalarGridSpec(
        num_scalar_prefetch=2,
        grid=(nrows // tm,),
        in_specs=[pl.BlockSpec((tm, d), rows), pl.BlockSpec((tm, LANES), rows), pl.BlockSpec(memory_space=pl.ANY)],
        out_specs=pl.BlockSpec((tm, d), lambda i, a, b: (i, 0)),
        scratch_shapes=[pltpu.VMEM((2, 2, tm * ROW_TILES, LANES), F32), pltpu.SemaphoreType.DMA((2,))],
    )
    return pl.pallas_call(
        functools.partial(_combine_kernel, row0=row0),
        grid_spec=grid_spec,
        out_shape=jax.ShapeDtypeStruct((nrows, d), F32),
        compiler_params=_cparams(("arbitrary",)),
        name="moe_combine",
    )(dest1, dest2, h, info, ybuf)


def _combine_qkv_kernel(d1_ref, d2_ref, hp_ref, hs_ref, info_ref, y_hbm, kvn_ref, an_ref, wkv_ref, wq_ref, kn_ref,
                        qn_ref, rc_ref, rs1_ref, rs2_ref, hsum_ref, rq_ref, eq_ref,
                        h_ref, q_ref, k_ref, v_ref, r_sc, sem, *, n_p):
    y1, y2 = _gather_moe_rows(d1_ref, d2_ref, y_hbm, r_sc, sem, h_ref.shape[0])
    info = info_ref[...]
    h = jnp.where(pl.program_id(0) < n_p, hp_ref[...], hs_ref[...]) + (y1 * info[:, 2:3] + y2 * info[:, 3:4])
    h_ref[...] = h
    _qkv_math(h, kvn_ref, an_ref, wkv_ref, wq_ref, kn_ref, qn_ref, rc_ref, rs1_ref, rs2_ref,
              hsum_ref, rq_ref, eq_ref, q_ref, k_ref, v_ref)


def _combine_qkv(dest1, dest2, h_p, h_s, info, ybuf, kvn, an, wkv, wq, knt, qnt, rc, rs1, rs2, hsum, rq, eq):
    d = h_p.shape[1]
    tm = TOKEN_TILE
    n_p = h_p.shape[0] // tm
    t = h_p.shape[0] + h_s.shape[0]
    kvw = wkv.shape[1] // 2
    qw = wq.shape[1]
    rows = lambda i, a, b: (i, 0)
    const = lambda arr: pl.BlockSpec(arr.shape, lambda i, a, b: (0,) * arr.ndim)
    seq_tiles = (rc.shape[0] - h_s.shape[0]) // tm
    rope_rows = lambda i, a, b: (jnp.where(i < n_p, i % seq_tiles, seq_tiles + i - n_p), 0)
    grid_spec = pltpu.PrefetchScalarGridSpec(
        num_scalar_prefetch=2,
        grid=(t // tm,),
        in_specs=[pl.BlockSpec((tm, d), lambda i, a, b: (jnp.minimum(i, n_p - 1), 0)),
                  pl.BlockSpec((tm, d), lambda i, a, b: (jnp.maximum(i - n_p, 0), 0)),
                  pl.BlockSpec((tm, LANES), rows), pl.BlockSpec(memory_space=pl.ANY)]
        + [const(a) for a in (kvn, an, wkv, wq, knt, qnt)] + [pl.BlockSpec((tm, LANES), rope_rows)] * 3
        + [const(a) for a in (hsum, rq, eq)],
        out_specs=[pl.BlockSpec((tm, d), rows), pl.BlockSpec((tm, qw), rows), pl.BlockSpec((tm, kvw), rows),
                   pl.BlockSpec((tm, kvw), rows)],
        scratch_shapes=[pltpu.VMEM((2, 2, tm * ROW_TILES, LANES), F32), pltpu.SemaphoreType.DMA((2,))],
    )
    return pl.pallas_call(
        functools.partial(_combine_qkv_kernel, n_p=n_p),
        grid_spec=grid_spec,
        out_shape=[jax.ShapeDtypeStruct((t, d), F32), jax.ShapeDtypeStruct((t, qw), F32),
                   jax.ShapeDtypeStruct((t, kvw), F32), jax.ShapeDtypeStruct((t, kvw), F32)],
        compiler_params=_cparams(("arbitrary",)),
        name="combine_qkv",
    )(dest1, dest2, h_p, h_s, info, ybuf, kvn, an, wkv, wq, knt, qnt, rc, rs1, rs2, hsum, rq, eq)


def _router_weights(w_grp, b_grp, w_rt, b_rt):
    d = w_rt.shape[0]
    w_cat = jnp.zeros((d, LANES), F32).at[:, :MOE_EXPERTS].set(w_rt).at[:, MOE_EXPERTS:MOE_EXPERTS + MOE_GROUPS].set(w_grp)
    b_cat = jnp.zeros((1, LANES), F32).at[0, :MOE_EXPERTS].set(b_rt).at[0, MOE_EXPERTS:MOE_EXPERTS + MOE_GROUPS].set(b_grp)
    w_hi = w_cat.astype(BF16)
    w_lo = (w_cat - w_hi.astype(F32)).astype(BF16)
    return w_hi, w_lo, b_cat


def _moe_experts(xn_tm, info_t, cnt, wg, wu, wd, layer):
    t = info_t.shape[1]
    counts = cnt[:MOE_EXPERTS, 0].astype(I32)
    pcounts = (counts + EXPERT_ROWS - 1) // EXPERT_ROWS * EXPERT_ROWS
    pends = jnp.cumsum(pcounts)
    pstarts = pends - pcounts
    nblk = -(-(2 * t + MOE_EXPERTS * (EXPERT_ROWS - 1)) // EXPERT_ROWS)
    blk_start = jnp.arange(nblk, dtype=I32) * EXPERT_ROWS
    blk_e = jnp.minimum(jnp.sum((pends[None, :] <= blk_start[:, None]).astype(I32), axis=1), MOE_EXPERTS - 1)
    nvalid = pends[-1:] // EXPERT_ROWS
    eid = jnp.arange(MOE_EXPERTS, dtype=I32)
    later = (eid[None, :] > eid[:, None]) & (pcounts[None, :] > 0)
    next_e = jnp.min(jnp.where(later, eid[None, :], MOE_EXPERTS), axis=1).astype(I32)

    dest = _dest(pstarts, info_t)
    dest1, dest2 = dest[0], dest[1]
    xbuf = _dispatch(dest1, dest2, pends, pcounts, xn_tm, nblk * EXPERT_ROWS)
    return dest1, dest2, _experts(blk_e, nvalid, next_e, xbuf, wg, wu, wd, layer)


def _rope_tables(pos):
    half = ROT_DIM // 2
    inv = ROPE_THETA ** (-np.arange(0, ROT_DIM, 2, dtype=np.float64) / ROT_DIM)
    ang = pos.astype(np.float64)[:, None] * inv[None, :]
    cos, sin = np.cos(ang), np.sin(ang)
    n = pos.shape[0]
    ones = np.ones((n, ATT_HEAD_DIM - ROT_DIM))
    zeros_r = np.zeros((n, ATT_HEAD_DIM - ROT_DIM))
    zeros_h = np.zeros((n, half))
    c = np.concatenate([cos, cos, ones], axis=1)
    s1 = np.concatenate([-sin, zeros_h, zeros_r], axis=1)
    s2 = np.concatenate([zeros_h, sin, zeros_r], axis=1)
    reps = LANES // ATT_HEAD_DIM
    return tuple(jnp.asarray(np.tile(a, (1, reps)).astype(np.float32)) for a in (c, s1, s2))


def _mask01(m):
    return jnp.asarray(m.astype(np.float32), dtype=BF16)


def _to_step_order(a, nsteps, n_tok):
    c = a.shape[-1]
    return a.reshape(nsteps, SEQ_PER_STEP, n_tok, c).transpose(0, 2, 1, 3).reshape(nsteps * n_tok * SEQ_PER_STEP, c)


def _from_step_order(a, nsteps, n_tok):
    c = a.shape[-1]
    return a.reshape(nsteps, n_tok, SEQ_PER_STEP, c).transpose(0, 2, 1, 3).reshape(nsteps * SEQ_PER_STEP, n_tok, c)


def kernel(x_prompt, x_sample, state_ssm, state_conv, cache_k_win, cache_v_win, ssm_norm, ssm_w_in, ssm_conv_w, ssm_conv_b, ssm_dt_bias, ssm_a_log, ssm_d, ssm_gate_norm, ssm_w_out, kv_norm, w_kv, k_norm, attn_norm, w_q, q_norm, sinks, w_o, ffn_norm, moe_w_group, moe_b_group, moe_w_router, moe_b_router, moe_w_gate, moe_w_up, moe_w_down):
    bp, seq, d = x_prompt.shape
    bs, n_tok, _ = x_sample.shape
    tp, ts = bp * seq, bs * n_tok
    nsteps = bs // SEQ_PER_STEP
    n_heads = ssm_d.shape[1]
    di = n_heads * SSM_HEAD_DIM
    gn_w = SSM_GROUPS * SSM_STATE
    cdim = di + 2 * gn_w
    n_q = sinks.shape[1]
    kvw = ATT_KV_HEADS * ATT_HEAD_DIM

    xp2 = x_prompt.reshape(tp, d)
    xs2 = _to_step_order(x_sample, nsteps, n_tok)

    lane_i = np.arange(LANES)
    e01 = _mask01(lane_i[:, None] == (np.arange(di) // SSM_HEAD_DIM)[None, :])
    hpg = di // SSM_GROUPS
    g1 = ((np.arange(gn_w) // SSM_STATE)[:, None] == (lane_i // (hpg // SSM_HEAD_DIM))[None, :])
    g1 = _mask01(g1 & (lane_i < n_heads)[None, :])
    tril_c = _mask01(np.arange(SSM_CHUNK)[:, None] >= np.arange(SSM_CHUNK)[None, :])
    earlier_x = _mask01(np.arange(ROUTER_TILE)[:, None] < np.arange(ROUTER_TILE)[None, :])
    hsum = _mask01((np.arange(kvw) // ATT_HEAD_DIM)[:, None] == (np.arange(kvw) // ATT_HEAD_DIM)[None, :])
    qw = n_q * ATT_HEAD_DIM
    rq_np = (np.arange(qw) // ATT_HEAD_DIM)[:, None] == lane_i[None, :]
    rq, eq = _mask01(rq_np), _mask01(rq_np.T)

    w_in = ssm_w_in[0]
    wz = w_in[:, :di].astype(BF16)
    wx = w_in[:, di:di + cdim].astype(BF16)
    wd = jnp.zeros((d, LANES), F32).at[:, :n_heads].set(w_in[:, di + cdim:]).astype(BF16)
    cw, cb = ssm_conv_w[0], ssm_conv_b[0].reshape(1, cdim)
    z, xbc, dt = _inproj(xp2, xs2, ssm_norm[0].reshape(1, d), wz, wx, wd)

    pad_h = lambda v: jnp.zeros((1, LANES), F32).at[0, :n_heads].set(v)
    dtb, alog = pad_h(ssm_dt_bias[0]), pad_h(ssm_a_log[0])
    dsk = jnp.repeat(ssm_d[0], SSM_HEAD_DIM).reshape(1, di)
    gnw = ssm_gate_norm[0].reshape(1, di)

    w_out = ssm_w_out[0].astype(BF16)
    h_p, s_fin, c_fin = _ssd_prompt(z, xbc, dt, xp2, cw, cb, dtb, alog, dsk, gnw, tril_c, e01, w_out, bp, seq)
    ssm_p = s_fin.reshape(1, bp, n_heads, SSM_HEAD_DIM, SSM_STATE)
    conv_p = c_fin[:, SUBLANES - (SSM_CONV - 1):, :].reshape(1, bp, SSM_CONV - 1, cdim)

    xbc_s = xbc[tp:].reshape(nsteps, n_tok, SEQ_PER_STEP, cdim)
    conv_in = state_conv[0].reshape(nsteps, SEQ_PER_STEP, SSM_CONV - 1, cdim).transpose(0, 2, 1, 3)
    xp7 = jnp.concatenate([conv_in, xbc_s], axis=1)
    s0 = state_ssm[0].reshape(bs, di, SSM_STATE)
    h_s, s_new = _ssd_sample(z, xp7, dt, s0, xs2, cw, cb, dtb, alog, dsk, gnw, e01, g1, w_out, tp, n_tok)
    ssm_s = s_new.reshape(1, bs, n_heads, SSM_HEAD_DIM, SSM_STATE)
    conv_s = _from_step_order(xbc[tp:], nsteps, n_tok)[:, n_tok - (SSM_CONV - 1):, :].reshape(
        1, bs, SSM_CONV - 1, cdim)

    w_hi, w_lo, b_cat = _router_weights(moe_w_group[0], moe_b_group[0], moe_w_router[0], moe_b_router[0])
    xn_tm, info, info_t, cnt = _router(h_p, h_s, ffn_norm[0].reshape(1, d), w_hi, w_lo, b_cat, earlier_x)
    dest1, dest2, ybuf = _moe_experts(xn_tm, info_t, cnt, moe_w_gate, moe_w_up, moe_w_down, 0)

    pos = np.concatenate([np.arange(seq), np.tile(np.repeat(PAST_LEN + np.arange(n_tok), SEQ_PER_STEP), nsteps)])
    rc, rs1, rs2 = _rope_tables(pos)
    h1, q, k, v = _combine_qkv(dest1, dest2, h_p, h_s, info, ybuf, kv_norm.reshape(1, d), attn_norm[0].reshape(1, d),
                               w_kv.astype(BF16), w_q[0].astype(BF16),
                               jnp.tile(k_norm, ATT_KV_HEADS).reshape(1, kvw), jnp.tile(q_norm[0], n_q).reshape(1, qw),
                               rc, rs1, rs2, hsum, rq, eq)
    sk = sinks[0]
    lc = cache_k_win.shape[1]
    kc = cache_k_win.reshape(bs, lc, kvw)
    vc = cache_v_win.reshape(bs, lc, kvw)
    o_p = _attn_prompt(sk, q, k, v, bp, seq)
    o_s, k_win, v_win = _attn_sample(sk, q, k, v, kc, vc, tp, n_tok)
    w_hi, w_lo, b_cat = _router_weights(moe_w_group[1], moe_b_group[1], moe_w_router[1], moe_b_router[1])
    h2, xn_tm, info, info_t, cnt = _wo_router(o_p, o_s, h1, w_o[0].astype(BF16), ffn_norm[1].reshape(1, d),
                                             w_hi, w_lo, b_cat, earlier_x)
    dest1, dest2, ybuf = _moe_experts(xn_tm, info_t, cnt, moe_w_gate, moe_w_up, moe_w_down, 1)
    y_p = _combine(dest1, dest2, h2, info, ybuf, 0, tp)
    y_s = _combine(dest1, dest2, h2, info, ybuf, tp, ts)

    wl = min(WINDOW, seq)
    k_p = k[:tp].reshape(bp, seq, kvw)[:, seq - wl:].reshape(bp, wl, ATT_KV_HEADS, ATT_HEAD_DIM)
    v_p = v[:tp].reshape(bp, seq, kvw)[:, seq - wl:].reshape(bp, wl, ATT_KV_HEADS, ATT_HEAD_DIM)
    k_s = k_win.reshape(bs, lc, ATT_KV_HEADS, ATT_HEAD_DIM)
    v_s = v_win.reshape(bs, lc, ATT_KV_HEADS, ATT_HEAD_DIM)
    return (y_p.reshape(bp, seq, d), _from_step_order(y_s, nsteps, n_tok),
            ssm_p, conv_p, k_p, v_p, ssm_s, conv_s, k_s, v_s)
```

```python
import functools

import numpy as np
import jax
import jax.numpy as jnp
from jax import lax
from jax.experimental import pallas as pl
from jax.experimental.pallas import tpu as pltpu

F32 = jnp.float32
BF16 = jnp.bfloat16
I32 = jnp.int32

EPS = 1e-6
SSM_HEAD_DIM = 64
SSM_GROUPS = 4
SSM_STATE = 128
SSM_CONV = 4
SSM_CHUNK = 128
ATT_HEAD_DIM = 64
ATT_KV_HEADS = 4
WINDOW = 128
ROT_DIM = ATT_HEAD_DIM // 4
ROPE_THETA = 500000.0
MOE_GROUPS = 4
MOE_EXPERTS_PER_GROUP = 8
MOE_EXPERTS = MOE_GROUPS * MOE_EXPERTS_PER_GROUP
MOE_BLOCK = 128
PAST_LEN = 16384

LANES = 128
SUBLANES = 8
BF16_ROWS = 16
SEQ_PER_STEP = SUBLANES
TOKEN_TILE = 256
ROUTER_TILE = 512
INPROJ_TILE = 512
ROW_TILES = 8
DISPATCH_TILE = 1536
EXPERT_RING = 3
EXPERT_ROWS = 512
VMEM_LIMIT = 56 * 1024 * 1024


def _cparams(sem):
    return pltpu.CompilerParams(dimension_semantics=sem, vmem_limit_bytes=VMEM_LIMIT)


def _const_spec(shape):
    nd = len(shape)
    return pl.BlockSpec(shape, lambda *_: (0,) * nd)


def _split_bf16(v, n):
    parts = []
    r = v
    for k in range(n):
        p = r.astype(BF16)
        parts.append(p)
        if k + 1 < n:
            r = r - p.astype(F32)
    return parts


def _mm01(v, m01, n=3):
    acc = None
    for p in _split_bf16(v, n):
        d = jnp.dot(p, m01, preferred_element_type=F32)
        acc = d if acc is None else acc + d
    return acc


def _mm01_left(m01, v, n=3):
    acc = None
    for p in _split_bf16(v, n):
        d = jnp.dot(m01, p, preferred_element_type=F32)
        acc = d if acc is None else acc + d
    return acc


def _dot_nt(a, b):
    return lax.dot_general(a, b, (((1,), (1,)), ((), ())), preferred_element_type=F32)


def _sigmoid(x):
    return 0.5 * jnp.tanh(0.5 * x) + 0.5


def _silu(x):
    return x * _sigmoid(x)


def _softplus(x):
    return jnp.maximum(x, 0.0) + jnp.log1p(jnp.exp(-jnp.abs(x)))


def _rms_scale(x):
    return x * lax.rsqrt(jnp.mean(x * x, axis=-1, keepdims=True) + EPS)


def _gate_norm(y, z, gn, n_groups):
    yz = y * _silu(z)
    w = yz.shape[-1] // n_groups
    outs = []
    for g in range(n_groups):
        v = yz[:, g * w:(g + 1) * w]
        outs.append(_rms_scale(v) * gn[:, g * w:(g + 1) * w])
    return jnp.concatenate(outs, axis=1)


def _rope(x, c, s1, s2):
    w = x.shape[-1]
    return x * c + pltpu.roll(x, w - ROT_DIM // 2, 1) * s1 + pltpu.roll(x, ROT_DIM // 2, 1) * s2


def _tile_lanes(t, reps):
    return t if reps == 1 else jnp.concatenate([t] * reps, axis=1)


def _inproj_kernel(xp_ref, xs_ref, g_ref, wz_ref, wx_ref, wd_ref, z_ref, xbc_ref, dt_ref, *, n_p):
    i = pl.program_id(0)
    x = jnp.where(i < n_p, xp_ref[...], xs_ref[...])
    xn = (_rms_scale(x) * g_ref[...]).astype(BF16)
    z_ref[...] = jnp.dot(xn, wz_ref[...], preferred_element_type=F32)
    xbc_ref[...] = jnp.dot(xn, wx_ref[...], preferred_element_type=F32)
    dt_ref[...] = jnp.dot(xn, wd_ref[...], preferred_element_type=F32)


def _inproj(xp2, xs2, g, wz, wx, wd):
    tp, d = xp2.shape
    ts = xs2.shape[0]
    tm = INPROJ_TILE
    n_p, n_s = tp // tm, ts // tm
    t = tp + ts
    return pl.pallas_call(
        functools.partial(_inproj_kernel, n_p=n_p),
        grid=(n_p + n_s,),
        in_specs=_two_source_specs(tm, d, n_p) + [_const_spec(a.shape) for a in (g, wz, wx, wd)],
        out_specs=[
            pl.BlockSpec((tm, wz.shape[1]), lambda i: (i, 0)),
            pl.BlockSpec((tm, wx.shape[1]), lambda i: (i, 0)),
            pl.BlockSpec((tm, wd.shape[1]), lambda i: (i, 0)),
        ],
        out_shape=[jax.ShapeDtypeStruct((t, wz.shape[1]), F32),
                   jax.ShapeDtypeStruct((t, wx.shape[1]), F32),
                   jax.ShapeDtypeStruct((t, wd.shape[1]), F32)],
        compiler_params=_cparams(("arbitrary",)),
        name="inproj",
    )(xp2, xs2, g, wz, wx, wd)


def _ssd_prompt_kernel(z_ref, xbc_ref, dt_ref, x_ref, cw_ref, cb_ref, dtb_ref, alog_ref, dsk_ref, gn_ref,
                       tril_ref, e_ref, wout_ref, h_ref, sfin_ref, cfin_ref, xpad_sc, st_sc):
    c = pl.program_id(1)
    q = SSM_CHUNK
    cd = xbc_ref.shape[1]
    di = z_ref.shape[1]
    gn_w = SSM_GROUPS * SSM_STATE
    hpg = di // SSM_GROUPS
    pad = SUBLANES

    n_slab = cd // LANES

    @pl.when(c == 0)
    def _():
        xpad_sc[:, 0:pad, :] = jnp.zeros((n_slab, pad, LANES), F32)
        st_sc[...] = jnp.zeros(st_sc.shape, F32)

    @pl.when(c > 0)
    def _():
        xpad_sc[:, 0:pad, :] = xpad_sc[:, q:q + pad, :]

    slabs = []
    for j in range(n_slab):
        ls = slice(j * LANES, (j + 1) * LANES)
        xpad_sc[j, pad:pad + q, :] = xbc_ref[:, ls]
        acc = cb_ref[:, ls]
        for k in range(SSM_CONV):
            off = pad - (SSM_CONV - 1) + k
            acc = acc + xpad_sc[j, off:off + q, :] * cw_ref[k:k + 1, ls]
        slabs.append(_silu(acc))
    xc = jnp.concatenate(slabs, axis=1)
    xs = xc[:, :di]
    bm = xc[:, di:di + gn_w]
    cm = xc[:, di + gn_w:]

    dt = _softplus(dt_ref[...] + dtb_ref[...])
    a = -jnp.exp(alog_ref[...])
    act = _mm01_left(tril_ref[...], dt * a)
    act_t = act.T
    act_last = act[q - 1:q, :]
    pieces = (_split_bf16(dt, 2) + _split_bf16(jnp.exp(act_last - act), 2) + _split_bf16(jnp.exp(act), 2)
              + _split_bf16(jnp.exp(act[q - BF16_ROWS:q, :]), 3))
    ex = jnp.dot(jnp.concatenate(pieces, axis=0), e_ref[...], preferred_element_type=F32)
    xdt = xs * (ex[0:q] + ex[q:2 * q])
    xd = xdt * (ex[2 * q:3 * q] + ex[3 * q:4 * q])
    eax = ex[4 * q:5 * q] + ex[5 * q:6 * q]
    tail = ex[6 * q:]
    cd = (tail[0:BF16_ROWS] + tail[BF16_ROWS:2 * BF16_ROWS]) + tail[2 * BF16_ROWS:]
    cdx = cd[BF16_ROWS - 1:BF16_ROWS, :]

    row = lax.broadcasted_iota(I32, (q, q), 0)
    col = lax.broadcasted_iota(I32, (q, q), 1)
    causal = row >= col
    lane = lax.broadcasted_iota(I32, (q, LANES), 1)
    lo_half = lane < SSM_HEAD_DIM

    y_parts = []
    heads_per_group = hpg // SSM_HEAD_DIM
    for g in range(SSM_GROUPS):
        cg = cm[:, g * SSM_STATE:(g + 1) * SSM_STATE].astype(BF16)
        bg = bm[:, g * SSM_STATE:(g + 1) * SSM_STATE]
        cb = _dot_nt(cg, bg.astype(BF16))
        st_g = st_sc[:, g * hpg:(g + 1) * hpg]
        y_off = jnp.dot(cg, st_g.astype(BF16), preferred_element_type=F32)
        for pr in range(heads_per_group // 2):
            h0 = g * heads_per_group + 2 * pr
            ms = []
            for h in (h0, h0 + 1):
                seg = act[:, h:h + 1] - act_t[h:h + 1, :]
                lm = jnp.exp(jnp.where(causal, seg, -jnp.inf))
                ms.append((cb * lm).astype(BF16))
            m2 = jnp.concatenate(ms, axis=1)
            xpair = xdt[:, h0 * SSM_HEAD_DIM:(h0 + 2) * SSM_HEAD_DIM]
            rhs = jnp.concatenate([jnp.where(lo_half, xpair, 0.0),
                                   jnp.where(lo_half, 0.0, xpair)], axis=0).astype(BF16)
            y_d = jnp.dot(m2, rhs, preferred_element_type=F32)
            lo = 2 * pr * SSM_HEAD_DIM
            y_parts.append(y_d + y_off[:, lo:lo + LANES] * eax[:, g * hpg + lo:g * hpg + lo + LANES])
        upd = jnp.dot(bg.T.astype(BF16), xd[:, g * hpg:(g + 1) * hpg].astype(BF16),
                      preferred_element_type=F32)
        st_sc[:, g * hpg:(g + 1) * hpg] = st_g * cdx[:, g * hpg:(g + 1) * hpg] + upd

    y = jnp.concatenate(y_parts, axis=1) + xs * dsk_ref[...]
    yg = _gate_norm(y, z_ref[...], gn_ref[...], SSM_GROUPS)
    h_ref[...] = x_ref[...] + jnp.dot(yg.astype(BF16), wout_ref[...], preferred_element_type=F32)

    @pl.when(c == pl.num_programs(1) - 1)
    def _():
        sfin_ref[0] = st_sc[...].T
        cfin_ref[0] = jnp.concatenate([xpad_sc[j, q:q + pad, :] for j in range(n_slab)], axis=1)


def _ssd_prompt(z, xbc, dt, x, cw, cb, dtb, alog, dsk, gnw, tril, e01, wout, bp, seq):
    nc = seq // SSM_CHUNK
    q = SSM_CHUNK
    di, cd, d = z.shape[1], xbc.shape[1], x.shape[1]
    rows = lambda b, c: (b * nc + c, 0)
    return pl.pallas_call(
        _ssd_prompt_kernel,
        grid=(bp, nc),
        in_specs=[
            pl.BlockSpec((q, di), rows), pl.BlockSpec((q, cd), rows), pl.BlockSpec((q, LANES), rows),
            pl.BlockSpec((q, d), rows),
            _const_spec(cw.shape), _const_spec(cb.shape), _const_spec(dtb.shape), _const_spec(alog.shape),
            _const_spec(dsk.shape), _const_spec(gnw.shape), _const_spec(tril.shape), _const_spec(e01.shape),
            _const_spec(wout.shape),
        ],
        out_specs=[
            pl.BlockSpec((q, d), rows),
            pl.BlockSpec((1, di, SSM_STATE), lambda b, c: (b, 0, 0)),
            pl.BlockSpec((1, SUBLANES, cd), lambda b, c: (b, 0, 0)),
        ],
        out_shape=[jax.ShapeDtypeStruct((bp * seq, d), F32),
                   jax.ShapeDtypeStruct((bp, di, SSM_STATE), F32),
                   jax.ShapeDtypeStruct((bp, SUBLANES, cd), F32)],
        scratch_shapes=[pltpu.VMEM((cd // LANES, q + 2 * SUBLANES, LANES), F32), pltpu.VMEM((SSM_STATE, di), F32)],
        compiler_params=_cparams(("arbitrary", "arbitrary")),
        name="ssd_prompt",
    )(z, xbc, dt, x, cw, cb, dtb, alog, dsk, gnw, tril, e01, wout)


def _ssd_sample_kernel(z_ref, xp_ref, dt_ref, s0_ref, x_ref, cw_ref, cb_ref, dtb_ref, alog_ref, dsk_ref, gn_ref,
                       e_ref, g1_ref, wout_ref, h_ref, sn_ref, yoff_sc, *, n_tok):
    hf = pl.program_id(1)
    nb = SEQ_PER_STEP
    half = nb // 2
    q = n_tok * nb
    di = z_ref.shape[1]
    gn_w = SSM_GROUPS * SSM_STATE
    hpg = di // SSM_GROUPS

    taps = [xp_ref[0, m] for m in range(n_tok + SSM_CONV - 1)]
    slabs = []
    for t in range(n_tok):
        acc = cb_ref[...]
        for k in range(SSM_CONV):
            acc = acc + taps[t + k] * cw_ref[k:k + 1, :]
        slabs.append(_silu(acc))
    xc = jnp.concatenate(slabs, axis=0)
    xs = xc[:, :di]
    bm = xc[:, di:di + gn_w]
    cm = xc[:, di + gn_w:]

    dt = _softplus(dt_ref[...] + dtb_ref[...])
    da = dt * (-jnp.exp(alog_ref[...]))
    acts = []
    run = None
    for t in range(n_tok):
        d = da[t * nb:(t + 1) * nb, :]
        run = d if run is None else run + d
        acts.append(run)
    act = jnp.concatenate(acts, axis=0)
    act_last = jnp.concatenate([acts[-1]] * n_tok, axis=0)
    e01 = e_ref[...]
    xdt = xs * _mm01(dt, e01)
    xd = xdt * _mm01(jnp.exp(act_last - act), e01)
    eax = _mm01(jnp.exp(act), e01)
    cdx = _mm01(jnp.exp(acts[-1]), e01)

    pairs = [(t, u) for t in range(n_tok) for u in range(t + 1)]
    cbp = jnp.concatenate([cm[t * nb:(t + 1) * nb, :] * bm[u * nb:(u + 1) * nb, :] for t, u in pairs], axis=0)
    seg = jnp.concatenate([acts[t] - acts[u] for t, u in pairs], axis=0)
    coef = _mm01(_mm01(cbp, g1_ref[...]) * jnp.exp(seg), e01)
    y_slabs = []
    for t in range(n_tok):
        acc = None
        for pi, (tt, u) in enumerate(pairs):
            if tt != t:
                continue
            term = coef[pi * nb:(pi + 1) * nb, :] * xdt[u * nb:(u + 1) * nb, :]
            acc = term if acc is None else acc + term
        y_slabs.append(acc)
    y_diag = jnp.concatenate(y_slabs, axis=0)

    zpad = jnp.concatenate([xd,
                            jnp.where(hf == 0, cdx[0:half, :], cdx[half:nb, :]),
                            jnp.zeros((LANES - q - half, di), F32)], axis=0)
    zt = zpad.T
    row_seq = lax.broadcasted_iota(I32, (q, 1), 0) % nb
    cm_b = cm.astype(BF16)
    y_off_g = [None] * SSM_GROUPS
    for sl in range(half):
        in_seq = row_seq == hf * half + sl
        for g in range(SSM_GROUPS):
            s_old = s0_ref[sl, g * hpg:(g + 1) * hpg, :]
            c_g = jnp.where(in_seq, cm_b[:, g * SSM_STATE:(g + 1) * SSM_STATE], jnp.zeros((), BF16))
            yo = _dot_nt(c_g, s_old.astype(BF16))
            y_off_g[g] = yo if y_off_g[g] is None else y_off_g[g] + yo
            b_g = jnp.where(in_seq, bm[:, g * SSM_STATE:(g + 1) * SSM_STATE], 0.0)
            b_pad = jnp.concatenate([b_g, jnp.zeros((LANES - q, SSM_STATE), F32)], axis=0).astype(BF16)
            zt_g = zt[g * hpg:(g + 1) * hpg, :]
            upd = jnp.dot(zt_g.astype(BF16), b_pad, preferred_element_type=F32)
            decay = zt_g[:, q + sl:q + sl + 1]
            sn_ref[sl, g * hpg:(g + 1) * hpg, :] = s_old * decay + upd
    y_off = jnp.concatenate(y_off_g, axis=1)

    @pl.when(hf == 0)
    def _():
        yoff_sc[...] = y_off

    @pl.when(hf == 1)
    def _():
        y = y_diag + (yoff_sc[...] + y_off) * eax + xs * dsk_ref[...]
        yg = _gate_norm(y, z_ref[...], gn_ref[...], SSM_GROUPS)
        h_ref[...] = x_ref[...] + jnp.dot(yg.astype(BF16), wout_ref[...], preferred_element_type=F32)


def _ssd_sample(z, xp7, dt, s0, x, cw, cb, dtb, alog, dsk, gnw, e01, g1, wout, tp, n_tok):
    nsteps = xp7.shape[0]
    nb = SEQ_PER_STEP
    half = nb // 2
    q = n_tok * nb
    di, d = z.shape[1], x.shape[1]
    cd = xp7.shape[3]
    base = tp // q
    rows = lambda s, hf: (base + s, 0)
    return pl.pallas_call(
        functools.partial(_ssd_sample_kernel, n_tok=n_tok),
        grid=(nsteps, 2),
        in_specs=[
            pl.BlockSpec((q, di), rows),
            pl.BlockSpec((1, n_tok + SSM_CONV - 1, nb, cd), lambda s, hf: (s, 0, 0, 0)),
            pl.BlockSpec((q, LANES), rows),
            pl.BlockSpec((half, di, SSM_STATE), lambda s, hf: (2 * s + hf, 0, 0)),
            pl.BlockSpec((q, d), lambda s, hf: (s, 0)),
            _const_spec(cw.shape), _const_spec(cb.shape), _const_spec(dtb.shape), _const_spec(alog.shape),
            _const_spec(dsk.shape), _const_spec(gnw.shape), _const_spec(e01.shape), _const_spec(g1.shape),
            _const_spec(wout.shape),
        ],
        out_specs=[
            pl.BlockSpec((q, d), lambda s, hf: (s, 0)),
            pl.BlockSpec((half, di, SSM_STATE), lambda s, hf: (2 * s + hf, 0, 0)),
        ],
        out_shape=[jax.ShapeDtypeStruct((nsteps * q, d), F32),
                   jax.ShapeDtypeStruct(s0.shape, F32)],
        scratch_shapes=[pltpu.VMEM((q, di), F32)],
        compiler_params=_cparams(("arbitrary", "arbitrary")),
        name="ssd_sample",
    )(z, xp7, dt, s0, x, cw, cb, dtb, alog, dsk, gnw, e01, g1, wout)


def _qkv_math(h, kvn_ref, an_ref, wkv_ref, wq_ref, kn_ref, qn_ref, rc_ref, rs1_ref, rs2_ref,
              hsum_ref, rq_ref, eq_ref, q_ref, k_ref, v_ref):
    hn = _rms_scale(h)
    kvw = k_ref.shape[1]
    kv = jnp.dot((hn * kvn_ref[...]).astype(BF16), wkv_ref[...], preferred_element_type=F32)
    k = kv[:, :kvw]
    v_ref[...] = kv[:, kvw:]
    inv_hd = 1.0 / ATT_HEAD_DIM
    k = k * lax.rsqrt(_mm01(k * k, hsum_ref[...], 2) * inv_hd + EPS) * kn_ref[...]
    rc, rs1, rs2 = rc_ref[...], rs1_ref[...], rs2_ref[...]
    rk = kvw // LANES
    k_ref[...] = _rope(k, _tile_lanes(rc, rk), _tile_lanes(rs1, rk), _tile_lanes(rs2, rk))
    q = jnp.dot((hn * an_ref[...]).astype(BF16), wq_ref[...], preferred_element_type=F32)
    rsq = lax.rsqrt(_mm01(q * q, rq_ref[...], 2) * inv_hd + EPS)
    q = q * _mm01(rsq, eq_ref[...], 2) * qn_ref[...]
    rq = q.shape[1] // LANES
    q_ref[...] = _rope(q, _tile_lanes(rc, rq), _tile_lanes(rs1, rq), _tile_lanes(rs2, rq))


def _attn_prompt_kernel(sink_ref, q_ref, kc_ref, kp_ref, vc_ref, vp_ref, o_ref):
    i = pl.program_id(1)
    w = WINDOW
    hd = ATT_HEAD_DIM
    n_q = q_ref.shape[1] // hd
    grp = n_q // ATT_KV_HEADS
    pair = LANES // hd
    rows = pair * w
    row = lax.broadcasted_iota(I32, (rows, 2 * w), 0) % w
    col = lax.broadcasted_iota(I32, (rows, 2 * w), 1)
    dist = row + w - col
    mask = (dist >= 0) & (dist < w) & ((col >= w) | (i > 0))
    first_head = lax.broadcasted_iota(I32, (rows, 1), 0) < w
    lo_lanes = lax.broadcasted_iota(I32, (w, LANES), 1) < hd
    zeros_kv = jnp.zeros((2 * w, hd), BF16)
    outs = []
    for g in range(ATT_KV_HEADS):
        sl = slice(g * hd, (g + 1) * hd)
        kk = jnp.concatenate([kp_ref[:, sl], kc_ref[:, sl]], axis=0).astype(BF16)
        vv = jnp.concatenate([vp_ref[:, sl], vc_ref[:, sl]], axis=0).astype(BF16)
        k2 = jnp.concatenate([kk, kk], axis=1)
        v_lo = jnp.concatenate([vv, zeros_kv], axis=1)
        v_hi = jnp.concatenate([zeros_kv, vv], axis=1)
        for h0 in range(g * grp, (g + 1) * grp, pair):
            qp = q_ref[:, h0 * hd:(h0 + pair) * hd] * (hd ** -0.5)
            qs = jnp.concatenate([jnp.where(lo_lanes, qp, 0.0), jnp.where(lo_lanes, 0.0, qp)],
                                 axis=0).astype(BF16)
            s = jnp.where(mask, _dot_nt(qs, k2), -jnp.inf)
            sink = jnp.where(first_head, sink_ref[h0], sink_ref[h0 + 1])
            m = jnp.maximum(jnp.max(s, axis=-1, keepdims=True), sink)
            p = jnp.exp(s - m)
            inv = 1.0 / (jnp.sum(p, axis=-1, keepdims=True) + jnp.exp(sink - m))
            pb = p.astype(BF16)
            outs.append(jnp.dot(pb[0:w], v_lo, preferred_element_type=F32) * inv[0:w]
                        + jnp.dot(pb[w:2 * w], v_hi, preferred_element_type=F32) * inv[w:2 * w])
    o_ref[...] = jnp.concatenate(outs, axis=1)


def _attn_prompt(sinks, q, k, v, bp, seq):
    w = WINDOW
    nb = seq // w
    qw, kvw = q.shape[1], k.shape[1]
    cur = lambda b, i: (b * nb + i, 0)
    prev = lambda b, i: (b * nb + jnp.maximum(i - 1, 0), 0)
    return pl.pallas_call(
        _attn_prompt_kernel,
        grid=(bp, nb),
        in_specs=[pl.BlockSpec(memory_space=pltpu.SMEM),
                  pl.BlockSpec((w, qw), cur), pl.BlockSpec((w, kvw), cur), pl.BlockSpec((w, kvw), prev),
                  pl.BlockSpec((w, kvw), cur), pl.BlockSpec((w, kvw), prev)],
        out_specs=pl.BlockSpec((w, qw), cur),
        out_shape=jax.ShapeDtypeStruct((bp * seq, qw), F32),
        compiler_params=_cparams(("arbitrary", "arbitrary")),
        name="attn_prompt",
    )(sinks, q, k, k, v, v)


def _attn_sample_kernel(sink_ref, q_ref, kn_ref, vn_ref, kc_ref, vc_ref, o_ref, kw_ref, vw_ref, *, n_tok):
    nb = SEQ_PER_STEP
    qn = n_tok * nb
    hd = ATT_HEAD_DIM
    lc = kc_ref.shape[1]
    for new_ref, old_ref, win_ref in ((kn_ref, kc_ref, kw_ref), (vn_ref, vc_ref, vw_ref)):
        new = new_ref[...]
        for sq in range(nb):
            win_ref[sq, 0:lc - n_tok, :] = old_ref[sq, n_tok:lc, :]
            win_ref[sq, lc - n_tok:lc, :] = jnp.concatenate(
                [new[t * nb + sq:t * nb + sq + 1, :] for t in range(n_tok)], axis=0)
    n_q = q_ref.shape[1] // hd
    grp = n_q // ATT_KV_HEADS
    rows = grp * qn
    r = lax.broadcasted_iota(I32, (rows, 1), 0)
    r_seq = r % nb
    r_tok = (r % qn) // nb
    ccol = lax.broadcasted_iota(I32, (rows, lc), 1)
    mask_c = ccol >= r_tok + 1 + (lc - WINDOW)
    ncol = lax.broadcasted_iota(I32, (rows, LANES), 1)
    mask_n = (ncol < qn) & (ncol % nb == r_seq) & (ncol // nb <= r_tok)
    q = q_ref[...]
    zpad = jnp.zeros((LANES - qn, hd), F32)
    outs = [None] * n_q
    for g in range(ATT_KV_HEADS):
        sl = slice(g * hd, (g + 1) * hd)
        qs = jnp.concatenate([q[:, (g * grp + hq) * hd:(g * grp + hq + 1) * hd] for hq in range(grp)],
                             axis=0).astype(BF16)
        k_new = jnp.concatenate([kn_ref[:, sl], zpad], axis=0).astype(BF16)
        v_new = jnp.concatenate([vn_ref[:, sl], zpad], axis=0).astype(BF16)
        s_n = jnp.where(mask_n, _dot_nt(qs, k_new) * (hd ** -0.5), -jnp.inf)
        s_c = jnp.zeros((rows, lc), F32)
        for sq in range(nb):
            s_sq = _dot_nt(qs, kc_ref[sq, :, sl].astype(BF16))
            s_c = jnp.where(r_seq == sq, s_sq, s_c)
        s_c = jnp.where(mask_c, s_c * (hd ** -0.5), -jnp.inf)
        sink = jnp.zeros((rows, 1), F32)
        for hq in range(grp):
            sink = jnp.where(r // qn == hq, sink_ref[g * grp + hq], sink)
        m = jnp.maximum(jnp.maximum(jnp.max(s_c, axis=-1, keepdims=True),
                                    jnp.max(s_n, axis=-1, keepdims=True)), sink)
        p_c = jnp.exp(s_c - m)
        p_n = jnp.exp(s_n - m)
        denom = (jnp.sum(p_c, axis=-1, keepdims=True) + jnp.sum(p_n, axis=-1, keepdims=True)
                 + jnp.exp(sink - m))
        p_c = p_c / denom
        o = jnp.dot((p_n / denom).astype(BF16), v_new, preferred_element_type=F32)
        for sq in range(nb):
            o = o + jnp.dot(jnp.where(r_seq == sq, p_c, 0.0).astype(BF16), vc_ref[sq, :, sl].astype(BF16),
                            preferred_element_type=F32)
        for hq in range(grp):
            outs[g * grp + hq] = o[hq * qn:(hq + 1) * qn, :]
    o_ref[...] = jnp.concatenate(outs, axis=1)


def _attn_sample(sinks, q, k, v, kc, vc, tp, n_tok):
    nb = SEQ_PER_STEP
    qn = n_tok * nb
    nsteps = kc.shape[0] // nb
    lc, kvw = kc.shape[1], kc.shape[2]
    qw = q.shape[1]
    base = tp // qn
    rows = lambda s: (base + s, 0)
    return pl.pallas_call(
        functools.partial(_attn_sample_kernel, n_tok=n_tok),
        grid=(nsteps,),
        in_specs=[pl.BlockSpec(memory_space=pltpu.SMEM),
                  pl.BlockSpec((qn, qw), rows), pl.BlockSpec((qn, kvw), rows), pl.BlockSpec((qn, kvw), rows),
                  pl.BlockSpec((nb, lc, kvw), lambda s: (s, 0, 0)), pl.BlockSpec((nb, lc, kvw), lambda s: (s, 0, 0))],
        out_specs=[pl.BlockSpec((qn, qw), lambda s: (s, 0)),
                   pl.BlockSpec((nb, lc, kvw), lambda s: (s, 0, 0)), pl.BlockSpec((nb, lc, kvw), lambda s: (s, 0, 0))],
        out_shape=[jax.ShapeDtypeStruct((nsteps * qn, qw), F32),
                   jax.ShapeDtypeStruct(kc.shape, F32), jax.ShapeDtypeStruct(vc.shape, F32)],
        compiler_params=_cparams(("arbitrary",)),
        name="attn_sample",
    )(sinks, q, k, v, kc, vc)


def _router_kernel(hp_ref, hs_ref, fn_ref, wh_ref, wl_ref, b_ref, triu_ref, xn_ref, info_ref, info_t_ref, cnt_ref,
                   carry_sc, *, n_p):
    h = jnp.where(pl.program_id(0) < n_p, hp_ref[...], hs_ref[...])
    _router_math(h, fn_ref, wh_ref, wl_ref, b_ref, triu_ref, xn_ref, info_ref, info_t_ref, cnt_ref, carry_sc)


def _wo_router_kernel(op_ref, os_ref, res_ref, wo_ref, fn_ref, wh_ref, wl_ref, b_ref, triu_ref,
                      h_ref, xn_ref, info_ref, info_t_ref, cnt_ref, carry_sc, *, n_p):
    o = jnp.where(pl.program_id(0) < n_p, op_ref[...], os_ref[...])
    h = res_ref[...] + jnp.dot(o.astype(BF16), wo_ref[...], preferred_element_type=F32)
    h_ref[...] = h
    _router_math(h, fn_ref, wh_ref, wl_ref, b_ref, triu_ref, xn_ref, info_ref, info_t_ref, cnt_ref, carry_sc)


def _router_math(h, fn_ref, wh_ref, wl_ref, b_ref, triu_ref, xn_ref, info_ref, info_t_ref, cnt_ref, carry_sc):
    i = pl.program_id(0)

    @pl.when(i == 0)
    def _():
        carry_sc[...] = jnp.zeros(carry_sc.shape, F32)

    xn = _rms_scale(h) * fn_ref[...]
    _store_token_major(xn_ref, xn)
    x_hi, x_lo = _split_bf16(xn, 2)
    wh, wl = wh_ref[...], wl_ref[...]
    logits = (jnp.dot(x_hi, wh, preferred_element_type=F32) + jnp.dot(x_hi, wl, preferred_element_type=F32)
              + jnp.dot(x_lo, wh, preferred_element_type=F32)) + b_ref[...]
    tm = logits.shape[0]
    lt = logits.T
    per = MOE_EXPERTS_PER_GROUP
    row = lax.broadcasted_iota(I32, (per, tm), 0).astype(F32)
    big = float(LANES)
    neg = -jnp.inf

    lg = jnp.where(row < MOE_GROUPS, lt[MOE_EXPERTS:MOE_EXPERTS + per, :], neg)
    mg = jnp.max(lg, axis=0, keepdims=True)
    gp = 1.0 / jnp.sum(jnp.exp(lg - mg), axis=0, keepdims=True)
    gi = jnp.min(jnp.where(lg == mg, row, big), axis=0, keepdims=True)

    le = lt[0:per, :]
    for g in range(1, MOE_GROUPS):
        le = jnp.where(gi == g, lt[g * per:(g + 1) * per, :], le)
    m1 = jnp.max(le, axis=0, keepdims=True)
    i1 = jnp.min(jnp.where(le == m1, row, big), axis=0, keepdims=True)
    le2 = jnp.where(row == i1, neg, le)
    m2 = jnp.max(le2, axis=0, keepdims=True)
    i2 = jnp.min(jnp.where(le2 == m2, row, big), axis=0, keepdims=True)
    e2 = jnp.exp(m2 - m1)
    g1 = gp * (1.0 / (1.0 + e2))
    g2 = gp * (e2 / (1.0 + e2))
    x1 = gi * per + i1
    x2 = gi * per + i2

    row_e = lax.broadcasted_iota(I32, (LANES, tm), 0).astype(F32)
    a1 = row_e == x1
    a2 = row_e == x2
    onehot = jnp.where(a1 | a2, 1.0, 0.0)
    before = jnp.dot(onehot.astype(BF16), triu_ref[...], preferred_element_type=F32) + carry_sc[:, 0:1]
    r1 = jnp.sum(jnp.where(a1, before, 0.0), axis=0, keepdims=True)
    r2 = jnp.sum(jnp.where(a2, before, 0.0), axis=0, keepdims=True)
    carry_sc[...] = carry_sc[...] + jnp.sum(onehot, axis=1, keepdims=True)
    cnt_ref[...] = carry_sc[...]

    info_t = jnp.zeros((SUBLANES, tm), F32)
    field = lax.broadcasted_iota(I32, (SUBLANES, tm), 0)
    for k, val in enumerate((x1, x2, g1, g2, r1, r2)):
        info_t = jnp.where(field == k, val, info_t)
    info_t_ref[...] = info_t
    info_ref[...] = jnp.concatenate([info_t, jnp.zeros((LANES - SUBLANES, tm), F32)], axis=0).T


def _two_source_specs(tm, width, n_p):
    return [pl.BlockSpec((tm, width), lambda i: (jnp.minimum(i, n_p - 1), 0)),
            pl.BlockSpec((tm, width), lambda i: (jnp.maximum(i - n_p, 0), 0))]


def _router(h_p, h_s, fn, wh, wl, b, tril):
    d = h_p.shape[1]
    tm = ROUTER_TILE
    n_p = h_p.shape[0] // tm
    t = h_p.shape[0] + h_s.shape[0]
    rows = lambda i: (i, 0)
    return pl.pallas_call(
        functools.partial(_router_kernel, n_p=n_p),
        grid=(t // tm,),
        in_specs=_two_source_specs(tm, d, n_p) + [_const_spec(a.shape) for a in (fn, wh, wl, b, tril)],
        out_specs=[pl.BlockSpec((tm * ROW_TILES, LANES), rows), pl.BlockSpec((tm, LANES), rows),
                   pl.BlockSpec((SUBLANES, tm), lambda i: (0, i)), _const_spec((LANES, LANES))],
        out_shape=[jax.ShapeDtypeStruct((t * ROW_TILES, LANES), F32), jax.ShapeDtypeStruct((t, LANES), F32),
                   jax.ShapeDtypeStruct((SUBLANES, t), F32), jax.ShapeDtypeStruct((LANES, LANES), F32)],
        scratch_shapes=[pltpu.VMEM((LANES, LANES), F32)],
        compiler_params=_cparams(("arbitrary",)),
        name="moe_router",
    )(h_p, h_s, fn, wh, wl, b, tril)


def _wo_router(o_p, o_s, res, wo, fn, wh, wl, b, tril):
    t, d = res.shape
    tm = ROUTER_TILE
    n_p = o_p.shape[0] // tm
    rows = lambda i: (i, 0)
    return pl.pallas_call(
        functools.partial(_wo_router_kernel, n_p=n_p),
        grid=(t // tm,),
        in_specs=_two_source_specs(tm, o_p.shape[1], n_p) + [pl.BlockSpec((tm, d), rows)]
        + [_const_spec(a.shape) for a in (wo, fn, wh, wl, b, tril)],
        out_specs=[pl.BlockSpec((tm, d), rows), pl.BlockSpec((tm * ROW_TILES, LANES), rows),
                   pl.BlockSpec((tm, LANES), rows), pl.BlockSpec((SUBLANES, tm), lambda i: (0, i)),
                   _const_spec((LANES, LANES))],
        out_shape=[jax.ShapeDtypeStruct((t, d), F32), jax.ShapeDtypeStruct((t * ROW_TILES, LANES), F32),
                   jax.ShapeDtypeStruct((t, LANES), F32), jax.ShapeDtypeStruct((SUBLANES, t), F32),
                   jax.ShapeDtypeStruct((LANES, LANES), F32)],
        scratch_shapes=[pltpu.VMEM((LANES, LANES), F32)],
        compiler_params=_cparams(("arbitrary",)),
        name="wo_router",
    )(o_p, o_s, res, wo, fn, wh, wl, b, tril)


def _store_token_major(ref, x):
    n = x.shape[0]
    for j in range(ROW_TILES):
        ref[pl.ds(j, n, stride=ROW_TILES), :] = x[:, j * LANES:(j + 1) * LANES]


def _load_token_major(ref, n):
    return jnp.concatenate([ref[pl.ds(j, n, stride=ROW_TILES), :] for j in range(ROW_TILES)], axis=1)


def _dest_kernel(pst_ref, info_ref, dest_ref):
    info = info_ref[...]
    e = info[0:2, :]
    start = jnp.zeros(e.shape, F32)
    for k in range(MOE_EXPERTS):
        start = jnp.where(e == k, pst_ref[k].astype(F32), start)
    dest = (start + info[4:6, :]).astype(I32)
    dest_ref[...] = jnp.concatenate([dest, jnp.zeros((SUBLANES - 2, dest.shape[1]), I32)], axis=0)


def _dest(pstarts, info_t):
    return pl.pallas_call(
        _dest_kernel,
        in_specs=[pl.BlockSpec(memory_space=pltpu.SMEM), pl.BlockSpec(memory_space=pltpu.VMEM)],
        out_specs=pl.BlockSpec(memory_space=pltpu.VMEM),
        out_shape=jax.ShapeDtypeStruct(info_t.shape, I32),
        name="moe_dest",
    )(pstarts, info_t)


def _tile_copy(src, src_row, dst, dst_row, sem):
    return pltpu.make_async_copy(src.at[pl.ds(pl.multiple_of(src_row * ROW_TILES, ROW_TILES), ROW_TILES)],
                                 dst.at[pl.ds(pl.multiple_of(dst_row * ROW_TILES, ROW_TILES), ROW_TILES)], sem)


def _wait_tiles(ref, n_tokens, sem):
    blk = ref.at[pl.ds(0, n_tokens * ROW_TILES)]
    pltpu.make_async_copy(blk, blk, sem).wait()


def _dispatch_kernel(d1_ref, d2_ref, pend_ref, pad_ref, xn_ref, xbuf_hbm, zero_sc, sem_z, sem):
    i = pl.program_id(0)
    tm = xn_ref.shape[0] // ROW_TILES
    blk_rows = zero_sc.shape[0]
    first_unused = pend_ref[MOE_EXPERTS - 1] * ROW_TILES // blk_rows
    n_blocks = xbuf_hbm.shape[0] // blk_rows
    pieces = [EXPERT_ROWS >> k for k in range(1, EXPERT_ROWS.bit_length())]

    def pad_copies(e):
        pad = pad_ref[e]
        off = pend_ref[e] - pad
        out = []
        for n in pieces:
            used = (pad & n) != 0
            dst = xbuf_hbm.at[pl.ds(pl.multiple_of(off * ROW_TILES, ROW_TILES), n * ROW_TILES)]
            out.append((used, pltpu.make_async_copy(zero_sc.at[pl.ds(0, n * ROW_TILES)], dst, sem_z)))
            off = off + jnp.where(used, n, 0)
        return out

    def tail_copy(b):
        dst = xbuf_hbm.at[pl.ds(pl.multiple_of(b * blk_rows, blk_rows), blk_rows)]
        return pltpu.make_async_copy(zero_sc, dst, sem_z)

    def clear(op):
        for e in range(MOE_EXPERTS):
            for used, cp in pad_copies(e):
                @pl.when(used)
                def _():
                    op(cp)

        def tail(b, carry):
            op(tail_copy(b))
            return carry
        lax.fori_loop(first_unused, n_blocks, tail, 0)

    @pl.when(i == 0)
    def _():
        zero_sc[...] = jnp.zeros(zero_sc.shape, F32)
        clear(lambda cp: cp.start())

    def body(r, carry):
        t = i * tm + r
        _tile_copy(xn_ref, r, xbuf_hbm, d1_ref[t], sem).start(priority=0)
        _tile_copy(xn_ref, r, xbuf_hbm, d2_ref[t], sem).start(priority=1)
        return carry
    lax.fori_loop(0, tm, body, 0, unroll=8)
    _wait_tiles(xn_ref, tm, sem)
    _wait_tiles(xn_ref, tm, sem)

    @pl.when(i == 0)
    def _():
        clear(lambda cp: cp.wait())


def _dispatch(dest1, dest2, pends, pads, xn_tm, n_rows):
    n_tok = dest1.shape[0]
    tm = DISPATCH_TILE
    grid_spec = pltpu.PrefetchScalarGridSpec(
        num_scalar_prefetch=4,
        grid=(n_tok // tm,),
        in_specs=[pl.BlockSpec((tm * ROW_TILES, LANES), lambda i, *_: (i, 0))],
        out_specs=pl.BlockSpec(memory_space=pl.ANY),
        scratch_shapes=[pltpu.VMEM((EXPERT_ROWS * ROW_TILES, LANES), F32), pltpu.SemaphoreType.DMA(()),
                        pltpu.SemaphoreType.DMA(())],
    )
    return pl.pallas_call(
        _dispatch_kernel,
        grid_spec=grid_spec,
        out_shape=jax.ShapeDtypeStruct((n_rows * ROW_TILES, LANES), F32),
        compiler_params=_cparams(("arbitrary",)),
        name="moe_dispatch",
    )(dest1, dest2, pends, pads, xn_tm)


def _expert_kernel(blk_e_ref, nvalid_ref, next_e_ref, x_hbm, wg_hbm, wu_hbm, wd_hbm, y_ref,
                   wg_sc, wu_sc, wd_sc, wg_st, wu_st, wd_st, x_sc, sem, xsem, *, layer):
    i = pl.program_id(0)
    nv = nvalid_ref[0]
    blk_rows = x_sc.shape[1]
    rows = blk_rows // ROW_TILES

    def weight_copies(e):
        return (pltpu.make_async_copy(wg_hbm.at[layer, e], wg_st, sem.at[0]),
                pltpu.make_async_copy(wu_hbm.at[layer, e], wu_st, sem.at[1]),
                pltpu.make_async_copy(wd_hbm.at[layer, e], wd_st, sem.at[2]))

    def row_copy(step):
        slot = step % EXPERT_RING
        src = x_hbm.at[pl.ds(pl.multiple_of(step * blk_rows, blk_rows), blk_rows)]
        return pltpu.make_async_copy(src, x_sc.at[slot], xsem.at[slot])

    @pl.when((i == 0) & (nv > 0))
    def _():
        for c in weight_copies(blk_e_ref[0]):
            c.start()
        for s in range(EXPERT_RING - 1):
            @pl.when(s < nv)
            def _():
                row_copy(s).start()

    @pl.when(i + EXPERT_RING - 1 < nv)
    def _():
        row_copy(i + EXPERT_RING - 1).start()

    @pl.when(i < nv)
    def _():
        e = blk_e_ref[i]
        e_prev = blk_e_ref[jnp.maximum(i - 1, 0)]
        row_copy(i).wait()
        x_ref = x_sc.at[i % EXPERT_RING]

        @pl.when((i == 0) | (e != e_prev))
        def _():
            for c in weight_copies(e):
                c.wait()
            wg_sc[...] = wg_st[...].astype(BF16)
            wu_sc[...] = wu_st[...].astype(BF16)
            wd_sc[...] = wd_st[...].astype(BF16)
            nxt = next_e_ref[e]

            @pl.when(nxt < MOE_EXPERTS)
            def _():
                for c in weight_copies(nxt):
                    c.start()

        x = _load_token_major(x_ref, rows).astype(BF16)
        hid = _silu(jnp.dot(x, wg_sc[...], preferred_element_type=F32)) * jnp.dot(
            x, wu_sc[...], preferred_element_type=F32)
        _store_token_major(y_ref, jnp.dot(hid.astype(BF16), wd_sc[...], preferred_element_type=F32))

    @pl.when(i >= nv)
    def _():
        y_ref[...] = jnp.zeros(y_ref.shape, F32)


def _experts(blk_e, nvalid, next_e, xbuf, wg, wu, wd, layer):
    nblk = blk_e.shape[0]
    d, hdim = wg.shape[2], wg.shape[3]
    rows = EXPERT_ROWS * ROW_TILES
    grid_spec = pltpu.PrefetchScalarGridSpec(
        num_scalar_prefetch=3,
        grid=(nblk,),
        in_specs=[pl.BlockSpec(memory_space=pl.ANY),
                  pl.BlockSpec(memory_space=pl.ANY), pl.BlockSpec(memory_space=pl.ANY),
                  pl.BlockSpec(memory_space=pl.ANY)],
        out_specs=pl.BlockSpec((rows, LANES), lambda i, be, nv, ne: (i, 0)),
        scratch_shapes=[pltpu.VMEM((d, hdim), BF16), pltpu.VMEM((d, hdim), BF16), pltpu.VMEM((hdim, d), BF16),
                        pltpu.VMEM((d, hdim), F32), pltpu.VMEM((d, hdim), F32), pltpu.VMEM((hdim, d), F32),
                        pltpu.VMEM((EXPERT_RING, rows, LANES), F32),
                        pltpu.SemaphoreType.DMA((3,)), pltpu.SemaphoreType.DMA((EXPERT_RING,))],
    )
    return pl.pallas_call(
        functools.partial(_expert_kernel, layer=layer),
        grid_spec=grid_spec,
        out_shape=jax.ShapeDtypeStruct((nblk * rows, LANES), F32),
        compiler_params=_cparams(("arbitrary",)),
        name="moe_experts",
    )(blk_e, nvalid, next_e, xbuf, wg, wu, wd)


def _gather_moe_rows(d1_ref, d2_ref, y_hbm, r_sc, sem, tm, row0=0):
    i = pl.program_id(0)
    n = pl.num_programs(0)

    def start(step):
        slot = step % 2
        base = row0 + step * tm

        def body(r, carry):
            _tile_copy(y_hbm, d1_ref[base + r], r_sc.at[slot, 0], r, sem.at[slot]).start(priority=0)
            _tile_copy(y_hbm, d2_ref[base + r], r_sc.at[slot, 1], r, sem.at[slot]).start(priority=1)
            return carry
        lax.fori_loop(0, tm, body, 0, unroll=8)

    @pl.when(i == 0)
    def _():
        start(i)

    @pl.when(i + 1 < n)
    def _():
        start(i + 1)

    slot = i % 2
    _wait_tiles(r_sc.at[slot, 0], tm, sem.at[slot])
    _wait_tiles(r_sc.at[slot, 1], tm, sem.at[slot])
    return _load_token_major(r_sc.at[slot, 0], tm), _load_token_major(r_sc.at[slot, 1], tm)


def _combine_kernel(d1_ref, d2_ref, h_ref, info_ref, y_hbm, o_ref, r_sc, sem, *, row0):
    y1, y2 = _gather_moe_rows(d1_ref, d2_ref, y_hbm, r_sc, sem, h_ref.shape[0], row0)
    info = info_ref[...]
    o_ref[...] = h_ref[...] + (y1 * info[:, 2:3] + y2 * info[:, 3:4])


def _combine(dest1, dest2, h, info, ybuf, row0, nrows):
    d = h.shape[1]
    tm = ROUTER_TILE
    base_tile = row0 // tm
    rows = lambda i, a, b: (base_tile + i, 0)
    grid_spec = pltpu.PrefetchScalarGridSpec(
        num_scalar_prefetch=2,
        grid=(nrows // tm,),
        in_specs=[pl.BlockSpec((tm, d), rows), pl.BlockSpec((tm, LANES), rows), pl.BlockSpec(memory_space=pl.ANY)],
        out_specs=pl.BlockSpec((tm, d), lambda i, a, b: (i, 0)),
        scratch_shapes=[pltpu.VMEM((2, 2, tm * ROW_TILES, LANES), F32), pltpu.SemaphoreType.DMA((2,))],
    )
    return pl.pallas_call(
        functools.partial(_combine_kernel, row0=row0),
        grid_spec=grid_spec,
        out_shape=jax.ShapeDtypeStruct((nrows, d), F32),
        compiler_params=_cparams(("arbitrary",)),
        name="moe_combine",
    )(dest1, dest2, h, info, ybuf)


def _combine_qkv_kernel(d1_ref, d2_ref, hp_ref, hs_ref, info_ref, y_hbm, kvn_ref, an_ref, wkv_ref, wq_ref, kn_ref,
                        qn_ref, rc_ref, rs1_ref, rs2_ref, hsum_ref, rq_ref, eq_ref,
                        h_ref, q_ref, k_ref, v_ref, r_sc, sem, *, n_p):
    y1, y2 = _gather_moe_rows(d1_ref, d2_ref, y_hbm, r_sc, sem, h_ref.shape[0])
    info = info_ref[...]
    h = jnp.where(pl.program_id(0) < n_p, hp_ref[...], hs_ref[...]) + (y1 * info[:, 2:3] + y2 * info[:, 3:4])
    h_ref[...] = h
    _qkv_math(h, kvn_ref, an_ref, wkv_ref, wq_ref, kn_ref, qn_ref, rc_ref, rs1_ref, rs2_ref,
              hsum_ref, rq_ref, eq_ref, q_ref, k_ref, v_ref)


def _combine_qkv(dest1, dest2, h_p, h_s, info, ybuf, kvn, an, wkv, wq, knt, qnt, rc, rs1, rs2, hsum, rq, eq):
    d = h_p.shape[1]
    tm = TOKEN_TILE
    n_p = h_p.shape[0] // tm
    t = h_p.shape[0] + h_s.shape[0]
    kvw = wkv.shape[1] // 2
    qw = wq.shape[1]
    rows = lambda i, a, b: (i, 0)
    const = lambda arr: pl.BlockSpec(arr.shape, lambda i, a, b: (0,) * arr.ndim)
    seq_tiles = (rc.shape[0] - h_s.shape[0]) // tm
    rope_rows = lambda i, a, b: (jnp.where(i < n_p, i % seq_tiles, seq_tiles + i - n_p), 0)
    grid_spec = pltpu.PrefetchScalarGridSpec(
        num_scalar_prefetch=2,
        grid=(t // tm,),
        in_specs=[pl.BlockSpec((tm, d), lambda i, a, b: (jnp.minimum(i, n_p - 1), 0)),
                  pl.BlockSpec((tm, d), lambda i, a, b: (jnp.maximum(i - n_p, 0), 0)),
                  pl.BlockSpec((tm, LANES), rows), pl.BlockSpec(memory_space=pl.ANY)]
        + [const(a) for a in (kvn, an, wkv, wq, knt, qnt)] + [pl.BlockSpec((tm, LANES), rope_rows)] * 3
        + [const(a) for a in (hsum, rq, eq)],
        out_specs=[pl.BlockSpec((tm, d), rows), pl.BlockSpec((tm, qw), rows), pl.BlockSpec((tm, kvw), rows),
                   pl.BlockSpec((tm, kvw), rows)],
        scratch_shapes=[pltpu.VMEM((2, 2, tm * ROW_TILES, LANES), F32), pltpu.SemaphoreType.DMA((2,))],
    )
    return pl.pallas_call(
        functools.partial(_combine_qkv_kernel, n_p=n_p),
        grid_spec=grid_spec,
        out_shape=[jax.ShapeDtypeStruct((t, d), F32), jax.ShapeDtypeStruct((t, qw), F32),
                   jax.ShapeDtypeStruct((t, kvw), F32), jax.ShapeDtypeStruct((t, kvw), F32)],
        compiler_params=_cparams(("arbitrary",)),
        name="combine_qkv",
    )(dest1, dest2, h_p, h_s, info, ybuf, kvn, an, wkv, wq, knt, qnt, rc, rs1, rs2, hsum, rq, eq)


def _router_weights(w_grp, b_grp, w_rt, b_rt):
    d = w_rt.shape[0]
    w_cat = jnp.zeros((d, LANES), F32).at[:, :MOE_EXPERTS].set(w_rt).at[:, MOE_EXPERTS:MOE_EXPERTS + MOE_GROUPS].set(w_grp)
    b_cat = jnp.zeros((1, LANES), F32).at[0, :MOE_EXPERTS].set(b_rt).at[0, MOE_EXPERTS:MOE_EXPERTS + MOE_GROUPS].set(b_grp)
    w_hi = w_cat.astype(BF16)
    w_lo = (w_cat - w_hi.astype(F32)).astype(BF16)
    return w_hi, w_lo, b_cat


def _moe_experts(xn_tm, info_t, cnt, wg, wu, wd, layer):
    t = info_t.shape[1]
    counts = cnt[:MOE_EXPERTS, 0].astype(I32)
    pcounts = (counts + EXPERT_ROWS - 1) // EXPERT_ROWS * EXPERT_ROWS
    pends = jnp.cumsum(pcounts)
    pstarts = pends - pcounts
    nblk = -(-(2 * t + MOE_EXPERTS * (EXPERT_ROWS - 1)) // EXPERT_ROWS)
    blk_start = jnp.arange(nblk, dtype=I32) * EXPERT_ROWS
    blk_e = jnp.minimum(jnp.sum((pends[None, :] <= blk_start[:, None]).astype(I32), axis=1), MOE_EXPERTS - 1)
    nvalid = pends[-1:] // EXPERT_ROWS
    eid = jnp.arange(MOE_EXPERTS, dtype=I32)
    later = (eid[None, :] > eid[:, None]) & (pcounts[None, :] > 0)
    next_e = jnp.min(jnp.where(later, eid[None, :], MOE_EXPERTS), axis=1).astype(I32)

    dest = _dest(pstarts, info_t)
    dest1, dest2 = dest[0], dest[1]
    xbuf = _dispatch(dest1, dest2, pends, pcounts - counts, xn_tm, nblk * EXPERT_ROWS)
    return dest1, dest2, _experts(blk_e, nvalid, next_e, xbuf, wg, wu, wd, layer)


def _rope_tables(pos):
    half = ROT_DIM // 2
    inv = ROPE_THETA ** (-np.arange(0, ROT_DIM, 2, dtype=np.float64) / ROT_DIM)
    ang = pos.astype(np.float64)[:, None] * inv[None, :]
    cos, sin = np.cos(ang), np.sin(ang)
    n = pos.shape[0]
    ones = np.ones((n, ATT_HEAD_DIM - ROT_DIM))
    zeros_r = np.zeros((n, ATT_HEAD_DIM - ROT_DIM))
    zeros_h = np.zeros((n, half))
    c = np.concatenate([cos, cos, ones], axis=1)
    s1 = np.concatenate([-sin, zeros_h, zeros_r], axis=1)
    s2 = np.concatenate([zeros_h, sin, zeros_r], axis=1)
    reps = LANES // ATT_HEAD_DIM
    return tuple(jnp.asarray(np.tile(a, (1, reps)).astype(np.float32)) for a in (c, s1, s2))


def _mask01(m):
    return jnp.asarray(m.astype(np.float32), dtype=BF16)


def _to_step_order(a, nsteps, n_tok):
    c = a.shape[-1]
    return a.reshape(nsteps, SEQ_PER_STEP, n_tok, c).transpose(0, 2, 1, 3).reshape(nsteps * n_tok * SEQ_PER_STEP, c)


def _from_step_order(a, nsteps, n_tok):
    c = a.shape[-1]
    return a.reshape(nsteps, n_tok, SEQ_PER_STEP, c).transpose(0, 2, 1, 3).reshape(nsteps * SEQ_PER_STEP, n_tok, c)


def kernel(x_prompt, x_sample, state_ssm, state_conv, cache_k_win, cache_v_win, ssm_norm, ssm_w_in, ssm_conv_w, ssm_conv_b, ssm_dt_bias, ssm_a_log, ssm_d, ssm_gate_norm, ssm_w_out, kv_norm, w_kv, k_norm, attn_norm, w_q, q_norm, sinks, w_o, ffn_norm, moe_w_group, moe_b_group, moe_w_router, moe_b_router, moe_w_gate, moe_w_up, moe_w_down):
    bp, seq, d = x_prompt.shape
    bs, n_tok, _ = x_sample.shape
    tp, ts = bp * seq, bs * n_tok
    nsteps = bs // SEQ_PER_STEP
    n_heads = ssm_d.shape[1]
    di = n_heads * SSM_HEAD_DIM
    gn_w = SSM_GROUPS * SSM_STATE
    cdim = di + 2 * gn_w
    n_q = sinks.shape[1]
    kvw = ATT_KV_HEADS * ATT_HEAD_DIM

    xp2 = x_prompt.reshape(tp, d)
    xs2 = _to_step_order(x_sample, nsteps, n_tok)

    lane_i = np.arange(LANES)
    e01 = _mask01(lane_i[:, None] == (np.arange(di) // SSM_HEAD_DIM)[None, :])
    hpg = di // SSM_GROUPS
    g1 = ((np.arange(gn_w) // SSM_STATE)[:, None] == (lane_i // (hpg // SSM_HEAD_DIM))[None, :])
    g1 = _mask01(g1 & (lane_i < n_heads)[None, :])
    tril_c = _mask01(np.arange(SSM_CHUNK)[:, None] >= np.arange(SSM_CHUNK)[None, :])
    earlier_x = _mask01(np.arange(ROUTER_TILE)[:, None] < np.arange(ROUTER_TILE)[None, :])
    hsum = _mask01((np.arange(kvw) // ATT_HEAD_DIM)[:, None] == (np.arange(kvw) // ATT_HEAD_DIM)[None, :])
    qw = n_q * ATT_HEAD_DIM
    rq_np = (np.arange(qw) // ATT_HEAD_DIM)[:, None] == lane_i[None, :]
    rq, eq = _mask01(rq_np), _mask01(rq_np.T)

    w_in = ssm_w_in[0]
    wz = w_in[:, :di].astype(BF16)
    wx = w_in[:, di:di + cdim].astype(BF16)
    wd = jnp.zeros((d, LANES), F32).at[:, :n_heads].set(w_in[:, di + cdim:]).astype(BF16)
    cw, cb = ssm_conv_w[0], ssm_conv_b[0].reshape(1, cdim)
    z, xbc, dt = _inproj(xp2, xs2, ssm_norm[0].reshape(1, d), wz, wx, wd)

    pad_h = lambda v: jnp.zeros((1, LANES), F32).at[0, :n_heads].set(v)
    dtb, alog = pad_h(ssm_dt_bias[0]), pad_h(ssm_a_log[0])
    dsk = jnp.repeat(ssm_d[0], SSM_HEAD_DIM).reshape(1, di)
    gnw = ssm_gate_norm[0].reshape(1, di)

    w_out = ssm_w_out[0].astype(BF16)
    h_p, s_fin, c_fin = _ssd_prompt(z, xbc, dt, xp2, cw, cb, dtb, alog, dsk, gnw, tril_c, e01, w_out, bp, seq)
    ssm_p = s_fin.reshape(1, bp, n_heads, SSM_HEAD_DIM, SSM_STATE)
    conv_p = c_fin[:, SUBLANES - (SSM_CONV - 1):, :].reshape(1, bp, SSM_CONV - 1, cdim)

    xbc_s = xbc[tp:].reshape(nsteps, n_tok, SEQ_PER_STEP, cdim)
    conv_in = state_conv[0].reshape(nsteps, SEQ_PER_STEP, SSM_CONV - 1, cdim).transpose(0, 2, 1, 3)
    xp7 = jnp.concatenate([conv_in, xbc_s], axis=1)
    s0 = state_ssm[0].reshape(bs, di, SSM_STATE)
    h_s, s_new = _ssd_sample(z, xp7, dt, s0, xs2, cw, cb, dtb, alog, dsk, gnw, e01, g1, w_out, tp, n_tok)
    ssm_s = s_new.reshape(1, bs, n_heads, SSM_HEAD_DIM, SSM_STATE)
    conv_s = _from_step_order(xbc[tp:], nsteps, n_tok)[:, n_tok - (SSM_CONV - 1):, :].reshape(
        1, bs, SSM_CONV - 1, cdim)

    w_hi, w_lo, b_cat = _router_weights(moe_w_group[0], moe_b_group[0], moe_w_router[0], moe_b_router[0])
    xn_tm, info, info_t, cnt = _router(h_p, h_s, ffn_norm[0].reshape(1, d), w_hi, w_lo, b_cat, earlier_x)
    dest1, dest2, ybuf = _moe_experts(xn_tm, info_t, cnt, moe_w_gate, moe_w_up, moe_w_down, 0)

    pos = np.concatenate([np.arange(seq), np.tile(np.repeat(PAST_LEN + np.arange(n_tok), SEQ_PER_STEP), nsteps)])
    rc, rs1, rs2 = _rope_tables(pos)
    h1, q, k, v = _combine_qkv(dest1, dest2, h_p, h_s, info, ybuf, kv_norm.reshape(1, d), attn_norm[0].reshape(1, d),
                               w_kv.astype(BF16), w_q[0].astype(BF16),
                               jnp.tile(k_norm, ATT_KV_HEADS).reshape(1, kvw), jnp.tile(q_norm[0], n_q).reshape(1, qw),
                               rc, rs1, rs2, hsum, rq, eq)
    sk = sinks[0]
    lc = cache_k_win.shape[1]
    kc = cache_k_win.reshape(bs, lc, kvw)
    vc = cache_v_win.reshape(bs, lc, kvw)
    o_p = _attn_prompt(sk, q, k, v, bp, seq)
    o_s, k_win, v_win = _attn_sample(sk, q, k, v, kc, vc, tp, n_tok)
    w_hi, w_lo, b_cat = _router_weights(moe_w_group[1], moe_b_group[1], moe_w_router[1], moe_b_router[1])
    h2, xn_tm, info, info_t, cnt = _wo_router(o_p, o_s, h1, w_o[0].astype(BF16), ffn_norm[1].reshape(1, d),
                                             w_hi, w_lo, b_cat, earlier_x)
    dest1, dest2, ybuf = _moe_experts(xn_tm, info_t, cnt, moe_w_gate, moe_w_up, moe_w_down, 1)
    y_p = _combine(dest1, dest2, h2, info, ybuf, 0, tp)
    y_s = _combine(dest1, dest2, h2, info, ybuf, tp, ts)

    wl = min(WINDOW, seq)
    k_p = k[:tp].reshape(bp, seq, kvw)[:, seq - wl:].reshape(bp, wl, ATT_KV_HEADS, ATT_HEAD_DIM)
    v_p = v[:tp].reshape(bp, seq, kvw)[:, seq - wl:].reshape(bp, wl, ATT_KV_HEADS, ATT_HEAD_DIM)
    k_s = k_win.reshape(bs, lc, ATT_KV_HEADS, ATT_HEAD_DIM)
    v_s = v_win.reshape(bs, lc, ATT_KV_HEADS, ATT_HEAD_DIM)
    return (y_p.reshape(bp, seq, d), _from_step_order(y_s, nsteps, n_tok),
            ssm_p, conv_p, k_p, v_p, ssm_s, conv_s, k_s, v_s)
```
